```python
import jax, jax.numpy as jnp
from jax import lax
import numpy as np

D_MODEL = 1024
BATCH = 8
SEQ = 4096
DEPTH = 4

N_MIXERS = 3
EXPAND = 2
D_INNER = EXPAND * D_MODEL
POOL_WINDOWS = (2, 4, 8, 16)
N_POOL_GROUPS = len(POOL_WINDOWS)
POOL_GROUP = D_INNER // N_POOL_GROUPS
CONV_WIDTH = 3
N_HEADS = 16
QK_NOPE_DIM = 128
QK_ROPE_DIM = 64
V_HEAD_DIM = D_INNER // N_HEADS
Q_LORA_RANK = 384
KV_LORA_RANK = 256
MLA_IN_DIM = Q_LORA_RANK + KV_LORA_RANK + QK_ROPE_DIM + D_INNER
ATTN_SCALE = (QK_NOPE_DIM + QK_ROPE_DIM) ** -0.5
ROPE_BASE = 10000.0
Q_BLOCK = 128
NORM_EPS = 1e-6
MAX_POS_OFFSET = 1024
N_POOL = (DEPTH + 2) // 3
N_CONV = (DEPTH + 1) // 3
N_MLA = DEPTH // 3

kernel_name = "hybrid_pool_conv_mla_gated_trunk"


def rms_norm(x, g):
    xf = x.astype(jnp.float32)
    y = xf * lax.rsqrt(jnp.mean(xf * xf, axis=-1, keepdims=True) + NORM_EPS)
    return (y * g.astype(jnp.float32)).astype(x.dtype)


def pool_mixer(xn, w_in, w_grp, scale, w_out):
    B, S, _ = xn.shape
    u, z = jnp.split(xn @ w_in, 2, axis=-1)
    uf = u.astype(jnp.float32).reshape(B, S, N_POOL_GROUPS, POOL_GROUP)
    cs = jnp.cumsum(uf, axis=1)
    count_base = jnp.arange(1, S + 1, dtype=jnp.float32)
    pooled = []
    for g, w in enumerate(POOL_WINDOWS):
        c = cs[:, :, g]
        prev = jnp.pad(c, ((0, 0), (w, 0), (0, 0)))[:, :S]
        mean = (c - prev) / jnp.minimum(count_base, float(w))[None, :, None]
        pooled.append(mean - uf[:, :, g])
    pooled = jnp.stack(pooled, axis=2).astype(u.dtype)
    mixed = jnp.einsum('bsgc,gcd->bsgd', pooled, w_grp).reshape(B, S, D_INNER) * scale
    return (mixed * jax.nn.silu(z)) @ w_out


def causal_depthwise_conv(x, w):
    return lax.conv_general_dilated(
        x, w[:, None, :].astype(x.dtype), window_strides=(1,),
        padding=[(CONV_WIDTH - 1, 0)], dimension_numbers=('NWC', 'WIO', 'NWC'),
        feature_group_count=x.shape[-1])


def conv_mixer(xn, w_in, conv_w, w_out):
    b, c, h, z = jnp.split(xn @ w_in, 4, axis=-1)
    y = b * causal_depthwise_conv(c * h, conv_w)
    return (y * jax.nn.silu(z)) @ w_out


def apply_rope(x, cos, sin):
    half = x.shape[-1] // 2
    x1, x2 = x[..., :half], x[..., half:]
    return jnp.concatenate([x1 * cos - x2 * sin, x2 * cos + x1 * sin], axis=-1).astype(x.dtype)


def causal_block_attention(q_nope, q_rope, k_nope, k_rope, v):
    B, S, H, _ = q_nope.shape
    nb = S // Q_BLOCK
    qn = q_nope.reshape(B, nb, Q_BLOCK, H, QK_NOPE_DIM).transpose(1, 0, 2, 3, 4)
    qr = q_rope.reshape(B, nb, Q_BLOCK, H, QK_ROPE_DIM).transpose(1, 0, 2, 3, 4)
    starts = jnp.arange(nb, dtype=jnp.int32) * Q_BLOCK
    key_idx = jnp.arange(S, dtype=jnp.int32)

    def one_block(args):
        qn_b, qr_b, start = args
        s = (jnp.einsum('bqhd,bkhd->bhqk', qn_b, k_nope).astype(jnp.float32)
             + jnp.einsum('bqhr,bkr->bhqk', qr_b, k_rope).astype(jnp.float32)) * ATTN_SCALE
        q_idx = start + jnp.arange(Q_BLOCK, dtype=jnp.int32)
        mask = key_idx[None, :] <= q_idx[:, None]
        s = jnp.where(mask[None, None], s, jnp.float32(-1e30))
        p = jax.nn.softmax(s, axis=-1).astype(v.dtype)
        return jnp.einsum('bhqk,bkhd->bqhd', p, v)

    o = lax.map(one_block, (qn, qr, starts))
    return o.transpose(1, 0, 2, 3, 4).reshape(B, S, H, V_HEAD_DIM)


def mla_mixer(xn, cos, sin, w_in, q_norm, w_q_up, kv_norm, w_kv_up, w_out):
    B, S, _ = xn.shape
    h = xn @ w_in
    q_lat, kv_lat, k_rope, z = jnp.split(
        h, [Q_LORA_RANK, Q_LORA_RANK + KV_LORA_RANK,
            Q_LORA_RANK + KV_LORA_RANK + QK_ROPE_DIM], axis=-1)
    q = (rms_norm(q_lat, q_norm) @ w_q_up).reshape(B, S, N_HEADS, QK_NOPE_DIM + QK_ROPE_DIM)
    q_nope = q[..., :QK_NOPE_DIM]
    q_rope = apply_rope(q[..., QK_NOPE_DIM:], cos[:, :, None, :], sin[:, :, None, :])
    kv = (rms_norm(kv_lat, kv_norm) @ w_kv_up).reshape(B, S, N_HEADS, QK_NOPE_DIM + V_HEAD_DIM)
    k_nope, v = kv[..., :QK_NOPE_DIM], kv[..., QK_NOPE_DIM:]
    k_rope = apply_rope(k_rope, cos, sin)
    o = causal_block_attention(q_nope, q_rope, k_nope, k_rope, v)
    return (o.reshape(B, S, D_INNER) * jax.nn.silu(z)) @ w_out


def _fwd_setup_inputs(seed: int = 0) -> dict:
    key = jax.random.key(seed)
    ks = jax.random.split(key, 20)

    def normal(k, shape, scale):
        return jax.random.normal(k, shape, jnp.float32) * scale

    def gain(k, shape):
        return 1.0 + 0.02 * jax.random.normal(k, shape, jnp.float32)

    x = normal(ks[0], (BATCH, SEQ, D_MODEL), 1.0)
    offset = jax.random.randint(ks[1], (BATCH, 1), 0, MAX_POS_OFFSET, dtype=jnp.int32)
    positions = (offset + jnp.arange(SEQ, dtype=jnp.int32)[None, :]).astype(jnp.int32)
    return {
        "x": x,
        "positions": positions,
        "pool_norm": gain(ks[2], (N_POOL, D_MODEL)),
        "pool_w_in": normal(ks[3], (N_POOL, D_MODEL, 2 * D_INNER), D_MODEL ** -0.5),
        "pool_w_grp": normal(ks[4], (N_POOL, N_POOL_GROUPS, POOL_GROUP, POOL_GROUP), POOL_GROUP ** -0.5),
        "pool_scale": gain(ks[5], (N_POOL, D_INNER)),
        "pool_w_out": normal(ks[6], (N_POOL, D_INNER, D_MODEL), D_INNER ** -0.5),
        "conv_norm": gain(ks[7], (N_CONV, D_MODEL)),
        "conv_w_in": normal(ks[8], (N_CONV, D_MODEL, 4 * D_INNER), D_MODEL ** -0.5),
        "conv_w": normal(ks[9], (N_CONV, CONV_WIDTH, D_INNER), CONV_WIDTH ** -0.5),
        "conv_w_out": normal(ks[10], (N_CONV, D_INNER, D_MODEL), D_INNER ** -0.5),
        "mla_norm": gain(ks[11], (N_MLA, D_MODEL)),
        "mla_w_in": normal(ks[12], (N_MLA, D_MODEL, MLA_IN_DIM), D_MODEL ** -0.5),
        "mla_q_norm": gain(ks[13], (N_MLA, Q_LORA_RANK)),
        "mla_w_q_up": normal(ks[14], (N_MLA, Q_LORA_RANK, N_HEADS * (QK_NOPE_DIM + QK_ROPE_DIM)), Q_LORA_RANK ** -0.5),
        "mla_kv_norm": gain(ks[15], (N_MLA, KV_LORA_RANK)),
        "mla_w_kv_up": normal(ks[16], (N_MLA, KV_LORA_RANK, N_HEADS * (QK_NOPE_DIM + V_HEAD_DIM)), KV_LORA_RANK ** -0.5),
        "mla_w_out": normal(ks[17], (N_MLA, D_INNER, D_MODEL), D_INNER ** -0.5),
        "final_norm": gain(ks[18], (D_MODEL,)),
    }


def _fwd_reference(x, positions, pool_norm, pool_w_in, pool_w_grp, pool_scale, pool_w_out,
              conv_norm, conv_w_in, conv_w, conv_w_out,
              mla_norm, mla_w_in, mla_q_norm, mla_w_q_up, mla_kv_norm, mla_w_kv_up, mla_w_out,
              final_norm):
    inv_freq = ROPE_BASE ** (-jnp.arange(0, QK_ROPE_DIM, 2, dtype=jnp.float32) / QK_ROPE_DIM)
    angles = positions.astype(jnp.float32)[..., None] * inv_freq
    cos, sin = jnp.cos(angles).astype(x.dtype), jnp.sin(angles).astype(x.dtype)

    for i in range(DEPTH):
        kind, j = i % N_MIXERS, i // N_MIXERS
        if kind == 0:
            xn = rms_norm(x, pool_norm[j])
            x = x + pool_mixer(xn, pool_w_in[j], pool_w_grp[j], pool_scale[j], pool_w_out[j])
        elif kind == 1:
            xn = rms_norm(x, conv_norm[j])
            x = x + conv_mixer(xn, conv_w_in[j], conv_w[j], conv_w_out[j])
        else:
            xn = rms_norm(x, mla_norm[j])
            x = x + mla_mixer(xn, cos, sin, mla_w_in[j], mla_q_norm[j], mla_w_q_up[j],
                              mla_kv_norm[j], mla_w_kv_up[j], mla_w_out[j])
    return rms_norm(x, final_norm)


import jax as _jax
import jax.numpy as _jnp

TWIN_FORMAT = 'train_step'
FWD_PARAMS = ['x', 'positions', 'pool_norm', 'pool_w_in', 'pool_w_grp', 'pool_scale', 'pool_w_out', 'conv_norm', 'conv_w_in', 'conv_w', 'conv_w_out', 'mla_norm', 'mla_w_in', 'mla_q_norm', 'mla_w_q_up', 'mla_kv_norm', 'mla_w_kv_up', 'mla_w_out', 'final_norm']
TWIN_WEIGHTS = ['pool_norm', 'pool_w_in', 'pool_w_grp', 'pool_scale', 'pool_w_out', 'conv_norm', 'conv_w_in', 'conv_w', 'conv_w_out', 'mla_norm', 'mla_w_in', 'mla_q_norm', 'mla_w_q_up', 'mla_kv_norm', 'mla_w_kv_up', 'mla_w_out', 'final_norm']
TWIN_DIFF_INPUT = 'x'
TWIN_INPUTS = ['x', 'positions', 'pool_norm', 'pool_w_in', 'pool_w_grp', 'pool_scale', 'pool_w_out', 'conv_norm', 'conv_w_in', 'conv_w', 'conv_w_out', 'mla_norm', 'mla_w_in', 'mla_q_norm', 'mla_w_q_up', 'mla_kv_norm', 'mla_w_kv_up', 'mla_w_out', 'final_norm', 'loss_target', 'm_pool_norm', 'm_pool_w_in', 'm_pool_w_grp', 'm_pool_scale', 'm_pool_w_out', 'm_conv_norm', 'm_conv_w_in', 'm_conv_w', 'm_conv_w_out', 'm_mla_norm', 'm_mla_w_in', 'm_mla_q_norm', 'm_mla_w_q_up', 'm_mla_kv_norm', 'm_mla_w_kv_up', 'm_mla_w_out', 'm_final_norm', 'v_pool_norm', 'v_pool_w_in', 'v_pool_w_grp', 'v_pool_scale', 'v_pool_w_out', 'v_conv_norm', 'v_conv_w_in', 'v_conv_w', 'v_conv_w_out', 'v_mla_norm', 'v_mla_w_in', 'v_mla_q_norm', 'v_mla_w_q_up', 'v_mla_kv_norm', 'v_mla_w_kv_up', 'v_mla_w_out', 'v_final_norm']
TWIN_OUTPUTS = ['loss', 'grad_x', 'grad_pool_norm', 'grad_pool_w_in', 'grad_pool_w_grp', 'grad_pool_scale', 'grad_pool_w_out', 'grad_conv_norm', 'grad_conv_w_in', 'grad_conv_w', 'grad_conv_w_out', 'grad_mla_norm', 'grad_mla_w_in', 'grad_mla_q_norm', 'grad_mla_w_q_up', 'grad_mla_kv_norm', 'grad_mla_w_kv_up', 'grad_mla_w_out', 'grad_final_norm', 'delta_pool_norm', 'delta_pool_w_in', 'delta_pool_w_grp', 'delta_pool_scale', 'delta_pool_w_out', 'delta_conv_norm', 'delta_conv_w_in', 'delta_conv_w', 'delta_conv_w_out', 'delta_mla_norm', 'delta_mla_w_in', 'delta_mla_q_norm', 'delta_mla_w_q_up', 'delta_mla_kv_norm', 'delta_mla_w_kv_up', 'delta_mla_w_out', 'delta_final_norm', 'new_m_pool_norm', 'new_m_pool_w_in', 'new_m_pool_w_grp', 'new_m_pool_scale', 'new_m_pool_w_out', 'new_m_conv_norm', 'new_m_conv_w_in', 'new_m_conv_w', 'new_m_conv_w_out', 'new_m_mla_norm', 'new_m_mla_w_in', 'new_m_mla_q_norm', 'new_m_mla_w_q_up', 'new_m_mla_kv_norm', 'new_m_mla_w_kv_up', 'new_m_mla_w_out', 'new_m_final_norm', 'new_v_pool_norm', 'new_v_pool_w_in', 'new_v_pool_w_grp', 'new_v_pool_scale', 'new_v_pool_w_out', 'new_v_conv_norm', 'new_v_conv_w_in', 'new_v_conv_w', 'new_v_conv_w_out', 'new_v_mla_norm', 'new_v_mla_w_in', 'new_v_mla_q_norm', 'new_v_mla_w_q_up', 'new_v_mla_kv_norm', 'new_v_mla_w_kv_up', 'new_v_mla_w_out', 'new_v_final_norm']
TWIN_LEAF_KINDS = {'loss': 'loss', 'grad_x': 'grad_x', 'grad_pool_norm': 'grad_w', 'grad_pool_w_in': 'grad_w', 'grad_pool_w_grp': 'grad_w', 'grad_pool_scale': 'grad_w', 'grad_pool_w_out': 'grad_w', 'grad_conv_norm': 'grad_w', 'grad_conv_w_in': 'grad_w', 'grad_conv_w': 'grad_w', 'grad_conv_w_out': 'grad_w', 'grad_mla_norm': 'grad_w', 'grad_mla_w_in': 'grad_w', 'grad_mla_q_norm': 'grad_w', 'grad_mla_w_q_up': 'grad_w', 'grad_mla_kv_norm': 'grad_w', 'grad_mla_w_kv_up': 'grad_w', 'grad_mla_w_out': 'grad_w', 'grad_final_norm': 'grad_w', 'delta_pool_norm': 'delta_w', 'delta_pool_w_in': 'delta_w', 'delta_pool_w_grp': 'delta_w', 'delta_pool_scale': 'delta_w', 'delta_pool_w_out': 'delta_w', 'delta_conv_norm': 'delta_w', 'delta_conv_w_in': 'delta_w', 'delta_conv_w': 'delta_w', 'delta_conv_w_out': 'delta_w', 'delta_mla_norm': 'delta_w', 'delta_mla_w_in': 'delta_w', 'delta_mla_q_norm': 'delta_w', 'delta_mla_w_q_up': 'delta_w', 'delta_mla_kv_norm': 'delta_w', 'delta_mla_w_kv_up': 'delta_w', 'delta_mla_w_out': 'delta_w', 'delta_final_norm': 'delta_w', 'new_m_pool_norm': 'new_m', 'new_m_pool_w_in': 'new_m', 'new_m_pool_w_grp': 'new_m', 'new_m_pool_scale': 'new_m', 'new_m_pool_w_out': 'new_m', 'new_m_conv_norm': 'new_m', 'new_m_conv_w_in': 'new_m', 'new_m_conv_w': 'new_m', 'new_m_conv_w_out': 'new_m', 'new_m_mla_norm': 'new_m', 'new_m_mla_w_in': 'new_m', 'new_m_mla_q_norm': 'new_m', 'new_m_mla_w_q_up': 'new_m', 'new_m_mla_kv_norm': 'new_m', 'new_m_mla_w_kv_up': 'new_m', 'new_m_mla_w_out': 'new_m', 'new_m_final_norm': 'new_m', 'new_v_pool_norm': 'new_v', 'new_v_pool_w_in': 'new_v', 'new_v_pool_w_grp': 'new_v', 'new_v_pool_scale': 'new_v', 'new_v_pool_w_out': 'new_v', 'new_v_conv_norm': 'new_v', 'new_v_conv_w_in': 'new_v', 'new_v_conv_w': 'new_v', 'new_v_conv_w_out': 'new_v', 'new_v_mla_norm': 'new_v', 'new_v_mla_w_in': 'new_v', 'new_v_mla_q_norm': 'new_v', 'new_v_mla_w_q_up': 'new_v', 'new_v_mla_kv_norm': 'new_v', 'new_v_mla_w_kv_up': 'new_v', 'new_v_mla_w_out': 'new_v', 'new_v_final_norm': 'new_v'}


def _forward(args):
    return _fwd_reference(*[args[k] for k in FWD_PARAMS])


def _output_shape():
    def fwd():
        inp = _fwd_setup_inputs(0)
        return _fwd_reference(*[inp[k] for k in FWD_PARAMS])
    out = _jax.eval_shape(fwd)
    return out.shape, out.dtype

N_MICROBATCH = 1
ADAM_LR = 0.001
ADAM_B1 = 0.9
ADAM_B2 = 0.999
ADAM_EPS = 1e-08
ADAM_WD = 0.01
ADAM_STEP = 10
PER_EXAMPLE_BATCH_AXIS = {'x': 0, 'positions': 0, 'loss_target': 0}
SHARED_INPUTS = []
_WEIGHT_DTYPES = {'pool_norm': _jnp.float32, 'pool_w_in': _jnp.float32, 'pool_w_grp': _jnp.float32, 'pool_scale': _jnp.float32, 'pool_w_out': _jnp.float32, 'conv_norm': _jnp.float32, 'conv_w_in': _jnp.float32, 'conv_w': _jnp.float32, 'conv_w_out': _jnp.float32, 'mla_norm': _jnp.float32, 'mla_w_in': _jnp.float32, 'mla_q_norm': _jnp.float32, 'mla_w_q_up': _jnp.float32, 'mla_kv_norm': _jnp.float32, 'mla_w_kv_up': _jnp.float32, 'mla_w_out': _jnp.float32, 'final_norm': _jnp.float32}
MOMENT_SCALE = {'pool_norm': 1.327481e-01, 'pool_w_in': 6.719710e-02, 'pool_w_grp': 6.606018e-02, 'pool_scale': 6.555051e-02, 'pool_w_out': 9.343975e-02, 'conv_norm': 1.852576e-01, 'conv_w_in': 6.366064e-02, 'conv_w': 6.309336e-02, 'conv_w_out': 8.896152e-02, 'mla_norm': 3.619204e-02, 'mla_w_in': 2.179741e-02, 'mla_q_norm': 2.582629e-02, 'mla_w_q_up': 9.270226e-03, 'mla_kv_norm': 4.732328e-02, 'mla_w_kv_up': 1.178729e-02, 'mla_w_out': 1.937060e-02, 'final_norm': 3.202453e+01}


def _to_microbatches(a, axis):
    t = _jnp.moveaxis(a, axis, 0)
    t = t.reshape((N_MICROBATCH, t.shape[0] // N_MICROBATCH) + t.shape[1:])
    return _jnp.moveaxis(t, 1, axis + 1)


def setup_inputs(seed: int = 0) -> dict:
    inp = _fwd_setup_inputs(seed)
    key = _jax.random.fold_in(_jax.random.key(seed), 7919)
    shape, _ = _output_shape()
    out = dict(inp)
    out["loss_target"] = _jax.random.normal(_jax.random.fold_in(key, 0), shape, _jnp.float32)
    for i, name in enumerate(TWIN_WEIGHTS):
        w = inp[name].astype(_jnp.float32)
        if MOMENT_SCALE is None:
            s = _jnp.sqrt(_jnp.mean(_jnp.square(w)) + 1e-30)
        else:
            s = MOMENT_SCALE[name]
        km, kv = _jax.random.split(_jax.random.fold_in(key, i + 1))
        out[name] = w
        out["m_" + name] = s * _jax.random.normal(km, w.shape, _jnp.float32)
        out["v_" + name] = (s * s) * _jax.random.uniform(kv, w.shape, _jnp.float32, 0.5, 1.5)
    if N_MICROBATCH > 1:
        for name, axis in PER_EXAMPLE_BATCH_AXIS.items():
            out[name] = _to_microbatches(out[name], axis)
    return {'x': out['x'], 'positions': out['positions'], 'pool_norm': out['pool_norm'], 'pool_w_in': out['pool_w_in'], 'pool_w_grp': out['pool_w_grp'], 'pool_scale': out['pool_scale'], 'pool_w_out': out['pool_w_out'], 'conv_norm': out['conv_norm'], 'conv_w_in': out['conv_w_in'], 'conv_w': out['conv_w'], 'conv_w_out': out['conv_w_out'], 'mla_norm': out['mla_norm'], 'mla_w_in': out['mla_w_in'], 'mla_q_norm': out['mla_q_norm'], 'mla_w_q_up': out['mla_w_q_up'], 'mla_kv_norm': out['mla_kv_norm'], 'mla_w_kv_up': out['mla_w_kv_up'], 'mla_w_out': out['mla_w_out'], 'final_norm': out['final_norm'], 'loss_target': out['loss_target'], 'm_pool_norm': out['m_pool_norm'], 'm_pool_w_in': out['m_pool_w_in'], 'm_pool_w_grp': out['m_pool_w_grp'], 'm_pool_scale': out['m_pool_scale'], 'm_pool_w_out': out['m_pool_w_out'], 'm_conv_norm': out['m_conv_norm'], 'm_conv_w_in': out['m_conv_w_in'], 'm_conv_w': out['m_conv_w'], 'm_conv_w_out': out['m_conv_w_out'], 'm_mla_norm': out['m_mla_norm'], 'm_mla_w_in': out['m_mla_w_in'], 'm_mla_q_norm': out['m_mla_q_norm'], 'm_mla_w_q_up': out['m_mla_w_q_up'], 'm_mla_kv_norm': out['m_mla_kv_norm'], 'm_mla_w_kv_up': out['m_mla_w_kv_up'], 'm_mla_w_out': out['m_mla_w_out'], 'm_final_norm': out['m_final_norm'], 'v_pool_norm': out['v_pool_norm'], 'v_pool_w_in': out['v_pool_w_in'], 'v_pool_w_grp': out['v_pool_w_grp'], 'v_pool_scale': out['v_pool_scale'], 'v_pool_w_out': out['v_pool_w_out'], 'v_conv_norm': out['v_conv_norm'], 'v_conv_w_in': out['v_conv_w_in'], 'v_conv_w': out['v_conv_w'], 'v_conv_w_out': out['v_conv_w_out'], 'v_mla_norm': out['v_mla_norm'], 'v_mla_w_in': out['v_mla_w_in'], 'v_mla_q_norm': out['v_mla_q_norm'], 'v_mla_w_q_up': out['v_mla_w_q_up'], 'v_mla_kv_norm': out['v_mla_kv_norm'], 'v_mla_w_kv_up': out['v_mla_w_kv_up'], 'v_mla_w_out': out['v_mla_w_out'], 'v_final_norm': out['v_final_norm']}


def _loss(weights, diff, rest, loss_target):
    with _jax.named_scope("forward"):
        args = {**rest, TWIN_DIFF_INPUT: diff, **{k: w.astype(_WEIGHT_DTYPES[k]) for k, w in weights.items()}}
        y = _forward(args)
    with _jax.named_scope("loss_head"):
        err = _jnp.square(y.astype(_jnp.float32) - loss_target)
        return 0.5 * _jnp.sum(_jnp.mean(err, axis=-1)) if err.ndim else 0.5 * err


def _adamw(w, g, m, v):
    m = ADAM_B1 * m + (1.0 - ADAM_B1) * g
    v = ADAM_B2 * v + (1.0 - ADAM_B2) * _jnp.square(g)
    m_hat = m / (1.0 - ADAM_B1 ** ADAM_STEP)
    v_hat = v / (1.0 - ADAM_B2 ** ADAM_STEP)
    delta = -ADAM_LR * (m_hat / (_jnp.sqrt(v_hat) + ADAM_EPS) + ADAM_WD * w)
    return delta, m, v


def reference(x, positions, pool_norm, pool_w_in, pool_w_grp, pool_scale, pool_w_out, conv_norm, conv_w_in, conv_w, conv_w_out, mla_norm, mla_w_in, mla_q_norm, mla_w_q_up, mla_kv_norm, mla_w_kv_up, mla_w_out, final_norm, loss_target, m_pool_norm, m_pool_w_in, m_pool_w_grp, m_pool_scale, m_pool_w_out, m_conv_norm, m_conv_w_in, m_conv_w, m_conv_w_out, m_mla_norm, m_mla_w_in, m_mla_q_norm, m_mla_w_q_up, m_mla_kv_norm, m_mla_w_kv_up, m_mla_w_out, m_final_norm, v_pool_norm, v_pool_w_in, v_pool_w_grp, v_pool_scale, v_pool_w_out, v_conv_norm, v_conv_w_in, v_conv_w, v_conv_w_out, v_mla_norm, v_mla_w_in, v_mla_q_norm, v_mla_w_q_up, v_mla_kv_norm, v_mla_w_kv_up, v_mla_w_out, v_final_norm):
    given = dict(x=x, positions=positions, pool_norm=pool_norm, pool_w_in=pool_w_in, pool_w_grp=pool_w_grp, pool_scale=pool_scale, pool_w_out=pool_w_out, conv_norm=conv_norm, conv_w_in=conv_w_in, conv_w=conv_w, conv_w_out=conv_w_out, mla_norm=mla_norm, mla_w_in=mla_w_in, mla_q_norm=mla_q_norm, mla_w_q_up=mla_w_q_up, mla_kv_norm=mla_kv_norm, mla_w_kv_up=mla_w_kv_up, mla_w_out=mla_w_out, final_norm=final_norm, loss_target=loss_target, m_pool_norm=m_pool_norm, m_pool_w_in=m_pool_w_in, m_pool_w_grp=m_pool_w_grp, m_pool_scale=m_pool_scale, m_pool_w_out=m_pool_w_out, m_conv_norm=m_conv_norm, m_conv_w_in=m_conv_w_in, m_conv_w=m_conv_w, m_conv_w_out=m_conv_w_out, m_mla_norm=m_mla_norm, m_mla_w_in=m_mla_w_in, m_mla_q_norm=m_mla_q_norm, m_mla_w_q_up=m_mla_w_q_up, m_mla_kv_norm=m_mla_kv_norm, m_mla_w_kv_up=m_mla_w_kv_up, m_mla_w_out=m_mla_w_out, m_final_norm=m_final_norm, v_pool_norm=v_pool_norm, v_pool_w_in=v_pool_w_in, v_pool_w_grp=v_pool_w_grp, v_pool_scale=v_pool_scale, v_pool_w_out=v_pool_w_out, v_conv_norm=v_conv_norm, v_conv_w_in=v_conv_w_in, v_conv_w=v_conv_w, v_conv_w_out=v_conv_w_out, v_mla_norm=v_mla_norm, v_mla_w_in=v_mla_w_in, v_mla_q_norm=v_mla_q_norm, v_mla_w_q_up=v_mla_w_q_up, v_mla_kv_norm=v_mla_kv_norm, v_mla_w_kv_up=v_mla_w_kv_up, v_mla_w_out=v_mla_w_out, v_final_norm=v_final_norm)
    weights = {n: given[n] for n in TWIN_WEIGHTS}
    shared = {n: given[n] for n in SHARED_INPUTS}
    per_example = {n: given[n] for n in ['x', 'positions']}
    grad_fn = _jax.value_and_grad(_loss, argnums=(0, 1))

    def one_microbatch(ex, loss_target):
        ex = dict(ex)
        diff = ex.pop(TWIN_DIFF_INPUT)
        return grad_fn(weights, diff, {**shared, **ex}, loss_target)

    if N_MICROBATCH == 1:
        loss, (grad_w, grad_x) = one_microbatch(per_example, given["loss_target"])
    else:
        def body(carry, xs):
            loss_sum, grad_sum = carry
            l_k, (gw_k, gx_k) = one_microbatch(xs[0], xs[1])
            with _jax.named_scope("update"):
                return (loss_sum + l_k, _jax.tree.map(_jnp.add, grad_sum, gw_k)), gx_k

        init = (_jnp.zeros((), _jnp.float32), _jax.tree.map(_jnp.zeros_like, weights))
        (loss, grad_w), grad_x = _jax.lax.scan(body, init, (per_example, given["loss_target"]))
    with _jax.named_scope("update"):
        delta_w, new_m, new_v = {}, {}, {}
        for n in TWIN_WEIGHTS:
            delta_w[n], new_m[n], new_v[n] = _adamw(weights[n], grad_w[n], given["m_" + n], given["v_" + n])
    return (loss, grad_x, *[grad_w[n] for n in TWIN_WEIGHTS], *[delta_w[n] for n in TWIN_WEIGHTS],
            *[new_m[n] for n in TWIN_WEIGHTS], *[new_v[n] for n in TWIN_WEIGHTS])
```

```python
import functools

import jax
import jax.numpy as jnp
from jax import lax
from jax.experimental import pallas as pl
from jax.experimental.pallas import tpu as pltpu

F32 = jnp.float32
BF16 = jnp.bfloat16

D_MODEL = 1024
D_INNER = 2048
POOL_WINDOWS = (2, 4, 8, 16)
POOL_GROUP = 512
N_HEADS = 16
QK_NOPE = 128
QK_ROPE = 64
V_DIM = 128
HEAD_PAD = 256
Q_LORA = 384
KV_LORA = 256
MLA_IN = Q_LORA + KV_LORA + QK_ROPE + D_INNER
MLA_IN_PAD = 2816
ATTN_SCALE = (QK_NOPE + QK_ROPE) ** -0.5
ROPE_BASE = 10000.0
NORM_EPS = 1e-6
HALO = 16

ADAM_LR = 0.001
ADAM_B1 = 0.9
ADAM_B2 = 0.999
ADAM_EPS = 1e-08
ADAM_WD = 0.01
ADAM_STEP = 10

N_CHIPS = 4
N_DEV = 8
LANES = 128
PACK_COLS = 1024
V7X_VMEM_LIMIT = 56 * 1024 * 1024
MESH = pl.DeviceIdType.MESH


def _cparams(*sem):
    return pltpu.CompilerParams(dimension_semantics=sem, vmem_limit_bytes=V7X_VMEM_LIMIT)


def _pick(n, cap):
    best = None
    for d in range(LANES, min(n, cap) + 1, LANES):
        if n % d == 0:
            best = d
    assert best is not None, (n, cap)
    return best


def _sigmoid(z):
    return 1.0 / (1.0 + jnp.exp(-z))


def _mm(a, b, *, trans_a=False, trans_b=False, out_dtype=F32, residual=None, tm, tn, tk, name):
    if trans_a:
        K, M = a.shape
    else:
        M, K = a.shape
    if trans_b:
        N, K2 = b.shape
    else:
        K2, N = b.shape
    assert K == K2 and M % tm == 0 and N % tn == 0 and K % tk == 0, (name, a.shape, b.shape, tm, tn, tk)
    nk = K // tk
    dn = (((0 if trans_a else 1,), (1 if trans_b else 0,)), ((), ()))
    has_res = residual is not None

    def body(*refs):
        if has_res:
            a_ref, b_ref, r_ref, o_ref = refs[:4]
            rest = refs[4:]
        else:
            a_ref, b_ref, o_ref = refs[:3]
            r_ref = None
            rest = refs[3:]
        part = lax.dot_general(a_ref[...], b_ref[...], dn, preferred_element_type=F32)

        def finish(acc):
            if has_res:
                acc = acc + r_ref[...]
            o_ref[...] = acc.astype(o_ref.dtype)

        if nk == 1:
            finish(part)
        else:
            acc_ref = rest[0]
            k = pl.program_id(2)

            @pl.when(k == 0)
            def _():
                acc_ref[...] = part

            @pl.when(k > 0)
            def _():
                acc_ref[...] += part

            @pl.when(k == nk - 1)
            def _():
                finish(acc_ref[...])

    a_spec = pl.BlockSpec((tk, tm), lambda i, j, k: (k, i)) if trans_a else pl.BlockSpec((tm, tk), lambda i, j, k: (i, k))
    b_spec = pl.BlockSpec((tn, tk), lambda i, j, k: (j, k)) if trans_b else pl.BlockSpec((tk, tn), lambda i, j, k: (k, j))
    o_spec = pl.BlockSpec((tm, tn), lambda i, j, k: (i, j))
    in_specs = [a_spec, b_spec] + ([o_spec] if has_res else [])
    args = (a, b) + ((residual,) if has_res else ())
    return pl.pallas_call(
        body, name=name,
        out_shape=jax.ShapeDtypeStruct((M, N), out_dtype),
        grid=(M // tm, N // tn, nk),
        in_specs=in_specs, out_specs=o_spec,
        scratch_shapes=[pltpu.VMEM((tm, tn), F32)] if nk > 1 else [],
        compiler_params=_cparams("parallel", "parallel", "arbitrary"),
    )(*args)


def _grouped_tn(a, b, *, tk, name):
    S = a.shape[0]
    G = POOL_GROUP
    nk = S // tk

    def body(a_ref, b_ref, o_ref, acc_ref):
        k = pl.program_id(1)
        part = lax.dot_general(a_ref[...], b_ref[...], (((0,), (0,)), ((), ())), preferred_element_type=F32)

        @pl.when(k == 0)
        def _():
            acc_ref[...] = part

        @pl.when(k > 0)
        def _():
            acc_ref[...] += part

        @pl.when(k == nk - 1)
        def _():
            o_ref[...] = acc_ref[...].astype(o_ref.dtype)

    return pl.pallas_call(
        body, name=name,
        out_shape=jax.ShapeDtypeStruct((len(POOL_WINDOWS), G, G), BF16),
        grid=(len(POOL_WINDOWS), nk),
        in_specs=[pl.BlockSpec((tk, G), lambda g, k: (k, g)), pl.BlockSpec((tk, G), lambda g, k: (k, g))],
        out_specs=pl.BlockSpec((None, G, G), lambda g, k: (g, 0, 0)),
        scratch_shapes=[pltpu.VMEM((G, G), F32)],
        compiler_params=_cparams("parallel", "arbitrary"),
    )(a, b)


def _rms_fwd(x, g, *, tm, name):
    S, D = x.shape

    def body(x_ref, g_ref, o_ref):
        xv = x_ref[...]
        rstd = lax.rsqrt(jnp.mean(xv * xv, axis=-1, keepdims=True) + NORM_EPS)
        o_ref[...] = (xv * rstd * g_ref[...]).astype(o_ref.dtype)

    return pl.pallas_call(
        body, name=name,
        out_shape=jax.ShapeDtypeStruct((S, D), BF16),
        grid=(S // tm,),
        in_specs=[pl.BlockSpec((tm, D), lambda i: (i, 0)), pl.BlockSpec((1, D), lambda i: (0, 0))],
        out_specs=pl.BlockSpec((tm, D), lambda i: (i, 0)),
        compiler_params=_cparams("parallel"),
    )(x, g)


def _rms_bwd_math(xv, gv, dxn):
    rstd = lax.rsqrt(jnp.mean(xv * xv, axis=-1, keepdims=True) + NORM_EPS)
    xh = xv * rstd
    dg = jnp.sum(dxn * xh, axis=0, keepdims=True)
    dxh = dxn * gv
    dx = rstd * (dxh - xh * jnp.mean(dxh * xh, axis=-1, keepdims=True))
    return dx, dg


def _rms_bwd(x, g, dxn, dres, *, tm, name):
    S, D = x.shape

    def body(x_ref, g_ref, dxn_ref, dres_ref, dx_ref, dxb_ref, dg_ref):
        dx, dg = _rms_bwd_math(x_ref[...], g_ref[...], dxn_ref[...])
        dx = dx + dres_ref[...]
        dx_ref[...] = dx
        dxb_ref[...] = dx.astype(BF16)

        @pl.when(pl.program_id(0) == 0)
        def _():
            dg_ref[...] = dg

        @pl.when(pl.program_id(0) > 0)
        def _():
            dg_ref[...] += dg

    row = pl.BlockSpec((tm, D), lambda i: (i, 0))
    vec = pl.BlockSpec((1, D), lambda i: (0, 0))
    return pl.pallas_call(
        body, name=name,
        out_shape=(jax.ShapeDtypeStruct((S, D), F32), jax.ShapeDtypeStruct((S, D), BF16), jax.ShapeDtypeStruct((1, D), F32)),
        grid=(S // tm,),
        in_specs=[row, vec, row, row],
        out_specs=(row, row, vec),
        compiler_params=_cparams("arbitrary"),
    )(x, g, dxn, dres)


def _final_loss(x, g, target, *, tm, name):
    S, D = x.shape

    def body(x_ref, g_ref, t_ref, loss_ref, dx_ref, dxb_ref, dg_ref):
        xv = x_ref[...]
        gv = g_ref[...]
        rstd = lax.rsqrt(jnp.mean(xv * xv, axis=-1, keepdims=True) + NORM_EPS)
        xh = xv * rstd
        err = xh * gv - t_ref[...]
        part = 0.5 * jnp.sum(jnp.mean(err * err, axis=-1, keepdims=True), axis=0, keepdims=True)
        dy = err * (1.0 / D)
        dg = jnp.sum(dy * xh, axis=0, keepdims=True)
        dxh = dy * gv
        dx = rstd * (dxh - xh * jnp.mean(dxh * xh, axis=-1, keepdims=True))
        dx_ref[...] = dx
        dxb_ref[...] = dx.astype(BF16)
        lossb = jnp.broadcast_to(part, loss_ref.shape)

        @pl.when(pl.program_id(0) == 0)
        def _():
            dg_ref[...] = dg
            loss_ref[...] = lossb

        @pl.when(pl.program_id(0) > 0)
        def _():
            dg_ref[...] += dg
            loss_ref[...] += lossb

    row = pl.BlockSpec((tm, D), lambda i: (i, 0))
    vec = pl.BlockSpec((1, D), lambda i: (0, 0))
    lspec = pl.BlockSpec((8, LANES), lambda i: (0, 0))
    return pl.pallas_call(
        body, name=name,
        out_shape=(jax.ShapeDtypeStruct((8, LANES), F32), jax.ShapeDtypeStruct((S, D), F32),
                   jax.ShapeDtypeStruct((S, D), BF16), jax.ShapeDtypeStruct((1, D), F32)),
        grid=(S // tm,),
        in_specs=[row, vec, row],
        out_specs=(lspec, row, row, vec),
        compiler_params=_cparams("arbitrary"),
    )(x, g, target)


def _prev_halo_spec(tm, width, col):
    r = tm // HALO
    return pl.BlockSpec((HALO, width), lambda i: (jnp.maximum(i * r - 1, 0), col))


def _next_halo_spec(tm, width, col, S):
    r = tm // HALO
    last = S // HALO - 1
    return pl.BlockSpec((HALO, width), lambda i: (jnp.minimum((i + 1) * r, last), col))


def _shift_down(ext, k):
    return pltpu.roll(ext, k, 0)[HALO:, :]


def _shift_up(ext, k, tm):
    n = ext.shape[0]
    return pltpu.roll(ext, n - k, 0)[:tm, :]


def _pool_window_sum(ext, w):
    s = ext
    k = 1
    while k < w:
        s = s + pltpu.roll(s, k, 0)
        k *= 2
    return s[HALO:, :]


def _pooled_group(u_ref, halo, g, w, t_idx):
    cs = slice(g * POOL_GROUP, (g + 1) * POOL_GROUP)
    u = u_ref[:, cs]
    ext = jnp.concatenate([halo[:, cs], u], axis=0)
    inv = 1.0 / jnp.minimum(t_idx + 1, w).astype(F32)
    return _pool_window_sum(ext, w) * inv - u


def _pool_fwd(h, w_grp, scale, *, tm, name):
    S = h.shape[0]
    E = D_INNER

    def body(u_ref, uh_ref, z_ref, wg_ref, sc_ref, y_ref):
        i = pl.program_id(0)
        halo = jnp.where(i > 0, uh_ref[...], 0.0)
        t_idx = i * tm + lax.broadcasted_iota(jnp.int32, (tm, 1), 0)
        for g, w in enumerate(POOL_WINDOWS):
            cs = slice(g * POOL_GROUP, (g + 1) * POOL_GROUP)
            pooled = _pooled_group(u_ref, halo, g, w, t_idx)
            mixed = jnp.dot(pooled.astype(BF16), wg_ref[g], preferred_element_type=F32)
            z = z_ref[:, cs]
            y_ref[:, cs] = (mixed * sc_ref[:, cs] * (z * _sigmoid(z))).astype(BF16)

    return pl.pallas_call(
        body, name=name,
        out_shape=jax.ShapeDtypeStruct((S, E), BF16),
        grid=(S // tm,),
        in_specs=[pl.BlockSpec((tm, E), lambda i: (i, 0)), _prev_halo_spec(tm, E, 0),
                  pl.BlockSpec((tm, E), lambda i: (i, 1)),
                  pl.BlockSpec((len(POOL_WINDOWS), POOL_GROUP, POOL_GROUP), lambda i: (0, 0, 0)),
                  pl.BlockSpec((1, E), lambda i: (0, 0))],
        out_specs=pl.BlockSpec((tm, E), lambda i: (i, 0)),
        compiler_params=_cparams("parallel"),
    )(h, h, h, w_grp, scale)


def _pool_bwd1(h, dy, w_grp, scale, *, tm, name):
    S = h.shape[0]
    E = D_INNER

    def body(u_ref, uh_ref, z_ref, dy_ref, wg_ref, sc_ref, pooled_ref, dmixed_ref, dpooled_ref, dz_ref, dsc_ref):
        i = pl.program_id(0)
        halo = jnp.where(i > 0, uh_ref[...], 0.0)
        t_idx = i * tm + lax.broadcasted_iota(jnp.int32, (tm, 1), 0)
        for g, w in enumerate(POOL_WINDOWS):
            cs = slice(g * POOL_GROUP, (g + 1) * POOL_GROUP)
            pooled = _pooled_group(u_ref, halo, g, w, t_idx).astype(BF16)
            wg = wg_ref[g]
            mixed = jnp.dot(pooled, wg, preferred_element_type=F32)
            z = z_ref[:, cs]
            sg = _sigmoid(z)
            dyv = dy_ref[:, cs]
            sc = sc_ref[:, cs]
            dms = dyv * (z * sg)
            dz = dyv * (mixed * sc) * (sg * (1.0 + z * (1.0 - sg)))
            dsc = jnp.sum(dms * mixed, axis=0, keepdims=True)
            dmixed = (dms * sc).astype(BF16)
            dpooled = lax.dot_general(dmixed, wg, (((1,), (1,)), ((), ())), preferred_element_type=F32)
            pooled_ref[:, cs] = pooled
            dmixed_ref[:, cs] = dmixed
            dpooled_ref[:, cs] = dpooled
            dz_ref[:, cs] = dz.astype(BF16)

            @pl.when(i == 0)
            def _():
                dsc_ref[:, cs] = dsc

            @pl.when(i > 0)
            def _():
                dsc_ref[:, cs] += dsc

    row = pl.BlockSpec((tm, E), lambda i: (i, 0))
    vec = pl.BlockSpec((1, E), lambda i: (0, 0))
    return pl.pallas_call(
        body, name=name,
        out_shape=(jax.ShapeDtypeStruct((S, E), BF16), jax.ShapeDtypeStruct((S, E), BF16),
                   jax.ShapeDtypeStruct((S, E), F32), jax.ShapeDtypeStruct((S, E), BF16),
                   jax.ShapeDtypeStruct((1, E), F32)),
        grid=(S // tm,),
        in_specs=[row, _prev_halo_spec(tm, E, 0), pl.BlockSpec((tm, E), lambda i: (i, 1)), row,
                  pl.BlockSpec((len(POOL_WINDOWS), POOL_GROUP, POOL_GROUP), lambda i: (0, 0, 0)), vec],
        out_specs=(row, row, row, row, vec),
        compiler_params=_cparams("arbitrary"),
    )(h, h, h, dy, w_grp, scale)


def _pool_bwd2(dpooled, dz, *, tm, name):
    S = dpooled.shape[0]
    E = D_INNER
    nt = S // tm

    def body(dp_ref, dpn_ref, dz_ref, dh_ref):
        i = pl.program_id(0)
        nxt = jnp.where(i < nt - 1, dpn_ref[...], 0.0)
        t_ext = i * tm + lax.broadcasted_iota(jnp.int32, (tm + HALO, 1), 0)
        for g, w in enumerate(POOL_WINDOWS):
            cs = slice(g * POOL_GROUP, (g + 1) * POOL_GROUP)
            dp = dp_ref[:, cs]
            inv = 1.0 / jnp.minimum(t_ext + 1, w).astype(F32)
            s = jnp.concatenate([dp, nxt[:, cs]], axis=0) * inv
            n = tm + HALO
            k = 1
            while k < w:
                s = s + pltpu.roll(s, n - k, 0)
                k *= 2
            dh_ref[:, cs] = (s[:tm, :] - dp).astype(BF16)
        dh_ref[:, E:] = dz_ref[...]

    return pl.pallas_call(
        body, name=name,
        out_shape=jax.ShapeDtypeStruct((S, 2 * E), BF16),
        grid=(nt,),
        in_specs=[pl.BlockSpec((tm, E), lambda i: (i, 0)), _next_halo_spec(tm, E, 0, S),
                  pl.BlockSpec((tm, E), lambda i: (i, 0))],
        out_specs=pl.BlockSpec((tm, 2 * E), lambda i: (i, 0)),
        compiler_params=_cparams("parallel"),
    )(dpooled, dpooled, dz)


CONV_CHUNK = 512


def _conv_fwd(h, cw, *, tm, name):
    S = h.shape[0]
    E = D_INNER

    def body(b_ref, c_ref, hh_ref, z_ref, ch_ref, hhh_ref, w_ref, y_ref):
        i = pl.program_id(0)
        for j in range(E // CONV_CHUNK):
            cs = slice(j * CONV_CHUNK, (j + 1) * CONV_CHUNK)
            p = c_ref[:, cs] * hh_ref[:, cs]
            ph = jnp.where(i > 0, ch_ref[:, cs] * hhh_ref[:, cs], 0.0)
            ext = jnp.concatenate([ph, p], axis=0)
            conv = w_ref[2:3, cs] * p + w_ref[1:2, cs] * _shift_down(ext, 1) + w_ref[0:1, cs] * _shift_down(ext, 2)
            z = z_ref[:, cs]
            y_ref[:, cs] = (b_ref[:, cs] * conv * (z * _sigmoid(z))).astype(BF16)

    col = lambda c: pl.BlockSpec((tm, E), lambda i: (i, c))
    return pl.pallas_call(
        body, name=name,
        out_shape=jax.ShapeDtypeStruct((S, E), BF16),
        grid=(S // tm,),
        in_specs=[col(0), col(1), col(2), col(3), _prev_halo_spec(tm, E, 1), _prev_halo_spec(tm, E, 2),
                  pl.BlockSpec((8, E), lambda i: (0, 0))],
        out_specs=pl.BlockSpec((tm, E), lambda i: (i, 0)),
        compiler_params=_cparams("parallel"),
    )(h, h, h, h, h, h, cw)


def _conv_bwd(h, dy, cw, *, tm, name):
    S = h.shape[0]
    E = D_INNER
    nt = S // tm

    def body(b_ref, c_ref, hh_ref, z_ref, dy_ref, ch_ref, hhh_ref, bn_ref, zn_ref, dyn_ref, w_ref, dh_ref, dw_ref):
        i = pl.program_id(0)
        for j in range(E // CONV_CHUNK):
            cs = slice(j * CONV_CHUNK, (j + 1) * CONV_CHUNK)
            w0, w1, w2 = w_ref[0:1, cs], w_ref[1:2, cs], w_ref[2:3, cs]
            c, hh, b, z, dyv = c_ref[:, cs], hh_ref[:, cs], b_ref[:, cs], z_ref[:, cs], dy_ref[:, cs]
            p = c * hh
            ph = jnp.where(i > 0, ch_ref[:, cs] * hhh_ref[:, cs], 0.0)
            ext = jnp.concatenate([ph, p], axis=0)
            pm1 = _shift_down(ext, 1)
            pm2 = _shift_down(ext, 2)
            conv = w2 * p + w1 * pm1 + w0 * pm2
            sg = _sigmoid(z)
            dy0 = dyv * (z * sg)
            dz = dyv * (b * conv) * (sg * (1.0 + z * (1.0 - sg)))
            db = dy0 * conv
            dconv = dy0 * b
            zn = zn_ref[:, cs]
            dconv_n = jnp.where(i < nt - 1, dyn_ref[:, cs] * (zn * _sigmoid(zn)) * bn_ref[:, cs], 0.0)
            dext = jnp.concatenate([dconv, dconv_n], axis=0)
            dp = w2 * dconv + w1 * _shift_up(dext, 1, tm) + w0 * _shift_up(dext, 2, tm)
            dh_ref[:, 0 * E + j * CONV_CHUNK:0 * E + (j + 1) * CONV_CHUNK] = db.astype(BF16)
            dh_ref[:, 1 * E + j * CONV_CHUNK:1 * E + (j + 1) * CONV_CHUNK] = (dp * hh).astype(BF16)
            dh_ref[:, 2 * E + j * CONV_CHUNK:2 * E + (j + 1) * CONV_CHUNK] = (dp * c).astype(BF16)
            dh_ref[:, 3 * E + j * CONV_CHUNK:3 * E + (j + 1) * CONV_CHUNK] = dz.astype(BF16)
            dw = jnp.concatenate([jnp.sum(dconv * pm2, axis=0, keepdims=True),
                                  jnp.sum(dconv * pm1, axis=0, keepdims=True),
                                  jnp.sum(dconv * p, axis=0, keepdims=True),
                                  jnp.zeros((5, CONV_CHUNK), F32)], axis=0)

            @pl.when(i == 0)
            def _():
                dw_ref[:, cs] = dw

            @pl.when(i > 0)
            def _():
                dw_ref[:, cs] += dw

    col = lambda c: pl.BlockSpec((tm, E), lambda i: (i, c))
    return pl.pallas_call(
        body, name=name,
        out_shape=(jax.ShapeDtypeStruct((S, 4 * E), BF16), jax.ShapeDtypeStruct((8, E), F32)),
        grid=(nt,),
        in_specs=[col(0), col(1), col(2), col(3), pl.BlockSpec((tm, E), lambda i: (i, 0)),
                  _prev_halo_spec(tm, E, 1), _prev_halo_spec(tm, E, 2),
                  _next_halo_spec(tm, E, 0, S), _next_halo_spec(tm, E, 3, S), _next_halo_spec(tm, E, 0, S),
                  pl.BlockSpec((8, E), lambda i: (0, 0))],
        out_specs=(pl.BlockSpec((tm, 4 * E), lambda i: (i, 0)), pl.BlockSpec((8, E), lambda i: (0, 0))),
        compiler_params=_cparams("arbitrary"),
    )(h, h, h, h, dy, h, h, h, h, dy, cw)


Z_COLS = D_INNER // LANES
KV_LAT_BLK = D_INNER // KV_LORA
Q_LAT_BLK = (D_INNER + KV_LORA) // Q_LORA
K_ROPE_BLK = (D_INNER + KV_LORA + Q_LORA) // LANES


def _rope(blk, c, s1, s2):
    return blk * c + pltpu.roll(blk, LANES - QK_ROPE // 2, 1) * s1 + pltpu.roll(blk, QK_ROPE // 2, 1) * s2


def _unrope(blk, c, s1, s2):
    return blk * c - pltpu.roll(blk, LANES - QK_ROPE // 2, 1) * s1 - pltpu.roll(blk, QK_ROPE // 2, 1) * s2


def _lat_norm(v, g):
    rstd = lax.rsqrt(jnp.mean(v * v, axis=-1, keepdims=True) + NORM_EPS)
    return v * rstd * g


def _mla_latent_fwd(h, q_norm, kv_norm, tabs, *, tm, name):
    S = h.shape[0]

    def body(kv_ref, q_ref, kr_ref, qg_ref, kvg_ref, c_ref, s1_ref, s2_ref, qn_ref, kvn_ref, krr_ref):
        qn_ref[...] = _lat_norm(q_ref[...], qg_ref[...]).astype(BF16)
        kvn_ref[...] = _lat_norm(kv_ref[...], kvg_ref[...]).astype(BF16)
        krr_ref[...] = _rope(kr_ref[...], c_ref[...], s1_ref[...], s2_ref[...]).astype(BF16)

    tab = pl.BlockSpec((tm, LANES), lambda i: (i, 0))
    return pl.pallas_call(
        body, name=name,
        out_shape=(jax.ShapeDtypeStruct((S, Q_LORA), BF16), jax.ShapeDtypeStruct((S, KV_LORA), BF16),
                   jax.ShapeDtypeStruct((S, LANES), BF16)),
        grid=(S // tm,),
        in_specs=[pl.BlockSpec((tm, KV_LORA), lambda i: (i, KV_LAT_BLK)), pl.BlockSpec((tm, Q_LORA), lambda i: (i, Q_LAT_BLK)),
                  pl.BlockSpec((tm, LANES), lambda i: (i, K_ROPE_BLK)),
                  pl.BlockSpec((1, Q_LORA), lambda i: (0, 0)), pl.BlockSpec((1, KV_LORA), lambda i: (0, 0)), tab, tab, tab],
        out_specs=(pl.BlockSpec((tm, Q_LORA), lambda i: (i, 0)), pl.BlockSpec((tm, KV_LORA), lambda i: (i, 0)), tab),
        compiler_params=_cparams("parallel"),
    )(h, h, h, q_norm, kv_norm, *tabs)


def _mla_q_up(q_n, w_q_pad, tabs, *, tm, name):
    S = q_n.shape[0]

    def body(a_ref, w_ref, c_ref, s1_ref, s2_ref, o_ref):
        a = a_ref[...]
        for hd in range(N_HEADS):
            acc = jnp.dot(a, w_ref[:, hd * HEAD_PAD:(hd + 1) * HEAD_PAD], preferred_element_type=F32)
            o_ref[hd, :, :QK_NOPE] = acc[:, :QK_NOPE].astype(BF16)
            o_ref[hd, :, QK_NOPE:] = _rope(acc[:, QK_NOPE:], c_ref[...], s1_ref[...], s2_ref[...]).astype(BF16)

    tab = pl.BlockSpec((tm, LANES), lambda i: (i, 0))
    return pl.pallas_call(
        body, name=name,
        out_shape=jax.ShapeDtypeStruct((N_HEADS, S, HEAD_PAD), BF16),
        grid=(S // tm,),
        in_specs=[pl.BlockSpec((tm, Q_LORA), lambda i: (i, 0)), pl.BlockSpec((Q_LORA, N_HEADS * HEAD_PAD), lambda i: (0, 0)),
                  tab, tab, tab],
        out_specs=pl.BlockSpec((N_HEADS, tm, HEAD_PAD), lambda i: (0, i, 0)),
        compiler_params=_cparams("parallel"),
    )(q_n, w_q_pad, *tabs)


def _mla_kv_up(kv_n, w_kv, krr, *, tm, name):
    S = kv_n.shape[0]

    def body(a_ref, w_ref, krr_ref, k_ref, v_ref):
        a = a_ref[...]
        for hd in range(N_HEADS):
            acc = jnp.dot(a, w_ref[:, hd * HEAD_PAD:(hd + 1) * HEAD_PAD], preferred_element_type=F32)
            k_ref[hd, :, :QK_NOPE] = acc[:, :QK_NOPE].astype(BF16)
            k_ref[hd, :, QK_NOPE:] = krr_ref[...]
            v_ref[hd] = acc[:, QK_NOPE:].astype(BF16)

    return pl.pallas_call(
        body, name=name,
        out_shape=(jax.ShapeDtypeStruct((N_HEADS, S, HEAD_PAD), BF16), jax.ShapeDtypeStruct((N_HEADS, S, V_DIM), BF16)),
        grid=(S // tm,),
        in_specs=[pl.BlockSpec((tm, KV_LORA), lambda i: (i, 0)), pl.BlockSpec((KV_LORA, N_HEADS * HEAD_PAD), lambda i: (0, 0)),
                  pl.BlockSpec((tm, LANES), lambda i: (i, 0))],
        out_specs=(pl.BlockSpec((N_HEADS, tm, HEAD_PAD), lambda i: (0, i, 0)), pl.BlockSpec((N_HEADS, tm, V_DIM), lambda i: (0, i, 0))),
        compiler_params=_cparams("parallel"),
    )(kv_n, w_kv, krr)


def _flash_fwd(q_full, k_full, v, h, *, tq, name):
    H, S, _ = q_full.shape
    tk = tq

    def body(q_ref, k_ref, v_ref, z_ref, o_ref, y_ref, lse_ref, m_sc, l_sc, acc_sc):
        qi = pl.program_id(1)
        q = q_ref[...]
        m_sc[...] = jnp.full(m_sc.shape, -1e30, F32)
        l_sc[...] = jnp.zeros(l_sc.shape, F32)
        acc_sc[...] = jnp.zeros(acc_sc.shape, F32)
        row = qi * tq + lax.broadcasted_iota(jnp.int32, (tq, tk), 0)
        col0 = lax.broadcasted_iota(jnp.int32, (tq, tk), 1)

        def step(j, carry):
            off = pl.multiple_of(j * tk, tk)
            kj = k_ref[pl.ds(off, tk), :]
            vj = v_ref[pl.ds(off, tk), :]
            s = lax.dot_general(q, kj, (((1,), (1,)), ((), ())), preferred_element_type=F32) * ATTN_SCALE
            s = jnp.where(col0 + j * tk <= row, s, -1e30)
            m_old = m_sc[...]
            m_new = jnp.maximum(m_old, jnp.max(s, axis=-1, keepdims=True))
            p = jnp.exp(s - m_new)
            alpha = jnp.exp(m_old - m_new)
            l_sc[...] = alpha * l_sc[...] + jnp.sum(p, axis=-1, keepdims=True)
            acc_sc[...] = alpha * acc_sc[...] + jnp.dot(p.astype(BF16), vj, preferred_element_type=F32)
            m_sc[...] = m_new
            return carry

        lax.fori_loop(0, qi + 1, step, 0)
        l = l_sc[...]
        o = acc_sc[...] / l
        z = z_ref[...]
        o_ref[...] = o
        y_ref[...] = (o * (z * _sigmoid(z))).astype(BF16)
        lse_ref[...] = m_sc[...] + jnp.log(l)

    return pl.pallas_call(
        body, name=name,
        out_shape=(jax.ShapeDtypeStruct((S, D_INNER), F32), jax.ShapeDtypeStruct((S, D_INNER), BF16),
                   jax.ShapeDtypeStruct((H, S, 1), F32)),
        grid=(H, S // tq),
        in_specs=[pl.BlockSpec((None, tq, HEAD_PAD), lambda hd, i: (hd, i, 0)),
                  pl.BlockSpec((None, S, HEAD_PAD), lambda hd, i: (hd, 0, 0)),
                  pl.BlockSpec((None, S, V_DIM), lambda hd, i: (hd, 0, 0)),
                  pl.BlockSpec((tq, V_DIM), lambda hd, i: (i, hd))],
        out_specs=(pl.BlockSpec((tq, V_DIM), lambda hd, i: (i, hd)), pl.BlockSpec((tq, V_DIM), lambda hd, i: (i, hd)),
                   pl.BlockSpec((None, tq, 1), lambda hd, i: (hd, i, 0))),
        scratch_shapes=[pltpu.VMEM((tq, 1), F32), pltpu.VMEM((tq, 1), F32), pltpu.VMEM((tq, V_DIM), F32)],
        compiler_params=_cparams("parallel", "parallel"),
    )(q_full, k_full, v, h)


def _mla_gate_bwd(dy, o, h, *, tm, name):
    S = dy.shape[0]
    E = D_INNER

    def body(dy_ref, o_ref, z_ref, do_ref, dz_ref, delta_ref):
        for hd in range(N_HEADS):
            cs = slice(hd * V_DIM, (hd + 1) * V_DIM)
            z = z_ref[:, cs]
            sg = _sigmoid(z)
            dyv = dy_ref[:, cs]
            ov = o_ref[:, cs]
            do = dyv * (z * sg)
            do_ref[:, cs] = do.astype(BF16)
            dz_ref[:, cs] = (dyv * ov * (sg * (1.0 + z * (1.0 - sg)))).astype(BF16)
            delta_ref[hd] = jnp.sum(do * ov, axis=-1, keepdims=True)

    row = pl.BlockSpec((tm, E), lambda i: (i, 0))
    return pl.pallas_call(
        body, name=name,
        out_shape=(jax.ShapeDtypeStruct((S, E), BF16), jax.ShapeDtypeStruct((S, E), BF16),
                   jax.ShapeDtypeStruct((N_HEADS, S, 1), F32)),
        grid=(S // tm,),
        in_specs=[row, row, row],
        out_specs=(row, row, pl.BlockSpec((N_HEADS, tm, 1), lambda i: (0, i, 0))),
        compiler_params=_cparams("parallel"),
    )(dy, o, h)


def _flash_bwd(q_full, k_full, v, do, lse_rows, delta_rows, *, tq, name):
    H, S, _ = q_full.shape
    tk = tq
    nq = S // tq

    def body(q_ref, k_ref, v_ref, do_ref, lse_ref, dl_ref, dq_ref, dkv_ref, dkr_ref, dk_sc, dv_sc):
        kj = pl.program_id(1)
        k = k_ref[...]
        vv = v_ref[...]

        @pl.when(kj == 0)
        def _():
            dq_ref[...] = jnp.zeros(dq_ref.shape, F32)

        dk_sc[...] = jnp.zeros(dk_sc.shape, F32)
        dv_sc[...] = jnp.zeros(dv_sc.shape, F32)
        key = kj * tk + lax.broadcasted_iota(jnp.int32, (tk, tq), 0)
        qry0 = lax.broadcasted_iota(jnp.int32, (tk, tq), 1)

        def step(qi, carry):
            off = pl.multiple_of(qi * tq, tq)
            q = q_ref[pl.ds(off, tq), :]
            dov = do_ref[pl.ds(off, tq), :]
            s_t = lax.dot_general(k, q, (((1,), (1,)), ((), ())), preferred_element_type=F32) * ATTN_SCALE
            p_t = jnp.where(key <= qry0 + qi * tq, jnp.exp(s_t - lse_ref[qi]), 0.0)
            pb = p_t.astype(BF16)
            dv_sc[...] += jnp.dot(pb, dov, preferred_element_type=F32)
            dp_t = lax.dot_general(vv, dov, (((1,), (1,)), ((), ())), preferred_element_type=F32)
            ds = (p_t * (dp_t - dl_ref[qi]) * ATTN_SCALE).astype(BF16)
            dk_sc[...] += jnp.dot(ds, q, preferred_element_type=F32)
            dq_ref[pl.ds(off, tq), :] += lax.dot_general(ds, k, (((0,), (0,)), ((), ())), preferred_element_type=F32)
            return carry

        lax.fori_loop(kj, nq, step, 0)
        dkv_ref[:, :QK_NOPE] = dk_sc[:, :QK_NOPE].astype(BF16)
        dkv_ref[:, QK_NOPE:] = dv_sc[...].astype(BF16)
        dkr_ref[...] = dk_sc[:, QK_NOPE:]

    return pl.pallas_call(
        body, name=name,
        out_shape=(jax.ShapeDtypeStruct((S, H * HEAD_PAD), F32), jax.ShapeDtypeStruct((S, H * HEAD_PAD), BF16),
                   jax.ShapeDtypeStruct((H, S, LANES), F32)),
        grid=(H, S // tk),
        in_specs=[pl.BlockSpec((None, S, HEAD_PAD), lambda hd, j: (hd, 0, 0)),
                  pl.BlockSpec((None, tk, HEAD_PAD), lambda hd, j: (hd, j, 0)),
                  pl.BlockSpec((None, tk, V_DIM), lambda hd, j: (hd, j, 0)),
                  pl.BlockSpec((S, V_DIM), lambda hd, j: (0, hd)),
                  pl.BlockSpec((None, nq, 1, tq), lambda hd, j: (hd, 0, 0, 0)),
                  pl.BlockSpec((None, nq, 1, tq), lambda hd, j: (hd, 0, 0, 0))],
        out_specs=(pl.BlockSpec((S, HEAD_PAD), lambda hd, j: (0, hd)), pl.BlockSpec((tk, HEAD_PAD), lambda hd, j: (j, hd)),
                   pl.BlockSpec((None, tk, LANES), lambda hd, j: (hd, j, 0))),
        scratch_shapes=[pltpu.VMEM((tk, HEAD_PAD), F32), pltpu.VMEM((tk, V_DIM), F32)],
        compiler_params=_cparams("parallel", "arbitrary"),
    )(q_full, k_full, v, do, lse_rows, delta_rows)


def _mla_unrope_q(dq, tabs, *, tm, name):
    S, W = dq.shape

    def body(dq_ref, c_ref, s1_ref, s2_ref, o_ref):
        for hd in range(N_HEADS):
            lo = hd * HEAD_PAD
            o_ref[:, lo:lo + QK_NOPE] = dq_ref[:, lo:lo + QK_NOPE].astype(BF16)
            o_ref[:, lo + QK_NOPE:lo + HEAD_PAD] = _unrope(dq_ref[:, lo + QK_NOPE:lo + HEAD_PAD], c_ref[...], s1_ref[...], s2_ref[...]).astype(BF16)

    tab = pl.BlockSpec((tm, LANES), lambda i: (i, 0))
    row = pl.BlockSpec((tm, W), lambda i: (i, 0))
    return pl.pallas_call(
        body, name=name,
        out_shape=jax.ShapeDtypeStruct((S, W), BF16),
        grid=(S // tm,),
        in_specs=[row, tab, tab, tab], out_specs=row,
        compiler_params=_cparams("parallel"),
    )(dq, *tabs)


def _mla_latent_bwd(h, dq_n, dkv_n, dkr, dz, q_norm, kv_norm, tabs, *, tm, name):
    S = h.shape[0]

    def body(kv_ref, q_ref, dqn_ref, dkvn_ref, dkr_ref, dz_ref, qg_ref, kvg_ref, c_ref, s1_ref, s2_ref, dh_ref, dqg_ref, dkvg_ref):
        i = pl.program_id(0)
        dq_lat, dqg = _rms_bwd_math(q_ref[...], qg_ref[...], dqn_ref[...])
        dkv_lat, dkvg = _rms_bwd_math(kv_ref[...], kvg_ref[...], dkvn_ref[...])
        dkr_sum = dkr_ref[0]
        for hd in range(1, N_HEADS):
            dkr_sum = dkr_sum + dkr_ref[hd]
        dh_ref[:, :D_INNER] = dz_ref[...]
        dh_ref[:, D_INNER:D_INNER + KV_LORA] = dkv_lat.astype(BF16)
        dh_ref[:, D_INNER + KV_LORA:D_INNER + KV_LORA + Q_LORA] = dq_lat.astype(BF16)
        dh_ref[:, D_INNER + KV_LORA + Q_LORA:] = _unrope(dkr_sum, c_ref[...], s1_ref[...], s2_ref[...]).astype(BF16)

        @pl.when(i == 0)
        def _():
            dqg_ref[...] = dqg
            dkvg_ref[...] = dkvg

        @pl.when(i > 0)
        def _():
            dqg_ref[...] += dqg
            dkvg_ref[...] += dkvg

    tab = pl.BlockSpec((tm, LANES), lambda i: (i, 0))
    qvec = pl.BlockSpec((1, Q_LORA), lambda i: (0, 0))
    kvvec = pl.BlockSpec((1, KV_LORA), lambda i: (0, 0))
    return pl.pallas_call(
        body, name=name,
        out_shape=(jax.ShapeDtypeStruct((S, MLA_IN_PAD), BF16), jax.ShapeDtypeStruct((1, Q_LORA), F32),
                   jax.ShapeDtypeStruct((1, KV_LORA), F32)),
        grid=(S // tm,),
        in_specs=[pl.BlockSpec((tm, KV_LORA), lambda i: (i, KV_LAT_BLK)), pl.BlockSpec((tm, Q_LORA), lambda i: (i, Q_LAT_BLK)),
                  pl.BlockSpec((tm, Q_LORA), lambda i: (i, 0)), pl.BlockSpec((tm, KV_LORA), lambda i: (i, 0)),
                  pl.BlockSpec((N_HEADS, tm, LANES), lambda i: (0, i, 0)), pl.BlockSpec((tm, D_INNER), lambda i: (i, 0)),
                  qvec, kvvec, tab, tab, tab],
        out_specs=(pl.BlockSpec((tm, MLA_IN_PAD), lambda i: (i, 0)), qvec, kvvec),
        compiler_params=_cparams("arbitrary"),
    )(h, h, dq_n, dkv_n, dkr, dz, q_norm, kv_norm, *tabs)


def _adamw(w, g, m, v, *, name):
    R, C = w.shape
    tr = R
    for cand in (512, 256, 128, 64, 32, 16, 8):
        if R % cand == 0 and cand * C * 4 <= 2 * 1024 * 1024:
            tr = cand
            break

    def body(w_ref, g_ref, m_ref, v_ref, d_ref, nm_ref, nv_ref):
        gv = g_ref[...]
        m_new = ADAM_B1 * m_ref[...] + (1.0 - ADAM_B1) * gv
        v_new = ADAM_B2 * v_ref[...] + (1.0 - ADAM_B2) * (gv * gv)
        m_hat = m_new / (1.0 - ADAM_B1 ** ADAM_STEP)
        v_hat = v_new / (1.0 - ADAM_B2 ** ADAM_STEP)
        d_ref[...] = -ADAM_LR * (m_hat / (jnp.sqrt(v_hat) + ADAM_EPS) + ADAM_WD * w_ref[...])
        nm_ref[...] = m_new
        nv_ref[...] = v_new

    spec = pl.BlockSpec((tr, C), lambda i: (i, 0))
    sds = jax.ShapeDtypeStruct((R, C), F32)
    return pl.pallas_call(
        body, name=name, out_shape=(sds, sds, sds), grid=(R // tr,),
        in_specs=[spec] * 4, out_specs=(spec,) * 3,
        compiler_params=_cparams("parallel"),
    )(w, g, m, v)


HBM_SPEC = pl.BlockSpec(memory_space=pltpu.HBM)


def _place():
    return lax.axis_index("x"), lax.axis_index("y"), lax.axis_index("c")


def _other_chips(x, y):
    return [(1 - x, y), (x, 1 - y), (1 - x, 1 - y)]


def _remote(src, dst, send_sem, recv_sem, dev):
    return pltpu.make_async_remote_copy(src_ref=src, dst_ref=dst, send_sem=send_sem, recv_sem=recv_sem,
                                        device_id=dev, device_id_type=MESH)


def _all_gather_big(wp, *, name):
    R, C = wp.shape
    H = R // 2

    def body(w_ref, out_ref, send_sems, recv_sems, local_sem):
        x, y, c = _place()
        sib = (x, y, 1 - c)
        chips = _other_chips(x, y)

        def blk(px, py, half):
            return out_ref.at[2 * px + py, pl.ds(pl.multiple_of(half * H, 16), H), :]

        mine = pltpu.make_async_copy(w_ref, out_ref.at[2 * x + y], local_sem)
        mine.start()
        my_half = w_ref.at[pl.ds(pl.multiple_of(c * H, 16), H), :]
        first = [_remote(my_half, blk(x, y, c), send_sems.at[r], recv_sems.at[r], (*chip, c)) for r, chip in enumerate(chips)]
        for cp in first:
            cp.start()
        passed = [_remote(blk(*chip, c), blk(*chip, c), send_sems.at[3 + r], recv_sems.at[3 + r], sib) for r, chip in enumerate(chips)]
        for r, chip in enumerate(chips):
            _remote(my_half, blk(*chip, c), send_sems.at[r], recv_sems.at[r], (*chip, c)).wait_recv()
            passed[r].start()
        for r, chip in enumerate(chips):
            _remote(my_half, blk(*chip, 1 - c), send_sems.at[3 + r], recv_sems.at[3 + r], sib).wait_recv()
        for cp in first + passed:
            cp.wait_send()
        mine.wait()

    return pl.pallas_call(
        body, name=name,
        out_shape=jax.ShapeDtypeStruct((N_CHIPS, R, C), wp.dtype),
        in_specs=[HBM_SPEC], out_specs=HBM_SPEC,
        scratch_shapes=[pltpu.SemaphoreType.DMA((6,)), pltpu.SemaphoreType.DMA((6,)), pltpu.SemaphoreType.DMA],
    )(wp)


def _rs_sibling_swap(g, *, name):
    _, R, C = g.shape
    H = R // 2

    def body(g_ref, theirs_ref, mine_ref, send_sems, recv_sems, local_sems):
        x, y, c = _place()
        sib = (x, y, 1 - c)
        copies = []
        for k in range(N_CHIPS):
            away = g_ref.at[k, pl.ds(pl.multiple_of((1 - c) * H, 16), H), :]
            keep = g_ref.at[k, pl.ds(pl.multiple_of(c * H, 16), H), :]
            copies.append(_remote(away, theirs_ref.at[k], send_sems.at[k], recv_sems.at[k], sib))
            copies.append(pltpu.make_async_copy(keep, mine_ref.at[k], local_sems.at[k]))
        for cp in copies:
            cp.start()
        for cp in copies:
            cp.wait()

    out = jax.ShapeDtypeStruct((N_CHIPS, H, C), g.dtype)
    return pl.pallas_call(
        body, name=name, out_shape=(out, out),
        in_specs=[HBM_SPEC], out_specs=(HBM_SPEC, HBM_SPEC),
        scratch_shapes=[pltpu.SemaphoreType.DMA((N_CHIPS,)), pltpu.SemaphoreType.DMA((N_CHIPS,)), pltpu.SemaphoreType.DMA((N_CHIPS,))],
    )(g)


def _add2_bf16(a, b, *, name):
    K, H, C = a.shape
    tr = _row_tile(H)

    def body(a_ref, b_ref, o_ref):
        o_ref[...] = (a_ref[...].astype(F32) + b_ref[...].astype(F32)).astype(o_ref.dtype)

    spec = pl.BlockSpec((None, tr, C), lambda k, i: (k, i, 0))
    return pl.pallas_call(
        body, name=name, out_shape=jax.ShapeDtypeStruct((K, H, C), a.dtype), grid=(K, H // tr),
        in_specs=[spec, spec], out_specs=spec, compiler_params=_cparams("parallel", "parallel"),
    )(a, b)


def _row_tile(h):
    best = 16
    for d in range(16, 1025, 16):
        if h % d == 0:
            best = d
    return best


def _rs_chip_exchange(p, *, name):
    _, H, C = p.shape

    def body(p_ref, recv_ref, own_ref, send_sems, recv_sems, local_sem):
        x, y, c = _place()
        chips = _other_chips(x, y)
        own = pltpu.make_async_copy(p_ref.at[2 * x + y], own_ref, local_sem)
        own.start()
        sends = [_remote(p_ref.at[2 * px + py], recv_ref.at[r], send_sems.at[r], recv_sems.at[r], (px, py, c))
                 for r, (px, py) in enumerate(chips)]
        for cp in sends:
            cp.start()
        for cp in sends:
            cp.wait()
        own.wait()

    return pl.pallas_call(
        body, name=name,
        out_shape=(jax.ShapeDtypeStruct((3, H, C), p.dtype), jax.ShapeDtypeStruct((H, C), p.dtype)),
        in_specs=[HBM_SPEC], out_specs=(HBM_SPEC, HBM_SPEC),
        scratch_shapes=[pltpu.SemaphoreType.DMA((3,)), pltpu.SemaphoreType.DMA((3,)), pltpu.SemaphoreType.DMA],
    )(p)


def _add4_f32(own, recv, *, name):
    H, C = own.shape
    tr = _row_tile(H)

    def body(o_ref, r_ref, out_ref):
        out_ref[...] = ((o_ref[...].astype(F32) + r_ref[0].astype(F32)) + r_ref[1].astype(F32)) + r_ref[2].astype(F32)

    return pl.pallas_call(
        body, name=name, out_shape=jax.ShapeDtypeStruct((H, C), F32), grid=(H // tr,),
        in_specs=[pl.BlockSpec((tr, C), lambda i: (i, 0)), pl.BlockSpec((3, tr, C), lambda i: (0, i, 0))],
        out_specs=pl.BlockSpec((tr, C), lambda i: (i, 0)), compiler_params=_cparams("parallel"),
    )(own, recv)


def _rs_sibling_join(f, *, name):
    H, C = f.shape

    def body(f_ref, out_ref, send_sem, recv_sem, local_sem):
        x, y, c = _place()
        sib = (x, y, 1 - c)
        mine = out_ref.at[pl.ds(pl.multiple_of(c * H, 8), H), :]
        theirs = out_ref.at[pl.ds(pl.multiple_of((1 - c) * H, 8), H), :]
        loc = pltpu.make_async_copy(f_ref, mine, local_sem)
        loc.start()
        cp = _remote(f_ref, mine, send_sem, recv_sem, sib)
        cp.start()
        _remote(f_ref, theirs, send_sem, recv_sem, sib).wait_recv()
        cp.wait_send()
        loc.wait()

    return pl.pallas_call(
        body, name=name, out_shape=jax.ShapeDtypeStruct((2 * H, C), f.dtype),
        in_specs=[HBM_SPEC], out_specs=HBM_SPEC,
        scratch_shapes=[pltpu.SemaphoreType.DMA, pltpu.SemaphoreType.DMA, pltpu.SemaphoreType.DMA],
    )(f)


def _small_exchange(vec, *, reduce, name):
    r, C = vec.shape

    def body(v_ref, out_ref, *rest):
        if reduce:
            all_ref, send_sems, recv_sems = rest
        else:
            all_ref = out_ref
            send_sems, recv_sems = rest
        x, y, c = _place()
        me = 4 * x + 2 * y + c
        all_ref[me] = v_ref[...]
        peers = []
        for rel in range(1, N_DEV):
            dx, dy, dc = (rel >> 2) & 1, (rel >> 1) & 1, rel & 1
            peers.append((1 - x if dx else x, 1 - y if dy else y, 1 - c if dc else c))
        sends = [_remote(v_ref, all_ref.at[me], send_sems.at[k], recv_sems.at[k], peer) for k, peer in enumerate(peers)]
        for cp in sends:
            cp.start()
        for k, (px, py, pc) in enumerate(peers):
            _remote(v_ref, all_ref.at[4 * px + 2 * py + pc], send_sems.at[k], recv_sems.at[k], (px, py, pc)).wait_recv()
        for cp in sends:
            cp.wait_send()
        if reduce:
            s = all_ref[0]
            for k in range(1, N_DEV):
                s = s + all_ref[k]
            out_ref[...] = s

    vm = pl.BlockSpec(memory_space=pltpu.VMEM)
    sems = [pltpu.SemaphoreType.DMA((N_DEV - 1,)), pltpu.SemaphoreType.DMA((N_DEV - 1,))]
    if reduce:
        return pl.pallas_call(
            body, name=name, out_shape=jax.ShapeDtypeStruct((r, C), F32), in_specs=[vm], out_specs=vm,
            scratch_shapes=[pltpu.VMEM((N_DEV, r, C), F32)] + sems,
        )(vec)
    return pl.pallas_call(
        body, name=name, out_shape=jax.ShapeDtypeStruct((N_DEV, r, C), F32), in_specs=[vm], out_specs=vm,
        scratch_shapes=sems,
    )(vec)


BIG = ("pool_w_in", "pool_w_grp", "pool_w_out", "conv_w_in", "conv_w_out", "mla_w_in", "mla_w_q_up", "mla_w_kv_up", "mla_w_out")
BIG_SHARD_AXIS = {"pool_w_in": 2, "pool_w_grp": 2, "pool_w_out": 1, "conv_w_in": 2, "conv_w_out": 1,
                  "mla_w_in": 2, "mla_w_q_up": 2, "mla_w_kv_up": 2, "mla_w_out": 1}
PACK_ROW_ALIGN = 32


def _pack_rows(parts):
    rows = [p.reshape(-1, PACK_COLS) for p in parts]
    n = sum(r.shape[0] for r in rows)
    pad = (-n) % PACK_ROW_ALIGN
    if pad:
        rows.append(jnp.zeros((pad, PACK_COLS), rows[0].dtype))
    return jnp.concatenate(rows, axis=0)


def _unpack_rows(buf, shapes):
    out, r0 = [], 0
    for shp in shapes:
        n = 1
        for d in shp:
            n *= d
        nr = n // PACK_COLS
        out.append(buf[r0:r0 + nr].reshape(shp))
        r0 += nr
    return out


def _join_shards(gathered, shard_shapes, axes):
    per_chip = [_unpack_rows(gathered[k], shard_shapes) for k in range(N_CHIPS)]
    return [jnp.concatenate([per_chip[k][i] for k in range(N_CHIPS)], axis=ax) for i, ax in enumerate(axes)]


def _split_shards(full, axis):
    return jnp.split(full, N_CHIPS, axis=axis)


SMALL = ("pool_norm", "pool_scale", "conv_norm", "conv_w", "mla_norm", "mla_q_norm", "mla_kv_norm", "final_norm")
SMALL_SHARDED = {"pool_norm": True, "pool_scale": True, "conv_norm": False, "conv_w": True, "mla_norm": True,
                 "mla_q_norm": True, "mla_kv_norm": True, "final_norm": False}


def _pack_small(parts):
    flat, offs, n = [], [], 0
    for p in parts:
        v = p.reshape(-1)
        pad = (-v.shape[0]) % LANES
        flat.append(jnp.pad(v, (0, pad)))
        offs.append(n)
        n += v.shape[0] + pad
    pad = (-n) % (8 * LANES)
    if pad:
        flat.append(jnp.zeros((pad,), F32))
    return jnp.concatenate(flat).reshape(-1, LANES), offs


def _unpack_small(buf, offs, shapes):
    v = buf.reshape(-1)
    out = []
    for o, shp in zip(offs, shapes):
        n = 1
        for d in shp:
            n *= d
        out.append(v[o:o + n].reshape(shp))
    return out


def _rope_tables(positions):
    inv_freq = ROPE_BASE ** (-jnp.arange(0, QK_ROPE, 2, dtype=F32) / QK_ROPE)
    ang = positions.astype(F32).reshape(-1, 1) * inv_freq
    cos, sin = jnp.cos(ang), jnp.sin(ang)
    z32 = jnp.zeros_like(cos)
    z64 = jnp.concatenate([z32, z32], axis=1)
    return (jnp.concatenate([cos, cos, z64], axis=1), jnp.concatenate([-sin, z32, z64], axis=1),
            jnp.concatenate([z32, sin, z64], axis=1))


def _mla_in_to_padded(w):
    q, kv, kr, z = w[:, :Q_LORA], w[:, Q_LORA:Q_LORA + KV_LORA], w[:, Q_LORA + KV_LORA:Q_LORA + KV_LORA + QK_ROPE], w[:, Q_LORA + KV_LORA + QK_ROPE:]
    return jnp.concatenate([z, kv, q, kr, jnp.zeros((w.shape[0], MLA_IN_PAD - MLA_IN), w.dtype)], axis=1)


def _mla_in_from_padded(w):
    z, kv, q, kr = w[:, :D_INNER], w[:, D_INNER:D_INNER + KV_LORA], w[:, D_INNER + KV_LORA:D_INNER + KV_LORA + Q_LORA], w[:, D_INNER + KV_LORA + Q_LORA:D_INNER + KV_LORA + Q_LORA + QK_ROPE]
    return jnp.concatenate([q, kv, kr, z], axis=1)


def _q_up_to_padded(w):
    k = w.shape[0]
    return jnp.pad(w.reshape(k, N_HEADS, QK_NOPE + QK_ROPE), ((0, 0), (0, 0), (0, HEAD_PAD - QK_NOPE - QK_ROPE))).reshape(k, N_HEADS * HEAD_PAD)


def _q_up_from_padded(w):
    k = w.shape[0]
    return w.reshape(k, N_HEADS, HEAD_PAD)[:, :, :QK_NOPE + QK_ROPE].reshape(k, N_HEADS * (QK_NOPE + QK_ROPE))


def _local_step(x, positions, target, wb, ws):
    S = x.shape[0]
    tm = min(512, S)
    te = min(256, S)
    tq = min(512, S)
    tabs = _rope_tables(positions)
    gb, gs = {}, {}

    def mm_in(xn, w, name):
        n = w.shape[1]
        return _mm(xn, w, tm=min(1024, S), tn=_pick(n, 1536 if n == MLA_IN_PAD else 1024), tk=D_MODEL, name=name)

    def mm_out(y, w, res, name):
        return _mm(y, w, residual=res, tm=tm, tn=D_MODEL, tk=D_INNER, name=name)

    def mm_dx(dy, w, name):
        k, n = w.shape
        return _mm(dy, w, trans_b=True, tm=min(1024, S), tn=_pick(k, 1024), tk=_pick(n, 1408), name=name)

    def mm_dw(a, b, name):
        ka, nb = a.shape[1], b.shape[1]
        return _mm(a, b, trans_a=True, out_dtype=BF16, tm=_pick(ka, 1024), tn=_pick(nb, 1408), tk=tm, name=name)

    def pool_layer_fwd(xin, j, tag):
        xn = _rms_fwd(xin, ws["pool_norm"][j:j + 1], tm=tm, name=f"{tag}_norm")
        h = mm_in(xn, wb["pool_w_in"][j], f"{tag}_in")
        y = _pool_fwd(h, wb["pool_w_grp"][j], ws["pool_scale"][j:j + 1], tm=te, name=f"{tag}_mix")
        xo = mm_out(y, wb["pool_w_out"][j], xin, f"{tag}_out")
        return xo, (xin, xn, h, y)

    def pool_layer_bwd(dx, dxb, saved, j, tag):
        xin, xn, h, y = saved
        dy = mm_dx(dxb, wb["pool_w_out"][j], f"{tag}_dy")
        dwo = mm_dw(y, dxb, f"{tag}_dwo")
        pooled, dmixed, dpooled, dz, dsc = _pool_bwd1(h, dy, wb["pool_w_grp"][j], ws["pool_scale"][j:j + 1], tm=te, name=f"{tag}_bmix")
        dwg = _grouped_tn(pooled, dmixed, tk=tm, name=f"{tag}_dwg")
        dh = _pool_bwd2(dpooled, dz, tm=te, name=f"{tag}_bshift")
        dxn = mm_dx(dh, wb["pool_w_in"][j], f"{tag}_dxn")
        dwi = mm_dw(xn, dh, f"{tag}_dwi")
        dxo, dxob, dg = _rms_bwd(xin, ws["pool_norm"][j:j + 1], dxn, dx, tm=tm, name=f"{tag}_bnorm")
        return dxo, dxob, dict(pool_w_in=dwi, pool_w_grp=dwg, pool_w_out=dwo), dict(pool_norm=dg, pool_scale=dsc)

    x1, sv0 = pool_layer_fwd(x, 0, "p0")

    xn1 = _rms_fwd(x1, ws["conv_norm"][0:1], tm=tm, name="cv_norm")
    h1 = mm_in(xn1, wb["conv_w_in"][0], "cv_in")
    cw = jnp.pad(ws["conv_w"][0], ((0, 5), (0, 0)))
    y1 = _conv_fwd(h1, cw, tm=te, name="cv_mix")
    x2 = mm_out(y1, wb["conv_w_out"][0], x1, "cv_out")

    w_mi = _mla_in_to_padded(wb["mla_w_in"][0])
    w_q = _q_up_to_padded(wb["mla_w_q_up"][0])
    w_kv = wb["mla_w_kv_up"][0]
    qg, kvg = ws["mla_q_norm"][0:1], ws["mla_kv_norm"][0:1]
    xn2 = _rms_fwd(x2, ws["mla_norm"][0:1], tm=tm, name="ml_norm")
    h2 = mm_in(xn2, w_mi, "ml_in")
    q_n, kv_n, krr = _mla_latent_fwd(h2, qg, kvg, tabs, tm=tm, name="ml_lat")
    q_full = _mla_q_up(q_n, w_q, tabs, tm=tm, name="ml_qup")
    k_full, v = _mla_kv_up(kv_n, w_kv, krr, tm=tm, name="ml_kvup")
    o, y2, lse = _flash_fwd(q_full, k_full, v, h2, tq=tq, name="ml_attn")
    x3 = mm_out(y2, wb["mla_w_out"][0], x2, "ml_out")

    x4, sv3 = pool_layer_fwd(x3, 1, "p1")

    loss_part, dx, dxb, dgf = _final_loss(x4, ws["final_norm"].reshape(1, -1), target, tm=tm, name="final")
    gs["final_norm"] = dgf.reshape(-1)

    dx, dxb, gb3, gs3 = pool_layer_bwd(dx, dxb, sv3, 1, "p1")

    dy = mm_dx(dxb, wb["mla_w_out"][0], "ml_dy")
    gb["mla_w_out"] = mm_dw(y2, dxb, "ml_dwo")[None]
    do, dz, delta = _mla_gate_bwd(dy, o, h2, tm=te, name="ml_bgate")
    nq = S // tq
    dq, dkv, dkr = _flash_bwd(q_full, k_full, v, do, lse.reshape(N_HEADS, nq, 1, tq), delta.reshape(N_HEADS, nq, 1, tq), tq=tq, name="ml_battn")
    dq_pre = _mla_unrope_q(dq, tabs, tm=te, name="ml_bqrope")
    dq_n = mm_dx(dq_pre, w_q, "ml_dqn")
    gb["mla_w_q_up"] = _q_up_from_padded(mm_dw(q_n, dq_pre, "ml_dwq"))[None]
    dkv_n = mm_dx(dkv, w_kv, "ml_dkvn")
    gb["mla_w_kv_up"] = mm_dw(kv_n, dkv, "ml_dwkv")[None]
    dh2, dqg, dkvg = _mla_latent_bwd(h2, dq_n, dkv_n, dkr, dz, qg, kvg, tabs, tm=te, name="ml_blat")
    dxn2 = mm_dx(dh2, w_mi, "ml_dxn")
    gb["mla_w_in"] = _mla_in_from_padded(mm_dw(xn2, dh2, "ml_dwi"))[None]
    dx, dxb, dg2 = _rms_bwd(x2, ws["mla_norm"][0:1], dxn2, dx, tm=tm, name="ml_bnorm")
    gs["mla_norm"], gs["mla_q_norm"], gs["mla_kv_norm"] = dg2, dqg, dkvg

    dy = mm_dx(dxb, wb["conv_w_out"][0], "cv_dy")
    gb["conv_w_out"] = mm_dw(y1, dxb, "cv_dwo")[None]
    dh1, dcw = _conv_bwd(h1, dy, cw, tm=te, name="cv_bmix")
    dxn1 = mm_dx(dh1, wb["conv_w_in"][0], "cv_dxn")
    gb["conv_w_in"] = mm_dw(xn1, dh1, "cv_dwi")[None]
    dx, dxb, dg1 = _rms_bwd(x1, ws["conv_norm"][0:1], dxn1, dx, tm=tm, name="cv_bnorm")
    gs["conv_norm"], gs["conv_w"] = dg1, dcw[None, :3]

    dx, dxb, gb0, gs0 = pool_layer_bwd(dx, dxb, sv0, 0, "p0")
    for k in ("pool_w_in", "pool_w_grp", "pool_w_out"):
        gb[k] = jnp.stack([gb0[k], gb3[k]])
    for k in ("pool_norm", "pool_scale"):
        gs[k] = jnp.concatenate([gs0[k], gs3[k]], axis=0)
    return loss_part, dx, gb, gs


def kernel(x, positions, pool_norm, pool_w_in, pool_w_grp, pool_scale, pool_w_out, conv_norm, conv_w_in, conv_w, conv_w_out, mla_norm, mla_w_in, mla_q_norm, mla_w_q_up, mla_kv_norm, mla_w_kv_up, mla_w_out, final_norm, loss_target, m_pool_norm, m_pool_w_in, m_pool_w_grp, m_pool_scale, m_pool_w_out, m_conv_norm, m_conv_w_in, m_conv_w, m_conv_w_out, m_mla_norm, m_mla_w_in, m_mla_q_norm, m_mla_w_q_up, m_mla_kv_norm, m_mla_w_kv_up, m_mla_w_out, m_final_norm, v_pool_norm, v_pool_w_in, v_pool_w_grp, v_pool_scale, v_pool_w_out, v_conv_norm, v_conv_w_in, v_conv_w, v_conv_w_out, v_mla_norm, v_mla_w_in, v_mla_q_norm, v_mla_w_q_up, v_mla_kv_norm, v_mla_w_kv_up, v_mla_w_out, v_final_norm):
    names = ("pool_norm", "pool_w_in", "pool_w_grp", "pool_scale", "pool_w_out", "conv_norm", "conv_w_in", "conv_w", "conv_w_out",
             "mla_norm", "mla_w_in", "mla_q_norm", "mla_w_q_up", "mla_kv_norm", "mla_w_kv_up", "mla_w_out", "final_norm")
    w = dict(zip(names, (pool_norm, pool_w_in, pool_w_grp, pool_scale, pool_w_out, conv_norm, conv_w_in, conv_w, conv_w_out,
                         mla_norm, mla_w_in, mla_q_norm, mla_w_q_up, mla_kv_norm, mla_w_kv_up, mla_w_out, final_norm)))
    m = dict(zip(names, (m_pool_norm, m_pool_w_in, m_pool_w_grp, m_pool_scale, m_pool_w_out, m_conv_norm, m_conv_w_in, m_conv_w, m_conv_w_out,
                         m_mla_norm, m_mla_w_in, m_mla_q_norm, m_mla_w_q_up, m_mla_kv_norm, m_mla_w_kv_up, m_mla_w_out, m_final_norm)))
    v = dict(zip(names, (v_pool_norm, v_pool_w_in, v_pool_w_grp, v_pool_scale, v_pool_w_out, v_conv_norm, v_conv_w_in, v_conv_w, v_conv_w_out,
                         v_mla_norm, v_mla_w_in, v_mla_q_norm, v_mla_w_q_up, v_mla_kv_norm, v_mla_w_kv_up, v_mla_w_out, v_final_norm)))
    chip = 2 * lax.axis_index("x") + lax.axis_index("y")
    core = lax.axis_index("c")

    big_shapes = [w[n].shape for n in BIG]
    big_axes = [BIG_SHARD_AXIS[n] for n in BIG]
    gathered = _all_gather_big(_pack_rows([w[n].astype(BF16) for n in BIG]), name="ag_big")
    wb = dict(zip(BIG, _join_shards(gathered, big_shapes, big_axes)))

    small_shapes = [w[n].shape for n in SMALL]
    small_pack, small_offs = _pack_small([w[n] for n in SMALL])
    small_all = _small_exchange(small_pack, reduce=False, name="ag_small")
    per_chip = [_unpack_small(small_all[4 * (k // 2) + 2 * (k % 2)], small_offs, small_shapes) for k in range(N_CHIPS)]
    ws = {}
    for i, n in enumerate(SMALL):
        ws[n] = jnp.concatenate([per_chip[k][i] for k in range(N_CHIPS)], axis=-1) if SMALL_SHARDED[n] else w[n]

    loss_part, grad_x, gb, gs = _local_step(x[0], positions, loss_target[0], wb, ws)
    loss = lax.psum(loss_part[0, 0], ("x", "y", "c"))

    packed = jnp.stack([_pack_rows([_split_shards(gb[n], BIG_SHARD_AXIS[n])[k] for n in BIG]) for k in range(N_CHIPS)])
    theirs, mine = _rs_sibling_swap(packed, name="rs_swap")
    chip_sum = _add2_bf16(mine, theirs, name="rs_add2")
    recv, own = _rs_chip_exchange(chip_sum, name="rs_chips")
    half_sum = _add4_f32(own, recv, name="rs_add4")
    g_rows = _rs_sibling_join(half_sum, name="rs_join")
    g = dict(zip(BIG, _unpack_rows(g_rows, big_shapes)))

    full_small_shapes = [gs[n].shape for n in SMALL]
    gs_pack, gs_offs = _pack_small([gs[n] for n in SMALL])
    gs_sum = _unpack_small(_small_exchange(gs_pack, reduce=True, name="ar_small"), gs_offs, full_small_shapes)
    for i, n in enumerate(SMALL):
        if SMALL_SHARDED[n]:
            width = w[n].shape[-1]
            g[n] = lax.dynamic_slice_in_dim(gs_sum[i], chip * width, width, axis=gs_sum[i].ndim - 1)
        else:
            g[n] = gs_sum[i].reshape(w[n].shape)

    delta, new_m, new_v = {}, {}, {}
    for n in BIG:
        shp = w[n].shape
        two_d = lambda a: a.reshape(-1, shp[-1])
        d_, m_, v_ = _adamw(two_d(w[n]), two_d(g[n]), two_d(m[n]), two_d(v[n]), name=f"adamw_{n}")
        delta[n], new_m[n], new_v[n] = d_.reshape(shp), m_.reshape(shp), v_.reshape(shp)
    sw, s_offs = _pack_small([w[n] for n in SMALL])
    sg, _ = _pack_small([g[n] for n in SMALL])
    sm, _ = _pack_small([m[n] for n in SMALL])
    sv, _ = _pack_small([v[n] for n in SMALL])
    sd, snm, snv = _adamw(sw, sg, sm, sv, name="adamw_small")
    for dst, buf in ((delta, sd), (new_m, snm), (new_v, snv)):
        for n, a in zip(SMALL, _unpack_small(buf, s_offs, small_shapes)):
            dst[n] = a

    return (loss, grad_x[None], *[g[n] for n in names], *[delta[n] for n in names],
            *[new_m[n] for n in names], *[new_v[n] for n in names])
```

```python
import functools

import jax
import jax.numpy as jnp
from jax import lax
from jax.experimental import pallas as pl
from jax.experimental.pallas import tpu as pltpu

F32 = jnp.float32
BF16 = jnp.bfloat16

D_MODEL = 1024
D_INNER = 2048
POOL_WINDOWS = (2, 4, 8, 16)
POOL_GROUP = 512
N_HEADS = 16
QK_NOPE = 128
QK_ROPE = 64
V_DIM = 128
HEAD_PAD = 256
Q_LORA = 384
KV_LORA = 256
MLA_IN = Q_LORA + KV_LORA + QK_ROPE + D_INNER
MLA_IN_PAD = 2816
ATTN_SCALE = (QK_NOPE + QK_ROPE) ** -0.5
ROPE_BASE = 10000.0
NORM_EPS = 1e-6
HALO = 16

ADAM_LR = 0.001
ADAM_B1 = 0.9
ADAM_B2 = 0.999
ADAM_EPS = 1e-08
ADAM_WD = 0.01
ADAM_STEP = 10

N_CHIPS = 4
N_DEV = 8
LANES = 128
PACK_COLS = 1024
V7X_VMEM_LIMIT = 56 * 1024 * 1024
MESH = pl.DeviceIdType.MESH


def _cparams(*sem):
    return pltpu.CompilerParams(dimension_semantics=sem, vmem_limit_bytes=V7X_VMEM_LIMIT)


def _pick(n, cap):
    best = None
    for d in range(LANES, min(n, cap) + 1, LANES):
        if n % d == 0:
            best = d
    assert best is not None, (n, cap)
    return best


def _sigmoid(z):
    return 1.0 / (1.0 + jnp.exp(-z))


def _mm(a, b, *, trans_a=False, trans_b=False, out_dtype=F32, residual=None, tm, tn, tk, name):
    if trans_a:
        K, M = a.shape
    else:
        M, K = a.shape
    if trans_b:
        N, K2 = b.shape
    else:
        K2, N = b.shape
    assert K == K2 and M % tm == 0 and N % tn == 0 and K % tk == 0, (name, a.shape, b.shape, tm, tn, tk)
    nk = K // tk
    dn = (((0 if trans_a else 1,), (1 if trans_b else 0,)), ((), ()))
    has_res = residual is not None

    def body(*refs):
        if has_res:
            a_ref, b_ref, r_ref, o_ref = refs[:4]
            rest = refs[4:]
        else:
            a_ref, b_ref, o_ref = refs[:3]
            r_ref = None
            rest = refs[3:]
        part = lax.dot_general(a_ref[...], b_ref[...], dn, preferred_element_type=F32)

        def finish(acc):
            if has_res:
                acc = acc + r_ref[...]
            o_ref[...] = acc.astype(o_ref.dtype)

        if nk == 1:
            finish(part)
        else:
            acc_ref = rest[0]
            k = pl.program_id(2)

            @pl.when(k == 0)
            def _():
                acc_ref[...] = part

            @pl.when(k > 0)
            def _():
                acc_ref[...] += part

            @pl.when(k == nk - 1)
            def _():
                finish(acc_ref[...])

    a_spec = pl.BlockSpec((tk, tm), lambda i, j, k: (k, i)) if trans_a else pl.BlockSpec((tm, tk), lambda i, j, k: (i, k))
    b_spec = pl.BlockSpec((tn, tk), lambda i, j, k: (j, k)) if trans_b else pl.BlockSpec((tk, tn), lambda i, j, k: (k, j))
    o_spec = pl.BlockSpec((tm, tn), lambda i, j, k: (i, j))
    in_specs = [a_spec, b_spec] + ([o_spec] if has_res else [])
    args = (a, b) + ((residual,) if has_res else ())
    return pl.pallas_call(
        body, name=name,
        out_shape=jax.ShapeDtypeStruct((M, N), out_dtype),
        grid=(M // tm, N // tn, nk),
        in_specs=in_specs, out_specs=o_spec,
        scratch_shapes=[pltpu.VMEM((tm, tn), F32)] if nk > 1 else [],
        compiler_params=_cparams("parallel", "parallel", "arbitrary"),
    )(*args)


def _grouped_tn(a, b, *, tk, name):
    S = a.shape[0]
    G = POOL_GROUP
    nk = S // tk

    def body(a_ref, b_ref, o_ref, acc_ref):
        k = pl.program_id(1)
        part = lax.dot_general(a_ref[...], b_ref[...], (((0,), (0,)), ((), ())), preferred_element_type=F32)

        @pl.when(k == 0)
        def _():
            acc_ref[...] = part

        @pl.when(k > 0)
        def _():
            acc_ref[...] += part

        @pl.when(k == nk - 1)
        def _():
            o_ref[...] = acc_ref[...].astype(o_ref.dtype)

    return pl.pallas_call(
        body, name=name,
        out_shape=jax.ShapeDtypeStruct((len(POOL_WINDOWS), G, G), BF16),
        grid=(len(POOL_WINDOWS), nk),
        in_specs=[pl.BlockSpec((tk, G), lambda g, k: (k, g)), pl.BlockSpec((tk, G), lambda g, k: (k, g))],
        out_specs=pl.BlockSpec((None, G, G), lambda g, k: (g, 0, 0)),
        scratch_shapes=[pltpu.VMEM((G, G), F32)],
        compiler_params=_cparams("parallel", "arbitrary"),
    )(a, b)


def _rms_fwd(x, g, *, tm, name):
    S, D = x.shape

    def body(x_ref, g_ref, o_ref):
        xv = x_ref[...]
        rstd = lax.rsqrt(jnp.mean(xv * xv, axis=-1, keepdims=True) + NORM_EPS)
        o_ref[...] = (xv * rstd * g_ref[...]).astype(o_ref.dtype)

    return pl.pallas_call(
        body, name=name,
        out_shape=jax.ShapeDtypeStruct((S, D), BF16),
        grid=(S // tm,),
        in_specs=[pl.BlockSpec((tm, D), lambda i: (i, 0)), pl.BlockSpec((1, D), lambda i: (0, 0))],
        out_specs=pl.BlockSpec((tm, D), lambda i: (i, 0)),
        compiler_params=_cparams("parallel"),
    )(x, g)


def _rms_bwd_math(xv, gv, dxn):
    rstd = lax.rsqrt(jnp.mean(xv * xv, axis=-1, keepdims=True) + NORM_EPS)
    xh = xv * rstd
    dg = jnp.sum(dxn * xh, axis=0, keepdims=True)
    dxh = dxn * gv
    dx = rstd * (dxh - xh * jnp.mean(dxh * xh, axis=-1, keepdims=True))
    return dx, dg


def _rms_bwd(x, g, dxn, dres, *, tm, name):
    S, D = x.shape

    def body(x_ref, g_ref, dxn_ref, dres_ref, dx_ref, dxb_ref, dg_ref):
        dx, dg = _rms_bwd_math(x_ref[...], g_ref[...], dxn_ref[...])
        dx = dx + dres_ref[...]
        dx_ref[...] = dx
        dxb_ref[...] = dx.astype(BF16)

        @pl.when(pl.program_id(0) == 0)
        def _():
            dg_ref[...] = dg

        @pl.when(pl.program_id(0) > 0)
        def _():
            dg_ref[...] += dg

    row = pl.BlockSpec((tm, D), lambda i: (i, 0))
    vec = pl.BlockSpec((1, D), lambda i: (0, 0))
    return pl.pallas_call(
        body, name=name,
        out_shape=(jax.ShapeDtypeStruct((S, D), F32), jax.ShapeDtypeStruct((S, D), BF16), jax.ShapeDtypeStruct((1, D), F32)),
        grid=(S // tm,),
        in_specs=[row, vec, row, row],
        out_specs=(row, row, vec),
        compiler_params=_cparams("arbitrary"),
    )(x, g, dxn, dres)


def _final_loss(x, g, target, *, tm, name):
    S, D = x.shape

    def body(x_ref, g_ref, t_ref, loss_ref, dx_ref, dxb_ref, dg_ref):
        xv = x_ref[...]
        gv = g_ref[...]
        rstd = lax.rsqrt(jnp.mean(xv * xv, axis=-1, keepdims=True) + NORM_EPS)
        xh = xv * rstd
        err = xh * gv - t_ref[...]
        part = 0.5 * jnp.sum(jnp.mean(err * err, axis=-1, keepdims=True), axis=0, keepdims=True)
        dy = err * (1.0 / D)
        dg = jnp.sum(dy * xh, axis=0, keepdims=True)
        dxh = dy * gv
        dx = rstd * (dxh - xh * jnp.mean(dxh * xh, axis=-1, keepdims=True))
        dx_ref[...] = dx
        dxb_ref[...] = dx.astype(BF16)
        lossb = jnp.broadcast_to(part, loss_ref.shape)

        @pl.when(pl.program_id(0) == 0)
        def _():
            dg_ref[...] = dg
            loss_ref[...] = lossb

        @pl.when(pl.program_id(0) > 0)
        def _():
            dg_ref[...] += dg
            loss_ref[...] += lossb

    row = pl.BlockSpec((tm, D), lambda i: (i, 0))
    vec = pl.BlockSpec((1, D), lambda i: (0, 0))
    lspec = pl.BlockSpec((8, LANES), lambda i: (0, 0))
    return pl.pallas_call(
        body, name=name,
        out_shape=(jax.ShapeDtypeStruct((8, LANES), F32), jax.ShapeDtypeStruct((S, D), F32),
                   jax.ShapeDtypeStruct((S, D), BF16), jax.ShapeDtypeStruct((1, D), F32)),
        grid=(S // tm,),
        in_specs=[row, vec, row],
        out_specs=(lspec, row, row, vec),
        compiler_params=_cparams("arbitrary"),
    )(x, g, target)


def _prev_halo_spec(tm, width, col):
    r = tm // HALO
    return pl.BlockSpec((HALO, width), lambda i: (jnp.maximum(i * r - 1, 0), col))


def _next_halo_spec(tm, width, col, S):
    r = tm // HALO
    last = S // HALO - 1
    return pl.BlockSpec((HALO, width), lambda i: (jnp.minimum((i + 1) * r, last), col))


def _shift_down(ext, k):
    return pltpu.roll(ext, k, 0)[HALO:, :]


def _shift_up(ext, k, tm):
    n = ext.shape[0]
    return pltpu.roll(ext, n - k, 0)[:tm, :]


def _pool_window_sum(ext, w):
    s = ext
    k = 1
    while k < w:
        s = s + pltpu.roll(s, k, 0)
        k *= 2
    return s[HALO:, :]


def _pooled_group(u_ref, halo, g, w, t_idx):
    cs = slice(g * POOL_GROUP, (g + 1) * POOL_GROUP)
    u = u_ref[:, cs]
    ext = jnp.concatenate([halo[:, cs], u], axis=0)
    inv = 1.0 / jnp.minimum(t_idx + 1, w).astype(F32)
    return _pool_window_sum(ext, w) * inv - u


def _pool_fwd(h, w_grp, scale, *, tm, name):
    S = h.shape[0]
    E = D_INNER

    def body(u_ref, uh_ref, z_ref, wg_ref, sc_ref, y_ref):
        i = pl.program_id(0)
        halo = jnp.where(i > 0, uh_ref[...], 0.0)
        t_idx = i * tm + lax.broadcasted_iota(jnp.int32, (tm, 1), 0)
        for g, w in enumerate(POOL_WINDOWS):
            cs = slice(g * POOL_GROUP, (g + 1) * POOL_GROUP)
            pooled = _pooled_group(u_ref, halo, g, w, t_idx)
            mixed = jnp.dot(pooled.astype(BF16), wg_ref[g], preferred_element_type=F32)
            z = z_ref[:, cs]
            y_ref[:, cs] = (mixed * sc_ref[:, cs] * (z * _sigmoid(z))).astype(BF16)

    return pl.pallas_call(
        body, name=name,
        out_shape=jax.ShapeDtypeStruct((S, E), BF16),
        grid=(S // tm,),
        in_specs=[pl.BlockSpec((tm, E), lambda i: (i, 0)), _prev_halo_spec(tm, E, 0),
                  pl.BlockSpec((tm, E), lambda i: (i, 1)),
                  pl.BlockSpec((len(POOL_WINDOWS), POOL_GROUP, POOL_GROUP), lambda i: (0, 0, 0)),
                  pl.BlockSpec((1, E), lambda i: (0, 0))],
        out_specs=pl.BlockSpec((tm, E), lambda i: (i, 0)),
        compiler_params=_cparams("parallel"),
    )(h, h, h, w_grp, scale)


def _pool_bwd1(h, dy, w_grp, scale, *, tm, name):
    S = h.shape[0]
    E = D_INNER

    def body(u_ref, uh_ref, z_ref, dy_ref, wg_ref, sc_ref, pooled_ref, dmixed_ref, dpooled_ref, dz_ref, dsc_ref):
        i = pl.program_id(0)
        halo = jnp.where(i > 0, uh_ref[...], 0.0)
        t_idx = i * tm + lax.broadcasted_iota(jnp.int32, (tm, 1), 0)
        for g, w in enumerate(POOL_WINDOWS):
            cs = slice(g * POOL_GROUP, (g + 1) * POOL_GROUP)
            pooled = _pooled_group(u_ref, halo, g, w, t_idx).astype(BF16)
            wg = wg_ref[g]
            mixed = jnp.dot(pooled, wg, preferred_element_type=F32)
            z = z_ref[:, cs]
            sg = _sigmoid(z)
            dyv = dy_ref[:, cs]
            sc = sc_ref[:, cs]
            dms = dyv * (z * sg)
            dz = dyv * (mixed * sc) * (sg * (1.0 + z * (1.0 - sg)))
            dsc = jnp.sum(dms * mixed, axis=0, keepdims=True)
            dmixed = (dms * sc).astype(BF16)
            dpooled = lax.dot_general(dmixed, wg, (((1,), (1,)), ((), ())), preferred_element_type=F32)
            pooled_ref[:, cs] = pooled
            dmixed_ref[:, cs] = dmixed
            dpooled_ref[:, cs] = dpooled
            dz_ref[:, cs] = dz.astype(BF16)

            @pl.when(i == 0)
            def _():
                dsc_ref[:, cs] = dsc

            @pl.when(i > 0)
            def _():
                dsc_ref[:, cs] += dsc

    row = pl.BlockSpec((tm, E), lambda i: (i, 0))
    vec = pl.BlockSpec((1, E), lambda i: (0, 0))
    return pl.pallas_call(
        body, name=name,
        out_shape=(jax.ShapeDtypeStruct((S, E), BF16), jax.ShapeDtypeStruct((S, E), BF16),
                   jax.ShapeDtypeStruct((S, E), F32), jax.ShapeDtypeStruct((S, E), BF16),
                   jax.ShapeDtypeStruct((1, E), F32)),
        grid=(S // tm,),
        in_specs=[row, _prev_halo_spec(tm, E, 0), pl.BlockSpec((tm, E), lambda i: (i, 1)), row,
                  pl.BlockSpec((len(POOL_WINDOWS), POOL_GROUP, POOL_GROUP), lambda i: (0, 0, 0)), vec],
        out_specs=(row, row, row, row, vec),
        compiler_params=_cparams("arbitrary"),
    )(h, h, h, dy, w_grp, scale)


def _pool_bwd2(dpooled, dz, *, tm, name):
    S = dpooled.shape[0]
    E = D_INNER
    nt = S // tm

    def body(dp_ref, dpn_ref, dz_ref, dh_ref):
        i = pl.program_id(0)
        nxt = jnp.where(i < nt - 1, dpn_ref[...], 0.0)
        t_ext = i * tm + lax.broadcasted_iota(jnp.int32, (tm + HALO, 1), 0)
        for g, w in enumerate(POOL_WINDOWS):
            cs = slice(g * POOL_GROUP, (g + 1) * POOL_GROUP)
            dp = dp_ref[:, cs]
            inv = 1.0 / jnp.minimum(t_ext + 1, w).astype(F32)
            s = jnp.concatenate([dp, nxt[:, cs]], axis=0) * inv
            n = tm + HALO
            k = 1
            while k < w:
                s = s + pltpu.roll(s, n - k, 0)
                k *= 2
            dh_ref[:, cs] = (s[:tm, :] - dp).astype(BF16)
        dh_ref[:, E:] = dz_ref[...]

    return pl.pallas_call(
        body, name=name,
        out_shape=jax.ShapeDtypeStruct((S, 2 * E), BF16),
        grid=(nt,),
        in_specs=[pl.BlockSpec((tm, E), lambda i: (i, 0)), _next_halo_spec(tm, E, 0, S),
                  pl.BlockSpec((tm, E), lambda i: (i, 0))],
        out_specs=pl.BlockSpec((tm, 2 * E), lambda i: (i, 0)),
        compiler_params=_cparams("parallel"),
    )(dpooled, dpooled, dz)


CONV_CHUNK = 512


def _conv_fwd(h, cw, *, tm, name):
    S = h.shape[0]
    E = D_INNER

    def body(b_ref, c_ref, hh_ref, z_ref, ch_ref, hhh_ref, w_ref, y_ref):
        i = pl.program_id(0)
        for j in range(E // CONV_CHUNK):
            cs = slice(j * CONV_CHUNK, (j + 1) * CONV_CHUNK)
            p = c_ref[:, cs] * hh_ref[:, cs]
            ph = jnp.where(i > 0, ch_ref[:, cs] * hhh_ref[:, cs], 0.0)
            ext = jnp.concatenate([ph, p], axis=0)
            conv = w_ref[2:3, cs] * p + w_ref[1:2, cs] * _shift_down(ext, 1) + w_ref[0:1, cs] * _shift_down(ext, 2)
            z = z_ref[:, cs]
            y_ref[:, cs] = (b_ref[:, cs] * conv * (z * _sigmoid(z))).astype(BF16)

    col = lambda c: pl.BlockSpec((tm, E), lambda i: (i, c))
    return pl.pallas_call(
        body, name=name,
        out_shape=jax.ShapeDtypeStruct((S, E), BF16),
        grid=(S // tm,),
        in_specs=[col(0), col(1), col(2), col(3), _prev_halo_spec(tm, E, 1), _prev_halo_spec(tm, E, 2),
                  pl.BlockSpec((8, E), lambda i: (0, 0))],
        out_specs=pl.BlockSpec((tm, E), lambda i: (i, 0)),
        compiler_params=_cparams("parallel"),
    )(h, h, h, h, h, h, cw)


def _conv_bwd(h, dy, cw, *, tm, name):
    S = h.shape[0]
    E = D_INNER
    nt = S // tm

    def body(b_ref, c_ref, hh_ref, z_ref, dy_ref, ch_ref, hhh_ref, bn_ref, zn_ref, dyn_ref, w_ref, dh_ref, dw_ref):
        i = pl.program_id(0)
        for j in range(E // CONV_CHUNK):
            cs = slice(j * CONV_CHUNK, (j + 1) * CONV_CHUNK)
            w0, w1, w2 = w_ref[0:1, cs], w_ref[1:2, cs], w_ref[2:3, cs]
            c, hh, b, z, dyv = c_ref[:, cs], hh_ref[:, cs], b_ref[:, cs], z_ref[:, cs], dy_ref[:, cs]
            p = c * hh
            ph = jnp.where(i > 0, ch_ref[:, cs] * hhh_ref[:, cs], 0.0)
            ext = jnp.concatenate([ph, p], axis=0)
            pm1 = _shift_down(ext, 1)
            pm2 = _shift_down(ext, 2)
            conv = w2 * p + w1 * pm1 + w0 * pm2
            sg = _sigmoid(z)
            dy0 = dyv * (z * sg)
            dz = dyv * (b * conv) * (sg * (1.0 + z * (1.0 - sg)))
            db = dy0 * conv
            dconv = dy0 * b
            zn = zn_ref[:, cs]
            dconv_n = jnp.where(i < nt - 1, dyn_ref[:, cs] * (zn * _sigmoid(zn)) * bn_ref[:, cs], 0.0)
            dext = jnp.concatenate([dconv, dconv_n], axis=0)
            dp = w2 * dconv + w1 * _shift_up(dext, 1, tm) + w0 * _shift_up(dext, 2, tm)
            dh_ref[:, 0 * E + j * CONV_CHUNK:0 * E + (j + 1) * CONV_CHUNK] = db.astype(BF16)
            dh_ref[:, 1 * E + j * CONV_CHUNK:1 * E + (j + 1) * CONV_CHUNK] = (dp * hh).astype(BF16)
            dh_ref[:, 2 * E + j * CONV_CHUNK:2 * E + (j + 1) * CONV_CHUNK] = (dp * c).astype(BF16)
            dh_ref[:, 3 * E + j * CONV_CHUNK:3 * E + (j + 1) * CONV_CHUNK] = dz.astype(BF16)
            dw = jnp.concatenate([jnp.sum(dconv * pm2, axis=0, keepdims=True),
                                  jnp.sum(dconv * pm1, axis=0, keepdims=True),
                                  jnp.sum(dconv * p, axis=0, keepdims=True),
                                  jnp.zeros((5, CONV_CHUNK), F32)], axis=0)

            @pl.when(i == 0)
            def _():
                dw_ref[:, cs] = dw

            @pl.when(i > 0)
            def _():
                dw_ref[:, cs] += dw

    col = lambda c: pl.BlockSpec((tm, E), lambda i: (i, c))
    return pl.pallas_call(
        body, name=name,
        out_shape=(jax.ShapeDtypeStruct((S, 4 * E), BF16), jax.ShapeDtypeStruct((8, E), F32)),
        grid=(nt,),
        in_specs=[col(0), col(1), col(2), col(3), pl.BlockSpec((tm, E), lambda i: (i, 0)),
                  _prev_halo_spec(tm, E, 1), _prev_halo_spec(tm, E, 2),
                  _next_halo_spec(tm, E, 0, S), _next_halo_spec(tm, E, 3, S), _next_halo_spec(tm, E, 0, S),
                  pl.BlockSpec((8, E), lambda i: (0, 0))],
        out_specs=(pl.BlockSpec((tm, 4 * E), lambda i: (i, 0)), pl.BlockSpec((8, E), lambda i: (0, 0))),
        compiler_params=_cparams("arbitrary"),
    )(h, h, h, h, dy, h, h, h, h, dy, cw)


Z_COLS = D_INNER // LANES
KV_LAT_BLK = D_INNER // KV_LORA
Q_LAT_BLK = (D_INNER + KV_LORA) // Q_LORA
K_ROPE_BLK = (D_INNER + KV_LORA + Q_LORA) // LANES


def _rope(blk, c, s1, s2):
    return blk * c + pltpu.roll(blk, LANES - QK_ROPE // 2, 1) * s1 + pltpu.roll(blk, QK_ROPE // 2, 1) * s2


def _unrope(blk, c, s1, s2):
    return blk * c - pltpu.roll(blk, LANES - QK_ROPE // 2, 1) * s1 - pltpu.roll(blk, QK_ROPE // 2, 1) * s2


def _lat_norm(v, g):
    rstd = lax.rsqrt(jnp.mean(v * v, axis=-1, keepdims=True) + NORM_EPS)
    return v * rstd * g


def _mla_latent_fwd(h, q_norm, kv_norm, tabs, *, tm, name):
    S = h.shape[0]

    def body(kv_ref, q_ref, kr_ref, qg_ref, kvg_ref, c_ref, s1_ref, s2_ref, qn_ref, kvn_ref, krr_ref):
        qn_ref[...] = _lat_norm(q_ref[...], qg_ref[...]).astype(BF16)
        kvn_ref[...] = _lat_norm(kv_ref[...], kvg_ref[...]).astype(BF16)
        krr_ref[...] = _rope(kr_ref[...], c_ref[...], s1_ref[...], s2_ref[...]).astype(BF16)

    tab = pl.BlockSpec((tm, LANES), lambda i: (i, 0))
    return pl.pallas_call(
        body, name=name,
        out_shape=(jax.ShapeDtypeStruct((S, Q_LORA), BF16), jax.ShapeDtypeStruct((S, KV_LORA), BF16),
                   jax.ShapeDtypeStruct((S, LANES), BF16)),
        grid=(S // tm,),
        in_specs=[pl.BlockSpec((tm, KV_LORA), lambda i: (i, KV_LAT_BLK)), pl.BlockSpec((tm, Q_LORA), lambda i: (i, Q_LAT_BLK)),
                  pl.BlockSpec((tm, LANES), lambda i: (i, K_ROPE_BLK)),
                  pl.BlockSpec((1, Q_LORA), lambda i: (0, 0)), pl.BlockSpec((1, KV_LORA), lambda i: (0, 0)), tab, tab, tab],
        out_specs=(pl.BlockSpec((tm, Q_LORA), lambda i: (i, 0)), pl.BlockSpec((tm, KV_LORA), lambda i: (i, 0)), tab),
        compiler_params=_cparams("parallel"),
    )(h, h, h, q_norm, kv_norm, *tabs)


def _mla_q_up(q_n, w_q_pad, tabs, *, tm, name):
    S = q_n.shape[0]

    def body(a_ref, w_ref, c_ref, s1_ref, s2_ref, o_ref):
        a = a_ref[...]
        for hd in range(N_HEADS):
            acc = jnp.dot(a, w_ref[:, hd * HEAD_PAD:(hd + 1) * HEAD_PAD], preferred_element_type=F32)
            o_ref[hd, :, :QK_NOPE] = acc[:, :QK_NOPE].astype(BF16)
            o_ref[hd, :, QK_NOPE:] = _rope(acc[:, QK_NOPE:], c_ref[...], s1_ref[...], s2_ref[...]).astype(BF16)

    tab = pl.BlockSpec((tm, LANES), lambda i: (i, 0))
    return pl.pallas_call(
        body, name=name,
        out_shape=jax.ShapeDtypeStruct((N_HEADS, S, HEAD_PAD), BF16),
        grid=(S // tm,),
        in_specs=[pl.BlockSpec((tm, Q_LORA), lambda i: (i, 0)), pl.BlockSpec((Q_LORA, N_HEADS * HEAD_PAD), lambda i: (0, 0)),
                  tab, tab, tab],
        out_specs=pl.BlockSpec((N_HEADS, tm, HEAD_PAD), lambda i: (0, i, 0)),
        compiler_params=_cparams("parallel"),
    )(q_n, w_q_pad, *tabs)


def _mla_kv_up(kv_n, w_kv, krr, *, tm, name):
    S = kv_n.shape[0]

    def body(a_ref, w_ref, krr_ref, k_ref, v_ref):
        a = a_ref[...]
        for hd in range(N_HEADS):
            acc = jnp.dot(a, w_ref[:, hd * HEAD_PAD:(hd + 1) * HEAD_PAD], preferred_element_type=F32)
            k_ref[hd, :, :QK_NOPE] = acc[:, :QK_NOPE].astype(BF16)
            k_ref[hd, :, QK_NOPE:] = krr_ref[...]
            v_ref[hd] = acc[:, QK_NOPE:].astype(BF16)

    return pl.pallas_call(
        body, name=name,
        out_shape=(jax.ShapeDtypeStruct((N_HEADS, S, HEAD_PAD), BF16), jax.ShapeDtypeStruct((N_HEADS, S, V_DIM), BF16)),
        grid=(S // tm,),
        in_specs=[pl.BlockSpec((tm, KV_LORA), lambda i: (i, 0)), pl.BlockSpec((KV_LORA, N_HEADS * HEAD_PAD), lambda i: (0, 0)),
                  pl.BlockSpec((tm, LANES), lambda i: (i, 0))],
        out_specs=(pl.BlockSpec((N_HEADS, tm, HEAD_PAD), lambda i: (0, i, 0)), pl.BlockSpec((N_HEADS, tm, V_DIM), lambda i: (0, i, 0))),
        compiler_params=_cparams("parallel"),
    )(kv_n, w_kv, krr)


def _flash_fwd(q_full, k_full, v, h, *, tq, name):
    H, S, _ = q_full.shape
    tk = tq

    def body(q_ref, k_ref, v_ref, z_ref, o_ref, y_ref, lse_ref, m_sc, l_sc, acc_sc):
        qi = pl.program_id(1)
        q = q_ref[...]
        m_sc[...] = jnp.full(m_sc.shape, -1e30, F32)
        l_sc[...] = jnp.zeros(l_sc.shape, F32)
        acc_sc[...] = jnp.zeros(acc_sc.shape, F32)
        row = qi * tq + lax.broadcasted_iota(jnp.int32, (tq, tk), 0)
        col0 = lax.broadcasted_iota(jnp.int32, (tq, tk), 1)

        def step(j, carry):
            off = pl.multiple_of(j * tk, tk)
            kj = k_ref[pl.ds(off, tk), :]
            vj = v_ref[pl.ds(off, tk), :]
            s = lax.dot_general(q, kj, (((1,), (1,)), ((), ())), preferred_element_type=F32) * ATTN_SCALE
            s = jnp.where(col0 + j * tk <= row, s, -1e30)
            m_old = m_sc[...]
            m_new = jnp.maximum(m_old, jnp.max(s, axis=-1, keepdims=True))
            p = jnp.exp(s - m_new)
            alpha = jnp.exp(m_old - m_new)
            l_sc[...] = alpha * l_sc[...] + jnp.sum(p, axis=-1, keepdims=True)
            acc_sc[...] = alpha * acc_sc[...] + jnp.dot(p.astype(BF16), vj, preferred_element_type=F32)
            m_sc[...] = m_new
            return carry

        lax.fori_loop(0, qi + 1, step, 0)
        l = l_sc[...]
        o = acc_sc[...] / l
        z = z_ref[...]
        o_ref[...] = o
        y_ref[...] = (o * (z * _sigmoid(z))).astype(BF16)
        lse_ref[...] = m_sc[...] + jnp.log(l)

    return pl.pallas_call(
        body, name=name,
        out_shape=(jax.ShapeDtypeStruct((S, D_INNER), F32), jax.ShapeDtypeStruct((S, D_INNER), BF16),
                   jax.ShapeDtypeStruct((H, S, 1), F32)),
        grid=(H, S // tq),
        in_specs=[pl.BlockSpec((None, tq, HEAD_PAD), lambda hd, i: (hd, i, 0)),
                  pl.BlockSpec((None, S, HEAD_PAD), lambda hd, i: (hd, 0, 0)),
                  pl.BlockSpec((None, S, V_DIM), lambda hd, i: (hd, 0, 0)),
                  pl.BlockSpec((tq, V_DIM), lambda hd, i: (i, hd))],
        out_specs=(pl.BlockSpec((tq, V_DIM), lambda hd, i: (i, hd)), pl.BlockSpec((tq, V_DIM), lambda hd, i: (i, hd)),
                   pl.BlockSpec((None, tq, 1), lambda hd, i: (hd, i, 0))),
        scratch_shapes=[pltpu.VMEM((tq, 1), F32), pltpu.VMEM((tq, 1), F32), pltpu.VMEM((tq, V_DIM), F32)],
        compiler_params=_cparams("parallel", "parallel"),
    )(q_full, k_full, v, h)


def _mla_gate_bwd(dy, o, h, *, tm, name):
    S = dy.shape[0]
    E = D_INNER

    def body(dy_ref, o_ref, z_ref, do_ref, dz_ref, delta_ref):
        for hd in range(N_HEADS):
            cs = slice(hd * V_DIM, (hd + 1) * V_DIM)
            z = z_ref[:, cs]
            sg = _sigmoid(z)
            dyv = dy_ref[:, cs]
            ov = o_ref[:, cs]
            do = dyv * (z * sg)
            do_ref[:, cs] = do.astype(BF16)
            dz_ref[:, cs] = (dyv * ov * (sg * (1.0 + z * (1.0 - sg)))).astype(BF16)
            delta_ref[hd] = jnp.sum(do * ov, axis=-1, keepdims=True)

    row = pl.BlockSpec((tm, E), lambda i: (i, 0))
    return pl.pallas_call(
        body, name=name,
        out_shape=(jax.ShapeDtypeStruct((S, E), BF16), jax.ShapeDtypeStruct((S, E), BF16),
                   jax.ShapeDtypeStruct((N_HEADS, S, 1), F32)),
        grid=(S // tm,),
        in_specs=[row, row, row],
        out_specs=(row, row, pl.BlockSpec((N_HEADS, tm, 1), lambda i: (0, i, 0))),
        compiler_params=_cparams("parallel"),
    )(dy, o, h)


def _flash_bwd(q_full, k_full, v, do, lse_rows, delta_rows, *, tq, name):
    H, S, _ = q_full.shape
    tk = tq
    nq = S // tq

    def body(q_ref, k_ref, v_ref, do_ref, lse_ref, dl_ref, dq_ref, dkv_ref, dkr_ref, dk_sc, dv_sc):
        kj = pl.program_id(1)
        k = k_ref[...]
        vv = v_ref[...]

        @pl.when(kj == 0)
        def _():
            dq_ref[...] = jnp.zeros(dq_ref.shape, F32)

        dk_sc[...] = jnp.zeros(dk_sc.shape, F32)
        dv_sc[...] = jnp.zeros(dv_sc.shape, F32)
        key = kj * tk + lax.broadcasted_iota(jnp.int32, (tk, tq), 0)
        qry0 = lax.broadcasted_iota(jnp.int32, (tk, tq), 1)

        def step(qi, carry):
            off = pl.multiple_of(qi * tq, tq)
            q = q_ref[pl.ds(off, tq), :]
            dov = do_ref[pl.ds(off, tq), :]
            s_t = lax.dot_general(k, q, (((1,), (1,)), ((), ())), preferred_element_type=F32) * ATTN_SCALE
            p_t = jnp.where(key <= qry0 + qi * tq, jnp.exp(s_t - lse_ref[qi]), 0.0)
            pb = p_t.astype(BF16)
            dv_sc[...] += jnp.dot(pb, dov, preferred_element_type=F32)
            dp_t = lax.dot_general(vv, dov, (((1,), (1,)), ((), ())), preferred_element_type=F32)
            ds = (p_t * (dp_t - dl_ref[qi]) * ATTN_SCALE).astype(BF16)
            dk_sc[...] += jnp.dot(ds, q, preferred_element_type=F32)
            dq_ref[pl.ds(off, tq), :] += lax.dot_general(ds, k, (((0,), (0,)), ((), ())), preferred_element_type=F32)
            return carry

        lax.fori_loop(kj, nq, step, 0)
        dkv_ref[:, :QK_NOPE] = dk_sc[:, :QK_NOPE].astype(BF16)
        dkv_ref[:, QK_NOPE:] = dv_sc[...].astype(BF16)
        dkr_ref[...] = dk_sc[:, QK_NOPE:]

    return pl.pallas_call(
        body, name=name,
        out_shape=(jax.ShapeDtypeStruct((S, H * HEAD_PAD), F32), jax.ShapeDtypeStruct((S, H * HEAD_PAD), BF16),
                   jax.ShapeDtypeStruct((H, S, LANES), F32)),
        grid=(H, S // tk),
        in_specs=[pl.BlockSpec((None, S, HEAD_PAD), lambda hd, j: (hd, 0, 0)),
                  pl.BlockSpec((None, tk, HEAD_PAD), lambda hd, j: (hd, j, 0)),
                  pl.BlockSpec((None, tk, V_DIM), lambda hd, j: (hd, j, 0)),
                  pl.BlockSpec((S, V_DIM), lambda hd, j: (0, hd)),
                  pl.BlockSpec((None, nq, 1, tq), lambda hd, j: (hd, 0, 0, 0)),
                  pl.BlockSpec((None, nq, 1, tq), lambda hd, j: (hd, 0, 0, 0))],
        out_specs=(pl.BlockSpec((S, HEAD_PAD), lambda hd, j: (0, hd)), pl.BlockSpec((tk, HEAD_PAD), lambda hd, j: (j, hd)),
                   pl.BlockSpec((None, tk, LANES), lambda hd, j: (hd, j, 0))),
        scratch_shapes=[pltpu.VMEM((tk, HEAD_PAD), F32), pltpu.VMEM((tk, V_DIM), F32)],
        compiler_params=_cparams("parallel", "arbitrary"),
    )(q_full, k_full, v, do, lse_rows, delta_rows)


def _mla_unrope_q(dq, tabs, *, tm, name):
    S, W = dq.shape

    def body(dq_ref, c_ref, s1_ref, s2_ref, o_ref):
        for hd in range(N_HEADS):
            lo = hd * HEAD_PAD
            o_ref[:, lo:lo + QK_NOPE] = dq_ref[:, lo:lo + QK_NOPE].astype(BF16)
            o_ref[:, lo + QK_NOPE:lo + HEAD_PAD] = _unrope(dq_ref[:, lo + QK_NOPE:lo + HEAD_PAD], c_ref[...], s1_ref[...], s2_ref[...]).astype(BF16)

    tab = pl.BlockSpec((tm, LANES), lambda i: (i, 0))
    row = pl.BlockSpec((tm, W), lambda i: (i, 0))
    return pl.pallas_call(
        body, name=name,
        out_shape=jax.ShapeDtypeStruct((S, W), BF16),
        grid=(S // tm,),
        in_specs=[row, tab, tab, tab], out_specs=row,
        compiler_params=_cparams("parallel"),
    )(dq, *tabs)


def _mla_latent_bwd(h, dq_n, dkv_n, dkr, dz, q_norm, kv_norm, tabs, *, tm, name):
    S = h.shape[0]

    def body(kv_ref, q_ref, dqn_ref, dkvn_ref, dkr_ref, dz_ref, qg_ref, kvg_ref, c_ref, s1_ref, s2_ref, dh_ref, dqg_ref, dkvg_ref):
        i = pl.program_id(0)
        dq_lat, dqg = _rms_bwd_math(q_ref[...], qg_ref[...], dqn_ref[...])
        dkv_lat, dkvg = _rms_bwd_math(kv_ref[...], kvg_ref[...], dkvn_ref[...])
        dkr_sum = dkr_ref[0]
        for hd in range(1, N_HEADS):
            dkr_sum = dkr_sum + dkr_ref[hd]
        dh_ref[:, :D_INNER] = dz_ref[...]
        dh_ref[:, D_INNER:D_INNER + KV_LORA] = dkv_lat.astype(BF16)
        dh_ref[:, D_INNER + KV_LORA:D_INNER + KV_LORA + Q_LORA] = dq_lat.astype(BF16)
        dh_ref[:, D_INNER + KV_LORA + Q_LORA:] = _unrope(dkr_sum, c_ref[...], s1_ref[...], s2_ref[...]).astype(BF16)

        @pl.when(i == 0)
        def _():
            dqg_ref[...] = dqg
            dkvg_ref[...] = dkvg

        @pl.when(i > 0)
        def _():
            dqg_ref[...] += dqg
            dkvg_ref[...] += dkvg

    tab = pl.BlockSpec((tm, LANES), lambda i: (i, 0))
    qvec = pl.BlockSpec((1, Q_LORA), lambda i: (0, 0))
    kvvec = pl.BlockSpec((1, KV_LORA), lambda i: (0, 0))
    return pl.pallas_call(
        body, name=name,
        out_shape=(jax.ShapeDtypeStruct((S, MLA_IN_PAD), BF16), jax.ShapeDtypeStruct((1, Q_LORA), F32),
                   jax.ShapeDtypeStruct((1, KV_LORA), F32)),
        grid=(S // tm,),
        in_specs=[pl.BlockSpec((tm, KV_LORA), lambda i: (i, KV_LAT_BLK)), pl.BlockSpec((tm, Q_LORA), lambda i: (i, Q_LAT_BLK)),
                  pl.BlockSpec((tm, Q_LORA), lambda i: (i, 0)), pl.BlockSpec((tm, KV_LORA), lambda i: (i, 0)),
                  pl.BlockSpec((N_HEADS, tm, LANES), lambda i: (0, i, 0)), pl.BlockSpec((tm, D_INNER), lambda i: (i, 0)),
                  qvec, kvvec, tab, tab, tab],
        out_specs=(pl.BlockSpec((tm, MLA_IN_PAD), lambda i: (i, 0)), qvec, kvvec),
        compiler_params=_cparams("arbitrary"),
    )(h, h, dq_n, dkv_n, dkr, dz, q_norm, kv_norm, *tabs)


def _adamw(w, g, m, v, *, name):
    R, C = w.shape
    tr = R
    for cand in (512, 256, 128, 64, 32, 16, 8):
        if R % cand == 0 and cand * C * 4 <= 2 * 1024 * 1024:
            tr = cand
            break

    def body(w_ref, g_ref, m_ref, v_ref, d_ref, nm_ref, nv_ref):
        gv = g_ref[...]
        m_new = ADAM_B1 * m_ref[...] + (1.0 - ADAM_B1) * gv
        v_new = ADAM_B2 * v_ref[...] + (1.0 - ADAM_B2) * (gv * gv)
        m_hat = m_new / (1.0 - ADAM_B1 ** ADAM_STEP)
        v_hat = v_new / (1.0 - ADAM_B2 ** ADAM_STEP)
        d_ref[...] = -ADAM_LR * (m_hat / (jnp.sqrt(v_hat) + ADAM_EPS) + ADAM_WD * w_ref[...])
        nm_ref[...] = m_new
        nv_ref[...] = v_new

    spec = pl.BlockSpec((tr, C), lambda i: (i, 0))
    sds = jax.ShapeDtypeStruct((R, C), F32)
    return pl.pallas_call(
        body, name=name, out_shape=(sds, sds, sds), grid=(R // tr,),
        in_specs=[spec] * 4, out_specs=(spec,) * 3,
        compiler_params=_cparams("parallel"),
    )(w, g, m, v)


HBM_SPEC = pl.BlockSpec(memory_space=pltpu.HBM)


def _place():
    return lax.axis_index("x"), lax.axis_index("y"), lax.axis_index("c")


def _other_chips(x, y):
    return [(1 - x, y), (x, 1 - y), (1 - x, 1 - y)]


def _remote(src, dst, send_sem, recv_sem, dev):
    return pltpu.make_async_remote_copy(src_ref=src, dst_ref=dst, send_sem=send_sem, recv_sem=recv_sem,
                                        device_id=dev, device_id_type=MESH)


def _all_gather_big(wp, *, name):
    R, C = wp.shape
    H = R // 2

    def body(w_ref, out_ref, send_sems, recv_sems):
        x, y, c = _place()
        sib = (x, y, 1 - c)
        chips = _other_chips(x, y)

        def blk(px, py, half):
            return out_ref.at[2 * px + py, pl.ds(pl.multiple_of(half * H, 16), H), :]

        my_half = w_ref.at[pl.ds(pl.multiple_of(c * H, 16), H), :]
        first = [_remote(my_half, blk(x, y, c), send_sems.at[r], recv_sems.at[r], (*chip, c)) for r, chip in enumerate(chips)]
        for cp in first:
            cp.start()
        passed = [_remote(blk(*chip, c), blk(*chip, c), send_sems.at[3 + r], recv_sems.at[3 + r], sib) for r, chip in enumerate(chips)]
        for r, chip in enumerate(chips):
            _remote(my_half, blk(*chip, c), send_sems.at[r], recv_sems.at[r], (*chip, c)).wait_recv()
            passed[r].start()
        for r, chip in enumerate(chips):
            _remote(my_half, blk(*chip, 1 - c), send_sems.at[3 + r], recv_sems.at[3 + r], sib).wait_recv()
        for cp in first + passed:
            cp.wait_send()

    return pl.pallas_call(
        body, name=name,
        out_shape=jax.ShapeDtypeStruct((N_CHIPS, R, C), wp.dtype),
        in_specs=[HBM_SPEC], out_specs=HBM_SPEC,
        scratch_shapes=[pltpu.SemaphoreType.DMA((6,)), pltpu.SemaphoreType.DMA((6,))],
    )(wp)


def _rs_sibling_swap(g, *, name):
    _, R, C = g.shape
    H = R // 2

    def body(g_ref, theirs_ref, send_sems, recv_sems):
        x, y, c = _place()
        sib = (x, y, 1 - c)
        copies = [_remote(g_ref.at[k, pl.ds(pl.multiple_of((1 - c) * H, 16), H), :], theirs_ref.at[k],
                          send_sems.at[k], recv_sems.at[k], sib) for k in range(N_CHIPS)]
        for cp in copies:
            cp.start()
        for cp in copies:
            cp.wait()

    return pl.pallas_call(
        body, name=name, out_shape=jax.ShapeDtypeStruct((N_CHIPS, H, C), g.dtype),
        in_specs=[HBM_SPEC], out_specs=HBM_SPEC,
        scratch_shapes=[pltpu.SemaphoreType.DMA((N_CHIPS,)), pltpu.SemaphoreType.DMA((N_CHIPS,))],
    )(g)


def _row_tile(h):
    best = 16
    for d in range(16, 1025, 16):
        if h % d == 0:
            best = d
    return best


def _add2_bf16(g, theirs, core, *, name):
    K, H, C = theirs.shape
    tr = _row_tile(H)
    nb = H // tr

    def body(c_ref, a_ref, b_ref, o_ref):
        o_ref[...] = (a_ref[...].astype(F32) + b_ref[...].astype(F32)).astype(o_ref.dtype)

    spec = pl.BlockSpec((None, tr, C), lambda k, i, c: (k, i, 0))
    return pl.pallas_call(
        body, name=name, out_shape=jax.ShapeDtypeStruct((K, H, C), theirs.dtype),
        grid_spec=pltpu.PrefetchScalarGridSpec(
            num_scalar_prefetch=1, grid=(K, nb),
            in_specs=[pl.BlockSpec((None, tr, C), lambda k, i, c: (k, c[0] * nb + i, 0)), spec], out_specs=spec),
        compiler_params=_cparams("parallel", "parallel"),
    )(core, g, theirs)


def _rs_chip_exchange(p, *, name):
    _, H, C = p.shape

    def body(p_ref, recv_ref, send_sems, recv_sems):
        x, y, c = _place()
        sends = [_remote(p_ref.at[2 * px + py], recv_ref.at[r], send_sems.at[r], recv_sems.at[r], (px, py, c))
                 for r, (px, py) in enumerate(_other_chips(x, y))]
        for cp in sends:
            cp.start()
        for cp in sends:
            cp.wait()

    return pl.pallas_call(
        body, name=name, out_shape=jax.ShapeDtypeStruct((3, H, C), p.dtype),
        in_specs=[HBM_SPEC], out_specs=HBM_SPEC,
        scratch_shapes=[pltpu.SemaphoreType.DMA((3,)), pltpu.SemaphoreType.DMA((3,))],
    )(p)


def _add4_f32(p, recv, chip_core, *, name):
    _, H, C = p.shape
    tr = _row_tile(H)
    nb = H // tr

    def body(s_ref, o_ref, r_ref, out_ref):
        out_ref[...] = ((o_ref[...].astype(F32) + r_ref[0].astype(F32)) + r_ref[1].astype(F32)) + r_ref[2].astype(F32)

    return pl.pallas_call(
        body, name=name, out_shape=jax.ShapeDtypeStruct((2 * H, C), F32),
        grid_spec=pltpu.PrefetchScalarGridSpec(
            num_scalar_prefetch=1, grid=(nb,),
            in_specs=[pl.BlockSpec((None, tr, C), lambda i, s: (s[0], i, 0)), pl.BlockSpec((3, tr, C), lambda i, s: (0, i, 0))],
            out_specs=pl.BlockSpec((tr, C), lambda i, s: (s[1] * nb + i, 0))),
        compiler_params=_cparams("parallel"),
    )(chip_core, p, recv)


def _rs_sibling_join(f, *, name):
    R, C = f.shape
    H = R // 2

    def body(f_ref, out_ref, send_sem, recv_sem):
        x, y, c = _place()
        sib = (x, y, 1 - c)
        mine = pl.ds(pl.multiple_of(c * H, 8), H)
        theirs = pl.ds(pl.multiple_of((1 - c) * H, 8), H)
        cp = _remote(f_ref.at[mine, :], out_ref.at[mine, :], send_sem, recv_sem, sib)
        cp.start()
        _remote(f_ref.at[mine, :], out_ref.at[theirs, :], send_sem, recv_sem, sib).wait_recv()
        cp.wait_send()

    return pl.pallas_call(
        body, name=name, out_shape=jax.ShapeDtypeStruct((R, C), f.dtype),
        in_specs=[HBM_SPEC], out_specs=HBM_SPEC, input_output_aliases={0: 0},
        scratch_shapes=[pltpu.SemaphoreType.DMA, pltpu.SemaphoreType.DMA],
    )(f)


def _small_exchange(vec, *, reduce, name):
    r, C = vec.shape

    def body(v_ref, out_ref, *rest):
        if reduce:
            all_ref, send_sems, recv_sems = rest
        else:
            all_ref = out_ref
            send_sems, recv_sems = rest
        x, y, c = _place()
        me = 4 * x + 2 * y + c
        all_ref[me] = v_ref[...]
        peers = []
        for rel in range(1, N_DEV):
            dx, dy, dc = (rel >> 2) & 1, (rel >> 1) & 1, rel & 1
            peers.append((1 - x if dx else x, 1 - y if dy else y, 1 - c if dc else c))
        sends = [_remote(v_ref, all_ref.at[me], send_sems.at[k], recv_sems.at[k], peer) for k, peer in enumerate(peers)]
        for cp in sends:
            cp.start()
        for k, (px, py, pc) in enumerate(peers):
            _remote(v_ref, all_ref.at[4 * px + 2 * py + pc], send_sems.at[k], recv_sems.at[k], (px, py, pc)).wait_recv()
        for cp in sends:
            cp.wait_send()
        if reduce:
            s = all_ref[0]
            for k in range(1, N_DEV):
                s = s + all_ref[k]
            out_ref[...] = s

    vm = pl.BlockSpec(memory_space=pltpu.VMEM)
    sems = [pltpu.SemaphoreType.DMA((N_DEV - 1,)), pltpu.SemaphoreType.DMA((N_DEV - 1,))]
    if reduce:
        return pl.pallas_call(
            body, name=name, out_shape=jax.ShapeDtypeStruct((r, C), F32), in_specs=[vm], out_specs=vm,
            scratch_shapes=[pltpu.VMEM((N_DEV, r, C), F32)] + sems,
        )(vec)
    return pl.pallas_call(
        body, name=name, out_shape=jax.ShapeDtypeStruct((N_DEV, r, C), F32), in_specs=[vm], out_specs=vm,
        scratch_shapes=sems,
    )(vec)


BIG = ("pool_w_in", "pool_w_grp", "pool_w_out", "conv_w_in", "conv_w_out", "mla_w_in", "mla_w_q_up", "mla_w_kv_up", "mla_w_out")
BIG_SHARD_AXIS = {"pool_w_in": 2, "pool_w_grp": 2, "pool_w_out": 1, "conv_w_in": 2, "conv_w_out": 1,
                  "mla_w_in": 2, "mla_w_q_up": 2, "mla_w_kv_up": 2, "mla_w_out": 1}
PACK_ROW_ALIGN = 32


def _pack_rows(parts):
    rows = [p.reshape(-1, PACK_COLS) for p in parts]
    n = sum(r.shape[0] for r in rows)
    pad = (-n) % PACK_ROW_ALIGN
    if pad:
        rows.append(jnp.zeros((pad, PACK_COLS), rows[0].dtype))
    return jnp.concatenate(rows, axis=0)


def _unpack_rows(buf, shapes):
    out, r0 = [], 0
    for shp in shapes:
        n = 1
        for d in shp:
            n *= d
        nr = n // PACK_COLS
        out.append(buf[r0:r0 + nr].reshape(shp))
        r0 += nr
    return out


def _join_shards(gathered, shard_shapes, axes):
    per_chip = [_unpack_rows(gathered[k], shard_shapes) for k in range(N_CHIPS)]
    return [jnp.concatenate([per_chip[k][i] for k in range(N_CHIPS)], axis=ax) for i, ax in enumerate(axes)]


def _split_shards(full, axis):
    return jnp.split(full, N_CHIPS, axis=axis)


SMALL = ("pool_norm", "pool_scale", "conv_norm", "conv_w", "mla_norm", "mla_q_norm", "mla_kv_norm", "final_norm")
SMALL_SHARDED = {"pool_norm": True, "pool_scale": True, "conv_norm": False, "conv_w": True, "mla_norm": True,
                 "mla_q_norm": True, "mla_kv_norm": True, "final_norm": False}


def _pack_small(parts):
    flat, offs, n = [], [], 0
    for p in parts:
        v = p.reshape(-1)
        pad = (-v.shape[0]) % LANES
        flat.append(jnp.pad(v, (0, pad)))
        offs.append(n)
        n += v.shape[0] + pad
    pad = (-n) % (8 * LANES)
    if pad:
        flat.append(jnp.zeros((pad,), F32))
    return jnp.concatenate(flat).reshape(-1, LANES), offs


def _unpack_small(buf, offs, shapes):
    v = buf.reshape(-1)
    out = []
    for o, shp in zip(offs, shapes):
        n = 1
        for d in shp:
            n *= d
        out.append(v[o:o + n].reshape(shp))
    return out


def _rope_tables(positions):
    inv_freq = ROPE_BASE ** (-jnp.arange(0, QK_ROPE, 2, dtype=F32) / QK_ROPE)
    ang = positions.astype(F32).reshape(-1, 1) * inv_freq
    cos, sin = jnp.cos(ang), jnp.sin(ang)
    z32 = jnp.zeros_like(cos)
    z64 = jnp.concatenate([z32, z32], axis=1)
    return (jnp.concatenate([cos, cos, z64], axis=1), jnp.concatenate([-sin, z32, z64], axis=1),
            jnp.concatenate([z32, sin, z64], axis=1))


def _mla_in_to_padded(w):
    q, kv, kr, z = w[:, :Q_LORA], w[:, Q_LORA:Q_LORA + KV_LORA], w[:, Q_LORA + KV_LORA:Q_LORA + KV_LORA + QK_ROPE], w[:, Q_LORA + KV_LORA + QK_ROPE:]
    return jnp.concatenate([z, kv, q, kr, jnp.zeros((w.shape[0], MLA_IN_PAD - MLA_IN), w.dtype)], axis=1)


def _mla_in_from_padded(w):
    z, kv, q, kr = w[:, :D_INNER], w[:, D_INNER:D_INNER + KV_LORA], w[:, D_INNER + KV_LORA:D_INNER + KV_LORA + Q_LORA], w[:, D_INNER + KV_LORA + Q_LORA:D_INNER + KV_LORA + Q_LORA + QK_ROPE]
    return jnp.concatenate([q, kv, kr, z], axis=1)


def _q_up_to_padded(w):
    k = w.shape[0]
    return jnp.pad(w.reshape(k, N_HEADS, QK_NOPE + QK_ROPE), ((0, 0), (0, 0), (0, HEAD_PAD - QK_NOPE - QK_ROPE))).reshape(k, N_HEADS * HEAD_PAD)


def _q_up_from_padded(w):
    k = w.shape[0]
    return w.reshape(k, N_HEADS, HEAD_PAD)[:, :, :QK_NOPE + QK_ROPE].reshape(k, N_HEADS * (QK_NOPE + QK_ROPE))


def _local_step(x, positions, target, wb, ws):
    S = x.shape[0]
    tm = min(512, S)
    te = min(256, S)
    tq = min(512, S)
    tabs = _rope_tables(positions)
    gb, gs = {}, {}

    def mm_in(xn, w, name):
        n = w.shape[1]
        return _mm(xn, w, tm=min(1024, S), tn=_pick(n, 1536 if n == MLA_IN_PAD else 1024), tk=D_MODEL, name=name)

    def mm_out(y, w, res, name):
        return _mm(y, w, residual=res, tm=tm, tn=D_MODEL, tk=D_INNER, name=name)

    def mm_dx(dy, w, name):
        k, n = w.shape
        return _mm(dy, w, trans_b=True, tm=min(1024, S), tn=_pick(k, 1024), tk=_pick(n, 1408), name=name)

    def mm_dw(a, b, name):
        ka, nb = a.shape[1], b.shape[1]
        return _mm(a, b, trans_a=True, out_dtype=BF16, tm=_pick(ka, 1024), tn=_pick(nb, 1408), tk=tm, name=name)

    def pool_layer_fwd(xin, j, tag):
        xn = _rms_fwd(xin, ws["pool_norm"][j:j + 1], tm=tm, name=f"{tag}_norm")
        h = mm_in(xn, wb["pool_w_in"][j], f"{tag}_in")
        y = _pool_fwd(h, wb["pool_w_grp"][j], ws["pool_scale"][j:j + 1], tm=te, name=f"{tag}_mix")
        xo = mm_out(y, wb["pool_w_out"][j], xin, f"{tag}_out")
        return xo, (xin, xn, h, y)

    def pool_layer_bwd(dx, dxb, saved, j, tag):
        xin, xn, h, y = saved
        dy = mm_dx(dxb, wb["pool_w_out"][j], f"{tag}_dy")
        dwo = mm_dw(y, dxb, f"{tag}_dwo")
        pooled, dmixed, dpooled, dz, dsc = _pool_bwd1(h, dy, wb["pool_w_grp"][j], ws["pool_scale"][j:j + 1], tm=te, name=f"{tag}_bmix")
        dwg = _grouped_tn(pooled, dmixed, tk=tm, name=f"{tag}_dwg")
        dh = _pool_bwd2(dpooled, dz, tm=te, name=f"{tag}_bshift")
        dxn = mm_dx(dh, wb["pool_w_in"][j], f"{tag}_dxn")
        dwi = mm_dw(xn, dh, f"{tag}_dwi")
        dxo, dxob, dg = _rms_bwd(xin, ws["pool_norm"][j:j + 1], dxn, dx, tm=tm, name=f"{tag}_bnorm")
        return dxo, dxob, dict(pool_w_in=dwi, pool_w_grp=dwg, pool_w_out=dwo), dict(pool_norm=dg, pool_scale=dsc)

    x1, sv0 = pool_layer_fwd(x, 0, "p0")

    xn1 = _rms_fwd(x1, ws["conv_norm"][0:1], tm=tm, name="cv_norm")
    h1 = mm_in(xn1, wb["conv_w_in"][0], "cv_in")
    cw = jnp.pad(ws["conv_w"][0], ((0, 5), (0, 0)))
    y1 = _conv_fwd(h1, cw, tm=te, name="cv_mix")
    x2 = mm_out(y1, wb["conv_w_out"][0], x1, "cv_out")

    w_mi = _mla_in_to_padded(wb["mla_w_in"][0])
    w_q = _q_up_to_padded(wb["mla_w_q_up"][0])
    w_kv = wb["mla_w_kv_up"][0]
    qg, kvg = ws["mla_q_norm"][0:1], ws["mla_kv_norm"][0:1]
    xn2 = _rms_fwd(x2, ws["mla_norm"][0:1], tm=tm, name="ml_norm")
    h2 = mm_in(xn2, w_mi, "ml_in")
    q_n, kv_n, krr = _mla_latent_fwd(h2, qg, kvg, tabs, tm=tm, name="ml_lat")
    q_full = _mla_q_up(q_n, w_q, tabs, tm=tm, name="ml_qup")
    k_full, v = _mla_kv_up(kv_n, w_kv, krr, tm=tm, name="ml_kvup")
    o, y2, lse = _flash_fwd(q_full, k_full, v, h2, tq=tq, name="ml_attn")
    x3 = mm_out(y2, wb["mla_w_out"][0], x2, "ml_out")

    x4, sv3 = pool_layer_fwd(x3, 1, "p1")

    loss_part, dx, dxb, dgf = _final_loss(x4, ws["final_norm"].reshape(1, -1), target, tm=tm, name="final")
    gs["final_norm"] = dgf.reshape(-1)

    dx, dxb, gb3, gs3 = pool_layer_bwd(dx, dxb, sv3, 1, "p1")

    dy = mm_dx(dxb, wb["mla_w_out"][0], "ml_dy")
    gb["mla_w_out"] = mm_dw(y2, dxb, "ml_dwo")[None]
    do, dz, delta = _mla_gate_bwd(dy, o, h2, tm=te, name="ml_bgate")
    nq = S // tq
    dq, dkv, dkr = _flash_bwd(q_full, k_full, v, do, lse.reshape(N_HEADS, nq, 1, tq), delta.reshape(N_HEADS, nq, 1, tq), tq=tq, name="ml_battn")
    dq_pre = _mla_unrope_q(dq, tabs, tm=te, name="ml_bqrope")
    dq_n = mm_dx(dq_pre, w_q, "ml_dqn")
    gb["mla_w_q_up"] = _q_up_from_padded(mm_dw(q_n, dq_pre, "ml_dwq"))[None]
    dkv_n = mm_dx(dkv, w_kv, "ml_dkvn")
    gb["mla_w_kv_up"] = mm_dw(kv_n, dkv, "ml_dwkv")[None]
    dh2, dqg, dkvg = _mla_latent_bwd(h2, dq_n, dkv_n, dkr, dz, qg, kvg, tabs, tm=te, name="ml_blat")
    dxn2 = mm_dx(dh2, w_mi, "ml_dxn")
    gb["mla_w_in"] = _mla_in_from_padded(mm_dw(xn2, dh2, "ml_dwi"))[None]
    dx, dxb, dg2 = _rms_bwd(x2, ws["mla_norm"][0:1], dxn2, dx, tm=tm, name="ml_bnorm")
    gs["mla_norm"], gs["mla_q_norm"], gs["mla_kv_norm"] = dg2, dqg, dkvg

    dy = mm_dx(dxb, wb["conv_w_out"][0], "cv_dy")
    gb["conv_w_out"] = mm_dw(y1, dxb, "cv_dwo")[None]
    dh1, dcw = _conv_bwd(h1, dy, cw, tm=te, name="cv_bmix")
    dxn1 = mm_dx(dh1, wb["conv_w_in"][0], "cv_dxn")
    gb["conv_w_in"] = mm_dw(xn1, dh1, "cv_dwi")[None]
    dx, dxb, dg1 = _rms_bwd(x1, ws["conv_norm"][0:1], dxn1, dx, tm=tm, name="cv_bnorm")
    gs["conv_norm"], gs["conv_w"] = dg1, dcw[None, :3]

    dx, dxb, gb0, gs0 = pool_layer_bwd(dx, dxb, sv0, 0, "p0")
    for k in ("pool_w_in", "pool_w_grp", "pool_w_out"):
        gb[k] = jnp.stack([gb0[k], gb3[k]])
    for k in ("pool_norm", "pool_scale"):
        gs[k] = jnp.concatenate([gs0[k], gs3[k]], axis=0)
    return loss_part, dx, gb, gs


def kernel(x, positions, pool_norm, pool_w_in, pool_w_grp, pool_scale, pool_w_out, conv_norm, conv_w_in, conv_w, conv_w_out, mla_norm, mla_w_in, mla_q_norm, mla_w_q_up, mla_kv_norm, mla_w_kv_up, mla_w_out, final_norm, loss_target, m_pool_norm, m_pool_w_in, m_pool_w_grp, m_pool_scale, m_pool_w_out, m_conv_norm, m_conv_w_in, m_conv_w, m_conv_w_out, m_mla_norm, m_mla_w_in, m_mla_q_norm, m_mla_w_q_up, m_mla_kv_norm, m_mla_w_kv_up, m_mla_w_out, m_final_norm, v_pool_norm, v_pool_w_in, v_pool_w_grp, v_pool_scale, v_pool_w_out, v_conv_norm, v_conv_w_in, v_conv_w, v_conv_w_out, v_mla_norm, v_mla_w_in, v_mla_q_norm, v_mla_w_q_up, v_mla_kv_norm, v_mla_w_kv_up, v_mla_w_out, v_final_norm):
    names = ("pool_norm", "pool_w_in", "pool_w_grp", "pool_scale", "pool_w_out", "conv_norm", "conv_w_in", "conv_w", "conv_w_out",
             "mla_norm", "mla_w_in", "mla_q_norm", "mla_w_q_up", "mla_kv_norm", "mla_w_kv_up", "mla_w_out", "final_norm")
    w = dict(zip(names, (pool_norm, pool_w_in, pool_w_grp, pool_scale, pool_w_out, conv_norm, conv_w_in, conv_w, conv_w_out,
                         mla_norm, mla_w_in, mla_q_norm, mla_w_q_up, mla_kv_norm, mla_w_kv_up, mla_w_out, final_norm)))
    m = dict(zip(names, (m_pool_norm, m_pool_w_in, m_pool_w_grp, m_pool_scale, m_pool_w_out, m_conv_norm, m_conv_w_in, m_conv_w, m_conv_w_out,
                         m_mla_norm, m_mla_w_in, m_mla_q_norm, m_mla_w_q_up, m_mla_kv_norm, m_mla_w_kv_up, m_mla_w_out, m_final_norm)))
    v = dict(zip(names, (v_pool_norm, v_pool_w_in, v_pool_w_grp, v_pool_scale, v_pool_w_out, v_conv_norm, v_conv_w_in, v_conv_w, v_conv_w_out,
                         v_mla_norm, v_mla_w_in, v_mla_q_norm, v_mla_w_q_up, v_mla_kv_norm, v_mla_w_kv_up, v_mla_w_out, v_final_norm)))
    chip = 2 * lax.axis_index("x") + lax.axis_index("y")
    core = lax.axis_index("c")

    big_shapes = [w[n].shape for n in BIG]
    big_axes = [BIG_SHARD_AXIS[n] for n in BIG]
    own_rows = _pack_rows([w[n].astype(BF16) for n in BIG])
    gathered = lax.dynamic_update_slice(_all_gather_big(own_rows, name="ag_big"), own_rows[None], (chip, 0, 0))
    wb = dict(zip(BIG, _join_shards(gathered, big_shapes, big_axes)))

    small_shapes = [w[n].shape for n in SMALL]
    small_pack, small_offs = _pack_small([w[n] for n in SMALL])
    small_all = _small_exchange(small_pack, reduce=False, name="ag_small")
    per_chip = [_unpack_small(small_all[4 * (k // 2) + 2 * (k % 2)], small_offs, small_shapes) for k in range(N_CHIPS)]
    ws = {}
    for i, n in enumerate(SMALL):
        ws[n] = jnp.concatenate([per_chip[k][i] for k in range(N_CHIPS)], axis=-1) if SMALL_SHARDED[n] else w[n]

    loss_part, grad_x, gb, gs = _local_step(x[0], positions, loss_target[0], wb, ws)
    loss = lax.psum(loss_part[0, 0], ("x", "y", "c"))

    packed = jnp.stack([_pack_rows([_split_shards(gb[n], BIG_SHARD_AXIS[n])[k] for n in BIG]) for k in range(N_CHIPS)])
    theirs = _rs_sibling_swap(packed, name="rs_swap")
    chip_sum = _add2_bf16(packed, theirs, core.astype(jnp.int32).reshape(1), name="rs_add2")
    recv = _rs_chip_exchange(chip_sum, name="rs_chips")
    half_sum = _add4_f32(chip_sum, recv, jnp.stack([chip, core]).astype(jnp.int32), name="rs_add4")
    g_rows = _rs_sibling_join(half_sum, name="rs_join")
    g = dict(zip(BIG, _unpack_rows(g_rows, big_shapes)))

    full_small_shapes = [gs[n].shape for n in SMALL]
    gs_pack, gs_offs = _pack_small([gs[n] for n in SMALL])
    gs_sum = _unpack_small(_small_exchange(gs_pack, reduce=True, name="ar_small"), gs_offs, full_small_shapes)
    for i, n in enumerate(SMALL):
        if SMALL_SHARDED[n]:
            width = w[n].shape[-1]
            g[n] = lax.dynamic_slice_in_dim(gs_sum[i], chip * width, width, axis=gs_sum[i].ndim - 1)
        else:
            g[n] = gs_sum[i].reshape(w[n].shape)

    delta, new_m, new_v = {}, {}, {}
    for n in BIG:
        shp = w[n].shape
        two_d = lambda a: a.reshape(-1, shp[-1])
        d_, m_, v_ = _adamw(two_d(w[n]), two_d(g[n]), two_d(m[n]), two_d(v[n]), name=f"adamw_{n}")
        delta[n], new_m[n], new_v[n] = d_.reshape(shp), m_.reshape(shp), v_.reshape(shp)
    sw, s_offs = _pack_small([w[n] for n in SMALL])
    sg, _ = _pack_small([g[n] for n in SMALL])
    sm, _ = _pack_small([m[n] for n in SMALL])
    sv, _ = _pack_small([v[n] for n in SMALL])
    sd, snm, snv = _adamw(sw, sg, sm, sv, name="adamw_small")
    for dst, buf in ((delta, sd), (new_m, snm), (new_v, snv)):
        for n, a in zip(SMALL, _unpack_small(buf, s_offs, small_shapes)):
            dst[n] = a

    return (loss, grad_x[None], *[g[n] for n in names], *[delta[n] for n in names],
            *[new_m[n] for n in names], *[new_v[n] for n in names])
```

```python
import functools

import jax
import jax.numpy as jnp
from jax import lax
from jax.experimental import pallas as pl
from jax.experimental.pallas import tpu as pltpu

F32 = jnp.float32
BF16 = jnp.bfloat16

D_MODEL = 1024
D_INNER = 2048
POOL_WINDOWS = (2, 4, 8, 16)
POOL_GROUP = 512
N_HEADS = 16
QK_NOPE = 128
QK_ROPE = 64
V_DIM = 128
HEAD_PAD = 256
Q_LORA = 384
KV_LORA = 256
MLA_IN = Q_LORA + KV_LORA + QK_ROPE + D_INNER
MLA_IN_PAD = 2816
ATTN_SCALE = (QK_NOPE + QK_ROPE) ** -0.5
ROPE_BASE = 10000.0
NORM_EPS = 1e-6
HALO = 16

ADAM_LR = 0.001
ADAM_B1 = 0.9
ADAM_B2 = 0.999
ADAM_EPS = 1e-08
ADAM_WD = 0.01
ADAM_STEP = 10

N_CHIPS = 4
N_DEV = 8
LANES = 128
PACK_COLS = 1024
V7X_VMEM_LIMIT = 56 * 1024 * 1024
MESH = pl.DeviceIdType.MESH


def _cparams(*sem):
    return pltpu.CompilerParams(dimension_semantics=sem, vmem_limit_bytes=V7X_VMEM_LIMIT)


def _pick(n, cap):
    best = None
    for d in range(LANES, min(n, cap) + 1, LANES):
        if n % d == 0:
            best = d
    assert best is not None, (n, cap)
    return best


def _sigmoid(z):
    return 1.0 / (1.0 + jnp.exp(-z))


def _mm(a, b, *, trans_a=False, trans_b=False, out_dtype=F32, residual=None, tm, tn, tk, name):
    if trans_a:
        K, M = a.shape
    else:
        M, K = a.shape
    if trans_b:
        N, K2 = b.shape
    else:
        K2, N = b.shape
    assert K == K2 and M % tm == 0 and N % tn == 0 and K % tk == 0, (name, a.shape, b.shape, tm, tn, tk)
    nk = K // tk
    dn = (((0 if trans_a else 1,), (1 if trans_b else 0,)), ((), ()))
    has_res = residual is not None

    def body(*refs):
        if has_res:
            a_ref, b_ref, r_ref, o_ref = refs[:4]
            rest = refs[4:]
        else:
            a_ref, b_ref, o_ref = refs[:3]
            r_ref = None
            rest = refs[3:]
        part = lax.dot_general(a_ref[...], b_ref[...], dn, preferred_element_type=F32)

        def finish(acc):
            if has_res:
                acc = acc + r_ref[...]
            o_ref[...] = acc.astype(o_ref.dtype)

        if nk == 1:
            finish(part)
        else:
            acc_ref = rest[0]
            k = pl.program_id(2)

            @pl.when(k == 0)
            def _():
                acc_ref[...] = part

            @pl.when(k > 0)
            def _():
                acc_ref[...] += part

            @pl.when(k == nk - 1)
            def _():
                finish(acc_ref[...])

    a_spec = pl.BlockSpec((tk, tm), lambda i, j, k: (k, i)) if trans_a else pl.BlockSpec((tm, tk), lambda i, j, k: (i, k))
    b_spec = pl.BlockSpec((tn, tk), lambda i, j, k: (j, k)) if trans_b else pl.BlockSpec((tk, tn), lambda i, j, k: (k, j))
    o_spec = pl.BlockSpec((tm, tn), lambda i, j, k: (i, j))
    in_specs = [a_spec, b_spec] + ([o_spec] if has_res else [])
    args = (a, b) + ((residual,) if has_res else ())
    return pl.pallas_call(
        body, name=name,
        out_shape=jax.ShapeDtypeStruct((M, N), out_dtype),
        grid=(M // tm, N // tn, nk),
        in_specs=in_specs, out_specs=o_spec,
        scratch_shapes=[pltpu.VMEM((tm, tn), F32)] if nk > 1 else [],
        compiler_params=_cparams("parallel", "parallel", "arbitrary"),
    )(*args)


def _grouped_tn(a, b, *, tk, name):
    S = a.shape[0]
    G = POOL_GROUP
    nk = S // tk

    def body(a_ref, b_ref, o_ref, acc_ref):
        k = pl.program_id(1)
        part = lax.dot_general(a_ref[...], b_ref[...], (((0,), (0,)), ((), ())), preferred_element_type=F32)

        @pl.when(k == 0)
        def _():
            acc_ref[...] = part

        @pl.when(k > 0)
        def _():
            acc_ref[...] += part

        @pl.when(k == nk - 1)
        def _():
            o_ref[...] = acc_ref[...].astype(o_ref.dtype)

    return pl.pallas_call(
        body, name=name,
        out_shape=jax.ShapeDtypeStruct((len(POOL_WINDOWS), G, G), BF16),
        grid=(len(POOL_WINDOWS), nk),
        in_specs=[pl.BlockSpec((tk, G), lambda g, k: (k, g)), pl.BlockSpec((tk, G), lambda g, k: (k, g))],
        out_specs=pl.BlockSpec((None, G, G), lambda g, k: (g, 0, 0)),
        scratch_shapes=[pltpu.VMEM((G, G), F32)],
        compiler_params=_cparams("parallel", "arbitrary"),
    )(a, b)


def _rms_fwd(x, g, *, tm, name):
    S, D = x.shape

    def body(x_ref, g_ref, o_ref):
        xv = x_ref[...]
        rstd = lax.rsqrt(jnp.mean(xv * xv, axis=-1, keepdims=True) + NORM_EPS)
        o_ref[...] = (xv * rstd * g_ref[...]).astype(o_ref.dtype)

    return pl.pallas_call(
        body, name=name,
        out_shape=jax.ShapeDtypeStruct((S, D), BF16),
        grid=(S // tm,),
        in_specs=[pl.BlockSpec((tm, D), lambda i: (i, 0)), pl.BlockSpec((1, D), lambda i: (0, 0))],
        out_specs=pl.BlockSpec((tm, D), lambda i: (i, 0)),
        compiler_params=_cparams("parallel"),
    )(x, g)


def _rms_bwd_math(xv, gv, dxn):
    rstd = lax.rsqrt(jnp.mean(xv * xv, axis=-1, keepdims=True) + NORM_EPS)
    xh = xv * rstd
    dg = jnp.sum(dxn * xh, axis=0, keepdims=True)
    dxh = dxn * gv
    dx = rstd * (dxh - xh * jnp.mean(dxh * xh, axis=-1, keepdims=True))
    return dx, dg


def _rms_bwd(x, g, dxn, dres, *, tm, name):
    S, D = x.shape

    def body(x_ref, g_ref, dxn_ref, dres_ref, dx_ref, dxb_ref, dg_ref):
        dx, dg = _rms_bwd_math(x_ref[...], g_ref[...], dxn_ref[...])
        dx = dx + dres_ref[...]
        dx_ref[...] = dx
        dxb_ref[...] = dx.astype(BF16)

        @pl.when(pl.program_id(0) == 0)
        def _():
            dg_ref[...] = dg

        @pl.when(pl.program_id(0) > 0)
        def _():
            dg_ref[...] += dg

    row = pl.BlockSpec((tm, D), lambda i: (i, 0))
    vec = pl.BlockSpec((1, D), lambda i: (0, 0))
    return pl.pallas_call(
        body, name=name,
        out_shape=(jax.ShapeDtypeStruct((S, D), F32), jax.ShapeDtypeStruct((S, D), BF16), jax.ShapeDtypeStruct((1, D), F32)),
        grid=(S // tm,),
        in_specs=[row, vec, row, row],
        out_specs=(row, row, vec),
        compiler_params=_cparams("arbitrary"),
    )(x, g, dxn, dres)


def _final_loss(x, g, target, *, tm, name):
    S, D = x.shape

    def body(x_ref, g_ref, t_ref, loss_ref, dx_ref, dxb_ref, dg_ref):
        xv = x_ref[...]
        gv = g_ref[...]
        rstd = lax.rsqrt(jnp.mean(xv * xv, axis=-1, keepdims=True) + NORM_EPS)
        xh = xv * rstd
        err = xh * gv - t_ref[...]
        part = 0.5 * jnp.sum(jnp.mean(err * err, axis=-1, keepdims=True), axis=0, keepdims=True)
        dy = err * (1.0 / D)
        dg = jnp.sum(dy * xh, axis=0, keepdims=True)
        dxh = dy * gv
        dx = rstd * (dxh - xh * jnp.mean(dxh * xh, axis=-1, keepdims=True))
        dx_ref[...] = dx
        dxb_ref[...] = dx.astype(BF16)
        lossb = jnp.broadcast_to(part, loss_ref.shape)

        @pl.when(pl.program_id(0) == 0)
        def _():
            dg_ref[...] = dg
            loss_ref[...] = lossb

        @pl.when(pl.program_id(0) > 0)
        def _():
            dg_ref[...] += dg
            loss_ref[...] += lossb

    row = pl.BlockSpec((tm, D), lambda i: (i, 0))
    vec = pl.BlockSpec((1, D), lambda i: (0, 0))
    lspec = pl.BlockSpec((8, LANES), lambda i: (0, 0))
    return pl.pallas_call(
        body, name=name,
        out_shape=(jax.ShapeDtypeStruct((8, LANES), F32), jax.ShapeDtypeStruct((S, D), F32),
                   jax.ShapeDtypeStruct((S, D), BF16), jax.ShapeDtypeStruct((1, D), F32)),
        grid=(S // tm,),
        in_specs=[row, vec, row],
        out_specs=(lspec, row, row, vec),
        compiler_params=_cparams("arbitrary"),
    )(x, g, target)


def _prev_halo_spec(tm, width, col):
    r = tm // HALO
    return pl.BlockSpec((HALO, width), lambda i: (jnp.maximum(i * r - 1, 0), col))


def _next_halo_spec(tm, width, col, S):
    r = tm // HALO
    last = S // HALO - 1
    return pl.BlockSpec((HALO, width), lambda i: (jnp.minimum((i + 1) * r, last), col))


def _shift_down(ext, k):
    return pltpu.roll(ext, k, 0)[HALO:, :]


def _shift_up(ext, k, tm):
    n = ext.shape[0]
    return pltpu.roll(ext, n - k, 0)[:tm, :]


def _pool_window_sum(ext, w):
    s = ext
    k = 1
    while k < w:
        s = s + pltpu.roll(s, k, 0)
        k *= 2
    return s[HALO:, :]


def _pooled_group(u_ref, halo, g, w, t_idx):
    cs = slice(g * POOL_GROUP, (g + 1) * POOL_GROUP)
    u = u_ref[:, cs]
    ext = jnp.concatenate([halo[:, cs], u], axis=0)
    inv = 1.0 / jnp.minimum(t_idx + 1, w).astype(F32)
    return _pool_window_sum(ext, w) * inv - u


def _pool_fwd(h, w_grp, scale, *, tm, name):
    S = h.shape[0]
    E = D_INNER

    def body(u_ref, uh_ref, z_ref, wg_ref, sc_ref, y_ref):
        i = pl.program_id(0)
        halo = jnp.where(i > 0, uh_ref[...], 0.0)
        t_idx = i * tm + lax.broadcasted_iota(jnp.int32, (tm, 1), 0)
        for g, w in enumerate(POOL_WINDOWS):
            cs = slice(g * POOL_GROUP, (g + 1) * POOL_GROUP)
            pooled = _pooled_group(u_ref, halo, g, w, t_idx)
            mixed = jnp.dot(pooled.astype(BF16), wg_ref[g], preferred_element_type=F32)
            z = z_ref[:, cs]
            y_ref[:, cs] = (mixed * sc_ref[:, cs] * (z * _sigmoid(z))).astype(BF16)

    return pl.pallas_call(
        body, name=name,
        out_shape=jax.ShapeDtypeStruct((S, E), BF16),
        grid=(S // tm,),
        in_specs=[pl.BlockSpec((tm, E), lambda i: (i, 0)), _prev_halo_spec(tm, E, 0),
                  pl.BlockSpec((tm, E), lambda i: (i, 1)),
                  pl.BlockSpec((len(POOL_WINDOWS), POOL_GROUP, POOL_GROUP), lambda i: (0, 0, 0)),
                  pl.BlockSpec((1, E), lambda i: (0, 0))],
        out_specs=pl.BlockSpec((tm, E), lambda i: (i, 0)),
        compiler_params=_cparams("parallel"),
    )(h, h, h, w_grp, scale)


def _pool_bwd1(h, dy, w_grp, scale, *, tm, name):
    S = h.shape[0]
    E = D_INNER

    def body(u_ref, uh_ref, z_ref, dy_ref, wg_ref, sc_ref, pooled_ref, dmixed_ref, dpooled_ref, dz_ref, dsc_ref):
        i = pl.program_id(0)
        halo = jnp.where(i > 0, uh_ref[...], 0.0)
        t_idx = i * tm + lax.broadcasted_iota(jnp.int32, (tm, 1), 0)
        for g, w in enumerate(POOL_WINDOWS):
            cs = slice(g * POOL_GROUP, (g + 1) * POOL_GROUP)
            pooled = _pooled_group(u_ref, halo, g, w, t_idx).astype(BF16)
            wg = wg_ref[g]
            mixed = jnp.dot(pooled, wg, preferred_element_type=F32)
            z = z_ref[:, cs]
            sg = _sigmoid(z)
            dyv = dy_ref[:, cs]
            sc = sc_ref[:, cs]
            dms = dyv * (z * sg)
            dz = dyv * (mixed * sc) * (sg * (1.0 + z * (1.0 - sg)))
            dsc = jnp.sum(dms * mixed, axis=0, keepdims=True)
            dmixed = (dms * sc).astype(BF16)
            dpooled = lax.dot_general(dmixed, wg, (((1,), (1,)), ((), ())), preferred_element_type=F32)
            pooled_ref[:, cs] = pooled
            dmixed_ref[:, cs] = dmixed
            dpooled_ref[:, cs] = dpooled
            dz_ref[:, cs] = dz.astype(BF16)

            @pl.when(i == 0)
            def _():
                dsc_ref[:, cs] = dsc

            @pl.when(i > 0)
            def _():
                dsc_ref[:, cs] += dsc

    row = pl.BlockSpec((tm, E), lambda i: (i, 0))
    vec = pl.BlockSpec((1, E), lambda i: (0, 0))
    return pl.pallas_call(
        body, name=name,
        out_shape=(jax.ShapeDtypeStruct((S, E), BF16), jax.ShapeDtypeStruct((S, E), BF16),
                   jax.ShapeDtypeStruct((S, E), F32), jax.ShapeDtypeStruct((S, E), BF16),
                   jax.ShapeDtypeStruct((1, E), F32)),
        grid=(S // tm,),
        in_specs=[row, _prev_halo_spec(tm, E, 0), pl.BlockSpec((tm, E), lambda i: (i, 1)), row,
                  pl.BlockSpec((len(POOL_WINDOWS), POOL_GROUP, POOL_GROUP), lambda i: (0, 0, 0)), vec],
        out_specs=(row, row, row, row, vec),
        compiler_params=_cparams("arbitrary"),
    )(h, h, h, dy, w_grp, scale)


def _pool_bwd2(dpooled, dz, *, tm, name):
    S = dpooled.shape[0]
    E = D_INNER
    nt = S // tm

    def body(dp_ref, dpn_ref, dz_ref, dh_ref):
        i = pl.program_id(0)
        nxt = jnp.where(i < nt - 1, dpn_ref[...], 0.0)
        t_ext = i * tm + lax.broadcasted_iota(jnp.int32, (tm + HALO, 1), 0)
        for g, w in enumerate(POOL_WINDOWS):
            cs = slice(g * POOL_GROUP, (g + 1) * POOL_GROUP)
            dp = dp_ref[:, cs]
            inv = 1.0 / jnp.minimum(t_ext + 1, w).astype(F32)
            s = jnp.concatenate([dp, nxt[:, cs]], axis=0) * inv
            n = tm + HALO
            k = 1
            while k < w:
                s = s + pltpu.roll(s, n - k, 0)
                k *= 2
            dh_ref[:, cs] = (s[:tm, :] - dp).astype(BF16)
        dh_ref[:, E:] = dz_ref[...]

    return pl.pallas_call(
        body, name=name,
        out_shape=jax.ShapeDtypeStruct((S, 2 * E), BF16),
        grid=(nt,),
        in_specs=[pl.BlockSpec((tm, E), lambda i: (i, 0)), _next_halo_spec(tm, E, 0, S),
                  pl.BlockSpec((tm, E), lambda i: (i, 0))],
        out_specs=pl.BlockSpec((tm, 2 * E), lambda i: (i, 0)),
        compiler_params=_cparams("parallel"),
    )(dpooled, dpooled, dz)


CONV_CHUNK = 512


def _conv_fwd(h, cw, *, tm, name):
    S = h.shape[0]
    E = D_INNER

    def body(b_ref, c_ref, hh_ref, z_ref, ch_ref, hhh_ref, w_ref, y_ref):
        i = pl.program_id(0)
        for j in range(E // CONV_CHUNK):
            cs = slice(j * CONV_CHUNK, (j + 1) * CONV_CHUNK)
            p = c_ref[:, cs] * hh_ref[:, cs]
            ph = jnp.where(i > 0, ch_ref[:, cs] * hhh_ref[:, cs], 0.0)
            ext = jnp.concatenate([ph, p], axis=0)
            conv = w_ref[2:3, cs] * p + w_ref[1:2, cs] * _shift_down(ext, 1) + w_ref[0:1, cs] * _shift_down(ext, 2)
            z = z_ref[:, cs]
            y_ref[:, cs] = (b_ref[:, cs] * conv * (z * _sigmoid(z))).astype(BF16)

    col = lambda c: pl.BlockSpec((tm, E), lambda i: (i, c))
    return pl.pallas_call(
        body, name=name,
        out_shape=jax.ShapeDtypeStruct((S, E), BF16),
        grid=(S // tm,),
        in_specs=[col(0), col(1), col(2), col(3), _prev_halo_spec(tm, E, 1), _prev_halo_spec(tm, E, 2),
                  pl.BlockSpec((8, E), lambda i: (0, 0))],
        out_specs=pl.BlockSpec((tm, E), lambda i: (i, 0)),
        compiler_params=_cparams("parallel"),
    )(h, h, h, h, h, h, cw)


def _conv_bwd(h, dy, cw, *, tm, name):
    S = h.shape[0]
    E = D_INNER
    nt = S // tm

    def body(b_ref, c_ref, hh_ref, z_ref, dy_ref, ch_ref, hhh_ref, bn_ref, zn_ref, dyn_ref, w_ref, dh_ref, dw_ref):
        i = pl.program_id(0)
        for j in range(E // CONV_CHUNK):
            cs = slice(j * CONV_CHUNK, (j + 1) * CONV_CHUNK)
            w0, w1, w2 = w_ref[0:1, cs], w_ref[1:2, cs], w_ref[2:3, cs]
            c, hh, b, z, dyv = c_ref[:, cs], hh_ref[:, cs], b_ref[:, cs], z_ref[:, cs], dy_ref[:, cs]
            p = c * hh
            ph = jnp.where(i > 0, ch_ref[:, cs] * hhh_ref[:, cs], 0.0)
            ext = jnp.concatenate([ph, p], axis=0)
            pm1 = _shift_down(ext, 1)
            pm2 = _shift_down(ext, 2)
            conv = w2 * p + w1 * pm1 + w0 * pm2
            sg = _sigmoid(z)
            dy0 = dyv * (z * sg)
            dz = dyv * (b * conv) * (sg * (1.0 + z * (1.0 - sg)))
            db = dy0 * conv
            dconv = dy0 * b
            zn = zn_ref[:, cs]
            dconv_n = jnp.where(i < nt - 1, dyn_ref[:, cs] * (zn * _sigmoid(zn)) * bn_ref[:, cs], 0.0)
            dext = jnp.concatenate([dconv, dconv_n], axis=0)
            dp = w2 * dconv + w1 * _shift_up(dext, 1, tm) + w0 * _shift_up(dext, 2, tm)
            dh_ref[:, 0 * E + j * CONV_CHUNK:0 * E + (j + 1) * CONV_CHUNK] = db.astype(BF16)
            dh_ref[:, 1 * E + j * CONV_CHUNK:1 * E + (j + 1) * CONV_CHUNK] = (dp * hh).astype(BF16)
            dh_ref[:, 2 * E + j * CONV_CHUNK:2 * E + (j + 1) * CONV_CHUNK] = (dp * c).astype(BF16)
            dh_ref[:, 3 * E + j * CONV_CHUNK:3 * E + (j + 1) * CONV_CHUNK] = dz.astype(BF16)
            dw = jnp.concatenate([jnp.sum(dconv * pm2, axis=0, keepdims=True),
                                  jnp.sum(dconv * pm1, axis=0, keepdims=True),
                                  jnp.sum(dconv * p, axis=0, keepdims=True),
                                  jnp.zeros((5, CONV_CHUNK), F32)], axis=0)

            @pl.when(i == 0)
            def _():
                dw_ref[:, cs] = dw

            @pl.when(i > 0)
            def _():
                dw_ref[:, cs] += dw

    col = lambda c: pl.BlockSpec((tm, E), lambda i: (i, c))
    return pl.pallas_call(
        body, name=name,
        out_shape=(jax.ShapeDtypeStruct((S, 4 * E), BF16), jax.ShapeDtypeStruct((8, E), F32)),
        grid=(nt,),
        in_specs=[col(0), col(1), col(2), col(3), pl.BlockSpec((tm, E), lambda i: (i, 0)),
                  _prev_halo_spec(tm, E, 1), _prev_halo_spec(tm, E, 2),
                  _next_halo_spec(tm, E, 0, S), _next_halo_spec(tm, E, 3, S), _next_halo_spec(tm, E, 0, S),
                  pl.BlockSpec((8, E), lambda i: (0, 0))],
        out_specs=(pl.BlockSpec((tm, 4 * E), lambda i: (i, 0)), pl.BlockSpec((8, E), lambda i: (0, 0))),
        compiler_params=_cparams("arbitrary"),
    )(h, h, h, h, dy, h, h, h, h, dy, cw)


Z_COLS = D_INNER // LANES
KV_LAT_BLK = D_INNER // KV_LORA
Q_LAT_BLK = (D_INNER + KV_LORA) // Q_LORA
K_ROPE_BLK = (D_INNER + KV_LORA + Q_LORA) // LANES


def _rope(blk, c, s1, s2):
    return blk * c + pltpu.roll(blk, LANES - QK_ROPE // 2, 1) * s1 + pltpu.roll(blk, QK_ROPE // 2, 1) * s2


def _unrope(blk, c, s1, s2):
    return blk * c - pltpu.roll(blk, LANES - QK_ROPE // 2, 1) * s1 - pltpu.roll(blk, QK_ROPE // 2, 1) * s2


def _lat_norm(v, g):
    rstd = lax.rsqrt(jnp.mean(v * v, axis=-1, keepdims=True) + NORM_EPS)
    return v * rstd * g


def _mla_latent_fwd(h, q_norm, kv_norm, tabs, *, tm, name):
    S = h.shape[0]

    def body(kv_ref, q_ref, kr_ref, qg_ref, kvg_ref, c_ref, s1_ref, s2_ref, qn_ref, kvn_ref, krr_ref):
        qn_ref[...] = _lat_norm(q_ref[...], qg_ref[...]).astype(BF16)
        kvn_ref[...] = _lat_norm(kv_ref[...], kvg_ref[...]).astype(BF16)
        krr_ref[...] = _rope(kr_ref[...], c_ref[...], s1_ref[...], s2_ref[...]).astype(BF16)

    tab = pl.BlockSpec((tm, LANES), lambda i: (i, 0))
    return pl.pallas_call(
        body, name=name,
        out_shape=(jax.ShapeDtypeStruct((S, Q_LORA), BF16), jax.ShapeDtypeStruct((S, KV_LORA), BF16),
                   jax.ShapeDtypeStruct((S, LANES), BF16)),
        grid=(S // tm,),
        in_specs=[pl.BlockSpec((tm, KV_LORA), lambda i: (i, KV_LAT_BLK)), pl.BlockSpec((tm, Q_LORA), lambda i: (i, Q_LAT_BLK)),
                  pl.BlockSpec((tm, LANES), lambda i: (i, K_ROPE_BLK)),
                  pl.BlockSpec((1, Q_LORA), lambda i: (0, 0)), pl.BlockSpec((1, KV_LORA), lambda i: (0, 0)), tab, tab, tab],
        out_specs=(pl.BlockSpec((tm, Q_LORA), lambda i: (i, 0)), pl.BlockSpec((tm, KV_LORA), lambda i: (i, 0)), tab),
        compiler_params=_cparams("parallel"),
    )(h, h, h, q_norm, kv_norm, *tabs)


def _mla_q_up(q_n, w_q_pad, tabs, *, tm, name):
    S = q_n.shape[0]

    def body(a_ref, w_ref, c_ref, s1_ref, s2_ref, o_ref):
        a = a_ref[...]
        for hd in range(N_HEADS):
            acc = jnp.dot(a, w_ref[:, hd * HEAD_PAD:(hd + 1) * HEAD_PAD], preferred_element_type=F32)
            o_ref[hd, :, :QK_NOPE] = acc[:, :QK_NOPE].astype(BF16)
            o_ref[hd, :, QK_NOPE:] = _rope(acc[:, QK_NOPE:], c_ref[...], s1_ref[...], s2_ref[...]).astype(BF16)

    tab = pl.BlockSpec((tm, LANES), lambda i: (i, 0))
    return pl.pallas_call(
        body, name=name,
        out_shape=jax.ShapeDtypeStruct((N_HEADS, S, HEAD_PAD), BF16),
        grid=(S // tm,),
        in_specs=[pl.BlockSpec((tm, Q_LORA), lambda i: (i, 0)), pl.BlockSpec((Q_LORA, N_HEADS * HEAD_PAD), lambda i: (0, 0)),
                  tab, tab, tab],
        out_specs=pl.BlockSpec((N_HEADS, tm, HEAD_PAD), lambda i: (0, i, 0)),
        compiler_params=_cparams("parallel"),
    )(q_n, w_q_pad, *tabs)


def _mla_kv_up(kv_n, w_kv, krr, *, tm, name):
    S = kv_n.shape[0]

    def body(a_ref, w_ref, krr_ref, k_ref, v_ref):
        a = a_ref[...]
        ones = jnp.ones((tm, V_DIM), BF16)
        for hd in range(N_HEADS):
            acc = jnp.dot(a, w_ref[:, hd * HEAD_PAD:(hd + 1) * HEAD_PAD], preferred_element_type=F32)
            k_ref[hd, :, :QK_NOPE] = acc[:, :QK_NOPE].astype(BF16)
            k_ref[hd, :, QK_NOPE:] = krr_ref[...]
            v_ref[hd, :, :V_DIM] = acc[:, QK_NOPE:].astype(BF16)
            v_ref[hd, :, V_DIM:] = ones

    head_blk = pl.BlockSpec((N_HEADS, tm, HEAD_PAD), lambda i: (0, i, 0))
    sds = jax.ShapeDtypeStruct((N_HEADS, S, HEAD_PAD), BF16)
    return pl.pallas_call(
        body, name=name, out_shape=(sds, sds),
        grid=(S // tm,),
        in_specs=[pl.BlockSpec((tm, KV_LORA), lambda i: (i, 0)), pl.BlockSpec((KV_LORA, N_HEADS * HEAD_PAD), lambda i: (0, 0)),
                  pl.BlockSpec((tm, LANES), lambda i: (i, 0))],
        out_specs=(head_blk, head_blk),
        compiler_params=_cparams("parallel"),
    )(kv_n, w_kv, krr)


LOG2E = 1.4426950408889634
SCORE_TO_LOG2 = ATTN_SCALE * LOG2E


def _flash_fwd(q_full, k_full, v_aug, h, *, tq, name):
    H, S, _ = q_full.shape
    tk = tq
    HP = 2

    def body(q_ref, k_ref, v_ref, z_ref, o_ref, y_ref, lse_ref, m_sc, acc_sc):
        qi = pl.program_id(1)
        m_sc[...] = jnp.full(m_sc.shape, -1e30, F32)
        acc_sc[...] = jnp.zeros(acc_sc.shape, F32)

        def chunk(j, diag):
            off = pl.multiple_of(j * tk, tk)
            for hh in range(HP):
                kj = k_ref[hh, pl.ds(off, tk), :]
                vj = v_ref[hh, pl.ds(off, tk), :]
                s = lax.dot_general(q_ref[hh], kj, (((1,), (1,)), ((), ())), preferred_element_type=F32) * SCORE_TO_LOG2
                if diag:
                    keep = lax.broadcasted_iota(jnp.int32, (tq, tk), 1) <= lax.broadcasted_iota(jnp.int32, (tq, tk), 0)
                    s = jnp.where(keep, s, -1e30)
                m_old = m_sc[hh]
                m_new = jnp.maximum(m_old, jnp.max(s, axis=1, keepdims=True))
                p = jnp.exp2(s - jnp.tile(m_new, (1, tk // LANES)))
                alpha = jnp.exp2(m_old - m_new)
                acc_sc[hh] = jnp.tile(alpha, (1, 2)) * acc_sc[hh] + jnp.dot(p.astype(BF16), vj, preferred_element_type=F32)
                m_sc[hh] = m_new

        def step(j, carry):
            chunk(j, False)
            return carry

        lax.fori_loop(0, qi, step, 0)
        chunk(qi, True)
        for hh in range(HP):
            cs = slice(hh * V_DIM, (hh + 1) * V_DIM)
            l = acc_sc[hh, :, V_DIM:]
            o = acc_sc[hh, :, :V_DIM] / l
            z = z_ref[:, cs]
            o_ref[:, cs] = o
            y_ref[:, cs] = (o * (z * _sigmoid(z))).astype(BF16)
            lse_ref[hh] = (m_sc[hh] + jnp.log2(l))[:, :1]

    pair = pl.BlockSpec((tq, HP * V_DIM), lambda hd, i: (i, hd))
    return pl.pallas_call(
        body, name=name,
        out_shape=(jax.ShapeDtypeStruct((S, D_INNER), F32), jax.ShapeDtypeStruct((S, D_INNER), BF16),
                   jax.ShapeDtypeStruct((H, S, 1), F32)),
        grid=(H // HP, S // tq),
        in_specs=[pl.BlockSpec((HP, tq, HEAD_PAD), lambda hd, i: (hd, i, 0)),
                  pl.BlockSpec((HP, S, HEAD_PAD), lambda hd, i: (hd, 0, 0)),
                  pl.BlockSpec((HP, S, HEAD_PAD), lambda hd, i: (hd, 0, 0)),
                  pair],
        out_specs=(pair, pair, pl.BlockSpec((HP, tq, 1), lambda hd, i: (hd, i, 0))),
        scratch_shapes=[pltpu.VMEM((HP, tq, LANES), F32), pltpu.VMEM((HP, tq, HEAD_PAD), F32)],
        compiler_params=_cparams("parallel", "parallel"),
    )(q_full, k_full, v_aug, h)


def _mla_gate_bwd(dy, o, h, *, tm, name):
    S = dy.shape[0]
    E = D_INNER

    def body(dy_ref, o_ref, z_ref, do_ref, dz_ref, delta_ref):
        for hd in range(N_HEADS):
            cs = slice(hd * V_DIM, (hd + 1) * V_DIM)
            z = z_ref[:, cs]
            sg = _sigmoid(z)
            dyv = dy_ref[:, cs]
            ov = o_ref[:, cs]
            do = dyv * (z * sg)
            do_ref[:, cs] = do.astype(BF16)
            dz_ref[:, cs] = (dyv * ov * (sg * (1.0 + z * (1.0 - sg)))).astype(BF16)
            delta_ref[hd] = jnp.sum(do * ov, axis=-1, keepdims=True)

    row = pl.BlockSpec((tm, E), lambda i: (i, 0))
    return pl.pallas_call(
        body, name=name,
        out_shape=(jax.ShapeDtypeStruct((S, E), BF16), jax.ShapeDtypeStruct((S, E), BF16),
                   jax.ShapeDtypeStruct((N_HEADS, S, 1), F32)),
        grid=(S // tm,),
        in_specs=[row, row, row],
        out_specs=(row, row, pl.BlockSpec((N_HEADS, tm, 1), lambda i: (0, i, 0))),
        compiler_params=_cparams("parallel"),
    )(dy, o, h)


def _flash_bwd(q_full, k_full, v_aug, do, lse_rows, delta_rows, *, tq, name):
    H, S, _ = q_full.shape
    tk = tq
    nq = S // tq
    HP = 2

    def body(q_ref, k_ref, v_ref, do_ref, lse_ref, dl_ref, dq_ref, dkv_ref, dkr_ref, dk_sc, dv_sc):
        kj = pl.program_id(1)

        @pl.when(kj == 0)
        def _():
            dq_ref[...] = jnp.zeros(dq_ref.shape, F32)

        dk_sc[...] = jnp.zeros(dk_sc.shape, F32)
        dv_sc[...] = jnp.zeros(dv_sc.shape, F32)

        def chunk(qi, diag):
            off = pl.multiple_of(qi * tq, tq)
            for hh in range(HP):
                k = k_ref[hh]
                q = q_ref[hh, pl.ds(off, tq), :]
                dov = do_ref[pl.ds(off, tq), hh * V_DIM:(hh + 1) * V_DIM]
                s_t = lax.dot_general(k, q, (((1,), (1,)), ((), ())), preferred_element_type=F32) * SCORE_TO_LOG2
                p_t = jnp.exp2(s_t - lse_ref[hh, qi])
                if diag:
                    keep = lax.broadcasted_iota(jnp.int32, (tk, tq), 0) <= lax.broadcasted_iota(jnp.int32, (tk, tq), 1)
                    p_t = jnp.where(keep, p_t, 0.0)
                dv_sc[hh] += jnp.dot(p_t.astype(BF16), dov, preferred_element_type=F32)
                dp_t = lax.dot_general(v_ref[hh], dov, (((1,), (1,)), ((), ())), preferred_element_type=F32)
                ds = (p_t * (dp_t - dl_ref[hh, qi])).astype(BF16)
                dk_sc[hh] += jnp.dot(ds, q, preferred_element_type=F32)
                dq_ref[pl.ds(off, tq), hh * HEAD_PAD:(hh + 1) * HEAD_PAD] += lax.dot_general(
                    ds, k, (((0,), (0,)), ((), ())), preferred_element_type=F32)

        def step(qi, carry):
            chunk(qi, False)
            return carry

        chunk(kj, True)
        lax.fori_loop(kj + 1, nq, step, 0)
        for hh in range(HP):
            lo = hh * HEAD_PAD
            dkv_ref[:, lo:lo + QK_NOPE] = (dk_sc[hh, :, :QK_NOPE] * ATTN_SCALE).astype(BF16)
            dkv_ref[:, lo + QK_NOPE:lo + HEAD_PAD] = dv_sc[hh].astype(BF16)
            dkr_ref[hh] = dk_sc[hh, :, QK_NOPE:] * ATTN_SCALE

    return pl.pallas_call(
        body, name=name,
        out_shape=(jax.ShapeDtypeStruct((S, H * HEAD_PAD), F32), jax.ShapeDtypeStruct((S, H * HEAD_PAD), BF16),
                   jax.ShapeDtypeStruct((H, S, LANES), F32)),
        grid=(H // HP, S // tk),
        in_specs=[pl.BlockSpec((HP, S, HEAD_PAD), lambda hd, j: (hd, 0, 0)),
                  pl.BlockSpec((HP, tk, HEAD_PAD), lambda hd, j: (hd, j, 0)),
                  pl.BlockSpec((HP, tk, V_DIM), lambda hd, j: (hd, j, 0)),
                  pl.BlockSpec((S, HP * V_DIM), lambda hd, j: (0, hd)),
                  pl.BlockSpec((HP, nq, 1, tq), lambda hd, j: (hd, 0, 0, 0)),
                  pl.BlockSpec((HP, nq, 1, tq), lambda hd, j: (hd, 0, 0, 0))],
        out_specs=(pl.BlockSpec((S, HP * HEAD_PAD), lambda hd, j: (0, hd)), pl.BlockSpec((tk, HP * HEAD_PAD), lambda hd, j: (j, hd)),
                   pl.BlockSpec((HP, tk, LANES), lambda hd, j: (hd, j, 0))),
        scratch_shapes=[pltpu.VMEM((HP, tk, HEAD_PAD), F32), pltpu.VMEM((HP, tk, V_DIM), F32)],
        compiler_params=_cparams("parallel", "arbitrary"),
    )(q_full, k_full, v_aug, do, lse_rows, delta_rows)


def _mla_unrope_q(dq, tabs, *, tm, name):
    S, W = dq.shape

    def body(dq_ref, c_ref, s1_ref, s2_ref, o_ref):
        for hd in range(N_HEADS):
            lo = hd * HEAD_PAD
            o_ref[:, lo:lo + QK_NOPE] = (dq_ref[:, lo:lo + QK_NOPE] * ATTN_SCALE).astype(BF16)
            o_ref[:, lo + QK_NOPE:lo + HEAD_PAD] = _unrope(dq_ref[:, lo + QK_NOPE:lo + HEAD_PAD] * ATTN_SCALE, c_ref[...], s1_ref[...], s2_ref[...]).astype(BF16)

    tab = pl.BlockSpec((tm, LANES), lambda i: (i, 0))
    row = pl.BlockSpec((tm, W), lambda i: (i, 0))
    return pl.pallas_call(
        body, name=name,
        out_shape=jax.ShapeDtypeStruct((S, W), BF16),
        grid=(S // tm,),
        in_specs=[row, tab, tab, tab], out_specs=row,
        compiler_params=_cparams("parallel"),
    )(dq, *tabs)


def _mla_latent_bwd(h, dq_n, dkv_n, dkr, dz, q_norm, kv_norm, tabs, *, tm, name):
    S = h.shape[0]

    def body(kv_ref, q_ref, dqn_ref, dkvn_ref, dkr_ref, dz_ref, qg_ref, kvg_ref, c_ref, s1_ref, s2_ref, dh_ref, dqg_ref, dkvg_ref):
        i = pl.program_id(0)
        dq_lat, dqg = _rms_bwd_math(q_ref[...], qg_ref[...], dqn_ref[...])
        dkv_lat, dkvg = _rms_bwd_math(kv_ref[...], kvg_ref[...], dkvn_ref[...])
        dkr_sum = dkr_ref[0]
        for hd in range(1, N_HEADS):
            dkr_sum = dkr_sum + dkr_ref[hd]
        dh_ref[:, :D_INNER] = dz_ref[...]
        dh_ref[:, D_INNER:D_INNER + KV_LORA] = dkv_lat.astype(BF16)
        dh_ref[:, D_INNER + KV_LORA:D_INNER + KV_LORA + Q_LORA] = dq_lat.astype(BF16)
        dh_ref[:, D_INNER + KV_LORA + Q_LORA:] = _unrope(dkr_sum, c_ref[...], s1_ref[...], s2_ref[...]).astype(BF16)

        @pl.when(i == 0)
        def _():
            dqg_ref[...] = dqg
            dkvg_ref[...] = dkvg

        @pl.when(i > 0)
        def _():
            dqg_ref[...] += dqg
            dkvg_ref[...] += dkvg

    tab = pl.BlockSpec((tm, LANES), lambda i: (i, 0))
    qvec = pl.BlockSpec((1, Q_LORA), lambda i: (0, 0))
    kvvec = pl.BlockSpec((1, KV_LORA), lambda i: (0, 0))
    return pl.pallas_call(
        body, name=name,
        out_shape=(jax.ShapeDtypeStruct((S, MLA_IN_PAD), BF16), jax.ShapeDtypeStruct((1, Q_LORA), F32),
                   jax.ShapeDtypeStruct((1, KV_LORA), F32)),
        grid=(S // tm,),
        in_specs=[pl.BlockSpec((tm, KV_LORA), lambda i: (i, KV_LAT_BLK)), pl.BlockSpec((tm, Q_LORA), lambda i: (i, Q_LAT_BLK)),
                  pl.BlockSpec((tm, Q_LORA), lambda i: (i, 0)), pl.BlockSpec((tm, KV_LORA), lambda i: (i, 0)),
                  pl.BlockSpec((N_HEADS, tm, LANES), lambda i: (0, i, 0)), pl.BlockSpec((tm, D_INNER), lambda i: (i, 0)),
                  qvec, kvvec, tab, tab, tab],
        out_specs=(pl.BlockSpec((tm, MLA_IN_PAD), lambda i: (i, 0)), qvec, kvvec),
        compiler_params=_cparams("arbitrary"),
    )(h, h, dq_n, dkv_n, dkr, dz, q_norm, kv_norm, *tabs)


def _adamw(w, g, m, v, *, name):
    R, C = w.shape
    tr = R
    for cand in (512, 256, 128, 64, 32, 16, 8):
        if R % cand == 0 and cand * C * 4 <= 2 * 1024 * 1024:
            tr = cand
            break

    def body(w_ref, g_ref, m_ref, v_ref, d_ref, nm_ref, nv_ref):
        gv = g_ref[...]
        m_new = ADAM_B1 * m_ref[...] + (1.0 - ADAM_B1) * gv
        v_new = ADAM_B2 * v_ref[...] + (1.0 - ADAM_B2) * (gv * gv)
        m_hat = m_new / (1.0 - ADAM_B1 ** ADAM_STEP)
        v_hat = v_new / (1.0 - ADAM_B2 ** ADAM_STEP)
        d_ref[...] = -ADAM_LR * (m_hat / (jnp.sqrt(v_hat) + ADAM_EPS) + ADAM_WD * w_ref[...])
        nm_ref[...] = m_new
        nv_ref[...] = v_new

    spec = pl.BlockSpec((tr, C), lambda i: (i, 0))
    sds = jax.ShapeDtypeStruct((R, C), F32)
    return pl.pallas_call(
        body, name=name, out_shape=(sds, sds, sds), grid=(R // tr,),
        in_specs=[spec] * 4, out_specs=(spec,) * 3,
        compiler_params=_cparams("parallel"),
    )(w, g, m, v)


HBM_SPEC = pl.BlockSpec(memory_space=pltpu.HBM)


def _place():
    return lax.axis_index("x"), lax.axis_index("y"), lax.axis_index("c")


def _other_chips(x, y):
    return [(1 - x, y), (x, 1 - y), (1 - x, 1 - y)]


def _remote(src, dst, send_sem, recv_sem, dev):
    return pltpu.make_async_remote_copy(src_ref=src, dst_ref=dst, send_sem=send_sem, recv_sem=recv_sem,
                                        device_id=dev, device_id_type=MESH)


def _all_gather_big(wp, *, name):
    R, C = wp.shape
    H = R // 2

    def body(w_ref, out_ref, send_sems, recv_sems):
        x, y, c = _place()
        sib = (x, y, 1 - c)
        chips = _other_chips(x, y)

        def blk(px, py, half):
            return out_ref.at[2 * px + py, pl.ds(pl.multiple_of(half * H, 16), H), :]

        my_half = w_ref.at[pl.ds(pl.multiple_of(c * H, 16), H), :]
        first = [_remote(my_half, blk(x, y, c), send_sems.at[r], recv_sems.at[r], (*chip, c)) for r, chip in enumerate(chips)]
        for cp in first:
            cp.start()
        passed = [_remote(blk(*chip, c), blk(*chip, c), send_sems.at[3 + r], recv_sems.at[3 + r], sib) for r, chip in enumerate(chips)]
        for r, chip in enumerate(chips):
            _remote(my_half, blk(*chip, c), send_sems.at[r], recv_sems.at[r], (*chip, c)).wait_recv()
            passed[r].start()
        for r, chip in enumerate(chips):
            _remote(my_half, blk(*chip, 1 - c), send_sems.at[3 + r], recv_sems.at[3 + r], sib).wait_recv()
        for cp in first + passed:
            cp.wait_send()

    return pl.pallas_call(
        body, name=name,
        out_shape=jax.ShapeDtypeStruct((N_CHIPS, R, C), wp.dtype),
        in_specs=[HBM_SPEC], out_specs=HBM_SPEC,
        scratch_shapes=[pltpu.SemaphoreType.DMA((6,)), pltpu.SemaphoreType.DMA((6,))],
    )(wp)


def _rs_sibling_swap(g, *, name):
    _, R, C = g.shape
    H = R // 2

    def body(g_ref, theirs_ref, send_sems, recv_sems):
        x, y, c = _place()
        sib = (x, y, 1 - c)
        copies = [_remote(g_ref.at[k, pl.ds(pl.multiple_of((1 - c) * H, 16), H), :], theirs_ref.at[k],
                          send_sems.at[k], recv_sems.at[k], sib) for k in range(N_CHIPS)]
        for cp in copies:
            cp.start()
        for cp in copies:
            cp.wait()

    return pl.pallas_call(
        body, name=name, out_shape=jax.ShapeDtypeStruct((N_CHIPS, H, C), g.dtype),
        in_specs=[HBM_SPEC], out_specs=HBM_SPEC,
        scratch_shapes=[pltpu.SemaphoreType.DMA((N_CHIPS,)), pltpu.SemaphoreType.DMA((N_CHIPS,))],
    )(g)


def _row_tile(h):
    best = 16
    for d in range(16, 1025, 16):
        if h % d == 0:
            best = d
    return best


def _add2_bf16(g, theirs, core, *, name):
    K, H, C = theirs.shape
    tr = _row_tile(H)
    nb = H // tr

    def body(c_ref, a_ref, b_ref, o_ref):
        o_ref[...] = (a_ref[...].astype(F32) + b_ref[...].astype(F32)).astype(o_ref.dtype)

    spec = pl.BlockSpec((None, tr, C), lambda k, i, c: (k, i, 0))
    return pl.pallas_call(
        body, name=name, out_shape=jax.ShapeDtypeStruct((K, H, C), theirs.dtype),
        grid_spec=pltpu.PrefetchScalarGridSpec(
            num_scalar_prefetch=1, grid=(K, nb),
            in_specs=[pl.BlockSpec((None, tr, C), lambda k, i, c: (k, c[0] * nb + i, 0)), spec], out_specs=spec),
        compiler_params=_cparams("parallel", "parallel"),
    )(core, g, theirs)


def _rs_chip_exchange(p, *, name):
    _, H, C = p.shape

    def body(p_ref, recv_ref, send_sems, recv_sems):
        x, y, c = _place()
        sends = [_remote(p_ref.at[2 * px + py], recv_ref.at[r], send_sems.at[r], recv_sems.at[r], (px, py, c))
                 for r, (px, py) in enumerate(_other_chips(x, y))]
        for cp in sends:
            cp.start()
        for cp in sends:
            cp.wait()

    return pl.pallas_call(
        body, name=name, out_shape=jax.ShapeDtypeStruct((3, H, C), p.dtype),
        in_specs=[HBM_SPEC], out_specs=HBM_SPEC,
        scratch_shapes=[pltpu.SemaphoreType.DMA((3,)), pltpu.SemaphoreType.DMA((3,))],
    )(p)


def _add4_f32(p, recv, chip_core, *, name):
    _, H, C = p.shape
    tr = _row_tile(H)
    nb = H // tr

    def body(s_ref, o_ref, r_ref, out_ref):
        out_ref[...] = ((o_ref[...].astype(F32) + r_ref[0].astype(F32)) + r_ref[1].astype(F32)) + r_ref[2].astype(F32)

    return pl.pallas_call(
        body, name=name, out_shape=jax.ShapeDtypeStruct((2 * H, C), F32),
        grid_spec=pltpu.PrefetchScalarGridSpec(
            num_scalar_prefetch=1, grid=(nb,),
            in_specs=[pl.BlockSpec((None, tr, C), lambda i, s: (s[0], i, 0)), pl.BlockSpec((3, tr, C), lambda i, s: (0, i, 0))],
            out_specs=pl.BlockSpec((tr, C), lambda i, s: (s[1] * nb + i, 0))),
        compiler_params=_cparams("parallel"),
    )(chip_core, p, recv)


def _rs_sibling_join(f, *, name):
    R, C = f.shape
    H = R // 2

    def body(f_ref, out_ref, send_sem, recv_sem):
        x, y, c = _place()
        sib = (x, y, 1 - c)
        mine = pl.ds(pl.multiple_of(c * H, 8), H)
        theirs = pl.ds(pl.multiple_of((1 - c) * H, 8), H)
        cp = _remote(f_ref.at[mine, :], out_ref.at[mine, :], send_sem, recv_sem, sib)
        cp.start()
        _remote(f_ref.at[mine, :], out_ref.at[theirs, :], send_sem, recv_sem, sib).wait_recv()
        cp.wait_send()

    return pl.pallas_call(
        body, name=name, out_shape=jax.ShapeDtypeStruct((R, C), f.dtype),
        in_specs=[HBM_SPEC], out_specs=HBM_SPEC, input_output_aliases={0: 0},
        scratch_shapes=[pltpu.SemaphoreType.DMA, pltpu.SemaphoreType.DMA],
    )(f)


def _small_exchange(vec, *, reduce, name):
    r, C = vec.shape

    def body(v_ref, out_ref, *rest):
        if reduce:
            all_ref, send_sems, recv_sems = rest
        else:
            all_ref = out_ref
            send_sems, recv_sems = rest
        x, y, c = _place()
        me = 4 * x + 2 * y + c
        all_ref[me] = v_ref[...]
        peers = []
        for rel in range(1, N_DEV):
            dx, dy, dc = (rel >> 2) & 1, (rel >> 1) & 1, rel & 1
            peers.append((1 - x if dx else x, 1 - y if dy else y, 1 - c if dc else c))
        sends = [_remote(v_ref, all_ref.at[me], send_sems.at[k], recv_sems.at[k], peer) for k, peer in enumerate(peers)]
        for cp in sends:
            cp.start()
        for k, (px, py, pc) in enumerate(peers):
            _remote(v_ref, all_ref.at[4 * px + 2 * py + pc], send_sems.at[k], recv_sems.at[k], (px, py, pc)).wait_recv()
        for cp in sends:
            cp.wait_send()
        if reduce:
            s = all_ref[0]
            for k in range(1, N_DEV):
                s = s + all_ref[k]
            out_ref[...] = s

    vm = pl.BlockSpec(memory_space=pltpu.VMEM)
    sems = [pltpu.SemaphoreType.DMA((N_DEV - 1,)), pltpu.SemaphoreType.DMA((N_DEV - 1,))]
    if reduce:
        return pl.pallas_call(
            body, name=name, out_shape=jax.ShapeDtypeStruct((r, C), F32), in_specs=[vm], out_specs=vm,
            scratch_shapes=[pltpu.VMEM((N_DEV, r, C), F32)] + sems,
        )(vec)
    return pl.pallas_call(
        body, name=name, out_shape=jax.ShapeDtypeStruct((N_DEV, r, C), F32), in_specs=[vm], out_specs=vm,
        scratch_shapes=sems,
    )(vec)


BIG = ("pool_w_in", "pool_w_grp", "pool_w_out", "conv_w_in", "conv_w_out", "mla_w_in", "mla_w_q_up", "mla_w_kv_up", "mla_w_out")
BIG_SHARD_AXIS = {"pool_w_in": 2, "pool_w_grp": 2, "pool_w_out": 1, "conv_w_in": 2, "conv_w_out": 1,
                  "mla_w_in": 2, "mla_w_q_up": 2, "mla_w_kv_up": 2, "mla_w_out": 1}
PACK_ROW_ALIGN = 32


def _pack_rows(parts):
    rows = [p.reshape(-1, PACK_COLS) for p in parts]
    n = sum(r.shape[0] for r in rows)
    pad = (-n) % PACK_ROW_ALIGN
    if pad:
        rows.append(jnp.zeros((pad, PACK_COLS), rows[0].dtype))
    return jnp.concatenate(rows, axis=0)


def _unpack_rows(buf, shapes):
    out, r0 = [], 0
    for shp in shapes:
        n = 1
        for d in shp:
            n *= d
        nr = n // PACK_COLS
        out.append(buf[r0:r0 + nr].reshape(shp))
        r0 += nr
    return out


def _join_shards(gathered, shard_shapes, axes):
    per_chip = [_unpack_rows(gathered[k], shard_shapes) for k in range(N_CHIPS)]
    return [jnp.concatenate([per_chip[k][i] for k in range(N_CHIPS)], axis=ax) for i, ax in enumerate(axes)]


def _split_shards(full, axis):
    return jnp.split(full, N_CHIPS, axis=axis)


SMALL = ("pool_norm", "pool_scale", "conv_norm", "conv_w", "mla_norm", "mla_q_norm", "mla_kv_norm", "final_norm")
SMALL_SHARDED = {"pool_norm": True, "pool_scale": True, "conv_norm": False, "conv_w": True, "mla_norm": True,
                 "mla_q_norm": True, "mla_kv_norm": True, "final_norm": False}


def _pack_small(parts):
    flat, offs, n = [], [], 0
    for p in parts:
        v = p.reshape(-1)
        pad = (-v.shape[0]) % LANES
        flat.append(jnp.pad(v, (0, pad)))
        offs.append(n)
        n += v.shape[0] + pad
    pad = (-n) % (8 * LANES)
    if pad:
        flat.append(jnp.zeros((pad,), F32))
    return jnp.concatenate(flat).reshape(-1, LANES), offs


def _unpack_small(buf, offs, shapes):
    v = buf.reshape(-1)
    out = []
    for o, shp in zip(offs, shapes):
        n = 1
        for d in shp:
            n *= d
        out.append(v[o:o + n].reshape(shp))
    return out


def _rope_tables(positions):
    inv_freq = ROPE_BASE ** (-jnp.arange(0, QK_ROPE, 2, dtype=F32) / QK_ROPE)
    ang = positions.astype(F32).reshape(-1, 1) * inv_freq
    cos, sin = jnp.cos(ang), jnp.sin(ang)
    z32 = jnp.zeros_like(cos)
    z64 = jnp.concatenate([z32, z32], axis=1)
    return (jnp.concatenate([cos, cos, z64], axis=1), jnp.concatenate([-sin, z32, z64], axis=1),
            jnp.concatenate([z32, sin, z64], axis=1))


def _mla_in_to_padded(w):
    q, kv, kr, z = w[:, :Q_LORA], w[:, Q_LORA:Q_LORA + KV_LORA], w[:, Q_LORA + KV_LORA:Q_LORA + KV_LORA + QK_ROPE], w[:, Q_LORA + KV_LORA + QK_ROPE:]
    return jnp.concatenate([z, kv, q, kr, jnp.zeros((w.shape[0], MLA_IN_PAD - MLA_IN), w.dtype)], axis=1)


def _mla_in_from_padded(w):
    z, kv, q, kr = w[:, :D_INNER], w[:, D_INNER:D_INNER + KV_LORA], w[:, D_INNER + KV_LORA:D_INNER + KV_LORA + Q_LORA], w[:, D_INNER + KV_LORA + Q_LORA:D_INNER + KV_LORA + Q_LORA + QK_ROPE]
    return jnp.concatenate([q, kv, kr, z], axis=1)


def _q_up_to_padded(w):
    k = w.shape[0]
    return jnp.pad(w.reshape(k, N_HEADS, QK_NOPE + QK_ROPE), ((0, 0), (0, 0), (0, HEAD_PAD - QK_NOPE - QK_ROPE))).reshape(k, N_HEADS * HEAD_PAD)


def _q_up_from_padded(w):
    k = w.shape[0]
    return w.reshape(k, N_HEADS, HEAD_PAD)[:, :, :QK_NOPE + QK_ROPE].reshape(k, N_HEADS * (QK_NOPE + QK_ROPE))


def _local_step(x, positions, target, wb, ws):
    S = x.shape[0]
    tm = min(512, S)
    te = min(256, S)
    tq = min(512, S)
    tabs = _rope_tables(positions)
    gb, gs = {}, {}

    def mm_in(xn, w, name):
        n = w.shape[1]
        return _mm(xn, w, tm=min(1024, S), tn=_pick(n, 1536 if n == MLA_IN_PAD else 1024), tk=D_MODEL, name=name)

    def mm_out(y, w, res, name):
        return _mm(y, w, residual=res, tm=tm, tn=D_MODEL, tk=D_INNER, name=name)

    def mm_dx(dy, w, name):
        k, n = w.shape
        return _mm(dy, w, trans_b=True, tm=min(1024, S), tn=_pick(k, 1024), tk=_pick(n, 1408), name=name)

    def mm_dw(a, b, name):
        ka, nb = a.shape[1], b.shape[1]
        return _mm(a, b, trans_a=True, out_dtype=BF16, tm=_pick(ka, 1024), tn=_pick(nb, 1408), tk=tm, name=name)

    def pool_layer_fwd(xin, j, tag):
        xn = _rms_fwd(xin, ws["pool_norm"][j:j + 1], tm=tm, name=f"{tag}_norm")
        h = mm_in(xn, wb["pool_w_in"][j], f"{tag}_in")
        y = _pool_fwd(h, wb["pool_w_grp"][j], ws["pool_scale"][j:j + 1], tm=te, name=f"{tag}_mix")
        xo = mm_out(y, wb["pool_w_out"][j], xin, f"{tag}_out")
        return xo, (xin, xn, h, y)

    def pool_layer_bwd(dx, dxb, saved, j, tag):
        xin, xn, h, y = saved
        dy = mm_dx(dxb, wb["pool_w_out"][j], f"{tag}_dy")
        dwo = mm_dw(y, dxb, f"{tag}_dwo")
        pooled, dmixed, dpooled, dz, dsc = _pool_bwd1(h, dy, wb["pool_w_grp"][j], ws["pool_scale"][j:j + 1], tm=te, name=f"{tag}_bmix")
        dwg = _grouped_tn(pooled, dmixed, tk=tm, name=f"{tag}_dwg")
        dh = _pool_bwd2(dpooled, dz, tm=te, name=f"{tag}_bshift")
        dxn = mm_dx(dh, wb["pool_w_in"][j], f"{tag}_dxn")
        dwi = mm_dw(xn, dh, f"{tag}_dwi")
        dxo, dxob, dg = _rms_bwd(xin, ws["pool_norm"][j:j + 1], dxn, dx, tm=tm, name=f"{tag}_bnorm")
        return dxo, dxob, dict(pool_w_in=dwi, pool_w_grp=dwg, pool_w_out=dwo), dict(pool_norm=dg, pool_scale=dsc)

    x1, sv0 = pool_layer_fwd(x, 0, "p0")

    xn1 = _rms_fwd(x1, ws["conv_norm"][0:1], tm=tm, name="cv_norm")
    h1 = mm_in(xn1, wb["conv_w_in"][0], "cv_in")
    cw = jnp.pad(ws["conv_w"][0], ((0, 5), (0, 0)))
    y1 = _conv_fwd(h1, cw, tm=te, name="cv_mix")
    x2 = mm_out(y1, wb["conv_w_out"][0], x1, "cv_out")

    w_mi = _mla_in_to_padded(wb["mla_w_in"][0])
    w_q = _q_up_to_padded(wb["mla_w_q_up"][0])
    w_kv = wb["mla_w_kv_up"][0]
    qg, kvg = ws["mla_q_norm"][0:1], ws["mla_kv_norm"][0:1]
    xn2 = _rms_fwd(x2, ws["mla_norm"][0:1], tm=tm, name="ml_norm")
    h2 = mm_in(xn2, w_mi, "ml_in")
    q_n, kv_n, krr = _mla_latent_fwd(h2, qg, kvg, tabs, tm=tm, name="ml_lat")
    q_full = _mla_q_up(q_n, w_q, tabs, tm=tm, name="ml_qup")
    k_full, v = _mla_kv_up(kv_n, w_kv, krr, tm=tm, name="ml_kvup")
    o, y2, lse = _flash_fwd(q_full, k_full, v, h2, tq=tq, name="ml_attn")
    x3 = mm_out(y2, wb["mla_w_out"][0], x2, "ml_out")

    x4, sv3 = pool_layer_fwd(x3, 1, "p1")

    loss_part, dx, dxb, dgf = _final_loss(x4, ws["final_norm"].reshape(1, -1), target, tm=tm, name="final")
    gs["final_norm"] = dgf.reshape(-1)

    dx, dxb, gb3, gs3 = pool_layer_bwd(dx, dxb, sv3, 1, "p1")

    dy = mm_dx(dxb, wb["mla_w_out"][0], "ml_dy")
    gb["mla_w_out"] = mm_dw(y2, dxb, "ml_dwo")[None]
    do, dz, delta = _mla_gate_bwd(dy, o, h2, tm=te, name="ml_bgate")
    nq = S // tq
    dq, dkv, dkr = _flash_bwd(q_full, k_full, v, do, lse.reshape(N_HEADS, nq, 1, tq), delta.reshape(N_HEADS, nq, 1, tq), tq=tq, name="ml_battn")
    dq_pre = _mla_unrope_q(dq, tabs, tm=te, name="ml_bqrope")
    dq_n = mm_dx(dq_pre, w_q, "ml_dqn")
    gb["mla_w_q_up"] = _q_up_from_padded(mm_dw(q_n, dq_pre, "ml_dwq"))[None]
    dkv_n = mm_dx(dkv, w_kv, "ml_dkvn")
    gb["mla_w_kv_up"] = mm_dw(kv_n, dkv, "ml_dwkv")[None]
    dh2, dqg, dkvg = _mla_latent_bwd(h2, dq_n, dkv_n, dkr, dz, qg, kvg, tabs, tm=te, name="ml_blat")
    dxn2 = mm_dx(dh2, w_mi, "ml_dxn")
    gb["mla_w_in"] = _mla_in_from_padded(mm_dw(xn2, dh2, "ml_dwi"))[None]
    dx, dxb, dg2 = _rms_bwd(x2, ws["mla_norm"][0:1], dxn2, dx, tm=tm, name="ml_bnorm")
    gs["mla_norm"], gs["mla_q_norm"], gs["mla_kv_norm"] = dg2, dqg, dkvg

    dy = mm_dx(dxb, wb["conv_w_out"][0], "cv_dy")
    gb["conv_w_out"] = mm_dw(y1, dxb, "cv_dwo")[None]
    dh1, dcw = _conv_bwd(h1, dy, cw, tm=te, name="cv_bmix")
    dxn1 = mm_dx(dh1, wb["conv_w_in"][0], "cv_dxn")
    gb["conv_w_in"] = mm_dw(xn1, dh1, "cv_dwi")[None]
    dx, dxb, dg1 = _rms_bwd(x1, ws["conv_norm"][0:1], dxn1, dx, tm=tm, name="cv_bnorm")
    gs["conv_norm"], gs["conv_w"] = dg1, dcw[None, :3]

    dx, dxb, gb0, gs0 = pool_layer_bwd(dx, dxb, sv0, 0, "p0")
    for k in ("pool_w_in", "pool_w_grp", "pool_w_out"):
        gb[k] = jnp.stack([gb0[k], gb3[k]])
    for k in ("pool_norm", "pool_scale"):
        gs[k] = jnp.concatenate([gs0[k], gs3[k]], axis=0)
    return loss_part, dx, gb, gs


def kernel(x, positions, pool_norm, pool_w_in, pool_w_grp, pool_scale, pool_w_out, conv_norm, conv_w_in, conv_w, conv_w_out, mla_norm, mla_w_in, mla_q_norm, mla_w_q_up, mla_kv_norm, mla_w_kv_up, mla_w_out, final_norm, loss_target, m_pool_norm, m_pool_w_in, m_pool_w_grp, m_pool_scale, m_pool_w_out, m_conv_norm, m_conv_w_in, m_conv_w, m_conv_w_out, m_mla_norm, m_mla_w_in, m_mla_q_norm, m_mla_w_q_up, m_mla_kv_norm, m_mla_w_kv_up, m_mla_w_out, m_final_norm, v_pool_norm, v_pool_w_in, v_pool_w_grp, v_pool_scale, v_pool_w_out, v_conv_norm, v_conv_w_in, v_conv_w, v_conv_w_out, v_mla_norm, v_mla_w_in, v_mla_q_norm, v_mla_w_q_up, v_mla_kv_norm, v_mla_w_kv_up, v_mla_w_out, v_final_norm):
    names = ("pool_norm", "pool_w_in", "pool_w_grp", "pool_scale", "pool_w_out", "conv_norm", "conv_w_in", "conv_w", "conv_w_out",
             "mla_norm", "mla_w_in", "mla_q_norm", "mla_w_q_up", "mla_kv_norm", "mla_w_kv_up", "mla_w_out", "final_norm")
    w = dict(zip(names, (pool_norm, pool_w_in, pool_w_grp, pool_scale, pool_w_out, conv_norm, conv_w_in, conv_w, conv_w_out,
                         mla_norm, mla_w_in, mla_q_norm, mla_w_q_up, mla_kv_norm, mla_w_kv_up, mla_w_out, final_norm)))
    m = dict(zip(names, (m_pool_norm, m_pool_w_in, m_pool_w_grp, m_pool_scale, m_pool_w_out, m_conv_norm, m_conv_w_in, m_conv_w, m_conv_w_out,
                         m_mla_norm, m_mla_w_in, m_mla_q_norm, m_mla_w_q_up, m_mla_kv_norm, m_mla_w_kv_up, m_mla_w_out, m_final_norm)))
    v = dict(zip(names, (v_pool_norm, v_pool_w_in, v_pool_w_grp, v_pool_scale, v_pool_w_out, v_conv_norm, v_conv_w_in, v_conv_w, v_conv_w_out,
                         v_mla_norm, v_mla_w_in, v_mla_q_norm, v_mla_w_q_up, v_mla_kv_norm, v_mla_w_kv_up, v_mla_w_out, v_final_norm)))
    chip = 2 * lax.axis_index("x") + lax.axis_index("y")
    core = lax.axis_index("c")

    big_shapes = [w[n].shape for n in BIG]
    big_axes = [BIG_SHARD_AXIS[n] for n in BIG]
    own_rows = _pack_rows([w[n].astype(BF16) for n in BIG])
    gathered = lax.dynamic_update_slice(_all_gather_big(own_rows, name="ag_big"), own_rows[None], (chip, 0, 0))
    wb = dict(zip(BIG, _join_shards(gathered, big_shapes, big_axes)))

    small_shapes = [w[n].shape for n in SMALL]
    small_pack, small_offs = _pack_small([w[n] for n in SMALL])
    small_all = _small_exchange(small_pack, reduce=False, name="ag_small")
    per_chip = [_unpack_small(small_all[4 * (k // 2) + 2 * (k % 2)], small_offs, small_shapes) for k in range(N_CHIPS)]
    ws = {}
    for i, n in enumerate(SMALL):
        ws[n] = jnp.concatenate([per_chip[k][i] for k in range(N_CHIPS)], axis=-1) if SMALL_SHARDED[n] else w[n]

    loss_part, grad_x, gb, gs = _local_step(x[0], positions, loss_target[0], wb, ws)
    loss = lax.psum(loss_part[0, 0], ("x", "y", "c"))

    packed = jnp.stack([_pack_rows([_split_shards(gb[n], BIG_SHARD_AXIS[n])[k] for n in BIG]) for k in range(N_CHIPS)])
    theirs = _rs_sibling_swap(packed, name="rs_swap")
    chip_sum = _add2_bf16(packed, theirs, core.astype(jnp.int32).reshape(1), name="rs_add2")
    recv = _rs_chip_exchange(chip_sum, name="rs_chips")
    half_sum = _add4_f32(chip_sum, recv, jnp.stack([chip, core]).astype(jnp.int32), name="rs_add4")
    g_rows = _rs_sibling_join(half_sum, name="rs_join")
    g = dict(zip(BIG, _unpack_rows(g_rows, big_shapes)))

    full_small_shapes = [gs[n].shape for n in SMALL]
    gs_pack, gs_offs = _pack_small([gs[n] for n in SMALL])
    gs_sum = _unpack_small(_small_exchange(gs_pack, reduce=True, name="ar_small"), gs_offs, full_small_shapes)
    for i, n in enumerate(SMALL):
        if SMALL_SHARDED[n]:
            width = w[n].shape[-1]
            g[n] = lax.dynamic_slice_in_dim(gs_sum[i], chip * width, width, axis=gs_sum[i].ndim - 1)
        else:
            g[n] = gs_sum[i].reshape(w[n].shape)

    delta, new_m, new_v = {}, {}, {}
    for n in BIG:
        shp = w[n].shape
        two_d = lambda a: a.reshape(-1, shp[-1])
        d_, m_, v_ = _adamw(two_d(w[n]), two_d(g[n]), two_d(m[n]), two_d(v[n]), name=f"adamw_{n}")
        delta[n], new_m[n], new_v[n] = d_.reshape(shp), m_.reshape(shp), v_.reshape(shp)
    sw, s_offs = _pack_small([w[n] for n in SMALL])
    sg, _ = _pack_small([g[n] for n in SMALL])
    sm, _ = _pack_small([m[n] for n in SMALL])
    sv, _ = _pack_small([v[n] for n in SMALL])
    sd, snm, snv = _adamw(sw, sg, sm, sv, name="adamw_small")
    for dst, buf in ((delta, sd), (new_m, snm), (new_v, snv)):
        for n, a in zip(SMALL, _unpack_small(buf, s_offs, small_shapes)):
            dst[n] = a

    return (loss, grad_x[None], *[g[n] for n in names], *[delta[n] for n in names],
            *[new_m[n] for n in names], *[new_v[n] for n in names])
```

```python
import functools

import jax
import jax.numpy as jnp
from jax import lax
from jax.experimental import pallas as pl
from jax.experimental.pallas import tpu as pltpu

F32 = jnp.float32
BF16 = jnp.bfloat16

D_MODEL = 1024
D_INNER = 2048
POOL_WINDOWS = (2, 4, 8, 16)
POOL_GROUP = 512
N_HEADS = 16
QK_NOPE = 128
QK_ROPE = 64
V_DIM = 128
HEAD_PAD = 256
Q_LORA = 384
KV_LORA = 256
MLA_IN = Q_LORA + KV_LORA + QK_ROPE + D_INNER
MLA_IN_PAD = 2816
ATTN_SCALE = (QK_NOPE + QK_ROPE) ** -0.5
ROPE_BASE = 10000.0
NORM_EPS = 1e-6
HALO = 16

ADAM_LR = 0.001
ADAM_B1 = 0.9
ADAM_B2 = 0.999
ADAM_EPS = 1e-08
ADAM_WD = 0.01
ADAM_STEP = 10

N_CHIPS = 4
N_DEV = 8
LANES = 128
PACK_COLS = 1024
V7X_VMEM_LIMIT = 56 * 1024 * 1024
MESH = pl.DeviceIdType.MESH


def _cparams(*sem):
    return pltpu.CompilerParams(dimension_semantics=sem, vmem_limit_bytes=V7X_VMEM_LIMIT)


def _pick(n, cap):
    best = None
    for d in range(LANES, min(n, cap) + 1, LANES):
        if n % d == 0:
            best = d
    assert best is not None, (n, cap)
    return best


def _sigmoid(z):
    return 1.0 / (1.0 + jnp.exp(-z))


def _mm(a, b, *, trans_a=False, trans_b=False, out_dtype=F32, residual=None, after=None, tm, tn, tk, name):
    if trans_a:
        K, M = a.shape
    else:
        M, K = a.shape
    if trans_b:
        N, K2 = b.shape
    else:
        K2, N = b.shape
    assert K == K2 and M % tm == 0 and N % tn == 0 and K % tk == 0, (name, a.shape, b.shape, tm, tn, tk)
    nk = K // tk
    dn = (((0 if trans_a else 1,), (1 if trans_b else 0,)), ((), ()))
    has_res = residual is not None

    def body(*refs):
        if has_res:
            a_ref, b_ref, r_ref = refs[:3]
            refs = refs[3:]
        else:
            a_ref, b_ref = refs[:2]
            r_ref = None
            refs = refs[2:]
        if after is not None:
            refs = refs[1:]
        o_ref, rest = refs[0], refs[1:]
        part = lax.dot_general(a_ref[...], b_ref[...], dn, preferred_element_type=F32)

        def finish(acc):
            if has_res:
                acc = acc + r_ref[...]
            o_ref[...] = acc.astype(o_ref.dtype)

        if nk == 1:
            finish(part)
        else:
            acc_ref = rest[0]
            k = pl.program_id(2)

            @pl.when(k == 0)
            def _():
                acc_ref[...] = part

            @pl.when(k > 0)
            def _():
                acc_ref[...] += part

            @pl.when(k == nk - 1)
            def _():
                finish(acc_ref[...])

    a_spec = pl.BlockSpec((tk, tm), lambda i, j, k: (k, i)) if trans_a else pl.BlockSpec((tm, tk), lambda i, j, k: (i, k))
    b_spec = pl.BlockSpec((tn, tk), lambda i, j, k: (j, k)) if trans_b else pl.BlockSpec((tk, tn), lambda i, j, k: (k, j))
    o_spec = pl.BlockSpec((tm, tn), lambda i, j, k: (i, j))
    in_specs = [a_spec, b_spec] + ([o_spec] if has_res else []) + ([pl.BlockSpec(memory_space=pl.ANY)] if after is not None else [])
    args = (a, b) + ((residual,) if has_res else ()) + ((after,) if after is not None else ())
    return pl.pallas_call(
        body, name=name,
        out_shape=jax.ShapeDtypeStruct((M, N), out_dtype),
        grid=(M // tm, N // tn, nk),
        in_specs=in_specs, out_specs=o_spec,
        scratch_shapes=[pltpu.VMEM((tm, tn), F32)] if nk > 1 else [],
        compiler_params=_cparams("parallel", "parallel", "arbitrary"),
    )(*args)


def _grouped_tn(a, b, *, tk, name):
    S = a.shape[0]
    G = POOL_GROUP
    nk = S // tk

    def body(a_ref, b_ref, o_ref, acc_ref):
        k = pl.program_id(1)
        part = lax.dot_general(a_ref[...], b_ref[...], (((0,), (0,)), ((), ())), preferred_element_type=F32)

        @pl.when(k == 0)
        def _():
            acc_ref[...] = part

        @pl.when(k > 0)
        def _():
            acc_ref[...] += part

        @pl.when(k == nk - 1)
        def _():
            o_ref[...] = acc_ref[...].astype(o_ref.dtype)

    return pl.pallas_call(
        body, name=name,
        out_shape=jax.ShapeDtypeStruct((len(POOL_WINDOWS), G, G), BF16),
        grid=(len(POOL_WINDOWS), nk),
        in_specs=[pl.BlockSpec((tk, G), lambda g, k: (k, g)), pl.BlockSpec((tk, G), lambda g, k: (k, g))],
        out_specs=pl.BlockSpec((None, G, G), lambda g, k: (g, 0, 0)),
        scratch_shapes=[pltpu.VMEM((G, G), F32)],
        compiler_params=_cparams("parallel", "arbitrary"),
    )(a, b)


def _rms_fwd(x, g, *, tm, name):
    S, D = x.shape

    def body(x_ref, g_ref, o_ref):
        xv = x_ref[...]
        rstd = lax.rsqrt(jnp.mean(xv * xv, axis=-1, keepdims=True) + NORM_EPS)
        o_ref[...] = (xv * rstd * g_ref[...]).astype(o_ref.dtype)

    return pl.pallas_call(
        body, name=name,
        out_shape=jax.ShapeDtypeStruct((S, D), BF16),
        grid=(S // tm,),
        in_specs=[pl.BlockSpec((tm, D), lambda i: (i, 0)), pl.BlockSpec((1, D), lambda i: (0, 0))],
        out_specs=pl.BlockSpec((tm, D), lambda i: (i, 0)),
        compiler_params=_cparams("parallel"),
    )(x, g)


def _rms_bwd_math(xv, gv, dxn):
    rstd = lax.rsqrt(jnp.mean(xv * xv, axis=-1, keepdims=True) + NORM_EPS)
    xh = xv * rstd
    dg = jnp.sum(dxn * xh, axis=0, keepdims=True)
    dxh = dxn * gv
    dx = rstd * (dxh - xh * jnp.mean(dxh * xh, axis=-1, keepdims=True))
    return dx, dg


def _rms_bwd(x, g, dxn, dres, *, tm, name):
    S, D = x.shape

    def body(x_ref, g_ref, dxn_ref, dres_ref, dx_ref, dxb_ref, dg_ref):
        dx, dg = _rms_bwd_math(x_ref[...], g_ref[...], dxn_ref[...])
        dx = dx + dres_ref[...]
        dx_ref[...] = dx
        dxb_ref[...] = dx.astype(BF16)

        @pl.when(pl.program_id(0) == 0)
        def _():
            dg_ref[...] = dg

        @pl.when(pl.program_id(0) > 0)
        def _():
            dg_ref[...] += dg

    row = pl.BlockSpec((tm, D), lambda i: (i, 0))
    vec = pl.BlockSpec((1, D), lambda i: (0, 0))
    return pl.pallas_call(
        body, name=name,
        out_shape=(jax.ShapeDtypeStruct((S, D), F32), jax.ShapeDtypeStruct((S, D), BF16), jax.ShapeDtypeStruct((1, D), F32)),
        grid=(S // tm,),
        in_specs=[row, vec, row, row],
        out_specs=(row, row, vec),
        compiler_params=_cparams("arbitrary"),
    )(x, g, dxn, dres)


def _final_loss(x, g, target, *, tm, name):
    S, D = x.shape

    def body(x_ref, g_ref, t_ref, loss_ref, dx_ref, dxb_ref, dg_ref):
        xv = x_ref[...]
        gv = g_ref[...]
        rstd = lax.rsqrt(jnp.mean(xv * xv, axis=-1, keepdims=True) + NORM_EPS)
        xh = xv * rstd
        err = xh * gv - t_ref[...]
        part = 0.5 * jnp.sum(jnp.mean(err * err, axis=-1, keepdims=True), axis=0, keepdims=True)
        dy = err * (1.0 / D)
        dg = jnp.sum(dy * xh, axis=0, keepdims=True)
        dxh = dy * gv
        dx = rstd * (dxh - xh * jnp.mean(dxh * xh, axis=-1, keepdims=True))
        dx_ref[...] = dx
        dxb_ref[...] = dx.astype(BF16)
        lossb = jnp.broadcast_to(part, loss_ref.shape)

        @pl.when(pl.program_id(0) == 0)
        def _():
            dg_ref[...] = dg
            loss_ref[...] = lossb

        @pl.when(pl.program_id(0) > 0)
        def _():
            dg_ref[...] += dg
            loss_ref[...] += lossb

    row = pl.BlockSpec((tm, D), lambda i: (i, 0))
    vec = pl.BlockSpec((1, D), lambda i: (0, 0))
    lspec = pl.BlockSpec((8, LANES), lambda i: (0, 0))
    return pl.pallas_call(
        body, name=name,
        out_shape=(jax.ShapeDtypeStruct((8, LANES), F32), jax.ShapeDtypeStruct((S, D), F32),
                   jax.ShapeDtypeStruct((S, D), BF16), jax.ShapeDtypeStruct((1, D), F32)),
        grid=(S // tm,),
        in_specs=[row, vec, row],
        out_specs=(lspec, row, row, vec),
        compiler_params=_cparams("arbitrary"),
    )(x, g, target)


def _prev_halo_spec(tm, width, col):
    r = tm // HALO
    return pl.BlockSpec((HALO, width), lambda i: (jnp.maximum(i * r - 1, 0), col))


def _next_halo_spec(tm, width, col, S):
    r = tm // HALO
    last = S // HALO - 1
    return pl.BlockSpec((HALO, width), lambda i: (jnp.minimum((i + 1) * r, last), col))


def _shift_down(ext, k):
    return pltpu.roll(ext, k, 0)[HALO:, :]


def _shift_up(ext, k, tm):
    n = ext.shape[0]
    return pltpu.roll(ext, n - k, 0)[:tm, :]


def _pool_window_sum(ext, w):
    s = ext
    k = 1
    while k < w:
        s = s + pltpu.roll(s, k, 0)
        k *= 2
    return s[HALO:, :]


def _pooled_group(u_ref, halo, g, w, t_idx):
    cs = slice(g * POOL_GROUP, (g + 1) * POOL_GROUP)
    u = u_ref[:, cs]
    ext = jnp.concatenate([halo[:, cs], u], axis=0)
    inv = 1.0 / jnp.minimum(t_idx + 1, w).astype(F32)
    return _pool_window_sum(ext, w) * inv - u


def _pool_fwd(h, w_grp, scale, *, tm, name):
    S = h.shape[0]
    E = D_INNER

    def body(u_ref, uh_ref, z_ref, wg_ref, sc_ref, y_ref):
        i = pl.program_id(0)
        halo = jnp.where(i > 0, uh_ref[...], 0.0)
        t_idx = i * tm + lax.broadcasted_iota(jnp.int32, (tm, 1), 0)
        for g, w in enumerate(POOL_WINDOWS):
            cs = slice(g * POOL_GROUP, (g + 1) * POOL_GROUP)
            pooled = _pooled_group(u_ref, halo, g, w, t_idx)
            mixed = jnp.dot(pooled.astype(BF16), wg_ref[g], preferred_element_type=F32)
            z = z_ref[:, cs]
            y_ref[:, cs] = (mixed * sc_ref[:, cs] * (z * _sigmoid(z))).astype(BF16)

    return pl.pallas_call(
        body, name=name,
        out_shape=jax.ShapeDtypeStruct((S, E), BF16),
        grid=(S // tm,),
        in_specs=[pl.BlockSpec((tm, E), lambda i: (i, 0)), _prev_halo_spec(tm, E, 0),
                  pl.BlockSpec((tm, E), lambda i: (i, 1)),
                  pl.BlockSpec((len(POOL_WINDOWS), POOL_GROUP, POOL_GROUP), lambda i: (0, 0, 0)),
                  pl.BlockSpec((1, E), lambda i: (0, 0))],
        out_specs=pl.BlockSpec((tm, E), lambda i: (i, 0)),
        compiler_params=_cparams("parallel"),
    )(h, h, h, w_grp, scale)


def _pool_bwd1(h, dy, w_grp, scale, *, tm, name):
    S = h.shape[0]
    E = D_INNER

    def body(u_ref, uh_ref, z_ref, dy_ref, wg_ref, sc_ref, pooled_ref, dmixed_ref, dpooled_ref, dz_ref, dsc_ref):
        i = pl.program_id(0)
        halo = jnp.where(i > 0, uh_ref[...], 0.0)
        t_idx = i * tm + lax.broadcasted_iota(jnp.int32, (tm, 1), 0)
        for g, w in enumerate(POOL_WINDOWS):
            cs = slice(g * POOL_GROUP, (g + 1) * POOL_GROUP)
            pooled = _pooled_group(u_ref, halo, g, w, t_idx).astype(BF16)
            wg = wg_ref[g]
            mixed = jnp.dot(pooled, wg, preferred_element_type=F32)
            z = z_ref[:, cs]
            sg = _sigmoid(z)
            dyv = dy_ref[:, cs]
            sc = sc_ref[:, cs]
            dms = dyv * (z * sg)
            dz = dyv * (mixed * sc) * (sg * (1.0 + z * (1.0 - sg)))
            dsc = jnp.sum(dms * mixed, axis=0, keepdims=True)
            dmixed = (dms * sc).astype(BF16)
            dpooled = lax.dot_general(dmixed, wg, (((1,), (1,)), ((), ())), preferred_element_type=F32)
            pooled_ref[:, cs] = pooled
            dmixed_ref[:, cs] = dmixed
            dpooled_ref[:, cs] = dpooled
            dz_ref[:, cs] = dz.astype(BF16)

            @pl.when(i == 0)
            def _():
                dsc_ref[:, cs] = dsc

            @pl.when(i > 0)
            def _():
                dsc_ref[:, cs] += dsc

    row = pl.BlockSpec((tm, E), lambda i: (i, 0))
    vec = pl.BlockSpec((1, E), lambda i: (0, 0))
    return pl.pallas_call(
        body, name=name,
        out_shape=(jax.ShapeDtypeStruct((S, E), BF16), jax.ShapeDtypeStruct((S, E), BF16),
                   jax.ShapeDtypeStruct((S, E), F32), jax.ShapeDtypeStruct((S, E), BF16),
                   jax.ShapeDtypeStruct((1, E), F32)),
        grid=(S // tm,),
        in_specs=[row, _prev_halo_spec(tm, E, 0), pl.BlockSpec((tm, E), lambda i: (i, 1)), row,
                  pl.BlockSpec((len(POOL_WINDOWS), POOL_GROUP, POOL_GROUP), lambda i: (0, 0, 0)), vec],
        out_specs=(row, row, row, row, vec),
        compiler_params=_cparams("arbitrary"),
    )(h, h, h, dy, w_grp, scale)


def _pool_bwd2(dpooled, dz, *, tm, name):
    S = dpooled.shape[0]
    E = D_INNER
    nt = S // tm

    def body(dp_ref, dpn_ref, dz_ref, dh_ref):
        i = pl.program_id(0)
        nxt = jnp.where(i < nt - 1, dpn_ref[...], 0.0)
        t_ext = i * tm + lax.broadcasted_iota(jnp.int32, (tm + HALO, 1), 0)
        for g, w in enumerate(POOL_WINDOWS):
            cs = slice(g * POOL_GROUP, (g + 1) * POOL_GROUP)
            dp = dp_ref[:, cs]
            inv = 1.0 / jnp.minimum(t_ext + 1, w).astype(F32)
            s = jnp.concatenate([dp, nxt[:, cs]], axis=0) * inv
            n = tm + HALO
            k = 1
            while k < w:
                s = s + pltpu.roll(s, n - k, 0)
                k *= 2
            dh_ref[:, cs] = (s[:tm, :] - dp).astype(BF16)
        dh_ref[:, E:] = dz_ref[...]

    return pl.pallas_call(
        body, name=name,
        out_shape=jax.ShapeDtypeStruct((S, 2 * E), BF16),
        grid=(nt,),
        in_specs=[pl.BlockSpec((tm, E), lambda i: (i, 0)), _next_halo_spec(tm, E, 0, S),
                  pl.BlockSpec((tm, E), lambda i: (i, 0))],
        out_specs=pl.BlockSpec((tm, 2 * E), lambda i: (i, 0)),
        compiler_params=_cparams("parallel"),
    )(dpooled, dpooled, dz)


CONV_CHUNK = 512


def _conv_fwd(h, cw, *, tm, name):
    S = h.shape[0]
    E = D_INNER

    def body(b_ref, c_ref, hh_ref, z_ref, ch_ref, hhh_ref, w_ref, y_ref):
        i = pl.program_id(0)
        for j in range(E // CONV_CHUNK):
            cs = slice(j * CONV_CHUNK, (j + 1) * CONV_CHUNK)
            p = c_ref[:, cs] * hh_ref[:, cs]
            ph = jnp.where(i > 0, ch_ref[:, cs] * hhh_ref[:, cs], 0.0)
            ext = jnp.concatenate([ph, p], axis=0)
            conv = w_ref[2:3, cs] * p + w_ref[1:2, cs] * _shift_down(ext, 1) + w_ref[0:1, cs] * _shift_down(ext, 2)
            z = z_ref[:, cs]
            y_ref[:, cs] = (b_ref[:, cs] * conv * (z * _sigmoid(z))).astype(BF16)

    col = lambda c: pl.BlockSpec((tm, E), lambda i: (i, c))
    return pl.pallas_call(
        body, name=name,
        out_shape=jax.ShapeDtypeStruct((S, E), BF16),
        grid=(S // tm,),
        in_specs=[col(0), col(1), col(2), col(3), _prev_halo_spec(tm, E, 1), _prev_halo_spec(tm, E, 2),
                  pl.BlockSpec((8, E), lambda i: (0, 0))],
        out_specs=pl.BlockSpec((tm, E), lambda i: (i, 0)),
        compiler_params=_cparams("parallel"),
    )(h, h, h, h, h, h, cw)


def _conv_bwd(h, dy, cw, *, tm, name):
    S = h.shape[0]
    E = D_INNER
    nt = S // tm

    def body(b_ref, c_ref, hh_ref, z_ref, dy_ref, ch_ref, hhh_ref, bn_ref, zn_ref, dyn_ref, w_ref, dh_ref, dw_ref):
        i = pl.program_id(0)
        for j in range(E // CONV_CHUNK):
            cs = slice(j * CONV_CHUNK, (j + 1) * CONV_CHUNK)
            w0, w1, w2 = w_ref[0:1, cs], w_ref[1:2, cs], w_ref[2:3, cs]
            c, hh, b, z, dyv = c_ref[:, cs], hh_ref[:, cs], b_ref[:, cs], z_ref[:, cs], dy_ref[:, cs]
            p = c * hh
            ph = jnp.where(i > 0, ch_ref[:, cs] * hhh_ref[:, cs], 0.0)
            ext = jnp.concatenate([ph, p], axis=0)
            pm1 = _shift_down(ext, 1)
            pm2 = _shift_down(ext, 2)
            conv = w2 * p + w1 * pm1 + w0 * pm2
            sg = _sigmoid(z)
            dy0 = dyv * (z * sg)
            dz = dyv * (b * conv) * (sg * (1.0 + z * (1.0 - sg)))
            db = dy0 * conv
            dconv = dy0 * b
            zn = zn_ref[:, cs]
            dconv_n = jnp.where(i < nt - 1, dyn_ref[:, cs] * (zn * _sigmoid(zn)) * bn_ref[:, cs], 0.0)
            dext = jnp.concatenate([dconv, dconv_n], axis=0)
            dp = w2 * dconv + w1 * _shift_up(dext, 1, tm) + w0 * _shift_up(dext, 2, tm)
            dh_ref[:, 0 * E + j * CONV_CHUNK:0 * E + (j + 1) * CONV_CHUNK] = db.astype(BF16)
            dh_ref[:, 1 * E + j * CONV_CHUNK:1 * E + (j + 1) * CONV_CHUNK] = (dp * hh).astype(BF16)
            dh_ref[:, 2 * E + j * CONV_CHUNK:2 * E + (j + 1) * CONV_CHUNK] = (dp * c).astype(BF16)
            dh_ref[:, 3 * E + j * CONV_CHUNK:3 * E + (j + 1) * CONV_CHUNK] = dz.astype(BF16)
            dw = jnp.concatenate([jnp.sum(dconv * pm2, axis=0, keepdims=True),
                                  jnp.sum(dconv * pm1, axis=0, keepdims=True),
                                  jnp.sum(dconv * p, axis=0, keepdims=True),
                                  jnp.zeros((5, CONV_CHUNK), F32)], axis=0)

            @pl.when(i == 0)
            def _():
                dw_ref[:, cs] = dw

            @pl.when(i > 0)
            def _():
                dw_ref[:, cs] += dw

    col = lambda c: pl.BlockSpec((tm, E), lambda i: (i, c))
    return pl.pallas_call(
        body, name=name,
        out_shape=(jax.ShapeDtypeStruct((S, 4 * E), BF16), jax.ShapeDtypeStruct((8, E), F32)),
        grid=(nt,),
        in_specs=[col(0), col(1), col(2), col(3), pl.BlockSpec((tm, E), lambda i: (i, 0)),
                  _prev_halo_spec(tm, E, 1), _prev_halo_spec(tm, E, 2),
                  _next_halo_spec(tm, E, 0, S), _next_halo_spec(tm, E, 3, S), _next_halo_spec(tm, E, 0, S),
                  pl.BlockSpec((8, E), lambda i: (0, 0))],
        out_specs=(pl.BlockSpec((tm, 4 * E), lambda i: (i, 0)), pl.BlockSpec((8, E), lambda i: (0, 0))),
        compiler_params=_cparams("arbitrary"),
    )(h, h, h, h, dy, h, h, h, h, dy, cw)


Z_COLS = D_INNER // LANES
KV_LAT_BLK = D_INNER // KV_LORA
Q_LAT_BLK = (D_INNER + KV_LORA) // Q_LORA
K_ROPE_BLK = (D_INNER + KV_LORA + Q_LORA) // LANES


def _rope(blk, c, s1, s2):
    return blk * c + pltpu.roll(blk, LANES - QK_ROPE // 2, 1) * s1 + pltpu.roll(blk, QK_ROPE // 2, 1) * s2


def _unrope(blk, c, s1, s2):
    return blk * c - pltpu.roll(blk, LANES - QK_ROPE // 2, 1) * s1 - pltpu.roll(blk, QK_ROPE // 2, 1) * s2


def _lat_norm(v, g):
    rstd = lax.rsqrt(jnp.mean(v * v, axis=-1, keepdims=True) + NORM_EPS)
    return v * rstd * g


def _mla_latent_fwd(h, q_norm, kv_norm, tabs, *, tm, name):
    S = h.shape[0]

    def body(kv_ref, q_ref, kr_ref, qg_ref, kvg_ref, c_ref, s1_ref, s2_ref, qn_ref, kvn_ref, krr_ref):
        qn_ref[...] = _lat_norm(q_ref[...], qg_ref[...]).astype(BF16)
        kvn_ref[...] = _lat_norm(kv_ref[...], kvg_ref[...]).astype(BF16)
        krr_ref[...] = _rope(kr_ref[...], c_ref[...], s1_ref[...], s2_ref[...]).astype(BF16)

    tab = pl.BlockSpec((tm, LANES), lambda i: (i, 0))
    return pl.pallas_call(
        body, name=name,
        out_shape=(jax.ShapeDtypeStruct((S, Q_LORA), BF16), jax.ShapeDtypeStruct((S, KV_LORA), BF16),
                   jax.ShapeDtypeStruct((S, LANES), BF16)),
        grid=(S // tm,),
        in_specs=[pl.BlockSpec((tm, KV_LORA), lambda i: (i, KV_LAT_BLK)), pl.BlockSpec((tm, Q_LORA), lambda i: (i, Q_LAT_BLK)),
                  pl.BlockSpec((tm, LANES), lambda i: (i, K_ROPE_BLK)),
                  pl.BlockSpec((1, Q_LORA), lambda i: (0, 0)), pl.BlockSpec((1, KV_LORA), lambda i: (0, 0)), tab, tab, tab],
        out_specs=(pl.BlockSpec((tm, Q_LORA), lambda i: (i, 0)), pl.BlockSpec((tm, KV_LORA), lambda i: (i, 0)), tab),
        compiler_params=_cparams("parallel"),
    )(h, h, h, q_norm, kv_norm, *tabs)


def _mla_q_up(q_n, w_q_pad, tabs, *, tm, name):
    S = q_n.shape[0]

    def body(a_ref, w_ref, c_ref, s1_ref, s2_ref, o_ref):
        a = a_ref[...]
        for hd in range(N_HEADS):
            acc = jnp.dot(a, w_ref[:, hd * HEAD_PAD:(hd + 1) * HEAD_PAD], preferred_element_type=F32)
            o_ref[hd, :, :QK_NOPE] = acc[:, :QK_NOPE].astype(BF16)
            o_ref[hd, :, QK_NOPE:] = _rope(acc[:, QK_NOPE:], c_ref[...], s1_ref[...], s2_ref[...]).astype(BF16)

    tab = pl.BlockSpec((tm, LANES), lambda i: (i, 0))
    return pl.pallas_call(
        body, name=name,
        out_shape=jax.ShapeDtypeStruct((N_HEADS, S, HEAD_PAD), BF16),
        grid=(S // tm,),
        in_specs=[pl.BlockSpec((tm, Q_LORA), lambda i: (i, 0)), pl.BlockSpec((Q_LORA, N_HEADS * HEAD_PAD), lambda i: (0, 0)),
                  tab, tab, tab],
        out_specs=pl.BlockSpec((N_HEADS, tm, HEAD_PAD), lambda i: (0, i, 0)),
        compiler_params=_cparams("parallel"),
    )(q_n, w_q_pad, *tabs)


def _mla_kv_up(kv_n, w_kv, krr, *, tm, name):
    S = kv_n.shape[0]

    def body(a_ref, w_ref, krr_ref, k_ref, v_ref):
        a = a_ref[...]
        ones = jnp.ones((tm, V_DIM), BF16)
        for hd in range(N_HEADS):
            acc = jnp.dot(a, w_ref[:, hd * HEAD_PAD:(hd + 1) * HEAD_PAD], preferred_element_type=F32)
            k_ref[hd, :, :QK_NOPE] = acc[:, :QK_NOPE].astype(BF16)
            k_ref[hd, :, QK_NOPE:] = krr_ref[...]
            v_ref[hd, :, :V_DIM] = acc[:, QK_NOPE:].astype(BF16)
            v_ref[hd, :, V_DIM:] = ones

    head_blk = pl.BlockSpec((N_HEADS, tm, HEAD_PAD), lambda i: (0, i, 0))
    sds = jax.ShapeDtypeStruct((N_HEADS, S, HEAD_PAD), BF16)
    return pl.pallas_call(
        body, name=name, out_shape=(sds, sds),
        grid=(S // tm,),
        in_specs=[pl.BlockSpec((tm, KV_LORA), lambda i: (i, 0)), pl.BlockSpec((KV_LORA, N_HEADS * HEAD_PAD), lambda i: (0, 0)),
                  pl.BlockSpec((tm, LANES), lambda i: (i, 0))],
        out_specs=(head_blk, head_blk),
        compiler_params=_cparams("parallel"),
    )(kv_n, w_kv, krr)


LOG2E = 1.4426950408889634
SCORE_TO_LOG2 = ATTN_SCALE * LOG2E


def _flash_fwd(q_full, k_full, v_aug, h, *, tq, name):
    H, S, _ = q_full.shape
    tk = tq
    HP = 2

    def body(q_ref, k_ref, v_ref, z_ref, o_ref, y_ref, lse_ref, m_sc, acc_sc):
        qi = pl.program_id(1)
        m_sc[...] = jnp.full(m_sc.shape, -1e30, F32)
        acc_sc[...] = jnp.zeros(acc_sc.shape, F32)

        def chunk(j, diag):
            off = pl.multiple_of(j * tk, tk)
            for hh in range(HP):
                kj = k_ref[hh, pl.ds(off, tk), :]
                vj = v_ref[hh, pl.ds(off, tk), :]
                s = lax.dot_general(q_ref[hh], kj, (((1,), (1,)), ((), ())), preferred_element_type=F32) * SCORE_TO_LOG2
                if diag:
                    keep = lax.broadcasted_iota(jnp.int32, (tq, tk), 1) <= lax.broadcasted_iota(jnp.int32, (tq, tk), 0)
                    s = jnp.where(keep, s, -1e30)
                m_old = m_sc[hh]
                m_new = jnp.maximum(m_old, jnp.max(s, axis=1, keepdims=True))
                p = jnp.exp2(s - jnp.tile(m_new, (1, tk // LANES)))
                alpha = jnp.exp2(m_old - m_new)
                acc_sc[hh] = jnp.tile(alpha, (1, 2)) * acc_sc[hh] + jnp.dot(p.astype(BF16), vj, preferred_element_type=F32)
                m_sc[hh] = m_new

        def step(j, carry):
            chunk(j, False)
            return carry

        lax.fori_loop(0, qi, step, 0)
        chunk(qi, True)
        for hh in range(HP):
            cs = slice(hh * V_DIM, (hh + 1) * V_DIM)
            l = acc_sc[hh, :, V_DIM:]
            o = acc_sc[hh, :, :V_DIM] / l
            z = z_ref[:, cs]
            o_ref[:, cs] = o
            y_ref[:, cs] = (o * (z * _sigmoid(z))).astype(BF16)
            lse_ref[hh] = (m_sc[hh] + jnp.log2(l))[:, :1]

    pair = pl.BlockSpec((tq, HP * V_DIM), lambda hd, i: (i, hd))
    return pl.pallas_call(
        body, name=name,
        out_shape=(jax.ShapeDtypeStruct((S, D_INNER), F32), jax.ShapeDtypeStruct((S, D_INNER), BF16),
                   jax.ShapeDtypeStruct((H, S, 1), F32)),
        grid=(H // HP, S // tq),
        in_specs=[pl.BlockSpec((HP, tq, HEAD_PAD), lambda hd, i: (hd, i, 0)),
                  pl.BlockSpec((HP, S, HEAD_PAD), lambda hd, i: (hd, 0, 0)),
                  pl.BlockSpec((HP, S, HEAD_PAD), lambda hd, i: (hd, 0, 0)),
                  pair],
        out_specs=(pair, pair, pl.BlockSpec((HP, tq, 1), lambda hd, i: (hd, i, 0))),
        scratch_shapes=[pltpu.VMEM((HP, tq, LANES), F32), pltpu.VMEM((HP, tq, HEAD_PAD), F32)],
        compiler_params=_cparams("parallel", "parallel"),
    )(q_full, k_full, v_aug, h)


def _mla_gate_bwd(dy, o, h, *, tm, name):
    S = dy.shape[0]
    E = D_INNER

    def body(dy_ref, o_ref, z_ref, do_ref, dz_ref, delta_ref):
        for hd in range(N_HEADS):
            cs = slice(hd * V_DIM, (hd + 1) * V_DIM)
            z = z_ref[:, cs]
            sg = _sigmoid(z)
            dyv = dy_ref[:, cs]
            ov = o_ref[:, cs]
            do = dyv * (z * sg)
            do_ref[:, cs] = do.astype(BF16)
            dz_ref[:, cs] = (dyv * ov * (sg * (1.0 + z * (1.0 - sg)))).astype(BF16)
            delta_ref[hd] = jnp.sum(do * ov, axis=-1, keepdims=True)

    row = pl.BlockSpec((tm, E), lambda i: (i, 0))
    return pl.pallas_call(
        body, name=name,
        out_shape=(jax.ShapeDtypeStruct((S, E), BF16), jax.ShapeDtypeStruct((S, E), BF16),
                   jax.ShapeDtypeStruct((N_HEADS, S, 1), F32)),
        grid=(S // tm,),
        in_specs=[row, row, row],
        out_specs=(row, row, pl.BlockSpec((N_HEADS, tm, 1), lambda i: (0, i, 0))),
        compiler_params=_cparams("parallel"),
    )(dy, o, h)


def _flash_bwd(q_full, k_full, v_aug, do, lse_rows, delta_rows, *, tq, name):
    H, S, _ = q_full.shape
    tk = tq
    nq = S // tq
    HP = 2

    def body(q_ref, k_ref, v_ref, do_ref, lse_ref, dl_ref, dq_ref, dkv_ref, dkr_ref, dk_sc, dv_sc):
        kj = pl.program_id(1)

        @pl.when(kj == 0)
        def _():
            dq_ref[...] = jnp.zeros(dq_ref.shape, F32)

        dk_sc[...] = jnp.zeros(dk_sc.shape, F32)
        dv_sc[...] = jnp.zeros(dv_sc.shape, F32)

        def chunk(qi, diag):
            off = pl.multiple_of(qi * tq, tq)
            for hh in range(HP):
                k = k_ref[hh]
                q = q_ref[hh, pl.ds(off, tq), :]
                dov = do_ref[pl.ds(off, tq), hh * V_DIM:(hh + 1) * V_DIM]
                s_t = lax.dot_general(k, q, (((1,), (1,)), ((), ())), preferred_element_type=F32) * SCORE_TO_LOG2
                p_t = jnp.exp2(s_t - lse_ref[hh, qi])
                if diag:
                    keep = lax.broadcasted_iota(jnp.int32, (tk, tq), 0) <= lax.broadcasted_iota(jnp.int32, (tk, tq), 1)
                    p_t = jnp.where(keep, p_t, 0.0)
                dv_sc[hh] += jnp.dot(p_t.astype(BF16), dov, preferred_element_type=F32)
                dp_t = lax.dot_general(v_ref[hh], dov, (((1,), (1,)), ((), ())), preferred_element_type=F32)
                ds = (p_t * (dp_t - dl_ref[hh, qi])).astype(BF16)
                dk_sc[hh] += jnp.dot(ds, q, preferred_element_type=F32)
                dq_ref[pl.ds(off, tq), hh * HEAD_PAD:(hh + 1) * HEAD_PAD] += lax.dot_general(
                    ds, k, (((0,), (0,)), ((), ())), preferred_element_type=F32)

        def step(qi, carry):
            chunk(qi, False)
            return carry

        chunk(kj, True)
        lax.fori_loop(kj + 1, nq, step, 0)
        for hh in range(HP):
            lo = hh * HEAD_PAD
            dkv_ref[:, lo:lo + QK_NOPE] = (dk_sc[hh, :, :QK_NOPE] * ATTN_SCALE).astype(BF16)
            dkv_ref[:, lo + QK_NOPE:lo + HEAD_PAD] = dv_sc[hh].astype(BF16)
            dkr_ref[hh] = dk_sc[hh, :, QK_NOPE:] * ATTN_SCALE

    return pl.pallas_call(
        body, name=name,
        out_shape=(jax.ShapeDtypeStruct((S, H * HEAD_PAD), F32), jax.ShapeDtypeStruct((S, H * HEAD_PAD), BF16),
                   jax.ShapeDtypeStruct((H, S, LANES), F32)),
        grid=(H // HP, S // tk),
        in_specs=[pl.BlockSpec((HP, S, HEAD_PAD), lambda hd, j: (hd, 0, 0)),
                  pl.BlockSpec((HP, tk, HEAD_PAD), lambda hd, j: (hd, j, 0)),
                  pl.BlockSpec((HP, tk, V_DIM), lambda hd, j: (hd, j, 0)),
                  pl.BlockSpec((S, HP * V_DIM), lambda hd, j: (0, hd)),
                  pl.BlockSpec((HP, nq, 1, tq), lambda hd, j: (hd, 0, 0, 0)),
                  pl.BlockSpec((HP, nq, 1, tq), lambda hd, j: (hd, 0, 0, 0))],
        out_specs=(pl.BlockSpec((S, HP * HEAD_PAD), lambda hd, j: (0, hd)), pl.BlockSpec((tk, HP * HEAD_PAD), lambda hd, j: (j, hd)),
                   pl.BlockSpec((HP, tk, LANES), lambda hd, j: (hd, j, 0))),
        scratch_shapes=[pltpu.VMEM((HP, tk, HEAD_PAD), F32), pltpu.VMEM((HP, tk, V_DIM), F32)],
        compiler_params=_cparams("parallel", "arbitrary"),
    )(q_full, k_full, v_aug, do, lse_rows, delta_rows)


def _mla_unrope_q(dq, tabs, *, tm, name):
    S, W = dq.shape

    def body(dq_ref, c_ref, s1_ref, s2_ref, o_ref):
        for hd in range(N_HEADS):
            lo = hd * HEAD_PAD
            o_ref[:, lo:lo + QK_NOPE] = (dq_ref[:, lo:lo + QK_NOPE] * ATTN_SCALE).astype(BF16)
            o_ref[:, lo + QK_NOPE:lo + HEAD_PAD] = _unrope(dq_ref[:, lo + QK_NOPE:lo + HEAD_PAD] * ATTN_SCALE, c_ref[...], s1_ref[...], s2_ref[...]).astype(BF16)

    tab = pl.BlockSpec((tm, LANES), lambda i: (i, 0))
    row = pl.BlockSpec((tm, W), lambda i: (i, 0))
    return pl.pallas_call(
        body, name=name,
        out_shape=jax.ShapeDtypeStruct((S, W), BF16),
        grid=(S // tm,),
        in_specs=[row, tab, tab, tab], out_specs=row,
        compiler_params=_cparams("parallel"),
    )(dq, *tabs)


def _mla_latent_bwd(h, dq_n, dkv_n, dkr, dz, q_norm, kv_norm, tabs, *, tm, name):
    S = h.shape[0]

    def body(kv_ref, q_ref, dqn_ref, dkvn_ref, dkr_ref, dz_ref, qg_ref, kvg_ref, c_ref, s1_ref, s2_ref, dh_ref, dqg_ref, dkvg_ref):
        i = pl.program_id(0)
        dq_lat, dqg = _rms_bwd_math(q_ref[...], qg_ref[...], dqn_ref[...])
        dkv_lat, dkvg = _rms_bwd_math(kv_ref[...], kvg_ref[...], dkvn_ref[...])
        dkr_sum = dkr_ref[0]
        for hd in range(1, N_HEADS):
            dkr_sum = dkr_sum + dkr_ref[hd]
        dh_ref[:, :D_INNER] = dz_ref[...]
        dh_ref[:, D_INNER:D_INNER + KV_LORA] = dkv_lat.astype(BF16)
        dh_ref[:, D_INNER + KV_LORA:D_INNER + KV_LORA + Q_LORA] = dq_lat.astype(BF16)
        dh_ref[:, D_INNER + KV_LORA + Q_LORA:] = _unrope(dkr_sum, c_ref[...], s1_ref[...], s2_ref[...]).astype(BF16)

        @pl.when(i == 0)
        def _():
            dqg_ref[...] = dqg
            dkvg_ref[...] = dkvg

        @pl.when(i > 0)
        def _():
            dqg_ref[...] += dqg
            dkvg_ref[...] += dkvg

    tab = pl.BlockSpec((tm, LANES), lambda i: (i, 0))
    qvec = pl.BlockSpec((1, Q_LORA), lambda i: (0, 0))
    kvvec = pl.BlockSpec((1, KV_LORA), lambda i: (0, 0))
    return pl.pallas_call(
        body, name=name,
        out_shape=(jax.ShapeDtypeStruct((S, MLA_IN_PAD), BF16), jax.ShapeDtypeStruct((1, Q_LORA), F32),
                   jax.ShapeDtypeStruct((1, KV_LORA), F32)),
        grid=(S // tm,),
        in_specs=[pl.BlockSpec((tm, KV_LORA), lambda i: (i, KV_LAT_BLK)), pl.BlockSpec((tm, Q_LORA), lambda i: (i, Q_LAT_BLK)),
                  pl.BlockSpec((tm, Q_LORA), lambda i: (i, 0)), pl.BlockSpec((tm, KV_LORA), lambda i: (i, 0)),
                  pl.BlockSpec((N_HEADS, tm, LANES), lambda i: (0, i, 0)), pl.BlockSpec((tm, D_INNER), lambda i: (i, 0)),
                  qvec, kvvec, tab, tab, tab],
        out_specs=(pl.BlockSpec((tm, MLA_IN_PAD), lambda i: (i, 0)), qvec, kvvec),
        compiler_params=_cparams("arbitrary"),
    )(h, h, dq_n, dkv_n, dkr, dz, q_norm, kv_norm, *tabs)


def _adamw(w, g, m, v, *, name):
    R, C = w.shape
    tr = R
    for cand in (512, 256, 128, 64, 32, 16, 8):
        if R % cand == 0 and cand * C * 4 <= 2 * 1024 * 1024:
            tr = cand
            break

    def body(w_ref, g_ref, m_ref, v_ref, d_ref, nm_ref, nv_ref):
        d_ref[...], nm_ref[...], nv_ref[...] = _adam_math(w_ref[...], g_ref[...], m_ref[...], v_ref[...])

    spec = pl.BlockSpec((tr, C), lambda i: (i, 0))
    sds = jax.ShapeDtypeStruct((R, C), F32)
    return pl.pallas_call(
        body, name=name, out_shape=(sds, sds, sds), grid=(R // tr,),
        in_specs=[spec] * 4, out_specs=(spec,) * 3,
        compiler_params=_cparams("parallel"),
    )(w, g, m, v)


HBM_SPEC = pl.BlockSpec(memory_space=pltpu.HBM)
VMEM_SPEC = pl.BlockSpec(memory_space=pltpu.VMEM)
SEM_SPEC = pl.BlockSpec(memory_space=pltpu.SEMAPHORE)
ANY_SPEC = pl.BlockSpec(memory_space=pl.ANY)
SPLIT_EFFECT = pltpu.SideEffectType.DATAFLOW_SIDE_EFFECTING


def _place():
    return lax.axis_index("x"), lax.axis_index("y"), lax.axis_index("c")


def _other_chips(x, y):
    return [(1 - x, y), (x, 1 - y), (1 - x, 1 - y)]


def _remote(src, dst, send_sem, recv_sem, dev):
    return pltpu.make_async_remote_copy(src_ref=src, dst_ref=dst, send_sem=send_sem, recv_sem=recv_sem,
                                        device_id=dev, device_id_type=MESH)


def _all_gather_big(wp, *, name):
    R, C = wp.shape
    H = R // 2

    def body(w_ref, out_ref, send_sems, recv_sems):
        x, y, c = _place()
        sib = (x, y, 1 - c)
        chips = _other_chips(x, y)

        def blk(px, py, half):
            return out_ref.at[2 * px + py, pl.ds(pl.multiple_of(half * H, 16), H), :]

        my_half = w_ref.at[pl.ds(pl.multiple_of(c * H, 16), H), :]
        first = [_remote(my_half, blk(x, y, c), send_sems.at[r], recv_sems.at[r], (*chip, c)) for r, chip in enumerate(chips)]
        for cp in first:
            cp.start()
        passed = [_remote(blk(*chip, c), blk(*chip, c), send_sems.at[3 + r], recv_sems.at[3 + r], sib) for r, chip in enumerate(chips)]
        for r, chip in enumerate(chips):
            _remote(my_half, blk(*chip, c), send_sems.at[r], recv_sems.at[r], (*chip, c)).wait_recv()
            passed[r].start()
        for r, chip in enumerate(chips):
            _remote(my_half, blk(*chip, 1 - c), send_sems.at[3 + r], recv_sems.at[3 + r], sib).wait_recv()
        for cp in first + passed:
            cp.wait_send()

    return pl.pallas_call(
        body, name=name,
        out_shape=jax.ShapeDtypeStruct((N_CHIPS, R, C), wp.dtype),
        in_specs=[HBM_SPEC], out_specs=HBM_SPEC,
        scratch_shapes=[pltpu.SemaphoreType.DMA((6,)), pltpu.SemaphoreType.DMA((6,))],
    )(wp)


def _ag_ici_start(wp, after, *, name):
    R, C = wp.shape
    H = R // 2

    def body(w_ref, land_ref, after_ref, send_sems, recv_sems, w_thru, land_thru, token):
        x, y, c = _place()
        rows = pl.ds(pl.multiple_of(c * H, 16), H)
        for r, chip in enumerate(_other_chips(x, y)):
            _remote(w_ref.at[rows, :], land_ref.at[2 * x + y, rows, :], send_sems.at[r], recv_sems.at[r], (*chip, c)).start()
        token[...] = jnp.zeros(token.shape, F32)

    land = lax.empty((N_CHIPS, R, C), wp.dtype)
    return pl.pallas_call(
        body, name=name,
        out_shape=(pltpu.SemaphoreType.DMA((3,)), pltpu.SemaphoreType.DMA((3,)), pltpu.HBM(wp.shape, wp.dtype), pltpu.HBM(land.shape, land.dtype),
                   jax.ShapeDtypeStruct((8, LANES), F32)),
        in_specs=(HBM_SPEC, HBM_SPEC, ANY_SPEC), out_specs=(SEM_SPEC, SEM_SPEC, HBM_SPEC, HBM_SPEC, VMEM_SPEC),
        input_output_aliases={0: 2, 1: 3},
        compiler_params=pltpu.CompilerParams(has_side_effects=SPLIT_EFFECT),
    )(pltpu.with_memory_space_constraint(wp, pltpu.HBM), pltpu.with_memory_space_constraint(land, pltpu.HBM), after)


def _ag_ici_wait(send_sems, recv_sems, w_thru, land_thru, after, *, name):
    R, C = w_thru.shape
    H = R // 2

    def body(w_ref, land_ref, send_sems, recv_sems, after_ref, w_out, land_out):
        x, y, c = _place()
        rows = pl.ds(pl.multiple_of(c * H, 16), H)
        for r, (px, py) in enumerate(_other_chips(x, y)):
            cp = _remote(w_ref.at[rows, :], land_ref.at[2 * px + py, rows, :], send_sems.at[r], recv_sems.at[r], (px, py, c))
            cp.wait_send()
            cp.wait_recv()

    return pl.pallas_call(
        body, name=name,
        out_shape=(pltpu.HBM(w_thru.shape, w_thru.dtype), pltpu.HBM(land_thru.shape, land_thru.dtype)),
        in_specs=(HBM_SPEC, HBM_SPEC, SEM_SPEC, SEM_SPEC, ANY_SPEC), out_specs=(HBM_SPEC, HBM_SPEC),
        input_output_aliases={0: 0, 1: 1},
        compiler_params=pltpu.CompilerParams(has_side_effects=SPLIT_EFFECT),
    )(w_thru, land_thru, send_sems, recv_sems, after)


def _ag_sibling_forward(land, *, name):
    _, R, C = land.shape
    H = R // 2

    def body(land_ref, out_ref, send_sems, recv_sems):
        x, y, c = _place()
        sib = (x, y, 1 - c)
        mine = pl.ds(pl.multiple_of(c * H, 16), H)
        theirs = pl.ds(pl.multiple_of((1 - c) * H, 16), H)
        chips = _other_chips(x, y)
        sends = [_remote(land_ref.at[2 * px + py, mine, :], out_ref.at[2 * px + py, mine, :], send_sems.at[r], recv_sems.at[r], sib)
                 for r, (px, py) in enumerate(chips)]
        for cp in sends:
            cp.start()
        for r, (px, py) in enumerate(chips):
            _remote(land_ref.at[2 * px + py, mine, :], out_ref.at[2 * px + py, theirs, :], send_sems.at[r], recv_sems.at[r], sib).wait_recv()
        for cp in sends:
            cp.wait_send()

    return pl.pallas_call(
        body, name=name, out_shape=jax.ShapeDtypeStruct(land.shape, land.dtype),
        in_specs=[HBM_SPEC], out_specs=HBM_SPEC, input_output_aliases={0: 0},
        scratch_shapes=[pltpu.SemaphoreType.DMA((3,)), pltpu.SemaphoreType.DMA((3,))],
    )(land)


def _rs_sibling_swap(g, *, name):
    _, R, C = g.shape
    H = R // 2

    def body(g_ref, theirs_ref, send_sems, recv_sems):
        x, y, c = _place()
        sib = (x, y, 1 - c)
        copies = [_remote(g_ref.at[k, pl.ds(pl.multiple_of((1 - c) * H, 16), H), :], theirs_ref.at[k],
                          send_sems.at[k], recv_sems.at[k], sib) for k in range(N_CHIPS)]
        for cp in copies:
            cp.start()
        for cp in copies:
            cp.wait()

    return pl.pallas_call(
        body, name=name, out_shape=jax.ShapeDtypeStruct((N_CHIPS, H, C), g.dtype),
        in_specs=[HBM_SPEC], out_specs=HBM_SPEC,
        scratch_shapes=[pltpu.SemaphoreType.DMA((N_CHIPS,)), pltpu.SemaphoreType.DMA((N_CHIPS,))],
    )(g)


def _row_tile(h):
    best = 16
    for d in range(16, 1025, 16):
        if h % d == 0:
            best = d
    return best


def _add2_bf16(g, theirs, core, *, name):
    K, H, C = theirs.shape
    tr = _row_tile(H)
    nb = H // tr

    def body(c_ref, a_ref, b_ref, o_ref):
        o_ref[...] = (a_ref[...].astype(F32) + b_ref[...].astype(F32)).astype(o_ref.dtype)

    spec = pl.BlockSpec((None, tr, C), lambda k, i, c: (k, i, 0))
    return pl.pallas_call(
        body, name=name, out_shape=jax.ShapeDtypeStruct((K, H, C), theirs.dtype),
        grid_spec=pltpu.PrefetchScalarGridSpec(
            num_scalar_prefetch=1, grid=(K, nb),
            in_specs=[pl.BlockSpec((None, tr, C), lambda k, i, c: (k, c[0] * nb + i, 0)), spec], out_specs=spec),
        compiler_params=_cparams("parallel", "parallel"),
    )(core, g, theirs)


def _rs_chip_exchange_start(p, *, name):
    _, H, C = p.shape

    def body(p_ref, land_ref, send_sems, recv_sems, p_thru, land_thru, token):
        x, y, c = _place()
        for r, (px, py) in enumerate(_other_chips(x, y)):
            _remote(p_ref.at[2 * px + py], land_ref.at[r], send_sems.at[r], recv_sems.at[r], (px, py, c)).start()
        token[...] = jnp.zeros(token.shape, F32)

    land = lax.empty((3, H, C), p.dtype)
    return pl.pallas_call(
        body, name=name,
        out_shape=(pltpu.SemaphoreType.DMA((3,)), pltpu.SemaphoreType.DMA((3,)), pltpu.HBM(p.shape, p.dtype), pltpu.HBM(land.shape, land.dtype),
                   jax.ShapeDtypeStruct((8, LANES), F32)),
        in_specs=(HBM_SPEC, HBM_SPEC), out_specs=(SEM_SPEC, SEM_SPEC, HBM_SPEC, HBM_SPEC, VMEM_SPEC),
        input_output_aliases={0: 2, 1: 3},
        compiler_params=pltpu.CompilerParams(has_side_effects=SPLIT_EFFECT),
    )(pltpu.with_memory_space_constraint(p, pltpu.HBM), pltpu.with_memory_space_constraint(land, pltpu.HBM))


def _rs_chip_exchange_wait(send_sems, recv_sems, p_thru, land_thru, after, *, name):
    def body(p_ref, land_ref, send_sems, recv_sems, after_ref, p_out, land_out):
        x, y, c = _place()
        for r, (px, py) in enumerate(_other_chips(x, y)):
            cp = _remote(p_ref.at[2 * px + py], land_ref.at[r], send_sems.at[r], recv_sems.at[r], (px, py, c))
            cp.wait_send()
            cp.wait_recv()

    return pl.pallas_call(
        body, name=name,
        out_shape=(pltpu.HBM(p_thru.shape, p_thru.dtype), pltpu.HBM(land_thru.shape, land_thru.dtype)),
        in_specs=(HBM_SPEC, HBM_SPEC, SEM_SPEC, SEM_SPEC, ANY_SPEC), out_specs=(HBM_SPEC, HBM_SPEC),
        input_output_aliases={0: 0, 1: 1},
        compiler_params=pltpu.CompilerParams(has_side_effects=SPLIT_EFFECT),
    )(p_thru, land_thru, send_sems, recv_sems, after)


def _add4_f32(p, recv, chip_core, *, name):
    _, H, C = p.shape
    tr = _row_tile(H)
    nb = H // tr

    def body(s_ref, o_ref, r_ref, out_ref):
        out_ref[...] = ((o_ref[...].astype(F32) + r_ref[0].astype(F32)) + r_ref[1].astype(F32)) + r_ref[2].astype(F32)

    return pl.pallas_call(
        body, name=name, out_shape=jax.ShapeDtypeStruct((2 * H, C), F32),
        grid_spec=pltpu.PrefetchScalarGridSpec(
            num_scalar_prefetch=1, grid=(nb,),
            in_specs=[pl.BlockSpec((None, tr, C), lambda i, s: (s[0], i, 0)), pl.BlockSpec((3, tr, C), lambda i, s: (0, i, 0))],
            out_specs=pl.BlockSpec((tr, C), lambda i, s: (s[1] * nb + i, 0))),
        compiler_params=_cparams("parallel"),
    )(chip_core, p, recv)


def _rs_sibling_join(f, *, name):
    R, C = f.shape
    H = R // 2

    def body(f_ref, out_ref, send_sem, recv_sem):
        x, y, c = _place()
        sib = (x, y, 1 - c)
        mine = pl.ds(pl.multiple_of(c * H, 8), H)
        theirs = pl.ds(pl.multiple_of((1 - c) * H, 8), H)
        cp = _remote(f_ref.at[mine, :], out_ref.at[mine, :], send_sem, recv_sem, sib)
        cp.start()
        _remote(f_ref.at[mine, :], out_ref.at[theirs, :], send_sem, recv_sem, sib).wait_recv()
        cp.wait_send()

    return pl.pallas_call(
        body, name=name, out_shape=jax.ShapeDtypeStruct((R, C), f.dtype),
        in_specs=[HBM_SPEC], out_specs=HBM_SPEC, input_output_aliases={0: 0},
        scratch_shapes=[pltpu.SemaphoreType.DMA, pltpu.SemaphoreType.DMA],
    )(f)


SMALL_GATHER = (("pool_norm", 2, 256), ("pool_scale", 2, 512), ("conv_w", 3, 512), ("mla_norm", 1, 256),
                ("mla_q_norm", 1, 96), ("mla_kv_norm", 1, 64))
SMALL_SLOT = (16, 512)


def _gather_small(shards, *, name):
    def body(pn_ref, ps_ref, cw_ref, mn_ref, qn_ref, kn_ref, pn_o, ps_o, cw_o, mn_o, qn_o, kn_o, all_ref, send_sems, recv_sems):
        x, y, c = _place()
        mine = 2 * x + y
        all_ref[mine] = jnp.zeros(SMALL_SLOT, F32)
        all_ref[mine, 0:2, 0:256] = pn_ref[...]
        all_ref[mine, 2:4, :] = ps_ref[...]
        all_ref[mine, 4:7, :] = cw_ref[0]
        all_ref[mine, 7:8, 0:256] = mn_ref[...]
        all_ref[mine, 8:9, 0:96] = qn_ref[...]
        all_ref[mine, 9:10, 0:64] = kn_ref[...]
        chips = _other_chips(x, y)
        sends = [_remote(all_ref.at[mine], all_ref.at[mine], send_sems.at[r], recv_sems.at[r], (*chip, c)) for r, chip in enumerate(chips)]
        for cp in sends:
            cp.start()
        for r, (px, py) in enumerate(chips):
            _remote(all_ref.at[mine], all_ref.at[2 * px + py], send_sems.at[r], recv_sems.at[r], (px, py, c)).wait_recv()
        for cp in sends:
            cp.wait_send()
        for k in range(N_CHIPS):
            pn_o[:, k * 256:(k + 1) * 256] = all_ref[k, 0:2, 0:256]
            ps_o[:, k * 512:(k + 1) * 512] = all_ref[k, 2:4, :]
            cw_o[0, :, k * 512:(k + 1) * 512] = all_ref[k, 4:7, :]
            mn_o[:, k * 256:(k + 1) * 256] = all_ref[k, 7:8, 0:256]
            qn_o[k] = all_ref[k, 8:9, 0:96]
            kn_o[k] = all_ref[k, 9:10, 0:64]

    sds = lambda *shape: jax.ShapeDtypeStruct(shape, F32)
    out = pl.pallas_call(
        body, name=name,
        out_shape=(sds(2, 1024), sds(2, 2048), sds(1, 3, 2048), sds(1, 1024), sds(N_CHIPS, 1, 96), sds(N_CHIPS, 1, 64)),
        in_specs=[VMEM_SPEC] * 6, out_specs=(VMEM_SPEC,) * 6,
        scratch_shapes=[pltpu.VMEM((N_CHIPS,) + SMALL_SLOT, F32), pltpu.SemaphoreType.DMA((3,)), pltpu.SemaphoreType.DMA((3,))],
    )(*[shards[n] for n, _, _ in SMALL_GATHER])
    full = dict(zip([n for n, _, _ in SMALL_GATHER], out))
    full["mla_q_norm"] = full["mla_q_norm"].reshape(1, Q_LORA)
    full["mla_kv_norm"] = full["mla_kv_norm"].reshape(1, KV_LORA)
    return full


SMALL_REDUCE = (("pool_norm_0", 0, 1, 1024), ("pool_norm_1", 1, 1, 1024), ("pool_scale_0", 2, 1, 2048), ("pool_scale_1", 3, 1, 2048),
                ("conv_norm", 4, 1, 1024), ("mla_norm", 5, 1, 1024), ("mla_q_norm", 6, 1, 384), ("mla_kv_norm", 7, 1, 256),
                ("conv_w", 8, 8, 2048), ("final_norm", 16, 1, 1024))
REDUCE_SLOT = (24, 2048)


def _reduce_small(parts, *, name):
    keys = [k for k, _, _, _ in SMALL_REDUCE]

    def body(*refs):
        ins = dict(zip(keys, refs[:len(keys)]))
        pn_o, ps_o, cn_o, cw_o, mn_o, qn_o, kn_o, fn_o, all_ref, send_sems, recv_sems = refs[len(keys):]
        x, y, c = _place()
        me = 4 * x + 2 * y + c
        all_ref[me] = jnp.zeros(REDUCE_SLOT, F32)
        for k, r0, nr, wd in SMALL_REDUCE:
            all_ref[me, r0:r0 + nr, 0:wd] = ins[k][...]
        peers = []
        for rel in range(1, N_DEV):
            dx, dy, dc = (rel >> 2) & 1, (rel >> 1) & 1, rel & 1
            peers.append((1 - x if dx else x, 1 - y if dy else y, 1 - c if dc else c))
        sends = [_remote(all_ref.at[me], all_ref.at[me], send_sems.at[k], recv_sems.at[k], peer) for k, peer in enumerate(peers)]
        for cp in sends:
            cp.start()
        for k, (px, py, pc) in enumerate(peers):
            _remote(all_ref.at[me], all_ref.at[4 * px + 2 * py + pc], send_sems.at[k], recv_sems.at[k], (px, py, pc)).wait_recv()
        for cp in sends:
            cp.wait_send()

        def total(r0, nr, wd):
            acc = all_ref[0, r0:r0 + nr, 0:wd]
            for d in range(1, N_DEV):
                acc = acc + all_ref[d, r0:r0 + nr, 0:wd]
            return acc

        pn_o[0:1, :] = total(0, 1, 1024)
        pn_o[1:2, :] = total(1, 1, 1024)
        ps_o[0:1, :] = total(2, 1, 2048)
        ps_o[1:2, :] = total(3, 1, 2048)
        cn_o[...] = total(4, 1, 1024)
        mn_o[...] = total(5, 1, 1024)
        qn_o[...] = total(6, 1, Q_LORA)
        kn_o[...] = total(7, 1, KV_LORA)
        cw_o[0] = total(8, 3, 2048)
        fn_o[...] = total(16, 1, 1024)

    sds = lambda *shape: jax.ShapeDtypeStruct(shape, F32)
    out = pl.pallas_call(
        body, name=name,
        out_shape=(sds(2, 1024), sds(2, 2048), sds(1, 1024), sds(1, 3, 2048), sds(1, 1024), sds(1, Q_LORA), sds(1, KV_LORA), sds(1, 1024)),
        in_specs=[VMEM_SPEC] * len(keys), out_specs=(VMEM_SPEC,) * 8,
        scratch_shapes=[pltpu.VMEM((N_DEV,) + REDUCE_SLOT, F32), pltpu.SemaphoreType.DMA((N_DEV - 1,)), pltpu.SemaphoreType.DMA((N_DEV - 1,))],
    )(*[parts[k] for k in keys])
    return dict(zip(("pool_norm", "pool_scale", "conv_norm", "conv_w", "mla_norm", "mla_q_norm", "mla_kv_norm", "final_norm"), out))


def _adam_math(w, g, m, v):
    m_new = ADAM_B1 * m + (1.0 - ADAM_B1) * g
    v_new = ADAM_B2 * v + (1.0 - ADAM_B2) * (g * g)
    m_hat = m_new / (1.0 - ADAM_B1 ** ADAM_STEP)
    v_hat = v_new / (1.0 - ADAM_B2 ** ADAM_STEP)
    return -ADAM_LR * (m_hat / (jnp.sqrt(v_hat) + ADAM_EPS) + ADAM_WD * w), m_new, v_new


def _adamw_small(w, m, v, g_full, chip, *, name):
    shp = {n: w[n].shape for n in SMALL}
    whole = lambda s: pl.BlockSpec(s, lambda i, c: (0,) * len(s))
    g_in, g_specs = {}, {}
    for n in SMALL:
        if not SMALL_SHARDED[n]:
            g_in[n], g_specs[n] = g_full[n].reshape(shp[n]), whole(shp[n])
        elif shp[n][-1] % LANES:
            g_in[n] = g_full[n].reshape(N_CHIPS, 1, shp[n][-1])
            g_specs[n] = pl.BlockSpec((None,) + shp[n], lambda i, c: (c[0], 0, 0))
        else:
            g_in[n] = g_full[n]
            nd = len(shp[n])
            g_specs[n] = pl.BlockSpec(shp[n], lambda i, c, nd=nd: (0,) * (nd - 1) + (c[0],))

    def body(c_ref, *refs):
        k = len(SMALL)
        w_r, m_r, v_r, g_r = refs[0:k], refs[k:2 * k], refs[2 * k:3 * k], refs[3 * k:4 * k]
        go_r, d_r, nm_r, nv_r = refs[4 * k:5 * k], refs[5 * k:6 * k], refs[6 * k:7 * k], refs[7 * k:8 * k]
        for i in range(k):
            gv = g_r[i][...]
            d, m_new, v_new = _adam_math(w_r[i][...], gv, m_r[i][...], v_r[i][...])
            go_r[i][...] = gv
            d_r[i][...] = d
            nm_r[i][...] = m_new
            nv_r[i][...] = v_new

    nat = [whole(shp[n]) for n in SMALL]
    out_sds = tuple(jax.ShapeDtypeStruct(shp[n], F32) for n in SMALL)
    out = pl.pallas_call(
        body, name=name, out_shape=out_sds * 4,
        grid_spec=pltpu.PrefetchScalarGridSpec(
            num_scalar_prefetch=1, grid=(1,),
            in_specs=nat * 3 + [g_specs[n] for n in SMALL], out_specs=tuple(nat) * 4),
        compiler_params=_cparams("arbitrary"),
    )(chip, *[w[n] for n in SMALL], *[m[n] for n in SMALL], *[v[n] for n in SMALL], *[g_in[n] for n in SMALL])
    k = len(SMALL)
    return tuple(dict(zip(SMALL, out[j * k:(j + 1) * k])) for j in range(4))


BIG = ("pool_w_in", "pool_w_grp", "pool_w_out", "conv_w_in", "conv_w_out", "mla_w_in", "mla_w_q_up", "mla_w_kv_up", "mla_w_out")
BIG_SHARD_AXIS = {"pool_w_in": 2, "pool_w_grp": 2, "pool_w_out": 1, "conv_w_in": 2, "conv_w_out": 1,
                  "mla_w_in": 2, "mla_w_q_up": 2, "mla_w_kv_up": 2, "mla_w_out": 1}
GROUPS = {
    "p0": (("pool_w_in", 0), ("pool_w_grp", 0), ("pool_w_out", 0)),
    "cv": (("conv_w_in", 0), ("conv_w_out", 0)),
    "ml": (("mla_w_in", 0), ("mla_w_q_up", 0), ("mla_w_kv_up", 0), ("mla_w_out", 0), ("pool_w_in", 1), ("pool_w_grp", 1), ("pool_w_out", 1)),
}
PACK_ROW_ALIGN = 32
RS_ROW_ALIGN = 512


def _pack_rows(parts, align):
    rows = [p.reshape(-1, PACK_COLS) for p in parts]
    n = sum(r.shape[0] for r in rows)
    pad = (-n) % align
    if pad:
        rows.append(jnp.zeros((pad, PACK_COLS), rows[0].dtype))
    return jnp.concatenate(rows, axis=0)


def _unpack_rows(buf, shapes):
    out, r0 = [], 0
    for shp in shapes:
        n = 1
        for d in shp:
            n *= d
        nr = n // PACK_COLS
        out.append(buf[r0:r0 + nr].reshape(shp))
        r0 += nr
    return out


def _join_shards(gathered, shard_shapes, axes):
    per_chip = [_unpack_rows(gathered[k], shard_shapes) for k in range(N_CHIPS)]
    return [jnp.concatenate([per_chip[k][i] for k in range(N_CHIPS)], axis=ax) for i, ax in enumerate(axes)]


SMALL = ("pool_norm", "pool_scale", "conv_norm", "conv_w", "mla_norm", "mla_q_norm", "mla_kv_norm", "final_norm")
SMALL_SHARDED = {"pool_norm": True, "pool_scale": True, "conv_norm": False, "conv_w": True, "mla_norm": True,
                 "mla_q_norm": True, "mla_kv_norm": True, "final_norm": False}


def _rope_tables(positions):
    inv_freq = ROPE_BASE ** (-jnp.arange(0, QK_ROPE, 2, dtype=F32) / QK_ROPE)
    ang = positions.astype(F32).reshape(-1, 1) * inv_freq
    cos, sin = jnp.cos(ang), jnp.sin(ang)
    z32 = jnp.zeros_like(cos)
    z64 = jnp.concatenate([z32, z32], axis=1)
    return (jnp.concatenate([cos, cos, z64], axis=1), jnp.concatenate([-sin, z32, z64], axis=1),
            jnp.concatenate([z32, sin, z64], axis=1))


def _mla_in_to_padded(w):
    q, kv, kr, z = w[:, :Q_LORA], w[:, Q_LORA:Q_LORA + KV_LORA], w[:, Q_LORA + KV_LORA:Q_LORA + KV_LORA + QK_ROPE], w[:, Q_LORA + KV_LORA + QK_ROPE:]
    return jnp.concatenate([z, kv, q, kr, jnp.zeros((w.shape[0], MLA_IN_PAD - MLA_IN), w.dtype)], axis=1)


def _mla_in_from_padded(w):
    z, kv, q, kr = w[:, :D_INNER], w[:, D_INNER:D_INNER + KV_LORA], w[:, D_INNER + KV_LORA:D_INNER + KV_LORA + Q_LORA], w[:, D_INNER + KV_LORA + Q_LORA:D_INNER + KV_LORA + Q_LORA + QK_ROPE]
    return jnp.concatenate([q, kv, kr, z], axis=1)


def _q_up_to_padded(w):
    k = w.shape[0]
    return jnp.pad(w.reshape(k, N_HEADS, QK_NOPE + QK_ROPE), ((0, 0), (0, 0), (0, HEAD_PAD - QK_NOPE - QK_ROPE))).reshape(k, N_HEADS * HEAD_PAD)


def _q_up_from_padded(w):
    k = w.shape[0]
    return w.reshape(k, N_HEADS, HEAD_PAD)[:, :, :QK_NOPE + QK_ROPE].reshape(k, N_HEADS * (QK_NOPE + QK_ROPE))


def _local_step(x, positions, target, weights_for, ws, grads_ready):
    S = x.shape[0]
    tm = min(512, S)
    te = min(256, S)
    tq = min(512, S)
    tabs = _rope_tables(positions)
    gb, gs = {}, {}

    def mm_in(xn, w, name):
        n = w.shape[1]
        return _mm(xn, w, tm=min(1024, S), tn=_pick(n, 1536 if n == MLA_IN_PAD else 1024), tk=D_MODEL, name=name)

    def mm_out(y, w, res, name):
        return _mm(y, w, residual=res, tm=tm, tn=D_MODEL, tk=D_INNER, name=name)

    def mm_dx(dy, w, name, after=None):
        k, n = w.shape
        return _mm(dy, w, trans_b=True, after=after, tm=min(1024, S), tn=_pick(k, 1024), tk=_pick(n, 1408), name=name)

    def mm_dw(a, b, name, after=None):
        ka, nb = a.shape[1], b.shape[1]
        return _mm(a, b, trans_a=True, out_dtype=BF16, after=after, tm=_pick(ka, 1024), tn=_pick(nb, 1408), tk=tm, name=name)

    def pool_layer_fwd(xin, wts, j, tag):
        xn = _rms_fwd(xin, ws["pool_norm"][j:j + 1], tm=tm, name=f"{tag}_norm")
        h = mm_in(xn, wts[("pool_w_in", j)], f"{tag}_in")
        y = _pool_fwd(h, wts[("pool_w_grp", j)], ws["pool_scale"][j:j + 1], tm=te, name=f"{tag}_mix")
        xo = mm_out(y, wts[("pool_w_out", j)], xin, f"{tag}_out")
        return xo, (xin, xn, h, y)

    def pool_layer_bwd(dx, dxb, saved, wts, j, tag, after=None):
        xin, xn, h, y = saved
        dy = mm_dx(dxb, wts[("pool_w_out", j)], f"{tag}_dy", after)
        gb[("pool_w_out", j)] = mm_dw(y, dxb, f"{tag}_dwo", after)
        pooled, dmixed, dpooled, dz, dsc = _pool_bwd1(h, dy, wts[("pool_w_grp", j)], ws["pool_scale"][j:j + 1], tm=te, name=f"{tag}_bmix")
        gb[("pool_w_grp", j)] = _grouped_tn(pooled, dmixed, tk=tm, name=f"{tag}_dwg")
        dh = _pool_bwd2(dpooled, dz, tm=te, name=f"{tag}_bshift")
        dxn = mm_dx(dh, wts[("pool_w_in", j)], f"{tag}_dxn")
        gb[("pool_w_in", j)] = mm_dw(xn, dh, f"{tag}_dwi")
        dxo, dxob, dg = _rms_bwd(xin, ws["pool_norm"][j:j + 1], dxn, dx, tm=tm, name=f"{tag}_bnorm")
        gs[f"pool_norm_{j}"], gs[f"pool_scale_{j}"] = dg, dsc
        return dxo, dxob

    w_p0 = weights_for("p0", None)
    x1, sv0 = pool_layer_fwd(x, w_p0, 0, "p0")

    w_cv = weights_for("cv", x1)
    xn1 = _rms_fwd(x1, ws["conv_norm"][0:1], tm=tm, name="cv_norm")
    h1 = mm_in(xn1, w_cv[("conv_w_in", 0)], "cv_in")
    cw = jnp.pad(ws["conv_w"][0], ((0, 5), (0, 0)))
    y1 = _conv_fwd(h1, cw, tm=te, name="cv_mix")
    x2 = mm_out(y1, w_cv[("conv_w_out", 0)], x1, "cv_out")

    w_ml = weights_for("ml", x2)
    w_mi = _mla_in_to_padded(w_ml[("mla_w_in", 0)])
    w_q = _q_up_to_padded(w_ml[("mla_w_q_up", 0)])
    w_kv = w_ml[("mla_w_kv_up", 0)]
    qg, kvg = ws["mla_q_norm"][0:1], ws["mla_kv_norm"][0:1]
    xn2 = _rms_fwd(x2, ws["mla_norm"][0:1], tm=tm, name="ml_norm")
    h2 = mm_in(xn2, w_mi, "ml_in")
    q_n, kv_n, krr = _mla_latent_fwd(h2, qg, kvg, tabs, tm=tm, name="ml_lat")
    q_full = _mla_q_up(q_n, w_q, tabs, tm=tm, name="ml_qup")
    k_full, v = _mla_kv_up(kv_n, w_kv, krr, tm=tm, name="ml_kvup")
    o, y2, lse = _flash_fwd(q_full, k_full, v, h2, tq=tq, name="ml_attn")
    x3 = mm_out(y2, w_ml[("mla_w_out", 0)], x2, "ml_out")

    x4, sv3 = pool_layer_fwd(x3, w_ml, 1, "p1")

    loss_part, dx, dxb, dgf = _final_loss(x4, ws["final_norm"].reshape(1, -1), target, tm=tm, name="final")
    gs["final_norm"] = dgf

    dx, dxb = pool_layer_bwd(dx, dxb, sv3, w_ml, 1, "p1")

    dy = mm_dx(dxb, w_ml[("mla_w_out", 0)], "ml_dy")
    gb[("mla_w_out", 0)] = mm_dw(y2, dxb, "ml_dwo")
    do, dz, delta = _mla_gate_bwd(dy, o, h2, tm=te, name="ml_bgate")
    nq = S // tq
    dq, dkv, dkr = _flash_bwd(q_full, k_full, v, do, lse.reshape(N_HEADS, nq, 1, tq), delta.reshape(N_HEADS, nq, 1, tq), tq=tq, name="ml_battn")
    dq_pre = _mla_unrope_q(dq, tabs, tm=te, name="ml_bqrope")
    dq_n = mm_dx(dq_pre, w_q, "ml_dqn")
    gb[("mla_w_q_up", 0)] = _q_up_from_padded(mm_dw(q_n, dq_pre, "ml_dwq"))
    dkv_n = mm_dx(dkv, w_kv, "ml_dkvn")
    gb[("mla_w_kv_up", 0)] = mm_dw(kv_n, dkv, "ml_dwkv")
    dh2, dqg, dkvg = _mla_latent_bwd(h2, dq_n, dkv_n, dkr, dz, qg, kvg, tabs, tm=te, name="ml_blat")
    dxn2 = mm_dx(dh2, w_mi, "ml_dxn")
    gb[("mla_w_in", 0)] = _mla_in_from_padded(mm_dw(xn2, dh2, "ml_dwi"))
    dx, dxb, dg2 = _rms_bwd(x2, ws["mla_norm"][0:1], dxn2, dx, tm=tm, name="ml_bnorm")
    gs["mla_norm"], gs["mla_q_norm"], gs["mla_kv_norm"] = dg2, dqg, dkvg

    dy = mm_dx(dxb, w_cv[("conv_w_out", 0)], "cv_dy")
    gb[("conv_w_out", 0)] = mm_dw(y1, dxb, "cv_dwo")
    dh1, dcw = _conv_bwd(h1, dy, cw, tm=te, name="cv_bmix")
    dxn1 = mm_dx(dh1, w_cv[("conv_w_in", 0)], "cv_dxn")
    gb[("conv_w_in", 0)] = mm_dw(xn1, dh1, "cv_dwi")
    dx, dxb, dg1 = _rms_bwd(x1, ws["conv_norm"][0:1], dxn1, dx, tm=tm, name="cv_bnorm")
    gs["conv_norm"], gs["conv_w"] = dg1, dcw

    dx, dxb = pool_layer_bwd(dx, dxb, sv0, w_p0, 0, "p0", after=grads_ready(gb))
    return loss_part, dx, gb, gs


def kernel(x, positions, pool_norm, pool_w_in, pool_w_grp, pool_scale, pool_w_out, conv_norm, conv_w_in, conv_w, conv_w_out, mla_norm, mla_w_in, mla_q_norm, mla_w_q_up, mla_kv_norm, mla_w_kv_up, mla_w_out, final_norm, loss_target, m_pool_norm, m_pool_w_in, m_pool_w_grp, m_pool_scale, m_pool_w_out, m_conv_norm, m_conv_w_in, m_conv_w, m_conv_w_out, m_mla_norm, m_mla_w_in, m_mla_q_norm, m_mla_w_q_up, m_mla_kv_norm, m_mla_w_kv_up, m_mla_w_out, m_final_norm, v_pool_norm, v_pool_w_in, v_pool_w_grp, v_pool_scale, v_pool_w_out, v_conv_norm, v_conv_w_in, v_conv_w, v_conv_w_out, v_mla_norm, v_mla_w_in, v_mla_q_norm, v_mla_w_q_up, v_mla_kv_norm, v_mla_w_kv_up, v_mla_w_out, v_final_norm):
    names = ("pool_norm", "pool_w_in", "pool_w_grp", "pool_scale", "pool_w_out", "conv_norm", "conv_w_in", "conv_w", "conv_w_out",
             "mla_norm", "mla_w_in", "mla_q_norm", "mla_w_q_up", "mla_kv_norm", "mla_w_kv_up", "mla_w_out", "final_norm")
    w = dict(zip(names, (pool_norm, pool_w_in, pool_w_grp, pool_scale, pool_w_out, conv_norm, conv_w_in, conv_w, conv_w_out,
                         mla_norm, mla_w_in, mla_q_norm, mla_w_q_up, mla_kv_norm, mla_w_kv_up, mla_w_out, final_norm)))
    m = dict(zip(names, (m_pool_norm, m_pool_w_in, m_pool_w_grp, m_pool_scale, m_pool_w_out, m_conv_norm, m_conv_w_in, m_conv_w, m_conv_w_out,
                         m_mla_norm, m_mla_w_in, m_mla_q_norm, m_mla_w_q_up, m_mla_kv_norm, m_mla_w_kv_up, m_mla_w_out, m_final_norm)))
    v = dict(zip(names, (v_pool_norm, v_pool_w_in, v_pool_w_grp, v_pool_scale, v_pool_w_out, v_conv_norm, v_conv_w_in, v_conv_w, v_conv_w_out,
                         v_mla_norm, v_mla_w_in, v_mla_q_norm, v_mla_w_q_up, v_mla_kv_norm, v_mla_w_kv_up, v_mla_w_out, v_final_norm)))
    chip = 2 * lax.axis_index("x") + lax.axis_index("y")
    core = lax.axis_index("c")

    core1 = core.astype(jnp.int32).reshape(1)
    chip_core = jnp.stack([chip, core]).astype(jnp.int32)
    shard_shape = lambda piece: w[piece[0]].shape[1:]
    shard_axis = lambda piece: BIG_SHARD_AXIS[piece[0]] - 1

    packs = {grp: _pack_rows([w[n][j].astype(BF16) for n, j in pieces], PACK_ROW_ALIGN) for grp, pieces in GROUPS.items()}

    def joined(grp, gathered):
        own = lax.dynamic_update_slice(gathered, packs[grp][None], (chip, 0, 0))
        pieces = GROUPS[grp]
        return dict(zip(pieces, _join_shards(own, [shard_shape(p) for p in pieces], [shard_axis(p) for p in pieces])))

    gathered_p0 = _all_gather_big(packs["p0"], name="ag_p0")
    cv_start = _ag_ici_start(packs["cv"], gathered_p0, name="ag_cv_start")
    ml_start = _ag_ici_start(packs["ml"], cv_start[4], name="ag_ml_start")
    in_flight = {"cv": cv_start, "ml": ml_start}

    def weights_for(grp, after):
        if grp == "p0":
            return joined(grp, gathered_p0)
        send_sems, recv_sems, w_thru, land, _ = in_flight[grp]
        _, land = _ag_ici_wait(send_sems, recv_sems, w_thru, land, after, name=f"ag_{grp}_wait")
        return joined(grp, _ag_sibling_forward(land, name=f"ag_{grp}_fwd"))

    ws = {"conv_norm": w["conv_norm"], "final_norm": w["final_norm"]}
    ws.update(_gather_small({n: w[n] for n, _, _ in SMALL_GATHER}, name="ag_small"))

    def rs_start(gb, pieces, tag):
        packed = jnp.stack([_pack_rows([jnp.split(gb[p], N_CHIPS, axis=shard_axis(p))[k] for p in pieces], RS_ROW_ALIGN) for k in range(N_CHIPS)])
        theirs = _rs_sibling_swap(packed, name=f"{tag}_swap")
        chip_sum = _add2_bf16(packed, theirs, core1, name=f"{tag}_add2")
        return _rs_chip_exchange_start(chip_sum, name=f"{tag}_chips_start")

    def rs_finish(started, after, pieces, tag):
        send_sems, recv_sems, chip_sum, land, _ = started
        chip_sum, recv = _rs_chip_exchange_wait(send_sems, recv_sems, chip_sum, land, after, name=f"{tag}_chips_wait")
        half_sum = _add4_f32(chip_sum, recv, chip_core, name=f"{tag}_add4")
        g_rows = _rs_sibling_join(half_sum, name=f"{tag}_join")
        return dict(zip(pieces, _unpack_rows(g_rows, [shard_shape(p) for p in pieces])))

    late = GROUPS["cv"] + GROUPS["ml"]
    started = {}

    def grads_ready(gb):
        started["late"] = rs_start(gb, late, "rsa")
        return started["late"][4]

    loss_part, grad_x, gb, gs = _local_step(x[0], positions, loss_target[0], weights_for, ws, grads_ready)
    loss = lax.psum(loss_part[0, 0], ("x", "y", "c"))

    g_piece = rs_finish(started["late"], grad_x, late, "rsa")
    first = rs_start(gb, GROUPS["p0"], "rsb")
    g_piece.update(rs_finish(first, first[4], GROUPS["p0"], "rsb"))
    g = {n: jnp.stack([g_piece[(n, j)] for j in range(w[n].shape[0])]) for n in BIG}

    gs_sum = _reduce_small(gs, name="ar_small")

    delta, new_m, new_v = {}, {}, {}
    for n in BIG:
        shp = w[n].shape
        two_d = lambda a: a.reshape(-1, shp[-1])
        d_, m_, v_ = _adamw(two_d(w[n]), two_d(g[n]), two_d(m[n]), two_d(v[n]), name=f"adamw_{n}")
        delta[n], new_m[n], new_v[n] = d_.reshape(shp), m_.reshape(shp), v_.reshape(shp)
    row = lambda d: {n: (d[n].reshape(1, -1) if d[n].ndim == 1 else d[n]) for n in SMALL}
    small_out = _adamw_small(row(w), row(m), row(v), gs_sum, chip.astype(jnp.int32).reshape(1), name="adamw_small")
    for dst, res in zip((g, delta, new_m, new_v), small_out):
        for n in SMALL:
            dst[n] = res[n].reshape(w[n].shape)

    return (loss, grad_x[None], *[g[n] for n in names], *[delta[n] for n in names],
            *[new_m[n] for n in names], *[new_v[n] for n in names])
```

```python
import functools

import jax
import jax.numpy as jnp
from jax import lax
from jax.experimental import pallas as pl
from jax.experimental.pallas import tpu as pltpu

F32 = jnp.float32
BF16 = jnp.bfloat16

D_MODEL = 1024
D_INNER = 2048
POOL_WINDOWS = (2, 4, 8, 16)
POOL_GROUP = 512
N_HEADS = 16
QK_NOPE = 128
QK_ROPE = 64
V_DIM = 128
HEAD_PAD = 256
Q_LORA = 384
KV_LORA = 256
MLA_IN = Q_LORA + KV_LORA + QK_ROPE + D_INNER
MLA_IN_PAD = 2816
ATTN_SCALE = (QK_NOPE + QK_ROPE) ** -0.5
ROPE_BASE = 10000.0
NORM_EPS = 1e-6
HALO = 16

ADAM_LR = 0.001
ADAM_B1 = 0.9
ADAM_B2 = 0.999
ADAM_EPS = 1e-08
ADAM_WD = 0.01
ADAM_STEP = 10

N_CHIPS = 4
N_DEV = 8
LANES = 128
PACK_COLS = 1024
V7X_VMEM_LIMIT = 56 * 1024 * 1024
MESH = pl.DeviceIdType.MESH


def _cparams(*sem):
    return pltpu.CompilerParams(dimension_semantics=sem, vmem_limit_bytes=V7X_VMEM_LIMIT)


def _pick(n, cap):
    best = None
    for d in range(LANES, min(n, cap) + 1, LANES):
        if n % d == 0:
            best = d
    assert best is not None, (n, cap)
    return best


def _sigmoid(z):
    return 1.0 / (1.0 + jnp.exp(-z))


class Packed:
    def __init__(self, buf, r0, kind, shape):
        self.buf, self.r0, self.kind, self.shape = buf, r0, kind, shape

    def block(self, rb, cb, bk):
        assert self.r0 % bk == 0, (self.r0, bk)
        if self.kind == "cols":
            K = self.shape[0]
            per = self.shape[1] // (N_CHIPS * PACK_COLS)
            assert K % bk == 0
            return cb // per, (self.r0 + (cb % per) * K) // bk + rb
        kk = self.shape[0] // N_CHIPS
        assert kk % bk == 0
        per = kk // bk
        return rb // per, self.r0 // bk + rb % per


def _mm(a, b, *, trans_a=False, trans_b=False, out_dtype=F32, residual=None, after=None, into=None, tm, tn, tk, name):
    if trans_a:
        K, M = a.shape
    else:
        M, K = a.shape
    if trans_b:
        N, K2 = b.shape
    else:
        K2, N = b.shape
    assert K == K2 and M % tm == 0 and N % tn == 0 and K % tk == 0, (name, a.shape, b.shape, tm, tn, tk)
    nk = K // tk
    dn = (((0 if trans_a else 1,), (1 if trans_b else 0,)), ((), ()))
    has_res = residual is not None
    n_skip = (after is not None) + (into is not None)

    def body(*refs):
        if has_res:
            a_ref, b_ref, r_ref = refs[:3]
            refs = refs[3:]
        else:
            a_ref, b_ref = refs[:2]
            r_ref = None
            refs = refs[2:]
        refs = refs[n_skip:]
        o_ref, rest = refs[0], refs[1:]
        part = lax.dot_general(a_ref[...], b_ref[...], dn, preferred_element_type=F32)

        def finish(acc):
            if has_res:
                acc = acc + r_ref[...]
            o_ref[...] = acc.astype(o_ref.dtype)

        if nk == 1:
            finish(part)
        else:
            acc_ref = rest[0]
            k = pl.program_id(2)

            @pl.when(k == 0)
            def _():
                acc_ref[...] = part

            @pl.when(k > 0)
            def _():
                acc_ref[...] += part

            @pl.when(k == nk - 1)
            def _():
                finish(acc_ref[...])

    a_spec = pl.BlockSpec((tk, tm), lambda i, j, k: (k, i)) if trans_a else pl.BlockSpec((tm, tk), lambda i, j, k: (i, k))
    if isinstance(b, Packed):
        if trans_b:
            assert tk == PACK_COLS
            b_spec = pl.BlockSpec((None, tn, tk), lambda i, j, k: (*b.block(j, k, tn), 0))
        else:
            assert tn == PACK_COLS
            b_spec = pl.BlockSpec((None, tk, tn), lambda i, j, k: (*b.block(k, j, tk), 0))
        b_arg = b.buf
    else:
        b_spec = pl.BlockSpec((tn, tk), lambda i, j, k: (j, k)) if trans_b else pl.BlockSpec((tk, tn), lambda i, j, k: (k, j))
        b_arg = b
    o_spec = pl.BlockSpec((tm, tn), lambda i, j, k: (i, j))
    in_specs = [a_spec, b_spec] + ([o_spec] if has_res else [])
    args = (a, b_arg) + ((residual,) if has_res else ())
    aliases = {}
    if after is not None:
        in_specs.append(pl.BlockSpec(memory_space=pl.ANY))
        args += (after,)
    if into is None:
        out_shape, out_spec = jax.ShapeDtypeStruct((M, N), out_dtype), o_spec
    else:
        assert tn == PACK_COLS and into.shape == (M, N)
        in_specs.append(pl.BlockSpec(memory_space=pl.ANY))
        aliases = {len(args): 0}
        args += (into.buf,)
        out_shape = jax.ShapeDtypeStruct(into.buf.shape, into.buf.dtype)
        out_spec = pl.BlockSpec((None, tm, tn), lambda i, j, k: (*into.block(i, j, tm), 0))
    return pl.pallas_call(
        body, name=name, out_shape=out_shape,
        grid=(M // tm, N // tn, nk),
        in_specs=in_specs, out_specs=out_spec, input_output_aliases=aliases,
        scratch_shapes=[pltpu.VMEM((tm, tn), F32)] if nk > 1 else [],
        compiler_params=_cparams("parallel", "parallel", "arbitrary"),
    )(*args)


def _grouped_tn(a, b, *, tk, name):
    S = a.shape[0]
    G = POOL_GROUP
    nk = S // tk

    def body(a_ref, b_ref, o_ref, acc_ref):
        k = pl.program_id(1)
        part = lax.dot_general(a_ref[...], b_ref[...], (((0,), (0,)), ((), ())), preferred_element_type=F32)

        @pl.when(k == 0)
        def _():
            acc_ref[...] = part

        @pl.when(k > 0)
        def _():
            acc_ref[...] += part

        @pl.when(k == nk - 1)
        def _():
            o_ref[...] = acc_ref[...].astype(o_ref.dtype)

    return pl.pallas_call(
        body, name=name,
        out_shape=jax.ShapeDtypeStruct((len(POOL_WINDOWS), G, G), BF16),
        grid=(len(POOL_WINDOWS), nk),
        in_specs=[pl.BlockSpec((tk, G), lambda g, k: (k, g)), pl.BlockSpec((tk, G), lambda g, k: (k, g))],
        out_specs=pl.BlockSpec((None, G, G), lambda g, k: (g, 0, 0)),
        scratch_shapes=[pltpu.VMEM((G, G), F32)],
        compiler_params=_cparams("parallel", "arbitrary"),
    )(a, b)


def _rms_fwd(x, g, *, tm, name):
    S, D = x.shape

    def body(x_ref, g_ref, o_ref):
        xv = x_ref[...]
        rstd = lax.rsqrt(jnp.mean(xv * xv, axis=-1, keepdims=True) + NORM_EPS)
        o_ref[...] = (xv * rstd * g_ref[...]).astype(o_ref.dtype)

    return pl.pallas_call(
        body, name=name,
        out_shape=jax.ShapeDtypeStruct((S, D), BF16),
        grid=(S // tm,),
        in_specs=[pl.BlockSpec((tm, D), lambda i: (i, 0)), pl.BlockSpec((1, D), lambda i: (0, 0))],
        out_specs=pl.BlockSpec((tm, D), lambda i: (i, 0)),
        compiler_params=_cparams("parallel"),
    )(x, g)


def _rms_bwd_math(xv, gv, dxn):
    rstd = lax.rsqrt(jnp.mean(xv * xv, axis=-1, keepdims=True) + NORM_EPS)
    xh = xv * rstd
    dg = jnp.sum(dxn * xh, axis=0, keepdims=True)
    dxh = dxn * gv
    dx = rstd * (dxh - xh * jnp.mean(dxh * xh, axis=-1, keepdims=True))
    return dx, dg


def _rms_bwd(x, g, dxn, dres, *, tm, name):
    S, D = x.shape

    def body(x_ref, g_ref, dxn_ref, dres_ref, dx_ref, dxb_ref, dg_ref):
        dx, dg = _rms_bwd_math(x_ref[...], g_ref[...], dxn_ref[...])
        dx = dx + dres_ref[...]
        dx_ref[...] = dx
        dxb_ref[...] = dx.astype(BF16)

        @pl.when(pl.program_id(0) == 0)
        def _():
            dg_ref[...] = dg

        @pl.when(pl.program_id(0) > 0)
        def _():
            dg_ref[...] += dg

    row = pl.BlockSpec((tm, D), lambda i: (i, 0))
    vec = pl.BlockSpec((1, D), lambda i: (0, 0))
    return pl.pallas_call(
        body, name=name,
        out_shape=(jax.ShapeDtypeStruct((S, D), F32), jax.ShapeDtypeStruct((S, D), BF16), jax.ShapeDtypeStruct((1, D), F32)),
        grid=(S // tm,),
        in_specs=[row, vec, row, row],
        out_specs=(row, row, vec),
        compiler_params=_cparams("arbitrary"),
    )(x, g, dxn, dres)


def _final_loss(x, g, target, *, tm, name):
    S, D = x.shape

    def body(x_ref, g_ref, t_ref, loss_ref, dx_ref, dxb_ref, dg_ref):
        xv = x_ref[...]
        gv = g_ref[...]
        rstd = lax.rsqrt(jnp.mean(xv * xv, axis=-1, keepdims=True) + NORM_EPS)
        xh = xv * rstd
        err = xh * gv - t_ref[...]
        part = 0.5 * jnp.sum(jnp.mean(err * err, axis=-1, keepdims=True), axis=0, keepdims=True)
        dy = err * (1.0 / D)
        dg = jnp.sum(dy * xh, axis=0, keepdims=True)
        dxh = dy * gv
        dx = rstd * (dxh - xh * jnp.mean(dxh * xh, axis=-1, keepdims=True))
        dx_ref[...] = dx
        dxb_ref[...] = dx.astype(BF16)
        lossb = jnp.broadcast_to(part, loss_ref.shape)

        @pl.when(pl.program_id(0) == 0)
        def _():
            dg_ref[...] = dg
            loss_ref[...] = lossb

        @pl.when(pl.program_id(0) > 0)
        def _():
            dg_ref[...] += dg
            loss_ref[...] += lossb

    row = pl.BlockSpec((tm, D), lambda i: (i, 0))
    vec = pl.BlockSpec((1, D), lambda i: (0, 0))
    lspec = pl.BlockSpec((8, LANES), lambda i: (0, 0))
    return pl.pallas_call(
        body, name=name,
        out_shape=(jax.ShapeDtypeStruct((8, LANES), F32), jax.ShapeDtypeStruct((S, D), F32),
                   jax.ShapeDtypeStruct((S, D), BF16), jax.ShapeDtypeStruct((1, D), F32)),
        grid=(S // tm,),
        in_specs=[row, vec, row],
        out_specs=(lspec, row, row, vec),
        compiler_params=_cparams("arbitrary"),
    )(x, g, target)


def _prev_halo_spec(tm, width, col):
    r = tm // HALO
    return pl.BlockSpec((HALO, width), lambda i: (jnp.maximum(i * r - 1, 0), col))


def _next_halo_spec(tm, width, col, S):
    r = tm // HALO
    last = S // HALO - 1
    return pl.BlockSpec((HALO, width), lambda i: (jnp.minimum((i + 1) * r, last), col))


def _shift_down(ext, k):
    return pltpu.roll(ext, k, 0)[HALO:, :]


def _shift_up(ext, k, tm):
    n = ext.shape[0]
    return pltpu.roll(ext, n - k, 0)[:tm, :]


def _pool_window_sum(ext, w):
    s = ext
    k = 1
    while k < w:
        s = s + pltpu.roll(s, k, 0)
        k *= 2
    return s[HALO:, :]


def _pooled_group(u_ref, halo, g, w, t_idx):
    cs = slice(g * POOL_GROUP, (g + 1) * POOL_GROUP)
    u = u_ref[:, cs]
    ext = jnp.concatenate([halo[:, cs], u], axis=0)
    inv = 1.0 / jnp.minimum(t_idx + 1, w).astype(F32)
    return _pool_window_sum(ext, w) * inv - u


def _pool_fwd(h, w_grp, scale, *, tm, name):
    S = h.shape[0]
    E = D_INNER

    def body(u_ref, uh_ref, z_ref, wg_ref, sc_ref, y_ref):
        i = pl.program_id(0)
        halo = jnp.where(i > 0, uh_ref[...], 0.0)
        t_idx = i * tm + lax.broadcasted_iota(jnp.int32, (tm, 1), 0)
        for g, w in enumerate(POOL_WINDOWS):
            cs = slice(g * POOL_GROUP, (g + 1) * POOL_GROUP)
            pooled = _pooled_group(u_ref, halo, g, w, t_idx)
            mixed = jnp.dot(pooled.astype(BF16), wg_ref[g], preferred_element_type=F32)
            z = z_ref[:, cs]
            y_ref[:, cs] = (mixed * sc_ref[:, cs] * (z * _sigmoid(z))).astype(BF16)

    return pl.pallas_call(
        body, name=name,
        out_shape=jax.ShapeDtypeStruct((S, E), BF16),
        grid=(S // tm,),
        in_specs=[pl.BlockSpec((tm, E), lambda i: (i, 0)), _prev_halo_spec(tm, E, 0),
                  pl.BlockSpec((tm, E), lambda i: (i, 1)),
                  pl.BlockSpec((len(POOL_WINDOWS), POOL_GROUP, POOL_GROUP), lambda i: (0, 0, 0)),
                  pl.BlockSpec((1, E), lambda i: (0, 0))],
        out_specs=pl.BlockSpec((tm, E), lambda i: (i, 0)),
        compiler_params=_cparams("parallel"),
    )(h, h, h, w_grp, scale)


def _pool_bwd1(h, dy, w_grp, scale, *, tm, name):
    S = h.shape[0]
    E = D_INNER

    def body(u_ref, uh_ref, z_ref, dy_ref, wg_ref, sc_ref, pooled_ref, dmixed_ref, dpooled_ref, dz_ref, dsc_ref):
        i = pl.program_id(0)
        halo = jnp.where(i > 0, uh_ref[...], 0.0)
        t_idx = i * tm + lax.broadcasted_iota(jnp.int32, (tm, 1), 0)
        for g, w in enumerate(POOL_WINDOWS):
            cs = slice(g * POOL_GROUP, (g + 1) * POOL_GROUP)
            pooled = _pooled_group(u_ref, halo, g, w, t_idx).astype(BF16)
            wg = wg_ref[g]
            mixed = jnp.dot(pooled, wg, preferred_element_type=F32)
            z = z_ref[:, cs]
            sg = _sigmoid(z)
            dyv = dy_ref[:, cs]
            sc = sc_ref[:, cs]
            dms = dyv * (z * sg)
            dz = dyv * (mixed * sc) * (sg * (1.0 + z * (1.0 - sg)))
            dsc = jnp.sum(dms * mixed, axis=0, keepdims=True)
            dmixed = (dms * sc).astype(BF16)
            dpooled = lax.dot_general(dmixed, wg, (((1,), (1,)), ((), ())), preferred_element_type=F32)
            pooled_ref[:, cs] = pooled
            dmixed_ref[:, cs] = dmixed
            dpooled_ref[:, cs] = dpooled
            dz_ref[:, cs] = dz.astype(BF16)

            @pl.when(i == 0)
            def _():
                dsc_ref[:, cs] = dsc

            @pl.when(i > 0)
            def _():
                dsc_ref[:, cs] += dsc

    row = pl.BlockSpec((tm, E), lambda i: (i, 0))
    vec = pl.BlockSpec((1, E), lambda i: (0, 0))
    return pl.pallas_call(
        body, name=name,
        out_shape=(jax.ShapeDtypeStruct((S, E), BF16), jax.ShapeDtypeStruct((S, E), BF16),
                   jax.ShapeDtypeStruct((S, E), F32), jax.ShapeDtypeStruct((S, E), BF16),
                   jax.ShapeDtypeStruct((1, E), F32)),
        grid=(S // tm,),
        in_specs=[row, _prev_halo_spec(tm, E, 0), pl.BlockSpec((tm, E), lambda i: (i, 1)), row,
                  pl.BlockSpec((len(POOL_WINDOWS), POOL_GROUP, POOL_GROUP), lambda i: (0, 0, 0)), vec],
        out_specs=(row, row, row, row, vec),
        compiler_params=_cparams("arbitrary"),
    )(h, h, h, dy, w_grp, scale)


def _pool_bwd2(dpooled, dz, *, tm, name):
    S = dpooled.shape[0]
    E = D_INNER
    nt = S // tm

    def body(dp_ref, dpn_ref, dz_ref, dh_ref):
        i = pl.program_id(0)
        nxt = jnp.where(i < nt - 1, dpn_ref[...], 0.0)
        t_ext = i * tm + lax.broadcasted_iota(jnp.int32, (tm + HALO, 1), 0)
        for g, w in enumerate(POOL_WINDOWS):
            cs = slice(g * POOL_GROUP, (g + 1) * POOL_GROUP)
            dp = dp_ref[:, cs]
            inv = 1.0 / jnp.minimum(t_ext + 1, w).astype(F32)
            s = jnp.concatenate([dp, nxt[:, cs]], axis=0) * inv
            n = tm + HALO
            k = 1
            while k < w:
                s = s + pltpu.roll(s, n - k, 0)
                k *= 2
            dh_ref[:, cs] = (s[:tm, :] - dp).astype(BF16)
        dh_ref[:, E:] = dz_ref[...]

    return pl.pallas_call(
        body, name=name,
        out_shape=jax.ShapeDtypeStruct((S, 2 * E), BF16),
        grid=(nt,),
        in_specs=[pl.BlockSpec((tm, E), lambda i: (i, 0)), _next_halo_spec(tm, E, 0, S),
                  pl.BlockSpec((tm, E), lambda i: (i, 0))],
        out_specs=pl.BlockSpec((tm, 2 * E), lambda i: (i, 0)),
        compiler_params=_cparams("parallel"),
    )(dpooled, dpooled, dz)


CONV_CHUNK = 512


def _conv_fwd(h, cw, *, tm, name):
    S = h.shape[0]
    E = D_INNER

    def body(b_ref, c_ref, hh_ref, z_ref, ch_ref, hhh_ref, w_ref, y_ref):
        i = pl.program_id(0)
        for j in range(E // CONV_CHUNK):
            cs = slice(j * CONV_CHUNK, (j + 1) * CONV_CHUNK)
            p = c_ref[:, cs] * hh_ref[:, cs]
            ph = jnp.where(i > 0, ch_ref[:, cs] * hhh_ref[:, cs], 0.0)
            ext = jnp.concatenate([ph, p], axis=0)
            conv = w_ref[2:3, cs] * p + w_ref[1:2, cs] * _shift_down(ext, 1) + w_ref[0:1, cs] * _shift_down(ext, 2)
            z = z_ref[:, cs]
            y_ref[:, cs] = (b_ref[:, cs] * conv * (z * _sigmoid(z))).astype(BF16)

    col = lambda c: pl.BlockSpec((tm, E), lambda i: (i, c))
    return pl.pallas_call(
        body, name=name,
        out_shape=jax.ShapeDtypeStruct((S, E), BF16),
        grid=(S // tm,),
        in_specs=[col(0), col(1), col(2), col(3), _prev_halo_spec(tm, E, 1), _prev_halo_spec(tm, E, 2),
                  pl.BlockSpec((8, E), lambda i: (0, 0))],
        out_specs=pl.BlockSpec((tm, E), lambda i: (i, 0)),
        compiler_params=_cparams("parallel"),
    )(h, h, h, h, h, h, cw)


def _conv_bwd(h, dy, cw, *, tm, name):
    S = h.shape[0]
    E = D_INNER
    nt = S // tm

    def body(b_ref, c_ref, hh_ref, z_ref, dy_ref, ch_ref, hhh_ref, bn_ref, zn_ref, dyn_ref, w_ref, dh_ref, dw_ref):
        i = pl.program_id(0)
        for j in range(E // CONV_CHUNK):
            cs = slice(j * CONV_CHUNK, (j + 1) * CONV_CHUNK)
            w0, w1, w2 = w_ref[0:1, cs], w_ref[1:2, cs], w_ref[2:3, cs]
            c, hh, b, z, dyv = c_ref[:, cs], hh_ref[:, cs], b_ref[:, cs], z_ref[:, cs], dy_ref[:, cs]
            p = c * hh
            ph = jnp.where(i > 0, ch_ref[:, cs] * hhh_ref[:, cs], 0.0)
            ext = jnp.concatenate([ph, p], axis=0)
            pm1 = _shift_down(ext, 1)
            pm2 = _shift_down(ext, 2)
            conv = w2 * p + w1 * pm1 + w0 * pm2
            sg = _sigmoid(z)
            dy0 = dyv * (z * sg)
            dz = dyv * (b * conv) * (sg * (1.0 + z * (1.0 - sg)))
            db = dy0 * conv
            dconv = dy0 * b
            zn = zn_ref[:, cs]
            dconv_n = jnp.where(i < nt - 1, dyn_ref[:, cs] * (zn * _sigmoid(zn)) * bn_ref[:, cs], 0.0)
            dext = jnp.concatenate([dconv, dconv_n], axis=0)
            dp = w2 * dconv + w1 * _shift_up(dext, 1, tm) + w0 * _shift_up(dext, 2, tm)
            dh_ref[:, 0 * E + j * CONV_CHUNK:0 * E + (j + 1) * CONV_CHUNK] = db.astype(BF16)
            dh_ref[:, 1 * E + j * CONV_CHUNK:1 * E + (j + 1) * CONV_CHUNK] = (dp * hh).astype(BF16)
            dh_ref[:, 2 * E + j * CONV_CHUNK:2 * E + (j + 1) * CONV_CHUNK] = (dp * c).astype(BF16)
            dh_ref[:, 3 * E + j * CONV_CHUNK:3 * E + (j + 1) * CONV_CHUNK] = dz.astype(BF16)
            dw = jnp.concatenate([jnp.sum(dconv * pm2, axis=0, keepdims=True),
                                  jnp.sum(dconv * pm1, axis=0, keepdims=True),
                                  jnp.sum(dconv * p, axis=0, keepdims=True),
                                  jnp.zeros((5, CONV_CHUNK), F32)], axis=0)

            @pl.when(i == 0)
            def _():
                dw_ref[:, cs] = dw

            @pl.when(i > 0)
            def _():
                dw_ref[:, cs] += dw

    col = lambda c: pl.BlockSpec((tm, E), lambda i: (i, c))
    return pl.pallas_call(
        body, name=name,
        out_shape=(jax.ShapeDtypeStruct((S, 4 * E), BF16), jax.ShapeDtypeStruct((8, E), F32)),
        grid=(nt,),
        in_specs=[col(0), col(1), col(2), col(3), pl.BlockSpec((tm, E), lambda i: (i, 0)),
                  _prev_halo_spec(tm, E, 1), _prev_halo_spec(tm, E, 2),
                  _next_halo_spec(tm, E, 0, S), _next_halo_spec(tm, E, 3, S), _next_halo_spec(tm, E, 0, S),
                  pl.BlockSpec((8, E), lambda i: (0, 0))],
        out_specs=(pl.BlockSpec((tm, 4 * E), lambda i: (i, 0)), pl.BlockSpec((8, E), lambda i: (0, 0))),
        compiler_params=_cparams("arbitrary"),
    )(h, h, h, h, dy, h, h, h, h, dy, cw)


Z_COLS = D_INNER // LANES
KV_LAT_BLK = D_INNER // KV_LORA
Q_LAT_BLK = (D_INNER + KV_LORA) // Q_LORA
K_ROPE_BLK = (D_INNER + KV_LORA + Q_LORA) // LANES


def _rope(blk, c, s1, s2):
    return blk * c + pltpu.roll(blk, LANES - QK_ROPE // 2, 1) * s1 + pltpu.roll(blk, QK_ROPE // 2, 1) * s2


def _unrope(blk, c, s1, s2):
    return blk * c - pltpu.roll(blk, LANES - QK_ROPE // 2, 1) * s1 - pltpu.roll(blk, QK_ROPE // 2, 1) * s2


def _lat_norm(v, g):
    rstd = lax.rsqrt(jnp.mean(v * v, axis=-1, keepdims=True) + NORM_EPS)
    return v * rstd * g


def _mla_latent_fwd(h, q_norm, kv_norm, tabs, *, tm, name):
    S = h.shape[0]

    def body(kv_ref, q_ref, kr_ref, qg_ref, kvg_ref, c_ref, s1_ref, s2_ref, qn_ref, kvn_ref, krr_ref):
        qn_ref[...] = _lat_norm(q_ref[...], qg_ref[...]).astype(BF16)
        kvn_ref[...] = _lat_norm(kv_ref[...], kvg_ref[...]).astype(BF16)
        krr_ref[...] = _rope(kr_ref[...], c_ref[...], s1_ref[...], s2_ref[...]).astype(BF16)

    tab = pl.BlockSpec((tm, LANES), lambda i: (i, 0))
    return pl.pallas_call(
        body, name=name,
        out_shape=(jax.ShapeDtypeStruct((S, Q_LORA), BF16), jax.ShapeDtypeStruct((S, KV_LORA), BF16),
                   jax.ShapeDtypeStruct((S, LANES), BF16)),
        grid=(S // tm,),
        in_specs=[pl.BlockSpec((tm, KV_LORA), lambda i: (i, KV_LAT_BLK)), pl.BlockSpec((tm, Q_LORA), lambda i: (i, Q_LAT_BLK)),
                  pl.BlockSpec((tm, LANES), lambda i: (i, K_ROPE_BLK)),
                  pl.BlockSpec((1, Q_LORA), lambda i: (0, 0)), pl.BlockSpec((1, KV_LORA), lambda i: (0, 0)), tab, tab, tab],
        out_specs=(pl.BlockSpec((tm, Q_LORA), lambda i: (i, 0)), pl.BlockSpec((tm, KV_LORA), lambda i: (i, 0)), tab),
        compiler_params=_cparams("parallel"),
    )(h, h, h, q_norm, kv_norm, *tabs)


def _mla_q_up(q_n, w_q_pad, tabs, *, tm, name):
    S = q_n.shape[0]

    def body(a_ref, w_ref, c_ref, s1_ref, s2_ref, o_ref):
        a = a_ref[...]
        for hd in range(N_HEADS):
            acc = jnp.dot(a, w_ref[:, hd * HEAD_PAD:(hd + 1) * HEAD_PAD], preferred_element_type=F32)
            o_ref[hd, :, :QK_NOPE] = acc[:, :QK_NOPE].astype(BF16)
            o_ref[hd, :, QK_NOPE:] = _rope(acc[:, QK_NOPE:], c_ref[...], s1_ref[...], s2_ref[...]).astype(BF16)

    tab = pl.BlockSpec((tm, LANES), lambda i: (i, 0))
    return pl.pallas_call(
        body, name=name,
        out_shape=jax.ShapeDtypeStruct((N_HEADS, S, HEAD_PAD), BF16),
        grid=(S // tm,),
        in_specs=[pl.BlockSpec((tm, Q_LORA), lambda i: (i, 0)), pl.BlockSpec((Q_LORA, N_HEADS * HEAD_PAD), lambda i: (0, 0)),
                  tab, tab, tab],
        out_specs=pl.BlockSpec((N_HEADS, tm, HEAD_PAD), lambda i: (0, i, 0)),
        compiler_params=_cparams("parallel"),
    )(q_n, w_q_pad, *tabs)


def _mla_kv_up(kv_n, w_kv, krr, *, tm, name):
    S = kv_n.shape[0]
    heads_per_chip = N_HEADS // N_CHIPS

    def body(a_ref, w_ref, krr_ref, k_ref, v_ref):
        a = a_ref[...]
        ones = jnp.ones((tm, V_DIM), BF16)
        for hd in range(N_HEADS):
            lo = (hd % heads_per_chip) * HEAD_PAD
            acc = jnp.dot(a, w_ref[hd // heads_per_chip, :, lo:lo + HEAD_PAD], preferred_element_type=F32)
            k_ref[hd, :, :QK_NOPE] = acc[:, :QK_NOPE].astype(BF16)
            k_ref[hd, :, QK_NOPE:] = krr_ref[...]
            v_ref[hd, :, :V_DIM] = acc[:, QK_NOPE:].astype(BF16)
            v_ref[hd, :, V_DIM:] = ones

    head_blk = pl.BlockSpec((N_HEADS, tm, HEAD_PAD), lambda i: (0, i, 0))
    sds = jax.ShapeDtypeStruct((N_HEADS, S, HEAD_PAD), BF16)
    return pl.pallas_call(
        body, name=name, out_shape=(sds, sds),
        grid=(S // tm,),
        in_specs=[pl.BlockSpec((tm, KV_LORA), lambda i: (i, 0)),
                  pl.BlockSpec((N_CHIPS, KV_LORA, PACK_COLS), lambda i: (0, w_kv.r0 // KV_LORA, 0)),
                  pl.BlockSpec((tm, LANES), lambda i: (i, 0))],
        out_specs=(head_blk, head_blk),
        compiler_params=_cparams("parallel"),
    )(kv_n, w_kv.buf, krr)


LOG2E = 1.4426950408889634
SCORE_TO_LOG2 = ATTN_SCALE * LOG2E


def _flash_fwd(q_full, k_full, v_aug, h, *, tq, name):
    H, S, _ = q_full.shape
    tk = tq
    HP = 2

    def body(q_ref, k_ref, v_ref, z_ref, o_ref, y_ref, lse_ref, m_sc, acc_sc):
        qi = pl.program_id(1)
        m_sc[...] = jnp.full(m_sc.shape, -1e30, F32)
        acc_sc[...] = jnp.zeros(acc_sc.shape, F32)

        def chunk(j, diag):
            off = pl.multiple_of(j * tk, tk)
            for hh in range(HP):
                kj = k_ref[hh, pl.ds(off, tk), :]
                vj = v_ref[hh, pl.ds(off, tk), :]
                s = lax.dot_general(q_ref[hh], kj, (((1,), (1,)), ((), ())), preferred_element_type=F32) * SCORE_TO_LOG2
                if diag:
                    keep = lax.broadcasted_iota(jnp.int32, (tq, tk), 1) <= lax.broadcasted_iota(jnp.int32, (tq, tk), 0)
                    s = jnp.where(keep, s, -1e30)
                m_old = m_sc[hh]
                m_new = jnp.maximum(m_old, jnp.max(s, axis=1, keepdims=True))
                p = jnp.exp2(s - jnp.tile(m_new, (1, tk // LANES)))
                alpha = jnp.exp2(m_old - m_new)
                acc_sc[hh] = jnp.tile(alpha, (1, 2)) * acc_sc[hh] + jnp.dot(p.astype(BF16), vj, preferred_element_type=F32)
                m_sc[hh] = m_new

        def step(j, carry):
            chunk(j, False)
            return carry

        lax.fori_loop(0, qi, step, 0)
        chunk(qi, True)
        for hh in range(HP):
            cs = slice(hh * V_DIM, (hh + 1) * V_DIM)
            l = acc_sc[hh, :, V_DIM:]
            o = acc_sc[hh, :, :V_DIM] / l
            z = z_ref[:, cs]
            o_ref[:, cs] = o
            y_ref[:, cs] = (o * (z * _sigmoid(z))).astype(BF16)
            lse_ref[hh, 0] = (m_sc[hh] + jnp.log2(l)).T[0:1, :]

    pair = pl.BlockSpec((tq, HP * V_DIM), lambda hd, i: (i, hd))
    return pl.pallas_call(
        body, name=name,
        out_shape=(jax.ShapeDtypeStruct((S, D_INNER), F32), jax.ShapeDtypeStruct((S, D_INNER), BF16),
                   jax.ShapeDtypeStruct((H, S // tq, 1, tq), F32)),
        grid=(H // HP, S // tq),
        in_specs=[pl.BlockSpec((HP, tq, HEAD_PAD), lambda hd, i: (hd, i, 0)),
                  pl.BlockSpec((HP, S, HEAD_PAD), lambda hd, i: (hd, 0, 0)),
                  pl.BlockSpec((HP, S, HEAD_PAD), lambda hd, i: (hd, 0, 0)),
                  pair],
        out_specs=(pair, pair, pl.BlockSpec((HP, 1, 1, tq), lambda hd, i: (hd, i, 0, 0))),
        scratch_shapes=[pltpu.VMEM((HP, tq, LANES), F32), pltpu.VMEM((HP, tq, HEAD_PAD), F32)],
        compiler_params=_cparams("parallel", "parallel"),
    )(q_full, k_full, v_aug, h)


def _mla_gate_bwd(dy, o, h, *, tm, name):
    S = dy.shape[0]
    E = D_INNER

    def body(dy_ref, o_ref, z_ref, do_ref, dz_ref, delta_ref):
        for hd in range(N_HEADS):
            cs = slice(hd * V_DIM, (hd + 1) * V_DIM)
            z = z_ref[:, cs]
            sg = _sigmoid(z)
            dyv = dy_ref[:, cs]
            ov = o_ref[:, cs]
            do = dyv * (z * sg)
            do_ref[:, cs] = do.astype(BF16)
            dz_ref[:, cs] = (dyv * ov * (sg * (1.0 + z * (1.0 - sg)))).astype(BF16)
            delta_ref[hd, 0] = jnp.broadcast_to(jnp.sum(do * ov, axis=-1, keepdims=True), (tm, LANES)).T[0:1, :]

    row = pl.BlockSpec((tm, E), lambda i: (i, 0))
    return pl.pallas_call(
        body, name=name,
        out_shape=(jax.ShapeDtypeStruct((S, E), BF16), jax.ShapeDtypeStruct((S, E), BF16),
                   jax.ShapeDtypeStruct((N_HEADS, S // tm, 1, tm), F32)),
        grid=(S // tm,),
        in_specs=[row, row, row],
        out_specs=(row, row, pl.BlockSpec((N_HEADS, 1, 1, tm), lambda i: (0, i, 0, 0))),
        compiler_params=_cparams("parallel"),
    )(dy, o, h)


def _flash_bwd(q_full, k_full, v_aug, do, lse_rows, delta_rows, *, tq, name):
    H, S, _ = q_full.shape
    tk = tq
    nq = S // tq
    HP = 2

    def body(q_ref, k_ref, v_ref, do_ref, lse_ref, dl_ref, dq_ref, dkv_ref, dkr_ref, dk_sc, dv_sc):
        kj = pl.program_id(1)

        @pl.when(kj == 0)
        def _():
            dq_ref[...] = jnp.zeros(dq_ref.shape, F32)

        dk_sc[...] = jnp.zeros(dk_sc.shape, F32)
        dv_sc[...] = jnp.zeros(dv_sc.shape, F32)

        def chunk(qi, diag):
            off = pl.multiple_of(qi * tq, tq)
            for hh in range(HP):
                k = k_ref[hh]
                q = q_ref[hh, pl.ds(off, tq), :]
                dov = do_ref[pl.ds(off, tq), hh * V_DIM:(hh + 1) * V_DIM]
                s_t = lax.dot_general(k, q, (((1,), (1,)), ((), ())), preferred_element_type=F32) * SCORE_TO_LOG2
                p_t = jnp.exp2(s_t - lse_ref[hh, qi])
                if diag:
                    keep = lax.broadcasted_iota(jnp.int32, (tk, tq), 0) <= lax.broadcasted_iota(jnp.int32, (tk, tq), 1)
                    p_t = jnp.where(keep, p_t, 0.0)
                dv_sc[hh] += jnp.dot(p_t.astype(BF16), dov, preferred_element_type=F32)
                dp_t = lax.dot_general(v_ref[hh], dov, (((1,), (1,)), ((), ())), preferred_element_type=F32)
                ds = (p_t * (dp_t - dl_ref[hh, qi])).astype(BF16)
                dk_sc[hh] += jnp.dot(ds, q, preferred_element_type=F32)
                dq_ref[pl.ds(off, tq), hh * HEAD_PAD:(hh + 1) * HEAD_PAD] += lax.dot_general(
                    ds, k, (((0,), (0,)), ((), ())), preferred_element_type=F32)

        def step(qi, carry):
            chunk(qi, False)
            return carry

        chunk(kj, True)
        lax.fori_loop(kj + 1, nq, step, 0)
        for hh in range(HP):
            lo = hh * HEAD_PAD
            dkv_ref[:, lo:lo + QK_NOPE] = (dk_sc[hh, :, :QK_NOPE] * ATTN_SCALE).astype(BF16)
            dkv_ref[:, lo + QK_NOPE:lo + HEAD_PAD] = dv_sc[hh].astype(BF16)
            dkr_ref[hh] = dk_sc[hh, :, QK_NOPE:] * ATTN_SCALE

    return pl.pallas_call(
        body, name=name,
        out_shape=(jax.ShapeDtypeStruct((S, H * HEAD_PAD), F32), jax.ShapeDtypeStruct((S, H * HEAD_PAD), BF16),
                   jax.ShapeDtypeStruct((H, S, LANES), F32)),
        grid=(H // HP, S // tk),
        in_specs=[pl.BlockSpec((HP, S, HEAD_PAD), lambda hd, j: (hd, 0, 0)),
                  pl.BlockSpec((HP, tk, HEAD_PAD), lambda hd, j: (hd, j, 0)),
                  pl.BlockSpec((HP, tk, V_DIM), lambda hd, j: (hd, j, 0)),
                  pl.BlockSpec((S, HP * V_DIM), lambda hd, j: (0, hd)),
                  pl.BlockSpec((HP, nq, 1, tq), lambda hd, j: (hd, 0, 0, 0)),
                  pl.BlockSpec((HP, nq, 1, tq), lambda hd, j: (hd, 0, 0, 0))],
        out_specs=(pl.BlockSpec((S, HP * HEAD_PAD), lambda hd, j: (0, hd)), pl.BlockSpec((tk, HP * HEAD_PAD), lambda hd, j: (j, hd)),
                   pl.BlockSpec((HP, tk, LANES), lambda hd, j: (hd, j, 0))),
        scratch_shapes=[pltpu.VMEM((HP, tk, HEAD_PAD), F32), pltpu.VMEM((HP, tk, V_DIM), F32)],
        compiler_params=_cparams("parallel", "arbitrary"),
    )(q_full, k_full, v_aug, do, lse_rows, delta_rows)


def _mla_unrope_q(dq, tabs, *, tm, name):
    S, W = dq.shape

    def body(dq_ref, c_ref, s1_ref, s2_ref, o_ref):
        for hd in range(N_HEADS):
            lo = hd * HEAD_PAD
            o_ref[:, lo:lo + QK_NOPE] = (dq_ref[:, lo:lo + QK_NOPE] * ATTN_SCALE).astype(BF16)
            o_ref[:, lo + QK_NOPE:lo + HEAD_PAD] = _unrope(dq_ref[:, lo + QK_NOPE:lo + HEAD_PAD] * ATTN_SCALE, c_ref[...], s1_ref[...], s2_ref[...]).astype(BF16)

    tab = pl.BlockSpec((tm, LANES), lambda i: (i, 0))
    row = pl.BlockSpec((tm, W), lambda i: (i, 0))
    return pl.pallas_call(
        body, name=name,
        out_shape=jax.ShapeDtypeStruct((S, W), BF16),
        grid=(S // tm,),
        in_specs=[row, tab, tab, tab], out_specs=row,
        compiler_params=_cparams("parallel"),
    )(dq, *tabs)


def _mla_latent_bwd(h, dq_n, dkv_n, dkr, dz, q_norm, kv_norm, tabs, *, tm, name):
    S = h.shape[0]

    def body(kv_ref, q_ref, dqn_ref, dkvn_ref, dkr_ref, dz_ref, qg_ref, kvg_ref, c_ref, s1_ref, s2_ref, dh_ref, dqg_ref, dkvg_ref):
        i = pl.program_id(0)
        dq_lat, dqg = _rms_bwd_math(q_ref[...], qg_ref[...], dqn_ref[...])
        dkv_lat, dkvg = _rms_bwd_math(kv_ref[...], kvg_ref[...], dkvn_ref[...])
        dkr_sum = dkr_ref[0]
        for hd in range(1, N_HEADS):
            dkr_sum = dkr_sum + dkr_ref[hd]
        dh_ref[:, :D_INNER] = dz_ref[...]
        dh_ref[:, D_INNER:D_INNER + KV_LORA] = dkv_lat.astype(BF16)
        dh_ref[:, D_INNER + KV_LORA:D_INNER + KV_LORA + Q_LORA] = dq_lat.astype(BF16)
        dh_ref[:, D_INNER + KV_LORA + Q_LORA:] = _unrope(dkr_sum, c_ref[...], s1_ref[...], s2_ref[...]).astype(BF16)

        @pl.when(i == 0)
        def _():
            dqg_ref[...] = dqg
            dkvg_ref[...] = dkvg

        @pl.when(i > 0)
        def _():
            dqg_ref[...] += dqg
            dkvg_ref[...] += dkvg

    tab = pl.BlockSpec((tm, LANES), lambda i: (i, 0))
    qvec = pl.BlockSpec((1, Q_LORA), lambda i: (0, 0))
    kvvec = pl.BlockSpec((1, KV_LORA), lambda i: (0, 0))
    return pl.pallas_call(
        body, name=name,
        out_shape=(jax.ShapeDtypeStruct((S, MLA_IN_PAD), BF16), jax.ShapeDtypeStruct((1, Q_LORA), F32),
                   jax.ShapeDtypeStruct((1, KV_LORA), F32)),
        grid=(S // tm,),
        in_specs=[pl.BlockSpec((tm, KV_LORA), lambda i: (i, KV_LAT_BLK)), pl.BlockSpec((tm, Q_LORA), lambda i: (i, Q_LAT_BLK)),
                  pl.BlockSpec((tm, Q_LORA), lambda i: (i, 0)), pl.BlockSpec((tm, KV_LORA), lambda i: (i, 0)),
                  pl.BlockSpec((N_HEADS, tm, LANES), lambda i: (0, i, 0)), pl.BlockSpec((tm, D_INNER), lambda i: (i, 0)),
                  qvec, kvvec, tab, tab, tab],
        out_specs=(pl.BlockSpec((tm, MLA_IN_PAD), lambda i: (i, 0)), qvec, kvvec),
        compiler_params=_cparams("arbitrary"),
    )(h, h, dq_n, dkv_n, dkr, dz, q_norm, kv_norm, *tabs)


def _adamw(w, g, m, v, *, name):
    R, C = w.shape
    tr = R
    for cand in (512, 256, 128, 64, 32, 16, 8):
        if R % cand == 0 and cand * C * 4 <= 2 * 1024 * 1024:
            tr = cand
            break

    def body(w_ref, g_ref, m_ref, v_ref, d_ref, nm_ref, nv_ref):
        d_ref[...], nm_ref[...], nv_ref[...] = _adam_math(w_ref[...], g_ref[...], m_ref[...], v_ref[...])

    spec = pl.BlockSpec((tr, C), lambda i: (i, 0))
    sds = jax.ShapeDtypeStruct((R, C), F32)
    return pl.pallas_call(
        body, name=name, out_shape=(sds, sds, sds), grid=(R // tr,),
        in_specs=[spec] * 4, out_specs=(spec,) * 3,
        compiler_params=_cparams("parallel"),
    )(w, g, m, v)


def _adamw_rows(w, m, v, srcs, *, name):
    nj, R, C = w.shape
    assert len(srcs) == nj and C % PACK_COLS == 0
    tr = min(R, 256)
    assert R % tr == 0 and all(r0 % tr == 0 for _, r0 in srcs)

    def body(*refs):
        w_ref, m_ref, v_ref = refs[:3]
        g_refs = refs[3:3 + nj]
        go_ref, d_ref, nm_ref, nv_ref = refs[3 + nj:]
        gv = g_refs[0][...]
        for jj in range(1, nj):
            gv = jnp.where(pl.program_id(0) == jj, g_refs[jj][...], gv)
        d, m_new, v_new = _adam_math(w_ref[...], gv, m_ref[...], v_ref[...])
        go_ref[...] = gv
        d_ref[...] = d
        nm_ref[...] = m_new
        nv_ref[...] = v_new

    nat = pl.BlockSpec((None, tr, PACK_COLS), lambda j, cb, i: (j, i, cb))

    def src_spec(jj, r0):
        return pl.BlockSpec((tr, PACK_COLS), lambda j, cb, i: (jnp.where(j == jj, (r0 + cb * R) // tr + i, r0 // tr), 0))

    sds = jax.ShapeDtypeStruct((nj, R, C), F32)
    return pl.pallas_call(
        body, name=name, out_shape=(sds,) * 4, grid=(nj, C // PACK_COLS, R // tr),
        in_specs=[nat] * 3 + [src_spec(jj, r0) for jj, (_, r0) in enumerate(srcs)], out_specs=(nat,) * 4,
        compiler_params=_cparams("parallel", "parallel", "parallel"),
    )(w, m, v, *[rows for rows, _ in srcs])


HBM_SPEC = pl.BlockSpec(memory_space=pltpu.HBM)
VMEM_SPEC = pl.BlockSpec(memory_space=pltpu.VMEM)
SEM_SPEC = pl.BlockSpec(memory_space=pltpu.SEMAPHORE)
ANY_SPEC = pl.BlockSpec(memory_space=pl.ANY)
SPLIT_EFFECT = pltpu.SideEffectType.DATAFLOW_SIDE_EFFECTING


def _place():
    return lax.axis_index("x"), lax.axis_index("y"), lax.axis_index("c")


def _other_chips(x, y):
    return [(1 - x, y), (x, 1 - y), (1 - x, 1 - y)]


def _remote(src, dst, send_sem, recv_sem, dev):
    return pltpu.make_async_remote_copy(src_ref=src, dst_ref=dst, send_sem=send_sem, recv_sem=recv_sem,
                                        device_id=dev, device_id_type=MESH)


def _all_gather_big(wp, *, name):
    R, C = wp.shape
    H = R // 2

    def body(w_ref, out_ref, send_sems, recv_sems):
        x, y, c = _place()
        sib = (x, y, 1 - c)
        chips = _other_chips(x, y)

        def blk(px, py, half):
            return out_ref.at[2 * px + py, pl.ds(pl.multiple_of(half * H, 16), H), :]

        my_half = w_ref.at[pl.ds(pl.multiple_of(c * H, 16), H), :]
        first = [_remote(my_half, blk(x, y, c), send_sems.at[r], recv_sems.at[r], (*chip, c)) for r, chip in enumerate(chips)]
        own = _remote(w_ref, out_ref.at[2 * x + y], send_sems.at[6], recv_sems.at[6], sib)
        for cp in first + [own]:
            cp.start()
        passed = [_remote(blk(*chip, c), blk(*chip, c), send_sems.at[3 + r], recv_sems.at[3 + r], sib) for r, chip in enumerate(chips)]
        for r, chip in enumerate(chips):
            _remote(my_half, blk(*chip, c), send_sems.at[r], recv_sems.at[r], (*chip, c)).wait_recv()
            passed[r].start()
        for r, chip in enumerate(chips):
            _remote(my_half, blk(*chip, 1 - c), send_sems.at[3 + r], recv_sems.at[3 + r], sib).wait_recv()
        own.wait_recv()
        for cp in first + passed + [own]:
            cp.wait_send()

    return pl.pallas_call(
        body, name=name,
        out_shape=jax.ShapeDtypeStruct((N_CHIPS, R, C), wp.dtype),
        in_specs=[HBM_SPEC], out_specs=HBM_SPEC,
        scratch_shapes=[pltpu.SemaphoreType.DMA((7,)), pltpu.SemaphoreType.DMA((7,))],
    )(wp)


def _ag_ici_start(wp, after, *, name):
    R, C = wp.shape
    H = R // 2

    def body(w_ref, land_ref, after_ref, send_sems, recv_sems, w_thru, land_thru, token):
        x, y, c = _place()
        rows = pl.ds(pl.multiple_of(c * H, 16), H)
        for r, chip in enumerate(_other_chips(x, y)):
            _remote(w_ref.at[rows, :], land_ref.at[2 * x + y, rows, :], send_sems.at[r], recv_sems.at[r], (*chip, c)).start()
        token[...] = jnp.zeros(token.shape, F32)

    land = lax.empty((N_CHIPS, R, C), wp.dtype)
    return pl.pallas_call(
        body, name=name,
        out_shape=(pltpu.SemaphoreType.DMA((3,)), pltpu.SemaphoreType.DMA((3,)), pltpu.HBM(wp.shape, wp.dtype), pltpu.HBM(land.shape, land.dtype),
                   jax.ShapeDtypeStruct((8, LANES), F32)),
        in_specs=(HBM_SPEC, HBM_SPEC, ANY_SPEC), out_specs=(SEM_SPEC, SEM_SPEC, HBM_SPEC, HBM_SPEC, VMEM_SPEC),
        input_output_aliases={0: 2, 1: 3},
        compiler_params=pltpu.CompilerParams(has_side_effects=SPLIT_EFFECT),
    )(pltpu.with_memory_space_constraint(wp, pltpu.HBM), pltpu.with_memory_space_constraint(land, pltpu.HBM), after)


def _ag_ici_wait(send_sems, recv_sems, w_thru, land_thru, after, *, name):
    R, C = w_thru.shape
    H = R // 2

    def body(w_ref, land_ref, send_sems, recv_sems, after_ref, w_out, land_out):
        x, y, c = _place()
        rows = pl.ds(pl.multiple_of(c * H, 16), H)
        for r, (px, py) in enumerate(_other_chips(x, y)):
            cp = _remote(w_ref.at[rows, :], land_ref.at[2 * px + py, rows, :], send_sems.at[r], recv_sems.at[r], (px, py, c))
            cp.wait_send()
            cp.wait_recv()

    return pl.pallas_call(
        body, name=name,
        out_shape=(pltpu.HBM(w_thru.shape, w_thru.dtype), pltpu.HBM(land_thru.shape, land_thru.dtype)),
        in_specs=(HBM_SPEC, HBM_SPEC, SEM_SPEC, SEM_SPEC, ANY_SPEC), out_specs=(HBM_SPEC, HBM_SPEC),
        input_output_aliases={0: 0, 1: 1},
        compiler_params=pltpu.CompilerParams(has_side_effects=SPLIT_EFFECT),
    )(w_thru, land_thru, send_sems, recv_sems, after)


def _ag_sibling_forward(land, wp, *, name):
    _, R, C = land.shape
    H = R // 2

    def body(land_ref, w_ref, out_ref, send_sems, recv_sems):
        x, y, c = _place()
        sib = (x, y, 1 - c)
        mine = pl.ds(pl.multiple_of(c * H, 16), H)
        theirs = pl.ds(pl.multiple_of((1 - c) * H, 16), H)
        chips = _other_chips(x, y)
        sends = [_remote(land_ref.at[2 * px + py, mine, :], out_ref.at[2 * px + py, mine, :], send_sems.at[r], recv_sems.at[r], sib)
                 for r, (px, py) in enumerate(chips)]
        sends.append(_remote(w_ref, out_ref.at[2 * x + y], send_sems.at[3], recv_sems.at[3], sib))
        for cp in sends:
            cp.start()
        for r, (px, py) in enumerate(chips):
            _remote(land_ref.at[2 * px + py, mine, :], out_ref.at[2 * px + py, theirs, :], send_sems.at[r], recv_sems.at[r], sib).wait_recv()
        sends[3].wait_recv()
        for cp in sends:
            cp.wait_send()

    return pl.pallas_call(
        body, name=name, out_shape=jax.ShapeDtypeStruct(land.shape, land.dtype),
        in_specs=[HBM_SPEC, HBM_SPEC], out_specs=HBM_SPEC, input_output_aliases={0: 0},
        scratch_shapes=[pltpu.SemaphoreType.DMA((4,)), pltpu.SemaphoreType.DMA((4,))],
    )(land, wp)


def _rs_sibling_swap(g, *, name):
    _, R, C = g.shape
    H = R // 2

    def body(g_ref, theirs_ref, send_sems, recv_sems):
        x, y, c = _place()
        sib = (x, y, 1 - c)
        copies = [_remote(g_ref.at[k, pl.ds(pl.multiple_of((1 - c) * H, 16), H), :], theirs_ref.at[k],
                          send_sems.at[k], recv_sems.at[k], sib) for k in range(N_CHIPS)]
        for cp in copies:
            cp.start()
        for cp in copies:
            cp.wait()

    return pl.pallas_call(
        body, name=name, out_shape=jax.ShapeDtypeStruct((N_CHIPS, H, C), g.dtype),
        in_specs=[HBM_SPEC], out_specs=HBM_SPEC,
        scratch_shapes=[pltpu.SemaphoreType.DMA((N_CHIPS,)), pltpu.SemaphoreType.DMA((N_CHIPS,))],
    )(g)


def _row_tile(h):
    best = 16
    for d in range(16, 1025, 16):
        if h % d == 0:
            best = d
    return best


def _add2_bf16(g, theirs, core, *, name):
    K, H, C = theirs.shape
    tr = _row_tile(H)
    nb = H // tr

    def body(c_ref, a_ref, b_ref, o_ref):
        o_ref[...] = (a_ref[...].astype(F32) + b_ref[...].astype(F32)).astype(o_ref.dtype)

    spec = pl.BlockSpec((None, tr, C), lambda k, i, c: (k, i, 0))
    return pl.pallas_call(
        body, name=name, out_shape=jax.ShapeDtypeStruct((K, H, C), theirs.dtype),
        grid_spec=pltpu.PrefetchScalarGridSpec(
            num_scalar_prefetch=1, grid=(K, nb),
            in_specs=[pl.BlockSpec((None, tr, C), lambda k, i, c: (k, c[0] * nb + i, 0)), spec], out_specs=spec),
        compiler_params=_cparams("parallel", "parallel"),
    )(core, g, theirs)


def _rs_chip_exchange_start(p, *, name):
    _, H, C = p.shape

    def body(p_ref, land_ref, send_sems, recv_sems, p_thru, land_thru, token):
        x, y, c = _place()
        for r, (px, py) in enumerate(_other_chips(x, y)):
            _remote(p_ref.at[2 * px + py], land_ref.at[r], send_sems.at[r], recv_sems.at[r], (px, py, c)).start()
        token[...] = jnp.zeros(token.shape, F32)

    land = lax.empty((3, H, C), p.dtype)
    return pl.pallas_call(
        body, name=name,
        out_shape=(pltpu.SemaphoreType.DMA((3,)), pltpu.SemaphoreType.DMA((3,)), pltpu.HBM(p.shape, p.dtype), pltpu.HBM(land.shape, land.dtype),
                   jax.ShapeDtypeStruct((8, LANES), F32)),
        in_specs=(HBM_SPEC, HBM_SPEC), out_specs=(SEM_SPEC, SEM_SPEC, HBM_SPEC, HBM_SPEC, VMEM_SPEC),
        input_output_aliases={0: 2, 1: 3},
        compiler_params=pltpu.CompilerParams(has_side_effects=SPLIT_EFFECT),
    )(pltpu.with_memory_space_constraint(p, pltpu.HBM), pltpu.with_memory_space_constraint(land, pltpu.HBM))


def _rs_chip_exchange_wait(send_sems, recv_sems, p_thru, land_thru, after, *, name):
    def body(p_ref, land_ref, send_sems, recv_sems, after_ref, p_out, land_out):
        x, y, c = _place()
        for r, (px, py) in enumerate(_other_chips(x, y)):
            cp = _remote(p_ref.at[2 * px + py], land_ref.at[r], send_sems.at[r], recv_sems.at[r], (px, py, c))
            cp.wait_send()
            cp.wait_recv()

    return pl.pallas_call(
        body, name=name,
        out_shape=(pltpu.HBM(p_thru.shape, p_thru.dtype), pltpu.HBM(land_thru.shape, land_thru.dtype)),
        in_specs=(HBM_SPEC, HBM_SPEC, SEM_SPEC, SEM_SPEC, ANY_SPEC), out_specs=(HBM_SPEC, HBM_SPEC),
        input_output_aliases={0: 0, 1: 1},
        compiler_params=pltpu.CompilerParams(has_side_effects=SPLIT_EFFECT),
    )(p_thru, land_thru, send_sems, recv_sems, after)


def _add4_f32(p, recv, chip_core, *, name):
    _, H, C = p.shape
    tr = _row_tile(H)
    nb = H // tr

    def body(s_ref, o_ref, r_ref, out_ref):
        out_ref[...] = ((o_ref[...].astype(F32) + r_ref[0].astype(F32)) + r_ref[1].astype(F32)) + r_ref[2].astype(F32)

    return pl.pallas_call(
        body, name=name, out_shape=jax.ShapeDtypeStruct((2 * H, C), F32),
        grid_spec=pltpu.PrefetchScalarGridSpec(
            num_scalar_prefetch=1, grid=(nb,),
            in_specs=[pl.BlockSpec((None, tr, C), lambda i, s: (s[0], i, 0)), pl.BlockSpec((3, tr, C), lambda i, s: (0, i, 0))],
            out_specs=pl.BlockSpec((tr, C), lambda i, s: (s[1] * nb + i, 0))),
        compiler_params=_cparams("parallel"),
    )(chip_core, p, recv)


def _rs_sibling_join(f, *, name):
    R, C = f.shape
    H = R // 2

    def body(f_ref, out_ref, send_sem, recv_sem):
        x, y, c = _place()
        sib = (x, y, 1 - c)
        mine = pl.ds(pl.multiple_of(c * H, 8), H)
        theirs = pl.ds(pl.multiple_of((1 - c) * H, 8), H)
        cp = _remote(f_ref.at[mine, :], out_ref.at[mine, :], send_sem, recv_sem, sib)
        cp.start()
        _remote(f_ref.at[mine, :], out_ref.at[theirs, :], send_sem, recv_sem, sib).wait_recv()
        cp.wait_send()

    return pl.pallas_call(
        body, name=name, out_shape=jax.ShapeDtypeStruct((R, C), f.dtype),
        in_specs=[HBM_SPEC], out_specs=HBM_SPEC, input_output_aliases={0: 0},
        scratch_shapes=[pltpu.SemaphoreType.DMA, pltpu.SemaphoreType.DMA],
    )(f)


SMALL_GATHER = (("pool_norm", 2, 256), ("pool_scale", 2, 512), ("conv_w", 3, 512), ("mla_norm", 1, 256),
                ("mla_q_norm", 1, 96), ("mla_kv_norm", 1, 64))
SMALL_SLOT = (16, 512)


def _gather_small(shards, *, name):
    def body(pn_ref, ps_ref, cw_ref, mn_ref, qn_ref, kn_ref, pn_o, ps_o, cw_o, mn_o, qn_o, kn_o, all_ref, send_sems, recv_sems):
        x, y, c = _place()
        mine = 2 * x + y
        all_ref[mine] = jnp.zeros(SMALL_SLOT, F32)
        all_ref[mine, 0:2, 0:256] = pn_ref[...]
        all_ref[mine, 2:4, :] = ps_ref[...]
        all_ref[mine, 4:7, :] = cw_ref[0]
        all_ref[mine, 7:8, 0:256] = mn_ref[...]
        all_ref[mine, 8:9, 0:96] = qn_ref[...]
        all_ref[mine, 9:10, 0:64] = kn_ref[...]
        chips = _other_chips(x, y)
        sends = [_remote(all_ref.at[mine], all_ref.at[mine], send_sems.at[r], recv_sems.at[r], (*chip, c)) for r, chip in enumerate(chips)]
        for cp in sends:
            cp.start()
        for r, (px, py) in enumerate(chips):
            _remote(all_ref.at[mine], all_ref.at[2 * px + py], send_sems.at[r], recv_sems.at[r], (px, py, c)).wait_recv()
        for cp in sends:
            cp.wait_send()
        for k in range(N_CHIPS):
            pn_o[:, k * 256:(k + 1) * 256] = all_ref[k, 0:2, 0:256]
            ps_o[:, k * 512:(k + 1) * 512] = all_ref[k, 2:4, :]
            cw_o[0, :, k * 512:(k + 1) * 512] = all_ref[k, 4:7, :]
            mn_o[:, k * 256:(k + 1) * 256] = all_ref[k, 7:8, 0:256]
            qn_o[k] = all_ref[k, 8:9, 0:96]
            kn_o[k] = all_ref[k, 9:10, 0:64]

    sds = lambda *shape: jax.ShapeDtypeStruct(shape, F32)
    out = pl.pallas_call(
        body, name=name,
        out_shape=(sds(2, 1024), sds(2, 2048), sds(1, 3, 2048), sds(1, 1024), sds(N_CHIPS, 1, 96), sds(N_CHIPS, 1, 64)),
        in_specs=[VMEM_SPEC] * 6, out_specs=(VMEM_SPEC,) * 6,
        scratch_shapes=[pltpu.VMEM((N_CHIPS,) + SMALL_SLOT, F32), pltpu.SemaphoreType.DMA((3,)), pltpu.SemaphoreType.DMA((3,))],
    )(*[shards[n] for n, _, _ in SMALL_GATHER])
    full = dict(zip([n for n, _, _ in SMALL_GATHER], out))
    full["mla_q_norm"] = full["mla_q_norm"].reshape(1, Q_LORA)
    full["mla_kv_norm"] = full["mla_kv_norm"].reshape(1, KV_LORA)
    return full


SMALL_REDUCE = (("pool_norm_0", 0, 1, 1024), ("pool_norm_1", 1, 1, 1024), ("pool_scale_0", 2, 1, 2048), ("pool_scale_1", 3, 1, 2048),
                ("conv_norm", 4, 1, 1024), ("mla_norm", 5, 1, 1024), ("mla_q_norm", 6, 1, 384), ("mla_kv_norm", 7, 1, 256),
                ("conv_w", 8, 8, 2048), ("final_norm", 16, 1, 1024))
REDUCE_SLOT = (24, 2048)


def _reduce_small(parts, *, name):
    keys = [k for k, _, _, _ in SMALL_REDUCE]

    def body(*refs):
        ins = dict(zip(keys, refs[:len(keys)]))
        pn_o, ps_o, cn_o, cw_o, mn_o, qn_o, kn_o, fn_o, all_ref, send_sems, recv_sems = refs[len(keys):]
        x, y, c = _place()
        me = 4 * x + 2 * y + c
        all_ref[me] = jnp.zeros(REDUCE_SLOT, F32)
        for k, r0, nr, wd in SMALL_REDUCE:
            all_ref[me, r0:r0 + nr, 0:wd] = ins[k][...]
        peers = []
        for rel in range(1, N_DEV):
            dx, dy, dc = (rel >> 2) & 1, (rel >> 1) & 1, rel & 1
            peers.append((1 - x if dx else x, 1 - y if dy else y, 1 - c if dc else c))
        sends = [_remote(all_ref.at[me], all_ref.at[me], send_sems.at[k], recv_sems.at[k], peer) for k, peer in enumerate(peers)]
        for cp in sends:
            cp.start()
        for k, (px, py, pc) in enumerate(peers):
            _remote(all_ref.at[me], all_ref.at[4 * px + 2 * py + pc], send_sems.at[k], recv_sems.at[k], (px, py, pc)).wait_recv()
        for cp in sends:
            cp.wait_send()

        def total(r0, nr, wd):
            acc = all_ref[0, r0:r0 + nr, 0:wd]
            for d in range(1, N_DEV):
                acc = acc + all_ref[d, r0:r0 + nr, 0:wd]
            return acc

        pn_o[0:1, :] = total(0, 1, 1024)
        pn_o[1:2, :] = total(1, 1, 1024)
        ps_o[0:1, :] = total(2, 1, 2048)
        ps_o[1:2, :] = total(3, 1, 2048)
        cn_o[...] = total(4, 1, 1024)
        mn_o[...] = total(5, 1, 1024)
        qn_o[...] = total(6, 1, Q_LORA)
        kn_o[...] = total(7, 1, KV_LORA)
        cw_o[0] = total(8, 3, 2048)
        fn_o[...] = total(16, 1, 1024)

    sds = lambda *shape: jax.ShapeDtypeStruct(shape, F32)
    out = pl.pallas_call(
        body, name=name,
        out_shape=(sds(2, 1024), sds(2, 2048), sds(1, 1024), sds(1, 3, 2048), sds(1, 1024), sds(1, Q_LORA), sds(1, KV_LORA), sds(1, 1024)),
        in_specs=[VMEM_SPEC] * len(keys), out_specs=(VMEM_SPEC,) * 8,
        scratch_shapes=[pltpu.VMEM((N_DEV,) + REDUCE_SLOT, F32), pltpu.SemaphoreType.DMA((N_DEV - 1,)), pltpu.SemaphoreType.DMA((N_DEV - 1,))],
    )(*[parts[k] for k in keys])
    return dict(zip(("pool_norm", "pool_scale", "conv_norm", "conv_w", "mla_norm", "mla_q_norm", "mla_kv_norm", "final_norm"), out))


def _adam_math(w, g, m, v):
    m_new = ADAM_B1 * m + (1.0 - ADAM_B1) * g
    v_new = ADAM_B2 * v + (1.0 - ADAM_B2) * (g * g)
    m_hat = m_new / (1.0 - ADAM_B1 ** ADAM_STEP)
    v_hat = v_new / (1.0 - ADAM_B2 ** ADAM_STEP)
    return -ADAM_LR * (m_hat / (jnp.sqrt(v_hat) + ADAM_EPS) + ADAM_WD * w), m_new, v_new


def _adamw_small(w, m, v, g_full, chip, *, name):
    shp = {n: w[n].shape for n in SMALL}
    whole = lambda s: pl.BlockSpec(s, lambda i, c: (0,) * len(s))
    g_in, g_specs = {}, {}
    for n in SMALL:
        if not SMALL_SHARDED[n]:
            g_in[n], g_specs[n] = g_full[n].reshape(shp[n]), whole(shp[n])
        elif shp[n][-1] % LANES:
            g_in[n] = g_full[n].reshape(N_CHIPS, 1, shp[n][-1])
            g_specs[n] = pl.BlockSpec((None,) + shp[n], lambda i, c: (c[0], 0, 0))
        else:
            g_in[n] = g_full[n]
            nd = len(shp[n])
            g_specs[n] = pl.BlockSpec(shp[n], lambda i, c, nd=nd: (0,) * (nd - 1) + (c[0],))

    def body(c_ref, *refs):
        k = len(SMALL)
        w_r, m_r, v_r, g_r = refs[0:k], refs[k:2 * k], refs[2 * k:3 * k], refs[3 * k:4 * k]
        go_r, d_r, nm_r, nv_r = refs[4 * k:5 * k], refs[5 * k:6 * k], refs[6 * k:7 * k], refs[7 * k:8 * k]
        for i in range(k):
            gv = g_r[i][...]
            d, m_new, v_new = _adam_math(w_r[i][...], gv, m_r[i][...], v_r[i][...])
            go_r[i][...] = gv
            d_r[i][...] = d
            nm_r[i][...] = m_new
            nv_r[i][...] = v_new

    nat = [whole(shp[n]) for n in SMALL]
    out_sds = tuple(jax.ShapeDtypeStruct(shp[n], F32) for n in SMALL)
    out = pl.pallas_call(
        body, name=name, out_shape=out_sds * 4,
        grid_spec=pltpu.PrefetchScalarGridSpec(
            num_scalar_prefetch=1, grid=(1,),
            in_specs=nat * 3 + [g_specs[n] for n in SMALL], out_specs=tuple(nat) * 4),
        compiler_params=_cparams("arbitrary"),
    )(chip, *[w[n] for n in SMALL], *[m[n] for n in SMALL], *[v[n] for n in SMALL], *[g_in[n] for n in SMALL])
    k = len(SMALL)
    return tuple(dict(zip(SMALL, out[j * k:(j + 1) * k])) for j in range(4))


BIG = ("pool_w_in", "pool_w_grp", "pool_w_out", "conv_w_in", "conv_w_out", "mla_w_in", "mla_w_q_up", "mla_w_kv_up", "mla_w_out")
BIG_SHARD_AXIS = {"pool_w_in": 2, "pool_w_grp": 2, "pool_w_out": 1, "conv_w_in": 2, "conv_w_out": 1,
                  "mla_w_in": 2, "mla_w_q_up": 2, "mla_w_kv_up": 2, "mla_w_out": 1}
GATHER_LAYOUT = {
    "p0": ((("pool_w_in", 0), 0, "cols"), (("pool_w_out", 0), 1024, "rows"), (("pool_w_grp", 0), 1536, "flat")),
    "cv": ((("conv_w_in", 0), 0, "cols"), (("conv_w_out", 0), 2048, "rows")),
    "ml": ((("pool_w_in", 1), 0, "cols"), (("mla_w_out", 0), 1024, "rows"), (("pool_w_out", 1), 1536, "rows"),
           (("mla_w_kv_up", 0), 2048, "cols"), (("pool_w_grp", 1), 2304, "flat"), (("mla_w_q_up", 0), 2560, "flat"),
           (("mla_w_in", 0), 2848, "flat")),
}
REDUCE_LAYOUT = {
    "late": ((("conv_w_in", 0), 0, "cols"), (("pool_w_in", 1), 2048, "cols"), (("conv_w_out", 0), 3072, "rows"),
             (("mla_w_out", 0), 3584, "rows"), (("pool_w_out", 1), 4096, "rows"), (("mla_w_kv_up", 0), 4608, "cols"),
             (("pool_w_grp", 1), 4864, "flat"), (("mla_w_q_up", 0), 5120, "flat"), (("mla_w_in", 0), 5408, "flat")),
    "first": ((("pool_w_in", 0), 0, "cols"), (("pool_w_out", 0), 1024, "rows"), (("pool_w_grp", 0), 1536, "flat")),
}
PACK_ROW_ALIGN = 32
RS_ROW_ALIGN = 512


def _slot_rows(layout, shard_shape, align):
    where, end = {}, 0
    for piece, r0, kind in layout:
        n = 1
        for d in shard_shape(piece):
            n *= d
        assert r0 >= end and n % PACK_COLS == 0, (piece, r0, end)
        where[piece] = (r0, n // PACK_COLS, kind)
        end = r0 + n // PACK_COLS
    return end + (-end) % align, where


def _as_slot_rows(shard, kind):
    if kind == "cols":
        k, n = shard.shape
        return shard.reshape(k, n // PACK_COLS, PACK_COLS).swapaxes(0, 1).reshape(-1, PACK_COLS)
    return shard.reshape(-1, PACK_COLS)


def _pack_slot(shards, layout, rows, dtype):
    parts, end = [], 0
    for piece, r0, kind in layout:
        if r0 > end:
            parts.append(jnp.zeros((r0 - end, PACK_COLS), dtype))
        parts.append(_as_slot_rows(shards[piece], kind).astype(dtype))
        end = r0 + parts[-1].shape[0]
    if rows > end:
        parts.append(jnp.zeros((rows - end, PACK_COLS), dtype))
    return jnp.concatenate(parts, axis=0)


SMALL = ("pool_norm", "pool_scale", "conv_norm", "conv_w", "mla_norm", "mla_q_norm", "mla_kv_norm", "final_norm")
SMALL_SHARDED = {"pool_norm": True, "pool_scale": True, "conv_norm": False, "conv_w": True, "mla_norm": True,
                 "mla_q_norm": True, "mla_kv_norm": True, "final_norm": False}


def _rope_tables(positions):
    inv_freq = ROPE_BASE ** (-jnp.arange(0, QK_ROPE, 2, dtype=F32) / QK_ROPE)
    ang = positions.astype(F32).reshape(-1, 1) * inv_freq
    cos, sin = jnp.cos(ang), jnp.sin(ang)
    z32 = jnp.zeros_like(cos)
    z64 = jnp.concatenate([z32, z32], axis=1)
    return (jnp.concatenate([cos, cos, z64], axis=1), jnp.concatenate([-sin, z32, z64], axis=1),
            jnp.concatenate([z32, sin, z64], axis=1))


def _mla_in_to_padded(w):
    q, kv, kr, z = w[:, :Q_LORA], w[:, Q_LORA:Q_LORA + KV_LORA], w[:, Q_LORA + KV_LORA:Q_LORA + KV_LORA + QK_ROPE], w[:, Q_LORA + KV_LORA + QK_ROPE:]
    return jnp.concatenate([z, kv, q, kr, jnp.zeros((w.shape[0], MLA_IN_PAD - MLA_IN), w.dtype)], axis=1)


def _mla_in_from_padded(w):
    z, kv, q, kr = w[:, :D_INNER], w[:, D_INNER:D_INNER + KV_LORA], w[:, D_INNER + KV_LORA:D_INNER + KV_LORA + Q_LORA], w[:, D_INNER + KV_LORA + Q_LORA:D_INNER + KV_LORA + Q_LORA + QK_ROPE]
    return jnp.concatenate([q, kv, kr, z], axis=1)


def _q_up_to_padded(w):
    k = w.shape[0]
    return jnp.pad(w.reshape(k, N_HEADS, QK_NOPE + QK_ROPE), ((0, 0), (0, 0), (0, HEAD_PAD - QK_NOPE - QK_ROPE))).reshape(k, N_HEADS * HEAD_PAD)


def _q_up_from_padded(w):
    k = w.shape[0]
    return w.reshape(k, N_HEADS, HEAD_PAD)[:, :, :QK_NOPE + QK_ROPE].reshape(k, N_HEADS * (QK_NOPE + QK_ROPE))


def _local_step(x, positions, target, weights_for, ws, sink):
    S = x.shape[0]
    tm = min(512, S)
    te = min(256, S)
    tq = min(512, S)
    tabs = _rope_tables(positions)
    gs = {}

    def mm_in(xn, w, name):
        n = w.shape[1]
        tn = PACK_COLS if isinstance(w, Packed) else _pick(n, 1536 if n == MLA_IN_PAD else 1024)
        return _mm(xn, w, tm=min(1024, S), tn=tn, tk=D_MODEL, name=name)

    def mm_out(y, w, res, name):
        return _mm(y, w, residual=res, tm=tm, tn=D_MODEL, tk=D_INNER // N_CHIPS if isinstance(w, Packed) else D_INNER, name=name)

    def mm_dx(dy, w, name, after=None):
        k, n = w.shape
        if isinstance(w, Packed):
            tn, tk = (k // N_CHIPS if w.kind == "rows" else min(k, 1024)), PACK_COLS
        else:
            tn, tk = _pick(k, 1024), _pick(n, 1408)
        return _mm(dy, w, trans_b=True, after=after, tm=min(1024, S), tn=tn, tk=tk, name=name)

    def mm_dw(piece, a, b, name, after=None, post=None):
        ka, nb = a.shape[1], b.shape[1]
        into = sink.dest(piece)
        if into is None:
            out = _mm(a, b, trans_a=True, out_dtype=BF16, after=after, tm=_pick(ka, 1024), tn=_pick(nb, 1408), tk=tm, name=name)
            sink.put(piece, out if post is None else post(out))
        else:
            rows = ka // N_CHIPS if into.kind == "rows" else min(ka, 1024)
            sink.put(piece, _mm(a, b, trans_a=True, after=after, into=into, tm=rows, tn=PACK_COLS, tk=tm, name=name))

    def pool_layer_fwd(xin, wts, j, tag):
        xn = _rms_fwd(xin, ws["pool_norm"][j:j + 1], tm=tm, name=f"{tag}_norm")
        h = mm_in(xn, wts[("pool_w_in", j)], f"{tag}_in")
        y = _pool_fwd(h, wts[("pool_w_grp", j)], ws["pool_scale"][j:j + 1], tm=te, name=f"{tag}_mix")
        xo = mm_out(y, wts[("pool_w_out", j)], xin, f"{tag}_out")
        return xo, (xin, xn, h, y)

    def pool_layer_bwd(dx, dxb, saved, wts, j, tag, after=None):
        xin, xn, h, y = saved
        dy = mm_dx(dxb, wts[("pool_w_out", j)], f"{tag}_dy", after)
        mm_dw(("pool_w_out", j), y, dxb, f"{tag}_dwo", after)
        pooled, dmixed, dpooled, dz, dsc = _pool_bwd1(h, dy, wts[("pool_w_grp", j)], ws["pool_scale"][j:j + 1], tm=te, name=f"{tag}_bmix")
        sink.put(("pool_w_grp", j), _grouped_tn(pooled, dmixed, tk=tm, name=f"{tag}_dwg"))
        dh = _pool_bwd2(dpooled, dz, tm=te, name=f"{tag}_bshift")
        dxn = mm_dx(dh, wts[("pool_w_in", j)], f"{tag}_dxn")
        mm_dw(("pool_w_in", j), xn, dh, f"{tag}_dwi")
        dxo, dxob, dg = _rms_bwd(xin, ws["pool_norm"][j:j + 1], dxn, dx, tm=tm, name=f"{tag}_bnorm")
        gs[f"pool_norm_{j}"], gs[f"pool_scale_{j}"] = dg, dsc
        return dxo, dxob

    w_p0 = weights_for("p0", None)
    x1, sv0 = pool_layer_fwd(x, w_p0, 0, "p0")

    w_cv = weights_for("cv", x1)
    xn1 = _rms_fwd(x1, ws["conv_norm"][0:1], tm=tm, name="cv_norm")
    h1 = mm_in(xn1, w_cv[("conv_w_in", 0)], "cv_in")
    cw = jnp.pad(ws["conv_w"][0], ((0, 5), (0, 0)))
    y1 = _conv_fwd(h1, cw, tm=te, name="cv_mix")
    x2 = mm_out(y1, w_cv[("conv_w_out", 0)], x1, "cv_out")

    w_ml = weights_for("ml", x2)
    w_mi = _mla_in_to_padded(w_ml[("mla_w_in", 0)])
    w_q = _q_up_to_padded(w_ml[("mla_w_q_up", 0)])
    w_kv = w_ml[("mla_w_kv_up", 0)]
    qg, kvg = ws["mla_q_norm"][0:1], ws["mla_kv_norm"][0:1]
    xn2 = _rms_fwd(x2, ws["mla_norm"][0:1], tm=tm, name="ml_norm")
    h2 = mm_in(xn2, w_mi, "ml_in")
    q_n, kv_n, krr = _mla_latent_fwd(h2, qg, kvg, tabs, tm=tm, name="ml_lat")
    q_full = _mla_q_up(q_n, w_q, tabs, tm=tm, name="ml_qup")
    k_full, v = _mla_kv_up(kv_n, w_kv, krr, tm=tm, name="ml_kvup")
    o, y2, lse = _flash_fwd(q_full, k_full, v, h2, tq=tq, name="ml_attn")
    x3 = mm_out(y2, w_ml[("mla_w_out", 0)], x2, "ml_out")

    x4, sv3 = pool_layer_fwd(x3, w_ml, 1, "p1")

    loss_part, dx, dxb, dgf = _final_loss(x4, ws["final_norm"].reshape(1, -1), target, tm=tm, name="final")
    gs["final_norm"] = dgf

    dx, dxb = pool_layer_bwd(dx, dxb, sv3, w_ml, 1, "p1")

    dy = mm_dx(dxb, w_ml[("mla_w_out", 0)], "ml_dy")
    mm_dw(("mla_w_out", 0), y2, dxb, "ml_dwo")
    do, dz, delta = _mla_gate_bwd(dy, o, h2, tm=tq, name="ml_bgate")
    dq, dkv, dkr = _flash_bwd(q_full, k_full, v, do, lse, delta, tq=tq, name="ml_battn")
    dq_pre = _mla_unrope_q(dq, tabs, tm=te, name="ml_bqrope")
    dq_n = mm_dx(dq_pre, w_q, "ml_dqn")
    mm_dw(("mla_w_q_up", 0), q_n, dq_pre, "ml_dwq", post=_q_up_from_padded)
    dkv_n = mm_dx(dkv, w_kv, "ml_dkvn")
    mm_dw(("mla_w_kv_up", 0), kv_n, dkv, "ml_dwkv")
    dh2, dqg, dkvg = _mla_latent_bwd(h2, dq_n, dkv_n, dkr, dz, qg, kvg, tabs, tm=te, name="ml_blat")
    dxn2 = mm_dx(dh2, w_mi, "ml_dxn")
    mm_dw(("mla_w_in", 0), xn2, dh2, "ml_dwi", post=_mla_in_from_padded)
    dx, dxb, dg2 = _rms_bwd(x2, ws["mla_norm"][0:1], dxn2, dx, tm=tm, name="ml_bnorm")
    gs["mla_norm"], gs["mla_q_norm"], gs["mla_kv_norm"] = dg2, dqg, dkvg

    dy = mm_dx(dxb, w_cv[("conv_w_out", 0)], "cv_dy")
    mm_dw(("conv_w_out", 0), y1, dxb, "cv_dwo")
    dh1, dcw = _conv_bwd(h1, dy, cw, tm=te, name="cv_bmix")
    dxn1 = mm_dx(dh1, w_cv[("conv_w_in", 0)], "cv_dxn")
    mm_dw(("conv_w_in", 0), xn1, dh1, "cv_dwi")
    dx, dxb, dg1 = _rms_bwd(x1, ws["conv_norm"][0:1], dxn1, dx, tm=tm, name="cv_bnorm")
    gs["conv_norm"], gs["conv_w"] = dg1, dcw

    dx, dxb = pool_layer_bwd(dx, dxb, sv0, w_p0, 0, "p0", after=sink.late_ready())
    return loss_part, dx, gs


def kernel(x, positions, pool_norm, pool_w_in, pool_w_grp, pool_scale, pool_w_out, conv_norm, conv_w_in, conv_w, conv_w_out, mla_norm, mla_w_in, mla_q_norm, mla_w_q_up, mla_kv_norm, mla_w_kv_up, mla_w_out, final_norm, loss_target, m_pool_norm, m_pool_w_in, m_pool_w_grp, m_pool_scale, m_pool_w_out, m_conv_norm, m_conv_w_in, m_conv_w, m_conv_w_out, m_mla_norm, m_mla_w_in, m_mla_q_norm, m_mla_w_q_up, m_mla_kv_norm, m_mla_w_kv_up, m_mla_w_out, m_final_norm, v_pool_norm, v_pool_w_in, v_pool_w_grp, v_pool_scale, v_pool_w_out, v_conv_norm, v_conv_w_in, v_conv_w, v_conv_w_out, v_mla_norm, v_mla_w_in, v_mla_q_norm, v_mla_w_q_up, v_mla_kv_norm, v_mla_w_kv_up, v_mla_w_out, v_final_norm):
    names = ("pool_norm", "pool_w_in", "pool_w_grp", "pool_scale", "pool_w_out", "conv_norm", "conv_w_in", "conv_w", "conv_w_out",
             "mla_norm", "mla_w_in", "mla_q_norm", "mla_w_q_up", "mla_kv_norm", "mla_w_kv_up", "mla_w_out", "final_norm")
    w = dict(zip(names, (pool_norm, pool_w_in, pool_w_grp, pool_scale, pool_w_out, conv_norm, conv_w_in, conv_w, conv_w_out,
                         mla_norm, mla_w_in, mla_q_norm, mla_w_q_up, mla_kv_norm, mla_w_kv_up, mla_w_out, final_norm)))
    m = dict(zip(names, (m_pool_norm, m_pool_w_in, m_pool_w_grp, m_pool_scale, m_pool_w_out, m_conv_norm, m_conv_w_in, m_conv_w, m_conv_w_out,
                         m_mla_norm, m_mla_w_in, m_mla_q_norm, m_mla_w_q_up, m_mla_kv_norm, m_mla_w_kv_up, m_mla_w_out, m_final_norm)))
    v = dict(zip(names, (v_pool_norm, v_pool_w_in, v_pool_w_grp, v_pool_scale, v_pool_w_out, v_conv_norm, v_conv_w_in, v_conv_w, v_conv_w_out,
                         v_mla_norm, v_mla_w_in, v_mla_q_norm, v_mla_w_q_up, v_mla_kv_norm, v_mla_w_kv_up, v_mla_w_out, v_final_norm)))
    chip = 2 * lax.axis_index("x") + lax.axis_index("y")
    core = lax.axis_index("c")

    core1 = core.astype(jnp.int32).reshape(1)
    chip_core = jnp.stack([chip, core]).astype(jnp.int32)
    shard_shape = lambda piece: w[piece[0]].shape[1:]
    shard_axis = lambda piece: BIG_SHARD_AXIS[piece[0]] - 1
    full_shape = lambda piece: tuple(d * (N_CHIPS if a == shard_axis(piece) else 1) for a, d in enumerate(shard_shape(piece)))

    w_bf = {n: w[n].astype(BF16) for n in BIG}
    gather_rows, gather_at, packs = {}, {}, {}
    for grp, layout in GATHER_LAYOUT.items():
        gather_rows[grp], gather_at[grp] = _slot_rows(layout, shard_shape, PACK_ROW_ALIGN)
        packs[grp] = _pack_slot({(n, j): w_bf[n][j] for (n, j), _, _ in layout}, layout, gather_rows[grp], BF16)

    def gathered_weights(grp, gathered):
        out = {}
        for piece, (r0, n, kind) in gather_at[grp].items():
            if kind == "flat":
                out[piece] = jnp.concatenate([gathered[k, r0:r0 + n].reshape(shard_shape(piece)) for k in range(N_CHIPS)], axis=shard_axis(piece))
            else:
                out[piece] = Packed(gathered, r0, kind, full_shape(piece))
        return out

    gathered_p0 = _all_gather_big(packs["p0"], name="ag_p0")
    cv_start = _ag_ici_start(packs["cv"], gathered_p0, name="ag_cv_start")
    ml_start = _ag_ici_start(packs["ml"], cv_start[4], name="ag_ml_start")
    in_flight = {"cv": cv_start, "ml": ml_start}

    def weights_for(grp, after):
        if grp == "p0":
            return gathered_weights(grp, gathered_p0)
        send_sems, recv_sems, w_thru, land, _ = in_flight[grp]
        w_thru, land = _ag_ici_wait(send_sems, recv_sems, w_thru, land, after, name=f"ag_{grp}_wait")
        return gathered_weights(grp, _ag_sibling_forward(land, w_thru, name=f"ag_{grp}_fwd"))

    ws = {"conv_norm": w["conv_norm"], "final_norm": w["final_norm"]}
    ws.update(_gather_small({n: w[n] for n, _, _ in SMALL_GATHER}, name="ag_small"))

    reduce_rows, reduce_at = {}, {}
    for grp, layout in REDUCE_LAYOUT.items():
        reduce_rows[grp], reduce_at[grp] = _slot_rows(layout, shard_shape, RS_ROW_ALIGN)
    group_of = {piece: grp for grp, layout in REDUCE_LAYOUT.items() for piece, _, _ in layout}

    class Sink:
        def __init__(self):
            self.buf = {grp: jnp.zeros((N_CHIPS, rows, PACK_COLS), BF16) for grp, rows in reduce_rows.items()}
            self.started = {}

        def dest(self, piece):
            grp = group_of[piece]
            r0, _, kind = reduce_at[grp][piece]
            return None if kind == "flat" else Packed(self.buf[grp], r0, kind, full_shape(piece))

        def put(self, piece, result):
            grp = group_of[piece]
            r0, n, kind = reduce_at[grp][piece]
            if kind == "flat":
                parts = jnp.split(result, N_CHIPS, axis=shard_axis(piece))
                result = lax.dynamic_update_slice(self.buf[grp], jnp.stack([p.reshape(n, PACK_COLS) for p in parts]), (0, r0, 0))
            self.buf[grp] = result

        def start(self, grp, tag):
            theirs = _rs_sibling_swap(self.buf[grp], name=f"{tag}_swap")
            chip_sum = _add2_bf16(self.buf[grp], theirs, core1, name=f"{tag}_add2")
            self.started[grp] = _rs_chip_exchange_start(chip_sum, name=f"{tag}_chips_start")
            return self.started[grp][4]

        def finish(self, grp, after, tag):
            send_sems, recv_sems, chip_sum, land, _ = self.started[grp]
            chip_sum, recv = _rs_chip_exchange_wait(send_sems, recv_sems, chip_sum, land, after, name=f"{tag}_chips_wait")
            half_sum = _add4_f32(chip_sum, recv, chip_core, name=f"{tag}_add4")
            return _rs_sibling_join(half_sum, name=f"{tag}_join")

        def late_ready(self):
            return self.start("late", "rsa")

    sink = Sink()

    loss_part, grad_x, gs = _local_step(x[0], positions, loss_target[0], weights_for, ws, sink)
    loss = lax.psum(loss_part[0, 0], ("x", "y", "c"))

    g_rows = {"late": sink.finish("late", grad_x, "rsa")}
    g_rows["first"] = sink.finish("first", sink.start("first", "rsb"), "rsb")

    gs_sum = _reduce_small(gs, name="ar_small")

    g, delta, new_m, new_v = {}, {}, {}, {}
    for n in BIG:
        nj = w[n].shape[0]
        where = [(group_of[(n, j)],) + reduce_at[group_of[(n, j)]][(n, j)] for j in range(nj)]
        if where[0][3] == "flat":
            g[n] = jnp.stack([g_rows[grp][r0:r0 + rows].reshape(w[n].shape[1:]) for grp, r0, rows, _ in where])
            shp = w[n].shape
            two_d = lambda a: a.reshape(-1, shp[-1])
            d_, m_, v_ = _adamw(two_d(w[n]), two_d(g[n]), two_d(m[n]), two_d(v[n]), name=f"adamw_{n}")
            delta[n], new_m[n], new_v[n] = d_.reshape(shp), m_.reshape(shp), v_.reshape(shp)
        else:
            g[n], delta[n], new_m[n], new_v[n] = _adamw_rows(w[n], m[n], v[n], [(g_rows[grp], r0) for grp, r0, _, _ in where], name=f"adamw_{n}")
    row = lambda d: {n: (d[n].reshape(1, -1) if d[n].ndim == 1 else d[n]) for n in SMALL}
    small_out = _adamw_small(row(w), row(m), row(v), gs_sum, chip.astype(jnp.int32).reshape(1), name="adamw_small")
    for dst, res in zip((g, delta, new_m, new_v), small_out):
        for n in SMALL:
            dst[n] = res[n].reshape(w[n].shape)

    return (loss, grad_x[None], *[g[n] for n in names], *[delta[n] for n in names],
            *[new_m[n] for n in names], *[new_v[n] for n in names])
```

```python
import functools

import jax
import jax.numpy as jnp
from jax import lax
from jax.experimental import pallas as pl
from jax.experimental.pallas import tpu as pltpu

F32 = jnp.float32
BF16 = jnp.bfloat16

D_MODEL = 1024
D_INNER = 2048
POOL_WINDOWS = (2, 4, 8, 16)
POOL_GROUP = 512
N_HEADS = 16
QK_NOPE = 128
QK_ROPE = 64
V_DIM = 128
HEAD_PAD = 256
Q_LORA = 384
KV_LORA = 256
MLA_IN = Q_LORA + KV_LORA + QK_ROPE + D_INNER
MLA_IN_PAD = 2816
ATTN_SCALE = (QK_NOPE + QK_ROPE) ** -0.5
ROPE_BASE = 10000.0
NORM_EPS = 1e-6
HALO = 16

ADAM_LR = 0.001
ADAM_B1 = 0.9
ADAM_B2 = 0.999
ADAM_EPS = 1e-08
ADAM_WD = 0.01
ADAM_STEP = 10

N_CHIPS = 4
N_DEV = 8
LANES = 128
PACK_COLS = 1024
V7X_VMEM_LIMIT = 56 * 1024 * 1024
MESH = pl.DeviceIdType.MESH


def _cparams(*sem):
    return pltpu.CompilerParams(dimension_semantics=sem, vmem_limit_bytes=V7X_VMEM_LIMIT)


def _pick(n, cap):
    best = None
    for d in range(LANES, min(n, cap) + 1, LANES):
        if n % d == 0:
            best = d
    assert best is not None, (n, cap)
    return best


def _sigmoid(z):
    return 1.0 / (1.0 + jnp.exp(-z))


class Packed:
    def __init__(self, buf, r0, kind, shape):
        self.buf, self.r0, self.kind, self.shape = buf, r0, kind, shape

    def block(self, rb, cb, bk):
        assert self.r0 % bk == 0, (self.r0, bk)
        if self.kind == "cols":
            K = self.shape[0]
            per = self.shape[1] // (N_CHIPS * PACK_COLS)
            assert K % bk == 0
            return cb // per, (self.r0 + (cb % per) * K) // bk + rb
        kk = self.shape[0] // N_CHIPS
        assert kk % bk == 0
        per = kk // bk
        return rb // per, self.r0 // bk + rb % per


def _mm(a, b, *, trans_a=False, trans_b=False, out_dtype=F32, residual=None, after=None, into=None, tm, tn, tk, name):
    if trans_a:
        K, M = a.shape
    else:
        M, K = a.shape
    if trans_b:
        N, K2 = b.shape
    else:
        K2, N = b.shape
    assert K == K2 and M % tm == 0 and N % tn == 0 and K % tk == 0, (name, a.shape, b.shape, tm, tn, tk)
    nk = K // tk
    dn = (((0 if trans_a else 1,), (1 if trans_b else 0,)), ((), ()))
    has_res = residual is not None
    n_skip = (after is not None) + (into is not None)
    b_all = isinstance(b, Packed) and b.kind == "rows"
    o_all = into is not None and into.kind == "rows"
    assert not b_all or (not trans_a and (tn == N if trans_b else tk == K)), name
    assert not o_all or tm == M, name

    def body(*refs):
        if has_res:
            a_ref, b_ref, r_ref = refs[:3]
            refs = refs[3:]
        else:
            a_ref, b_ref = refs[:2]
            r_ref = None
            refs = refs[2:]
        refs = refs[n_skip:]
        o_ref, rest = refs[0], refs[1:]
        if b_all and trans_b:
            part = jnp.concatenate([lax.dot_general(a_ref[...], b_ref[c], dn, preferred_element_type=F32) for c in range(N_CHIPS)], axis=1)
        elif b_all:
            kk = K // N_CHIPS
            part = lax.dot_general(a_ref[:, 0:kk], b_ref[0], dn, preferred_element_type=F32)
            for c in range(1, N_CHIPS):
                part = part + lax.dot_general(a_ref[:, c * kk:(c + 1) * kk], b_ref[c], dn, preferred_element_type=F32)
        else:
            part = lax.dot_general(a_ref[...], b_ref[...], dn, preferred_element_type=F32)

        def finish(acc):
            if has_res:
                acc = acc + r_ref[...]
            if o_all:
                mk = M // N_CHIPS
                for c in range(N_CHIPS):
                    o_ref[c] = acc[c * mk:(c + 1) * mk].astype(o_ref.dtype)
            else:
                o_ref[...] = acc.astype(o_ref.dtype)

        if nk == 1:
            finish(part)
        else:
            acc_ref = rest[0]
            k = pl.program_id(2)

            @pl.when(k == 0)
            def _():
                acc_ref[...] = part

            @pl.when(k > 0)
            def _():
                acc_ref[...] += part

            @pl.when(k == nk - 1)
            def _():
                finish(acc_ref[...])

    a_spec = pl.BlockSpec((tk, tm), lambda i, j, k: (k, i)) if trans_a else pl.BlockSpec((tm, tk), lambda i, j, k: (i, k))
    if b_all:
        kkb = b.shape[0] // N_CHIPS
        b_spec = pl.BlockSpec((N_CHIPS, kkb, PACK_COLS), lambda i, j, k: (0, b.r0 // kkb, 0))
        b_arg = b.buf
    elif isinstance(b, Packed):
        if trans_b:
            assert tk == PACK_COLS
            b_spec = pl.BlockSpec((None, tn, tk), lambda i, j, k: (*b.block(j, k, tn), 0))
        else:
            assert tn == PACK_COLS
            b_spec = pl.BlockSpec((None, tk, tn), lambda i, j, k: (*b.block(k, j, tk), 0))
        b_arg = b.buf
    else:
        b_spec = pl.BlockSpec((tn, tk), lambda i, j, k: (j, k)) if trans_b else pl.BlockSpec((tk, tn), lambda i, j, k: (k, j))
        b_arg = b
    o_spec = pl.BlockSpec((tm, tn), lambda i, j, k: (i, j))
    in_specs = [a_spec, b_spec] + ([o_spec] if has_res else [])
    args = (a, b_arg) + ((residual,) if has_res else ())
    aliases = {}
    if after is not None:
        in_specs.append(pl.BlockSpec(memory_space=pl.ANY))
        args += (after,)
    if into is None:
        out_shape, out_spec = jax.ShapeDtypeStruct((M, N), out_dtype), o_spec
    else:
        assert tn == PACK_COLS and into.shape == (M, N)
        in_specs.append(pl.BlockSpec(memory_space=pl.ANY))
        aliases = {len(args): 0}
        args += (into.buf,)
        out_shape = jax.ShapeDtypeStruct(into.buf.shape, into.buf.dtype)
        if o_all:
            out_spec = pl.BlockSpec((N_CHIPS, M // N_CHIPS, tn), lambda i, j, k: (0, into.r0 // (M // N_CHIPS), 0))
        else:
            out_spec = pl.BlockSpec((None, tm, tn), lambda i, j, k: (*into.block(i, j, tm), 0))
    return pl.pallas_call(
        body, name=name, out_shape=out_shape,
        grid=(M // tm, N // tn, nk),
        in_specs=in_specs, out_specs=out_spec, input_output_aliases=aliases,
        scratch_shapes=[pltpu.VMEM((tm, tn), F32)] if nk > 1 else [],
        compiler_params=_cparams("parallel", "parallel", "arbitrary"),
    )(*args)


def _grouped_tn(a, b, *, tk, name):
    S = a.shape[0]
    G = POOL_GROUP
    nk = S // tk

    def body(a_ref, b_ref, o_ref, acc_ref):
        k = pl.program_id(1)
        part = lax.dot_general(a_ref[...], b_ref[...], (((0,), (0,)), ((), ())), preferred_element_type=F32)

        @pl.when(k == 0)
        def _():
            acc_ref[...] = part

        @pl.when(k > 0)
        def _():
            acc_ref[...] += part

        @pl.when(k == nk - 1)
        def _():
            o_ref[...] = acc_ref[...].astype(o_ref.dtype)

    return pl.pallas_call(
        body, name=name,
        out_shape=jax.ShapeDtypeStruct((len(POOL_WINDOWS), G, G), BF16),
        grid=(len(POOL_WINDOWS), nk),
        in_specs=[pl.BlockSpec((tk, G), lambda g, k: (k, g)), pl.BlockSpec((tk, G), lambda g, k: (k, g))],
        out_specs=pl.BlockSpec((None, G, G), lambda g, k: (g, 0, 0)),
        scratch_shapes=[pltpu.VMEM((G, G), F32)],
        compiler_params=_cparams("parallel", "arbitrary"),
    )(a, b)


def _rms_fwd(x, g, *, tm, name):
    S, D = x.shape

    def body(x_ref, g_ref, o_ref):
        xv = x_ref[...]
        rstd = lax.rsqrt(jnp.mean(xv * xv, axis=-1, keepdims=True) + NORM_EPS)
        o_ref[...] = (xv * rstd * g_ref[...]).astype(o_ref.dtype)

    return pl.pallas_call(
        body, name=name,
        out_shape=jax.ShapeDtypeStruct((S, D), BF16),
        grid=(S // tm,),
        in_specs=[pl.BlockSpec((tm, D), lambda i: (i, 0)), pl.BlockSpec((1, D), lambda i: (0, 0))],
        out_specs=pl.BlockSpec((tm, D), lambda i: (i, 0)),
        compiler_params=_cparams("parallel"),
    )(x, g)


def _rms_bwd_math(xv, gv, dxn):
    rstd = lax.rsqrt(jnp.mean(xv * xv, axis=-1, keepdims=True) + NORM_EPS)
    xh = xv * rstd
    dg = jnp.sum(dxn * xh, axis=0, keepdims=True)
    dxh = dxn * gv
    dx = rstd * (dxh - xh * jnp.mean(dxh * xh, axis=-1, keepdims=True))
    return dx, dg


def _rms_bwd(x, g, dxn, dres, *, tm, name):
    S, D = x.shape

    def body(x_ref, g_ref, dxn_ref, dres_ref, dx_ref, dxb_ref, dg_ref):
        dx, dg = _rms_bwd_math(x_ref[...], g_ref[...], dxn_ref[...])
        dx = dx + dres_ref[...]
        dx_ref[...] = dx
        dxb_ref[...] = dx.astype(BF16)

        @pl.when(pl.program_id(0) == 0)
        def _():
            dg_ref[...] = dg

        @pl.when(pl.program_id(0) > 0)
        def _():
            dg_ref[...] += dg

    row = pl.BlockSpec((tm, D), lambda i: (i, 0))
    vec = pl.BlockSpec((1, D), lambda i: (0, 0))
    return pl.pallas_call(
        body, name=name,
        out_shape=(jax.ShapeDtypeStruct((S, D), F32), jax.ShapeDtypeStruct((S, D), BF16), jax.ShapeDtypeStruct((1, D), F32)),
        grid=(S // tm,),
        in_specs=[row, vec, row, row],
        out_specs=(row, row, vec),
        compiler_params=_cparams("arbitrary"),
    )(x, g, dxn, dres)


def _final_loss(x, g, target, *, tm, name):
    S, D = x.shape

    def body(x_ref, g_ref, t_ref, loss_ref, dx_ref, dxb_ref, dg_ref):
        xv = x_ref[...]
        gv = g_ref[...]
        rstd = lax.rsqrt(jnp.mean(xv * xv, axis=-1, keepdims=True) + NORM_EPS)
        xh = xv * rstd
        err = xh * gv - t_ref[...]
        part = 0.5 * jnp.sum(jnp.mean(err * err, axis=-1, keepdims=True), axis=0, keepdims=True)
        dy = err * (1.0 / D)
        dg = jnp.sum(dy * xh, axis=0, keepdims=True)
        dxh = dy * gv
        dx = rstd * (dxh - xh * jnp.mean(dxh * xh, axis=-1, keepdims=True))
        dx_ref[...] = dx
        dxb_ref[...] = dx.astype(BF16)
        lossb = jnp.broadcast_to(part, loss_ref.shape)

        @pl.when(pl.program_id(0) == 0)
        def _():
            dg_ref[...] = dg
            loss_ref[...] = lossb

        @pl.when(pl.program_id(0) > 0)
        def _():
            dg_ref[...] += dg
            loss_ref[...] += lossb

    row = pl.BlockSpec((tm, D), lambda i: (i, 0))
    vec = pl.BlockSpec((1, D), lambda i: (0, 0))
    lspec = pl.BlockSpec((8, LANES), lambda i: (0, 0))
    return pl.pallas_call(
        body, name=name,
        out_shape=(jax.ShapeDtypeStruct((8, LANES), F32), jax.ShapeDtypeStruct((S, D), F32),
                   jax.ShapeDtypeStruct((S, D), BF16), jax.ShapeDtypeStruct((1, D), F32)),
        grid=(S // tm,),
        in_specs=[row, vec, row],
        out_specs=(lspec, row, row, vec),
        compiler_params=_cparams("arbitrary"),
    )(x, g, target)


def _prev_halo_spec(tm, width, col):
    r = tm // HALO
    return pl.BlockSpec((HALO, width), lambda i: (jnp.maximum(i * r - 1, 0), col))


def _next_halo_spec(tm, width, col, S):
    r = tm // HALO
    last = S // HALO - 1
    return pl.BlockSpec((HALO, width), lambda i: (jnp.minimum((i + 1) * r, last), col))


def _shift_down(ext, k):
    return pltpu.roll(ext, k, 0)[HALO:, :]


def _shift_up(ext, k, tm):
    n = ext.shape[0]
    return pltpu.roll(ext, n - k, 0)[:tm, :]


def _pool_window_sum(ext, w):
    s = ext
    k = 1
    while k < w:
        s = s + pltpu.roll(s, k, 0)
        k *= 2
    return s[HALO:, :]


def _pooled_group(u_ref, halo, g, w, t_idx):
    cs = slice(g * POOL_GROUP, (g + 1) * POOL_GROUP)
    u = u_ref[:, cs]
    ext = jnp.concatenate([halo[:, cs], u], axis=0)
    inv = 1.0 / jnp.minimum(t_idx + 1, w).astype(F32)
    return _pool_window_sum(ext, w) * inv - u


def _pool_fwd(h, w_grp, scale, *, tm, name):
    S = h.shape[0]
    E = D_INNER

    def body(u_ref, uh_ref, z_ref, wg_ref, sc_ref, y_ref):
        i = pl.program_id(0)
        halo = jnp.where(i > 0, uh_ref[...], 0.0)
        t_idx = i * tm + lax.broadcasted_iota(jnp.int32, (tm, 1), 0)
        for g, w in enumerate(POOL_WINDOWS):
            cs = slice(g * POOL_GROUP, (g + 1) * POOL_GROUP)
            pooled = _pooled_group(u_ref, halo, g, w, t_idx)
            mixed = jnp.dot(pooled.astype(BF16), wg_ref[g], preferred_element_type=F32)
            z = z_ref[:, cs]
            y_ref[:, cs] = (mixed * sc_ref[:, cs] * (z * _sigmoid(z))).astype(BF16)

    return pl.pallas_call(
        body, name=name,
        out_shape=jax.ShapeDtypeStruct((S, E), BF16),
        grid=(S // tm,),
        in_specs=[pl.BlockSpec((tm, E), lambda i: (i, 0)), _prev_halo_spec(tm, E, 0),
                  pl.BlockSpec((tm, E), lambda i: (i, 1)),
                  pl.BlockSpec((len(POOL_WINDOWS), POOL_GROUP, POOL_GROUP), lambda i: (0, 0, 0)),
                  pl.BlockSpec((1, E), lambda i: (0, 0))],
        out_specs=pl.BlockSpec((tm, E), lambda i: (i, 0)),
        compiler_params=_cparams("parallel"),
    )(h, h, h, w_grp, scale)


def _pool_bwd1(h, dy, w_grp, scale, *, tm, name):
    S = h.shape[0]
    E = D_INNER

    def body(u_ref, uh_ref, z_ref, dy_ref, wg_ref, sc_ref, pooled_ref, dmixed_ref, dpooled_ref, dz_ref, dsc_ref):
        i = pl.program_id(0)
        halo = jnp.where(i > 0, uh_ref[...], 0.0)
        t_idx = i * tm + lax.broadcasted_iota(jnp.int32, (tm, 1), 0)
        for g, w in enumerate(POOL_WINDOWS):
            cs = slice(g * POOL_GROUP, (g + 1) * POOL_GROUP)
            pooled = _pooled_group(u_ref, halo, g, w, t_idx).astype(BF16)
            wg = wg_ref[g]
            mixed = jnp.dot(pooled, wg, preferred_element_type=F32)
            z = z_ref[:, cs]
            sg = _sigmoid(z)
            dyv = dy_ref[:, cs]
            sc = sc_ref[:, cs]
            dms = dyv * (z * sg)
            dz = dyv * (mixed * sc) * (sg * (1.0 + z * (1.0 - sg)))
            dsc = jnp.sum(dms * mixed, axis=0, keepdims=True)
            dmixed = (dms * sc).astype(BF16)
            dpooled = lax.dot_general(dmixed, wg, (((1,), (1,)), ((), ())), preferred_element_type=F32)
            pooled_ref[:, cs] = pooled
            dmixed_ref[:, cs] = dmixed
            dpooled_ref[:, cs] = dpooled
            dz_ref[:, cs] = dz.astype(BF16)

            @pl.when(i == 0)
            def _():
                dsc_ref[:, cs] = dsc

            @pl.when(i > 0)
            def _():
                dsc_ref[:, cs] += dsc

    row = pl.BlockSpec((tm, E), lambda i: (i, 0))
    vec = pl.BlockSpec((1, E), lambda i: (0, 0))
    return pl.pallas_call(
        body, name=name,
        out_shape=(jax.ShapeDtypeStruct((S, E), BF16), jax.ShapeDtypeStruct((S, E), BF16),
                   jax.ShapeDtypeStruct((S, E), F32), jax.ShapeDtypeStruct((S, E), BF16),
                   jax.ShapeDtypeStruct((1, E), F32)),
        grid=(S // tm,),
        in_specs=[row, _prev_halo_spec(tm, E, 0), pl.BlockSpec((tm, E), lambda i: (i, 1)), row,
                  pl.BlockSpec((len(POOL_WINDOWS), POOL_GROUP, POOL_GROUP), lambda i: (0, 0, 0)), vec],
        out_specs=(row, row, row, row, vec),
        compiler_params=_cparams("arbitrary"),
    )(h, h, h, dy, w_grp, scale)


def _pool_bwd2(dpooled, dz, *, tm, name):
    S = dpooled.shape[0]
    E = D_INNER
    nt = S // tm

    def body(dp_ref, dpn_ref, dz_ref, dh_ref):
        i = pl.program_id(0)
        nxt = jnp.where(i < nt - 1, dpn_ref[...], 0.0)
        t_ext = i * tm + lax.broadcasted_iota(jnp.int32, (tm + HALO, 1), 0)
        for g, w in enumerate(POOL_WINDOWS):
            cs = slice(g * POOL_GROUP, (g + 1) * POOL_GROUP)
            dp = dp_ref[:, cs]
            inv = 1.0 / jnp.minimum(t_ext + 1, w).astype(F32)
            s = jnp.concatenate([dp, nxt[:, cs]], axis=0) * inv
            n = tm + HALO
            k = 1
            while k < w:
                s = s + pltpu.roll(s, n - k, 0)
                k *= 2
            dh_ref[:, cs] = (s[:tm, :] - dp).astype(BF16)
        dh_ref[:, E:] = dz_ref[...]

    return pl.pallas_call(
        body, name=name,
        out_shape=jax.ShapeDtypeStruct((S, 2 * E), BF16),
        grid=(nt,),
        in_specs=[pl.BlockSpec((tm, E), lambda i: (i, 0)), _next_halo_spec(tm, E, 0, S),
                  pl.BlockSpec((tm, E), lambda i: (i, 0))],
        out_specs=pl.BlockSpec((tm, 2 * E), lambda i: (i, 0)),
        compiler_params=_cparams("parallel"),
    )(dpooled, dpooled, dz)


CONV_CHUNK = 512


def _conv_fwd(h, cw, *, tm, name):
    S = h.shape[0]
    E = D_INNER

    def body(b_ref, c_ref, hh_ref, z_ref, ch_ref, hhh_ref, w_ref, y_ref):
        i = pl.program_id(0)
        for j in range(E // CONV_CHUNK):
            cs = slice(j * CONV_CHUNK, (j + 1) * CONV_CHUNK)
            p = c_ref[:, cs] * hh_ref[:, cs]
            ph = jnp.where(i > 0, ch_ref[:, cs] * hhh_ref[:, cs], 0.0)
            ext = jnp.concatenate([ph, p], axis=0)
            conv = w_ref[2:3, cs] * p + w_ref[1:2, cs] * _shift_down(ext, 1) + w_ref[0:1, cs] * _shift_down(ext, 2)
            z = z_ref[:, cs]
            y_ref[:, cs] = (b_ref[:, cs] * conv * (z * _sigmoid(z))).astype(BF16)

    col = lambda c: pl.BlockSpec((tm, E), lambda i: (i, c))
    return pl.pallas_call(
        body, name=name,
        out_shape=jax.ShapeDtypeStruct((S, E), BF16),
        grid=(S // tm,),
        in_specs=[col(0), col(1), col(2), col(3), _prev_halo_spec(tm, E, 1), _prev_halo_spec(tm, E, 2),
                  pl.BlockSpec((8, E), lambda i: (0, 0))],
        out_specs=pl.BlockSpec((tm, E), lambda i: (i, 0)),
        compiler_params=_cparams("parallel"),
    )(h, h, h, h, h, h, cw)


def _conv_bwd(h, dy, cw, *, tm, name):
    S = h.shape[0]
    E = D_INNER
    nt = S // tm

    def body(b_ref, c_ref, hh_ref, z_ref, dy_ref, ch_ref, hhh_ref, bn_ref, zn_ref, dyn_ref, w_ref, dh_ref, dw_ref):
        i = pl.program_id(0)
        for j in range(E // CONV_CHUNK):
            cs = slice(j * CONV_CHUNK, (j + 1) * CONV_CHUNK)
            w0, w1, w2 = w_ref[0:1, cs], w_ref[1:2, cs], w_ref[2:3, cs]
            c, hh, b, z, dyv = c_ref[:, cs], hh_ref[:, cs], b_ref[:, cs], z_ref[:, cs], dy_ref[:, cs]
            p = c * hh
            ph = jnp.where(i > 0, ch_ref[:, cs] * hhh_ref[:, cs], 0.0)
            ext = jnp.concatenate([ph, p], axis=0)
            pm1 = _shift_down(ext, 1)
            pm2 = _shift_down(ext, 2)
            conv = w2 * p + w1 * pm1 + w0 * pm2
            sg = _sigmoid(z)
            dy0 = dyv * (z * sg)
            dz = dyv * (b * conv) * (sg * (1.0 + z * (1.0 - sg)))
            db = dy0 * conv
            dconv = dy0 * b
            zn = zn_ref[:, cs]
            dconv_n = jnp.where(i < nt - 1, dyn_ref[:, cs] * (zn * _sigmoid(zn)) * bn_ref[:, cs], 0.0)
            dext = jnp.concatenate([dconv, dconv_n], axis=0)
            dp = w2 * dconv + w1 * _shift_up(dext, 1, tm) + w0 * _shift_up(dext, 2, tm)
            dh_ref[:, 0 * E + j * CONV_CHUNK:0 * E + (j + 1) * CONV_CHUNK] = db.astype(BF16)
            dh_ref[:, 1 * E + j * CONV_CHUNK:1 * E + (j + 1) * CONV_CHUNK] = (dp * hh).astype(BF16)
            dh_ref[:, 2 * E + j * CONV_CHUNK:2 * E + (j + 1) * CONV_CHUNK] = (dp * c).astype(BF16)
            dh_ref[:, 3 * E + j * CONV_CHUNK:3 * E + (j + 1) * CONV_CHUNK] = dz.astype(BF16)
            dw = jnp.concatenate([jnp.sum(dconv * pm2, axis=0, keepdims=True),
                                  jnp.sum(dconv * pm1, axis=0, keepdims=True),
                                  jnp.sum(dconv * p, axis=0, keepdims=True),
                                  jnp.zeros((5, CONV_CHUNK), F32)], axis=0)

            @pl.when(i == 0)
            def _():
                dw_ref[:, cs] = dw

            @pl.when(i > 0)
            def _():
                dw_ref[:, cs] += dw

    col = lambda c: pl.BlockSpec((tm, E), lambda i: (i, c))
    return pl.pallas_call(
        body, name=name,
        out_shape=(jax.ShapeDtypeStruct((S, 4 * E), BF16), jax.ShapeDtypeStruct((8, E), F32)),
        grid=(nt,),
        in_specs=[col(0), col(1), col(2), col(3), pl.BlockSpec((tm, E), lambda i: (i, 0)),
                  _prev_halo_spec(tm, E, 1), _prev_halo_spec(tm, E, 2),
                  _next_halo_spec(tm, E, 0, S), _next_halo_spec(tm, E, 3, S), _next_halo_spec(tm, E, 0, S),
                  pl.BlockSpec((8, E), lambda i: (0, 0))],
        out_specs=(pl.BlockSpec((tm, 4 * E), lambda i: (i, 0)), pl.BlockSpec((8, E), lambda i: (0, 0))),
        compiler_params=_cparams("arbitrary"),
    )(h, h, h, h, dy, h, h, h, h, dy, cw)


Z_COLS = D_INNER // LANES
KV_LAT_BLK = D_INNER // KV_LORA
Q_LAT_BLK = (D_INNER + KV_LORA) // Q_LORA
K_ROPE_BLK = (D_INNER + KV_LORA + Q_LORA) // LANES


def _rope(blk, c, s1, s2):
    return blk * c + pltpu.roll(blk, LANES - QK_ROPE // 2, 1) * s1 + pltpu.roll(blk, QK_ROPE // 2, 1) * s2


def _unrope(blk, c, s1, s2):
    return blk * c - pltpu.roll(blk, LANES - QK_ROPE // 2, 1) * s1 - pltpu.roll(blk, QK_ROPE // 2, 1) * s2


def _lat_norm(v, g):
    rstd = lax.rsqrt(jnp.mean(v * v, axis=-1, keepdims=True) + NORM_EPS)
    return v * rstd * g


def _mla_latent_fwd(h, q_norm, kv_norm, tabs, *, tm, name):
    S = h.shape[0]

    def body(kv_ref, q_ref, kr_ref, qg_ref, kvg_ref, c_ref, s1_ref, s2_ref, qn_ref, kvn_ref, krr_ref):
        qn_ref[...] = _lat_norm(q_ref[...], qg_ref[...]).astype(BF16)
        kvn_ref[...] = _lat_norm(kv_ref[...], kvg_ref[...]).astype(BF16)
        krr_ref[...] = _rope(kr_ref[...], c_ref[...], s1_ref[...], s2_ref[...]).astype(BF16)

    tab = pl.BlockSpec((tm, LANES), lambda i: (i, 0))
    return pl.pallas_call(
        body, name=name,
        out_shape=(jax.ShapeDtypeStruct((S, Q_LORA), BF16), jax.ShapeDtypeStruct((S, KV_LORA), BF16),
                   jax.ShapeDtypeStruct((S, LANES), BF16)),
        grid=(S // tm,),
        in_specs=[pl.BlockSpec((tm, KV_LORA), lambda i: (i, KV_LAT_BLK)), pl.BlockSpec((tm, Q_LORA), lambda i: (i, Q_LAT_BLK)),
                  pl.BlockSpec((tm, LANES), lambda i: (i, K_ROPE_BLK)),
                  pl.BlockSpec((1, Q_LORA), lambda i: (0, 0)), pl.BlockSpec((1, KV_LORA), lambda i: (0, 0)), tab, tab, tab],
        out_specs=(pl.BlockSpec((tm, Q_LORA), lambda i: (i, 0)), pl.BlockSpec((tm, KV_LORA), lambda i: (i, 0)), tab),
        compiler_params=_cparams("parallel"),
    )(h, h, h, q_norm, kv_norm, *tabs)


def _mla_q_up(q_n, w_q_pad, tabs, *, tm, name):
    S = q_n.shape[0]

    def body(a_ref, w_ref, c_ref, s1_ref, s2_ref, o_ref):
        a = a_ref[...]
        for hd in range(N_HEADS):
            acc = jnp.dot(a, w_ref[:, hd * HEAD_PAD:(hd + 1) * HEAD_PAD], preferred_element_type=F32)
            o_ref[hd, :, :QK_NOPE] = acc[:, :QK_NOPE].astype(BF16)
            o_ref[hd, :, QK_NOPE:] = _rope(acc[:, QK_NOPE:], c_ref[...], s1_ref[...], s2_ref[...]).astype(BF16)

    tab = pl.BlockSpec((tm, LANES), lambda i: (i, 0))
    return pl.pallas_call(
        body, name=name,
        out_shape=jax.ShapeDtypeStruct((N_HEADS, S, HEAD_PAD), BF16),
        grid=(S // tm,),
        in_specs=[pl.BlockSpec((tm, Q_LORA), lambda i: (i, 0)), pl.BlockSpec((Q_LORA, N_HEADS * HEAD_PAD), lambda i: (0, 0)),
                  tab, tab, tab],
        out_specs=pl.BlockSpec((N_HEADS, tm, HEAD_PAD), lambda i: (0, i, 0)),
        compiler_params=_cparams("parallel"),
    )(q_n, w_q_pad, *tabs)


def _mla_kv_up(kv_n, w_kv, krr, *, tm, name):
    S = kv_n.shape[0]
    heads_per_chip = N_HEADS // N_CHIPS

    def body(a_ref, w_ref, krr_ref, k_ref, v_ref):
        a = a_ref[...]
        ones = jnp.ones((tm, V_DIM), BF16)
        for hd in range(N_HEADS):
            lo = (hd % heads_per_chip) * HEAD_PAD
            acc = jnp.dot(a, w_ref[hd // heads_per_chip, :, lo:lo + HEAD_PAD], preferred_element_type=F32)
            k_ref[hd, :, :QK_NOPE] = acc[:, :QK_NOPE].astype(BF16)
            k_ref[hd, :, QK_NOPE:] = krr_ref[...]
            v_ref[hd, :, :V_DIM] = acc[:, QK_NOPE:].astype(BF16)
            v_ref[hd, :, V_DIM:] = ones

    head_blk = pl.BlockSpec((N_HEADS, tm, HEAD_PAD), lambda i: (0, i, 0))
    sds = jax.ShapeDtypeStruct((N_HEADS, S, HEAD_PAD), BF16)
    return pl.pallas_call(
        body, name=name, out_shape=(sds, sds),
        grid=(S // tm,),
        in_specs=[pl.BlockSpec((tm, KV_LORA), lambda i: (i, 0)),
                  pl.BlockSpec((N_CHIPS, KV_LORA, PACK_COLS), lambda i: (0, w_kv.r0 // KV_LORA, 0)),
                  pl.BlockSpec((tm, LANES), lambda i: (i, 0))],
        out_specs=(head_blk, head_blk),
        compiler_params=_cparams("parallel"),
    )(kv_n, w_kv.buf, krr)


LOG2E = 1.4426950408889634
SCORE_TO_LOG2 = ATTN_SCALE * LOG2E


def _flash_fwd(q_full, k_full, v_aug, h, *, tq, name):
    H, S, _ = q_full.shape
    tk = tq
    HP = 2

    def body(q_ref, k_ref, v_ref, z_ref, o_ref, y_ref, lse_ref, m_sc, acc_sc):
        qi = pl.program_id(1)
        m_sc[...] = jnp.full(m_sc.shape, -1e30, F32)
        acc_sc[...] = jnp.zeros(acc_sc.shape, F32)

        def chunk(j, diag):
            off = pl.multiple_of(j * tk, tk)
            for hh in range(HP):
                kj = k_ref[hh, pl.ds(off, tk), :]
                vj = v_ref[hh, pl.ds(off, tk), :]
                s = lax.dot_general(q_ref[hh], kj, (((1,), (1,)), ((), ())), preferred_element_type=F32) * SCORE_TO_LOG2
                if diag:
                    keep = lax.broadcasted_iota(jnp.int32, (tq, tk), 1) <= lax.broadcasted_iota(jnp.int32, (tq, tk), 0)
                    s = jnp.where(keep, s, -1e30)
                m_old = m_sc[hh]
                m_new = jnp.maximum(m_old, jnp.max(s, axis=1, keepdims=True))
                p = jnp.exp2(s - jnp.tile(m_new, (1, tk // LANES)))
                alpha = jnp.exp2(m_old - m_new)
                acc_sc[hh] = jnp.tile(alpha, (1, 2)) * acc_sc[hh] + jnp.dot(p.astype(BF16), vj, preferred_element_type=F32)
                m_sc[hh] = m_new

        def step(j, carry):
            chunk(j, False)
            return carry

        lax.fori_loop(0, qi, step, 0)
        chunk(qi, True)
        for hh in range(HP):
            cs = slice(hh * V_DIM, (hh + 1) * V_DIM)
            l = acc_sc[hh, :, V_DIM:]
            o = acc_sc[hh, :, :V_DIM] / l
            z = z_ref[:, cs]
            o_ref[:, cs] = o
            y_ref[:, cs] = (o * (z * _sigmoid(z))).astype(BF16)
            lse_ref[hh, 0] = (m_sc[hh] + jnp.log2(l)).T[0:1, :]

    pair = pl.BlockSpec((tq, HP * V_DIM), lambda hd, i: (i, hd))
    return pl.pallas_call(
        body, name=name,
        out_shape=(jax.ShapeDtypeStruct((S, D_INNER), F32), jax.ShapeDtypeStruct((S, D_INNER), BF16),
                   jax.ShapeDtypeStruct((H, S // tq, 1, tq), F32)),
        grid=(H // HP, S // tq),
        in_specs=[pl.BlockSpec((HP, tq, HEAD_PAD), lambda hd, i: (hd, i, 0)),
                  pl.BlockSpec((HP, S, HEAD_PAD), lambda hd, i: (hd, 0, 0)),
                  pl.BlockSpec((HP, S, HEAD_PAD), lambda hd, i: (hd, 0, 0)),
                  pair],
        out_specs=(pair, pair, pl.BlockSpec((HP, 1, 1, tq), lambda hd, i: (hd, i, 0, 0))),
        scratch_shapes=[pltpu.VMEM((HP, tq, LANES), F32), pltpu.VMEM((HP, tq, HEAD_PAD), F32)],
        compiler_params=_cparams("parallel", "parallel"),
    )(q_full, k_full, v_aug, h)


def _mla_gate_bwd(dy, o, h, *, tm, name):
    S = dy.shape[0]
    E = D_INNER

    def body(dy_ref, o_ref, z_ref, do_ref, dz_ref, delta_ref):
        for hd in range(N_HEADS):
            cs = slice(hd * V_DIM, (hd + 1) * V_DIM)
            z = z_ref[:, cs]
            sg = _sigmoid(z)
            dyv = dy_ref[:, cs]
            ov = o_ref[:, cs]
            do = dyv * (z * sg)
            do_ref[:, cs] = do.astype(BF16)
            dz_ref[:, cs] = (dyv * ov * (sg * (1.0 + z * (1.0 - sg)))).astype(BF16)
            delta_ref[hd, 0] = jnp.broadcast_to(jnp.sum(do * ov, axis=-1, keepdims=True), (tm, LANES)).T[0:1, :]

    row = pl.BlockSpec((tm, E), lambda i: (i, 0))
    return pl.pallas_call(
        body, name=name,
        out_shape=(jax.ShapeDtypeStruct((S, E), BF16), jax.ShapeDtypeStruct((S, E), BF16),
                   jax.ShapeDtypeStruct((N_HEADS, S // tm, 1, tm), F32)),
        grid=(S // tm,),
        in_specs=[row, row, row],
        out_specs=(row, row, pl.BlockSpec((N_HEADS, 1, 1, tm), lambda i: (0, i, 0, 0))),
        compiler_params=_cparams("parallel"),
    )(dy, o, h)


def _flash_bwd(q_full, k_full, v_aug, do, lse_rows, delta_rows, *, tq, name):
    H, S, _ = q_full.shape
    tk = tq
    nq = S // tq
    HP = 2

    def body(q_ref, k_ref, v_ref, do_ref, lse_ref, dl_ref, dq_ref, dkv_ref, dkr_ref, dk_sc, dv_sc):
        kj = pl.program_id(1)

        @pl.when(kj == 0)
        def _():
            dq_ref[...] = jnp.zeros(dq_ref.shape, F32)

        dk_sc[...] = jnp.zeros(dk_sc.shape, F32)
        dv_sc[...] = jnp.zeros(dv_sc.shape, F32)

        def chunk(qi, diag):
            off = pl.multiple_of(qi * tq, tq)
            for hh in range(HP):
                k = k_ref[hh]
                q = q_ref[hh, pl.ds(off, tq), :]
                dov = do_ref[pl.ds(off, tq), hh * V_DIM:(hh + 1) * V_DIM]
                s_t = lax.dot_general(k, q, (((1,), (1,)), ((), ())), preferred_element_type=F32) * SCORE_TO_LOG2
                p_t = jnp.exp2(s_t - lse_ref[hh, qi])
                if diag:
                    keep = lax.broadcasted_iota(jnp.int32, (tk, tq), 0) <= lax.broadcasted_iota(jnp.int32, (tk, tq), 1)
                    p_t = jnp.where(keep, p_t, 0.0)
                dv_sc[hh] += jnp.dot(p_t.astype(BF16), dov, preferred_element_type=F32)
                dp_t = lax.dot_general(v_ref[hh], dov, (((1,), (1,)), ((), ())), preferred_element_type=F32)
                ds = (p_t * (dp_t - dl_ref[hh, qi])).astype(BF16)
                dk_sc[hh] += jnp.dot(ds, q, preferred_element_type=F32)
                dq_ref[pl.ds(off, tq), hh * HEAD_PAD:(hh + 1) * HEAD_PAD] += lax.dot_general(
                    ds, k, (((0,), (0,)), ((), ())), preferred_element_type=F32)

        def step(qi, carry):
            chunk(qi, False)
            return carry

        chunk(kj, True)
        lax.fori_loop(kj + 1, nq, step, 0)
        for hh in range(HP):
            lo = hh * HEAD_PAD
            dkv_ref[:, lo:lo + QK_NOPE] = (dk_sc[hh, :, :QK_NOPE] * ATTN_SCALE).astype(BF16)
            dkv_ref[:, lo + QK_NOPE:lo + HEAD_PAD] = dv_sc[hh].astype(BF16)
            dkr_ref[hh] = dk_sc[hh, :, QK_NOPE:] * ATTN_SCALE

    return pl.pallas_call(
        body, name=name,
        out_shape=(jax.ShapeDtypeStruct((S, H * HEAD_PAD), F32), jax.ShapeDtypeStruct((S, H * HEAD_PAD), BF16),
                   jax.ShapeDtypeStruct((H, S, LANES), F32)),
        grid=(H // HP, S // tk),
        in_specs=[pl.BlockSpec((HP, S, HEAD_PAD), lambda hd, j: (hd, 0, 0)),
                  pl.BlockSpec((HP, tk, HEAD_PAD), lambda hd, j: (hd, j, 0)),
                  pl.BlockSpec((HP, tk, V_DIM), lambda hd, j: (hd, j, 0)),
                  pl.BlockSpec((S, HP * V_DIM), lambda hd, j: (0, hd)),
                  pl.BlockSpec((HP, nq, 1, tq), lambda hd, j: (hd, 0, 0, 0)),
                  pl.BlockSpec((HP, nq, 1, tq), lambda hd, j: (hd, 0, 0, 0))],
        out_specs=(pl.BlockSpec((S, HP * HEAD_PAD), lambda hd, j: (0, hd)), pl.BlockSpec((tk, HP * HEAD_PAD), lambda hd, j: (j, hd)),
                   pl.BlockSpec((HP, tk, LANES), lambda hd, j: (hd, j, 0))),
        scratch_shapes=[pltpu.VMEM((HP, tk, HEAD_PAD), F32), pltpu.VMEM((HP, tk, V_DIM), F32)],
        compiler_params=_cparams("parallel", "arbitrary"),
    )(q_full, k_full, v_aug, do, lse_rows, delta_rows)


def _mla_unrope_q(dq, tabs, *, tm, name):
    S, W = dq.shape

    def body(dq_ref, c_ref, s1_ref, s2_ref, o_ref):
        for hd in range(N_HEADS):
            lo = hd * HEAD_PAD
            o_ref[:, lo:lo + QK_NOPE] = (dq_ref[:, lo:lo + QK_NOPE] * ATTN_SCALE).astype(BF16)
            o_ref[:, lo + QK_NOPE:lo + HEAD_PAD] = _unrope(dq_ref[:, lo + QK_NOPE:lo + HEAD_PAD] * ATTN_SCALE, c_ref[...], s1_ref[...], s2_ref[...]).astype(BF16)

    tab = pl.BlockSpec((tm, LANES), lambda i: (i, 0))
    row = pl.BlockSpec((tm, W), lambda i: (i, 0))
    return pl.pallas_call(
        body, name=name,
        out_shape=jax.ShapeDtypeStruct((S, W), BF16),
        grid=(S // tm,),
        in_specs=[row, tab, tab, tab], out_specs=row,
        compiler_params=_cparams("parallel"),
    )(dq, *tabs)


def _mla_latent_bwd(h, dq_n, dkv_n, dkr, dz, q_norm, kv_norm, tabs, *, tm, name):
    S = h.shape[0]

    def body(kv_ref, q_ref, dqn_ref, dkvn_ref, dkr_ref, dz_ref, qg_ref, kvg_ref, c_ref, s1_ref, s2_ref, dh_ref, dqg_ref, dkvg_ref):
        i = pl.program_id(0)
        dq_lat, dqg = _rms_bwd_math(q_ref[...], qg_ref[...], dqn_ref[...])
        dkv_lat, dkvg = _rms_bwd_math(kv_ref[...], kvg_ref[...], dkvn_ref[...])
        dkr_sum = dkr_ref[0]
        for hd in range(1, N_HEADS):
            dkr_sum = dkr_sum + dkr_ref[hd]
        dh_ref[:, :D_INNER] = dz_ref[...]
        dh_ref[:, D_INNER:D_INNER + KV_LORA] = dkv_lat.astype(BF16)
        dh_ref[:, D_INNER + KV_LORA:D_INNER + KV_LORA + Q_LORA] = dq_lat.astype(BF16)
        dh_ref[:, D_INNER + KV_LORA + Q_LORA:] = _unrope(dkr_sum, c_ref[...], s1_ref[...], s2_ref[...]).astype(BF16)

        @pl.when(i == 0)
        def _():
            dqg_ref[...] = dqg
            dkvg_ref[...] = dkvg

        @pl.when(i > 0)
        def _():
            dqg_ref[...] += dqg
            dkvg_ref[...] += dkvg

    tab = pl.BlockSpec((tm, LANES), lambda i: (i, 0))
    qvec = pl.BlockSpec((1, Q_LORA), lambda i: (0, 0))
    kvvec = pl.BlockSpec((1, KV_LORA), lambda i: (0, 0))
    return pl.pallas_call(
        body, name=name,
        out_shape=(jax.ShapeDtypeStruct((S, MLA_IN_PAD), BF16), jax.ShapeDtypeStruct((1, Q_LORA), F32),
                   jax.ShapeDtypeStruct((1, KV_LORA), F32)),
        grid=(S // tm,),
        in_specs=[pl.BlockSpec((tm, KV_LORA), lambda i: (i, KV_LAT_BLK)), pl.BlockSpec((tm, Q_LORA), lambda i: (i, Q_LAT_BLK)),
                  pl.BlockSpec((tm, Q_LORA), lambda i: (i, 0)), pl.BlockSpec((tm, KV_LORA), lambda i: (i, 0)),
                  pl.BlockSpec((N_HEADS, tm, LANES), lambda i: (0, i, 0)), pl.BlockSpec((tm, D_INNER), lambda i: (i, 0)),
                  qvec, kvvec, tab, tab, tab],
        out_specs=(pl.BlockSpec((tm, MLA_IN_PAD), lambda i: (i, 0)), qvec, kvvec),
        compiler_params=_cparams("arbitrary"),
    )(h, h, dq_n, dkv_n, dkr, dz, q_norm, kv_norm, *tabs)


def _adamw(w, g, m, v, *, name):
    R, C = w.shape
    tr = R
    for cand in (512, 256, 128, 64, 32, 16, 8):
        if R % cand == 0 and cand * C * 4 <= 2 * 1024 * 1024:
            tr = cand
            break

    def body(w_ref, g_ref, m_ref, v_ref, d_ref, nm_ref, nv_ref):
        d_ref[...], nm_ref[...], nv_ref[...] = _adam_math(w_ref[...], g_ref[...], m_ref[...], v_ref[...])

    spec = pl.BlockSpec((tr, C), lambda i: (i, 0))
    sds = jax.ShapeDtypeStruct((R, C), F32)
    return pl.pallas_call(
        body, name=name, out_shape=(sds, sds, sds), grid=(R // tr,),
        in_specs=[spec] * 4, out_specs=(spec,) * 3,
        compiler_params=_cparams("parallel"),
    )(w, g, m, v)


def _adamw_rows(w, m, v, srcs, *, name):
    nj, R, C = w.shape
    assert len(srcs) == nj and C % PACK_COLS == 0
    tr = min(R, 256)
    assert R % tr == 0 and all(r0 % tr == 0 for _, r0 in srcs)

    def body(*refs):
        w_ref, m_ref, v_ref = refs[:3]
        g_refs = refs[3:3 + nj]
        go_ref, d_ref, nm_ref, nv_ref = refs[3 + nj:]
        gv = g_refs[0][...]
        for jj in range(1, nj):
            gv = jnp.where(pl.program_id(0) == jj, g_refs[jj][...], gv)
        d, m_new, v_new = _adam_math(w_ref[...], gv, m_ref[...], v_ref[...])
        go_ref[...] = gv
        d_ref[...] = d
        nm_ref[...] = m_new
        nv_ref[...] = v_new

    nat = pl.BlockSpec((None, tr, PACK_COLS), lambda j, cb, i: (j, i, cb))

    def src_spec(jj, r0):
        return pl.BlockSpec((tr, PACK_COLS), lambda j, cb, i: (jnp.where(j == jj, (r0 + cb * R) // tr + i, r0 // tr), 0))

    sds = jax.ShapeDtypeStruct((nj, R, C), F32)
    return pl.pallas_call(
        body, name=name, out_shape=(sds,) * 4, grid=(nj, C // PACK_COLS, R // tr),
        in_specs=[nat] * 3 + [src_spec(jj, r0) for jj, (_, r0) in enumerate(srcs)], out_specs=(nat,) * 4,
        compiler_params=_cparams("parallel", "parallel", "parallel"),
    )(w, m, v, *[rows for rows, _ in srcs])


HBM_SPEC = pl.BlockSpec(memory_space=pltpu.HBM)
VMEM_SPEC = pl.BlockSpec(memory_space=pltpu.VMEM)
SEM_SPEC = pl.BlockSpec(memory_space=pltpu.SEMAPHORE)
ANY_SPEC = pl.BlockSpec(memory_space=pl.ANY)
SPLIT_EFFECT = pltpu.SideEffectType.DATAFLOW_SIDE_EFFECTING


def _place():
    return lax.axis_index("x"), lax.axis_index("y"), lax.axis_index("c")


def _other_chips(x, y):
    return [(1 - x, y), (x, 1 - y), (1 - x, 1 - y)]


def _remote(src, dst, send_sem, recv_sem, dev):
    return pltpu.make_async_remote_copy(src_ref=src, dst_ref=dst, send_sem=send_sem, recv_sem=recv_sem,
                                        device_id=dev, device_id_type=MESH)


def _all_gather_big(wp, *, name):
    R, C = wp.shape
    H = R // 2

    def body(w_ref, out_ref, send_sems, recv_sems):
        x, y, c = _place()
        sib = (x, y, 1 - c)
        chips = _other_chips(x, y)

        def blk(px, py, half):
            return out_ref.at[2 * px + py, pl.ds(pl.multiple_of(half * H, 16), H), :]

        my_half = w_ref.at[pl.ds(pl.multiple_of(c * H, 16), H), :]
        first = [_remote(my_half, blk(x, y, c), send_sems.at[r], recv_sems.at[r], (*chip, c)) for r, chip in enumerate(chips)]
        own = _remote(w_ref, out_ref.at[2 * x + y], send_sems.at[6], recv_sems.at[6], sib)
        for cp in first + [own]:
            cp.start()
        passed = [_remote(blk(*chip, c), blk(*chip, c), send_sems.at[3 + r], recv_sems.at[3 + r], sib) for r, chip in enumerate(chips)]
        for r, chip in enumerate(chips):
            _remote(my_half, blk(*chip, c), send_sems.at[r], recv_sems.at[r], (*chip, c)).wait_recv()
            passed[r].start()
        for r, chip in enumerate(chips):
            _remote(my_half, blk(*chip, 1 - c), send_sems.at[3 + r], recv_sems.at[3 + r], sib).wait_recv()
        own.wait_recv()
        for cp in first + passed + [own]:
            cp.wait_send()

    return pl.pallas_call(
        body, name=name,
        out_shape=jax.ShapeDtypeStruct((N_CHIPS, R, C), wp.dtype),
        in_specs=[HBM_SPEC], out_specs=HBM_SPEC,
        scratch_shapes=[pltpu.SemaphoreType.DMA((7,)), pltpu.SemaphoreType.DMA((7,))],
    )(wp)


def _ag_ici_start(wp, after, *, name):
    R, C = wp.shape
    H = R // 2

    def body(w_ref, land_ref, after_ref, send_sems, recv_sems, w_thru, land_thru, token):
        x, y, c = _place()
        rows = pl.ds(pl.multiple_of(c * H, 16), H)
        for r, chip in enumerate(_other_chips(x, y)):
            _remote(w_ref.at[rows, :], land_ref.at[2 * x + y, rows, :], send_sems.at[r], recv_sems.at[r], (*chip, c)).start()
        token[...] = jnp.zeros(token.shape, F32)

    land = lax.empty((N_CHIPS, R, C), wp.dtype)
    return pl.pallas_call(
        body, name=name,
        out_shape=(pltpu.SemaphoreType.DMA((3,)), pltpu.SemaphoreType.DMA((3,)), pltpu.HBM(wp.shape, wp.dtype), pltpu.HBM(land.shape, land.dtype),
                   jax.ShapeDtypeStruct((8, LANES), F32)),
        in_specs=(HBM_SPEC, HBM_SPEC, ANY_SPEC), out_specs=(SEM_SPEC, SEM_SPEC, HBM_SPEC, HBM_SPEC, VMEM_SPEC),
        input_output_aliases={0: 2, 1: 3},
        compiler_params=pltpu.CompilerParams(has_side_effects=SPLIT_EFFECT),
    )(pltpu.with_memory_space_constraint(wp, pltpu.HBM), pltpu.with_memory_space_constraint(land, pltpu.HBM), after)


def _ag_ici_wait(send_sems, recv_sems, w_thru, land_thru, after, *, name):
    R, C = w_thru.shape
    H = R // 2

    def body(w_ref, land_ref, send_sems, recv_sems, after_ref, w_out, land_out):
        x, y, c = _place()
        rows = pl.ds(pl.multiple_of(c * H, 16), H)
        for r, (px, py) in enumerate(_other_chips(x, y)):
            cp = _remote(w_ref.at[rows, :], land_ref.at[2 * px + py, rows, :], send_sems.at[r], recv_sems.at[r], (px, py, c))
            cp.wait_send()
            cp.wait_recv()

    return pl.pallas_call(
        body, name=name,
        out_shape=(pltpu.HBM(w_thru.shape, w_thru.dtype), pltpu.HBM(land_thru.shape, land_thru.dtype)),
        in_specs=(HBM_SPEC, HBM_SPEC, SEM_SPEC, SEM_SPEC, ANY_SPEC), out_specs=(HBM_SPEC, HBM_SPEC),
        input_output_aliases={0: 0, 1: 1},
        compiler_params=pltpu.CompilerParams(has_side_effects=SPLIT_EFFECT),
    )(w_thru, land_thru, send_sems, recv_sems, after)


def _ag_sibling_forward(land, wp, *, name):
    _, R, C = land.shape
    H = R // 2

    def body(land_ref, w_ref, out_ref, send_sems, recv_sems):
        x, y, c = _place()
        sib = (x, y, 1 - c)
        mine = pl.ds(pl.multiple_of(c * H, 16), H)
        theirs = pl.ds(pl.multiple_of((1 - c) * H, 16), H)
        chips = _other_chips(x, y)
        sends = [_remote(land_ref.at[2 * px + py, mine, :], out_ref.at[2 * px + py, mine, :], send_sems.at[r], recv_sems.at[r], sib)
                 for r, (px, py) in enumerate(chips)]
        sends.append(_remote(w_ref, out_ref.at[2 * x + y], send_sems.at[3], recv_sems.at[3], sib))
        for cp in sends:
            cp.start()
        for r, (px, py) in enumerate(chips):
            _remote(land_ref.at[2 * px + py, mine, :], out_ref.at[2 * px + py, theirs, :], send_sems.at[r], recv_sems.at[r], sib).wait_recv()
        sends[3].wait_recv()
        for cp in sends:
            cp.wait_send()

    return pl.pallas_call(
        body, name=name, out_shape=jax.ShapeDtypeStruct(land.shape, land.dtype),
        in_specs=[HBM_SPEC, HBM_SPEC], out_specs=HBM_SPEC, input_output_aliases={0: 0},
        scratch_shapes=[pltpu.SemaphoreType.DMA((4,)), pltpu.SemaphoreType.DMA((4,))],
    )(land, wp)


def _rs_sibling_swap(g, *, name):
    _, R, C = g.shape
    H = R // 2

    def body(g_ref, theirs_ref, send_sems, recv_sems):
        x, y, c = _place()
        sib = (x, y, 1 - c)
        copies = [_remote(g_ref.at[k, pl.ds(pl.multiple_of((1 - c) * H, 16), H), :], theirs_ref.at[k],
                          send_sems.at[k], recv_sems.at[k], sib) for k in range(N_CHIPS)]
        for cp in copies:
            cp.start()
        for cp in copies:
            cp.wait()

    return pl.pallas_call(
        body, name=name, out_shape=jax.ShapeDtypeStruct((N_CHIPS, H, C), g.dtype),
        in_specs=[HBM_SPEC], out_specs=HBM_SPEC,
        scratch_shapes=[pltpu.SemaphoreType.DMA((N_CHIPS,)), pltpu.SemaphoreType.DMA((N_CHIPS,))],
    )(g)


def _row_tile(h):
    best = 16
    for d in range(16, 1025, 16):
        if h % d == 0:
            best = d
    return best


def _add2_bf16(g, theirs, core, *, name):
    K, H, C = theirs.shape
    tr = _row_tile(H)
    nb = H // tr

    def body(c_ref, a_ref, b_ref, o_ref):
        o_ref[...] = (a_ref[...].astype(F32) + b_ref[...].astype(F32)).astype(o_ref.dtype)

    spec = pl.BlockSpec((None, tr, C), lambda k, i, c: (k, i, 0))
    return pl.pallas_call(
        body, name=name, out_shape=jax.ShapeDtypeStruct((K, H, C), theirs.dtype),
        grid_spec=pltpu.PrefetchScalarGridSpec(
            num_scalar_prefetch=1, grid=(K, nb),
            in_specs=[pl.BlockSpec((None, tr, C), lambda k, i, c: (k, c[0] * nb + i, 0)), spec], out_specs=spec),
        compiler_params=_cparams("parallel", "parallel"),
    )(core, g, theirs)


def _rs_chip_exchange_start(p, *, name):
    _, H, C = p.shape

    def body(p_ref, land_ref, send_sems, recv_sems, p_thru, land_thru, token):
        x, y, c = _place()
        for r, (px, py) in enumerate(_other_chips(x, y)):
            _remote(p_ref.at[2 * px + py], land_ref.at[r], send_sems.at[r], recv_sems.at[r], (px, py, c)).start()
        token[...] = jnp.zeros(token.shape, F32)

    land = lax.empty((3, H, C), p.dtype)
    return pl.pallas_call(
        body, name=name,
        out_shape=(pltpu.SemaphoreType.DMA((3,)), pltpu.SemaphoreType.DMA((3,)), pltpu.HBM(p.shape, p.dtype), pltpu.HBM(land.shape, land.dtype),
                   jax.ShapeDtypeStruct((8, LANES), F32)),
        in_specs=(HBM_SPEC, HBM_SPEC), out_specs=(SEM_SPEC, SEM_SPEC, HBM_SPEC, HBM_SPEC, VMEM_SPEC),
        input_output_aliases={0: 2, 1: 3},
        compiler_params=pltpu.CompilerParams(has_side_effects=SPLIT_EFFECT),
    )(pltpu.with_memory_space_constraint(p, pltpu.HBM), pltpu.with_memory_space_constraint(land, pltpu.HBM))


def _rs_chip_exchange_wait(send_sems, recv_sems, p_thru, land_thru, after, *, name):
    def body(p_ref, land_ref, send_sems, recv_sems, after_ref, p_out, land_out):
        x, y, c = _place()
        for r, (px, py) in enumerate(_other_chips(x, y)):
            cp = _remote(p_ref.at[2 * px + py], land_ref.at[r], send_sems.at[r], recv_sems.at[r], (px, py, c))
            cp.wait_send()
            cp.wait_recv()

    return pl.pallas_call(
        body, name=name,
        out_shape=(pltpu.HBM(p_thru.shape, p_thru.dtype), pltpu.HBM(land_thru.shape, land_thru.dtype)),
        in_specs=(HBM_SPEC, HBM_SPEC, SEM_SPEC, SEM_SPEC, ANY_SPEC), out_specs=(HBM_SPEC, HBM_SPEC),
        input_output_aliases={0: 0, 1: 1},
        compiler_params=pltpu.CompilerParams(has_side_effects=SPLIT_EFFECT),
    )(p_thru, land_thru, send_sems, recv_sems, after)


def _add4_f32(p, recv, chip_core, *, name):
    _, H, C = p.shape
    tr = _row_tile(H)
    nb = H // tr

    def body(s_ref, o_ref, r_ref, out_ref):
        out_ref[...] = ((o_ref[...].astype(F32) + r_ref[0].astype(F32)) + r_ref[1].astype(F32)) + r_ref[2].astype(F32)

    return pl.pallas_call(
        body, name=name, out_shape=jax.ShapeDtypeStruct((2 * H, C), F32),
        grid_spec=pltpu.PrefetchScalarGridSpec(
            num_scalar_prefetch=1, grid=(nb,),
            in_specs=[pl.BlockSpec((None, tr, C), lambda i, s: (s[0], i, 0)), pl.BlockSpec((3, tr, C), lambda i, s: (0, i, 0))],
            out_specs=pl.BlockSpec((tr, C), lambda i, s: (s[1] * nb + i, 0))),
        compiler_params=_cparams("parallel"),
    )(chip_core, p, recv)


def _rs_sibling_join(f, *, name):
    R, C = f.shape
    H = R // 2

    def body(f_ref, out_ref, send_sem, recv_sem):
        x, y, c = _place()
        sib = (x, y, 1 - c)
        mine = pl.ds(pl.multiple_of(c * H, 8), H)
        theirs = pl.ds(pl.multiple_of((1 - c) * H, 8), H)
        cp = _remote(f_ref.at[mine, :], out_ref.at[mine, :], send_sem, recv_sem, sib)
        cp.start()
        _remote(f_ref.at[mine, :], out_ref.at[theirs, :], send_sem, recv_sem, sib).wait_recv()
        cp.wait_send()

    return pl.pallas_call(
        body, name=name, out_shape=jax.ShapeDtypeStruct((R, C), f.dtype),
        in_specs=[HBM_SPEC], out_specs=HBM_SPEC, input_output_aliases={0: 0},
        scratch_shapes=[pltpu.SemaphoreType.DMA, pltpu.SemaphoreType.DMA],
    )(f)


SMALL_GATHER = (("pool_norm", 2, 256), ("pool_scale", 2, 512), ("conv_w", 3, 512), ("mla_norm", 1, 256),
                ("mla_q_norm", 1, 96), ("mla_kv_norm", 1, 64))
SMALL_SLOT = (16, 512)


def _gather_small(shards, *, name):
    def body(pn_ref, ps_ref, cw_ref, mn_ref, qn_ref, kn_ref, pn_o, ps_o, cw_o, mn_o, qn_o, kn_o, all_ref, send_sems, recv_sems):
        x, y, c = _place()
        mine = 2 * x + y
        all_ref[mine] = jnp.zeros(SMALL_SLOT, F32)
        all_ref[mine, 0:2, 0:256] = pn_ref[...]
        all_ref[mine, 2:4, :] = ps_ref[...]
        all_ref[mine, 4:7, :] = cw_ref[0]
        all_ref[mine, 7:8, 0:256] = mn_ref[...]
        all_ref[mine, 8:9, 0:96] = qn_ref[...]
        all_ref[mine, 9:10, 0:64] = kn_ref[...]
        chips = _other_chips(x, y)
        sends = [_remote(all_ref.at[mine], all_ref.at[mine], send_sems.at[r], recv_sems.at[r], (*chip, c)) for r, chip in enumerate(chips)]
        for cp in sends:
            cp.start()
        for r, (px, py) in enumerate(chips):
            _remote(all_ref.at[mine], all_ref.at[2 * px + py], send_sems.at[r], recv_sems.at[r], (px, py, c)).wait_recv()
        for cp in sends:
            cp.wait_send()
        for k in range(N_CHIPS):
            pn_o[:, k * 256:(k + 1) * 256] = all_ref[k, 0:2, 0:256]
            ps_o[:, k * 512:(k + 1) * 512] = all_ref[k, 2:4, :]
            cw_o[0, :, k * 512:(k + 1) * 512] = all_ref[k, 4:7, :]
            mn_o[:, k * 256:(k + 1) * 256] = all_ref[k, 7:8, 0:256]
            qn_o[k] = all_ref[k, 8:9, 0:96]
            kn_o[k] = all_ref[k, 9:10, 0:64]

    sds = lambda *shape: jax.ShapeDtypeStruct(shape, F32)
    out = pl.pallas_call(
        body, name=name,
        out_shape=(sds(2, 1024), sds(2, 2048), sds(1, 3, 2048), sds(1, 1024), sds(N_CHIPS, 1, 96), sds(N_CHIPS, 1, 64)),
        in_specs=[VMEM_SPEC] * 6, out_specs=(VMEM_SPEC,) * 6,
        scratch_shapes=[pltpu.VMEM((N_CHIPS,) + SMALL_SLOT, F32), pltpu.SemaphoreType.DMA((3,)), pltpu.SemaphoreType.DMA((3,))],
    )(*[shards[n] for n, _, _ in SMALL_GATHER])
    full = dict(zip([n for n, _, _ in SMALL_GATHER], out))
    full["mla_q_norm"] = full["mla_q_norm"].reshape(1, Q_LORA)
    full["mla_kv_norm"] = full["mla_kv_norm"].reshape(1, KV_LORA)
    return full


SMALL_REDUCE = (("pool_norm_0", 0, 1, 1024), ("pool_norm_1", 1, 1, 1024), ("pool_scale_0", 2, 1, 2048), ("pool_scale_1", 3, 1, 2048),
                ("conv_norm", 4, 1, 1024), ("mla_norm", 5, 1, 1024), ("mla_q_norm", 6, 1, 384), ("mla_kv_norm", 7, 1, 256),
                ("conv_w", 8, 8, 2048), ("final_norm", 16, 1, 1024))
REDUCE_SLOT = (24, 2048)


def _reduce_small(parts, after, *, name):
    keys = [k for k, _, _, _ in SMALL_REDUCE]

    def body(*refs):
        ins = dict(zip(keys, refs[:len(keys)]))
        pn_o, ps_o, cn_o, cw_o, mn_o, qn_o, kn_o, fn_o, all_ref, send_sems, recv_sems = refs[len(keys) + 1:]
        x, y, c = _place()
        me = 4 * x + 2 * y + c
        all_ref[me] = jnp.zeros(REDUCE_SLOT, F32)
        for k, r0, nr, wd in SMALL_REDUCE:
            all_ref[me, r0:r0 + nr, 0:wd] = ins[k][...]
        peers = []
        for rel in range(1, N_DEV):
            dx, dy, dc = (rel >> 2) & 1, (rel >> 1) & 1, rel & 1
            peers.append((1 - x if dx else x, 1 - y if dy else y, 1 - c if dc else c))
        sends = [_remote(all_ref.at[me], all_ref.at[me], send_sems.at[k], recv_sems.at[k], peer) for k, peer in enumerate(peers)]
        for cp in sends:
            cp.start()
        for k, (px, py, pc) in enumerate(peers):
            _remote(all_ref.at[me], all_ref.at[4 * px + 2 * py + pc], send_sems.at[k], recv_sems.at[k], (px, py, pc)).wait_recv()
        for cp in sends:
            cp.wait_send()

        def total(r0, nr, wd):
            acc = all_ref[0, r0:r0 + nr, 0:wd]
            for d in range(1, N_DEV):
                acc = acc + all_ref[d, r0:r0 + nr, 0:wd]
            return acc

        pn_o[0:1, :] = total(0, 1, 1024)
        pn_o[1:2, :] = total(1, 1, 1024)
        ps_o[0:1, :] = total(2, 1, 2048)
        ps_o[1:2, :] = total(3, 1, 2048)
        cn_o[...] = total(4, 1, 1024)
        mn_o[...] = total(5, 1, 1024)
        qn_o[...] = total(6, 1, Q_LORA)
        kn_o[...] = total(7, 1, KV_LORA)
        cw_o[0] = total(8, 3, 2048)
        fn_o[...] = total(16, 1, 1024)

    sds = lambda *shape: jax.ShapeDtypeStruct(shape, F32)
    out = pl.pallas_call(
        body, name=name,
        out_shape=(sds(2, 1024), sds(2, 2048), sds(1, 1024), sds(1, 3, 2048), sds(1, 1024), sds(1, Q_LORA), sds(1, KV_LORA), sds(1, 1024)),
        in_specs=[VMEM_SPEC] * len(keys) + [ANY_SPEC], out_specs=(VMEM_SPEC,) * 8,
        scratch_shapes=[pltpu.VMEM((N_DEV,) + REDUCE_SLOT, F32), pltpu.SemaphoreType.DMA((N_DEV - 1,)), pltpu.SemaphoreType.DMA((N_DEV - 1,))],
    )(*[parts[k] for k in keys], after)
    return dict(zip(("pool_norm", "pool_scale", "conv_norm", "conv_w", "mla_norm", "mla_q_norm", "mla_kv_norm", "final_norm"), out))


def _adam_math(w, g, m, v):
    m_new = ADAM_B1 * m + (1.0 - ADAM_B1) * g
    v_new = ADAM_B2 * v + (1.0 - ADAM_B2) * (g * g)
    m_hat = m_new / (1.0 - ADAM_B1 ** ADAM_STEP)
    v_hat = v_new / (1.0 - ADAM_B2 ** ADAM_STEP)
    return -ADAM_LR * (m_hat / (jnp.sqrt(v_hat) + ADAM_EPS) + ADAM_WD * w), m_new, v_new


def _adamw_small(w, m, v, g_full, chip, *, name):
    shp = {n: w[n].shape for n in SMALL}
    whole = lambda s: pl.BlockSpec(s, lambda i, c: (0,) * len(s))
    g_in, g_specs = {}, {}
    for n in SMALL:
        if not SMALL_SHARDED[n]:
            g_in[n], g_specs[n] = g_full[n].reshape(shp[n]), whole(shp[n])
        elif shp[n][-1] % LANES:
            g_in[n] = g_full[n].reshape(N_CHIPS, 1, shp[n][-1])
            g_specs[n] = pl.BlockSpec((None,) + shp[n], lambda i, c: (c[0], 0, 0))
        else:
            g_in[n] = g_full[n]
            nd = len(shp[n])
            g_specs[n] = pl.BlockSpec(shp[n], lambda i, c, nd=nd: (0,) * (nd - 1) + (c[0],))

    def body(c_ref, *refs):
        k = len(SMALL)
        w_r, m_r, v_r, g_r = refs[0:k], refs[k:2 * k], refs[2 * k:3 * k], refs[3 * k:4 * k]
        go_r, d_r, nm_r, nv_r = refs[4 * k:5 * k], refs[5 * k:6 * k], refs[6 * k:7 * k], refs[7 * k:8 * k]
        for i in range(k):
            gv = g_r[i][...]
            d, m_new, v_new = _adam_math(w_r[i][...], gv, m_r[i][...], v_r[i][...])
            go_r[i][...] = gv
            d_r[i][...] = d
            nm_r[i][...] = m_new
            nv_r[i][...] = v_new

    nat = [whole(shp[n]) for n in SMALL]
    out_sds = tuple(jax.ShapeDtypeStruct(shp[n], F32) for n in SMALL)
    out = pl.pallas_call(
        body, name=name, out_shape=out_sds * 4,
        grid_spec=pltpu.PrefetchScalarGridSpec(
            num_scalar_prefetch=1, grid=(1,),
            in_specs=nat * 3 + [g_specs[n] for n in SMALL], out_specs=tuple(nat) * 4),
        compiler_params=_cparams("arbitrary"),
    )(chip, *[w[n] for n in SMALL], *[m[n] for n in SMALL], *[v[n] for n in SMALL], *[g_in[n] for n in SMALL])
    k = len(SMALL)
    return tuple(dict(zip(SMALL, out[j * k:(j + 1) * k])) for j in range(4))


BIG = ("pool_w_in", "pool_w_grp", "pool_w_out", "conv_w_in", "conv_w_out", "mla_w_in", "mla_w_q_up", "mla_w_kv_up", "mla_w_out")
BIG_SHARD_AXIS = {"pool_w_in": 2, "pool_w_grp": 2, "pool_w_out": 1, "conv_w_in": 2, "conv_w_out": 1,
                  "mla_w_in": 2, "mla_w_q_up": 2, "mla_w_kv_up": 2, "mla_w_out": 1}
GATHER_LAYOUT = {
    "p0": ((("pool_w_in", 0), 0, "cols"), (("pool_w_out", 0), 1024, "rows"), (("pool_w_grp", 0), 1536, "flat")),
    "cv": ((("conv_w_in", 0), 0, "cols"), (("conv_w_out", 0), 2048, "rows")),
    "ml": ((("pool_w_in", 1), 0, "cols"), (("mla_w_out", 0), 1024, "rows"), (("pool_w_out", 1), 1536, "rows"),
           (("mla_w_kv_up", 0), 2048, "cols"), (("pool_w_grp", 1), 2304, "flat"), (("mla_w_q_up", 0), 2560, "flat"),
           (("mla_w_in", 0), 2848, "flat")),
}
REDUCE_LAYOUT = {
    "late": ((("conv_w_in", 0), 0, "cols"), (("pool_w_in", 1), 2048, "cols"), (("conv_w_out", 0), 3072, "rows"),
             (("mla_w_out", 0), 3584, "rows"), (("pool_w_out", 1), 4096, "rows"), (("mla_w_kv_up", 0), 4608, "cols"),
             (("pool_w_grp", 1), 4864, "flat"), (("mla_w_q_up", 0), 5120, "flat"), (("mla_w_in", 0), 5408, "flat")),
    "first": ((("pool_w_in", 0), 0, "cols"), (("pool_w_out", 0), 1024, "rows"), (("pool_w_grp", 0), 1536, "flat")),
}
PACK_ROW_ALIGN = 32
RS_ROW_ALIGN = 512


def _slot_rows(layout, shard_shape, align):
    where, end = {}, 0
    for piece, r0, kind in layout:
        n = 1
        for d in shard_shape(piece):
            n *= d
        assert r0 >= end and n % PACK_COLS == 0, (piece, r0, end)
        where[piece] = (r0, n // PACK_COLS, kind)
        end = r0 + n // PACK_COLS
    return end + (-end) % align, where


def _as_slot_rows(shard, kind):
    if kind == "cols":
        k, n = shard.shape
        return shard.reshape(k, n // PACK_COLS, PACK_COLS).swapaxes(0, 1).reshape(-1, PACK_COLS)
    return shard.reshape(-1, PACK_COLS)


def _pack_slot(shards, layout, rows, dtype):
    parts, end = [], 0
    for piece, r0, kind in layout:
        if r0 > end:
            parts.append(jnp.zeros((r0 - end, PACK_COLS), dtype))
        parts.append(_as_slot_rows(shards[piece], kind).astype(dtype))
        end = r0 + parts[-1].shape[0]
    if rows > end:
        parts.append(jnp.zeros((rows - end, PACK_COLS), dtype))
    return jnp.concatenate(parts, axis=0)


SMALL = ("pool_norm", "pool_scale", "conv_norm", "conv_w", "mla_norm", "mla_q_norm", "mla_kv_norm", "final_norm")
SMALL_SHARDED = {"pool_norm": True, "pool_scale": True, "conv_norm": False, "conv_w": True, "mla_norm": True,
                 "mla_q_norm": True, "mla_kv_norm": True, "final_norm": False}


def _rope_tables(positions):
    inv_freq = ROPE_BASE ** (-jnp.arange(0, QK_ROPE, 2, dtype=F32) / QK_ROPE)
    ang = positions.astype(F32).reshape(-1, 1) * inv_freq
    cos, sin = jnp.cos(ang), jnp.sin(ang)
    z32 = jnp.zeros_like(cos)
    z64 = jnp.concatenate([z32, z32], axis=1)
    return (jnp.concatenate([cos, cos, z64], axis=1), jnp.concatenate([-sin, z32, z64], axis=1),
            jnp.concatenate([z32, sin, z64], axis=1))


def _mla_in_to_padded(w):
    q, kv, kr, z = w[:, :Q_LORA], w[:, Q_LORA:Q_LORA + KV_LORA], w[:, Q_LORA + KV_LORA:Q_LORA + KV_LORA + QK_ROPE], w[:, Q_LORA + KV_LORA + QK_ROPE:]
    return jnp.concatenate([z, kv, q, kr, jnp.zeros((w.shape[0], MLA_IN_PAD - MLA_IN), w.dtype)], axis=1)


def _mla_in_from_padded(w):
    z, kv, q, kr = w[:, :D_INNER], w[:, D_INNER:D_INNER + KV_LORA], w[:, D_INNER + KV_LORA:D_INNER + KV_LORA + Q_LORA], w[:, D_INNER + KV_LORA + Q_LORA:D_INNER + KV_LORA + Q_LORA + QK_ROPE]
    return jnp.concatenate([q, kv, kr, z], axis=1)


def _q_up_to_padded(w):
    k = w.shape[0]
    return jnp.pad(w.reshape(k, N_HEADS, QK_NOPE + QK_ROPE), ((0, 0), (0, 0), (0, HEAD_PAD - QK_NOPE - QK_ROPE))).reshape(k, N_HEADS * HEAD_PAD)


def _q_up_from_padded(w):
    k = w.shape[0]
    return w.reshape(k, N_HEADS, HEAD_PAD)[:, :, :QK_NOPE + QK_ROPE].reshape(k, N_HEADS * (QK_NOPE + QK_ROPE))


def _local_step(x, positions, target, weights_for, ws, sink):
    S = x.shape[0]
    tm = min(512, S)
    te = min(256, S)
    tq = min(512, S)
    tabs = _rope_tables(positions)
    gs = {}

    def mm_in(xn, w, name, after=None):
        n = w.shape[1]
        tn = PACK_COLS if isinstance(w, Packed) else _pick(n, 1536 if n == MLA_IN_PAD else 1024)
        return _mm(xn, w, after=after, tm=min(1024, S), tn=tn, tk=D_MODEL, name=name)

    def mm_out(y, w, res, name):
        return _mm(y, w, residual=res, tm=tm, tn=D_MODEL, tk=D_INNER, name=name)

    def mm_dx(dy, w, name, after=None):
        k, n = w.shape
        if isinstance(w, Packed):
            tn, tk = (k if w.kind == "rows" else min(k, 1024)), PACK_COLS
        else:
            tn, tk = _pick(k, 1024), _pick(n, 1408)
        return _mm(dy, w, trans_b=True, after=after, tm=min(1024, S), tn=tn, tk=tk, name=name)

    def mm_dw(piece, a, b, name, after=None, post=None):
        ka, nb = a.shape[1], b.shape[1]
        into = sink.dest(piece)
        if into is None:
            out = _mm(a, b, trans_a=True, out_dtype=BF16, after=after, tm=_pick(ka, 1024), tn=_pick(nb, 1408), tk=tm, name=name)
            sink.put(piece, out if post is None else post(out))
        else:
            rows = ka if into.kind == "rows" else min(ka, 1024)
            sink.put(piece, _mm(a, b, trans_a=True, after=after, into=into, tm=rows, tn=PACK_COLS, tk=tm, name=name))

    def pool_layer_fwd(xin, wts, j, tag):
        xn = _rms_fwd(xin, ws["pool_norm"][j:j + 1], tm=tm, name=f"{tag}_norm")
        h = mm_in(xn, wts[("pool_w_in", j)], f"{tag}_in", wts.get("after"))
        y = _pool_fwd(h, wts[("pool_w_grp", j)], ws["pool_scale"][j:j + 1], tm=te, name=f"{tag}_mix")
        xo = mm_out(y, wts[("pool_w_out", j)], xin, f"{tag}_out")
        return xo, (xin, xn, h, y)

    def pool_layer_bwd(dx, dxb, saved, wts, j, tag, after=None):
        xin, xn, h, y = saved
        dy = mm_dx(dxb, wts[("pool_w_out", j)], f"{tag}_dy", after)
        mm_dw(("pool_w_out", j), y, dxb, f"{tag}_dwo", after)
        pooled, dmixed, dpooled, dz, dsc = _pool_bwd1(h, dy, wts[("pool_w_grp", j)], ws["pool_scale"][j:j + 1], tm=te, name=f"{tag}_bmix")
        sink.put(("pool_w_grp", j), _grouped_tn(pooled, dmixed, tk=tm, name=f"{tag}_dwg"))
        dh = _pool_bwd2(dpooled, dz, tm=te, name=f"{tag}_bshift")
        dxn = mm_dx(dh, wts[("pool_w_in", j)], f"{tag}_dxn")
        mm_dw(("pool_w_in", j), xn, dh, f"{tag}_dwi")
        dxo, dxob, dg = _rms_bwd(xin, ws["pool_norm"][j:j + 1], dxn, dx, tm=tm, name=f"{tag}_bnorm")
        gs[f"pool_norm_{j}"], gs[f"pool_scale_{j}"] = dg, dsc
        return dxo, dxob

    w_p0 = weights_for("p0", None)
    x1, sv0 = pool_layer_fwd(x, w_p0, 0, "p0")

    w_cv = weights_for("cv", x1)
    xn1 = _rms_fwd(x1, ws["conv_norm"][0:1], tm=tm, name="cv_norm")
    h1 = mm_in(xn1, w_cv[("conv_w_in", 0)], "cv_in")
    cw = jnp.pad(ws["conv_w"][0], ((0, 5), (0, 0)))
    y1 = _conv_fwd(h1, cw, tm=te, name="cv_mix")
    x2 = mm_out(y1, w_cv[("conv_w_out", 0)], x1, "cv_out")

    w_ml = weights_for("ml", x2)
    w_mi = _mla_in_to_padded(w_ml[("mla_w_in", 0)])
    w_q = _q_up_to_padded(w_ml[("mla_w_q_up", 0)])
    w_kv = w_ml[("mla_w_kv_up", 0)]
    qg, kvg = ws["mla_q_norm"][0:1], ws["mla_kv_norm"][0:1]
    xn2 = _rms_fwd(x2, ws["mla_norm"][0:1], tm=tm, name="ml_norm")
    h2 = mm_in(xn2, w_mi, "ml_in")
    q_n, kv_n, krr = _mla_latent_fwd(h2, qg, kvg, tabs, tm=tm, name="ml_lat")
    q_full = _mla_q_up(q_n, w_q, tabs, tm=tm, name="ml_qup")
    k_full, v = _mla_kv_up(kv_n, w_kv, krr, tm=tm, name="ml_kvup")
    o, y2, lse = _flash_fwd(q_full, k_full, v, h2, tq=tq, name="ml_attn")
    x3 = mm_out(y2, w_ml[("mla_w_out", 0)], x2, "ml_out")

    x4, sv3 = pool_layer_fwd(x3, w_ml, 1, "p1")

    loss_part, dx, dxb, dgf = _final_loss(x4, ws["final_norm"].reshape(1, -1), target, tm=tm, name="final")
    gs["final_norm"] = dgf

    dx, dxb = pool_layer_bwd(dx, dxb, sv3, w_ml, 1, "p1")

    dy = mm_dx(dxb, w_ml[("mla_w_out", 0)], "ml_dy")
    mm_dw(("mla_w_out", 0), y2, dxb, "ml_dwo")
    do, dz, delta = _mla_gate_bwd(dy, o, h2, tm=tq, name="ml_bgate")
    dq, dkv, dkr = _flash_bwd(q_full, k_full, v, do, lse, delta, tq=tq, name="ml_battn")
    dq_pre = _mla_unrope_q(dq, tabs, tm=te, name="ml_bqrope")
    dq_n = mm_dx(dq_pre, w_q, "ml_dqn")
    mm_dw(("mla_w_q_up", 0), q_n, dq_pre, "ml_dwq", post=_q_up_from_padded)
    dkv_n = mm_dx(dkv, w_kv, "ml_dkvn")
    mm_dw(("mla_w_kv_up", 0), kv_n, dkv, "ml_dwkv")
    dh2, dqg, dkvg = _mla_latent_bwd(h2, dq_n, dkv_n, dkr, dz, qg, kvg, tabs, tm=te, name="ml_blat")
    dxn2 = mm_dx(dh2, w_mi, "ml_dxn")
    mm_dw(("mla_w_in", 0), xn2, dh2, "ml_dwi", post=_mla_in_from_padded)
    dx, dxb, dg2 = _rms_bwd(x2, ws["mla_norm"][0:1], dxn2, dx, tm=tm, name="ml_bnorm")
    gs["mla_norm"], gs["mla_q_norm"], gs["mla_kv_norm"] = dg2, dqg, dkvg

    dy = mm_dx(dxb, w_cv[("conv_w_out", 0)], "cv_dy")
    mm_dw(("conv_w_out", 0), y1, dxb, "cv_dwo")
    dh1, dcw = _conv_bwd(h1, dy, cw, tm=te, name="cv_bmix")
    dxn1 = mm_dx(dh1, w_cv[("conv_w_in", 0)], "cv_dxn")
    mm_dw(("conv_w_in", 0), xn1, dh1, "cv_dwi")
    dx, dxb, dg1 = _rms_bwd(x1, ws["conv_norm"][0:1], dxn1, dx, tm=tm, name="cv_bnorm")
    gs["conv_norm"], gs["conv_w"] = dg1, dcw

    dx, dxb = pool_layer_bwd(dx, dxb, sv0, w_p0, 0, "p0", after=sink.late_ready())
    return loss_part, dx, gs


def kernel(x, positions, pool_norm, pool_w_in, pool_w_grp, pool_scale, pool_w_out, conv_norm, conv_w_in, conv_w, conv_w_out, mla_norm, mla_w_in, mla_q_norm, mla_w_q_up, mla_kv_norm, mla_w_kv_up, mla_w_out, final_norm, loss_target, m_pool_norm, m_pool_w_in, m_pool_w_grp, m_pool_scale, m_pool_w_out, m_conv_norm, m_conv_w_in, m_conv_w, m_conv_w_out, m_mla_norm, m_mla_w_in, m_mla_q_norm, m_mla_w_q_up, m_mla_kv_norm, m_mla_w_kv_up, m_mla_w_out, m_final_norm, v_pool_norm, v_pool_w_in, v_pool_w_grp, v_pool_scale, v_pool_w_out, v_conv_norm, v_conv_w_in, v_conv_w, v_conv_w_out, v_mla_norm, v_mla_w_in, v_mla_q_norm, v_mla_w_q_up, v_mla_kv_norm, v_mla_w_kv_up, v_mla_w_out, v_final_norm):
    names = ("pool_norm", "pool_w_in", "pool_w_grp", "pool_scale", "pool_w_out", "conv_norm", "conv_w_in", "conv_w", "conv_w_out",
             "mla_norm", "mla_w_in", "mla_q_norm", "mla_w_q_up", "mla_kv_norm", "mla_w_kv_up", "mla_w_out", "final_norm")
    w = dict(zip(names, (pool_norm, pool_w_in, pool_w_grp, pool_scale, pool_w_out, conv_norm, conv_w_in, conv_w, conv_w_out,
                         mla_norm, mla_w_in, mla_q_norm, mla_w_q_up, mla_kv_norm, mla_w_kv_up, mla_w_out, final_norm)))
    m = dict(zip(names, (m_pool_norm, m_pool_w_in, m_pool_w_grp, m_pool_scale, m_pool_w_out, m_conv_norm, m_conv_w_in, m_conv_w, m_conv_w_out,
                         m_mla_norm, m_mla_w_in, m_mla_q_norm, m_mla_w_q_up, m_mla_kv_norm, m_mla_w_kv_up, m_mla_w_out, m_final_norm)))
    v = dict(zip(names, (v_pool_norm, v_pool_w_in, v_pool_w_grp, v_pool_scale, v_pool_w_out, v_conv_norm, v_conv_w_in, v_conv_w, v_conv_w_out,
                         v_mla_norm, v_mla_w_in, v_mla_q_norm, v_mla_w_q_up, v_mla_kv_norm, v_mla_w_kv_up, v_mla_w_out, v_final_norm)))
    chip = 2 * lax.axis_index("x") + lax.axis_index("y")
    core = lax.axis_index("c")

    core1 = core.astype(jnp.int32).reshape(1)
    chip_core = jnp.stack([chip, core]).astype(jnp.int32)
    shard_shape = lambda piece: w[piece[0]].shape[1:]
    shard_axis = lambda piece: BIG_SHARD_AXIS[piece[0]] - 1
    full_shape = lambda piece: tuple(d * (N_CHIPS if a == shard_axis(piece) else 1) for a, d in enumerate(shard_shape(piece)))

    w_bf = {n: w[n].astype(BF16) for n in BIG}
    gather_rows, gather_at, packs = {}, {}, {}
    for grp, layout in GATHER_LAYOUT.items():
        gather_rows[grp], gather_at[grp] = _slot_rows(layout, shard_shape, PACK_ROW_ALIGN)
        packs[grp] = _pack_slot({(n, j): w_bf[n][j] for (n, j), _, _ in layout}, layout, gather_rows[grp], BF16)

    def gathered_weights(grp, gathered):
        out = {}
        for piece, (r0, n, kind) in gather_at[grp].items():
            if kind == "flat":
                out[piece] = jnp.concatenate([gathered[k, r0:r0 + n].reshape(shard_shape(piece)) for k in range(N_CHIPS)], axis=shard_axis(piece))
            else:
                out[piece] = Packed(gathered, r0, kind, full_shape(piece))
        return out

    gathered_p0 = _all_gather_big(packs["p0"], name="ag_p0")
    cv_start = _ag_ici_start(packs["cv"], gathered_p0, name="ag_cv_start")
    ml_start = _ag_ici_start(packs["ml"], cv_start[4], name="ag_ml_start")
    in_flight = {"cv": cv_start, "ml": ml_start}

    def weights_for(grp, after):
        if grp == "p0":
            return {**gathered_weights(grp, gathered_p0), "after": ml_start[4]}
        send_sems, recv_sems, w_thru, land, _ = in_flight[grp]
        w_thru, land = _ag_ici_wait(send_sems, recv_sems, w_thru, land, after, name=f"ag_{grp}_wait")
        return gathered_weights(grp, _ag_sibling_forward(land, w_thru, name=f"ag_{grp}_fwd"))

    ws = {"conv_norm": w["conv_norm"], "final_norm": w["final_norm"]}
    ws.update(_gather_small({n: w[n] for n, _, _ in SMALL_GATHER}, name="ag_small"))

    reduce_rows, reduce_at = {}, {}
    for grp, layout in REDUCE_LAYOUT.items():
        reduce_rows[grp], reduce_at[grp] = _slot_rows(layout, shard_shape, RS_ROW_ALIGN)
    group_of = {piece: grp for grp, layout in REDUCE_LAYOUT.items() for piece, _, _ in layout}

    class Sink:
        def __init__(self):
            self.buf = {grp: jnp.zeros((N_CHIPS, rows, PACK_COLS), BF16) for grp, rows in reduce_rows.items()}
            self.started = {}

        def dest(self, piece):
            grp = group_of[piece]
            r0, _, kind = reduce_at[grp][piece]
            return None if kind == "flat" else Packed(self.buf[grp], r0, kind, full_shape(piece))

        def put(self, piece, result):
            grp = group_of[piece]
            r0, n, kind = reduce_at[grp][piece]
            if kind == "flat":
                parts = jnp.split(result, N_CHIPS, axis=shard_axis(piece))
                result = lax.dynamic_update_slice(self.buf[grp], jnp.stack([p.reshape(n, PACK_COLS) for p in parts]), (0, r0, 0))
            self.buf[grp] = result

        def start(self, grp, tag):
            theirs = _rs_sibling_swap(self.buf[grp], name=f"{tag}_swap")
            chip_sum = _add2_bf16(self.buf[grp], theirs, core1, name=f"{tag}_add2")
            self.started[grp] = _rs_chip_exchange_start(chip_sum, name=f"{tag}_chips_start")
            return self.started[grp][4]

        def finish(self, grp, after, tag):
            send_sems, recv_sems, chip_sum, land, _ = self.started[grp]
            chip_sum, recv = _rs_chip_exchange_wait(send_sems, recv_sems, chip_sum, land, after, name=f"{tag}_chips_wait")
            half_sum = _add4_f32(chip_sum, recv, chip_core, name=f"{tag}_add4")
            return _rs_sibling_join(half_sum, name=f"{tag}_join")

        def late_ready(self):
            return self.start("late", "rsa")

    sink = Sink()

    loss_part, grad_x, gs = _local_step(x[0], positions, loss_target[0], weights_for, ws, sink)
    loss = lax.psum(loss_part[0, 0], ("x", "y", "c"))

    g, delta, new_m, new_v = {}, {}, {}, {}

    def adam_big(n):
        nj = w[n].shape[0]
        where = [(group_of[(n, j)],) + reduce_at[group_of[(n, j)]][(n, j)] for j in range(nj)]
        if where[0][3] == "flat":
            g[n] = jnp.stack([g_rows[grp][r0:r0 + rows].reshape(w[n].shape[1:]) for grp, r0, rows, _ in where])
            shp = w[n].shape
            two_d = lambda a: a.reshape(-1, shp[-1])
            d_, m_, v_ = _adamw(two_d(w[n]), two_d(g[n]), two_d(m[n]), two_d(v[n]), name=f"adamw_{n}")
            delta[n], new_m[n], new_v[n] = d_.reshape(shp), m_.reshape(shp), v_.reshape(shp)
        else:
            g[n], delta[n], new_m[n], new_v[n] = _adamw_rows(w[n], m[n], v[n], [(g_rows[grp], r0) for grp, r0, _, _ in where], name=f"adamw_{n}")

    first_token = sink.start("first", "rsb")
    g_rows = {"late": sink.finish("late", first_token, "rsa")}
    late_only = [n for n in BIG if all(group_of[(n, j)] == "late" for j in range(w[n].shape[0]))]
    for n in late_only:
        adam_big(n)
    g_rows["first"] = sink.finish("first", delta[late_only[-1]], "rsb")
    for n in BIG:
        if n not in late_only:
            adam_big(n)

    gs_sum = _reduce_small(gs, g_rows["first"], name="ar_small")
    row = lambda d: {n: (d[n].reshape(1, -1) if d[n].ndim == 1 else d[n]) for n in SMALL}
    small_out = _adamw_small(row(w), row(m), row(v), gs_sum, chip.astype(jnp.int32).reshape(1), name="adamw_small")
    for dst, res in zip((g, delta, new_m, new_v), small_out):
        for n in SMALL:
            dst[n] = res[n].reshape(w[n].shape)

    return (loss, grad_x[None], *[g[n] for n in names], *[delta[n] for n in names],
            *[new_m[n] for n in names], *[new_v[n] for n in names])
```

```python
import functools

import jax
import jax.numpy as jnp
from jax import lax
from jax.experimental import pallas as pl
from jax.experimental.pallas import tpu as pltpu

F32 = jnp.float32
BF16 = jnp.bfloat16

D_MODEL = 1024
D_INNER = 2048
POOL_WINDOWS = (2, 4, 8, 16)
POOL_GROUP = 512
N_HEADS = 16
QK_NOPE = 128
QK_ROPE = 64
V_DIM = 128
HEAD_PAD = 256
Q_LORA = 384
KV_LORA = 256
MLA_IN = Q_LORA + KV_LORA + QK_ROPE + D_INNER
MLA_IN_PAD = 2816
ATTN_SCALE = (QK_NOPE + QK_ROPE) ** -0.5
ROPE_BASE = 10000.0
NORM_EPS = 1e-6
HALO = 16

ADAM_LR = 0.001
ADAM_B1 = 0.9
ADAM_B2 = 0.999
ADAM_EPS = 1e-08
ADAM_WD = 0.01
ADAM_STEP = 10

N_CHIPS = 4
N_DEV = 8
LANES = 128
PACK_COLS = 1024
V7X_VMEM_LIMIT = 56 * 1024 * 1024
MESH = pl.DeviceIdType.MESH


def _cparams(*sem):
    return pltpu.CompilerParams(dimension_semantics=sem, vmem_limit_bytes=V7X_VMEM_LIMIT)


def _pick(n, cap):
    best = None
    for d in range(LANES, min(n, cap) + 1, LANES):
        if n % d == 0:
            best = d
    assert best is not None, (n, cap)
    return best


def _sigmoid(z):
    return 1.0 / (1.0 + jnp.exp(-z))


class Packed:
    def __init__(self, buf, r0, kind, shape):
        self.buf, self.r0, self.kind, self.shape = buf, r0, kind, shape

    def block(self, rb, cb, bk):
        assert self.r0 % bk == 0, (self.r0, bk)
        if self.kind == "cols":
            K = self.shape[0]
            per = self.shape[1] // (N_CHIPS * PACK_COLS)
            assert K % bk == 0
            return cb // per, (self.r0 + (cb % per) * K) // bk + rb
        kk = self.shape[0] // N_CHIPS
        assert kk % bk == 0
        per = kk // bk
        return rb // per, self.r0 // bk + rb % per


def _mm(a, b, *, trans_a=False, trans_b=False, out_dtype=F32, residual=None, after=None, into=None, tm, tn, tk, name):
    if trans_a:
        K, M = a.shape
    else:
        M, K = a.shape
    if trans_b:
        N, K2 = b.shape
    else:
        K2, N = b.shape
    assert K == K2 and M % tm == 0 and N % tn == 0 and K % tk == 0, (name, a.shape, b.shape, tm, tn, tk)
    nk = K // tk
    dn = (((0 if trans_a else 1,), (1 if trans_b else 0,)), ((), ()))
    has_res = residual is not None
    n_skip = (after is not None) + (into is not None)
    b_all = isinstance(b, Packed) and b.kind == "rows"
    o_all = into is not None and into.kind == "rows"
    assert not b_all or (not trans_a and (tn == N if trans_b else tk == K)), name
    assert not o_all or tm == M, name

    def body(*refs):
        if has_res:
            a_ref, b_ref, r_ref = refs[:3]
            refs = refs[3:]
        else:
            a_ref, b_ref = refs[:2]
            r_ref = None
            refs = refs[2:]
        refs = refs[n_skip:]
        o_ref, rest = refs[0], refs[1:]
        if b_all and trans_b:
            part = jnp.concatenate([lax.dot_general(a_ref[...], b_ref[c], dn, preferred_element_type=F32) for c in range(N_CHIPS)], axis=1)
        elif b_all:
            kk = K // N_CHIPS
            part = lax.dot_general(a_ref[:, 0:kk], b_ref[0], dn, preferred_element_type=F32)
            for c in range(1, N_CHIPS):
                part = part + lax.dot_general(a_ref[:, c * kk:(c + 1) * kk], b_ref[c], dn, preferred_element_type=F32)
        else:
            part = lax.dot_general(a_ref[...], b_ref[...], dn, preferred_element_type=F32)

        def finish(acc):
            if has_res:
                acc = acc + r_ref[...]
            if o_all:
                mk = M // N_CHIPS
                for c in range(N_CHIPS):
                    o_ref[c] = acc[c * mk:(c + 1) * mk].astype(o_ref.dtype)
            else:
                o_ref[...] = acc.astype(o_ref.dtype)

        if nk == 1:
            finish(part)
        else:
            acc_ref = rest[0]
            k = pl.program_id(2)

            @pl.when(k == 0)
            def _():
                acc_ref[...] = part

            @pl.when(k > 0)
            def _():
                acc_ref[...] += part

            @pl.when(k == nk - 1)
            def _():
                finish(acc_ref[...])

    a_spec = pl.BlockSpec((tk, tm), lambda i, j, k: (k, i)) if trans_a else pl.BlockSpec((tm, tk), lambda i, j, k: (i, k))
    if b_all:
        kkb = b.shape[0] // N_CHIPS
        b_spec = pl.BlockSpec((N_CHIPS, kkb, PACK_COLS), lambda i, j, k: (0, b.r0 // kkb, 0))
        b_arg = b.buf
    elif isinstance(b, Packed):
        if trans_b:
            assert tk == PACK_COLS
            b_spec = pl.BlockSpec((None, tn, tk), lambda i, j, k: (*b.block(j, k, tn), 0))
        else:
            assert tn == PACK_COLS
            b_spec = pl.BlockSpec((None, tk, tn), lambda i, j, k: (*b.block(k, j, tk), 0))
        b_arg = b.buf
    else:
        b_spec = pl.BlockSpec((tn, tk), lambda i, j, k: (j, k)) if trans_b else pl.BlockSpec((tk, tn), lambda i, j, k: (k, j))
        b_arg = b
    o_spec = pl.BlockSpec((tm, tn), lambda i, j, k: (i, j))
    in_specs = [a_spec, b_spec] + ([o_spec] if has_res else [])
    args = (a, b_arg) + ((residual,) if has_res else ())
    aliases = {}
    if after is not None:
        in_specs.append(pl.BlockSpec(memory_space=pl.ANY))
        args += (after,)
    if into is None:
        out_shape, out_spec = jax.ShapeDtypeStruct((M, N), out_dtype), o_spec
    else:
        assert tn == PACK_COLS and into.shape == (M, N)
        in_specs.append(pl.BlockSpec(memory_space=pl.ANY))
        aliases = {len(args): 0}
        args += (into.buf,)
        out_shape = jax.ShapeDtypeStruct(into.buf.shape, into.buf.dtype)
        if o_all:
            out_spec = pl.BlockSpec((N_CHIPS, M // N_CHIPS, tn), lambda i, j, k: (0, into.r0 // (M // N_CHIPS), 0))
        else:
            out_spec = pl.BlockSpec((None, tm, tn), lambda i, j, k: (*into.block(i, j, tm), 0))
    return pl.pallas_call(
        body, name=name, out_shape=out_shape,
        grid=(M // tm, N // tn, nk),
        in_specs=in_specs, out_specs=out_spec, input_output_aliases=aliases,
        scratch_shapes=[pltpu.VMEM((tm, tn), F32)] if nk > 1 else [],
        compiler_params=_cparams("parallel", "parallel", "arbitrary"),
    )(*args)


def _grouped_tn(a, b, *, tk, name):
    S = a.shape[0]
    G = POOL_GROUP
    nk = S // tk

    def body(a_ref, b_ref, o_ref, acc_ref):
        k = pl.program_id(1)
        part = lax.dot_general(a_ref[...], b_ref[...], (((0,), (0,)), ((), ())), preferred_element_type=F32)

        @pl.when(k == 0)
        def _():
            acc_ref[...] = part

        @pl.when(k > 0)
        def _():
            acc_ref[...] += part

        @pl.when(k == nk - 1)
        def _():
            o_ref[...] = acc_ref[...].astype(o_ref.dtype)

    return pl.pallas_call(
        body, name=name,
        out_shape=jax.ShapeDtypeStruct((len(POOL_WINDOWS), G, G), BF16),
        grid=(len(POOL_WINDOWS), nk),
        in_specs=[pl.BlockSpec((tk, G), lambda g, k: (k, g)), pl.BlockSpec((tk, G), lambda g, k: (k, g))],
        out_specs=pl.BlockSpec((None, G, G), lambda g, k: (g, 0, 0)),
        scratch_shapes=[pltpu.VMEM((G, G), F32)],
        compiler_params=_cparams("parallel", "arbitrary"),
    )(a, b)


def _rms_fwd(x, g, *, tm, name):
    S, D = x.shape

    def body(x_ref, g_ref, o_ref):
        xv = x_ref[...]
        rstd = lax.rsqrt(jnp.mean(xv * xv, axis=-1, keepdims=True) + NORM_EPS)
        o_ref[...] = (xv * rstd * g_ref[...]).astype(o_ref.dtype)

    return pl.pallas_call(
        body, name=name,
        out_shape=jax.ShapeDtypeStruct((S, D), BF16),
        grid=(S // tm,),
        in_specs=[pl.BlockSpec((tm, D), lambda i: (i, 0)), pl.BlockSpec((1, D), lambda i: (0, 0))],
        out_specs=pl.BlockSpec((tm, D), lambda i: (i, 0)),
        compiler_params=_cparams("parallel"),
    )(x, g)


def _rms_bwd_math(xv, gv, dxn):
    rstd = lax.rsqrt(jnp.mean(xv * xv, axis=-1, keepdims=True) + NORM_EPS)
    xh = xv * rstd
    dg = jnp.sum(dxn * xh, axis=0, keepdims=True)
    dxh = dxn * gv
    dx = rstd * (dxh - xh * jnp.mean(dxh * xh, axis=-1, keepdims=True))
    return dx, dg


def _rms_bwd(x, g, dxn, dres, *, tm, name):
    S, D = x.shape

    def body(x_ref, g_ref, dxn_ref, dres_ref, dx_ref, dxb_ref, dg_ref):
        dx, dg = _rms_bwd_math(x_ref[...], g_ref[...], dxn_ref[...])
        dx = dx + dres_ref[...]
        dx_ref[...] = dx
        dxb_ref[...] = dx.astype(BF16)

        @pl.when(pl.program_id(0) == 0)
        def _():
            dg_ref[...] = dg

        @pl.when(pl.program_id(0) > 0)
        def _():
            dg_ref[...] += dg

    row = pl.BlockSpec((tm, D), lambda i: (i, 0))
    vec = pl.BlockSpec((1, D), lambda i: (0, 0))
    return pl.pallas_call(
        body, name=name,
        out_shape=(jax.ShapeDtypeStruct((S, D), F32), jax.ShapeDtypeStruct((S, D), BF16), jax.ShapeDtypeStruct((1, D), F32)),
        grid=(S // tm,),
        in_specs=[row, vec, row, row],
        out_specs=(row, row, vec),
        compiler_params=_cparams("arbitrary"),
    )(x, g, dxn, dres)


def _final_loss(x, g, target, *, tm, name):
    S, D = x.shape

    def body(x_ref, g_ref, t_ref, loss_ref, dx_ref, dxb_ref, dg_ref):
        xv = x_ref[...]
        gv = g_ref[...]
        rstd = lax.rsqrt(jnp.mean(xv * xv, axis=-1, keepdims=True) + NORM_EPS)
        xh = xv * rstd
        err = xh * gv - t_ref[...]
        part = 0.5 * jnp.sum(jnp.mean(err * err, axis=-1, keepdims=True), axis=0, keepdims=True)
        dy = err * (1.0 / D)
        dg = jnp.sum(dy * xh, axis=0, keepdims=True)
        dxh = dy * gv
        dx = rstd * (dxh - xh * jnp.mean(dxh * xh, axis=-1, keepdims=True))
        dx_ref[...] = dx
        dxb_ref[...] = dx.astype(BF16)
        lossb = jnp.broadcast_to(part, loss_ref.shape)

        @pl.when(pl.program_id(0) == 0)
        def _():
            dg_ref[...] = dg
            loss_ref[...] = lossb

        @pl.when(pl.program_id(0) > 0)
        def _():
            dg_ref[...] += dg
            loss_ref[...] += lossb

    row = pl.BlockSpec((tm, D), lambda i: (i, 0))
    vec = pl.BlockSpec((1, D), lambda i: (0, 0))
    lspec = pl.BlockSpec((8, LANES), lambda i: (0, 0))
    return pl.pallas_call(
        body, name=name,
        out_shape=(jax.ShapeDtypeStruct((8, LANES), F32), jax.ShapeDtypeStruct((S, D), F32),
                   jax.ShapeDtypeStruct((S, D), BF16), jax.ShapeDtypeStruct((1, D), F32)),
        grid=(S // tm,),
        in_specs=[row, vec, row],
        out_specs=(lspec, row, row, vec),
        compiler_params=_cparams("arbitrary"),
    )(x, g, target)


def _prev_halo_spec(tm, width, col):
    r = tm // HALO
    return pl.BlockSpec((HALO, width), lambda i: (jnp.maximum(i * r - 1, 0), col))


def _next_halo_spec(tm, width, col, S):
    r = tm // HALO
    last = S // HALO - 1
    return pl.BlockSpec((HALO, width), lambda i: (jnp.minimum((i + 1) * r, last), col))


def _shift_down(ext, k):
    return pltpu.roll(ext, k, 0)[HALO:, :]


def _shift_up(ext, k, tm):
    n = ext.shape[0]
    return pltpu.roll(ext, n - k, 0)[:tm, :]


def _pool_window_sum(ext, w):
    s = ext
    k = 1
    while k < w:
        s = s + pltpu.roll(s, k, 0)
        k *= 2
    return s[HALO:, :]


def _pooled_group(u_ref, halo, g, w, t_idx):
    cs = slice(g * POOL_GROUP, (g + 1) * POOL_GROUP)
    u = u_ref[:, cs]
    ext = jnp.concatenate([halo[:, cs], u], axis=0)
    inv = 1.0 / jnp.minimum(t_idx + 1, w).astype(F32)
    return _pool_window_sum(ext, w) * inv - u


def _pool_fwd(h, w_grp, scale, *, tm, name):
    S = h.shape[0]
    E = D_INNER

    def body(u_ref, uh_ref, z_ref, wg_ref, sc_ref, y_ref):
        i = pl.program_id(0)
        halo = jnp.where(i > 0, uh_ref[...], 0.0)
        t_idx = i * tm + lax.broadcasted_iota(jnp.int32, (tm, 1), 0)
        for g, w in enumerate(POOL_WINDOWS):
            cs = slice(g * POOL_GROUP, (g + 1) * POOL_GROUP)
            pooled = _pooled_group(u_ref, halo, g, w, t_idx)
            mixed = jnp.dot(pooled.astype(BF16), wg_ref[g], preferred_element_type=F32)
            z = z_ref[:, cs]
            y_ref[:, cs] = (mixed * sc_ref[:, cs] * (z * _sigmoid(z))).astype(BF16)

    return pl.pallas_call(
        body, name=name,
        out_shape=jax.ShapeDtypeStruct((S, E), BF16),
        grid=(S // tm,),
        in_specs=[pl.BlockSpec((tm, E), lambda i: (i, 0)), _prev_halo_spec(tm, E, 0),
                  pl.BlockSpec((tm, E), lambda i: (i, 1)),
                  pl.BlockSpec((len(POOL_WINDOWS), POOL_GROUP, POOL_GROUP), lambda i: (0, 0, 0)),
                  pl.BlockSpec((1, E), lambda i: (0, 0))],
        out_specs=pl.BlockSpec((tm, E), lambda i: (i, 0)),
        compiler_params=_cparams("parallel"),
    )(h, h, h, w_grp, scale)


def _pool_bwd1(h, dy, w_grp, scale, *, tm, name):
    S = h.shape[0]
    E = D_INNER

    def body(u_ref, uh_ref, z_ref, dy_ref, wg_ref, sc_ref, pooled_ref, dmixed_ref, dpooled_ref, dz_ref, dsc_ref):
        i = pl.program_id(0)
        halo = jnp.where(i > 0, uh_ref[...], 0.0)
        t_idx = i * tm + lax.broadcasted_iota(jnp.int32, (tm, 1), 0)
        for g, w in enumerate(POOL_WINDOWS):
            cs = slice(g * POOL_GROUP, (g + 1) * POOL_GROUP)
            pooled = _pooled_group(u_ref, halo, g, w, t_idx).astype(BF16)
            wg = wg_ref[g]
            mixed = jnp.dot(pooled, wg, preferred_element_type=F32)
            z = z_ref[:, cs]
            sg = _sigmoid(z)
            dyv = dy_ref[:, cs]
            sc = sc_ref[:, cs]
            dms = dyv * (z * sg)
            dz = dyv * (mixed * sc) * (sg * (1.0 + z * (1.0 - sg)))
            dsc = jnp.sum(dms * mixed, axis=0, keepdims=True)
            dmixed = (dms * sc).astype(BF16)
            dpooled = lax.dot_general(dmixed, wg, (((1,), (1,)), ((), ())), preferred_element_type=F32)
            pooled_ref[:, cs] = pooled
            dmixed_ref[:, cs] = dmixed
            dpooled_ref[:, cs] = dpooled
            dz_ref[:, cs] = dz.astype(BF16)

            @pl.when(i == 0)
            def _():
                dsc_ref[:, cs] = dsc

            @pl.when(i > 0)
            def _():
                dsc_ref[:, cs] += dsc

    row = pl.BlockSpec((tm, E), lambda i: (i, 0))
    vec = pl.BlockSpec((1, E), lambda i: (0, 0))
    return pl.pallas_call(
        body, name=name,
        out_shape=(jax.ShapeDtypeStruct((S, E), BF16), jax.ShapeDtypeStruct((S, E), BF16),
                   jax.ShapeDtypeStruct((S, E), F32), jax.ShapeDtypeStruct((S, E), BF16),
                   jax.ShapeDtypeStruct((1, E), F32)),
        grid=(S // tm,),
        in_specs=[row, _prev_halo_spec(tm, E, 0), pl.BlockSpec((tm, E), lambda i: (i, 1)), row,
                  pl.BlockSpec((len(POOL_WINDOWS), POOL_GROUP, POOL_GROUP), lambda i: (0, 0, 0)), vec],
        out_specs=(row, row, row, row, vec),
        compiler_params=_cparams("arbitrary"),
    )(h, h, h, dy, w_grp, scale)


def _pool_bwd2(dpooled, dz, *, tm, name):
    S = dpooled.shape[0]
    E = D_INNER
    nt = S // tm

    def body(dp_ref, dpn_ref, dz_ref, dh_ref):
        i = pl.program_id(0)
        nxt = jnp.where(i < nt - 1, dpn_ref[...], 0.0)
        t_ext = i * tm + lax.broadcasted_iota(jnp.int32, (tm + HALO, 1), 0)
        for g, w in enumerate(POOL_WINDOWS):
            cs = slice(g * POOL_GROUP, (g + 1) * POOL_GROUP)
            dp = dp_ref[:, cs]
            inv = 1.0 / jnp.minimum(t_ext + 1, w).astype(F32)
            s = jnp.concatenate([dp, nxt[:, cs]], axis=0) * inv
            n = tm + HALO
            k = 1
            while k < w:
                s = s + pltpu.roll(s, n - k, 0)
                k *= 2
            dh_ref[:, cs] = (s[:tm, :] - dp).astype(BF16)
        dh_ref[:, E:] = dz_ref[...]

    return pl.pallas_call(
        body, name=name,
        out_shape=jax.ShapeDtypeStruct((S, 2 * E), BF16),
        grid=(nt,),
        in_specs=[pl.BlockSpec((tm, E), lambda i: (i, 0)), _next_halo_spec(tm, E, 0, S),
                  pl.BlockSpec((tm, E), lambda i: (i, 0))],
        out_specs=pl.BlockSpec((tm, 2 * E), lambda i: (i, 0)),
        compiler_params=_cparams("parallel"),
    )(dpooled, dpooled, dz)


CONV_CHUNK = 512


def _conv_fwd(h, cw, *, tm, name):
    S = h.shape[0]
    E = D_INNER

    def body(b_ref, c_ref, hh_ref, z_ref, ch_ref, hhh_ref, w_ref, y_ref):
        i = pl.program_id(0)
        for j in range(E // CONV_CHUNK):
            cs = slice(j * CONV_CHUNK, (j + 1) * CONV_CHUNK)
            p = c_ref[:, cs] * hh_ref[:, cs]
            ph = jnp.where(i > 0, ch_ref[:, cs] * hhh_ref[:, cs], 0.0)
            ext = jnp.concatenate([ph, p], axis=0)
            conv = w_ref[2:3, cs] * p + w_ref[1:2, cs] * _shift_down(ext, 1) + w_ref[0:1, cs] * _shift_down(ext, 2)
            z = z_ref[:, cs]
            y_ref[:, cs] = (b_ref[:, cs] * conv * (z * _sigmoid(z))).astype(BF16)

    col = lambda c: pl.BlockSpec((tm, E), lambda i: (i, c))
    return pl.pallas_call(
        body, name=name,
        out_shape=jax.ShapeDtypeStruct((S, E), BF16),
        grid=(S // tm,),
        in_specs=[col(0), col(1), col(2), col(3), _prev_halo_spec(tm, E, 1), _prev_halo_spec(tm, E, 2),
                  pl.BlockSpec((8, E), lambda i: (0, 0))],
        out_specs=pl.BlockSpec((tm, E), lambda i: (i, 0)),
        compiler_params=_cparams("parallel"),
    )(h, h, h, h, h, h, cw)


def _conv_bwd(h, dy, cw, *, tm, name):
    S = h.shape[0]
    E = D_INNER
    nt = S // tm

    def body(b_ref, c_ref, hh_ref, z_ref, dy_ref, ch_ref, hhh_ref, bn_ref, zn_ref, dyn_ref, w_ref, dh_ref, dw_ref):
        i = pl.program_id(0)
        for j in range(E // CONV_CHUNK):
            cs = slice(j * CONV_CHUNK, (j + 1) * CONV_CHUNK)
            w0, w1, w2 = w_ref[0:1, cs], w_ref[1:2, cs], w_ref[2:3, cs]
            c, hh, b, z, dyv = c_ref[:, cs], hh_ref[:, cs], b_ref[:, cs], z_ref[:, cs], dy_ref[:, cs]
            p = c * hh
            ph = jnp.where(i > 0, ch_ref[:, cs] * hhh_ref[:, cs], 0.0)
            ext = jnp.concatenate([ph, p], axis=0)
            pm1 = _shift_down(ext, 1)
            pm2 = _shift_down(ext, 2)
            conv = w2 * p + w1 * pm1 + w0 * pm2
            sg = _sigmoid(z)
            dy0 = dyv * (z * sg)
            dz = dyv * (b * conv) * (sg * (1.0 + z * (1.0 - sg)))
            db = dy0 * conv
            dconv = dy0 * b
            zn = zn_ref[:, cs]
            dconv_n = jnp.where(i < nt - 1, dyn_ref[:, cs] * (zn * _sigmoid(zn)) * bn_ref[:, cs], 0.0)
            dext = jnp.concatenate([dconv, dconv_n], axis=0)
            dp = w2 * dconv + w1 * _shift_up(dext, 1, tm) + w0 * _shift_up(dext, 2, tm)
            dh_ref[:, 0 * E + j * CONV_CHUNK:0 * E + (j + 1) * CONV_CHUNK] = db.astype(BF16)
            dh_ref[:, 1 * E + j * CONV_CHUNK:1 * E + (j + 1) * CONV_CHUNK] = (dp * hh).astype(BF16)
            dh_ref[:, 2 * E + j * CONV_CHUNK:2 * E + (j + 1) * CONV_CHUNK] = (dp * c).astype(BF16)
            dh_ref[:, 3 * E + j * CONV_CHUNK:3 * E + (j + 1) * CONV_CHUNK] = dz.astype(BF16)
            dw = jnp.concatenate([jnp.sum(dconv * pm2, axis=0, keepdims=True),
                                  jnp.sum(dconv * pm1, axis=0, keepdims=True),
                                  jnp.sum(dconv * p, axis=0, keepdims=True),
                                  jnp.zeros((5, CONV_CHUNK), F32)], axis=0)

            @pl.when(i == 0)
            def _():
                dw_ref[:, cs] = dw

            @pl.when(i > 0)
            def _():
                dw_ref[:, cs] += dw

    col = lambda c: pl.BlockSpec((tm, E), lambda i: (i, c))
    return pl.pallas_call(
        body, name=name,
        out_shape=(jax.ShapeDtypeStruct((S, 4 * E), BF16), jax.ShapeDtypeStruct((8, E), F32)),
        grid=(nt,),
        in_specs=[col(0), col(1), col(2), col(3), pl.BlockSpec((tm, E), lambda i: (i, 0)),
                  _prev_halo_spec(tm, E, 1), _prev_halo_spec(tm, E, 2),
                  _next_halo_spec(tm, E, 0, S), _next_halo_spec(tm, E, 3, S), _next_halo_spec(tm, E, 0, S),
                  pl.BlockSpec((8, E), lambda i: (0, 0))],
        out_specs=(pl.BlockSpec((tm, 4 * E), lambda i: (i, 0)), pl.BlockSpec((8, E), lambda i: (0, 0))),
        compiler_params=_cparams("arbitrary"),
    )(h, h, h, h, dy, h, h, h, h, dy, cw)


Z_COLS = D_INNER // LANES
KV_LAT_BLK = D_INNER // KV_LORA
Q_LAT_BLK = (D_INNER + KV_LORA) // Q_LORA
K_ROPE_BLK = (D_INNER + KV_LORA + Q_LORA) // LANES


def _rope(blk, c, s1, s2):
    return blk * c + pltpu.roll(blk, LANES - QK_ROPE // 2, 1) * s1 + pltpu.roll(blk, QK_ROPE // 2, 1) * s2


def _unrope(blk, c, s1, s2):
    return blk * c - pltpu.roll(blk, LANES - QK_ROPE // 2, 1) * s1 - pltpu.roll(blk, QK_ROPE // 2, 1) * s2


def _lat_norm(v, g):
    rstd = lax.rsqrt(jnp.mean(v * v, axis=-1, keepdims=True) + NORM_EPS)
    return v * rstd * g


def _mla_latent_fwd(h, q_norm, kv_norm, tabs, *, tm, name):
    S = h.shape[0]

    def body(kv_ref, q_ref, kr_ref, qg_ref, kvg_ref, c_ref, s1_ref, s2_ref, qn_ref, kvn_ref, krr_ref):
        qn_ref[...] = _lat_norm(q_ref[...], qg_ref[...]).astype(BF16)
        kvn_ref[...] = _lat_norm(kv_ref[...], kvg_ref[...]).astype(BF16)
        krr_ref[...] = _rope(kr_ref[...], c_ref[...], s1_ref[...], s2_ref[...]).astype(BF16)

    tab = pl.BlockSpec((tm, LANES), lambda i: (i, 0))
    return pl.pallas_call(
        body, name=name,
        out_shape=(jax.ShapeDtypeStruct((S, Q_LORA), BF16), jax.ShapeDtypeStruct((S, KV_LORA), BF16),
                   jax.ShapeDtypeStruct((S, LANES), BF16)),
        grid=(S // tm,),
        in_specs=[pl.BlockSpec((tm, KV_LORA), lambda i: (i, KV_LAT_BLK)), pl.BlockSpec((tm, Q_LORA), lambda i: (i, Q_LAT_BLK)),
                  pl.BlockSpec((tm, LANES), lambda i: (i, K_ROPE_BLK)),
                  pl.BlockSpec((1, Q_LORA), lambda i: (0, 0)), pl.BlockSpec((1, KV_LORA), lambda i: (0, 0)), tab, tab, tab],
        out_specs=(pl.BlockSpec((tm, Q_LORA), lambda i: (i, 0)), pl.BlockSpec((tm, KV_LORA), lambda i: (i, 0)), tab),
        compiler_params=_cparams("parallel"),
    )(h, h, h, q_norm, kv_norm, *tabs)


def _mla_q_up(q_n, w_q_pad, tabs, *, tm, name):
    S = q_n.shape[0]

    def body(a_ref, w_ref, c_ref, s1_ref, s2_ref, o_ref):
        a = a_ref[...]
        for hd in range(N_HEADS):
            acc = jnp.dot(a, w_ref[:, hd * HEAD_PAD:(hd + 1) * HEAD_PAD], preferred_element_type=F32)
            o_ref[hd, :, :QK_NOPE] = acc[:, :QK_NOPE].astype(BF16)
            o_ref[hd, :, QK_NOPE:] = _rope(acc[:, QK_NOPE:], c_ref[...], s1_ref[...], s2_ref[...]).astype(BF16)

    tab = pl.BlockSpec((tm, LANES), lambda i: (i, 0))
    return pl.pallas_call(
        body, name=name,
        out_shape=jax.ShapeDtypeStruct((N_HEADS, S, HEAD_PAD), BF16),
        grid=(S // tm,),
        in_specs=[pl.BlockSpec((tm, Q_LORA), lambda i: (i, 0)), pl.BlockSpec((Q_LORA, N_HEADS * HEAD_PAD), lambda i: (0, 0)),
                  tab, tab, tab],
        out_specs=pl.BlockSpec((N_HEADS, tm, HEAD_PAD), lambda i: (0, i, 0)),
        compiler_params=_cparams("parallel"),
    )(q_n, w_q_pad, *tabs)


def _mla_kv_up(kv_n, w_kv, krr, *, tm, name):
    S = kv_n.shape[0]
    heads_per_chip = N_HEADS // N_CHIPS

    def body(a_ref, w_ref, krr_ref, k_ref, v_ref):
        a = a_ref[...]
        ones = jnp.ones((tm, V_DIM), BF16)
        for hd in range(N_HEADS):
            lo = (hd % heads_per_chip) * HEAD_PAD
            acc = jnp.dot(a, w_ref[hd // heads_per_chip, :, lo:lo + HEAD_PAD], preferred_element_type=F32)
            k_ref[hd, :, :QK_NOPE] = acc[:, :QK_NOPE].astype(BF16)
            k_ref[hd, :, QK_NOPE:] = krr_ref[...]
            v_ref[hd, :, :V_DIM] = acc[:, QK_NOPE:].astype(BF16)
            v_ref[hd, :, V_DIM:] = ones

    head_blk = pl.BlockSpec((N_HEADS, tm, HEAD_PAD), lambda i: (0, i, 0))
    sds = jax.ShapeDtypeStruct((N_HEADS, S, HEAD_PAD), BF16)
    return pl.pallas_call(
        body, name=name, out_shape=(sds, sds),
        grid=(S // tm,),
        in_specs=[pl.BlockSpec((tm, KV_LORA), lambda i: (i, 0)),
                  pl.BlockSpec((N_CHIPS, KV_LORA, PACK_COLS), lambda i: (0, w_kv.r0 // KV_LORA, 0)),
                  pl.BlockSpec((tm, LANES), lambda i: (i, 0))],
        out_specs=(head_blk, head_blk),
        compiler_params=_cparams("parallel"),
    )(kv_n, w_kv.buf, krr)


LOG2E = 1.4426950408889634
SCORE_TO_LOG2 = ATTN_SCALE * LOG2E


def _flash_fwd(q_full, k_full, v_aug, h, *, tq, name):
    H, S, _ = q_full.shape
    tk = tq
    HP = 2

    def body(q_ref, k_ref, v_ref, z_ref, o_ref, y_ref, lse_ref, m_sc, acc_sc):
        qi = pl.program_id(1)
        m_sc[...] = jnp.full(m_sc.shape, -1e30, F32)
        acc_sc[...] = jnp.zeros(acc_sc.shape, F32)

        def chunk(j, diag):
            off = pl.multiple_of(j * tk, tk)
            for hh in range(HP):
                kj = k_ref[hh, pl.ds(off, tk), :]
                vj = v_ref[hh, pl.ds(off, tk), :]
                s = lax.dot_general(q_ref[hh], kj, (((1,), (1,)), ((), ())), preferred_element_type=F32) * SCORE_TO_LOG2
                if diag:
                    keep = lax.broadcasted_iota(jnp.int32, (tq, tk), 1) <= lax.broadcasted_iota(jnp.int32, (tq, tk), 0)
                    s = jnp.where(keep, s, -1e30)
                m_old = m_sc[hh]
                m_new = jnp.maximum(m_old, jnp.max(s, axis=1, keepdims=True))
                p = jnp.exp2(s - jnp.tile(m_new, (1, tk // LANES)))
                alpha = jnp.exp2(m_old - m_new)
                acc_sc[hh] = jnp.tile(alpha, (1, 2)) * acc_sc[hh] + jnp.dot(p.astype(BF16), vj, preferred_element_type=F32)
                m_sc[hh] = m_new

        def step(j, carry):
            chunk(j, False)
            return carry

        lax.fori_loop(0, qi, step, 0)
        chunk(qi, True)
        for hh in range(HP):
            cs = slice(hh * V_DIM, (hh + 1) * V_DIM)
            l = acc_sc[hh, :, V_DIM:]
            o = acc_sc[hh, :, :V_DIM] / l
            z = z_ref[:, cs]
            o_ref[:, cs] = o
            y_ref[:, cs] = (o * (z * _sigmoid(z))).astype(BF16)
            lse_ref[hh, 0] = (m_sc[hh] + jnp.log2(l)).T[0:1, :]

    pair = pl.BlockSpec((tq, HP * V_DIM), lambda hd, i: (i, hd))
    return pl.pallas_call(
        body, name=name,
        out_shape=(jax.ShapeDtypeStruct((S, D_INNER), F32), jax.ShapeDtypeStruct((S, D_INNER), BF16),
                   jax.ShapeDtypeStruct((H, S // tq, 1, tq), F32)),
        grid=(H // HP, S // tq),
        in_specs=[pl.BlockSpec((HP, tq, HEAD_PAD), lambda hd, i: (hd, i, 0)),
                  pl.BlockSpec((HP, S, HEAD_PAD), lambda hd, i: (hd, 0, 0)),
                  pl.BlockSpec((HP, S, HEAD_PAD), lambda hd, i: (hd, 0, 0)),
                  pair],
        out_specs=(pair, pair, pl.BlockSpec((HP, 1, 1, tq), lambda hd, i: (hd, i, 0, 0))),
        scratch_shapes=[pltpu.VMEM((HP, tq, LANES), F32), pltpu.VMEM((HP, tq, HEAD_PAD), F32)],
        compiler_params=_cparams("parallel", "parallel"),
    )(q_full, k_full, v_aug, h)


def _mla_gate_bwd(dy, o, h, *, tm, name):
    S = dy.shape[0]
    E = D_INNER

    def body(dy_ref, o_ref, z_ref, do_ref, dz_ref, delta_ref):
        for hd in range(N_HEADS):
            cs = slice(hd * V_DIM, (hd + 1) * V_DIM)
            z = z_ref[:, cs]
            sg = _sigmoid(z)
            dyv = dy_ref[:, cs]
            ov = o_ref[:, cs]
            do = dyv * (z * sg)
            do_ref[:, cs] = do.astype(BF16)
            dz_ref[:, cs] = (dyv * ov * (sg * (1.0 + z * (1.0 - sg)))).astype(BF16)
            delta_ref[hd, 0] = jnp.broadcast_to(jnp.sum(do * ov, axis=-1, keepdims=True), (tm, LANES)).T[0:1, :]

    row = pl.BlockSpec((tm, E), lambda i: (i, 0))
    return pl.pallas_call(
        body, name=name,
        out_shape=(jax.ShapeDtypeStruct((S, E), BF16), jax.ShapeDtypeStruct((S, E), BF16),
                   jax.ShapeDtypeStruct((N_HEADS, S // tm, 1, tm), F32)),
        grid=(S // tm,),
        in_specs=[row, row, row],
        out_specs=(row, row, pl.BlockSpec((N_HEADS, 1, 1, tm), lambda i: (0, i, 0, 0))),
        compiler_params=_cparams("parallel"),
    )(dy, o, h)


def _flash_bwd(q_full, k_full, v_aug, do, lse_rows, delta_rows, tabs, *, tq, name):
    H, S, _ = q_full.shape
    tk = tq
    nq = S // tq
    HP = 2

    def body(q_ref, k_ref, v_ref, do_ref, lse_ref, dl_ref, c_ref, s1_ref, s2_ref, dqp_ref, dkv_ref, dkr_ref, dq_ref, dk_sc, dv_sc):
        kj = pl.program_id(1)

        @pl.when(kj == 0)
        def _():
            dq_ref[...] = jnp.zeros(dq_ref.shape, F32)

        dk_sc[...] = jnp.zeros(dk_sc.shape, F32)
        dv_sc[...] = jnp.zeros(dv_sc.shape, F32)

        def chunk(qi, diag):
            off = pl.multiple_of(qi * tq, tq)
            for hh in range(HP):
                k = k_ref[hh]
                q = q_ref[hh, pl.ds(off, tq), :]
                dov = do_ref[pl.ds(off, tq), hh * V_DIM:(hh + 1) * V_DIM]
                s_t = lax.dot_general(k, q, (((1,), (1,)), ((), ())), preferred_element_type=F32) * SCORE_TO_LOG2
                p_t = jnp.exp2(s_t - lse_ref[hh, qi])
                if diag:
                    keep = lax.broadcasted_iota(jnp.int32, (tk, tq), 0) <= lax.broadcasted_iota(jnp.int32, (tk, tq), 1)
                    p_t = jnp.where(keep, p_t, 0.0)
                dv_sc[hh] += jnp.dot(p_t.astype(BF16), dov, preferred_element_type=F32)
                dp_t = lax.dot_general(v_ref[hh], dov, (((1,), (1,)), ((), ())), preferred_element_type=F32)
                ds = (p_t * (dp_t - dl_ref[hh, qi])).astype(BF16)
                dk_sc[hh] += jnp.dot(ds, q, preferred_element_type=F32)
                dq_ref[pl.ds(off, tq), hh * HEAD_PAD:(hh + 1) * HEAD_PAD] += lax.dot_general(
                    ds, k, (((0,), (0,)), ((), ())), preferred_element_type=F32)

        def step(qi, carry):
            chunk(qi, False)
            return carry

        chunk(kj, True)
        done = pl.ds(pl.multiple_of(kj * tq, tq), tq)
        for hh in range(HP):
            lo = hh * HEAD_PAD
            dqp_ref[:, lo:lo + QK_NOPE] = (dq_ref[done, lo:lo + QK_NOPE] * ATTN_SCALE).astype(BF16)
            dqp_ref[:, lo + QK_NOPE:lo + HEAD_PAD] = _unrope(dq_ref[done, lo + QK_NOPE:lo + HEAD_PAD] * ATTN_SCALE,
                                                             c_ref[...], s1_ref[...], s2_ref[...]).astype(BF16)
        lax.fori_loop(kj + 1, nq, step, 0)
        for hh in range(HP):
            lo = hh * HEAD_PAD
            dkv_ref[:, lo:lo + QK_NOPE] = (dk_sc[hh, :, :QK_NOPE] * ATTN_SCALE).astype(BF16)
            dkv_ref[:, lo + QK_NOPE:lo + HEAD_PAD] = dv_sc[hh].astype(BF16)
            dkr_ref[hh] = dk_sc[hh, :, QK_NOPE:] * ATTN_SCALE

    tab = pl.BlockSpec((tk, LANES), lambda hd, j: (j, 0))
    pair_rows = pl.BlockSpec((tk, HP * HEAD_PAD), lambda hd, j: (j, hd))
    return pl.pallas_call(
        body, name=name,
        out_shape=(jax.ShapeDtypeStruct((S, H * HEAD_PAD), BF16), jax.ShapeDtypeStruct((S, H * HEAD_PAD), BF16),
                   jax.ShapeDtypeStruct((H, S, LANES), F32)),
        grid=(H // HP, S // tk),
        in_specs=[pl.BlockSpec((HP, S, HEAD_PAD), lambda hd, j: (hd, 0, 0)),
                  pl.BlockSpec((HP, tk, HEAD_PAD), lambda hd, j: (hd, j, 0)),
                  pl.BlockSpec((HP, tk, V_DIM), lambda hd, j: (hd, j, 0)),
                  pl.BlockSpec((S, HP * V_DIM), lambda hd, j: (0, hd)),
                  pl.BlockSpec((HP, nq, 1, tq), lambda hd, j: (hd, 0, 0, 0)),
                  pl.BlockSpec((HP, nq, 1, tq), lambda hd, j: (hd, 0, 0, 0)), tab, tab, tab],
        out_specs=(pair_rows, pair_rows, pl.BlockSpec((HP, tk, LANES), lambda hd, j: (hd, j, 0))),
        scratch_shapes=[pltpu.VMEM((S, HP * HEAD_PAD), F32), pltpu.VMEM((HP, tk, HEAD_PAD), F32), pltpu.VMEM((HP, tk, V_DIM), F32)],
        compiler_params=_cparams("parallel", "arbitrary"),
    )(q_full, k_full, v_aug, do, lse_rows, delta_rows, *tabs)


def _mla_latent_bwd(h, dq_n, dkv_n, dkr, dz, q_norm, kv_norm, tabs, *, tm, name):
    S = h.shape[0]

    def body(kv_ref, q_ref, dqn_ref, dkvn_ref, dkr_ref, dz_ref, qg_ref, kvg_ref, c_ref, s1_ref, s2_ref, dh_ref, dqg_ref, dkvg_ref):
        i = pl.program_id(0)
        dq_lat, dqg = _rms_bwd_math(q_ref[...], qg_ref[...], dqn_ref[...])
        dkv_lat, dkvg = _rms_bwd_math(kv_ref[...], kvg_ref[...], dkvn_ref[...])
        dkr_sum = dkr_ref[0]
        for hd in range(1, N_HEADS):
            dkr_sum = dkr_sum + dkr_ref[hd]
        dh_ref[:, :D_INNER] = dz_ref[...]
        dh_ref[:, D_INNER:D_INNER + KV_LORA] = dkv_lat.astype(BF16)
        dh_ref[:, D_INNER + KV_LORA:D_INNER + KV_LORA + Q_LORA] = dq_lat.astype(BF16)
        dh_ref[:, D_INNER + KV_LORA + Q_LORA:] = _unrope(dkr_sum, c_ref[...], s1_ref[...], s2_ref[...]).astype(BF16)

        @pl.when(i == 0)
        def _():
            dqg_ref[...] = dqg
            dkvg_ref[...] = dkvg

        @pl.when(i > 0)
        def _():
            dqg_ref[...] += dqg
            dkvg_ref[...] += dkvg

    tab = pl.BlockSpec((tm, LANES), lambda i: (i, 0))
    qvec = pl.BlockSpec((1, Q_LORA), lambda i: (0, 0))
    kvvec = pl.BlockSpec((1, KV_LORA), lambda i: (0, 0))
    return pl.pallas_call(
        body, name=name,
        out_shape=(jax.ShapeDtypeStruct((S, MLA_IN_PAD), BF16), jax.ShapeDtypeStruct((1, Q_LORA), F32),
                   jax.ShapeDtypeStruct((1, KV_LORA), F32)),
        grid=(S // tm,),
        in_specs=[pl.BlockSpec((tm, KV_LORA), lambda i: (i, KV_LAT_BLK)), pl.BlockSpec((tm, Q_LORA), lambda i: (i, Q_LAT_BLK)),
                  pl.BlockSpec((tm, Q_LORA), lambda i: (i, 0)), pl.BlockSpec((tm, KV_LORA), lambda i: (i, 0)),
                  pl.BlockSpec((N_HEADS, tm, LANES), lambda i: (0, i, 0)), pl.BlockSpec((tm, D_INNER), lambda i: (i, 0)),
                  qvec, kvvec, tab, tab, tab],
        out_specs=(pl.BlockSpec((tm, MLA_IN_PAD), lambda i: (i, 0)), qvec, kvvec),
        compiler_params=_cparams("arbitrary"),
    )(h, h, dq_n, dkv_n, dkr, dz, q_norm, kv_norm, *tabs)


def _adamw(w, g, m, v, *, name):
    R, C = w.shape
    tr = R
    for cand in (512, 256, 128, 64, 32, 16, 8):
        if R % cand == 0 and cand * C * 4 <= 2 * 1024 * 1024:
            tr = cand
            break

    def body(w_ref, g_ref, m_ref, v_ref, d_ref, nm_ref, nv_ref):
        d_ref[...], nm_ref[...], nv_ref[...] = _adam_math(w_ref[...], g_ref[...], m_ref[...], v_ref[...])

    spec = pl.BlockSpec((tr, C), lambda i: (i, 0))
    sds = jax.ShapeDtypeStruct((R, C), F32)
    return pl.pallas_call(
        body, name=name, out_shape=(sds, sds, sds), grid=(R // tr,),
        in_specs=[spec] * 4, out_specs=(spec,) * 3,
        compiler_params=_cparams("parallel"),
    )(w, g, m, v)


def _adamw_rows(w, m, v, srcs, *, name):
    nj, R, C = w.shape
    assert len(srcs) == nj and C % PACK_COLS == 0
    tr = min(R, 256)
    assert R % tr == 0 and all(r0 % tr == 0 for _, r0 in srcs)

    def body(*refs):
        w_ref, m_ref, v_ref = refs[:3]
        g_refs = refs[3:3 + nj]
        go_ref, d_ref, nm_ref, nv_ref = refs[3 + nj:]
        gv = g_refs[0][...]
        for jj in range(1, nj):
            gv = jnp.where(pl.program_id(0) == jj, g_refs[jj][...], gv)
        d, m_new, v_new = _adam_math(w_ref[...], gv, m_ref[...], v_ref[...])
        go_ref[...] = gv
        d_ref[...] = d
        nm_ref[...] = m_new
        nv_ref[...] = v_new

    nat = pl.BlockSpec((None, tr, PACK_COLS), lambda j, cb, i: (j, i, cb))

    def src_spec(jj, r0):
        return pl.BlockSpec((tr, PACK_COLS), lambda j, cb, i: (jnp.where(j == jj, (r0 + cb * R) // tr + i, r0 // tr), 0))

    sds = jax.ShapeDtypeStruct((nj, R, C), F32)
    return pl.pallas_call(
        body, name=name, out_shape=(sds,) * 4, grid=(nj, C // PACK_COLS, R // tr),
        in_specs=[nat] * 3 + [src_spec(jj, r0) for jj, (_, r0) in enumerate(srcs)], out_specs=(nat,) * 4,
        compiler_params=_cparams("parallel", "parallel", "parallel"),
    )(w, m, v, *[rows for rows, _ in srcs])


HBM_SPEC = pl.BlockSpec(memory_space=pltpu.HBM)
VMEM_SPEC = pl.BlockSpec(memory_space=pltpu.VMEM)
SEM_SPEC = pl.BlockSpec(memory_space=pltpu.SEMAPHORE)
ANY_SPEC = pl.BlockSpec(memory_space=pl.ANY)
SPLIT_EFFECT = pltpu.SideEffectType.DATAFLOW_SIDE_EFFECTING


def _place():
    return lax.axis_index("x"), lax.axis_index("y"), lax.axis_index("c")


def _other_chips(x, y):
    return [(1 - x, y), (x, 1 - y), (1 - x, 1 - y)]


def _remote(src, dst, send_sem, recv_sem, dev):
    return pltpu.make_async_remote_copy(src_ref=src, dst_ref=dst, send_sem=send_sem, recv_sem=recv_sem,
                                        device_id=dev, device_id_type=MESH)


def _all_gather_big(wp, *, name):
    R, C = wp.shape
    H = R // 2

    def body(w_ref, out_ref, send_sems, recv_sems):
        x, y, c = _place()
        sib = (x, y, 1 - c)
        chips = _other_chips(x, y)

        def blk(px, py, half):
            return out_ref.at[2 * px + py, pl.ds(pl.multiple_of(half * H, 16), H), :]

        my_half = w_ref.at[pl.ds(pl.multiple_of(c * H, 16), H), :]
        first = [_remote(my_half, blk(x, y, c), send_sems.at[r], recv_sems.at[r], (*chip, c)) for r, chip in enumerate(chips)]
        own = _remote(w_ref, out_ref.at[2 * x + y], send_sems.at[6], recv_sems.at[6], sib)
        for cp in first + [own]:
            cp.start()
        passed = [_remote(blk(*chip, c), blk(*chip, c), send_sems.at[3 + r], recv_sems.at[3 + r], sib) for r, chip in enumerate(chips)]
        for r, chip in enumerate(chips):
            _remote(my_half, blk(*chip, c), send_sems.at[r], recv_sems.at[r], (*chip, c)).wait_recv()
            passed[r].start()
        for r, chip in enumerate(chips):
            _remote(my_half, blk(*chip, 1 - c), send_sems.at[3 + r], recv_sems.at[3 + r], sib).wait_recv()
        own.wait_recv()
        for cp in first + passed + [own]:
            cp.wait_send()

    return pl.pallas_call(
        body, name=name,
        out_shape=jax.ShapeDtypeStruct((N_CHIPS, R, C), wp.dtype),
        in_specs=[HBM_SPEC], out_specs=HBM_SPEC,
        scratch_shapes=[pltpu.SemaphoreType.DMA((7,)), pltpu.SemaphoreType.DMA((7,))],
    )(wp)


def _ag_ici_start(wp, after, *, name):
    R, C = wp.shape
    H = R // 2

    def body(w_ref, land_ref, after_ref, send_sems, recv_sems, w_thru, land_thru, token):
        x, y, c = _place()
        rows = pl.ds(pl.multiple_of(c * H, 16), H)
        for r, chip in enumerate(_other_chips(x, y)):
            _remote(w_ref.at[rows, :], land_ref.at[2 * x + y, rows, :], send_sems.at[r], recv_sems.at[r], (*chip, c)).start()
        token[...] = jnp.zeros(token.shape, F32)

    land = lax.empty((N_CHIPS, R, C), wp.dtype)
    return pl.pallas_call(
        body, name=name,
        out_shape=(pltpu.SemaphoreType.DMA((3,)), pltpu.SemaphoreType.DMA((3,)), pltpu.HBM(wp.shape, wp.dtype), pltpu.HBM(land.shape, land.dtype),
                   jax.ShapeDtypeStruct((8, LANES), F32)),
        in_specs=(HBM_SPEC, HBM_SPEC, ANY_SPEC), out_specs=(SEM_SPEC, SEM_SPEC, HBM_SPEC, HBM_SPEC, VMEM_SPEC),
        input_output_aliases={0: 2, 1: 3},
        compiler_params=pltpu.CompilerParams(has_side_effects=SPLIT_EFFECT),
    )(pltpu.with_memory_space_constraint(wp, pltpu.HBM), pltpu.with_memory_space_constraint(land, pltpu.HBM), after)


def _ag_ici_wait(send_sems, recv_sems, w_thru, land_thru, after, *, name):
    R, C = w_thru.shape
    H = R // 2

    def body(w_ref, land_ref, send_sems, recv_sems, after_ref, w_out, land_out):
        x, y, c = _place()
        rows = pl.ds(pl.multiple_of(c * H, 16), H)
        for r, (px, py) in enumerate(_other_chips(x, y)):
            cp = _remote(w_ref.at[rows, :], land_ref.at[2 * px + py, rows, :], send_sems.at[r], recv_sems.at[r], (px, py, c))
            cp.wait_send()
            cp.wait_recv()

    return pl.pallas_call(
        body, name=name,
        out_shape=(pltpu.HBM(w_thru.shape, w_thru.dtype), pltpu.HBM(land_thru.shape, land_thru.dtype)),
        in_specs=(HBM_SPEC, HBM_SPEC, SEM_SPEC, SEM_SPEC, ANY_SPEC), out_specs=(HBM_SPEC, HBM_SPEC),
        input_output_aliases={0: 0, 1: 1},
        compiler_params=pltpu.CompilerParams(has_side_effects=SPLIT_EFFECT),
    )(w_thru, land_thru, send_sems, recv_sems, after)


def _ag_sibling_forward(land, wp, *, name):
    _, R, C = land.shape
    H = R // 2

    def body(land_ref, w_ref, out_ref, send_sems, recv_sems):
        x, y, c = _place()
        sib = (x, y, 1 - c)
        mine = pl.ds(pl.multiple_of(c * H, 16), H)
        theirs = pl.ds(pl.multiple_of((1 - c) * H, 16), H)
        chips = _other_chips(x, y)
        sends = [_remote(land_ref.at[2 * px + py, mine, :], out_ref.at[2 * px + py, mine, :], send_sems.at[r], recv_sems.at[r], sib)
                 for r, (px, py) in enumerate(chips)]
        sends.append(_remote(w_ref, out_ref.at[2 * x + y], send_sems.at[3], recv_sems.at[3], sib))
        for cp in sends:
            cp.start()
        for r, (px, py) in enumerate(chips):
            _remote(land_ref.at[2 * px + py, mine, :], out_ref.at[2 * px + py, theirs, :], send_sems.at[r], recv_sems.at[r], sib).wait_recv()
        sends[3].wait_recv()
        for cp in sends:
            cp.wait_send()

    return pl.pallas_call(
        body, name=name, out_shape=jax.ShapeDtypeStruct(land.shape, land.dtype),
        in_specs=[HBM_SPEC, HBM_SPEC], out_specs=HBM_SPEC, input_output_aliases={0: 0},
        scratch_shapes=[pltpu.SemaphoreType.DMA((4,)), pltpu.SemaphoreType.DMA((4,))],
    )(land, wp)


def _rs_sibling_swap(g, *, name):
    _, R, C = g.shape
    H = R // 2

    def body(g_ref, theirs_ref, send_sems, recv_sems):
        x, y, c = _place()
        sib = (x, y, 1 - c)
        copies = [_remote(g_ref.at[k, pl.ds(pl.multiple_of((1 - c) * H, 16), H), :], theirs_ref.at[k],
                          send_sems.at[k], recv_sems.at[k], sib) for k in range(N_CHIPS)]
        for cp in copies:
            cp.start()
        for cp in copies:
            cp.wait()

    return pl.pallas_call(
        body, name=name, out_shape=jax.ShapeDtypeStruct((N_CHIPS, H, C), g.dtype),
        in_specs=[HBM_SPEC], out_specs=HBM_SPEC,
        scratch_shapes=[pltpu.SemaphoreType.DMA((N_CHIPS,)), pltpu.SemaphoreType.DMA((N_CHIPS,))],
    )(g)


def _row_tile(h):
    best = 16
    for d in range(16, 1025, 16):
        if h % d == 0:
            best = d
    return best


def _add2_bf16(g, theirs, core, *, name):
    K, H, C = theirs.shape
    tr = _row_tile(H)
    nb = H // tr

    def body(c_ref, a_ref, b_ref, o_ref):
        o_ref[...] = (a_ref[...].astype(F32) + b_ref[...].astype(F32)).astype(o_ref.dtype)

    spec = pl.BlockSpec((None, tr, C), lambda k, i, c: (k, i, 0))
    return pl.pallas_call(
        body, name=name, out_shape=jax.ShapeDtypeStruct((K, H, C), theirs.dtype),
        grid_spec=pltpu.PrefetchScalarGridSpec(
            num_scalar_prefetch=1, grid=(K, nb),
            in_specs=[pl.BlockSpec((None, tr, C), lambda k, i, c: (k, c[0] * nb + i, 0)), spec], out_specs=spec),
        compiler_params=_cparams("parallel", "parallel"),
    )(core, g, theirs)


def _rs_chip_exchange_start(p, *, name):
    _, H, C = p.shape

    def body(p_ref, land_ref, send_sems, recv_sems, p_thru, land_thru, token):
        x, y, c = _place()
        for r, (px, py) in enumerate(_other_chips(x, y)):
            _remote(p_ref.at[2 * px + py], land_ref.at[r], send_sems.at[r], recv_sems.at[r], (px, py, c)).start()
        token[...] = jnp.zeros(token.shape, F32)

    land = lax.empty((3, H, C), p.dtype)
    return pl.pallas_call(
        body, name=name,
        out_shape=(pltpu.SemaphoreType.DMA((3,)), pltpu.SemaphoreType.DMA((3,)), pltpu.HBM(p.shape, p.dtype), pltpu.HBM(land.shape, land.dtype),
                   jax.ShapeDtypeStruct((8, LANES), F32)),
        in_specs=(HBM_SPEC, HBM_SPEC), out_specs=(SEM_SPEC, SEM_SPEC, HBM_SPEC, HBM_SPEC, VMEM_SPEC),
        input_output_aliases={0: 2, 1: 3},
        compiler_params=pltpu.CompilerParams(has_side_effects=SPLIT_EFFECT),
    )(pltpu.with_memory_space_constraint(p, pltpu.HBM), pltpu.with_memory_space_constraint(land, pltpu.HBM))


def _rs_chip_exchange_wait(send_sems, recv_sems, p_thru, land_thru, after, *, name):
    after = after if isinstance(after, (tuple, list)) else (after,)

    def body(p_ref, land_ref, send_sems, recv_sems, *rest):
        x, y, c = _place()
        for r, (px, py) in enumerate(_other_chips(x, y)):
            cp = _remote(p_ref.at[2 * px + py], land_ref.at[r], send_sems.at[r], recv_sems.at[r], (px, py, c))
            cp.wait_send()
            cp.wait_recv()

    return pl.pallas_call(
        body, name=name,
        out_shape=(pltpu.HBM(p_thru.shape, p_thru.dtype), pltpu.HBM(land_thru.shape, land_thru.dtype)),
        in_specs=(HBM_SPEC, HBM_SPEC, SEM_SPEC, SEM_SPEC) + (ANY_SPEC,) * len(after), out_specs=(HBM_SPEC, HBM_SPEC),
        input_output_aliases={0: 0, 1: 1},
        compiler_params=pltpu.CompilerParams(has_side_effects=SPLIT_EFFECT),
    )(p_thru, land_thru, send_sems, recv_sems, *after)


def _add4_f32(p, recv, chip_core, *, name):
    _, H, C = p.shape
    tr = _row_tile(H)
    nb = H // tr

    def body(s_ref, o_ref, r_ref, out_ref):
        out_ref[...] = ((o_ref[...].astype(F32) + r_ref[0].astype(F32)) + r_ref[1].astype(F32)) + r_ref[2].astype(F32)

    return pl.pallas_call(
        body, name=name, out_shape=jax.ShapeDtypeStruct((2 * H, C), F32),
        grid_spec=pltpu.PrefetchScalarGridSpec(
            num_scalar_prefetch=1, grid=(nb,),
            in_specs=[pl.BlockSpec((None, tr, C), lambda i, s: (s[0], i, 0)), pl.BlockSpec((3, tr, C), lambda i, s: (0, i, 0))],
            out_specs=pl.BlockSpec((tr, C), lambda i, s: (s[1] * nb + i, 0))),
        compiler_params=_cparams("parallel"),
    )(chip_core, p, recv)


def _rs_sibling_join(f, *, name):
    R, C = f.shape
    H = R // 2

    def body(f_ref, out_ref, send_sem, recv_sem):
        x, y, c = _place()
        sib = (x, y, 1 - c)
        mine = pl.ds(pl.multiple_of(c * H, 8), H)
        theirs = pl.ds(pl.multiple_of((1 - c) * H, 8), H)
        cp = _remote(f_ref.at[mine, :], out_ref.at[mine, :], send_sem, recv_sem, sib)
        cp.start()
        _remote(f_ref.at[mine, :], out_ref.at[theirs, :], send_sem, recv_sem, sib).wait_recv()
        cp.wait_send()

    return pl.pallas_call(
        body, name=name, out_shape=jax.ShapeDtypeStruct((R, C), f.dtype),
        in_specs=[HBM_SPEC], out_specs=HBM_SPEC, input_output_aliases={0: 0},
        scratch_shapes=[pltpu.SemaphoreType.DMA, pltpu.SemaphoreType.DMA],
    )(f)


SMALL_GATHER = (("pool_norm", 2, 256), ("pool_scale", 2, 512), ("conv_w", 3, 512), ("mla_norm", 1, 256),
                ("mla_q_norm", 1, 96), ("mla_kv_norm", 1, 64))
SMALL_SLOT = (16, 512)


def _gather_small(shards, *, name):
    def body(pn_ref, ps_ref, cw_ref, mn_ref, qn_ref, kn_ref, pn_o, ps_o, cw_o, mn_o, qn_o, kn_o, all_ref, send_sems, recv_sems):
        x, y, c = _place()
        mine = 2 * x + y
        all_ref[mine] = jnp.zeros(SMALL_SLOT, F32)
        all_ref[mine, 0:2, 0:256] = pn_ref[...]
        all_ref[mine, 2:4, :] = ps_ref[...]
        all_ref[mine, 4:7, :] = cw_ref[0]
        all_ref[mine, 7:8, 0:256] = mn_ref[...]
        all_ref[mine, 8:9, 0:96] = qn_ref[...]
        all_ref[mine, 9:10, 0:64] = kn_ref[...]
        chips = _other_chips(x, y)
        sends = [_remote(all_ref.at[mine], all_ref.at[mine], send_sems.at[r], recv_sems.at[r], (*chip, c)) for r, chip in enumerate(chips)]
        for cp in sends:
            cp.start()
        for r, (px, py) in enumerate(chips):
            _remote(all_ref.at[mine], all_ref.at[2 * px + py], send_sems.at[r], recv_sems.at[r], (px, py, c)).wait_recv()
        for cp in sends:
            cp.wait_send()
        for k in range(N_CHIPS):
            pn_o[:, k * 256:(k + 1) * 256] = all_ref[k, 0:2, 0:256]
            ps_o[:, k * 512:(k + 1) * 512] = all_ref[k, 2:4, :]
            cw_o[0, :, k * 512:(k + 1) * 512] = all_ref[k, 4:7, :]
            mn_o[:, k * 256:(k + 1) * 256] = all_ref[k, 7:8, 0:256]
            qn_o[k] = all_ref[k, 8:9, 0:96]
            kn_o[k] = all_ref[k, 9:10, 0:64]

    sds = lambda *shape: jax.ShapeDtypeStruct(shape, F32)
    out = pl.pallas_call(
        body, name=name,
        out_shape=(sds(2, 1024), sds(2, 2048), sds(1, 3, 2048), sds(1, 1024), sds(N_CHIPS, 1, 96), sds(N_CHIPS, 1, 64)),
        in_specs=[VMEM_SPEC] * 6, out_specs=(VMEM_SPEC,) * 6,
        scratch_shapes=[pltpu.VMEM((N_CHIPS,) + SMALL_SLOT, F32), pltpu.SemaphoreType.DMA((3,)), pltpu.SemaphoreType.DMA((3,))],
    )(*[shards[n] for n, _, _ in SMALL_GATHER])
    full = dict(zip([n for n, _, _ in SMALL_GATHER], out))
    full["mla_q_norm"] = full["mla_q_norm"].reshape(1, Q_LORA)
    full["mla_kv_norm"] = full["mla_kv_norm"].reshape(1, KV_LORA)
    return full


SMALL_REDUCE = (("pool_norm_0", 0, 1, 1024), ("pool_norm_1", 1, 1, 1024), ("pool_scale_0", 2, 1, 2048), ("pool_scale_1", 3, 1, 2048),
                ("conv_norm", 4, 1, 1024), ("mla_norm", 5, 1, 1024), ("mla_q_norm", 6, 1, 384), ("mla_kv_norm", 7, 1, 256),
                ("conv_w", 8, 8, 2048), ("final_norm", 16, 1, 1024))
REDUCE_SLOT = (24, 2048)


def _reduce_small(parts, after, *, name):
    keys = [k for k, _, _, _ in SMALL_REDUCE]

    def body(*refs):
        ins = dict(zip(keys, refs[:len(keys)]))
        pn_o, ps_o, cn_o, cw_o, mn_o, qn_o, kn_o, fn_o, all_ref, send_sems, recv_sems = refs[len(keys) + 1:]
        x, y, c = _place()
        me = 4 * x + 2 * y + c
        all_ref[me] = jnp.zeros(REDUCE_SLOT, F32)
        for k, r0, nr, wd in SMALL_REDUCE:
            all_ref[me, r0:r0 + nr, 0:wd] = ins[k][...]
        peers = []
        for rel in range(1, N_DEV):
            dx, dy, dc = (rel >> 2) & 1, (rel >> 1) & 1, rel & 1
            peers.append((1 - x if dx else x, 1 - y if dy else y, 1 - c if dc else c))
        sends = [_remote(all_ref.at[me], all_ref.at[me], send_sems.at[k], recv_sems.at[k], peer) for k, peer in enumerate(peers)]
        for cp in sends:
            cp.start()
        for k, (px, py, pc) in enumerate(peers):
            _remote(all_ref.at[me], all_ref.at[4 * px + 2 * py + pc], send_sems.at[k], recv_sems.at[k], (px, py, pc)).wait_recv()
        for cp in sends:
            cp.wait_send()

        def total(r0, nr, wd):
            acc = all_ref[0, r0:r0 + nr, 0:wd]
            for d in range(1, N_DEV):
                acc = acc + all_ref[d, r0:r0 + nr, 0:wd]
            return acc

        pn_o[0:1, :] = total(0, 1, 1024)
        pn_o[1:2, :] = total(1, 1, 1024)
        ps_o[0:1, :] = total(2, 1, 2048)
        ps_o[1:2, :] = total(3, 1, 2048)
        cn_o[...] = total(4, 1, 1024)
        mn_o[...] = total(5, 1, 1024)
        qn_o[...] = total(6, 1, Q_LORA)
        kn_o[...] = total(7, 1, KV_LORA)
        cw_o[0] = total(8, 3, 2048)
        fn_o[...] = total(16, 1, 1024)

    sds = lambda *shape: jax.ShapeDtypeStruct(shape, F32)
    out = pl.pallas_call(
        body, name=name,
        out_shape=(sds(2, 1024), sds(2, 2048), sds(1, 1024), sds(1, 3, 2048), sds(1, 1024), sds(1, Q_LORA), sds(1, KV_LORA), sds(1, 1024)),
        in_specs=[VMEM_SPEC] * len(keys) + [ANY_SPEC], out_specs=(VMEM_SPEC,) * 8,
        scratch_shapes=[pltpu.VMEM((N_DEV,) + REDUCE_SLOT, F32), pltpu.SemaphoreType.DMA((N_DEV - 1,)), pltpu.SemaphoreType.DMA((N_DEV - 1,))],
    )(*[parts[k] for k in keys], after)
    return dict(zip(("pool_norm", "pool_scale", "conv_norm", "conv_w", "mla_norm", "mla_q_norm", "mla_kv_norm", "final_norm"), out))


def _adam_math(w, g, m, v):
    m_new = ADAM_B1 * m + (1.0 - ADAM_B1) * g
    v_new = ADAM_B2 * v + (1.0 - ADAM_B2) * (g * g)
    m_hat = m_new / (1.0 - ADAM_B1 ** ADAM_STEP)
    v_hat = v_new / (1.0 - ADAM_B2 ** ADAM_STEP)
    return -ADAM_LR * (m_hat / (jnp.sqrt(v_hat) + ADAM_EPS) + ADAM_WD * w), m_new, v_new


def _adamw_small(w, m, v, g_full, chip, *, name):
    shp = {n: w[n].shape for n in SMALL}
    whole = lambda s: pl.BlockSpec(s, lambda i, c: (0,) * len(s))
    g_in, g_specs = {}, {}
    for n in SMALL:
        if not SMALL_SHARDED[n]:
            g_in[n], g_specs[n] = g_full[n].reshape(shp[n]), whole(shp[n])
        elif shp[n][-1] % LANES:
            g_in[n] = g_full[n].reshape(N_CHIPS, 1, shp[n][-1])
            g_specs[n] = pl.BlockSpec((None,) + shp[n], lambda i, c: (c[0], 0, 0))
        else:
            g_in[n] = g_full[n]
            nd = len(shp[n])
            g_specs[n] = pl.BlockSpec(shp[n], lambda i, c, nd=nd: (0,) * (nd - 1) + (c[0],))

    def body(c_ref, *refs):
        k = len(SMALL)
        w_r, m_r, v_r, g_r = refs[0:k], refs[k:2 * k], refs[2 * k:3 * k], refs[3 * k:4 * k]
        go_r, d_r, nm_r, nv_r = refs[4 * k:5 * k], refs[5 * k:6 * k], refs[6 * k:7 * k], refs[7 * k:8 * k]
        for i in range(k):
            gv = g_r[i][...]
            d, m_new, v_new = _adam_math(w_r[i][...], gv, m_r[i][...], v_r[i][...])
            go_r[i][...] = gv
            d_r[i][...] = d
            nm_r[i][...] = m_new
            nv_r[i][...] = v_new

    nat = [whole(shp[n]) for n in SMALL]
    out_sds = tuple(jax.ShapeDtypeStruct(shp[n], F32) for n in SMALL)
    out = pl.pallas_call(
        body, name=name, out_shape=out_sds * 4,
        grid_spec=pltpu.PrefetchScalarGridSpec(
            num_scalar_prefetch=1, grid=(1,),
            in_specs=nat * 3 + [g_specs[n] for n in SMALL], out_specs=tuple(nat) * 4),
        compiler_params=_cparams("arbitrary"),
    )(chip, *[w[n] for n in SMALL], *[m[n] for n in SMALL], *[v[n] for n in SMALL], *[g_in[n] for n in SMALL])
    k = len(SMALL)
    return tuple(dict(zip(SMALL, out[j * k:(j + 1) * k])) for j in range(4))


BIG = ("pool_w_in", "pool_w_grp", "pool_w_out", "conv_w_in", "conv_w_out", "mla_w_in", "mla_w_q_up", "mla_w_kv_up", "mla_w_out")
BIG_SHARD_AXIS = {"pool_w_in": 2, "pool_w_grp": 2, "pool_w_out": 1, "conv_w_in": 2, "conv_w_out": 1,
                  "mla_w_in": 2, "mla_w_q_up": 2, "mla_w_kv_up": 2, "mla_w_out": 1}
GATHER_LAYOUT = {
    "p0": ((("pool_w_in", 0), 0, "cols"), (("pool_w_out", 0), 1024, "rows"), (("pool_w_grp", 0), 1536, "flat")),
    "cv": ((("conv_w_in", 0), 0, "cols"), (("conv_w_out", 0), 2048, "rows")),
    "ml": ((("pool_w_in", 1), 0, "cols"), (("mla_w_out", 0), 1024, "rows"), (("pool_w_out", 1), 1536, "rows"),
           (("mla_w_kv_up", 0), 2048, "cols"), (("pool_w_grp", 1), 2304, "flat"), (("mla_w_q_up", 0), 2560, "flat"),
           (("mla_w_in", 0), 2848, "flat")),
}
REDUCE_LAYOUT = {
    "late": ((("conv_w_in", 0), 0, "cols"), (("pool_w_in", 1), 2048, "cols"), (("conv_w_out", 0), 3072, "rows"),
             (("mla_w_out", 0), 3584, "rows"), (("pool_w_out", 1), 4096, "rows"), (("mla_w_kv_up", 0), 4608, "cols"),
             (("pool_w_grp", 1), 4864, "flat"), (("mla_w_q_up", 0), 5120, "flat"), (("mla_w_in", 0), 5408, "flat")),
    "first": ((("pool_w_in", 0), 0, "cols"), (("pool_w_out", 0), 1024, "rows"), (("pool_w_grp", 0), 1536, "flat")),
}
PACK_ROW_ALIGN = 32
RS_ROW_ALIGN = 512


def _slot_rows(layout, shard_shape, align):
    where, end = {}, 0
    for piece, r0, kind in layout:
        n = 1
        for d in shard_shape(piece):
            n *= d
        assert r0 >= end and n % PACK_COLS == 0, (piece, r0, end)
        where[piece] = (r0, n // PACK_COLS, kind)
        end = r0 + n // PACK_COLS
    return end + (-end) % align, where


def _as_slot_rows(shard, kind):
    if kind == "cols":
        k, n = shard.shape
        return shard.reshape(k, n // PACK_COLS, PACK_COLS).swapaxes(0, 1).reshape(-1, PACK_COLS)
    return shard.reshape(-1, PACK_COLS)


def _pack_slot(shards, layout, rows, dtype):
    parts, end = [], 0
    for piece, r0, kind in layout:
        if r0 > end:
            parts.append(jnp.zeros((r0 - end, PACK_COLS), dtype))
        parts.append(_as_slot_rows(shards[piece], kind).astype(dtype))
        end = r0 + parts[-1].shape[0]
    if rows > end:
        parts.append(jnp.zeros((rows - end, PACK_COLS), dtype))
    return jnp.concatenate(parts, axis=0)


SMALL = ("pool_norm", "pool_scale", "conv_norm", "conv_w", "mla_norm", "mla_q_norm", "mla_kv_norm", "final_norm")
SMALL_SHARDED = {"pool_norm": True, "pool_scale": True, "conv_norm": False, "conv_w": True, "mla_norm": True,
                 "mla_q_norm": True, "mla_kv_norm": True, "final_norm": False}


def _rope_tables(positions):
    inv_freq = ROPE_BASE ** (-jnp.arange(0, QK_ROPE, 2, dtype=F32) / QK_ROPE)
    ang = positions.astype(F32).reshape(-1, 1) * inv_freq
    cos, sin = jnp.cos(ang), jnp.sin(ang)
    z32 = jnp.zeros_like(cos)
    z64 = jnp.concatenate([z32, z32], axis=1)
    return (jnp.concatenate([cos, cos, z64], axis=1), jnp.concatenate([-sin, z32, z64], axis=1),
            jnp.concatenate([z32, sin, z64], axis=1))


def _mla_in_to_padded(w):
    q, kv, kr, z = w[:, :Q_LORA], w[:, Q_LORA:Q_LORA + KV_LORA], w[:, Q_LORA + KV_LORA:Q_LORA + KV_LORA + QK_ROPE], w[:, Q_LORA + KV_LORA + QK_ROPE:]
    return jnp.concatenate([z, kv, q, kr, jnp.zeros((w.shape[0], MLA_IN_PAD - MLA_IN), w.dtype)], axis=1)


def _mla_in_from_padded(w):
    z, kv, q, kr = w[:, :D_INNER], w[:, D_INNER:D_INNER + KV_LORA], w[:, D_INNER + KV_LORA:D_INNER + KV_LORA + Q_LORA], w[:, D_INNER + KV_LORA + Q_LORA:D_INNER + KV_LORA + Q_LORA + QK_ROPE]
    return jnp.concatenate([q, kv, kr, z], axis=1)


def _q_up_to_padded(w):
    k = w.shape[0]
    return jnp.pad(w.reshape(k, N_HEADS, QK_NOPE + QK_ROPE), ((0, 0), (0, 0), (0, HEAD_PAD - QK_NOPE - QK_ROPE))).reshape(k, N_HEADS * HEAD_PAD)


def _q_up_from_padded(w):
    k = w.shape[0]
    return w.reshape(k, N_HEADS, HEAD_PAD)[:, :, :QK_NOPE + QK_ROPE].reshape(k, N_HEADS * (QK_NOPE + QK_ROPE))


def _local_step(x, positions, target, weights_for, ws, sink):
    S = x.shape[0]
    tm = min(512, S)
    te = min(256, S)
    tq = min(512, S)
    tabs = _rope_tables(positions)
    gs = {}

    def mm_in(xn, w, name, after=None):
        n = w.shape[1]
        tn = PACK_COLS if isinstance(w, Packed) else _pick(n, 1536 if n == MLA_IN_PAD else 1024)
        return _mm(xn, w, after=after, tm=min(1024, S), tn=tn, tk=D_MODEL, name=name)

    def mm_out(y, w, res, name):
        return _mm(y, w, residual=res, tm=tm, tn=D_MODEL, tk=D_INNER, name=name)

    def mm_dx(dy, w, name, after=None):
        k, n = w.shape
        if isinstance(w, Packed):
            tn, tk = (k if w.kind == "rows" else min(k, 1024)), PACK_COLS
        else:
            tn, tk = _pick(k, 1024), _pick(n, 1408)
        return _mm(dy, w, trans_b=True, after=after, tm=min(1024, S), tn=tn, tk=tk, name=name)

    def mm_dw(piece, a, b, name, after=None, post=None):
        ka, nb = a.shape[1], b.shape[1]
        into = sink.dest(piece)
        tokens = min(1024, S)
        if into is None:
            out = _mm(a, b, trans_a=True, out_dtype=BF16, after=after, tm=_pick(ka, 1024), tn=_pick(nb, 1408), tk=tokens, name=name)
            sink.put(piece, out if post is None else post(out))
        else:
            rows = ka if into.kind == "rows" else min(ka, 1024)
            sink.put(piece, _mm(a, b, trans_a=True, after=after, into=into, tm=rows, tn=PACK_COLS, tk=tokens, name=name))

    def pool_layer_fwd(xin, wts, j, tag):
        xn = _rms_fwd(xin, ws["pool_norm"][j:j + 1], tm=tm, name=f"{tag}_norm")
        h = mm_in(xn, wts[("pool_w_in", j)], f"{tag}_in", wts.get("after"))
        y = _pool_fwd(h, wts[("pool_w_grp", j)], ws["pool_scale"][j:j + 1], tm=te, name=f"{tag}_mix")
        xo = mm_out(y, wts[("pool_w_out", j)], xin, f"{tag}_out")
        return xo, (xin, xn, h, y)

    def pool_layer_bwd(dx, dxb, saved, wts, j, tag, after=None):
        xin, xn, h, y = saved
        dy = mm_dx(dxb, wts[("pool_w_out", j)], f"{tag}_dy", after)
        mm_dw(("pool_w_out", j), y, dxb, f"{tag}_dwo", after)
        pooled, dmixed, dpooled, dz, dsc = _pool_bwd1(h, dy, wts[("pool_w_grp", j)], ws["pool_scale"][j:j + 1], tm=te, name=f"{tag}_bmix")
        sink.put(("pool_w_grp", j), _grouped_tn(pooled, dmixed, tk=tm, name=f"{tag}_dwg"))
        dh = _pool_bwd2(dpooled, dz, tm=te, name=f"{tag}_bshift")
        dxn = mm_dx(dh, wts[("pool_w_in", j)], f"{tag}_dxn")
        mm_dw(("pool_w_in", j), xn, dh, f"{tag}_dwi")
        dxo, dxob, dg = _rms_bwd(xin, ws["pool_norm"][j:j + 1], dxn, dx, tm=tm, name=f"{tag}_bnorm")
        gs[f"pool_norm_{j}"], gs[f"pool_scale_{j}"] = dg, dsc
        return dxo, dxob

    w_p0 = weights_for("p0", None)
    x1, sv0 = pool_layer_fwd(x, w_p0, 0, "p0")

    w_cv = weights_for("cv", x1)
    xn1 = _rms_fwd(x1, ws["conv_norm"][0:1], tm=tm, name="cv_norm")
    h1 = mm_in(xn1, w_cv[("conv_w_in", 0)], "cv_in")
    cw = jnp.pad(ws["conv_w"][0], ((0, 5), (0, 0)))
    y1 = _conv_fwd(h1, cw, tm=te, name="cv_mix")
    x2 = mm_out(y1, w_cv[("conv_w_out", 0)], x1, "cv_out")

    w_ml = weights_for("ml", x2)
    w_mi = _mla_in_to_padded(w_ml[("mla_w_in", 0)])
    w_q = _q_up_to_padded(w_ml[("mla_w_q_up", 0)])
    w_kv = w_ml[("mla_w_kv_up", 0)]
    qg, kvg = ws["mla_q_norm"][0:1], ws["mla_kv_norm"][0:1]
    xn2 = _rms_fwd(x2, ws["mla_norm"][0:1], tm=tm, name="ml_norm")
    h2 = mm_in(xn2, w_mi, "ml_in")
    q_n, kv_n, krr = _mla_latent_fwd(h2, qg, kvg, tabs, tm=tm, name="ml_lat")
    q_full = _mla_q_up(q_n, w_q, tabs, tm=tm, name="ml_qup")
    k_full, v = _mla_kv_up(kv_n, w_kv, krr, tm=tm, name="ml_kvup")
    o, y2, lse = _flash_fwd(q_full, k_full, v, h2, tq=tq, name="ml_attn")
    x3 = mm_out(y2, w_ml[("mla_w_out", 0)], x2, "ml_out")

    x4, sv3 = pool_layer_fwd(x3, w_ml, 1, "p1")

    loss_part, dx, dxb, dgf = _final_loss(x4, ws["final_norm"].reshape(1, -1), target, tm=tm, name="final")
    gs["final_norm"] = dgf

    dx, dxb = pool_layer_bwd(dx, dxb, sv3, w_ml, 1, "p1")

    dy = mm_dx(dxb, w_ml[("mla_w_out", 0)], "ml_dy")
    mm_dw(("mla_w_out", 0), y2, dxb, "ml_dwo")
    do, dz, delta = _mla_gate_bwd(dy, o, h2, tm=tq, name="ml_bgate")
    dq_pre, dkv, dkr = _flash_bwd(q_full, k_full, v, do, lse, delta, tabs, tq=tq, name="ml_battn")
    dq_n = mm_dx(dq_pre, w_q, "ml_dqn")
    mm_dw(("mla_w_q_up", 0), q_n, dq_pre, "ml_dwq", post=_q_up_from_padded)
    dkv_n = mm_dx(dkv, w_kv, "ml_dkvn")
    mm_dw(("mla_w_kv_up", 0), kv_n, dkv, "ml_dwkv")
    dh2, dqg, dkvg = _mla_latent_bwd(h2, dq_n, dkv_n, dkr, dz, qg, kvg, tabs, tm=te, name="ml_blat")
    dxn2 = mm_dx(dh2, w_mi, "ml_dxn")
    mm_dw(("mla_w_in", 0), xn2, dh2, "ml_dwi", post=_mla_in_from_padded)
    dx, dxb, dg2 = _rms_bwd(x2, ws["mla_norm"][0:1], dxn2, dx, tm=tm, name="ml_bnorm")
    gs["mla_norm"], gs["mla_q_norm"], gs["mla_kv_norm"] = dg2, dqg, dkvg

    dy = mm_dx(dxb, w_cv[("conv_w_out", 0)], "cv_dy")
    mm_dw(("conv_w_out", 0), y1, dxb, "cv_dwo")
    dh1, dcw = _conv_bwd(h1, dy, cw, tm=te, name="cv_bmix")
    dxn1 = mm_dx(dh1, w_cv[("conv_w_in", 0)], "cv_dxn")
    mm_dw(("conv_w_in", 0), xn1, dh1, "cv_dwi")
    dx, dxb, dg1 = _rms_bwd(x1, ws["conv_norm"][0:1], dxn1, dx, tm=tm, name="cv_bnorm")
    gs["conv_norm"], gs["conv_w"] = dg1, dcw

    dx, dxb = pool_layer_bwd(dx, dxb, sv0, w_p0, 0, "p0", after=sink.late_ready())
    return loss_part, dx, gs


def kernel(x, positions, pool_norm, pool_w_in, pool_w_grp, pool_scale, pool_w_out, conv_norm, conv_w_in, conv_w, conv_w_out, mla_norm, mla_w_in, mla_q_norm, mla_w_q_up, mla_kv_norm, mla_w_kv_up, mla_w_out, final_norm, loss_target, m_pool_norm, m_pool_w_in, m_pool_w_grp, m_pool_scale, m_pool_w_out, m_conv_norm, m_conv_w_in, m_conv_w, m_conv_w_out, m_mla_norm, m_mla_w_in, m_mla_q_norm, m_mla_w_q_up, m_mla_kv_norm, m_mla_w_kv_up, m_mla_w_out, m_final_norm, v_pool_norm, v_pool_w_in, v_pool_w_grp, v_pool_scale, v_pool_w_out, v_conv_norm, v_conv_w_in, v_conv_w, v_conv_w_out, v_mla_norm, v_mla_w_in, v_mla_q_norm, v_mla_w_q_up, v_mla_kv_norm, v_mla_w_kv_up, v_mla_w_out, v_final_norm):
    names = ("pool_norm", "pool_w_in", "pool_w_grp", "pool_scale", "pool_w_out", "conv_norm", "conv_w_in", "conv_w", "conv_w_out",
             "mla_norm", "mla_w_in", "mla_q_norm", "mla_w_q_up", "mla_kv_norm", "mla_w_kv_up", "mla_w_out", "final_norm")
    w = dict(zip(names, (pool_norm, pool_w_in, pool_w_grp, pool_scale, pool_w_out, conv_norm, conv_w_in, conv_w, conv_w_out,
                         mla_norm, mla_w_in, mla_q_norm, mla_w_q_up, mla_kv_norm, mla_w_kv_up, mla_w_out, final_norm)))
    m = dict(zip(names, (m_pool_norm, m_pool_w_in, m_pool_w_grp, m_pool_scale, m_pool_w_out, m_conv_norm, m_conv_w_in, m_conv_w, m_conv_w_out,
                         m_mla_norm, m_mla_w_in, m_mla_q_norm, m_mla_w_q_up, m_mla_kv_norm, m_mla_w_kv_up, m_mla_w_out, m_final_norm)))
    v = dict(zip(names, (v_pool_norm, v_pool_w_in, v_pool_w_grp, v_pool_scale, v_pool_w_out, v_conv_norm, v_conv_w_in, v_conv_w, v_conv_w_out,
                         v_mla_norm, v_mla_w_in, v_mla_q_norm, v_mla_w_q_up, v_mla_kv_norm, v_mla_w_kv_up, v_mla_w_out, v_final_norm)))
    chip = 2 * lax.axis_index("x") + lax.axis_index("y")
    core = lax.axis_index("c")

    core1 = core.astype(jnp.int32).reshape(1)
    chip_core = jnp.stack([chip, core]).astype(jnp.int32)
    shard_shape = lambda piece: w[piece[0]].shape[1:]
    shard_axis = lambda piece: BIG_SHARD_AXIS[piece[0]] - 1
    full_shape = lambda piece: tuple(d * (N_CHIPS if a == shard_axis(piece) else 1) for a, d in enumerate(shard_shape(piece)))

    w_bf = {n: w[n].astype(BF16) for n in BIG}
    gather_rows, gather_at, packs = {}, {}, {}
    for grp, layout in GATHER_LAYOUT.items():
        gather_rows[grp], gather_at[grp] = _slot_rows(layout, shard_shape, PACK_ROW_ALIGN)
        packs[grp] = _pack_slot({(n, j): w_bf[n][j] for (n, j), _, _ in layout}, layout, gather_rows[grp], BF16)

    def gathered_weights(grp, gathered):
        out = {}
        for piece, (r0, n, kind) in gather_at[grp].items():
            if kind == "flat":
                out[piece] = jnp.concatenate([gathered[k, r0:r0 + n].reshape(shard_shape(piece)) for k in range(N_CHIPS)], axis=shard_axis(piece))
            else:
                out[piece] = Packed(gathered, r0, kind, full_shape(piece))
        return out

    gathered_p0 = _all_gather_big(packs["p0"], name="ag_p0")
    cv_start = _ag_ici_start(packs["cv"], gathered_p0, name="ag_cv_start")
    ml_start = _ag_ici_start(packs["ml"], cv_start[4], name="ag_ml_start")
    in_flight = {"cv": cv_start, "ml": ml_start}

    def weights_for(grp, after):
        if grp == "p0":
            return {**gathered_weights(grp, gathered_p0), "after": ml_start[4]}
        send_sems, recv_sems, w_thru, land, _ = in_flight[grp]
        w_thru, land = _ag_ici_wait(send_sems, recv_sems, w_thru, land, after, name=f"ag_{grp}_wait")
        return gathered_weights(grp, _ag_sibling_forward(land, w_thru, name=f"ag_{grp}_fwd"))

    ws = {"conv_norm": w["conv_norm"], "final_norm": w["final_norm"]}
    ws.update(_gather_small({n: w[n] for n, _, _ in SMALL_GATHER}, name="ag_small"))

    reduce_rows, reduce_at = {}, {}
    for grp, layout in REDUCE_LAYOUT.items():
        reduce_rows[grp], reduce_at[grp] = _slot_rows(layout, shard_shape, RS_ROW_ALIGN)
    group_of = {piece: grp for grp, layout in REDUCE_LAYOUT.items() for piece, _, _ in layout}

    class Sink:
        def __init__(self):
            self.buf = {grp: lax.empty((N_CHIPS, rows, PACK_COLS), BF16) for grp, rows in reduce_rows.items()}
            self.started = {}

        def dest(self, piece):
            grp = group_of[piece]
            r0, _, kind = reduce_at[grp][piece]
            return None if kind == "flat" else Packed(self.buf[grp], r0, kind, full_shape(piece))

        def put(self, piece, result):
            grp = group_of[piece]
            r0, n, kind = reduce_at[grp][piece]
            if kind == "flat":
                parts = jnp.split(result, N_CHIPS, axis=shard_axis(piece))
                result = lax.dynamic_update_slice(self.buf[grp], jnp.stack([p.reshape(n, PACK_COLS) for p in parts]), (0, r0, 0))
            self.buf[grp] = result

        def start(self, grp, tag):
            theirs = _rs_sibling_swap(self.buf[grp], name=f"{tag}_swap")
            chip_sum = _add2_bf16(self.buf[grp], theirs, core1, name=f"{tag}_add2")
            self.started[grp] = _rs_chip_exchange_start(chip_sum, name=f"{tag}_chips_start")
            return self.started[grp][4]

        def finish(self, grp, after, tag):
            send_sems, recv_sems, chip_sum, land, _ = self.started[grp]
            chip_sum, recv = _rs_chip_exchange_wait(send_sems, recv_sems, chip_sum, land, after, name=f"{tag}_chips_wait")
            half_sum = _add4_f32(chip_sum, recv, chip_core, name=f"{tag}_add4")
            return _rs_sibling_join(half_sum, name=f"{tag}_join")

        def late_ready(self):
            return self.start("late", "rsa")

    sink = Sink()

    loss_part, grad_x, gs = _local_step(x[0], positions, loss_target[0], weights_for, ws, sink)
    loss = lax.psum(loss_part[0, 0], ("x", "y", "c"))

    g, delta, new_m, new_v = {}, {}, {}, {}

    def adam_big(n):
        nj = w[n].shape[0]
        where = [(group_of[(n, j)],) + reduce_at[group_of[(n, j)]][(n, j)] for j in range(nj)]
        if where[0][3] == "flat":
            g[n] = jnp.stack([g_rows[grp][r0:r0 + rows].reshape(w[n].shape[1:]) for grp, r0, rows, _ in where])
            shp = w[n].shape
            two_d = lambda a: a.reshape(-1, shp[-1])
            d_, m_, v_ = _adamw(two_d(w[n]), two_d(g[n]), two_d(m[n]), two_d(v[n]), name=f"adamw_{n}")
            delta[n], new_m[n], new_v[n] = d_.reshape(shp), m_.reshape(shp), v_.reshape(shp)
        else:
            g[n], delta[n], new_m[n], new_v[n] = _adamw_rows(w[n], m[n], v[n], [(g_rows[grp], r0) for grp, r0, _, _ in where], name=f"adamw_{n}")

    first_token = sink.start("first", "rsb")
    g_rows = {"late": sink.finish("late", first_token, "rsa")}
    late_only = [n for n in BIG if all(group_of[(n, j)] == "late" for j in range(w[n].shape[0]))]
    for n in late_only:
        adam_big(n)
    g_rows["first"] = sink.finish("first", tuple(delta[n] for n in late_only), "rsb")
    for n in BIG:
        if n not in late_only:
            adam_big(n)

    gs_sum = _reduce_small(gs, g_rows["first"], name="ar_small")
    row = lambda d: {n: (d[n].reshape(1, -1) if d[n].ndim == 1 else d[n]) for n in SMALL}
    small_out = _adamw_small(row(w), row(m), row(v), gs_sum, chip.astype(jnp.int32).reshape(1), name="adamw_small")
    for dst, res in zip((g, delta, new_m, new_v), small_out):
        for n in SMALL:
            dst[n] = res[n].reshape(w[n].shape)

    return (loss, grad_x[None], *[g[n] for n in names], *[delta[n] for n in names],
            *[new_m[n] for n in names], *[new_v[n] for n in names])
```

```python
import functools

import jax
import jax.numpy as jnp
from jax import lax
from jax.experimental import pallas as pl
from jax.experimental.pallas import tpu as pltpu

F32 = jnp.float32
BF16 = jnp.bfloat16

D_MODEL = 1024
D_INNER = 2048
POOL_WINDOWS = (2, 4, 8, 16)
POOL_GROUP = 512
N_HEADS = 16
QK_NOPE = 128
QK_ROPE = 64
V_DIM = 128
HEAD_PAD = 256
Q_LORA = 384
KV_LORA = 256
MLA_IN = Q_LORA + KV_LORA + QK_ROPE + D_INNER
MLA_IN_PAD = 2816
ATTN_SCALE = (QK_NOPE + QK_ROPE) ** -0.5
ROPE_BASE = 10000.0
NORM_EPS = 1e-6
HALO = 16

ADAM_LR = 0.001
ADAM_B1 = 0.9
ADAM_B2 = 0.999
ADAM_EPS = 1e-08
ADAM_WD = 0.01
ADAM_STEP = 10

N_CHIPS = 4
N_DEV = 8
LANES = 128
PACK_COLS = 1024
V7X_VMEM_LIMIT = 56 * 1024 * 1024
MESH = pl.DeviceIdType.MESH


def _cparams(*sem):
    return pltpu.CompilerParams(dimension_semantics=sem, vmem_limit_bytes=V7X_VMEM_LIMIT)


def _pick(n, cap):
    best = None
    for d in range(LANES, min(n, cap) + 1, LANES):
        if n % d == 0:
            best = d
    assert best is not None, (n, cap)
    return best


def _sigmoid(z):
    return 1.0 / (1.0 + jnp.exp(-z))


class Packed:
    def __init__(self, buf, r0, kind, shape):
        self.buf, self.r0, self.kind, self.shape = buf, r0, kind, shape

    def block(self, rb, cb, bk):
        assert self.r0 % bk == 0, (self.r0, bk)
        if self.kind == "cols":
            K = self.shape[0]
            per = self.shape[1] // (N_CHIPS * PACK_COLS)
            assert K % bk == 0
            return cb // per, (self.r0 + (cb % per) * K) // bk + rb
        kk = self.shape[0] // N_CHIPS
        assert kk % bk == 0
        per = kk // bk
        return rb // per, self.r0 // bk + rb % per


def _mm(a, b, *, trans_a=False, trans_b=False, out_dtype=F32, residual=None, after=None, into=None, tm, tn, tk, name):
    if trans_a:
        K, M = a.shape
    else:
        M, K = a.shape
    if trans_b:
        N, K2 = b.shape
    else:
        K2, N = b.shape
    assert K == K2 and M % tm == 0 and N % tn == 0 and K % tk == 0, (name, a.shape, b.shape, tm, tn, tk)
    nk = K // tk
    dn = (((0 if trans_a else 1,), (1 if trans_b else 0,)), ((), ()))
    has_res = residual is not None
    n_skip = (after is not None) + (into is not None)
    b_all = isinstance(b, Packed) and b.kind == "rows"
    o_all = into is not None and into.kind == "rows"
    assert not b_all or (not trans_a and (tn == N if trans_b else tk == K)), name
    assert not o_all or tm == M, name

    def body(*refs):
        if has_res:
            a_ref, b_ref, r_ref = refs[:3]
            refs = refs[3:]
        else:
            a_ref, b_ref = refs[:2]
            r_ref = None
            refs = refs[2:]
        refs = refs[n_skip:]
        o_ref, rest = refs[0], refs[1:]
        if b_all and trans_b:
            part = jnp.concatenate([lax.dot_general(a_ref[...], b_ref[c], dn, preferred_element_type=F32) for c in range(N_CHIPS)], axis=1)
        elif b_all:
            kk = K // N_CHIPS
            part = lax.dot_general(a_ref[:, 0:kk], b_ref[0], dn, preferred_element_type=F32)
            for c in range(1, N_CHIPS):
                part = part + lax.dot_general(a_ref[:, c * kk:(c + 1) * kk], b_ref[c], dn, preferred_element_type=F32)
        else:
            part = lax.dot_general(a_ref[...], b_ref[...], dn, preferred_element_type=F32)

        def finish(acc):
            if has_res:
                acc = acc + r_ref[...]
            if o_all:
                mk = M // N_CHIPS
                for c in range(N_CHIPS):
                    o_ref[c] = acc[c * mk:(c + 1) * mk].astype(o_ref.dtype)
            else:
                o_ref[...] = acc.astype(o_ref.dtype)

        if nk == 1:
            finish(part)
        else:
            acc_ref = rest[0]
            k = pl.program_id(2)

            @pl.when(k == 0)
            def _():
                acc_ref[...] = part

            @pl.when(k > 0)
            def _():
                acc_ref[...] += part

            @pl.when(k == nk - 1)
            def _():
                finish(acc_ref[...])

    a_spec = pl.BlockSpec((tk, tm), lambda i, j, k: (k, i)) if trans_a else pl.BlockSpec((tm, tk), lambda i, j, k: (i, k))
    if b_all:
        kkb = b.shape[0] // N_CHIPS
        b_spec = pl.BlockSpec((N_CHIPS, kkb, PACK_COLS), lambda i, j, k: (0, b.r0 // kkb, 0))
        b_arg = b.buf
    elif isinstance(b, Packed):
        if trans_b:
            assert tk == PACK_COLS
            b_spec = pl.BlockSpec((None, tn, tk), lambda i, j, k: (*b.block(j, k, tn), 0))
        else:
            assert tn == PACK_COLS
            b_spec = pl.BlockSpec((None, tk, tn), lambda i, j, k: (*b.block(k, j, tk), 0))
        b_arg = b.buf
    else:
        b_spec = pl.BlockSpec((tn, tk), lambda i, j, k: (j, k)) if trans_b else pl.BlockSpec((tk, tn), lambda i, j, k: (k, j))
        b_arg = b
    o_spec = pl.BlockSpec((tm, tn), lambda i, j, k: (i, j))
    in_specs = [a_spec, b_spec] + ([o_spec] if has_res else [])
    args = (a, b_arg) + ((residual,) if has_res else ())
    aliases = {}
    if after is not None:
        in_specs.append(pl.BlockSpec(memory_space=pl.ANY))
        args += (after,)
    if into is None:
        out_shape, out_spec = jax.ShapeDtypeStruct((M, N), out_dtype), o_spec
    else:
        assert tn == PACK_COLS and into.shape == (M, N)
        in_specs.append(pl.BlockSpec(memory_space=pl.ANY))
        aliases = {len(args): 0}
        args += (into.buf,)
        out_shape = jax.ShapeDtypeStruct(into.buf.shape, into.buf.dtype)
        if o_all:
            out_spec = pl.BlockSpec((N_CHIPS, M // N_CHIPS, tn), lambda i, j, k: (0, into.r0 // (M // N_CHIPS), 0))
        else:
            out_spec = pl.BlockSpec((None, tm, tn), lambda i, j, k: (*into.block(i, j, tm), 0))
    return pl.pallas_call(
        body, name=name, out_shape=out_shape,
        grid=(M // tm, N // tn, nk),
        in_specs=in_specs, out_specs=out_spec, input_output_aliases=aliases,
        scratch_shapes=[pltpu.VMEM((tm, tn), F32)] if nk > 1 else [],
        compiler_params=_cparams("parallel", "parallel", "arbitrary"),
    )(*args)


def _grouped_tn(a, b, *, tk, name):
    S = a.shape[0]
    G = POOL_GROUP
    nk = S // tk

    def body(a_ref, b_ref, o_ref, acc_ref):
        k = pl.program_id(1)
        part = lax.dot_general(a_ref[...], b_ref[...], (((0,), (0,)), ((), ())), preferred_element_type=F32)

        @pl.when(k == 0)
        def _():
            acc_ref[...] = part

        @pl.when(k > 0)
        def _():
            acc_ref[...] += part

        @pl.when(k == nk - 1)
        def _():
            o_ref[...] = acc_ref[...].astype(o_ref.dtype)

    return pl.pallas_call(
        body, name=name,
        out_shape=jax.ShapeDtypeStruct((len(POOL_WINDOWS), G, G), BF16),
        grid=(len(POOL_WINDOWS), nk),
        in_specs=[pl.BlockSpec((tk, G), lambda g, k: (k, g)), pl.BlockSpec((tk, G), lambda g, k: (k, g))],
        out_specs=pl.BlockSpec((None, G, G), lambda g, k: (g, 0, 0)),
        scratch_shapes=[pltpu.VMEM((G, G), F32)],
        compiler_params=_cparams("parallel", "arbitrary"),
    )(a, b)


def _rms_fwd(x, g, *, tm, name):
    S, D = x.shape

    def body(x_ref, g_ref, o_ref):
        xv = x_ref[...]
        rstd = lax.rsqrt(jnp.mean(xv * xv, axis=-1, keepdims=True) + NORM_EPS)
        o_ref[...] = (xv * rstd * g_ref[...]).astype(o_ref.dtype)

    return pl.pallas_call(
        body, name=name,
        out_shape=jax.ShapeDtypeStruct((S, D), BF16),
        grid=(S // tm,),
        in_specs=[pl.BlockSpec((tm, D), lambda i: (i, 0)), pl.BlockSpec((1, D), lambda i: (0, 0))],
        out_specs=pl.BlockSpec((tm, D), lambda i: (i, 0)),
        compiler_params=_cparams("parallel"),
    )(x, g)


def _rms_bwd_math(xv, gv, dxn):
    rstd = lax.rsqrt(jnp.mean(xv * xv, axis=-1, keepdims=True) + NORM_EPS)
    xh = xv * rstd
    dg = jnp.sum(dxn * xh, axis=0, keepdims=True)
    dxh = dxn * gv
    dx = rstd * (dxh - xh * jnp.mean(dxh * xh, axis=-1, keepdims=True))
    return dx, dg


def _rms_bwd(x, g, dxn, dres, *, tm, name):
    S, D = x.shape

    def body(x_ref, g_ref, dxn_ref, dres_ref, dx_ref, dxb_ref, dg_ref):
        dx, dg = _rms_bwd_math(x_ref[...], g_ref[...], dxn_ref[...])
        dx = dx + dres_ref[...]
        dx_ref[...] = dx
        dxb_ref[...] = dx.astype(BF16)

        @pl.when(pl.program_id(0) == 0)
        def _():
            dg_ref[...] = dg

        @pl.when(pl.program_id(0) > 0)
        def _():
            dg_ref[...] += dg

    row = pl.BlockSpec((tm, D), lambda i: (i, 0))
    vec = pl.BlockSpec((1, D), lambda i: (0, 0))
    return pl.pallas_call(
        body, name=name,
        out_shape=(jax.ShapeDtypeStruct((S, D), F32), jax.ShapeDtypeStruct((S, D), BF16), jax.ShapeDtypeStruct((1, D), F32)),
        grid=(S // tm,),
        in_specs=[row, vec, row, row],
        out_specs=(row, row, vec),
        compiler_params=_cparams("arbitrary"),
    )(x, g, dxn, dres)


def _final_loss(x, g, target, *, tm, name):
    S, D = x.shape

    def body(x_ref, g_ref, t_ref, loss_ref, dx_ref, dxb_ref, dg_ref):
        xv = x_ref[...]
        gv = g_ref[...]
        rstd = lax.rsqrt(jnp.mean(xv * xv, axis=-1, keepdims=True) + NORM_EPS)
        xh = xv * rstd
        err = xh * gv - t_ref[...]
        part = 0.5 * jnp.sum(jnp.mean(err * err, axis=-1, keepdims=True), axis=0, keepdims=True)
        dy = err * (1.0 / D)
        dg = jnp.sum(dy * xh, axis=0, keepdims=True)
        dxh = dy * gv
        dx = rstd * (dxh - xh * jnp.mean(dxh * xh, axis=-1, keepdims=True))
        dx_ref[...] = dx
        dxb_ref[...] = dx.astype(BF16)
        lossb = jnp.broadcast_to(part, loss_ref.shape)

        @pl.when(pl.program_id(0) == 0)
        def _():
            dg_ref[...] = dg
            loss_ref[...] = lossb

        @pl.when(pl.program_id(0) > 0)
        def _():
            dg_ref[...] += dg
            loss_ref[...] += lossb

    row = pl.BlockSpec((tm, D), lambda i: (i, 0))
    vec = pl.BlockSpec((1, D), lambda i: (0, 0))
    lspec = pl.BlockSpec((8, LANES), lambda i: (0, 0))
    return pl.pallas_call(
        body, name=name,
        out_shape=(jax.ShapeDtypeStruct((8, LANES), F32), jax.ShapeDtypeStruct((S, D), F32),
                   jax.ShapeDtypeStruct((S, D), BF16), jax.ShapeDtypeStruct((1, D), F32)),
        grid=(S // tm,),
        in_specs=[row, vec, row],
        out_specs=(lspec, row, row, vec),
        compiler_params=_cparams("arbitrary"),
    )(x, g, target)


def _prev_halo_spec(tm, width, col):
    r = tm // HALO
    return pl.BlockSpec((HALO, width), lambda i: (jnp.maximum(i * r - 1, 0), col))


def _next_halo_spec(tm, width, col, S):
    r = tm // HALO
    last = S // HALO - 1
    return pl.BlockSpec((HALO, width), lambda i: (jnp.minimum((i + 1) * r, last), col))


def _shift_down(ext, k):
    return pltpu.roll(ext, k, 0)[HALO:, :]


def _shift_up(ext, k, tm):
    n = ext.shape[0]
    return pltpu.roll(ext, n - k, 0)[:tm, :]


def _pool_window_sum(ext, w):
    s = ext
    k = 1
    while k < w:
        s = s + pltpu.roll(s, k, 0)
        k *= 2
    return s[HALO:, :]


def _pooled_group(u_ref, halo, g, w, t_idx):
    cs = slice(g * POOL_GROUP, (g + 1) * POOL_GROUP)
    u = u_ref[:, cs]
    ext = jnp.concatenate([halo[:, cs], u], axis=0)
    inv = 1.0 / jnp.minimum(t_idx + 1, w).astype(F32)
    return _pool_window_sum(ext, w) * inv - u


def _pool_fwd(h, w_grp, scale, *, tm, name):
    S = h.shape[0]
    E = D_INNER

    def body(u_ref, uh_ref, z_ref, wg_ref, sc_ref, y_ref):
        i = pl.program_id(0)
        halo = jnp.where(i > 0, uh_ref[...], 0.0)
        t_idx = i * tm + lax.broadcasted_iota(jnp.int32, (tm, 1), 0)
        for g, w in enumerate(POOL_WINDOWS):
            cs = slice(g * POOL_GROUP, (g + 1) * POOL_GROUP)
            pooled = _pooled_group(u_ref, halo, g, w, t_idx)
            mixed = jnp.dot(pooled.astype(BF16), wg_ref[g], preferred_element_type=F32)
            z = z_ref[:, cs]
            y_ref[:, cs] = (mixed * sc_ref[:, cs] * (z * _sigmoid(z))).astype(BF16)

    return pl.pallas_call(
        body, name=name,
        out_shape=jax.ShapeDtypeStruct((S, E), BF16),
        grid=(S // tm,),
        in_specs=[pl.BlockSpec((tm, E), lambda i: (i, 0)), _prev_halo_spec(tm, E, 0),
                  pl.BlockSpec((tm, E), lambda i: (i, 1)),
                  pl.BlockSpec((len(POOL_WINDOWS), POOL_GROUP, POOL_GROUP), lambda i: (0, 0, 0)),
                  pl.BlockSpec((1, E), lambda i: (0, 0))],
        out_specs=pl.BlockSpec((tm, E), lambda i: (i, 0)),
        compiler_params=_cparams("parallel"),
    )(h, h, h, w_grp, scale)


def _pool_bwd1(h, dy, w_grp, scale, *, tm, name):
    S = h.shape[0]
    E = D_INNER

    def body(u_ref, uh_ref, z_ref, dy_ref, wg_ref, sc_ref, pooled_ref, dmixed_ref, dpooled_ref, dz_ref, dsc_ref):
        i = pl.program_id(0)
        halo = jnp.where(i > 0, uh_ref[...], 0.0)
        t_idx = i * tm + lax.broadcasted_iota(jnp.int32, (tm, 1), 0)
        for g, w in enumerate(POOL_WINDOWS):
            cs = slice(g * POOL_GROUP, (g + 1) * POOL_GROUP)
            pooled = _pooled_group(u_ref, halo, g, w, t_idx).astype(BF16)
            wg = wg_ref[g]
            mixed = jnp.dot(pooled, wg, preferred_element_type=F32)
            z = z_ref[:, cs]
            sg = _sigmoid(z)
            dyv = dy_ref[:, cs]
            sc = sc_ref[:, cs]
            dms = dyv * (z * sg)
            dz = dyv * (mixed * sc) * (sg * (1.0 + z * (1.0 - sg)))
            dsc = jnp.sum(dms * mixed, axis=0, keepdims=True)
            dmixed = (dms * sc).astype(BF16)
            dpooled = lax.dot_general(dmixed, wg, (((1,), (1,)), ((), ())), preferred_element_type=F32)
            pooled_ref[:, cs] = pooled
            dmixed_ref[:, cs] = dmixed
            dpooled_ref[:, cs] = dpooled
            dz_ref[:, cs] = dz.astype(BF16)

            @pl.when(i == 0)
            def _():
                dsc_ref[:, cs] = dsc

            @pl.when(i > 0)
            def _():
                dsc_ref[:, cs] += dsc

    row = pl.BlockSpec((tm, E), lambda i: (i, 0))
    vec = pl.BlockSpec((1, E), lambda i: (0, 0))
    return pl.pallas_call(
        body, name=name,
        out_shape=(jax.ShapeDtypeStruct((S, E), BF16), jax.ShapeDtypeStruct((S, E), BF16),
                   jax.ShapeDtypeStruct((S, E), F32), jax.ShapeDtypeStruct((S, E), BF16),
                   jax.ShapeDtypeStruct((1, E), F32)),
        grid=(S // tm,),
        in_specs=[row, _prev_halo_spec(tm, E, 0), pl.BlockSpec((tm, E), lambda i: (i, 1)), row,
                  pl.BlockSpec((len(POOL_WINDOWS), POOL_GROUP, POOL_GROUP), lambda i: (0, 0, 0)), vec],
        out_specs=(row, row, row, row, vec),
        compiler_params=_cparams("arbitrary"),
    )(h, h, h, dy, w_grp, scale)


def _pool_bwd2(dpooled, dz, *, tm, name):
    S = dpooled.shape[0]
    E = D_INNER
    nt = S // tm

    def body(dp_ref, dpn_ref, dz_ref, dh_ref):
        i = pl.program_id(0)
        nxt = jnp.where(i < nt - 1, dpn_ref[...], 0.0)
        t_ext = i * tm + lax.broadcasted_iota(jnp.int32, (tm + HALO, 1), 0)
        for g, w in enumerate(POOL_WINDOWS):
            cs = slice(g * POOL_GROUP, (g + 1) * POOL_GROUP)
            dp = dp_ref[:, cs]
            inv = 1.0 / jnp.minimum(t_ext + 1, w).astype(F32)
            s = jnp.concatenate([dp, nxt[:, cs]], axis=0) * inv
            n = tm + HALO
            k = 1
            while k < w:
                s = s + pltpu.roll(s, n - k, 0)
                k *= 2
            dh_ref[:, cs] = (s[:tm, :] - dp).astype(BF16)
        dh_ref[:, E:] = dz_ref[...]

    return pl.pallas_call(
        body, name=name,
        out_shape=jax.ShapeDtypeStruct((S, 2 * E), BF16),
        grid=(nt,),
        in_specs=[pl.BlockSpec((tm, E), lambda i: (i, 0)), _next_halo_spec(tm, E, 0, S),
                  pl.BlockSpec((tm, E), lambda i: (i, 0))],
        out_specs=pl.BlockSpec((tm, 2 * E), lambda i: (i, 0)),
        compiler_params=_cparams("parallel"),
    )(dpooled, dpooled, dz)


CONV_CHUNK = 512


def _conv_fwd(h, cw, *, tm, name):
    S = h.shape[0]
    E = D_INNER

    def body(b_ref, c_ref, hh_ref, z_ref, ch_ref, hhh_ref, w_ref, y_ref):
        i = pl.program_id(0)
        for j in range(E // CONV_CHUNK):
            cs = slice(j * CONV_CHUNK, (j + 1) * CONV_CHUNK)
            p = c_ref[:, cs] * hh_ref[:, cs]
            ph = jnp.where(i > 0, ch_ref[:, cs] * hhh_ref[:, cs], 0.0)
            ext = jnp.concatenate([ph, p], axis=0)
            conv = w_ref[2:3, cs] * p + w_ref[1:2, cs] * _shift_down(ext, 1) + w_ref[0:1, cs] * _shift_down(ext, 2)
            z = z_ref[:, cs]
            y_ref[:, cs] = (b_ref[:, cs] * conv * (z * _sigmoid(z))).astype(BF16)

    col = lambda c: pl.BlockSpec((tm, E), lambda i: (i, c))
    return pl.pallas_call(
        body, name=name,
        out_shape=jax.ShapeDtypeStruct((S, E), BF16),
        grid=(S // tm,),
        in_specs=[col(0), col(1), col(2), col(3), _prev_halo_spec(tm, E, 1), _prev_halo_spec(tm, E, 2),
                  pl.BlockSpec((8, E), lambda i: (0, 0))],
        out_specs=pl.BlockSpec((tm, E), lambda i: (i, 0)),
        compiler_params=_cparams("parallel"),
    )(h, h, h, h, h, h, cw)


def _conv_bwd(h, dy, cw, *, tm, name):
    S = h.shape[0]
    E = D_INNER
    nt = S // tm

    def body(b_ref, c_ref, hh_ref, z_ref, dy_ref, ch_ref, hhh_ref, bn_ref, zn_ref, dyn_ref, w_ref, dh_ref, dw_ref):
        i = pl.program_id(0)
        for j in range(E // CONV_CHUNK):
            cs = slice(j * CONV_CHUNK, (j + 1) * CONV_CHUNK)
            w0, w1, w2 = w_ref[0:1, cs], w_ref[1:2, cs], w_ref[2:3, cs]
            c, hh, b, z, dyv = c_ref[:, cs], hh_ref[:, cs], b_ref[:, cs], z_ref[:, cs], dy_ref[:, cs]
            p = c * hh
            ph = jnp.where(i > 0, ch_ref[:, cs] * hhh_ref[:, cs], 0.0)
            ext = jnp.concatenate([ph, p], axis=0)
            pm1 = _shift_down(ext, 1)
            pm2 = _shift_down(ext, 2)
            conv = w2 * p + w1 * pm1 + w0 * pm2
            sg = _sigmoid(z)
            dy0 = dyv * (z * sg)
            dz = dyv * (b * conv) * (sg * (1.0 + z * (1.0 - sg)))
            db = dy0 * conv
            dconv = dy0 * b
            zn = zn_ref[:, cs]
            dconv_n = jnp.where(i < nt - 1, dyn_ref[:, cs] * (zn * _sigmoid(zn)) * bn_ref[:, cs], 0.0)
            dext = jnp.concatenate([dconv, dconv_n], axis=0)
            dp = w2 * dconv + w1 * _shift_up(dext, 1, tm) + w0 * _shift_up(dext, 2, tm)
            dh_ref[:, 0 * E + j * CONV_CHUNK:0 * E + (j + 1) * CONV_CHUNK] = db.astype(BF16)
            dh_ref[:, 1 * E + j * CONV_CHUNK:1 * E + (j + 1) * CONV_CHUNK] = (dp * hh).astype(BF16)
            dh_ref[:, 2 * E + j * CONV_CHUNK:2 * E + (j + 1) * CONV_CHUNK] = (dp * c).astype(BF16)
            dh_ref[:, 3 * E + j * CONV_CHUNK:3 * E + (j + 1) * CONV_CHUNK] = dz.astype(BF16)
            dw = jnp.concatenate([jnp.sum(dconv * pm2, axis=0, keepdims=True),
                                  jnp.sum(dconv * pm1, axis=0, keepdims=True),
                                  jnp.sum(dconv * p, axis=0, keepdims=True),
                                  jnp.zeros((5, CONV_CHUNK), F32)], axis=0)

            @pl.when(i == 0)
            def _():
                dw_ref[:, cs] = dw

            @pl.when(i > 0)
            def _():
                dw_ref[:, cs] += dw

    col = lambda c: pl.BlockSpec((tm, E), lambda i: (i, c))
    return pl.pallas_call(
        body, name=name,
        out_shape=(jax.ShapeDtypeStruct((S, 4 * E), BF16), jax.ShapeDtypeStruct((8, E), F32)),
        grid=(nt,),
        in_specs=[col(0), col(1), col(2), col(3), pl.BlockSpec((tm, E), lambda i: (i, 0)),
                  _prev_halo_spec(tm, E, 1), _prev_halo_spec(tm, E, 2),
                  _next_halo_spec(tm, E, 0, S), _next_halo_spec(tm, E, 3, S), _next_halo_spec(tm, E, 0, S),
                  pl.BlockSpec((8, E), lambda i: (0, 0))],
        out_specs=(pl.BlockSpec((tm, 4 * E), lambda i: (i, 0)), pl.BlockSpec((8, E), lambda i: (0, 0))),
        compiler_params=_cparams("arbitrary"),
    )(h, h, h, h, dy, h, h, h, h, dy, cw)


Z_COLS = D_INNER // LANES
KV_LAT_BLK = D_INNER // KV_LORA
Q_LAT_BLK = (D_INNER + KV_LORA) // Q_LORA
K_ROPE_BLK = (D_INNER + KV_LORA + Q_LORA) // LANES


def _rope(blk, c, s1, s2):
    return blk * c + pltpu.roll(blk, LANES - QK_ROPE // 2, 1) * s1 + pltpu.roll(blk, QK_ROPE // 2, 1) * s2


def _unrope(blk, c, s1, s2):
    return blk * c - pltpu.roll(blk, LANES - QK_ROPE // 2, 1) * s1 - pltpu.roll(blk, QK_ROPE // 2, 1) * s2


def _lat_norm(v, g):
    rstd = lax.rsqrt(jnp.mean(v * v, axis=-1, keepdims=True) + NORM_EPS)
    return v * rstd * g


def _mla_latent_fwd(h, q_norm, kv_norm, tabs, *, tm, name):
    S = h.shape[0]

    def body(kv_ref, q_ref, kr_ref, qg_ref, kvg_ref, c_ref, s1_ref, s2_ref, qn_ref, kvn_ref, krr_ref):
        qn_ref[...] = _lat_norm(q_ref[...], qg_ref[...]).astype(BF16)
        kvn_ref[...] = _lat_norm(kv_ref[...], kvg_ref[...]).astype(BF16)
        krr_ref[...] = _rope(kr_ref[...], c_ref[...], s1_ref[...], s2_ref[...]).astype(BF16)

    tab = pl.BlockSpec((tm, LANES), lambda i: (i, 0))
    return pl.pallas_call(
        body, name=name,
        out_shape=(jax.ShapeDtypeStruct((S, Q_LORA), BF16), jax.ShapeDtypeStruct((S, KV_LORA), BF16),
                   jax.ShapeDtypeStruct((S, LANES), BF16)),
        grid=(S // tm,),
        in_specs=[pl.BlockSpec((tm, KV_LORA), lambda i: (i, KV_LAT_BLK)), pl.BlockSpec((tm, Q_LORA), lambda i: (i, Q_LAT_BLK)),
                  pl.BlockSpec((tm, LANES), lambda i: (i, K_ROPE_BLK)),
                  pl.BlockSpec((1, Q_LORA), lambda i: (0, 0)), pl.BlockSpec((1, KV_LORA), lambda i: (0, 0)), tab, tab, tab],
        out_specs=(pl.BlockSpec((tm, Q_LORA), lambda i: (i, 0)), pl.BlockSpec((tm, KV_LORA), lambda i: (i, 0)), tab),
        compiler_params=_cparams("parallel"),
    )(h, h, h, q_norm, kv_norm, *tabs)


def _mla_q_up(q_n, w_q_pad, tabs, *, tm, name):
    S = q_n.shape[0]

    def body(a_ref, w_ref, c_ref, s1_ref, s2_ref, o_ref):
        a = a_ref[...]
        for hd in range(N_HEADS):
            acc = jnp.dot(a, w_ref[:, hd * HEAD_PAD:(hd + 1) * HEAD_PAD], preferred_element_type=F32)
            o_ref[hd, :, :QK_NOPE] = acc[:, :QK_NOPE].astype(BF16)
            o_ref[hd, :, QK_NOPE:] = _rope(acc[:, QK_NOPE:], c_ref[...], s1_ref[...], s2_ref[...]).astype(BF16)

    tab = pl.BlockSpec((tm, LANES), lambda i: (i, 0))
    return pl.pallas_call(
        body, name=name,
        out_shape=jax.ShapeDtypeStruct((N_HEADS, S, HEAD_PAD), BF16),
        grid=(S // tm,),
        in_specs=[pl.BlockSpec((tm, Q_LORA), lambda i: (i, 0)), pl.BlockSpec((Q_LORA, N_HEADS * HEAD_PAD), lambda i: (0, 0)),
                  tab, tab, tab],
        out_specs=pl.BlockSpec((N_HEADS, tm, HEAD_PAD), lambda i: (0, i, 0)),
        compiler_params=_cparams("parallel"),
    )(q_n, w_q_pad, *tabs)


def _mla_kv_up(kv_n, w_kv, krr, *, tm, name):
    S = kv_n.shape[0]
    heads_per_chip = N_HEADS // N_CHIPS

    def body(a_ref, w_ref, krr_ref, k_ref, v_ref):
        a = a_ref[...]
        ones = jnp.ones((tm, V_DIM), BF16)
        for hd in range(N_HEADS):
            lo = (hd % heads_per_chip) * HEAD_PAD
            acc = jnp.dot(a, w_ref[hd // heads_per_chip, :, lo:lo + HEAD_PAD], preferred_element_type=F32)
            k_ref[hd, :, :QK_NOPE] = acc[:, :QK_NOPE].astype(BF16)
            k_ref[hd, :, QK_NOPE:] = krr_ref[...]
            v_ref[hd, :, :V_DIM] = acc[:, QK_NOPE:].astype(BF16)
            v_ref[hd, :, V_DIM:] = ones

    head_blk = pl.BlockSpec((N_HEADS, tm, HEAD_PAD), lambda i: (0, i, 0))
    sds = jax.ShapeDtypeStruct((N_HEADS, S, HEAD_PAD), BF16)
    return pl.pallas_call(
        body, name=name, out_shape=(sds, sds),
        grid=(S // tm,),
        in_specs=[pl.BlockSpec((tm, KV_LORA), lambda i: (i, 0)),
                  pl.BlockSpec((N_CHIPS, KV_LORA, PACK_COLS), lambda i: (0, w_kv.r0 // KV_LORA, 0)),
                  pl.BlockSpec((tm, LANES), lambda i: (i, 0))],
        out_specs=(head_blk, head_blk),
        compiler_params=_cparams("parallel"),
    )(kv_n, w_kv.buf, krr)


LOG2E = 1.4426950408889634
SCORE_TO_LOG2 = ATTN_SCALE * LOG2E


def _flash_fwd(q_full, k_full, v_aug, h, *, tq, name):
    H, S, _ = q_full.shape
    tk = tq
    HP = 2
    QT = 2 if (S // tq) % 2 == 0 else 1
    rows_of = lambda t: slice(t * tq, (t + 1) * tq)

    def body(q_ref, k_ref, v_ref, z_ref, o_ref, y_ref, lse_ref, m_sc, acc_sc):
        first = pl.program_id(1) * QT
        m_sc[...] = jnp.full(m_sc.shape, -1e30, F32)
        acc_sc[...] = jnp.zeros(acc_sc.shape, F32)

        def chunk(j, masked):
            off = pl.multiple_of(j * tk, tk)
            for hh in range(HP):
                kj = k_ref[hh, pl.ds(off, tk), :]
                vj = v_ref[hh, pl.ds(off, tk), :]
                for t in range(QT):
                    if masked[t] is None:
                        continue
                    rows = rows_of(t)
                    s = lax.dot_general(q_ref[hh, rows], kj, (((1,), (1,)), ((), ())), preferred_element_type=F32) * SCORE_TO_LOG2
                    if masked[t]:
                        keep = lax.broadcasted_iota(jnp.int32, (tq, tk), 1) <= lax.broadcasted_iota(jnp.int32, (tq, tk), 0)
                        s = jnp.where(keep, s, -1e30)
                    m_old = m_sc[hh, rows]
                    m_new = jnp.maximum(m_old, jnp.max(s, axis=1, keepdims=True))
                    p = jnp.exp2(s - jnp.tile(m_new, (1, tk // LANES)))
                    alpha = jnp.exp2(m_old - m_new)
                    acc_sc[hh, rows] = jnp.tile(alpha, (1, 2)) * acc_sc[hh, rows] + jnp.dot(p.astype(BF16), vj, preferred_element_type=F32)
                    m_sc[hh, rows] = m_new

        def step(j, carry):
            chunk(j, (False,) * QT)
            return carry

        lax.fori_loop(0, first, step, 0)
        for d in range(QT):
            chunk(first + d, tuple(None if t < d else t == d for t in range(QT)))
        for hh in range(HP):
            cs = slice(hh * V_DIM, (hh + 1) * V_DIM)
            for t in range(QT):
                rows = rows_of(t)
                l = acc_sc[hh, rows, V_DIM:]
                o = acc_sc[hh, rows, :V_DIM] / l
                z = z_ref[rows, cs]
                o_ref[rows, cs] = o
                y_ref[rows, cs] = (o * (z * _sigmoid(z))).astype(BF16)
                lse_ref[hh, t] = (m_sc[hh, rows] + jnp.log2(l)).T[0:1, :]

    pair = pl.BlockSpec((QT * tq, HP * V_DIM), lambda hd, i: (i, hd))
    return pl.pallas_call(
        body, name=name,
        out_shape=(jax.ShapeDtypeStruct((S, D_INNER), F32), jax.ShapeDtypeStruct((S, D_INNER), BF16),
                   jax.ShapeDtypeStruct((H, S // tq, 1, tq), F32)),
        grid=(H // HP, S // (QT * tq)),
        in_specs=[pl.BlockSpec((HP, QT * tq, HEAD_PAD), lambda hd, i: (hd, i, 0)),
                  pl.BlockSpec((HP, S, HEAD_PAD), lambda hd, i: (hd, 0, 0)),
                  pl.BlockSpec((HP, S, HEAD_PAD), lambda hd, i: (hd, 0, 0)),
                  pair],
        out_specs=(pair, pair, pl.BlockSpec((HP, QT, 1, tq), lambda hd, i: (hd, i, 0, 0))),
        scratch_shapes=[pltpu.VMEM((HP, QT * tq, LANES), F32), pltpu.VMEM((HP, QT * tq, HEAD_PAD), F32)],
        compiler_params=_cparams("parallel", "parallel"),
    )(q_full, k_full, v_aug, h)


def _mla_gate_bwd(dy, o, h, *, tm, name):
    S = dy.shape[0]
    E = D_INNER

    def body(dy_ref, o_ref, z_ref, do_ref, dz_ref, delta_ref):
        for hd in range(N_HEADS):
            cs = slice(hd * V_DIM, (hd + 1) * V_DIM)
            z = z_ref[:, cs]
            sg = _sigmoid(z)
            dyv = dy_ref[:, cs]
            ov = o_ref[:, cs]
            do = dyv * (z * sg)
            do_ref[:, cs] = do.astype(BF16)
            dz_ref[:, cs] = (dyv * ov * (sg * (1.0 + z * (1.0 - sg)))).astype(BF16)
            delta_ref[hd, 0] = jnp.broadcast_to(jnp.sum(do * ov, axis=-1, keepdims=True), (tm, LANES)).T[0:1, :]

    row = pl.BlockSpec((tm, E), lambda i: (i, 0))
    return pl.pallas_call(
        body, name=name,
        out_shape=(jax.ShapeDtypeStruct((S, E), BF16), jax.ShapeDtypeStruct((S, E), BF16),
                   jax.ShapeDtypeStruct((N_HEADS, S // tm, 1, tm), F32)),
        grid=(S // tm,),
        in_specs=[row, row, row],
        out_specs=(row, row, pl.BlockSpec((N_HEADS, 1, 1, tm), lambda i: (0, i, 0, 0))),
        compiler_params=_cparams("parallel"),
    )(dy, o, h)


def _flash_bwd(q_full, k_full, v_aug, do, lse_rows, delta_rows, tabs, *, tq, name):
    H, S, _ = q_full.shape
    tk = tq
    nq = S // tq
    HP = 2

    def body(q_ref, k_ref, v_ref, do_ref, lse_ref, dl_ref, c_ref, s1_ref, s2_ref, dqp_ref, dkv_ref, dkr_ref, dq_ref, dk_sc, dv_sc):
        kj = pl.program_id(1)

        @pl.when(kj == 0)
        def _():
            dq_ref[...] = jnp.zeros(dq_ref.shape, F32)

        dk_sc[...] = jnp.zeros(dk_sc.shape, F32)
        dv_sc[...] = jnp.zeros(dv_sc.shape, F32)

        def chunk(qi, diag):
            off = pl.multiple_of(qi * tq, tq)
            for hh in range(HP):
                k = k_ref[hh]
                q = q_ref[hh, pl.ds(off, tq), :]
                dov = do_ref[pl.ds(off, tq), hh * V_DIM:(hh + 1) * V_DIM]
                s_t = lax.dot_general(k, q, (((1,), (1,)), ((), ())), preferred_element_type=F32) * SCORE_TO_LOG2
                p_t = jnp.exp2(s_t - lse_ref[hh, qi])
                if diag:
                    keep = lax.broadcasted_iota(jnp.int32, (tk, tq), 0) <= lax.broadcasted_iota(jnp.int32, (tk, tq), 1)
                    p_t = jnp.where(keep, p_t, 0.0)
                dv_sc[hh] += jnp.dot(p_t.astype(BF16), dov, preferred_element_type=F32)
                dp_t = lax.dot_general(v_ref[hh], dov, (((1,), (1,)), ((), ())), preferred_element_type=F32)
                ds = (p_t * (dp_t - dl_ref[hh, qi])).astype(BF16)
                dk_sc[hh] += jnp.dot(ds, q, preferred_element_type=F32)
                dq_ref[pl.ds(off, tq), hh * HEAD_PAD:(hh + 1) * HEAD_PAD] += lax.dot_general(
                    ds, k, (((0,), (0,)), ((), ())), preferred_element_type=F32)

        def step(qi, carry):
            chunk(qi, False)
            return carry

        chunk(kj, True)
        done = pl.ds(pl.multiple_of(kj * tq, tq), tq)
        for hh in range(HP):
            lo = hh * HEAD_PAD
            dqp_ref[:, lo:lo + QK_NOPE] = (dq_ref[done, lo:lo + QK_NOPE] * ATTN_SCALE).astype(BF16)
            dqp_ref[:, lo + QK_NOPE:lo + HEAD_PAD] = _unrope(dq_ref[done, lo + QK_NOPE:lo + HEAD_PAD] * ATTN_SCALE,
                                                             c_ref[...], s1_ref[...], s2_ref[...]).astype(BF16)
        lax.fori_loop(kj + 1, nq, step, 0)
        for hh in range(HP):
            lo = hh * HEAD_PAD
            dkv_ref[:, lo:lo + QK_NOPE] = (dk_sc[hh, :, :QK_NOPE] * ATTN_SCALE).astype(BF16)
            dkv_ref[:, lo + QK_NOPE:lo + HEAD_PAD] = dv_sc[hh].astype(BF16)
            dkr_ref[hh] = dk_sc[hh, :, QK_NOPE:] * ATTN_SCALE

    tab = pl.BlockSpec((tk, LANES), lambda hd, j: (j, 0))
    pair_rows = pl.BlockSpec((tk, HP * HEAD_PAD), lambda hd, j: (j, hd))
    return pl.pallas_call(
        body, name=name,
        out_shape=(jax.ShapeDtypeStruct((S, H * HEAD_PAD), BF16), jax.ShapeDtypeStruct((S, H * HEAD_PAD), BF16),
                   jax.ShapeDtypeStruct((H, S, LANES), F32)),
        grid=(H // HP, S // tk),
        in_specs=[pl.BlockSpec((HP, S, HEAD_PAD), lambda hd, j: (hd, 0, 0)),
                  pl.BlockSpec((HP, tk, HEAD_PAD), lambda hd, j: (hd, j, 0)),
                  pl.BlockSpec((HP, tk, V_DIM), lambda hd, j: (hd, j, 0)),
                  pl.BlockSpec((S, HP * V_DIM), lambda hd, j: (0, hd)),
                  pl.BlockSpec((HP, nq, 1, tq), lambda hd, j: (hd, 0, 0, 0)),
                  pl.BlockSpec((HP, nq, 1, tq), lambda hd, j: (hd, 0, 0, 0)), tab, tab, tab],
        out_specs=(pair_rows, pair_rows, pl.BlockSpec((HP, tk, LANES), lambda hd, j: (hd, j, 0))),
        scratch_shapes=[pltpu.VMEM((S, HP * HEAD_PAD), F32), pltpu.VMEM((HP, tk, HEAD_PAD), F32), pltpu.VMEM((HP, tk, V_DIM), F32)],
        compiler_params=_cparams("parallel", "arbitrary"),
    )(q_full, k_full, v_aug, do, lse_rows, delta_rows, *tabs)


def _mla_latent_bwd(h, dq_n, dkv_n, dkr, dz, q_norm, kv_norm, tabs, *, tm, name):
    S = h.shape[0]

    def body(kv_ref, q_ref, dqn_ref, dkvn_ref, dkr_ref, dz_ref, qg_ref, kvg_ref, c_ref, s1_ref, s2_ref, dh_ref, dqg_ref, dkvg_ref):
        i = pl.program_id(0)
        dq_lat, dqg = _rms_bwd_math(q_ref[...], qg_ref[...], dqn_ref[...])
        dkv_lat, dkvg = _rms_bwd_math(kv_ref[...], kvg_ref[...], dkvn_ref[...])
        dkr_sum = dkr_ref[0]
        for hd in range(1, N_HEADS):
            dkr_sum = dkr_sum + dkr_ref[hd]
        dh_ref[:, :D_INNER] = dz_ref[...]
        dh_ref[:, D_INNER:D_INNER + KV_LORA] = dkv_lat.astype(BF16)
        dh_ref[:, D_INNER + KV_LORA:D_INNER + KV_LORA + Q_LORA] = dq_lat.astype(BF16)
        dh_ref[:, D_INNER + KV_LORA + Q_LORA:] = _unrope(dkr_sum, c_ref[...], s1_ref[...], s2_ref[...]).astype(BF16)

        @pl.when(i == 0)
        def _():
            dqg_ref[...] = dqg
            dkvg_ref[...] = dkvg

        @pl.when(i > 0)
        def _():
            dqg_ref[...] += dqg
            dkvg_ref[...] += dkvg

    tab = pl.BlockSpec((tm, LANES), lambda i: (i, 0))
    qvec = pl.BlockSpec((1, Q_LORA), lambda i: (0, 0))
    kvvec = pl.BlockSpec((1, KV_LORA), lambda i: (0, 0))
    return pl.pallas_call(
        body, name=name,
        out_shape=(jax.ShapeDtypeStruct((S, MLA_IN_PAD), BF16), jax.ShapeDtypeStruct((1, Q_LORA), F32),
                   jax.ShapeDtypeStruct((1, KV_LORA), F32)),
        grid=(S // tm,),
        in_specs=[pl.BlockSpec((tm, KV_LORA), lambda i: (i, KV_LAT_BLK)), pl.BlockSpec((tm, Q_LORA), lambda i: (i, Q_LAT_BLK)),
                  pl.BlockSpec((tm, Q_LORA), lambda i: (i, 0)), pl.BlockSpec((tm, KV_LORA), lambda i: (i, 0)),
                  pl.BlockSpec((N_HEADS, tm, LANES), lambda i: (0, i, 0)), pl.BlockSpec((tm, D_INNER), lambda i: (i, 0)),
                  qvec, kvvec, tab, tab, tab],
        out_specs=(pl.BlockSpec((tm, MLA_IN_PAD), lambda i: (i, 0)), qvec, kvvec),
        compiler_params=_cparams("arbitrary"),
    )(h, h, dq_n, dkv_n, dkr, dz, q_norm, kv_norm, *tabs)


def _adamw(w, g, m, v, *, name):
    R, C = w.shape
    tr = R
    for cand in (512, 256, 128, 64, 32, 16, 8):
        if R % cand == 0 and cand * C * 4 <= 2 * 1024 * 1024:
            tr = cand
            break

    def body(w_ref, g_ref, m_ref, v_ref, d_ref, nm_ref, nv_ref):
        d_ref[...], nm_ref[...], nv_ref[...] = _adam_math(w_ref[...], g_ref[...], m_ref[...], v_ref[...])

    spec = pl.BlockSpec((tr, C), lambda i: (i, 0))
    sds = jax.ShapeDtypeStruct((R, C), F32)
    return pl.pallas_call(
        body, name=name, out_shape=(sds, sds, sds), grid=(R // tr,),
        in_specs=[spec] * 4, out_specs=(spec,) * 3,
        compiler_params=_cparams("parallel"),
    )(w, g, m, v)


def _adamw_rows(w, m, v, srcs, *, name):
    nj, R, C = w.shape
    assert len(srcs) == nj and C % PACK_COLS == 0
    tr = min(R, 256)
    assert R % tr == 0 and all(r0 % tr == 0 for _, r0 in srcs)

    def body(*refs):
        w_ref, m_ref, v_ref = refs[:3]
        g_refs = refs[3:3 + nj]
        go_ref, d_ref, nm_ref, nv_ref = refs[3 + nj:]
        gv = g_refs[0][...]
        for jj in range(1, nj):
            gv = jnp.where(pl.program_id(0) == jj, g_refs[jj][...], gv)
        d, m_new, v_new = _adam_math(w_ref[...], gv, m_ref[...], v_ref[...])
        go_ref[...] = gv
        d_ref[...] = d
        nm_ref[...] = m_new
        nv_ref[...] = v_new

    nat = pl.BlockSpec((None, tr, PACK_COLS), lambda j, cb, i: (j, i, cb))

    def src_spec(jj, r0):
        return pl.BlockSpec((tr, PACK_COLS), lambda j, cb, i: (jnp.where(j == jj, (r0 + cb * R) // tr + i, r0 // tr), 0))

    sds = jax.ShapeDtypeStruct((nj, R, C), F32)
    return pl.pallas_call(
        body, name=name, out_shape=(sds,) * 4, grid=(nj, C // PACK_COLS, R // tr),
        in_specs=[nat] * 3 + [src_spec(jj, r0) for jj, (_, r0) in enumerate(srcs)], out_specs=(nat,) * 4,
        compiler_params=_cparams("parallel", "parallel", "parallel"),
    )(w, m, v, *[rows for rows, _ in srcs])


HBM_SPEC = pl.BlockSpec(memory_space=pltpu.HBM)
VMEM_SPEC = pl.BlockSpec(memory_space=pltpu.VMEM)
SEM_SPEC = pl.BlockSpec(memory_space=pltpu.SEMAPHORE)
ANY_SPEC = pl.BlockSpec(memory_space=pl.ANY)
SPLIT_EFFECT = pltpu.SideEffectType.DATAFLOW_SIDE_EFFECTING


def _place():
    return lax.axis_index("x"), lax.axis_index("y"), lax.axis_index("c")


def _other_chips(x, y):
    return [(1 - x, y), (x, 1 - y), (1 - x, 1 - y)]


def _remote(src, dst, send_sem, recv_sem, dev):
    return pltpu.make_async_remote_copy(src_ref=src, dst_ref=dst, send_sem=send_sem, recv_sem=recv_sem,
                                        device_id=dev, device_id_type=MESH)


def _ag_ici_start(wp, after, *, name):
    R, C = wp.shape
    H = R // 2

    def body(w_ref, land_ref, after_ref, send_sems, recv_sems, w_thru, land_thru, token):
        x, y, c = _place()
        rows = pl.ds(pl.multiple_of(c * H, 16), H)
        for r, chip in enumerate(_other_chips(x, y)):
            _remote(w_ref.at[rows, :], land_ref.at[2 * x + y, rows, :], send_sems.at[r], recv_sems.at[r], (*chip, c)).start()
        token[...] = jnp.zeros(token.shape, F32)

    land = lax.empty((N_CHIPS, R, C), wp.dtype)
    return pl.pallas_call(
        body, name=name,
        out_shape=(pltpu.SemaphoreType.DMA((3,)), pltpu.SemaphoreType.DMA((3,)), pltpu.HBM(wp.shape, wp.dtype), pltpu.HBM(land.shape, land.dtype),
                   jax.ShapeDtypeStruct((8, LANES), F32)),
        in_specs=(HBM_SPEC, HBM_SPEC, ANY_SPEC), out_specs=(SEM_SPEC, SEM_SPEC, HBM_SPEC, HBM_SPEC, VMEM_SPEC),
        input_output_aliases={0: 2, 1: 3},
        compiler_params=pltpu.CompilerParams(has_side_effects=SPLIT_EFFECT),
    )(pltpu.with_memory_space_constraint(wp, pltpu.HBM), pltpu.with_memory_space_constraint(land, pltpu.HBM), after)


def _ag_ici_wait(send_sems, recv_sems, w_thru, land_thru, after, *, name):
    R, C = w_thru.shape
    H = R // 2
    after = after if isinstance(after, (tuple, list)) else (after,)

    def body(w_ref, land_ref, send_sems, recv_sems, *rest):
        x, y, c = _place()
        rows = pl.ds(pl.multiple_of(c * H, 16), H)
        for r, (px, py) in enumerate(_other_chips(x, y)):
            cp = _remote(w_ref.at[rows, :], land_ref.at[2 * px + py, rows, :], send_sems.at[r], recv_sems.at[r], (px, py, c))
            cp.wait_send()
            cp.wait_recv()

    return pl.pallas_call(
        body, name=name,
        out_shape=(pltpu.HBM(w_thru.shape, w_thru.dtype), pltpu.HBM(land_thru.shape, land_thru.dtype)),
        in_specs=(HBM_SPEC, HBM_SPEC, SEM_SPEC, SEM_SPEC) + (ANY_SPEC,) * len(after), out_specs=(HBM_SPEC, HBM_SPEC),
        input_output_aliases={0: 0, 1: 1},
        compiler_params=pltpu.CompilerParams(has_side_effects=SPLIT_EFFECT),
    )(w_thru, land_thru, send_sems, recv_sems, *after)


def _ag_sibling_forward(land, wp, *, name):
    _, R, C = land.shape
    H = R // 2

    def body(land_ref, w_ref, out_ref, send_sems, recv_sems):
        x, y, c = _place()
        sib = (x, y, 1 - c)
        mine = pl.ds(pl.multiple_of(c * H, 16), H)
        theirs = pl.ds(pl.multiple_of((1 - c) * H, 16), H)
        chips = _other_chips(x, y)
        sends = [_remote(land_ref.at[2 * px + py, mine, :], out_ref.at[2 * px + py, mine, :], send_sems.at[r], recv_sems.at[r], sib)
                 for r, (px, py) in enumerate(chips)]
        sends.append(_remote(w_ref, out_ref.at[2 * x + y], send_sems.at[3], recv_sems.at[3], sib))
        for cp in sends:
            cp.start()
        for r, (px, py) in enumerate(chips):
            _remote(land_ref.at[2 * px + py, mine, :], out_ref.at[2 * px + py, theirs, :], send_sems.at[r], recv_sems.at[r], sib).wait_recv()
        sends[3].wait_recv()
        for cp in sends:
            cp.wait_send()

    return pl.pallas_call(
        body, name=name, out_shape=jax.ShapeDtypeStruct(land.shape, land.dtype),
        in_specs=[HBM_SPEC, HBM_SPEC], out_specs=HBM_SPEC, input_output_aliases={0: 0},
        scratch_shapes=[pltpu.SemaphoreType.DMA((4,)), pltpu.SemaphoreType.DMA((4,))],
    )(land, wp)


def _rs_sibling_swap(g, *, name):
    _, R, C = g.shape
    H = R // 2

    def body(g_ref, theirs_ref, send_sems, recv_sems):
        x, y, c = _place()
        sib = (x, y, 1 - c)
        copies = [_remote(g_ref.at[k, pl.ds(pl.multiple_of((1 - c) * H, 16), H), :], theirs_ref.at[k],
                          send_sems.at[k], recv_sems.at[k], sib) for k in range(N_CHIPS)]
        for cp in copies:
            cp.start()
        for cp in copies:
            cp.wait()

    return pl.pallas_call(
        body, name=name, out_shape=jax.ShapeDtypeStruct((N_CHIPS, H, C), g.dtype),
        in_specs=[HBM_SPEC], out_specs=HBM_SPEC,
        scratch_shapes=[pltpu.SemaphoreType.DMA((N_CHIPS,)), pltpu.SemaphoreType.DMA((N_CHIPS,))],
    )(g)


def _row_tile(h):
    best = 16
    for d in range(16, 1025, 16):
        if h % d == 0:
            best = d
    return best


def _add2_bf16(g, theirs, core, *, name):
    K, H, C = theirs.shape
    tr = _row_tile(H)
    nb = H // tr

    def body(c_ref, a_ref, b_ref, o_ref):
        o_ref[...] = (a_ref[...].astype(F32) + b_ref[...].astype(F32)).astype(o_ref.dtype)

    spec = pl.BlockSpec((None, tr, C), lambda k, i, c: (k, i, 0))
    return pl.pallas_call(
        body, name=name, out_shape=jax.ShapeDtypeStruct((K, H, C), theirs.dtype),
        grid_spec=pltpu.PrefetchScalarGridSpec(
            num_scalar_prefetch=1, grid=(K, nb),
            in_specs=[pl.BlockSpec((None, tr, C), lambda k, i, c: (k, c[0] * nb + i, 0)), spec], out_specs=spec),
        compiler_params=_cparams("parallel", "parallel"),
    )(core, g, theirs)


def _rs_chip_exchange_start(p, *, name):
    _, H, C = p.shape

    def body(p_ref, land_ref, send_sems, recv_sems, p_thru, land_thru, token):
        x, y, c = _place()
        for r, (px, py) in enumerate(_other_chips(x, y)):
            _remote(p_ref.at[2 * px + py], land_ref.at[r], send_sems.at[r], recv_sems.at[r], (px, py, c)).start()
        token[...] = jnp.zeros(token.shape, F32)

    land = lax.empty((3, H, C), p.dtype)
    return pl.pallas_call(
        body, name=name,
        out_shape=(pltpu.SemaphoreType.DMA((3,)), pltpu.SemaphoreType.DMA((3,)), pltpu.HBM(p.shape, p.dtype), pltpu.HBM(land.shape, land.dtype),
                   jax.ShapeDtypeStruct((8, LANES), F32)),
        in_specs=(HBM_SPEC, HBM_SPEC), out_specs=(SEM_SPEC, SEM_SPEC, HBM_SPEC, HBM_SPEC, VMEM_SPEC),
        input_output_aliases={0: 2, 1: 3},
        compiler_params=pltpu.CompilerParams(has_side_effects=SPLIT_EFFECT),
    )(pltpu.with_memory_space_constraint(p, pltpu.HBM), pltpu.with_memory_space_constraint(land, pltpu.HBM))


def _rs_chip_exchange_wait(send_sems, recv_sems, p_thru, land_thru, after, *, name):
    after = after if isinstance(after, (tuple, list)) else (after,)

    def body(p_ref, land_ref, send_sems, recv_sems, *rest):
        x, y, c = _place()
        for r, (px, py) in enumerate(_other_chips(x, y)):
            cp = _remote(p_ref.at[2 * px + py], land_ref.at[r], send_sems.at[r], recv_sems.at[r], (px, py, c))
            cp.wait_send()
            cp.wait_recv()

    return pl.pallas_call(
        body, name=name,
        out_shape=(pltpu.HBM(p_thru.shape, p_thru.dtype), pltpu.HBM(land_thru.shape, land_thru.dtype)),
        in_specs=(HBM_SPEC, HBM_SPEC, SEM_SPEC, SEM_SPEC) + (ANY_SPEC,) * len(after), out_specs=(HBM_SPEC, HBM_SPEC),
        input_output_aliases={0: 0, 1: 1},
        compiler_params=pltpu.CompilerParams(has_side_effects=SPLIT_EFFECT),
    )(p_thru, land_thru, send_sems, recv_sems, *after)


def _add4_f32(p, recv, chip_core, *, name):
    _, H, C = p.shape
    tr = _row_tile(H)
    nb = H // tr

    def body(s_ref, o_ref, r_ref, out_ref):
        out_ref[...] = ((o_ref[...].astype(F32) + r_ref[0].astype(F32)) + r_ref[1].astype(F32)) + r_ref[2].astype(F32)

    return pl.pallas_call(
        body, name=name, out_shape=jax.ShapeDtypeStruct((2 * H, C), F32),
        grid_spec=pltpu.PrefetchScalarGridSpec(
            num_scalar_prefetch=1, grid=(nb,),
            in_specs=[pl.BlockSpec((None, tr, C), lambda i, s: (s[0], i, 0)), pl.BlockSpec((3, tr, C), lambda i, s: (0, i, 0))],
            out_specs=pl.BlockSpec((tr, C), lambda i, s: (s[1] * nb + i, 0))),
        compiler_params=_cparams("parallel"),
    )(chip_core, p, recv)


def _rs_sibling_join(f, *, name):
    R, C = f.shape
    H = R // 2

    def body(f_ref, out_ref, send_sem, recv_sem):
        x, y, c = _place()
        sib = (x, y, 1 - c)
        mine = pl.ds(pl.multiple_of(c * H, 8), H)
        theirs = pl.ds(pl.multiple_of((1 - c) * H, 8), H)
        cp = _remote(f_ref.at[mine, :], out_ref.at[mine, :], send_sem, recv_sem, sib)
        cp.start()
        _remote(f_ref.at[mine, :], out_ref.at[theirs, :], send_sem, recv_sem, sib).wait_recv()
        cp.wait_send()

    return pl.pallas_call(
        body, name=name, out_shape=jax.ShapeDtypeStruct((R, C), f.dtype),
        in_specs=[HBM_SPEC], out_specs=HBM_SPEC, input_output_aliases={0: 0},
        scratch_shapes=[pltpu.SemaphoreType.DMA, pltpu.SemaphoreType.DMA],
    )(f)


SMALL_GATHER = (("pool_norm", 2, 256), ("pool_scale", 2, 512), ("conv_w", 3, 512), ("mla_norm", 1, 256),
                ("mla_q_norm", 1, 96), ("mla_kv_norm", 1, 64))
SMALL_SLOT = (16, 512)


def _gather_small(shards, *, name):
    def body(pn_ref, ps_ref, cw_ref, mn_ref, qn_ref, kn_ref, pn_o, ps_o, cw_o, mn_o, qn_o, kn_o, all_ref, send_sems, recv_sems):
        x, y, c = _place()
        mine = 2 * x + y
        all_ref[mine] = jnp.zeros(SMALL_SLOT, F32)
        all_ref[mine, 0:2, 0:256] = pn_ref[...]
        all_ref[mine, 2:4, :] = ps_ref[...]
        all_ref[mine, 4:7, :] = cw_ref[0]
        all_ref[mine, 7:8, 0:256] = mn_ref[...]
        all_ref[mine, 8:9, 0:96] = qn_ref[...]
        all_ref[mine, 9:10, 0:64] = kn_ref[...]
        chips = _other_chips(x, y)
        sends = [_remote(all_ref.at[mine], all_ref.at[mine], send_sems.at[r], recv_sems.at[r], (*chip, c)) for r, chip in enumerate(chips)]
        for cp in sends:
            cp.start()
        for r, (px, py) in enumerate(chips):
            _remote(all_ref.at[mine], all_ref.at[2 * px + py], send_sems.at[r], recv_sems.at[r], (px, py, c)).wait_recv()
        for cp in sends:
            cp.wait_send()
        for k in range(N_CHIPS):
            pn_o[:, k * 256:(k + 1) * 256] = all_ref[k, 0:2, 0:256]
            ps_o[:, k * 512:(k + 1) * 512] = all_ref[k, 2:4, :]
            cw_o[0, :, k * 512:(k + 1) * 512] = all_ref[k, 4:7, :]
            mn_o[:, k * 256:(k + 1) * 256] = all_ref[k, 7:8, 0:256]
            qn_o[k] = all_ref[k, 8:9, 0:96]
            kn_o[k] = all_ref[k, 9:10, 0:64]

    sds = lambda *shape: jax.ShapeDtypeStruct(shape, F32)
    out = pl.pallas_call(
        body, name=name,
        out_shape=(sds(2, 1024), sds(2, 2048), sds(1, 3, 2048), sds(1, 1024), sds(N_CHIPS, 1, 96), sds(N_CHIPS, 1, 64)),
        in_specs=[VMEM_SPEC] * 6, out_specs=(VMEM_SPEC,) * 6,
        scratch_shapes=[pltpu.VMEM((N_CHIPS,) + SMALL_SLOT, F32), pltpu.SemaphoreType.DMA((3,)), pltpu.SemaphoreType.DMA((3,))],
    )(*[shards[n] for n, _, _ in SMALL_GATHER])
    full = dict(zip([n for n, _, _ in SMALL_GATHER], out))
    full["mla_q_norm"] = full["mla_q_norm"].reshape(1, Q_LORA)
    full["mla_kv_norm"] = full["mla_kv_norm"].reshape(1, KV_LORA)
    return full


SMALL_REDUCE = (("pool_norm_0", 0, 1, 1024), ("pool_norm_1", 1, 1, 1024), ("pool_scale_0", 2, 1, 2048), ("pool_scale_1", 3, 1, 2048),
                ("conv_norm", 4, 1, 1024), ("mla_norm", 5, 1, 1024), ("mla_q_norm", 6, 1, 384), ("mla_kv_norm", 7, 1, 256),
                ("conv_w", 8, 8, 2048), ("final_norm", 16, 1, 1024))
REDUCE_SLOT = (24, 2048)


def _reduce_small(parts, after, *, name):
    keys = [k for k, _, _, _ in SMALL_REDUCE]

    def body(*refs):
        ins = dict(zip(keys, refs[:len(keys)]))
        pn_o, ps_o, cn_o, cw_o, mn_o, qn_o, kn_o, fn_o, all_ref, send_sems, recv_sems = refs[len(keys) + 1:]
        x, y, c = _place()
        me = 4 * x + 2 * y + c
        all_ref[me] = jnp.zeros(REDUCE_SLOT, F32)
        for k, r0, nr, wd in SMALL_REDUCE:
            all_ref[me, r0:r0 + nr, 0:wd] = ins[k][...]
        peers = []
        for rel in range(1, N_DEV):
            dx, dy, dc = (rel >> 2) & 1, (rel >> 1) & 1, rel & 1
            peers.append((1 - x if dx else x, 1 - y if dy else y, 1 - c if dc else c))
        sends = [_remote(all_ref.at[me], all_ref.at[me], send_sems.at[k], recv_sems.at[k], peer) for k, peer in enumerate(peers)]
        for cp in sends:
            cp.start()
        for k, (px, py, pc) in enumerate(peers):
            _remote(all_ref.at[me], all_ref.at[4 * px + 2 * py + pc], send_sems.at[k], recv_sems.at[k], (px, py, pc)).wait_recv()
        for cp in sends:
            cp.wait_send()

        def total(r0, nr, wd):
            acc = all_ref[0, r0:r0 + nr, 0:wd]
            for d in range(1, N_DEV):
                acc = acc + all_ref[d, r0:r0 + nr, 0:wd]
            return acc

        pn_o[0:1, :] = total(0, 1, 1024)
        pn_o[1:2, :] = total(1, 1, 1024)
        ps_o[0:1, :] = total(2, 1, 2048)
        ps_o[1:2, :] = total(3, 1, 2048)
        cn_o[...] = total(4, 1, 1024)
        mn_o[...] = total(5, 1, 1024)
        qn_o[...] = total(6, 1, Q_LORA)
        kn_o[...] = total(7, 1, KV_LORA)
        cw_o[0] = total(8, 3, 2048)
        fn_o[...] = total(16, 1, 1024)

    sds = lambda *shape: jax.ShapeDtypeStruct(shape, F32)
    out = pl.pallas_call(
        body, name=name,
        out_shape=(sds(2, 1024), sds(2, 2048), sds(1, 1024), sds(1, 3, 2048), sds(1, 1024), sds(1, Q_LORA), sds(1, KV_LORA), sds(1, 1024)),
        in_specs=[VMEM_SPEC] * len(keys) + [ANY_SPEC], out_specs=(VMEM_SPEC,) * 8,
        scratch_shapes=[pltpu.VMEM((N_DEV,) + REDUCE_SLOT, F32), pltpu.SemaphoreType.DMA((N_DEV - 1,)), pltpu.SemaphoreType.DMA((N_DEV - 1,))],
    )(*[parts[k] for k in keys], after)
    return dict(zip(("pool_norm", "pool_scale", "conv_norm", "conv_w", "mla_norm", "mla_q_norm", "mla_kv_norm", "final_norm"), out))


def _adam_math(w, g, m, v):
    m_new = ADAM_B1 * m + (1.0 - ADAM_B1) * g
    v_new = ADAM_B2 * v + (1.0 - ADAM_B2) * (g * g)
    m_hat = m_new / (1.0 - ADAM_B1 ** ADAM_STEP)
    v_hat = v_new / (1.0 - ADAM_B2 ** ADAM_STEP)
    return -ADAM_LR * (m_hat / (jnp.sqrt(v_hat) + ADAM_EPS) + ADAM_WD * w), m_new, v_new


def _adamw_small(w, m, v, g_full, chip, *, name):
    shp = {n: w[n].shape for n in SMALL}
    whole = lambda s: pl.BlockSpec(s, lambda i, c: (0,) * len(s))
    g_in, g_specs = {}, {}
    for n in SMALL:
        if not SMALL_SHARDED[n]:
            g_in[n], g_specs[n] = g_full[n].reshape(shp[n]), whole(shp[n])
        elif shp[n][-1] % LANES:
            g_in[n] = g_full[n].reshape(N_CHIPS, 1, shp[n][-1])
            g_specs[n] = pl.BlockSpec((None,) + shp[n], lambda i, c: (c[0], 0, 0))
        else:
            g_in[n] = g_full[n]
            nd = len(shp[n])
            g_specs[n] = pl.BlockSpec(shp[n], lambda i, c, nd=nd: (0,) * (nd - 1) + (c[0],))

    def body(c_ref, *refs):
        k = len(SMALL)
        w_r, m_r, v_r, g_r = refs[0:k], refs[k:2 * k], refs[2 * k:3 * k], refs[3 * k:4 * k]
        go_r, d_r, nm_r, nv_r = refs[4 * k:5 * k], refs[5 * k:6 * k], refs[6 * k:7 * k], refs[7 * k:8 * k]
        for i in range(k):
            gv = g_r[i][...]
            d, m_new, v_new = _adam_math(w_r[i][...], gv, m_r[i][...], v_r[i][...])
            go_r[i][...] = gv
            d_r[i][...] = d
            nm_r[i][...] = m_new
            nv_r[i][...] = v_new

    nat = [whole(shp[n]) for n in SMALL]
    out_sds = tuple(jax.ShapeDtypeStruct(shp[n], F32) for n in SMALL)
    out = pl.pallas_call(
        body, name=name, out_shape=out_sds * 4,
        grid_spec=pltpu.PrefetchScalarGridSpec(
            num_scalar_prefetch=1, grid=(1,),
            in_specs=nat * 3 + [g_specs[n] for n in SMALL], out_specs=tuple(nat) * 4),
        compiler_params=_cparams("arbitrary"),
    )(chip, *[w[n] for n in SMALL], *[m[n] for n in SMALL], *[v[n] for n in SMALL], *[g_in[n] for n in SMALL])
    k = len(SMALL)
    return tuple(dict(zip(SMALL, out[j * k:(j + 1) * k])) for j in range(4))


BIG = ("pool_w_in", "pool_w_grp", "pool_w_out", "conv_w_in", "conv_w_out", "mla_w_in", "mla_w_q_up", "mla_w_kv_up", "mla_w_out")
BIG_SHARD_AXIS = {"pool_w_in": 2, "pool_w_grp": 2, "pool_w_out": 1, "conv_w_in": 2, "conv_w_out": 1,
                  "mla_w_in": 2, "mla_w_q_up": 2, "mla_w_kv_up": 2, "mla_w_out": 1}
GATHER_LAYOUT = {
    "p0": ((("pool_w_in", 0), 0, "cols"), (("pool_w_out", 0), 1024, "rows"), (("pool_w_grp", 0), 1536, "flat")),
    "cv": ((("conv_w_in", 0), 0, "cols"), (("conv_w_out", 0), 2048, "rows")),
    "ml": ((("pool_w_in", 1), 0, "cols"), (("mla_w_out", 0), 1024, "rows"), (("pool_w_out", 1), 1536, "rows"),
           (("mla_w_kv_up", 0), 2048, "cols"), (("pool_w_grp", 1), 2304, "flat"), (("mla_w_q_up", 0), 2560, "flat"),
           (("mla_w_in", 0), 2848, "flat")),
}
REDUCE_LAYOUT = {
    "late": ((("conv_w_in", 0), 0, "cols"), (("pool_w_in", 1), 2048, "cols"), (("conv_w_out", 0), 3072, "rows"),
             (("mla_w_out", 0), 3584, "rows"), (("pool_w_out", 1), 4096, "rows"), (("mla_w_kv_up", 0), 4608, "cols"),
             (("pool_w_grp", 1), 4864, "flat"), (("mla_w_q_up", 0), 5120, "flat"), (("mla_w_in", 0), 5408, "flat")),
    "first": ((("pool_w_in", 0), 0, "cols"), (("pool_w_out", 0), 1024, "rows"), (("pool_w_grp", 0), 1536, "flat")),
}
PACK_ROW_ALIGN = 32
RS_ROW_ALIGN = 512


def _slot_rows(layout, shard_shape, align):
    where, end = {}, 0
    for piece, r0, kind in layout:
        n = 1
        for d in shard_shape(piece):
            n *= d
        assert r0 >= end and n % PACK_COLS == 0, (piece, r0, end)
        where[piece] = (r0, n // PACK_COLS, kind)
        end = r0 + n // PACK_COLS
    return end + (-end) % align, where


def _as_slot_rows(shard, kind):
    if kind == "cols":
        k, n = shard.shape
        return shard.reshape(k, n // PACK_COLS, PACK_COLS).swapaxes(0, 1).reshape(-1, PACK_COLS)
    return shard.reshape(-1, PACK_COLS)


def _pack_slot(shards, layout, rows, dtype):
    parts, end = [], 0
    for piece, r0, kind in layout:
        if r0 > end:
            parts.append(jnp.zeros((r0 - end, PACK_COLS), dtype))
        parts.append(_as_slot_rows(shards[piece], kind).astype(dtype))
        end = r0 + parts[-1].shape[0]
    if rows > end:
        parts.append(jnp.zeros((rows - end, PACK_COLS), dtype))
    return jnp.concatenate(parts, axis=0)


SMALL = ("pool_norm", "pool_scale", "conv_norm", "conv_w", "mla_norm", "mla_q_norm", "mla_kv_norm", "final_norm")
SMALL_SHARDED = {"pool_norm": True, "pool_scale": True, "conv_norm": False, "conv_w": True, "mla_norm": True,
                 "mla_q_norm": True, "mla_kv_norm": True, "final_norm": False}


def _rope_tables(positions):
    inv_freq = ROPE_BASE ** (-jnp.arange(0, QK_ROPE, 2, dtype=F32) / QK_ROPE)
    ang = positions.astype(F32).reshape(-1, 1) * inv_freq
    cos, sin = jnp.cos(ang), jnp.sin(ang)
    z32 = jnp.zeros_like(cos)
    z64 = jnp.concatenate([z32, z32], axis=1)
    return (jnp.concatenate([cos, cos, z64], axis=1), jnp.concatenate([-sin, z32, z64], axis=1),
            jnp.concatenate([z32, sin, z64], axis=1))


def _mla_in_to_padded(w):
    q, kv, kr, z = w[:, :Q_LORA], w[:, Q_LORA:Q_LORA + KV_LORA], w[:, Q_LORA + KV_LORA:Q_LORA + KV_LORA + QK_ROPE], w[:, Q_LORA + KV_LORA + QK_ROPE:]
    return jnp.concatenate([z, kv, q, kr, jnp.zeros((w.shape[0], MLA_IN_PAD - MLA_IN), w.dtype)], axis=1)


def _mla_in_from_padded(w):
    z, kv, q, kr = w[:, :D_INNER], w[:, D_INNER:D_INNER + KV_LORA], w[:, D_INNER + KV_LORA:D_INNER + KV_LORA + Q_LORA], w[:, D_INNER + KV_LORA + Q_LORA:D_INNER + KV_LORA + Q_LORA + QK_ROPE]
    return jnp.concatenate([q, kv, kr, z], axis=1)


def _q_up_to_padded(w):
    k = w.shape[0]
    return jnp.pad(w.reshape(k, N_HEADS, QK_NOPE + QK_ROPE), ((0, 0), (0, 0), (0, HEAD_PAD - QK_NOPE - QK_ROPE))).reshape(k, N_HEADS * HEAD_PAD)


def _q_up_from_padded(w):
    k = w.shape[0]
    return w.reshape(k, N_HEADS, HEAD_PAD)[:, :, :QK_NOPE + QK_ROPE].reshape(k, N_HEADS * (QK_NOPE + QK_ROPE))


def _local_step(x, positions, target, weights_for, ws, sink):
    S = x.shape[0]
    tm = min(512, S)
    te = min(256, S)
    tq = min(512, S)
    tabs = _rope_tables(positions)
    gs = {}

    def mm_in(xn, w, name):
        n = w.shape[1]
        tn = PACK_COLS if isinstance(w, Packed) else _pick(n, 1536 if n == MLA_IN_PAD else 1024)
        return _mm(xn, w, tm=min(1024, S), tn=tn, tk=D_MODEL, name=name)

    def mm_out(y, w, res, name):
        return _mm(y, w, residual=res, tm=tm, tn=D_MODEL, tk=D_INNER, name=name)

    def mm_dx(dy, w, name, after=None):
        k, n = w.shape
        if isinstance(w, Packed):
            tn, tk = (k if w.kind == "rows" else min(k, 1024)), PACK_COLS
        else:
            tn, tk = _pick(k, 1024), _pick(n, 1408)
        return _mm(dy, w, trans_b=True, after=after, tm=min(1024, S), tn=tn, tk=tk, name=name)

    def mm_dw(piece, a, b, name, after=None, post=None):
        ka, nb = a.shape[1], b.shape[1]
        into = sink.dest(piece)
        tokens = min(1024, S)
        if into is None:
            out = _mm(a, b, trans_a=True, out_dtype=BF16, after=after, tm=_pick(ka, 1024), tn=_pick(nb, 1408), tk=tokens, name=name)
            sink.put(piece, out if post is None else post(out))
        else:
            rows = ka if into.kind == "rows" else min(ka, 1024)
            sink.put(piece, _mm(a, b, trans_a=True, after=after, into=into, tm=rows, tn=PACK_COLS, tk=tokens, name=name))

    def pool_layer_fwd(xin, xn, wts, j, tag):
        h = mm_in(xn, wts[("pool_w_in", j)], f"{tag}_in")
        y = _pool_fwd(h, wts[("pool_w_grp", j)], ws["pool_scale"][j:j + 1], tm=te, name=f"{tag}_mix")
        xo = mm_out(y, wts[("pool_w_out", j)], xin, f"{tag}_out")
        return xo, (xin, xn, h, y)

    def pool_layer_bwd(dx, dxb, saved, wts, j, tag, after=None):
        xin, xn, h, y = saved
        dy = mm_dx(dxb, wts[("pool_w_out", j)], f"{tag}_dy", after)
        mm_dw(("pool_w_out", j), y, dxb, f"{tag}_dwo", after)
        pooled, dmixed, dpooled, dz, dsc = _pool_bwd1(h, dy, wts[("pool_w_grp", j)], ws["pool_scale"][j:j + 1], tm=te, name=f"{tag}_bmix")
        sink.put(("pool_w_grp", j), _grouped_tn(pooled, dmixed, tk=tm, name=f"{tag}_dwg"))
        dh = _pool_bwd2(dpooled, dz, tm=te, name=f"{tag}_bshift")
        dxn = mm_dx(dh, wts[("pool_w_in", j)], f"{tag}_dxn")
        mm_dw(("pool_w_in", j), xn, dh, f"{tag}_dwi")
        dxo, dxob, dg = _rms_bwd(xin, ws["pool_norm"][j:j + 1], dxn, dx, tm=tm, name=f"{tag}_bnorm")
        gs[f"pool_norm_{j}"], gs[f"pool_scale_{j}"] = dg, dsc
        return dxo, dxob

    xn0 = _rms_fwd(x, ws["pool_norm"][0:1], tm=tm, name="p0_norm")
    w_p0 = weights_for("p0", xn0)
    x1, sv0 = pool_layer_fwd(x, xn0, w_p0, 0, "p0")

    w_cv = weights_for("cv", x1)
    xn1 = _rms_fwd(x1, ws["conv_norm"][0:1], tm=tm, name="cv_norm")
    h1 = mm_in(xn1, w_cv[("conv_w_in", 0)], "cv_in")
    cw = jnp.pad(ws["conv_w"][0], ((0, 5), (0, 0)))
    y1 = _conv_fwd(h1, cw, tm=te, name="cv_mix")
    x2 = mm_out(y1, w_cv[("conv_w_out", 0)], x1, "cv_out")

    w_ml = weights_for("ml", x2)
    w_mi = _mla_in_to_padded(w_ml[("mla_w_in", 0)])
    w_q = _q_up_to_padded(w_ml[("mla_w_q_up", 0)])
    w_kv = w_ml[("mla_w_kv_up", 0)]
    qg, kvg = ws["mla_q_norm"][0:1], ws["mla_kv_norm"][0:1]
    xn2 = _rms_fwd(x2, ws["mla_norm"][0:1], tm=tm, name="ml_norm")
    h2 = mm_in(xn2, w_mi, "ml_in")
    q_n, kv_n, krr = _mla_latent_fwd(h2, qg, kvg, tabs, tm=tm, name="ml_lat")
    q_full = _mla_q_up(q_n, w_q, tabs, tm=tm, name="ml_qup")
    k_full, v = _mla_kv_up(kv_n, w_kv, krr, tm=tm, name="ml_kvup")
    o, y2, lse = _flash_fwd(q_full, k_full, v, h2, tq=tq, name="ml_attn")
    x3 = mm_out(y2, w_ml[("mla_w_out", 0)], x2, "ml_out")

    x4, sv3 = pool_layer_fwd(x3, _rms_fwd(x3, ws["pool_norm"][1:2], tm=tm, name="p1_norm"), w_ml, 1, "p1")

    loss_part, dx, dxb, dgf = _final_loss(x4, ws["final_norm"].reshape(1, -1), target, tm=tm, name="final")
    gs["final_norm"] = dgf

    dx, dxb = pool_layer_bwd(dx, dxb, sv3, w_ml, 1, "p1")

    dy = mm_dx(dxb, w_ml[("mla_w_out", 0)], "ml_dy")
    mm_dw(("mla_w_out", 0), y2, dxb, "ml_dwo")
    do, dz, delta = _mla_gate_bwd(dy, o, h2, tm=tq, name="ml_bgate")
    dq_pre, dkv, dkr = _flash_bwd(q_full, k_full, v, do, lse, delta, tabs, tq=tq, name="ml_battn")
    dq_n = mm_dx(dq_pre, w_q, "ml_dqn")
    mm_dw(("mla_w_q_up", 0), q_n, dq_pre, "ml_dwq", post=_q_up_from_padded)
    dkv_n = mm_dx(dkv, w_kv, "ml_dkvn")
    mm_dw(("mla_w_kv_up", 0), kv_n, dkv, "ml_dwkv")
    dh2, dqg, dkvg = _mla_latent_bwd(h2, dq_n, dkv_n, dkr, dz, qg, kvg, tabs, tm=te, name="ml_blat")
    dxn2 = mm_dx(dh2, w_mi, "ml_dxn")
    mm_dw(("mla_w_in", 0), xn2, dh2, "ml_dwi", post=_mla_in_from_padded)
    dx, dxb, dg2 = _rms_bwd(x2, ws["mla_norm"][0:1], dxn2, dx, tm=tm, name="ml_bnorm")
    gs["mla_norm"], gs["mla_q_norm"], gs["mla_kv_norm"] = dg2, dqg, dkvg

    dy = mm_dx(dxb, w_cv[("conv_w_out", 0)], "cv_dy")
    mm_dw(("conv_w_out", 0), y1, dxb, "cv_dwo")
    dh1, dcw = _conv_bwd(h1, dy, cw, tm=te, name="cv_bmix")
    dxn1 = mm_dx(dh1, w_cv[("conv_w_in", 0)], "cv_dxn")
    mm_dw(("conv_w_in", 0), xn1, dh1, "cv_dwi")
    dx, dxb, dg1 = _rms_bwd(x1, ws["conv_norm"][0:1], dxn1, dx, tm=tm, name="cv_bnorm")
    gs["conv_norm"], gs["conv_w"] = dg1, dcw

    dx, dxb = pool_layer_bwd(dx, dxb, sv0, w_p0, 0, "p0", after=sink.late_ready())
    return loss_part, dx, gs


def kernel(x, positions, pool_norm, pool_w_in, pool_w_grp, pool_scale, pool_w_out, conv_norm, conv_w_in, conv_w, conv_w_out, mla_norm, mla_w_in, mla_q_norm, mla_w_q_up, mla_kv_norm, mla_w_kv_up, mla_w_out, final_norm, loss_target, m_pool_norm, m_pool_w_in, m_pool_w_grp, m_pool_scale, m_pool_w_out, m_conv_norm, m_conv_w_in, m_conv_w, m_conv_w_out, m_mla_norm, m_mla_w_in, m_mla_q_norm, m_mla_w_q_up, m_mla_kv_norm, m_mla_w_kv_up, m_mla_w_out, m_final_norm, v_pool_norm, v_pool_w_in, v_pool_w_grp, v_pool_scale, v_pool_w_out, v_conv_norm, v_conv_w_in, v_conv_w, v_conv_w_out, v_mla_norm, v_mla_w_in, v_mla_q_norm, v_mla_w_q_up, v_mla_kv_norm, v_mla_w_kv_up, v_mla_w_out, v_final_norm):
    names = ("pool_norm", "pool_w_in", "pool_w_grp", "pool_scale", "pool_w_out", "conv_norm", "conv_w_in", "conv_w", "conv_w_out",
             "mla_norm", "mla_w_in", "mla_q_norm", "mla_w_q_up", "mla_kv_norm", "mla_w_kv_up", "mla_w_out", "final_norm")
    w = dict(zip(names, (pool_norm, pool_w_in, pool_w_grp, pool_scale, pool_w_out, conv_norm, conv_w_in, conv_w, conv_w_out,
                         mla_norm, mla_w_in, mla_q_norm, mla_w_q_up, mla_kv_norm, mla_w_kv_up, mla_w_out, final_norm)))
    m = dict(zip(names, (m_pool_norm, m_pool_w_in, m_pool_w_grp, m_pool_scale, m_pool_w_out, m_conv_norm, m_conv_w_in, m_conv_w, m_conv_w_out,
                         m_mla_norm, m_mla_w_in, m_mla_q_norm, m_mla_w_q_up, m_mla_kv_norm, m_mla_w_kv_up, m_mla_w_out, m_final_norm)))
    v = dict(zip(names, (v_pool_norm, v_pool_w_in, v_pool_w_grp, v_pool_scale, v_pool_w_out, v_conv_norm, v_conv_w_in, v_conv_w, v_conv_w_out,
                         v_mla_norm, v_mla_w_in, v_mla_q_norm, v_mla_w_q_up, v_mla_kv_norm, v_mla_w_kv_up, v_mla_w_out, v_final_norm)))
    chip = 2 * lax.axis_index("x") + lax.axis_index("y")
    core = lax.axis_index("c")

    core1 = core.astype(jnp.int32).reshape(1)
    chip_core = jnp.stack([chip, core]).astype(jnp.int32)
    shard_shape = lambda piece: w[piece[0]].shape[1:]
    shard_axis = lambda piece: BIG_SHARD_AXIS[piece[0]] - 1
    full_shape = lambda piece: tuple(d * (N_CHIPS if a == shard_axis(piece) else 1) for a, d in enumerate(shard_shape(piece)))

    gather_rows, gather_at, packs = {}, {}, {}
    for grp, layout in GATHER_LAYOUT.items():
        gather_rows[grp], gather_at[grp] = _slot_rows(layout, shard_shape, PACK_ROW_ALIGN)
        packs[grp] = _pack_slot({(n, j): w[n][j] for (n, j), _, _ in layout}, layout, gather_rows[grp], BF16)

    def gathered_weights(grp, gathered):
        out = {}
        for piece, (r0, n, kind) in gather_at[grp].items():
            if kind == "flat":
                out[piece] = jnp.concatenate([gathered[k, r0:r0 + n].reshape(shard_shape(piece)) for k in range(N_CHIPS)], axis=shard_axis(piece))
            else:
                out[piece] = Packed(gathered, r0, kind, full_shape(piece))
        return out

    p0_start = _ag_ici_start(packs["p0"], w["final_norm"], name="ag_p0_start")
    cv_start = _ag_ici_start(packs["cv"], p0_start[4], name="ag_cv_start")
    ml_start = _ag_ici_start(packs["ml"], cv_start[4], name="ag_ml_start")
    in_flight = {"p0": p0_start, "cv": cv_start, "ml": ml_start}

    def weights_for(grp, after):
        if grp == "p0":
            after = (after, ml_start[4])
        send_sems, recv_sems, w_thru, land, _ = in_flight[grp]
        w_thru, land = _ag_ici_wait(send_sems, recv_sems, w_thru, land, after, name=f"ag_{grp}_wait")
        return gathered_weights(grp, _ag_sibling_forward(land, w_thru, name=f"ag_{grp}_fwd"))

    ws = {"conv_norm": w["conv_norm"], "final_norm": w["final_norm"]}
    ws.update(_gather_small({n: w[n] for n, _, _ in SMALL_GATHER}, name="ag_small"))

    reduce_rows, reduce_at = {}, {}
    for grp, layout in REDUCE_LAYOUT.items():
        reduce_rows[grp], reduce_at[grp] = _slot_rows(layout, shard_shape, RS_ROW_ALIGN)
    group_of = {piece: grp for grp, layout in REDUCE_LAYOUT.items() for piece, _, _ in layout}

    class Sink:
        def __init__(self):
            self.buf = {grp: lax.empty((N_CHIPS, rows, PACK_COLS), BF16) for grp, rows in reduce_rows.items()}
            self.started = {}

        def dest(self, piece):
            grp = group_of[piece]
            r0, _, kind = reduce_at[grp][piece]
            return None if kind == "flat" else Packed(self.buf[grp], r0, kind, full_shape(piece))

        def put(self, piece, result):
            grp = group_of[piece]
            r0, n, kind = reduce_at[grp][piece]
            if kind == "flat":
                parts = jnp.split(result, N_CHIPS, axis=shard_axis(piece))
                result = lax.dynamic_update_slice(self.buf[grp], jnp.stack([p.reshape(n, PACK_COLS) for p in parts]), (0, r0, 0))
            self.buf[grp] = result

        def start(self, grp, tag):
            theirs = _rs_sibling_swap(self.buf[grp], name=f"{tag}_swap")
            chip_sum = _add2_bf16(self.buf[grp], theirs, core1, name=f"{tag}_add2")
            self.started[grp] = _rs_chip_exchange_start(chip_sum, name=f"{tag}_chips_start")
            return self.started[grp][4]

        def finish(self, grp, after, tag):
            send_sems, recv_sems, chip_sum, land, _ = self.started[grp]
            chip_sum, recv = _rs_chip_exchange_wait(send_sems, recv_sems, chip_sum, land, after, name=f"{tag}_chips_wait")
            half_sum = _add4_f32(chip_sum, recv, chip_core, name=f"{tag}_add4")
            return _rs_sibling_join(half_sum, name=f"{tag}_join")

        def late_ready(self):
            return self.start("late", "rsa")

    sink = Sink()

    loss_part, grad_x, gs = _local_step(x[0], positions, loss_target[0], weights_for, ws, sink)
    loss = lax.psum(loss_part[0, 0], ("x", "y", "c"))

    g, delta, new_m, new_v = {}, {}, {}, {}

    def adam_big(n):
        nj = w[n].shape[0]
        where = [(group_of[(n, j)],) + reduce_at[group_of[(n, j)]][(n, j)] for j in range(nj)]
        if where[0][3] == "flat":
            g[n] = jnp.stack([g_rows[grp][r0:r0 + rows].reshape(w[n].shape[1:]) for grp, r0, rows, _ in where])
            shp = w[n].shape
            two_d = lambda a: a.reshape(-1, shp[-1])
            d_, m_, v_ = _adamw(two_d(w[n]), two_d(g[n]), two_d(m[n]), two_d(v[n]), name=f"adamw_{n}")
            delta[n], new_m[n], new_v[n] = d_.reshape(shp), m_.reshape(shp), v_.reshape(shp)
        else:
            g[n], delta[n], new_m[n], new_v[n] = _adamw_rows(w[n], m[n], v[n], [(g_rows[grp], r0) for grp, r0, _, _ in where], name=f"adamw_{n}")

    first_token = sink.start("first", "rsb")
    g_rows = {"late": sink.finish("late", first_token, "rsa")}
    late_only = [n for n in BIG if all(group_of[(n, j)] == "late" for j in range(w[n].shape[0]))]
    for n in late_only:
        adam_big(n)
    g_rows["first"] = sink.finish("first", tuple(delta[n] for n in late_only), "rsb")
    for n in BIG:
        if n not in late_only:
            adam_big(n)

    gs_sum = _reduce_small(gs, g_rows["first"], name="ar_small")
    row = lambda d: {n: (d[n].reshape(1, -1) if d[n].ndim == 1 else d[n]) for n in SMALL}
    small_out = _adamw_small(row(w), row(m), row(v), gs_sum, chip.astype(jnp.int32).reshape(1), name="adamw_small")
    for dst, res in zip((g, delta, new_m, new_v), small_out):
        for n in SMALL:
            dst[n] = res[n].reshape(w[n].shape)

    return (loss, grad_x[None], *[g[n] for n in names], *[delta[n] for n in names],
            *[new_m[n] for n in names], *[new_v[n] for n in names])
```

```python
import functools

import jax
import jax.numpy as jnp
from jax import lax
from jax.experimental import pallas as pl
from jax.experimental.pallas import tpu as pltpu

F32 = jnp.float32
BF16 = jnp.bfloat16

D_MODEL = 1024
D_INNER = 2048
POOL_WINDOWS = (2, 4, 8, 16)
POOL_GROUP = 512
N_HEADS = 16
QK_NOPE = 128
QK_ROPE = 64
V_DIM = 128
HEAD_PAD = 256
Q_LORA = 384
KV_LORA = 256
MLA_IN = Q_LORA + KV_LORA + QK_ROPE + D_INNER
MLA_IN_PAD = 2816
ATTN_SCALE = (QK_NOPE + QK_ROPE) ** -0.5
ROPE_BASE = 10000.0
NORM_EPS = 1e-6
HALO = 16

ADAM_LR = 0.001
ADAM_B1 = 0.9
ADAM_B2 = 0.999
ADAM_EPS = 1e-08
ADAM_WD = 0.01
ADAM_STEP = 10

N_CHIPS = 4
N_DEV = 8
LANES = 128
PACK_COLS = 1024
V7X_VMEM_LIMIT = 56 * 1024 * 1024
MESH = pl.DeviceIdType.MESH


def _cparams(*sem):
    return pltpu.CompilerParams(dimension_semantics=sem, vmem_limit_bytes=V7X_VMEM_LIMIT)


def _pick(n, cap):
    best = None
    for d in range(LANES, min(n, cap) + 1, LANES):
        if n % d == 0:
            best = d
    assert best is not None, (n, cap)
    return best


def _sigmoid(z):
    return 1.0 / (1.0 + jnp.exp(-z))


class Packed:
    def __init__(self, buf, r0, kind, shape):
        self.buf, self.r0, self.kind, self.shape = buf, r0, kind, shape

    def block(self, rb, cb, bk):
        assert self.r0 % bk == 0, (self.r0, bk)
        if self.kind == "cols":
            K = self.shape[0]
            per = self.shape[1] // (N_CHIPS * PACK_COLS)
            assert K % bk == 0
            return cb // per, (self.r0 + (cb % per) * K) // bk + rb
        kk = self.shape[0] // N_CHIPS
        assert kk % bk == 0
        per = kk // bk
        return rb // per, self.r0 // bk + rb % per


def _mm(a, b, *, trans_a=False, trans_b=False, out_dtype=F32, residual=None, after=None, into=None, tm, tn, tk, name):
    if trans_a:
        K, M = a.shape
    else:
        M, K = a.shape
    if trans_b:
        N, K2 = b.shape
    else:
        K2, N = b.shape
    assert K == K2 and M % tm == 0 and N % tn == 0 and K % tk == 0, (name, a.shape, b.shape, tm, tn, tk)
    nk = K // tk
    dn = (((0 if trans_a else 1,), (1 if trans_b else 0,)), ((), ()))
    has_res = residual is not None
    n_skip = (after is not None) + (into is not None)
    b_all = isinstance(b, Packed) and b.kind == "rows"
    o_all = into is not None and into.kind == "rows"
    assert not b_all or (not trans_a and (tn == N if trans_b else tk == K)), name
    assert not o_all or tm == M, name

    def body(*refs):
        if has_res:
            a_ref, b_ref, r_ref = refs[:3]
            refs = refs[3:]
        else:
            a_ref, b_ref = refs[:2]
            r_ref = None
            refs = refs[2:]
        refs = refs[n_skip:]
        o_ref, rest = refs[0], refs[1:]
        if b_all and trans_b:
            part = jnp.concatenate([lax.dot_general(a_ref[...], b_ref[c], dn, preferred_element_type=F32) for c in range(N_CHIPS)], axis=1)
        elif b_all:
            kk = K // N_CHIPS
            part = lax.dot_general(a_ref[:, 0:kk], b_ref[0], dn, preferred_element_type=F32)
            for c in range(1, N_CHIPS):
                part = part + lax.dot_general(a_ref[:, c * kk:(c + 1) * kk], b_ref[c], dn, preferred_element_type=F32)
        else:
            part = lax.dot_general(a_ref[...], b_ref[...], dn, preferred_element_type=F32)

        def finish(acc):
            if has_res:
                acc = acc + r_ref[...]
            if o_all:
                mk = M // N_CHIPS
                for c in range(N_CHIPS):
                    o_ref[c] = acc[c * mk:(c + 1) * mk].astype(o_ref.dtype)
            else:
                o_ref[...] = acc.astype(o_ref.dtype)

        if nk == 1:
            finish(part)
        else:
            acc_ref = rest[0]
            k = pl.program_id(2)

            @pl.when(k == 0)
            def _():
                acc_ref[...] = part

            @pl.when(k > 0)
            def _():
                acc_ref[...] += part

            @pl.when(k == nk - 1)
            def _():
                finish(acc_ref[...])

    a_spec = pl.BlockSpec((tk, tm), lambda i, j, k: (k, i)) if trans_a else pl.BlockSpec((tm, tk), lambda i, j, k: (i, k))
    if b_all:
        kkb = b.shape[0] // N_CHIPS
        b_spec = pl.BlockSpec((N_CHIPS, kkb, PACK_COLS), lambda i, j, k: (0, b.r0 // kkb, 0))
        b_arg = b.buf
    elif isinstance(b, Packed):
        if trans_b:
            assert tk == PACK_COLS
            b_spec = pl.BlockSpec((None, tn, tk), lambda i, j, k: (*b.block(j, k, tn), 0))
        else:
            assert tn == PACK_COLS
            b_spec = pl.BlockSpec((None, tk, tn), lambda i, j, k: (*b.block(k, j, tk), 0))
        b_arg = b.buf
    else:
        b_spec = pl.BlockSpec((tn, tk), lambda i, j, k: (j, k)) if trans_b else pl.BlockSpec((tk, tn), lambda i, j, k: (k, j))
        b_arg = b
    o_spec = pl.BlockSpec((tm, tn), lambda i, j, k: (i, j))
    in_specs = [a_spec, b_spec] + ([o_spec] if has_res else [])
    args = (a, b_arg) + ((residual,) if has_res else ())
    aliases = {}
    if after is not None:
        in_specs.append(pl.BlockSpec(memory_space=pl.ANY))
        args += (after,)
    if into is None:
        out_shape, out_spec = jax.ShapeDtypeStruct((M, N), out_dtype), o_spec
    else:
        assert tn == PACK_COLS and into.shape == (M, N)
        in_specs.append(pl.BlockSpec(memory_space=pl.ANY))
        aliases = {len(args): 0}
        args += (into.buf,)
        out_shape = jax.ShapeDtypeStruct(into.buf.shape, into.buf.dtype)
        if o_all:
            out_spec = pl.BlockSpec((N_CHIPS, M // N_CHIPS, tn), lambda i, j, k: (0, into.r0 // (M // N_CHIPS), 0))
        else:
            out_spec = pl.BlockSpec((None, tm, tn), lambda i, j, k: (*into.block(i, j, tm), 0))
    return pl.pallas_call(
        body, name=name, out_shape=out_shape,
        grid=(M // tm, N // tn, nk),
        in_specs=in_specs, out_specs=out_spec, input_output_aliases=aliases,
        scratch_shapes=[pltpu.VMEM((tm, tn), F32)] if nk > 1 else [],
        compiler_params=_cparams("parallel", "parallel", "arbitrary"),
    )(*args)


def _grouped_tn(a, b, *, tk, name):
    S = a.shape[0]
    G = POOL_GROUP
    nk = S // tk

    def body(a_ref, b_ref, o_ref, acc_ref):
        k = pl.program_id(1)
        part = lax.dot_general(a_ref[...], b_ref[...], (((0,), (0,)), ((), ())), preferred_element_type=F32)

        @pl.when(k == 0)
        def _():
            acc_ref[...] = part

        @pl.when(k > 0)
        def _():
            acc_ref[...] += part

        @pl.when(k == nk - 1)
        def _():
            o_ref[...] = acc_ref[...].astype(o_ref.dtype)

    return pl.pallas_call(
        body, name=name,
        out_shape=jax.ShapeDtypeStruct((len(POOL_WINDOWS), G, G), BF16),
        grid=(len(POOL_WINDOWS), nk),
        in_specs=[pl.BlockSpec((tk, G), lambda g, k: (k, g)), pl.BlockSpec((tk, G), lambda g, k: (k, g))],
        out_specs=pl.BlockSpec((None, G, G), lambda g, k: (g, 0, 0)),
        scratch_shapes=[pltpu.VMEM((G, G), F32)],
        compiler_params=_cparams("parallel", "arbitrary"),
    )(a, b)


def _rms_fwd(x, g, *, tm, name):
    S, D = x.shape

    def body(x_ref, g_ref, o_ref):
        xv = x_ref[...]
        rstd = lax.rsqrt(jnp.mean(xv * xv, axis=-1, keepdims=True) + NORM_EPS)
        o_ref[...] = (xv * rstd * g_ref[...]).astype(o_ref.dtype)

    return pl.pallas_call(
        body, name=name,
        out_shape=jax.ShapeDtypeStruct((S, D), BF16),
        grid=(S // tm,),
        in_specs=[pl.BlockSpec((tm, D), lambda i: (i, 0)), pl.BlockSpec((1, D), lambda i: (0, 0))],
        out_specs=pl.BlockSpec((tm, D), lambda i: (i, 0)),
        compiler_params=_cparams("parallel"),
    )(x, g)


def _rms_bwd_math(xv, gv, dxn):
    rstd = lax.rsqrt(jnp.mean(xv * xv, axis=-1, keepdims=True) + NORM_EPS)
    xh = xv * rstd
    dg = jnp.sum(dxn * xh, axis=0, keepdims=True)
    dxh = dxn * gv
    dx = rstd * (dxh - xh * jnp.mean(dxh * xh, axis=-1, keepdims=True))
    return dx, dg


def _rms_bwd(x, g, dxn, dres, *, tm, name):
    S, D = x.shape

    def body(x_ref, g_ref, dxn_ref, dres_ref, dx_ref, dxb_ref, dg_ref):
        dx, dg = _rms_bwd_math(x_ref[...], g_ref[...], dxn_ref[...])
        dx = dx + dres_ref[...]
        dx_ref[...] = dx
        dxb_ref[...] = dx.astype(BF16)

        @pl.when(pl.program_id(0) == 0)
        def _():
            dg_ref[...] = dg

        @pl.when(pl.program_id(0) > 0)
        def _():
            dg_ref[...] += dg

    row = pl.BlockSpec((tm, D), lambda i: (i, 0))
    vec = pl.BlockSpec((1, D), lambda i: (0, 0))
    return pl.pallas_call(
        body, name=name,
        out_shape=(jax.ShapeDtypeStruct((S, D), F32), jax.ShapeDtypeStruct((S, D), BF16), jax.ShapeDtypeStruct((1, D), F32)),
        grid=(S // tm,),
        in_specs=[row, vec, row, row],
        out_specs=(row, row, vec),
        compiler_params=_cparams("arbitrary"),
    )(x, g, dxn, dres)


def _final_loss(x, g, target, *, tm, name):
    S, D = x.shape

    def body(x_ref, g_ref, t_ref, loss_ref, dx_ref, dxb_ref, dg_ref):
        xv = x_ref[...]
        gv = g_ref[...]
        rstd = lax.rsqrt(jnp.mean(xv * xv, axis=-1, keepdims=True) + NORM_EPS)
        xh = xv * rstd
        err = xh * gv - t_ref[...]
        part = 0.5 * jnp.sum(jnp.mean(err * err, axis=-1, keepdims=True), axis=0, keepdims=True)
        dy = err * (1.0 / D)
        dg = jnp.sum(dy * xh, axis=0, keepdims=True)
        dxh = dy * gv
        dx = rstd * (dxh - xh * jnp.mean(dxh * xh, axis=-1, keepdims=True))
        dx_ref[...] = dx
        dxb_ref[...] = dx.astype(BF16)
        lossb = jnp.broadcast_to(part, loss_ref.shape)

        @pl.when(pl.program_id(0) == 0)
        def _():
            dg_ref[...] = dg
            loss_ref[...] = lossb

        @pl.when(pl.program_id(0) > 0)
        def _():
            dg_ref[...] += dg
            loss_ref[...] += lossb

    row = pl.BlockSpec((tm, D), lambda i: (i, 0))
    vec = pl.BlockSpec((1, D), lambda i: (0, 0))
    lspec = pl.BlockSpec((8, LANES), lambda i: (0, 0))
    return pl.pallas_call(
        body, name=name,
        out_shape=(jax.ShapeDtypeStruct((8, LANES), F32), jax.ShapeDtypeStruct((S, D), F32),
                   jax.ShapeDtypeStruct((S, D), BF16), jax.ShapeDtypeStruct((1, D), F32)),
        grid=(S // tm,),
        in_specs=[row, vec, row],
        out_specs=(lspec, row, row, vec),
        compiler_params=_cparams("arbitrary"),
    )(x, g, target)


def _prev_halo_spec(tm, width, col):
    r = tm // HALO
    return pl.BlockSpec((HALO, width), lambda i: (jnp.maximum(i * r - 1, 0), col))


def _next_halo_spec(tm, width, col, S):
    r = tm // HALO
    last = S // HALO - 1
    return pl.BlockSpec((HALO, width), lambda i: (jnp.minimum((i + 1) * r, last), col))


def _shift_down(ext, k):
    return pltpu.roll(ext, k, 0)[HALO:, :]


def _shift_up(ext, k, tm):
    n = ext.shape[0]
    return pltpu.roll(ext, n - k, 0)[:tm, :]


def _pool_window_sum(ext, w):
    s = ext
    k = 1
    while k < w:
        s = s + pltpu.roll(s, k, 0)
        k *= 2
    return s[HALO:, :]


def _pooled_group(u_ref, halo, g, w, t_idx):
    cs = slice(g * POOL_GROUP, (g + 1) * POOL_GROUP)
    u = u_ref[:, cs]
    ext = jnp.concatenate([halo[:, cs], u], axis=0)
    inv = 1.0 / jnp.minimum(t_idx + 1, w).astype(F32)
    return _pool_window_sum(ext, w) * inv - u


def _pool_fwd(h, w_grp, scale, *, tm, name):
    S = h.shape[0]
    E = D_INNER

    def body(u_ref, uh_ref, z_ref, wg_ref, sc_ref, y_ref):
        i = pl.program_id(0)
        halo = jnp.where(i > 0, uh_ref[...], 0.0)
        t_idx = i * tm + lax.broadcasted_iota(jnp.int32, (tm, 1), 0)
        for g, w in enumerate(POOL_WINDOWS):
            cs = slice(g * POOL_GROUP, (g + 1) * POOL_GROUP)
            pooled = _pooled_group(u_ref, halo, g, w, t_idx)
            mixed = jnp.dot(pooled.astype(BF16), wg_ref[g], preferred_element_type=F32)
            z = z_ref[:, cs]
            y_ref[:, cs] = (mixed * sc_ref[:, cs] * (z * _sigmoid(z))).astype(BF16)

    return pl.pallas_call(
        body, name=name,
        out_shape=jax.ShapeDtypeStruct((S, E), BF16),
        grid=(S // tm,),
        in_specs=[pl.BlockSpec((tm, E), lambda i: (i, 0)), _prev_halo_spec(tm, E, 0),
                  pl.BlockSpec((tm, E), lambda i: (i, 1)),
                  pl.BlockSpec((len(POOL_WINDOWS), POOL_GROUP, POOL_GROUP), lambda i: (0, 0, 0)),
                  pl.BlockSpec((1, E), lambda i: (0, 0))],
        out_specs=pl.BlockSpec((tm, E), lambda i: (i, 0)),
        compiler_params=_cparams("parallel"),
    )(h, h, h, w_grp, scale)


def _pool_bwd1(h, dy, w_grp, scale, *, tm, name):
    S = h.shape[0]
    E = D_INNER

    def body(u_ref, uh_ref, z_ref, dy_ref, wg_ref, sc_ref, pooled_ref, dmixed_ref, dpooled_ref, dz_ref, dsc_ref):
        i = pl.program_id(0)
        halo = jnp.where(i > 0, uh_ref[...], 0.0)
        t_idx = i * tm + lax.broadcasted_iota(jnp.int32, (tm, 1), 0)
        for g, w in enumerate(POOL_WINDOWS):
            cs = slice(g * POOL_GROUP, (g + 1) * POOL_GROUP)
            pooled = _pooled_group(u_ref, halo, g, w, t_idx).astype(BF16)
            wg = wg_ref[g]
            mixed = jnp.dot(pooled, wg, preferred_element_type=F32)
            z = z_ref[:, cs]
            sg = _sigmoid(z)
            dyv = dy_ref[:, cs]
            sc = sc_ref[:, cs]
            dms = dyv * (z * sg)
            dz = dyv * (mixed * sc) * (sg * (1.0 + z * (1.0 - sg)))
            dsc = jnp.sum(dms * mixed, axis=0, keepdims=True)
            dmixed = (dms * sc).astype(BF16)
            dpooled = lax.dot_general(dmixed, wg, (((1,), (1,)), ((), ())), preferred_element_type=F32)
            pooled_ref[:, cs] = pooled
            dmixed_ref[:, cs] = dmixed
            dpooled_ref[:, cs] = dpooled
            dz_ref[:, cs] = dz.astype(BF16)

            @pl.when(i == 0)
            def _():
                dsc_ref[:, cs] = dsc

            @pl.when(i > 0)
            def _():
                dsc_ref[:, cs] += dsc

    row = pl.BlockSpec((tm, E), lambda i: (i, 0))
    vec = pl.BlockSpec((1, E), lambda i: (0, 0))
    return pl.pallas_call(
        body, name=name,
        out_shape=(jax.ShapeDtypeStruct((S, E), BF16), jax.ShapeDtypeStruct((S, E), BF16),
                   jax.ShapeDtypeStruct((S, E), F32), jax.ShapeDtypeStruct((S, E), BF16),
                   jax.ShapeDtypeStruct((1, E), F32)),
        grid=(S // tm,),
        in_specs=[row, _prev_halo_spec(tm, E, 0), pl.BlockSpec((tm, E), lambda i: (i, 1)), row,
                  pl.BlockSpec((len(POOL_WINDOWS), POOL_GROUP, POOL_GROUP), lambda i: (0, 0, 0)), vec],
        out_specs=(row, row, row, row, vec),
        compiler_params=_cparams("arbitrary"),
    )(h, h, h, dy, w_grp, scale)


def _pool_bwd2(dpooled, dz, *, tm, name):
    S = dpooled.shape[0]
    E = D_INNER
    nt = S // tm

    def body(dp_ref, dpn_ref, dz_ref, dh_ref):
        i = pl.program_id(0)
        nxt = jnp.where(i < nt - 1, dpn_ref[...], 0.0)
        t_ext = i * tm + lax.broadcasted_iota(jnp.int32, (tm + HALO, 1), 0)
        for g, w in enumerate(POOL_WINDOWS):
            cs = slice(g * POOL_GROUP, (g + 1) * POOL_GROUP)
            dp = dp_ref[:, cs]
            inv = 1.0 / jnp.minimum(t_ext + 1, w).astype(F32)
            s = jnp.concatenate([dp, nxt[:, cs]], axis=0) * inv
            n = tm + HALO
            k = 1
            while k < w:
                s = s + pltpu.roll(s, n - k, 0)
                k *= 2
            dh_ref[:, cs] = (s[:tm, :] - dp).astype(BF16)
        dh_ref[:, E:] = dz_ref[...]

    return pl.pallas_call(
        body, name=name,
        out_shape=jax.ShapeDtypeStruct((S, 2 * E), BF16),
        grid=(nt,),
        in_specs=[pl.BlockSpec((tm, E), lambda i: (i, 0)), _next_halo_spec(tm, E, 0, S),
                  pl.BlockSpec((tm, E), lambda i: (i, 0))],
        out_specs=pl.BlockSpec((tm, 2 * E), lambda i: (i, 0)),
        compiler_params=_cparams("parallel"),
    )(dpooled, dpooled, dz)


CONV_CHUNK = 512


def _conv_fwd(h, cw, *, tm, name):
    S = h.shape[0]
    E = D_INNER

    def body(b_ref, c_ref, hh_ref, z_ref, ch_ref, hhh_ref, w_ref, y_ref):
        i = pl.program_id(0)
        for j in range(E // CONV_CHUNK):
            cs = slice(j * CONV_CHUNK, (j + 1) * CONV_CHUNK)
            p = c_ref[:, cs] * hh_ref[:, cs]
            ph = jnp.where(i > 0, ch_ref[:, cs] * hhh_ref[:, cs], 0.0)
            ext = jnp.concatenate([ph, p], axis=0)
            conv = w_ref[2:3, cs] * p + w_ref[1:2, cs] * _shift_down(ext, 1) + w_ref[0:1, cs] * _shift_down(ext, 2)
            z = z_ref[:, cs]
            y_ref[:, cs] = (b_ref[:, cs] * conv * (z * _sigmoid(z))).astype(BF16)

    col = lambda c: pl.BlockSpec((tm, E), lambda i: (i, c))
    return pl.pallas_call(
        body, name=name,
        out_shape=jax.ShapeDtypeStruct((S, E), BF16),
        grid=(S // tm,),
        in_specs=[col(0), col(1), col(2), col(3), _prev_halo_spec(tm, E, 1), _prev_halo_spec(tm, E, 2),
                  pl.BlockSpec((8, E), lambda i: (0, 0))],
        out_specs=pl.BlockSpec((tm, E), lambda i: (i, 0)),
        compiler_params=_cparams("parallel"),
    )(h, h, h, h, h, h, cw)


def _conv_bwd(h, dy, cw, *, tm, name):
    S = h.shape[0]
    E = D_INNER
    nt = S // tm

    def body(b_ref, c_ref, hh_ref, z_ref, dy_ref, ch_ref, hhh_ref, bn_ref, zn_ref, dyn_ref, w_ref, dh_ref, dw_ref):
        i = pl.program_id(0)
        for j in range(E // CONV_CHUNK):
            cs = slice(j * CONV_CHUNK, (j + 1) * CONV_CHUNK)
            w0, w1, w2 = w_ref[0:1, cs], w_ref[1:2, cs], w_ref[2:3, cs]
            c, hh, b, z, dyv = c_ref[:, cs], hh_ref[:, cs], b_ref[:, cs], z_ref[:, cs], dy_ref[:, cs]
            p = c * hh
            ph = jnp.where(i > 0, ch_ref[:, cs] * hhh_ref[:, cs], 0.0)
            ext = jnp.concatenate([ph, p], axis=0)
            pm1 = _shift_down(ext, 1)
            pm2 = _shift_down(ext, 2)
            conv = w2 * p + w1 * pm1 + w0 * pm2
            sg = _sigmoid(z)
            dy0 = dyv * (z * sg)
            dz = dyv * (b * conv) * (sg * (1.0 + z * (1.0 - sg)))
            db = dy0 * conv
            dconv = dy0 * b
            zn = zn_ref[:, cs]
            dconv_n = jnp.where(i < nt - 1, dyn_ref[:, cs] * (zn * _sigmoid(zn)) * bn_ref[:, cs], 0.0)
            dext = jnp.concatenate([dconv, dconv_n], axis=0)
            dp = w2 * dconv + w1 * _shift_up(dext, 1, tm) + w0 * _shift_up(dext, 2, tm)
            dh_ref[:, 0 * E + j * CONV_CHUNK:0 * E + (j + 1) * CONV_CHUNK] = db.astype(BF16)
            dh_ref[:, 1 * E + j * CONV_CHUNK:1 * E + (j + 1) * CONV_CHUNK] = (dp * hh).astype(BF16)
            dh_ref[:, 2 * E + j * CONV_CHUNK:2 * E + (j + 1) * CONV_CHUNK] = (dp * c).astype(BF16)
            dh_ref[:, 3 * E + j * CONV_CHUNK:3 * E + (j + 1) * CONV_CHUNK] = dz.astype(BF16)
            dw = jnp.concatenate([jnp.sum(dconv * pm2, axis=0, keepdims=True),
                                  jnp.sum(dconv * pm1, axis=0, keepdims=True),
                                  jnp.sum(dconv * p, axis=0, keepdims=True),
                                  jnp.zeros((5, CONV_CHUNK), F32)], axis=0)

            @pl.when(i == 0)
            def _():
                dw_ref[:, cs] = dw

            @pl.when(i > 0)
            def _():
                dw_ref[:, cs] += dw

    col = lambda c: pl.BlockSpec((tm, E), lambda i: (i, c))
    return pl.pallas_call(
        body, name=name,
        out_shape=(jax.ShapeDtypeStruct((S, 4 * E), BF16), jax.ShapeDtypeStruct((8, E), F32)),
        grid=(nt,),
        in_specs=[col(0), col(1), col(2), col(3), pl.BlockSpec((tm, E), lambda i: (i, 0)),
                  _prev_halo_spec(tm, E, 1), _prev_halo_spec(tm, E, 2),
                  _next_halo_spec(tm, E, 0, S), _next_halo_spec(tm, E, 3, S), _next_halo_spec(tm, E, 0, S),
                  pl.BlockSpec((8, E), lambda i: (0, 0))],
        out_specs=(pl.BlockSpec((tm, 4 * E), lambda i: (i, 0)), pl.BlockSpec((8, E), lambda i: (0, 0))),
        compiler_params=_cparams("arbitrary"),
    )(h, h, h, h, dy, h, h, h, h, dy, cw)


Z_COLS = D_INNER // LANES
KV_LAT_BLK = D_INNER // KV_LORA
Q_LAT_BLK = (D_INNER + KV_LORA) // Q_LORA
K_ROPE_BLK = (D_INNER + KV_LORA + Q_LORA) // LANES


def _rope(blk, c, s1, s2):
    return blk * c + pltpu.roll(blk, LANES - QK_ROPE // 2, 1) * s1 + pltpu.roll(blk, QK_ROPE // 2, 1) * s2


def _unrope(blk, c, s1, s2):
    return blk * c - pltpu.roll(blk, LANES - QK_ROPE // 2, 1) * s1 - pltpu.roll(blk, QK_ROPE // 2, 1) * s2


def _lat_norm(v, g):
    rstd = lax.rsqrt(jnp.mean(v * v, axis=-1, keepdims=True) + NORM_EPS)
    return v * rstd * g


def _mla_latent_fwd(h, q_norm, kv_norm, tabs, *, tm, name):
    S = h.shape[0]

    def body(kv_ref, q_ref, kr_ref, qg_ref, kvg_ref, c_ref, s1_ref, s2_ref, qn_ref, kvn_ref, krr_ref):
        qn_ref[...] = _lat_norm(q_ref[...], qg_ref[...]).astype(BF16)
        kvn_ref[...] = _lat_norm(kv_ref[...], kvg_ref[...]).astype(BF16)
        krr_ref[...] = _rope(kr_ref[...], c_ref[...], s1_ref[...], s2_ref[...]).astype(BF16)

    tab = pl.BlockSpec((tm, LANES), lambda i: (i, 0))
    return pl.pallas_call(
        body, name=name,
        out_shape=(jax.ShapeDtypeStruct((S, Q_LORA), BF16), jax.ShapeDtypeStruct((S, KV_LORA), BF16),
                   jax.ShapeDtypeStruct((S, LANES), BF16)),
        grid=(S // tm,),
        in_specs=[pl.BlockSpec((tm, KV_LORA), lambda i: (i, KV_LAT_BLK)), pl.BlockSpec((tm, Q_LORA), lambda i: (i, Q_LAT_BLK)),
                  pl.BlockSpec((tm, LANES), lambda i: (i, K_ROPE_BLK)),
                  pl.BlockSpec((1, Q_LORA), lambda i: (0, 0)), pl.BlockSpec((1, KV_LORA), lambda i: (0, 0)), tab, tab, tab],
        out_specs=(pl.BlockSpec((tm, Q_LORA), lambda i: (i, 0)), pl.BlockSpec((tm, KV_LORA), lambda i: (i, 0)), tab),
        compiler_params=_cparams("parallel"),
    )(h, h, h, q_norm, kv_norm, *tabs)


def _mla_q_up(q_n, w_q_pad, tabs, *, tm, name):
    S = q_n.shape[0]

    def body(a_ref, w_ref, c_ref, s1_ref, s2_ref, o_ref):
        a = a_ref[...]
        for hd in range(N_HEADS):
            acc = jnp.dot(a, w_ref[:, hd * HEAD_PAD:(hd + 1) * HEAD_PAD], preferred_element_type=F32)
            o_ref[hd, :, :QK_NOPE] = acc[:, :QK_NOPE].astype(BF16)
            o_ref[hd, :, QK_NOPE:] = _rope(acc[:, QK_NOPE:], c_ref[...], s1_ref[...], s2_ref[...]).astype(BF16)

    tab = pl.BlockSpec((tm, LANES), lambda i: (i, 0))
    return pl.pallas_call(
        body, name=name,
        out_shape=jax.ShapeDtypeStruct((N_HEADS, S, HEAD_PAD), BF16),
        grid=(S // tm,),
        in_specs=[pl.BlockSpec((tm, Q_LORA), lambda i: (i, 0)), pl.BlockSpec((Q_LORA, N_HEADS * HEAD_PAD), lambda i: (0, 0)),
                  tab, tab, tab],
        out_specs=pl.BlockSpec((N_HEADS, tm, HEAD_PAD), lambda i: (0, i, 0)),
        compiler_params=_cparams("parallel"),
    )(q_n, w_q_pad, *tabs)


def _mla_kv_up(kv_n, w_kv, krr, *, tm, name):
    S = kv_n.shape[0]
    heads_per_chip = N_HEADS // N_CHIPS

    def body(a_ref, w_ref, krr_ref, k_ref, v_ref):
        a = a_ref[...]
        ones = jnp.ones((tm, V_DIM), BF16)
        for hd in range(N_HEADS):
            lo = (hd % heads_per_chip) * HEAD_PAD
            acc = jnp.dot(a, w_ref[hd // heads_per_chip, :, lo:lo + HEAD_PAD], preferred_element_type=F32)
            k_ref[hd, :, :QK_NOPE] = acc[:, :QK_NOPE].astype(BF16)
            k_ref[hd, :, QK_NOPE:] = krr_ref[...]
            v_ref[hd, :, :V_DIM] = acc[:, QK_NOPE:].astype(BF16)
            v_ref[hd, :, V_DIM:] = ones

    head_blk = pl.BlockSpec((N_HEADS, tm, HEAD_PAD), lambda i: (0, i, 0))
    sds = jax.ShapeDtypeStruct((N_HEADS, S, HEAD_PAD), BF16)
    return pl.pallas_call(
        body, name=name, out_shape=(sds, sds),
        grid=(S // tm,),
        in_specs=[pl.BlockSpec((tm, KV_LORA), lambda i: (i, 0)),
                  pl.BlockSpec((N_CHIPS, KV_LORA, PACK_COLS), lambda i: (0, w_kv.r0 // KV_LORA, 0)),
                  pl.BlockSpec((tm, LANES), lambda i: (i, 0))],
        out_specs=(head_blk, head_blk),
        compiler_params=_cparams("parallel"),
    )(kv_n, w_kv.buf, krr)


LOG2E = 1.4426950408889634
SCORE_TO_LOG2 = ATTN_SCALE * LOG2E


def _flash_fwd(q_full, k_full, v_aug, h, *, tq, name):
    H, S, _ = q_full.shape
    tk = tq
    HP = 2
    QT = 2 if (S // tq) % 2 == 0 else 1
    rows_of = lambda t: slice(t * tq, (t + 1) * tq)

    def body(q_ref, k_ref, v_ref, z_ref, o_ref, y_ref, lse_ref, m_sc, acc_sc):
        first = pl.program_id(1) * QT
        m_sc[...] = jnp.full(m_sc.shape, -1e30, F32)
        acc_sc[...] = jnp.zeros(acc_sc.shape, F32)

        def chunk(j, masked):
            off = pl.multiple_of(j * tk, tk)
            for hh in range(HP):
                kj = k_ref[hh, pl.ds(off, tk), :]
                vj = v_ref[hh, pl.ds(off, tk), :]
                for t in range(QT):
                    if masked[t] is None:
                        continue
                    rows = rows_of(t)
                    s = lax.dot_general(q_ref[hh, rows], kj, (((1,), (1,)), ((), ())), preferred_element_type=F32) * SCORE_TO_LOG2
                    if masked[t]:
                        keep = lax.broadcasted_iota(jnp.int32, (tq, tk), 1) <= lax.broadcasted_iota(jnp.int32, (tq, tk), 0)
                        s = jnp.where(keep, s, -1e30)
                    m_old = m_sc[hh, rows]
                    m_new = jnp.maximum(m_old, jnp.max(s, axis=1, keepdims=True))
                    p = jnp.exp2(s - jnp.tile(m_new, (1, tk // LANES)))
                    alpha = jnp.exp2(m_old - m_new)
                    acc_sc[hh, rows] = jnp.tile(alpha, (1, 2)) * acc_sc[hh, rows] + jnp.dot(p.astype(BF16), vj, preferred_element_type=F32)
                    m_sc[hh, rows] = m_new

        def step(j, carry):
            chunk(j, (False,) * QT)
            return carry

        lax.fori_loop(0, first, step, 0)
        for d in range(QT):
            chunk(first + d, tuple(None if t < d else t == d for t in range(QT)))
        for hh in range(HP):
            cs = slice(hh * V_DIM, (hh + 1) * V_DIM)
            for t in range(QT):
                rows = rows_of(t)
                l = acc_sc[hh, rows, V_DIM:]
                o = acc_sc[hh, rows, :V_DIM] / l
                z = z_ref[rows, cs]
                o_ref[rows, cs] = o
                y_ref[rows, cs] = (o * (z * _sigmoid(z))).astype(BF16)
                lse_ref[hh, t] = (m_sc[hh, rows] + jnp.log2(l)).T[0:1, :]

    pair = pl.BlockSpec((QT * tq, HP * V_DIM), lambda hd, i: (i, hd))
    return pl.pallas_call(
        body, name=name,
        out_shape=(jax.ShapeDtypeStruct((S, D_INNER), F32), jax.ShapeDtypeStruct((S, D_INNER), BF16),
                   jax.ShapeDtypeStruct((H, S // tq, 1, tq), F32)),
        grid=(H // HP, S // (QT * tq)),
        in_specs=[pl.BlockSpec((HP, QT * tq, HEAD_PAD), lambda hd, i: (hd, i, 0)),
                  pl.BlockSpec((HP, S, HEAD_PAD), lambda hd, i: (hd, 0, 0)),
                  pl.BlockSpec((HP, S, HEAD_PAD), lambda hd, i: (hd, 0, 0)),
                  pair],
        out_specs=(pair, pair, pl.BlockSpec((HP, QT, 1, tq), lambda hd, i: (hd, i, 0, 0))),
        scratch_shapes=[pltpu.VMEM((HP, QT * tq, LANES), F32), pltpu.VMEM((HP, QT * tq, HEAD_PAD), F32)],
        compiler_params=_cparams("parallel", "parallel"),
    )(q_full, k_full, v_aug, h)


def _mla_gate_bwd(dy, o, h, *, tm, name):
    S = dy.shape[0]
    E = D_INNER

    def body(dy_ref, o_ref, z_ref, do_ref, dz_ref, delta_ref):
        for hd in range(N_HEADS):
            cs = slice(hd * V_DIM, (hd + 1) * V_DIM)
            z = z_ref[:, cs]
            sg = _sigmoid(z)
            dyv = dy_ref[:, cs]
            ov = o_ref[:, cs]
            do = dyv * (z * sg)
            do_ref[:, cs] = do.astype(BF16)
            dz_ref[:, cs] = (dyv * ov * (sg * (1.0 + z * (1.0 - sg)))).astype(BF16)
            delta_ref[hd, 0] = jnp.broadcast_to(jnp.sum(do * ov, axis=-1, keepdims=True), (tm, LANES)).T[0:1, :]

    row = pl.BlockSpec((tm, E), lambda i: (i, 0))
    return pl.pallas_call(
        body, name=name,
        out_shape=(jax.ShapeDtypeStruct((S, E), BF16), jax.ShapeDtypeStruct((S, E), BF16),
                   jax.ShapeDtypeStruct((N_HEADS, S // tm, 1, tm), F32)),
        grid=(S // tm,),
        in_specs=[row, row, row],
        out_specs=(row, row, pl.BlockSpec((N_HEADS, 1, 1, tm), lambda i: (0, i, 0, 0))),
        compiler_params=_cparams("parallel"),
    )(dy, o, h)


def _flash_bwd(q_full, k_full, v_aug, do, lse_rows, delta_rows, tabs, *, tq, name):
    H, S, _ = q_full.shape
    tk = tq
    nq = S // tq
    HP = 2
    KT = 2 if nq % 2 == 0 else 1
    rows_of = lambda t: slice(t * tk, (t + 1) * tk)

    def body(q_ref, k_ref, v_ref, do_ref, lse_ref, dl_ref, c_ref, s1_ref, s2_ref, dqp_ref, dkv_ref, dkr_ref, dq_ref, dk_sc, dv_sc):
        first = pl.program_id(1) * KT

        @pl.when(first == 0)
        def _():
            dq_ref[...] = jnp.zeros(dq_ref.shape, F32)

        dk_sc[...] = jnp.zeros(dk_sc.shape, F32)
        dv_sc[...] = jnp.zeros(dv_sc.shape, F32)

        def chunk(qi, masked):
            off = pl.multiple_of(qi * tq, tq)
            for hh in range(HP):
                q = q_ref[hh, pl.ds(off, tq), :]
                dov = do_ref[pl.ds(off, tq), hh * V_DIM:(hh + 1) * V_DIM]
                for t in range(KT):
                    if masked[t] is None:
                        continue
                    rows = rows_of(t)
                    k = k_ref[hh, rows]
                    s_t = lax.dot_general(k, q, (((1,), (1,)), ((), ())), preferred_element_type=F32) * SCORE_TO_LOG2
                    p_t = jnp.exp2(s_t - lse_ref[hh, qi])
                    if masked[t]:
                        keep = lax.broadcasted_iota(jnp.int32, (tk, tq), 0) <= lax.broadcasted_iota(jnp.int32, (tk, tq), 1)
                        p_t = jnp.where(keep, p_t, 0.0)
                    dv_sc[hh, rows] += jnp.dot(p_t.astype(BF16), dov, preferred_element_type=F32)
                    dp_t = lax.dot_general(v_ref[hh, rows], dov, (((1,), (1,)), ((), ())), preferred_element_type=F32)
                    ds = (p_t * (dp_t - dl_ref[hh, qi])).astype(BF16)
                    dk_sc[hh, rows] += jnp.dot(ds, q, preferred_element_type=F32)
                    dq_ref[pl.ds(off, tq), hh * HEAD_PAD:(hh + 1) * HEAD_PAD] += lax.dot_general(
                        ds, k, (((0,), (0,)), ((), ())), preferred_element_type=F32)

        def step(qi, carry):
            chunk(qi, (False,) * KT)
            return carry

        for d in range(KT):
            chunk(first + d, tuple(None if t > d else t == d for t in range(KT)))
            done = pl.ds(pl.multiple_of((first + d) * tq, tq), tq)
            here = rows_of(d)
            for hh in range(HP):
                lo = hh * HEAD_PAD
                dqp_ref[here, lo:lo + QK_NOPE] = (dq_ref[done, lo:lo + QK_NOPE] * ATTN_SCALE).astype(BF16)
                dqp_ref[here, lo + QK_NOPE:lo + HEAD_PAD] = _unrope(dq_ref[done, lo + QK_NOPE:lo + HEAD_PAD] * ATTN_SCALE,
                                                                    c_ref[here, :], s1_ref[here, :], s2_ref[here, :]).astype(BF16)
        lax.fori_loop(first + KT, nq, step, 0)
        for hh in range(HP):
            lo = hh * HEAD_PAD
            dkv_ref[:, lo:lo + QK_NOPE] = (dk_sc[hh, :, :QK_NOPE] * ATTN_SCALE).astype(BF16)
            dkv_ref[:, lo + QK_NOPE:lo + HEAD_PAD] = dv_sc[hh].astype(BF16)
            dkr_ref[hh] = dk_sc[hh, :, QK_NOPE:] * ATTN_SCALE

    tab = pl.BlockSpec((KT * tk, LANES), lambda hd, j: (j, 0))
    pair_rows = pl.BlockSpec((KT * tk, HP * HEAD_PAD), lambda hd, j: (j, hd))
    return pl.pallas_call(
        body, name=name,
        out_shape=(jax.ShapeDtypeStruct((S, H * HEAD_PAD), BF16), jax.ShapeDtypeStruct((S, H * HEAD_PAD), BF16),
                   jax.ShapeDtypeStruct((H, S, LANES), F32)),
        grid=(H // HP, S // (KT * tk)),
        in_specs=[pl.BlockSpec((HP, S, HEAD_PAD), lambda hd, j: (hd, 0, 0)),
                  pl.BlockSpec((HP, KT * tk, HEAD_PAD), lambda hd, j: (hd, j, 0)),
                  pl.BlockSpec((HP, KT * tk, V_DIM), lambda hd, j: (hd, j, 0)),
                  pl.BlockSpec((S, HP * V_DIM), lambda hd, j: (0, hd)),
                  pl.BlockSpec((HP, nq, 1, tq), lambda hd, j: (hd, 0, 0, 0)),
                  pl.BlockSpec((HP, nq, 1, tq), lambda hd, j: (hd, 0, 0, 0)), tab, tab, tab],
        out_specs=(pair_rows, pair_rows, pl.BlockSpec((HP, KT * tk, LANES), lambda hd, j: (hd, j, 0))),
        scratch_shapes=[pltpu.VMEM((S, HP * HEAD_PAD), F32), pltpu.VMEM((HP, KT * tk, HEAD_PAD), F32), pltpu.VMEM((HP, KT * tk, V_DIM), F32)],
        compiler_params=_cparams("parallel", "arbitrary"),
    )(q_full, k_full, v_aug, do, lse_rows, delta_rows, *tabs)


def _mla_latent_bwd(h, dq_n, dkv_n, dkr, dz, q_norm, kv_norm, tabs, *, tm, name):
    S = h.shape[0]

    def body(kv_ref, q_ref, dqn_ref, dkvn_ref, dkr_ref, dz_ref, qg_ref, kvg_ref, c_ref, s1_ref, s2_ref, dh_ref, dqg_ref, dkvg_ref):
        i = pl.program_id(0)
        dq_lat, dqg = _rms_bwd_math(q_ref[...], qg_ref[...], dqn_ref[...])
        dkv_lat, dkvg = _rms_bwd_math(kv_ref[...], kvg_ref[...], dkvn_ref[...])
        dkr_sum = dkr_ref[0]
        for hd in range(1, N_HEADS):
            dkr_sum = dkr_sum + dkr_ref[hd]
        dh_ref[:, :D_INNER] = dz_ref[...]
        dh_ref[:, D_INNER:D_INNER + KV_LORA] = dkv_lat.astype(BF16)
        dh_ref[:, D_INNER + KV_LORA:D_INNER + KV_LORA + Q_LORA] = dq_lat.astype(BF16)
        dh_ref[:, D_INNER + KV_LORA + Q_LORA:] = _unrope(dkr_sum, c_ref[...], s1_ref[...], s2_ref[...]).astype(BF16)

        @pl.when(i == 0)
        def _():
            dqg_ref[...] = dqg
            dkvg_ref[...] = dkvg

        @pl.when(i > 0)
        def _():
            dqg_ref[...] += dqg
            dkvg_ref[...] += dkvg

    tab = pl.BlockSpec((tm, LANES), lambda i: (i, 0))
    qvec = pl.BlockSpec((1, Q_LORA), lambda i: (0, 0))
    kvvec = pl.BlockSpec((1, KV_LORA), lambda i: (0, 0))
    return pl.pallas_call(
        body, name=name,
        out_shape=(jax.ShapeDtypeStruct((S, MLA_IN_PAD), BF16), jax.ShapeDtypeStruct((1, Q_LORA), F32),
                   jax.ShapeDtypeStruct((1, KV_LORA), F32)),
        grid=(S // tm,),
        in_specs=[pl.BlockSpec((tm, KV_LORA), lambda i: (i, KV_LAT_BLK)), pl.BlockSpec((tm, Q_LORA), lambda i: (i, Q_LAT_BLK)),
                  pl.BlockSpec((tm, Q_LORA), lambda i: (i, 0)), pl.BlockSpec((tm, KV_LORA), lambda i: (i, 0)),
                  pl.BlockSpec((N_HEADS, tm, LANES), lambda i: (0, i, 0)), pl.BlockSpec((tm, D_INNER), lambda i: (i, 0)),
                  qvec, kvvec, tab, tab, tab],
        out_specs=(pl.BlockSpec((tm, MLA_IN_PAD), lambda i: (i, 0)), qvec, kvvec),
        compiler_params=_cparams("arbitrary"),
    )(h, h, dq_n, dkv_n, dkr, dz, q_norm, kv_norm, *tabs)


def _adamw(w, g, m, v, *, name):
    R, C = w.shape
    tr = R
    for cand in (512, 256, 128, 64, 32, 16, 8):
        if R % cand == 0 and cand * C * 4 <= 2 * 1024 * 1024:
            tr = cand
            break

    def body(w_ref, g_ref, m_ref, v_ref, d_ref, nm_ref, nv_ref):
        d_ref[...], nm_ref[...], nv_ref[...] = _adam_math(w_ref[...], g_ref[...], m_ref[...], v_ref[...])

    spec = pl.BlockSpec((tr, C), lambda i: (i, 0))
    sds = jax.ShapeDtypeStruct((R, C), F32)
    return pl.pallas_call(
        body, name=name, out_shape=(sds, sds, sds), grid=(R // tr,),
        in_specs=[spec] * 4, out_specs=(spec,) * 3,
        compiler_params=_cparams("parallel"),
    )(w, g, m, v)


def _adamw_rows(w, m, v, srcs, *, name):
    nj, R, C = w.shape
    assert len(srcs) == nj and C % PACK_COLS == 0
    tr = min(R, 256)
    assert R % tr == 0 and all(r0 % tr == 0 for _, r0 in srcs)

    def body(*refs):
        w_ref, m_ref, v_ref = refs[:3]
        g_refs = refs[3:3 + nj]
        go_ref, d_ref, nm_ref, nv_ref = refs[3 + nj:]
        gv = g_refs[0][...]
        for jj in range(1, nj):
            gv = jnp.where(pl.program_id(0) == jj, g_refs[jj][...], gv)
        d, m_new, v_new = _adam_math(w_ref[...], gv, m_ref[...], v_ref[...])
        go_ref[...] = gv
        d_ref[...] = d
        nm_ref[...] = m_new
        nv_ref[...] = v_new

    nat = pl.BlockSpec((None, tr, PACK_COLS), lambda j, cb, i: (j, i, cb))

    def src_spec(jj, r0):
        return pl.BlockSpec((tr, PACK_COLS), lambda j, cb, i: (jnp.where(j == jj, (r0 + cb * R) // tr + i, r0 // tr), 0))

    sds = jax.ShapeDtypeStruct((nj, R, C), F32)
    return pl.pallas_call(
        body, name=name, out_shape=(sds,) * 4, grid=(nj, C // PACK_COLS, R // tr),
        in_specs=[nat] * 3 + [src_spec(jj, r0) for jj, (_, r0) in enumerate(srcs)], out_specs=(nat,) * 4,
        compiler_params=_cparams("parallel", "parallel", "parallel"),
    )(w, m, v, *[rows for rows, _ in srcs])


HBM_SPEC = pl.BlockSpec(memory_space=pltpu.HBM)
VMEM_SPEC = pl.BlockSpec(memory_space=pltpu.VMEM)
SEM_SPEC = pl.BlockSpec(memory_space=pltpu.SEMAPHORE)
ANY_SPEC = pl.BlockSpec(memory_space=pl.ANY)
SPLIT_EFFECT = pltpu.SideEffectType.DATAFLOW_SIDE_EFFECTING


def _place():
    return lax.axis_index("x"), lax.axis_index("y"), lax.axis_index("c")


def _other_chips(x, y):
    return [(1 - x, y), (x, 1 - y), (1 - x, 1 - y)]


def _remote(src, dst, send_sem, recv_sem, dev):
    return pltpu.make_async_remote_copy(src_ref=src, dst_ref=dst, send_sem=send_sem, recv_sem=recv_sem,
                                        device_id=dev, device_id_type=MESH)


def _ag_ici_start(wp, after, *, name):
    R, C = wp.shape
    H = R // 2

    def body(w_ref, land_ref, after_ref, send_sems, recv_sems, w_thru, land_thru, token):
        x, y, c = _place()
        rows = pl.ds(pl.multiple_of(c * H, 16), H)
        for r, chip in enumerate(_other_chips(x, y)):
            _remote(w_ref.at[rows, :], land_ref.at[2 * x + y, rows, :], send_sems.at[r], recv_sems.at[r], (*chip, c)).start()
        token[...] = jnp.zeros(token.shape, F32)

    land = lax.empty((N_CHIPS, R, C), wp.dtype)
    return pl.pallas_call(
        body, name=name,
        out_shape=(pltpu.SemaphoreType.DMA((3,)), pltpu.SemaphoreType.DMA((3,)), pltpu.HBM(wp.shape, wp.dtype), pltpu.HBM(land.shape, land.dtype),
                   jax.ShapeDtypeStruct((8, LANES), F32)),
        in_specs=(HBM_SPEC, HBM_SPEC, ANY_SPEC), out_specs=(SEM_SPEC, SEM_SPEC, HBM_SPEC, HBM_SPEC, VMEM_SPEC),
        input_output_aliases={0: 2, 1: 3},
        compiler_params=pltpu.CompilerParams(has_side_effects=SPLIT_EFFECT),
    )(pltpu.with_memory_space_constraint(wp, pltpu.HBM), pltpu.with_memory_space_constraint(land, pltpu.HBM), after)


def _ag_ici_wait(send_sems, recv_sems, w_thru, land_thru, after, *, name):
    R, C = w_thru.shape
    H = R // 2
    after = after if isinstance(after, (tuple, list)) else (after,)

    def body(w_ref, land_ref, send_sems, recv_sems, *rest):
        x, y, c = _place()
        rows = pl.ds(pl.multiple_of(c * H, 16), H)
        for r, (px, py) in enumerate(_other_chips(x, y)):
            cp = _remote(w_ref.at[rows, :], land_ref.at[2 * px + py, rows, :], send_sems.at[r], recv_sems.at[r], (px, py, c))
            cp.wait_send()
            cp.wait_recv()

    return pl.pallas_call(
        body, name=name,
        out_shape=(pltpu.HBM(w_thru.shape, w_thru.dtype), pltpu.HBM(land_thru.shape, land_thru.dtype)),
        in_specs=(HBM_SPEC, HBM_SPEC, SEM_SPEC, SEM_SPEC) + (ANY_SPEC,) * len(after), out_specs=(HBM_SPEC, HBM_SPEC),
        input_output_aliases={0: 0, 1: 1},
        compiler_params=pltpu.CompilerParams(has_side_effects=SPLIT_EFFECT),
    )(w_thru, land_thru, send_sems, recv_sems, *after)


def _ag_sibling_forward(land, wp, *, name):
    _, R, C = land.shape
    H = R // 2

    def body(land_ref, w_ref, out_ref, send_sems, recv_sems):
        x, y, c = _place()
        sib = (x, y, 1 - c)
        mine = pl.ds(pl.multiple_of(c * H, 16), H)
        theirs = pl.ds(pl.multiple_of((1 - c) * H, 16), H)
        chips = _other_chips(x, y)
        sends = [_remote(land_ref.at[2 * px + py, mine, :], out_ref.at[2 * px + py, mine, :], send_sems.at[r], recv_sems.at[r], sib)
                 for r, (px, py) in enumerate(chips)]
        sends.append(_remote(w_ref, out_ref.at[2 * x + y], send_sems.at[3], recv_sems.at[3], sib))
        for cp in sends:
            cp.start()
        for r, (px, py) in enumerate(chips):
            _remote(land_ref.at[2 * px + py, mine, :], out_ref.at[2 * px + py, theirs, :], send_sems.at[r], recv_sems.at[r], sib).wait_recv()
        sends[3].wait_recv()
        for cp in sends:
            cp.wait_send()

    return pl.pallas_call(
        body, name=name, out_shape=jax.ShapeDtypeStruct(land.shape, land.dtype),
        in_specs=[HBM_SPEC, HBM_SPEC], out_specs=HBM_SPEC, input_output_aliases={0: 0},
        scratch_shapes=[pltpu.SemaphoreType.DMA((4,)), pltpu.SemaphoreType.DMA((4,))],
    )(land, wp)


def _rs_sibling_swap(g, *, name):
    _, R, C = g.shape
    H = R // 2

    def body(g_ref, theirs_ref, send_sems, recv_sems):
        x, y, c = _place()
        sib = (x, y, 1 - c)
        copies = [_remote(g_ref.at[k, pl.ds(pl.multiple_of((1 - c) * H, 16), H), :], theirs_ref.at[k],
                          send_sems.at[k], recv_sems.at[k], sib) for k in range(N_CHIPS)]
        for cp in copies:
            cp.start()
        for cp in copies:
            cp.wait()

    return pl.pallas_call(
        body, name=name, out_shape=jax.ShapeDtypeStruct((N_CHIPS, H, C), g.dtype),
        in_specs=[HBM_SPEC], out_specs=HBM_SPEC,
        scratch_shapes=[pltpu.SemaphoreType.DMA((N_CHIPS,)), pltpu.SemaphoreType.DMA((N_CHIPS,))],
    )(g)


def _row_tile(h):
    best = 16
    for d in range(16, 1025, 16):
        if h % d == 0:
            best = d
    return best


def _add2_bf16(g, theirs, core, *, name):
    K, H, C = theirs.shape
    tr = _row_tile(H)
    nb = H // tr

    def body(c_ref, a_ref, b_ref, o_ref):
        o_ref[...] = (a_ref[...].astype(F32) + b_ref[...].astype(F32)).astype(o_ref.dtype)

    spec = pl.BlockSpec((None, tr, C), lambda k, i, c: (k, i, 0))
    return pl.pallas_call(
        body, name=name, out_shape=jax.ShapeDtypeStruct((K, H, C), theirs.dtype),
        grid_spec=pltpu.PrefetchScalarGridSpec(
            num_scalar_prefetch=1, grid=(K, nb),
            in_specs=[pl.BlockSpec((None, tr, C), lambda k, i, c: (k, c[0] * nb + i, 0)), spec], out_specs=spec),
        compiler_params=_cparams("parallel", "parallel"),
    )(core, g, theirs)


def _rs_chip_exchange_start(p, *, name):
    _, H, C = p.shape

    def body(p_ref, land_ref, send_sems, recv_sems, p_thru, land_thru, token):
        x, y, c = _place()
        for r, (px, py) in enumerate(_other_chips(x, y)):
            _remote(p_ref.at[2 * px + py], land_ref.at[r], send_sems.at[r], recv_sems.at[r], (px, py, c)).start()
        token[...] = jnp.zeros(token.shape, F32)

    land = lax.empty((3, H, C), p.dtype)
    return pl.pallas_call(
        body, name=name,
        out_shape=(pltpu.SemaphoreType.DMA((3,)), pltpu.SemaphoreType.DMA((3,)), pltpu.HBM(p.shape, p.dtype), pltpu.HBM(land.shape, land.dtype),
                   jax.ShapeDtypeStruct((8, LANES), F32)),
        in_specs=(HBM_SPEC, HBM_SPEC), out_specs=(SEM_SPEC, SEM_SPEC, HBM_SPEC, HBM_SPEC, VMEM_SPEC),
        input_output_aliases={0: 2, 1: 3},
        compiler_params=pltpu.CompilerParams(has_side_effects=SPLIT_EFFECT),
    )(pltpu.with_memory_space_constraint(p, pltpu.HBM), pltpu.with_memory_space_constraint(land, pltpu.HBM))


def _rs_chip_exchange_wait(send_sems, recv_sems, p_thru, land_thru, after, *, name):
    after = after if isinstance(after, (tuple, list)) else (after,)

    def body(p_ref, land_ref, send_sems, recv_sems, *rest):
        x, y, c = _place()
        for r, (px, py) in enumerate(_other_chips(x, y)):
            cp = _remote(p_ref.at[2 * px + py], land_ref.at[r], send_sems.at[r], recv_sems.at[r], (px, py, c))
            cp.wait_send()
            cp.wait_recv()

    return pl.pallas_call(
        body, name=name,
        out_shape=(pltpu.HBM(p_thru.shape, p_thru.dtype), pltpu.HBM(land_thru.shape, land_thru.dtype)),
        in_specs=(HBM_SPEC, HBM_SPEC, SEM_SPEC, SEM_SPEC) + (ANY_SPEC,) * len(after), out_specs=(HBM_SPEC, HBM_SPEC),
        input_output_aliases={0: 0, 1: 1},
        compiler_params=pltpu.CompilerParams(has_side_effects=SPLIT_EFFECT),
    )(p_thru, land_thru, send_sems, recv_sems, *after)


def _add4_f32(p, recv, chip_core, *, name):
    _, H, C = p.shape
    tr = _row_tile(H)
    nb = H // tr

    def body(s_ref, o_ref, r_ref, out_ref):
        out_ref[...] = ((o_ref[...].astype(F32) + r_ref[0].astype(F32)) + r_ref[1].astype(F32)) + r_ref[2].astype(F32)

    return pl.pallas_call(
        body, name=name, out_shape=jax.ShapeDtypeStruct((2 * H, C), F32),
        grid_spec=pltpu.PrefetchScalarGridSpec(
            num_scalar_prefetch=1, grid=(nb,),
            in_specs=[pl.BlockSpec((None, tr, C), lambda i, s: (s[0], i, 0)), pl.BlockSpec((3, tr, C), lambda i, s: (0, i, 0))],
            out_specs=pl.BlockSpec((tr, C), lambda i, s: (s[1] * nb + i, 0))),
        compiler_params=_cparams("parallel"),
    )(chip_core, p, recv)


def _rs_sibling_join(f, *, name):
    R, C = f.shape
    H = R // 2

    def body(f_ref, out_ref, send_sem, recv_sem):
        x, y, c = _place()
        sib = (x, y, 1 - c)
        mine = pl.ds(pl.multiple_of(c * H, 8), H)
        theirs = pl.ds(pl.multiple_of((1 - c) * H, 8), H)
        cp = _remote(f_ref.at[mine, :], out_ref.at[mine, :], send_sem, recv_sem, sib)
        cp.start()
        _remote(f_ref.at[mine, :], out_ref.at[theirs, :], send_sem, recv_sem, sib).wait_recv()
        cp.wait_send()

    return pl.pallas_call(
        body, name=name, out_shape=jax.ShapeDtypeStruct((R, C), f.dtype),
        in_specs=[HBM_SPEC], out_specs=HBM_SPEC, input_output_aliases={0: 0},
        scratch_shapes=[pltpu.SemaphoreType.DMA, pltpu.SemaphoreType.DMA],
    )(f)


SMALL_GATHER = (("pool_norm", 2, 256), ("pool_scale", 2, 512), ("conv_w", 3, 512), ("mla_norm", 1, 256),
                ("mla_q_norm", 1, 96), ("mla_kv_norm", 1, 64))
SMALL_SLOT = (16, 512)


def _gather_small(shards, *, name):
    def body(pn_ref, ps_ref, cw_ref, mn_ref, qn_ref, kn_ref, pn_o, ps_o, cw_o, mn_o, qn_o, kn_o, all_ref, send_sems, recv_sems):
        x, y, c = _place()
        mine = 2 * x + y
        all_ref[mine] = jnp.zeros(SMALL_SLOT, F32)
        all_ref[mine, 0:2, 0:256] = pn_ref[...]
        all_ref[mine, 2:4, :] = ps_ref[...]
        all_ref[mine, 4:7, :] = cw_ref[0]
        all_ref[mine, 7:8, 0:256] = mn_ref[...]
        all_ref[mine, 8:9, 0:96] = qn_ref[...]
        all_ref[mine, 9:10, 0:64] = kn_ref[...]
        chips = _other_chips(x, y)
        sends = [_remote(all_ref.at[mine], all_ref.at[mine], send_sems.at[r], recv_sems.at[r], (*chip, c)) for r, chip in enumerate(chips)]
        for cp in sends:
            cp.start()
        for r, (px, py) in enumerate(chips):
            _remote(all_ref.at[mine], all_ref.at[2 * px + py], send_sems.at[r], recv_sems.at[r], (px, py, c)).wait_recv()
        for cp in sends:
            cp.wait_send()
        for k in range(N_CHIPS):
            pn_o[:, k * 256:(k + 1) * 256] = all_ref[k, 0:2, 0:256]
            ps_o[:, k * 512:(k + 1) * 512] = all_ref[k, 2:4, :]
            cw_o[0, :, k * 512:(k + 1) * 512] = all_ref[k, 4:7, :]
            mn_o[:, k * 256:(k + 1) * 256] = all_ref[k, 7:8, 0:256]
            qn_o[k] = all_ref[k, 8:9, 0:96]
            kn_o[k] = all_ref[k, 9:10, 0:64]

    sds = lambda *shape: jax.ShapeDtypeStruct(shape, F32)
    out = pl.pallas_call(
        body, name=name,
        out_shape=(sds(2, 1024), sds(2, 2048), sds(1, 3, 2048), sds(1, 1024), sds(N_CHIPS, 1, 96), sds(N_CHIPS, 1, 64)),
        in_specs=[VMEM_SPEC] * 6, out_specs=(VMEM_SPEC,) * 6,
        scratch_shapes=[pltpu.VMEM((N_CHIPS,) + SMALL_SLOT, F32), pltpu.SemaphoreType.DMA((3,)), pltpu.SemaphoreType.DMA((3,))],
    )(*[shards[n] for n, _, _ in SMALL_GATHER])
    full = dict(zip([n for n, _, _ in SMALL_GATHER], out))
    full["mla_q_norm"] = full["mla_q_norm"].reshape(1, Q_LORA)
    full["mla_kv_norm"] = full["mla_kv_norm"].reshape(1, KV_LORA)
    return full


SMALL_REDUCE = (("pool_norm_0", 0, 1, 1024), ("pool_norm_1", 1, 1, 1024), ("pool_scale_0", 2, 1, 2048), ("pool_scale_1", 3, 1, 2048),
                ("conv_norm", 4, 1, 1024), ("mla_norm", 5, 1, 1024), ("mla_q_norm", 6, 1, 384), ("mla_kv_norm", 7, 1, 256),
                ("conv_w", 8, 8, 2048), ("final_norm", 16, 1, 1024))
REDUCE_SLOT = (24, 2048)


def _reduce_small(parts, after, *, name):
    keys = [k for k, _, _, _ in SMALL_REDUCE]

    def body(*refs):
        ins = dict(zip(keys, refs[:len(keys)]))
        pn_o, ps_o, cn_o, cw_o, mn_o, qn_o, kn_o, fn_o, all_ref, send_sems, recv_sems = refs[len(keys) + 1:]
        x, y, c = _place()
        me = 4 * x + 2 * y + c
        all_ref[me] = jnp.zeros(REDUCE_SLOT, F32)
        for k, r0, nr, wd in SMALL_REDUCE:
            all_ref[me, r0:r0 + nr, 0:wd] = ins[k][...]
        peers = []
        for rel in range(1, N_DEV):
            dx, dy, dc = (rel >> 2) & 1, (rel >> 1) & 1, rel & 1
            peers.append((1 - x if dx else x, 1 - y if dy else y, 1 - c if dc else c))
        sends = [_remote(all_ref.at[me], all_ref.at[me], send_sems.at[k], recv_sems.at[k], peer) for k, peer in enumerate(peers)]
        for cp in sends:
            cp.start()
        for k, (px, py, pc) in enumerate(peers):
            _remote(all_ref.at[me], all_ref.at[4 * px + 2 * py + pc], send_sems.at[k], recv_sems.at[k], (px, py, pc)).wait_recv()
        for cp in sends:
            cp.wait_send()

        def total(r0, nr, wd):
            acc = all_ref[0, r0:r0 + nr, 0:wd]
            for d in range(1, N_DEV):
                acc = acc + all_ref[d, r0:r0 + nr, 0:wd]
            return acc

        pn_o[0:1, :] = total(0, 1, 1024)
        pn_o[1:2, :] = total(1, 1, 1024)
        ps_o[0:1, :] = total(2, 1, 2048)
        ps_o[1:2, :] = total(3, 1, 2048)
        cn_o[...] = total(4, 1, 1024)
        mn_o[...] = total(5, 1, 1024)
        qn_o[...] = total(6, 1, Q_LORA)
        kn_o[...] = total(7, 1, KV_LORA)
        cw_o[0] = total(8, 3, 2048)
        fn_o[...] = total(16, 1, 1024)

    sds = lambda *shape: jax.ShapeDtypeStruct(shape, F32)
    out = pl.pallas_call(
        body, name=name,
        out_shape=(sds(2, 1024), sds(2, 2048), sds(1, 1024), sds(1, 3, 2048), sds(1, 1024), sds(1, Q_LORA), sds(1, KV_LORA), sds(1, 1024)),
        in_specs=[VMEM_SPEC] * len(keys) + [ANY_SPEC], out_specs=(VMEM_SPEC,) * 8,
        scratch_shapes=[pltpu.VMEM((N_DEV,) + REDUCE_SLOT, F32), pltpu.SemaphoreType.DMA((N_DEV - 1,)), pltpu.SemaphoreType.DMA((N_DEV - 1,))],
    )(*[parts[k] for k in keys], after)
    return dict(zip(("pool_norm", "pool_scale", "conv_norm", "conv_w", "mla_norm", "mla_q_norm", "mla_kv_norm", "final_norm"), out))


def _adam_math(w, g, m, v):
    m_new = ADAM_B1 * m + (1.0 - ADAM_B1) * g
    v_new = ADAM_B2 * v + (1.0 - ADAM_B2) * (g * g)
    m_hat = m_new / (1.0 - ADAM_B1 ** ADAM_STEP)
    v_hat = v_new / (1.0 - ADAM_B2 ** ADAM_STEP)
    return -ADAM_LR * (m_hat / (jnp.sqrt(v_hat) + ADAM_EPS) + ADAM_WD * w), m_new, v_new


def _adamw_small(w, m, v, g_full, chip, *, name):
    shp = {n: w[n].shape for n in SMALL}
    whole = lambda s: pl.BlockSpec(s, lambda i, c: (0,) * len(s))
    g_in, g_specs = {}, {}
    for n in SMALL:
        if not SMALL_SHARDED[n]:
            g_in[n], g_specs[n] = g_full[n].reshape(shp[n]), whole(shp[n])
        elif shp[n][-1] % LANES:
            g_in[n] = g_full[n].reshape(N_CHIPS, 1, shp[n][-1])
            g_specs[n] = pl.BlockSpec((None,) + shp[n], lambda i, c: (c[0], 0, 0))
        else:
            g_in[n] = g_full[n]
            nd = len(shp[n])
            g_specs[n] = pl.BlockSpec(shp[n], lambda i, c, nd=nd: (0,) * (nd - 1) + (c[0],))

    def body(c_ref, *refs):
        k = len(SMALL)
        w_r, m_r, v_r, g_r = refs[0:k], refs[k:2 * k], refs[2 * k:3 * k], refs[3 * k:4 * k]
        go_r, d_r, nm_r, nv_r = refs[4 * k:5 * k], refs[5 * k:6 * k], refs[6 * k:7 * k], refs[7 * k:8 * k]
        for i in range(k):
            gv = g_r[i][...]
            d, m_new, v_new = _adam_math(w_r[i][...], gv, m_r[i][...], v_r[i][...])
            go_r[i][...] = gv
            d_r[i][...] = d
            nm_r[i][...] = m_new
            nv_r[i][...] = v_new

    nat = [whole(shp[n]) for n in SMALL]
    out_sds = tuple(jax.ShapeDtypeStruct(shp[n], F32) for n in SMALL)
    out = pl.pallas_call(
        body, name=name, out_shape=out_sds * 4,
        grid_spec=pltpu.PrefetchScalarGridSpec(
            num_scalar_prefetch=1, grid=(1,),
            in_specs=nat * 3 + [g_specs[n] for n in SMALL], out_specs=tuple(nat) * 4),
        compiler_params=_cparams("arbitrary"),
    )(chip, *[w[n] for n in SMALL], *[m[n] for n in SMALL], *[v[n] for n in SMALL], *[g_in[n] for n in SMALL])
    k = len(SMALL)
    return tuple(dict(zip(SMALL, out[j * k:(j + 1) * k])) for j in range(4))


BIG = ("pool_w_in", "pool_w_grp", "pool_w_out", "conv_w_in", "conv_w_out", "mla_w_in", "mla_w_q_up", "mla_w_kv_up", "mla_w_out")
BIG_SHARD_AXIS = {"pool_w_in": 2, "pool_w_grp": 2, "pool_w_out": 1, "conv_w_in": 2, "conv_w_out": 1,
                  "mla_w_in": 2, "mla_w_q_up": 2, "mla_w_kv_up": 2, "mla_w_out": 1}
GATHER_LAYOUT = {
    "p0": ((("pool_w_in", 0), 0, "cols"), (("pool_w_out", 0), 1024, "rows"), (("pool_w_grp", 0), 1536, "flat")),
    "cv": ((("conv_w_in", 0), 0, "cols"), (("conv_w_out", 0), 2048, "rows")),
    "ml": ((("pool_w_in", 1), 0, "cols"), (("mla_w_out", 0), 1024, "rows"), (("pool_w_out", 1), 1536, "rows"),
           (("mla_w_kv_up", 0), 2048, "cols"), (("pool_w_grp", 1), 2304, "flat"), (("mla_w_q_up", 0), 2560, "flat"),
           (("mla_w_in", 0), 2848, "flat")),
}
REDUCE_LAYOUT = {
    "late": ((("conv_w_in", 0), 0, "cols"), (("pool_w_in", 1), 2048, "cols"), (("conv_w_out", 0), 3072, "rows"),
             (("mla_w_out", 0), 3584, "rows"), (("pool_w_out", 1), 4096, "rows"), (("mla_w_kv_up", 0), 4608, "cols"),
             (("pool_w_grp", 1), 4864, "flat"), (("mla_w_q_up", 0), 5120, "flat"), (("mla_w_in", 0), 5408, "flat")),
    "first": ((("pool_w_in", 0), 0, "cols"), (("pool_w_out", 0), 1024, "rows"), (("pool_w_grp", 0), 1536, "flat")),
}
PACK_ROW_ALIGN = 32
RS_ROW_ALIGN = 512


def _slot_rows(layout, shard_shape, align):
    where, end = {}, 0
    for piece, r0, kind in layout:
        n = 1
        for d in shard_shape(piece):
            n *= d
        assert r0 >= end and n % PACK_COLS == 0, (piece, r0, end)
        where[piece] = (r0, n // PACK_COLS, kind)
        end = r0 + n // PACK_COLS
    return end + (-end) % align, where


def _as_slot_rows(shard, kind):
    if kind == "cols":
        k, n = shard.shape
        return shard.reshape(k, n // PACK_COLS, PACK_COLS).swapaxes(0, 1).reshape(-1, PACK_COLS)
    return shard.reshape(-1, PACK_COLS)


def _pack_slot(shards, layout, rows, dtype):
    parts, end = [], 0
    for piece, r0, kind in layout:
        if r0 > end:
            parts.append(jnp.zeros((r0 - end, PACK_COLS), dtype))
        parts.append(_as_slot_rows(shards[piece], kind).astype(dtype))
        end = r0 + parts[-1].shape[0]
    if rows > end:
        parts.append(jnp.zeros((rows - end, PACK_COLS), dtype))
    return jnp.concatenate(parts, axis=0)


SMALL = ("pool_norm", "pool_scale", "conv_norm", "conv_w", "mla_norm", "mla_q_norm", "mla_kv_norm", "final_norm")
SMALL_SHARDED = {"pool_norm": True, "pool_scale": True, "conv_norm": False, "conv_w": True, "mla_norm": True,
                 "mla_q_norm": True, "mla_kv_norm": True, "final_norm": False}


def _rope_tables(positions):
    inv_freq = ROPE_BASE ** (-jnp.arange(0, QK_ROPE, 2, dtype=F32) / QK_ROPE)
    ang = positions.astype(F32).reshape(-1, 1) * inv_freq
    cos, sin = jnp.cos(ang), jnp.sin(ang)
    z32 = jnp.zeros_like(cos)
    z64 = jnp.concatenate([z32, z32], axis=1)
    return (jnp.concatenate([cos, cos, z64], axis=1), jnp.concatenate([-sin, z32, z64], axis=1),
            jnp.concatenate([z32, sin, z64], axis=1))


def _mla_in_to_padded(w):
    q, kv, kr, z = w[:, :Q_LORA], w[:, Q_LORA:Q_LORA + KV_LORA], w[:, Q_LORA + KV_LORA:Q_LORA + KV_LORA + QK_ROPE], w[:, Q_LORA + KV_LORA + QK_ROPE:]
    return jnp.concatenate([z, kv, q, kr, jnp.zeros((w.shape[0], MLA_IN_PAD - MLA_IN), w.dtype)], axis=1)


def _mla_in_from_padded(w):
    z, kv, q, kr = w[:, :D_INNER], w[:, D_INNER:D_INNER + KV_LORA], w[:, D_INNER + KV_LORA:D_INNER + KV_LORA + Q_LORA], w[:, D_INNER + KV_LORA + Q_LORA:D_INNER + KV_LORA + Q_LORA + QK_ROPE]
    return jnp.concatenate([q, kv, kr, z], axis=1)


def _q_up_to_padded(w):
    k = w.shape[0]
    return jnp.pad(w.reshape(k, N_HEADS, QK_NOPE + QK_ROPE), ((0, 0), (0, 0), (0, HEAD_PAD - QK_NOPE - QK_ROPE))).reshape(k, N_HEADS * HEAD_PAD)


def _q_up_from_padded(w):
    k = w.shape[0]
    return w.reshape(k, N_HEADS, HEAD_PAD)[:, :, :QK_NOPE + QK_ROPE].reshape(k, N_HEADS * (QK_NOPE + QK_ROPE))


def _local_step(x, positions, target, weights_for, ws, sink):
    S = x.shape[0]
    tm = min(512, S)
    te = min(256, S)
    tq = min(512, S)
    tabs = _rope_tables(positions)
    gs = {}

    def mm_in(xn, w, name):
        n = w.shape[1]
        tn = PACK_COLS if isinstance(w, Packed) else _pick(n, 1536 if n == MLA_IN_PAD else 1024)
        return _mm(xn, w, tm=min(1024, S), tn=tn, tk=D_MODEL, name=name)

    def mm_out(y, w, res, name):
        return _mm(y, w, residual=res, tm=tm, tn=D_MODEL, tk=D_INNER, name=name)

    def mm_dx(dy, w, name, after=None):
        k, n = w.shape
        if isinstance(w, Packed):
            tn, tk = (k if w.kind == "rows" else min(k, 1024)), PACK_COLS
        else:
            tn, tk = _pick(k, 1024), _pick(n, 1408)
        return _mm(dy, w, trans_b=True, after=after, tm=min(1024, S), tn=tn, tk=tk, name=name)

    def mm_dw(piece, a, b, name, after=None, post=None):
        ka, nb = a.shape[1], b.shape[1]
        into = sink.dest(piece)
        tokens = min(1024, S)
        if into is None:
            out = _mm(a, b, trans_a=True, out_dtype=BF16, after=after, tm=_pick(ka, 1024), tn=_pick(nb, 1408), tk=tokens, name=name)
            sink.put(piece, out if post is None else post(out))
        else:
            rows = ka if into.kind == "rows" else min(ka, 1024)
            sink.put(piece, _mm(a, b, trans_a=True, after=after, into=into, tm=rows, tn=PACK_COLS, tk=tokens, name=name))

    def pool_layer_fwd(xin, xn, wts, j, tag):
        h = mm_in(xn, wts[("pool_w_in", j)], f"{tag}_in")
        y = _pool_fwd(h, wts[("pool_w_grp", j)], ws["pool_scale"][j:j + 1], tm=te, name=f"{tag}_mix")
        xo = mm_out(y, wts[("pool_w_out", j)], xin, f"{tag}_out")
        return xo, (xin, xn, h, y)

    def pool_layer_bwd(dx, dxb, saved, wts, j, tag, after=None):
        xin, xn, h, y = saved
        dy = mm_dx(dxb, wts[("pool_w_out", j)], f"{tag}_dy", after)
        mm_dw(("pool_w_out", j), y, dxb, f"{tag}_dwo", after)
        pooled, dmixed, dpooled, dz, dsc = _pool_bwd1(h, dy, wts[("pool_w_grp", j)], ws["pool_scale"][j:j + 1], tm=te, name=f"{tag}_bmix")
        sink.put(("pool_w_grp", j), _grouped_tn(pooled, dmixed, tk=tm, name=f"{tag}_dwg"))
        dh = _pool_bwd2(dpooled, dz, tm=te, name=f"{tag}_bshift")
        dxn = mm_dx(dh, wts[("pool_w_in", j)], f"{tag}_dxn")
        mm_dw(("pool_w_in", j), xn, dh, f"{tag}_dwi")
        dxo, dxob, dg = _rms_bwd(xin, ws["pool_norm"][j:j + 1], dxn, dx, tm=tm, name=f"{tag}_bnorm")
        gs[f"pool_norm_{j}"], gs[f"pool_scale_{j}"] = dg, dsc
        return dxo, dxob

    xn0 = _rms_fwd(x, ws["pool_norm"][0:1], tm=tm, name="p0_norm")
    w_p0 = weights_for("p0", xn0)
    x1, sv0 = pool_layer_fwd(x, xn0, w_p0, 0, "p0")

    w_cv = weights_for("cv", x1)
    xn1 = _rms_fwd(x1, ws["conv_norm"][0:1], tm=tm, name="cv_norm")
    h1 = mm_in(xn1, w_cv[("conv_w_in", 0)], "cv_in")
    cw = jnp.pad(ws["conv_w"][0], ((0, 5), (0, 0)))
    y1 = _conv_fwd(h1, cw, tm=te, name="cv_mix")
    x2 = mm_out(y1, w_cv[("conv_w_out", 0)], x1, "cv_out")

    w_ml = weights_for("ml", x2)
    w_mi = _mla_in_to_padded(w_ml[("mla_w_in", 0)])
    w_q = _q_up_to_padded(w_ml[("mla_w_q_up", 0)])
    w_kv = w_ml[("mla_w_kv_up", 0)]
    qg, kvg = ws["mla_q_norm"][0:1], ws["mla_kv_norm"][0:1]
    xn2 = _rms_fwd(x2, ws["mla_norm"][0:1], tm=tm, name="ml_norm")
    h2 = mm_in(xn2, w_mi, "ml_in")
    q_n, kv_n, krr = _mla_latent_fwd(h2, qg, kvg, tabs, tm=tm, name="ml_lat")
    q_full = _mla_q_up(q_n, w_q, tabs, tm=tm, name="ml_qup")
    k_full, v = _mla_kv_up(kv_n, w_kv, krr, tm=tm, name="ml_kvup")
    o, y2, lse = _flash_fwd(q_full, k_full, v, h2, tq=tq, name="ml_attn")
    x3 = mm_out(y2, w_ml[("mla_w_out", 0)], x2, "ml_out")

    x4, sv3 = pool_layer_fwd(x3, _rms_fwd(x3, ws["pool_norm"][1:2], tm=tm, name="p1_norm"), w_ml, 1, "p1")

    loss_part, dx, dxb, dgf = _final_loss(x4, ws["final_norm"].reshape(1, -1), target, tm=tm, name="final")
    gs["final_norm"] = dgf

    dx, dxb = pool_layer_bwd(dx, dxb, sv3, w_ml, 1, "p1")

    dy = mm_dx(dxb, w_ml[("mla_w_out", 0)], "ml_dy")
    mm_dw(("mla_w_out", 0), y2, dxb, "ml_dwo")
    do, dz, delta = _mla_gate_bwd(dy, o, h2, tm=tq, name="ml_bgate")
    dq_pre, dkv, dkr = _flash_bwd(q_full, k_full, v, do, lse, delta, tabs, tq=tq, name="ml_battn")
    dq_n = mm_dx(dq_pre, w_q, "ml_dqn")
    mm_dw(("mla_w_q_up", 0), q_n, dq_pre, "ml_dwq", post=_q_up_from_padded)
    dkv_n = mm_dx(dkv, w_kv, "ml_dkvn")
    mm_dw(("mla_w_kv_up", 0), kv_n, dkv, "ml_dwkv")
    dh2, dqg, dkvg = _mla_latent_bwd(h2, dq_n, dkv_n, dkr, dz, qg, kvg, tabs, tm=te, name="ml_blat")
    dxn2 = mm_dx(dh2, w_mi, "ml_dxn")
    mm_dw(("mla_w_in", 0), xn2, dh2, "ml_dwi", post=_mla_in_from_padded)
    dx, dxb, dg2 = _rms_bwd(x2, ws["mla_norm"][0:1], dxn2, dx, tm=tm, name="ml_bnorm")
    gs["mla_norm"], gs["mla_q_norm"], gs["mla_kv_norm"] = dg2, dqg, dkvg

    dy = mm_dx(dxb, w_cv[("conv_w_out", 0)], "cv_dy")
    mm_dw(("conv_w_out", 0), y1, dxb, "cv_dwo")
    dh1, dcw = _conv_bwd(h1, dy, cw, tm=te, name="cv_bmix")
    dxn1 = mm_dx(dh1, w_cv[("conv_w_in", 0)], "cv_dxn")
    mm_dw(("conv_w_in", 0), xn1, dh1, "cv_dwi")
    dx, dxb, dg1 = _rms_bwd(x1, ws["conv_norm"][0:1], dxn1, dx, tm=tm, name="cv_bnorm")
    gs["conv_norm"], gs["conv_w"] = dg1, dcw

    dx, dxb = pool_layer_bwd(dx, dxb, sv0, w_p0, 0, "p0", after=sink.late_ready())
    return loss_part, dx, gs


def kernel(x, positions, pool_norm, pool_w_in, pool_w_grp, pool_scale, pool_w_out, conv_norm, conv_w_in, conv_w, conv_w_out, mla_norm, mla_w_in, mla_q_norm, mla_w_q_up, mla_kv_norm, mla_w_kv_up, mla_w_out, final_norm, loss_target, m_pool_norm, m_pool_w_in, m_pool_w_grp, m_pool_scale, m_pool_w_out, m_conv_norm, m_conv_w_in, m_conv_w, m_conv_w_out, m_mla_norm, m_mla_w_in, m_mla_q_norm, m_mla_w_q_up, m_mla_kv_norm, m_mla_w_kv_up, m_mla_w_out, m_final_norm, v_pool_norm, v_pool_w_in, v_pool_w_grp, v_pool_scale, v_pool_w_out, v_conv_norm, v_conv_w_in, v_conv_w, v_conv_w_out, v_mla_norm, v_mla_w_in, v_mla_q_norm, v_mla_w_q_up, v_mla_kv_norm, v_mla_w_kv_up, v_mla_w_out, v_final_norm):
    names = ("pool_norm", "pool_w_in", "pool_w_grp", "pool_scale", "pool_w_out", "conv_norm", "conv_w_in", "conv_w", "conv_w_out",
             "mla_norm", "mla_w_in", "mla_q_norm", "mla_w_q_up", "mla_kv_norm", "mla_w_kv_up", "mla_w_out", "final_norm")
    w = dict(zip(names, (pool_norm, pool_w_in, pool_w_grp, pool_scale, pool_w_out, conv_norm, conv_w_in, conv_w, conv_w_out,
                         mla_norm, mla_w_in, mla_q_norm, mla_w_q_up, mla_kv_norm, mla_w_kv_up, mla_w_out, final_norm)))
    m = dict(zip(names, (m_pool_norm, m_pool_w_in, m_pool_w_grp, m_pool_scale, m_pool_w_out, m_conv_norm, m_conv_w_in, m_conv_w, m_conv_w_out,
                         m_mla_norm, m_mla_w_in, m_mla_q_norm, m_mla_w_q_up, m_mla_kv_norm, m_mla_w_kv_up, m_mla_w_out, m_final_norm)))
    v = dict(zip(names, (v_pool_norm, v_pool_w_in, v_pool_w_grp, v_pool_scale, v_pool_w_out, v_conv_norm, v_conv_w_in, v_conv_w, v_conv_w_out,
                         v_mla_norm, v_mla_w_in, v_mla_q_norm, v_mla_w_q_up, v_mla_kv_norm, v_mla_w_kv_up, v_mla_w_out, v_final_norm)))
    chip = 2 * lax.axis_index("x") + lax.axis_index("y")
    core = lax.axis_index("c")

    core1 = core.astype(jnp.int32).reshape(1)
    chip_core = jnp.stack([chip, core]).astype(jnp.int32)
    shard_shape = lambda piece: w[piece[0]].shape[1:]
    shard_axis = lambda piece: BIG_SHARD_AXIS[piece[0]] - 1
    full_shape = lambda piece: tuple(d * (N_CHIPS if a == shard_axis(piece) else 1) for a, d in enumerate(shard_shape(piece)))

    gather_rows, gather_at, packs = {}, {}, {}
    for grp, layout in GATHER_LAYOUT.items():
        gather_rows[grp], gather_at[grp] = _slot_rows(layout, shard_shape, PACK_ROW_ALIGN)
        packs[grp] = _pack_slot({(n, j): w[n][j] for (n, j), _, _ in layout}, layout, gather_rows[grp], BF16)

    def gathered_weights(grp, gathered):
        out = {}
        for piece, (r0, n, kind) in gather_at[grp].items():
            if kind == "flat":
                out[piece] = jnp.concatenate([gathered[k, r0:r0 + n].reshape(shard_shape(piece)) for k in range(N_CHIPS)], axis=shard_axis(piece))
            else:
                out[piece] = Packed(gathered, r0, kind, full_shape(piece))
        return out

    p0_start = _ag_ici_start(packs["p0"], w["final_norm"], name="ag_p0_start")
    cv_start = _ag_ici_start(packs["cv"], p0_start[4], name="ag_cv_start")
    ml_start = _ag_ici_start(packs["ml"], cv_start[4], name="ag_ml_start")
    in_flight = {"p0": p0_start, "cv": cv_start, "ml": ml_start}

    def weights_for(grp, after):
        if grp == "p0":
            after = (after, ml_start[4])
        send_sems, recv_sems, w_thru, land, _ = in_flight[grp]
        w_thru, land = _ag_ici_wait(send_sems, recv_sems, w_thru, land, after, name=f"ag_{grp}_wait")
        return gathered_weights(grp, _ag_sibling_forward(land, w_thru, name=f"ag_{grp}_fwd"))

    ws = {"conv_norm": w["conv_norm"], "final_norm": w["final_norm"]}
    ws.update(_gather_small({n: w[n] for n, _, _ in SMALL_GATHER}, name="ag_small"))

    reduce_rows, reduce_at = {}, {}
    for grp, layout in REDUCE_LAYOUT.items():
        reduce_rows[grp], reduce_at[grp] = _slot_rows(layout, shard_shape, RS_ROW_ALIGN)
    group_of = {piece: grp for grp, layout in REDUCE_LAYOUT.items() for piece, _, _ in layout}

    class Sink:
        def __init__(self):
            self.buf = {grp: lax.empty((N_CHIPS, rows, PACK_COLS), BF16) for grp, rows in reduce_rows.items()}
            self.started = {}

        def dest(self, piece):
            grp = group_of[piece]
            r0, _, kind = reduce_at[grp][piece]
            return None if kind == "flat" else Packed(self.buf[grp], r0, kind, full_shape(piece))

        def put(self, piece, result):
            grp = group_of[piece]
            r0, n, kind = reduce_at[grp][piece]
            if kind == "flat":
                parts = jnp.split(result, N_CHIPS, axis=shard_axis(piece))
                result = lax.dynamic_update_slice(self.buf[grp], jnp.stack([p.reshape(n, PACK_COLS) for p in parts]), (0, r0, 0))
            self.buf[grp] = result

        def start(self, grp, tag):
            theirs = _rs_sibling_swap(self.buf[grp], name=f"{tag}_swap")
            chip_sum = _add2_bf16(self.buf[grp], theirs, core1, name=f"{tag}_add2")
            self.started[grp] = _rs_chip_exchange_start(chip_sum, name=f"{tag}_chips_start")
            return self.started[grp][4]

        def finish(self, grp, after, tag):
            send_sems, recv_sems, chip_sum, land, _ = self.started[grp]
            chip_sum, recv = _rs_chip_exchange_wait(send_sems, recv_sems, chip_sum, land, after, name=f"{tag}_chips_wait")
            half_sum = _add4_f32(chip_sum, recv, chip_core, name=f"{tag}_add4")
            return _rs_sibling_join(half_sum, name=f"{tag}_join")

        def late_ready(self):
            return self.start("late", "rsa")

    sink = Sink()

    loss_part, grad_x, gs = _local_step(x[0], positions, loss_target[0], weights_for, ws, sink)
    loss = lax.psum(loss_part[0, 0], ("x", "y", "c"))

    g, delta, new_m, new_v = {}, {}, {}, {}

    def adam_big(n):
        nj = w[n].shape[0]
        where = [(group_of[(n, j)],) + reduce_at[group_of[(n, j)]][(n, j)] for j in range(nj)]
        if where[0][3] == "flat":
            g[n] = jnp.stack([g_rows[grp][r0:r0 + rows].reshape(w[n].shape[1:]) for grp, r0, rows, _ in where])
            shp = w[n].shape
            two_d = lambda a: a.reshape(-1, shp[-1])
            d_, m_, v_ = _adamw(two_d(w[n]), two_d(g[n]), two_d(m[n]), two_d(v[n]), name=f"adamw_{n}")
            delta[n], new_m[n], new_v[n] = d_.reshape(shp), m_.reshape(shp), v_.reshape(shp)
        else:
            g[n], delta[n], new_m[n], new_v[n] = _adamw_rows(w[n], m[n], v[n], [(g_rows[grp], r0) for grp, r0, _, _ in where], name=f"adamw_{n}")

    first_token = sink.start("first", "rsb")
    g_rows = {"late": sink.finish("late", first_token, "rsa")}
    late_only = [n for n in BIG if all(group_of[(n, j)] == "late" for j in range(w[n].shape[0]))]
    for n in late_only:
        adam_big(n)
    g_rows["first"] = sink.finish("first", tuple(delta[n] for n in late_only), "rsb")
    for n in BIG:
        if n not in late_only:
            adam_big(n)

    gs_sum = _reduce_small(gs, g_rows["first"], name="ar_small")
    row = lambda d: {n: (d[n].reshape(1, -1) if d[n].ndim == 1 else d[n]) for n in SMALL}
    small_out = _adamw_small(row(w), row(m), row(v), gs_sum, chip.astype(jnp.int32).reshape(1), name="adamw_small")
    for dst, res in zip((g, delta, new_m, new_v), small_out):
        for n in SMALL:
            dst[n] = res[n].reshape(w[n].shape)

    return (loss, grad_x[None], *[g[n] for n in names], *[delta[n] for n in names],
            *[new_m[n] for n in names], *[new_v[n] for n in names])
```

```python
import functools

import jax
import jax.numpy as jnp
from jax import lax
from jax.experimental import pallas as pl
from jax.experimental.pallas import tpu as pltpu

F32 = jnp.float32
BF16 = jnp.bfloat16

D_MODEL = 1024
D_INNER = 2048
POOL_WINDOWS = (2, 4, 8, 16)
POOL_GROUP = 512
N_HEADS = 16
QK_NOPE = 128
QK_ROPE = 64
V_DIM = 128
HEAD_PAD = 256
Q_LORA = 384
KV_LORA = 256
MLA_IN = Q_LORA + KV_LORA + QK_ROPE + D_INNER
MLA_IN_PAD = 2816
ATTN_SCALE = (QK_NOPE + QK_ROPE) ** -0.5
ROPE_BASE = 10000.0
NORM_EPS = 1e-6
HALO = 16

ADAM_LR = 0.001
ADAM_B1 = 0.9
ADAM_B2 = 0.999
ADAM_EPS = 1e-08
ADAM_WD = 0.01
ADAM_STEP = 10

N_CHIPS = 4
N_DEV = 8
LANES = 128
PACK_COLS = 1024
V7X_VMEM_LIMIT = 56 * 1024 * 1024
MESH = pl.DeviceIdType.MESH


def _cparams(*sem):
    return pltpu.CompilerParams(dimension_semantics=sem, vmem_limit_bytes=V7X_VMEM_LIMIT)


def _pick(n, cap):
    best = None
    for d in range(LANES, min(n, cap) + 1, LANES):
        if n % d == 0:
            best = d
    assert best is not None, (n, cap)
    return best


def _sigmoid(z):
    return 1.0 / (1.0 + jnp.exp(-z))


class Packed:
    def __init__(self, buf, r0, kind, shape):
        self.buf, self.r0, self.kind, self.shape = buf, r0, kind, shape

    def block(self, rb, cb, bk):
        assert self.r0 % bk == 0, (self.r0, bk)
        if self.kind == "cols":
            K = self.shape[0]
            per = self.shape[1] // (N_CHIPS * PACK_COLS)
            assert K % bk == 0
            return cb // per, (self.r0 + (cb % per) * K) // bk + rb
        kk = self.shape[0] // N_CHIPS
        assert kk % bk == 0
        per = kk // bk
        return rb // per, self.r0 // bk + rb % per


def _mm(a, b, *, trans_a=False, trans_b=False, out_dtype=F32, residual=None, after=None, into=None, tm, tn, tk, name):
    if trans_a:
        K, M = a.shape
    else:
        M, K = a.shape
    if trans_b:
        N, K2 = b.shape
    else:
        K2, N = b.shape
    assert K == K2 and M % tm == 0 and N % tn == 0 and K % tk == 0, (name, a.shape, b.shape, tm, tn, tk)
    nk = K // tk
    dn = (((0 if trans_a else 1,), (1 if trans_b else 0,)), ((), ()))
    has_res = residual is not None
    n_skip = (after is not None) + (into is not None)
    b_all = isinstance(b, Packed) and b.kind == "rows"
    o_all = into is not None and into.kind == "rows"
    assert not b_all or (not trans_a and (tn == N if trans_b else tk == K)), name
    assert not o_all or tm == M, name

    def body(*refs):
        if has_res:
            a_ref, b_ref, r_ref = refs[:3]
            refs = refs[3:]
        else:
            a_ref, b_ref = refs[:2]
            r_ref = None
            refs = refs[2:]
        refs = refs[n_skip:]
        o_ref, rest = refs[0], refs[1:]
        if b_all and trans_b:
            part = jnp.concatenate([lax.dot_general(a_ref[...], b_ref[c], dn, preferred_element_type=F32) for c in range(N_CHIPS)], axis=1)
        elif b_all:
            kk = K // N_CHIPS
            part = lax.dot_general(a_ref[:, 0:kk], b_ref[0], dn, preferred_element_type=F32)
            for c in range(1, N_CHIPS):
                part = part + lax.dot_general(a_ref[:, c * kk:(c + 1) * kk], b_ref[c], dn, preferred_element_type=F32)
        else:
            part = lax.dot_general(a_ref[...], b_ref[...], dn, preferred_element_type=F32)

        def finish(acc):
            if has_res:
                acc = acc + r_ref[...]
            if o_all:
                mk = M // N_CHIPS
                for c in range(N_CHIPS):
                    o_ref[c] = acc[c * mk:(c + 1) * mk].astype(o_ref.dtype)
            else:
                o_ref[...] = acc.astype(o_ref.dtype)

        if nk == 1:
            finish(part)
        else:
            acc_ref = rest[0]
            k = pl.program_id(2)

            @pl.when(k == 0)
            def _():
                acc_ref[...] = part

            @pl.when(k > 0)
            def _():
                acc_ref[...] += part

            @pl.when(k == nk - 1)
            def _():
                finish(acc_ref[...])

    a_spec = pl.BlockSpec((tk, tm), lambda i, j, k: (k, i)) if trans_a else pl.BlockSpec((tm, tk), lambda i, j, k: (i, k))
    if b_all:
        kkb = b.shape[0] // N_CHIPS
        b_spec = pl.BlockSpec((N_CHIPS, kkb, PACK_COLS), lambda i, j, k: (0, b.r0 // kkb, 0))
        b_arg = b.buf
    elif isinstance(b, Packed):
        if trans_b:
            assert tk == PACK_COLS
            b_spec = pl.BlockSpec((None, tn, tk), lambda i, j, k: (*b.block(j, k, tn), 0))
        else:
            assert tn == PACK_COLS
            b_spec = pl.BlockSpec((None, tk, tn), lambda i, j, k: (*b.block(k, j, tk), 0))
        b_arg = b.buf
    else:
        b_spec = pl.BlockSpec((tn, tk), lambda i, j, k: (j, k)) if trans_b else pl.BlockSpec((tk, tn), lambda i, j, k: (k, j))
        b_arg = b
    o_spec = pl.BlockSpec((tm, tn), lambda i, j, k: (i, j))
    in_specs = [a_spec, b_spec] + ([o_spec] if has_res else [])
    args = (a, b_arg) + ((residual,) if has_res else ())
    aliases = {}
    if after is not None:
        in_specs.append(pl.BlockSpec(memory_space=pl.ANY))
        args += (after,)
    if into is None:
        out_shape, out_spec = jax.ShapeDtypeStruct((M, N), out_dtype), o_spec
    else:
        assert tn == PACK_COLS and into.shape == (M, N)
        in_specs.append(pl.BlockSpec(memory_space=pl.ANY))
        aliases = {len(args): 0}
        args += (into.buf,)
        out_shape = jax.ShapeDtypeStruct(into.buf.shape, into.buf.dtype)
        if o_all:
            out_spec = pl.BlockSpec((N_CHIPS, M // N_CHIPS, tn), lambda i, j, k: (0, into.r0 // (M // N_CHIPS), 0))
        else:
            out_spec = pl.BlockSpec((None, tm, tn), lambda i, j, k: (*into.block(i, j, tm), 0))
    return pl.pallas_call(
        body, name=name, out_shape=out_shape,
        grid=(M // tm, N // tn, nk),
        in_specs=in_specs, out_specs=out_spec, input_output_aliases=aliases,
        scratch_shapes=[pltpu.VMEM((tm, tn), F32)] if nk > 1 else [],
        compiler_params=_cparams("parallel", "parallel", "arbitrary"),
    )(*args)


def _grouped_tn(a, b, *, tk, name):
    S = a.shape[0]
    G = POOL_GROUP
    nk = S // tk

    def body(a_ref, b_ref, o_ref, acc_ref):
        k = pl.program_id(1)
        part = lax.dot_general(a_ref[...], b_ref[...], (((0,), (0,)), ((), ())), preferred_element_type=F32)

        @pl.when(k == 0)
        def _():
            acc_ref[...] = part

        @pl.when(k > 0)
        def _():
            acc_ref[...] += part

        @pl.when(k == nk - 1)
        def _():
            o_ref[...] = acc_ref[...].astype(o_ref.dtype)

    return pl.pallas_call(
        body, name=name,
        out_shape=jax.ShapeDtypeStruct((len(POOL_WINDOWS), G, G), BF16),
        grid=(len(POOL_WINDOWS), nk),
        in_specs=[pl.BlockSpec((tk, G), lambda g, k: (k, g)), pl.BlockSpec((tk, G), lambda g, k: (k, g))],
        out_specs=pl.BlockSpec((None, G, G), lambda g, k: (g, 0, 0)),
        scratch_shapes=[pltpu.VMEM((G, G), F32)],
        compiler_params=_cparams("parallel", "arbitrary"),
    )(a, b)


def _rms_fwd(x, g, *, tm, name):
    S, D = x.shape

    def body(x_ref, g_ref, o_ref):
        xv = x_ref[...]
        rstd = lax.rsqrt(jnp.mean(xv * xv, axis=-1, keepdims=True) + NORM_EPS)
        o_ref[...] = (xv * rstd * g_ref[...]).astype(o_ref.dtype)

    return pl.pallas_call(
        body, name=name,
        out_shape=jax.ShapeDtypeStruct((S, D), BF16),
        grid=(S // tm,),
        in_specs=[pl.BlockSpec((tm, D), lambda i: (i, 0)), pl.BlockSpec((1, D), lambda i: (0, 0))],
        out_specs=pl.BlockSpec((tm, D), lambda i: (i, 0)),
        compiler_params=_cparams("parallel"),
    )(x, g)


def _rms_bwd_math(xv, gv, dxn):
    rstd = lax.rsqrt(jnp.mean(xv * xv, axis=-1, keepdims=True) + NORM_EPS)
    xh = xv * rstd
    dg = jnp.sum(dxn * xh, axis=0, keepdims=True)
    dxh = dxn * gv
    dx = rstd * (dxh - xh * jnp.mean(dxh * xh, axis=-1, keepdims=True))
    return dx, dg


def _rms_bwd(x, g, dxn, dres, *, tm, name):
    S, D = x.shape

    def body(x_ref, g_ref, dxn_ref, dres_ref, dx_ref, dxb_ref, dg_ref):
        dx, dg = _rms_bwd_math(x_ref[...], g_ref[...], dxn_ref[...])
        dx = dx + dres_ref[...]
        dx_ref[...] = dx
        dxb_ref[...] = dx.astype(BF16)

        @pl.when(pl.program_id(0) == 0)
        def _():
            dg_ref[...] = dg

        @pl.when(pl.program_id(0) > 0)
        def _():
            dg_ref[...] += dg

    row = pl.BlockSpec((tm, D), lambda i: (i, 0))
    vec = pl.BlockSpec((1, D), lambda i: (0, 0))
    return pl.pallas_call(
        body, name=name,
        out_shape=(jax.ShapeDtypeStruct((S, D), F32), jax.ShapeDtypeStruct((S, D), BF16), jax.ShapeDtypeStruct((1, D), F32)),
        grid=(S // tm,),
        in_specs=[row, vec, row, row],
        out_specs=(row, row, vec),
        compiler_params=_cparams("arbitrary"),
    )(x, g, dxn, dres)


def _final_loss(x, g, target, *, tm, name):
    S, D = x.shape

    def body(x_ref, g_ref, t_ref, loss_ref, dx_ref, dxb_ref, dg_ref):
        xv = x_ref[...]
        gv = g_ref[...]
        rstd = lax.rsqrt(jnp.mean(xv * xv, axis=-1, keepdims=True) + NORM_EPS)
        xh = xv * rstd
        err = xh * gv - t_ref[...]
        part = 0.5 * jnp.sum(jnp.mean(err * err, axis=-1, keepdims=True), axis=0, keepdims=True)
        dy = err * (1.0 / D)
        dg = jnp.sum(dy * xh, axis=0, keepdims=True)
        dxh = dy * gv
        dx = rstd * (dxh - xh * jnp.mean(dxh * xh, axis=-1, keepdims=True))
        dx_ref[...] = dx
        dxb_ref[...] = dx.astype(BF16)
        lossb = jnp.broadcast_to(part, loss_ref.shape)

        @pl.when(pl.program_id(0) == 0)
        def _():
            dg_ref[...] = dg
            loss_ref[...] = lossb

        @pl.when(pl.program_id(0) > 0)
        def _():
            dg_ref[...] += dg
            loss_ref[...] += lossb

    row = pl.BlockSpec((tm, D), lambda i: (i, 0))
    vec = pl.BlockSpec((1, D), lambda i: (0, 0))
    lspec = pl.BlockSpec((8, LANES), lambda i: (0, 0))
    return pl.pallas_call(
        body, name=name,
        out_shape=(jax.ShapeDtypeStruct((8, LANES), F32), jax.ShapeDtypeStruct((S, D), F32),
                   jax.ShapeDtypeStruct((S, D), BF16), jax.ShapeDtypeStruct((1, D), F32)),
        grid=(S // tm,),
        in_specs=[row, vec, row],
        out_specs=(lspec, row, row, vec),
        compiler_params=_cparams("arbitrary"),
    )(x, g, target)


def _prev_halo_spec(tm, width, col):
    r = tm // HALO
    return pl.BlockSpec((HALO, width), lambda i: (jnp.maximum(i * r - 1, 0), col))


def _next_halo_spec(tm, width, col, S):
    r = tm // HALO
    last = S // HALO - 1
    return pl.BlockSpec((HALO, width), lambda i: (jnp.minimum((i + 1) * r, last), col))


def _shift_down(ext, k):
    return pltpu.roll(ext, k, 0)[HALO:, :]


def _shift_up(ext, k, tm):
    n = ext.shape[0]
    return pltpu.roll(ext, n - k, 0)[:tm, :]


def _pool_window_sum(ext, w):
    s = ext
    k = 1
    while k < w:
        s = s + pltpu.roll(s, k, 0)
        k *= 2
    return s[HALO:, :]


def _pooled_group(u_ref, halo, g, w, t_idx):
    cs = slice(g * POOL_GROUP, (g + 1) * POOL_GROUP)
    u = u_ref[:, cs]
    ext = jnp.concatenate([halo[:, cs], u], axis=0)
    inv = 1.0 / jnp.minimum(t_idx + 1, w).astype(F32)
    return _pool_window_sum(ext, w) * inv - u


def _pool_fwd(h, w_grp, scale, *, tm, name):
    S = h.shape[0]
    E = D_INNER

    def body(u_ref, uh_ref, z_ref, wg_ref, sc_ref, y_ref):
        i = pl.program_id(0)
        halo = jnp.where(i > 0, uh_ref[...], 0.0)
        t_idx = i * tm + lax.broadcasted_iota(jnp.int32, (tm, 1), 0)
        for g, w in enumerate(POOL_WINDOWS):
            cs = slice(g * POOL_GROUP, (g + 1) * POOL_GROUP)
            pooled = _pooled_group(u_ref, halo, g, w, t_idx)
            mixed = jnp.dot(pooled.astype(BF16), wg_ref[g], preferred_element_type=F32)
            z = z_ref[:, cs]
            y_ref[:, cs] = (mixed * sc_ref[:, cs] * (z * _sigmoid(z))).astype(BF16)

    return pl.pallas_call(
        body, name=name,
        out_shape=jax.ShapeDtypeStruct((S, E), BF16),
        grid=(S // tm,),
        in_specs=[pl.BlockSpec((tm, E), lambda i: (i, 0)), _prev_halo_spec(tm, E, 0),
                  pl.BlockSpec((tm, E), lambda i: (i, 1)),
                  pl.BlockSpec((len(POOL_WINDOWS), POOL_GROUP, POOL_GROUP), lambda i: (0, 0, 0)),
                  pl.BlockSpec((1, E), lambda i: (0, 0))],
        out_specs=pl.BlockSpec((tm, E), lambda i: (i, 0)),
        compiler_params=_cparams("parallel"),
    )(h, h, h, w_grp, scale)


def _pool_bwd1(h, dy, w_grp, scale, *, tm, name):
    S = h.shape[0]
    E = D_INNER

    def body(u_ref, uh_ref, z_ref, dy_ref, wg_ref, sc_ref, pooled_ref, dmixed_ref, dpooled_ref, dz_ref, dsc_ref):
        i = pl.program_id(0)
        halo = jnp.where(i > 0, uh_ref[...], 0.0)
        t_idx = i * tm + lax.broadcasted_iota(jnp.int32, (tm, 1), 0)
        for g, w in enumerate(POOL_WINDOWS):
            cs = slice(g * POOL_GROUP, (g + 1) * POOL_GROUP)
            pooled = _pooled_group(u_ref, halo, g, w, t_idx).astype(BF16)
            wg = wg_ref[g]
            mixed = jnp.dot(pooled, wg, preferred_element_type=F32)
            z = z_ref[:, cs]
            sg = _sigmoid(z)
            dyv = dy_ref[:, cs]
            sc = sc_ref[:, cs]
            dms = dyv * (z * sg)
            dz = dyv * (mixed * sc) * (sg * (1.0 + z * (1.0 - sg)))
            dsc = jnp.sum(dms * mixed, axis=0, keepdims=True)
            dmixed = (dms * sc).astype(BF16)
            dpooled = lax.dot_general(dmixed, wg, (((1,), (1,)), ((), ())), preferred_element_type=F32)
            pooled_ref[:, cs] = pooled
            dmixed_ref[:, cs] = dmixed
            dpooled_ref[:, cs] = dpooled
            dz_ref[:, cs] = dz.astype(BF16)

            @pl.when(i == 0)
            def _():
                dsc_ref[:, cs] = dsc

            @pl.when(i > 0)
            def _():
                dsc_ref[:, cs] += dsc

    row = pl.BlockSpec((tm, E), lambda i: (i, 0))
    vec = pl.BlockSpec((1, E), lambda i: (0, 0))
    return pl.pallas_call(
        body, name=name,
        out_shape=(jax.ShapeDtypeStruct((S, E), BF16), jax.ShapeDtypeStruct((S, E), BF16),
                   jax.ShapeDtypeStruct((S, E), F32), jax.ShapeDtypeStruct((S, E), BF16),
                   jax.ShapeDtypeStruct((1, E), F32)),
        grid=(S // tm,),
        in_specs=[row, _prev_halo_spec(tm, E, 0), pl.BlockSpec((tm, E), lambda i: (i, 1)), row,
                  pl.BlockSpec((len(POOL_WINDOWS), POOL_GROUP, POOL_GROUP), lambda i: (0, 0, 0)), vec],
        out_specs=(row, row, row, row, vec),
        compiler_params=_cparams("arbitrary"),
    )(h, h, h, dy, w_grp, scale)


def _pool_bwd2(dpooled, dz, *, tm, name):
    S = dpooled.shape[0]
    E = D_INNER
    nt = S // tm

    def body(dp_ref, dpn_ref, dz_ref, dh_ref):
        i = pl.program_id(0)
        nxt = jnp.where(i < nt - 1, dpn_ref[...], 0.0)
        t_ext = i * tm + lax.broadcasted_iota(jnp.int32, (tm + HALO, 1), 0)
        for g, w in enumerate(POOL_WINDOWS):
            cs = slice(g * POOL_GROUP, (g + 1) * POOL_GROUP)
            dp = dp_ref[:, cs]
            inv = 1.0 / jnp.minimum(t_ext + 1, w).astype(F32)
            s = jnp.concatenate([dp, nxt[:, cs]], axis=0) * inv
            n = tm + HALO
            k = 1
            while k < w:
                s = s + pltpu.roll(s, n - k, 0)
                k *= 2
            dh_ref[:, cs] = (s[:tm, :] - dp).astype(BF16)
        dh_ref[:, E:] = dz_ref[...]

    return pl.pallas_call(
        body, name=name,
        out_shape=jax.ShapeDtypeStruct((S, 2 * E), BF16),
        grid=(nt,),
        in_specs=[pl.BlockSpec((tm, E), lambda i: (i, 0)), _next_halo_spec(tm, E, 0, S),
                  pl.BlockSpec((tm, E), lambda i: (i, 0))],
        out_specs=pl.BlockSpec((tm, 2 * E), lambda i: (i, 0)),
        compiler_params=_cparams("parallel"),
    )(dpooled, dpooled, dz)


CONV_CHUNK = 512


def _conv_fwd(h, cw, *, tm, name):
    S = h.shape[0]
    E = D_INNER

    def body(b_ref, c_ref, hh_ref, z_ref, ch_ref, hhh_ref, w_ref, y_ref):
        i = pl.program_id(0)
        for j in range(E // CONV_CHUNK):
            cs = slice(j * CONV_CHUNK, (j + 1) * CONV_CHUNK)
            p = c_ref[:, cs] * hh_ref[:, cs]
            ph = jnp.where(i > 0, ch_ref[:, cs] * hhh_ref[:, cs], 0.0)
            ext = jnp.concatenate([ph, p], axis=0)
            conv = w_ref[2:3, cs] * p + w_ref[1:2, cs] * _shift_down(ext, 1) + w_ref[0:1, cs] * _shift_down(ext, 2)
            z = z_ref[:, cs]
            y_ref[:, cs] = (b_ref[:, cs] * conv * (z * _sigmoid(z))).astype(BF16)

    col = lambda c: pl.BlockSpec((tm, E), lambda i: (i, c))
    return pl.pallas_call(
        body, name=name,
        out_shape=jax.ShapeDtypeStruct((S, E), BF16),
        grid=(S // tm,),
        in_specs=[col(0), col(1), col(2), col(3), _prev_halo_spec(tm, E, 1), _prev_halo_spec(tm, E, 2),
                  pl.BlockSpec((8, E), lambda i: (0, 0))],
        out_specs=pl.BlockSpec((tm, E), lambda i: (i, 0)),
        compiler_params=_cparams("parallel"),
    )(h, h, h, h, h, h, cw)


def _conv_bwd(h, dy, cw, *, tm, name):
    S = h.shape[0]
    E = D_INNER
    nt = S // tm

    def body(b_ref, c_ref, hh_ref, z_ref, dy_ref, ch_ref, hhh_ref, bn_ref, zn_ref, dyn_ref, w_ref, dh_ref, dw_ref):
        i = pl.program_id(0)
        for j in range(E // CONV_CHUNK):
            cs = slice(j * CONV_CHUNK, (j + 1) * CONV_CHUNK)
            w0, w1, w2 = w_ref[0:1, cs], w_ref[1:2, cs], w_ref[2:3, cs]
            c, hh, b, z, dyv = c_ref[:, cs], hh_ref[:, cs], b_ref[:, cs], z_ref[:, cs], dy_ref[:, cs]
            p = c * hh
            ph = jnp.where(i > 0, ch_ref[:, cs] * hhh_ref[:, cs], 0.0)
            ext = jnp.concatenate([ph, p], axis=0)
            pm1 = _shift_down(ext, 1)
            pm2 = _shift_down(ext, 2)
            conv = w2 * p + w1 * pm1 + w0 * pm2
            sg = _sigmoid(z)
            dy0 = dyv * (z * sg)
            dz = dyv * (b * conv) * (sg * (1.0 + z * (1.0 - sg)))
            db = dy0 * conv
            dconv = dy0 * b
            zn = zn_ref[:, cs]
            dconv_n = jnp.where(i < nt - 1, dyn_ref[:, cs] * (zn * _sigmoid(zn)) * bn_ref[:, cs], 0.0)
            dext = jnp.concatenate([dconv, dconv_n], axis=0)
            dp = w2 * dconv + w1 * _shift_up(dext, 1, tm) + w0 * _shift_up(dext, 2, tm)
            dh_ref[:, 0 * E + j * CONV_CHUNK:0 * E + (j + 1) * CONV_CHUNK] = db.astype(BF16)
            dh_ref[:, 1 * E + j * CONV_CHUNK:1 * E + (j + 1) * CONV_CHUNK] = (dp * hh).astype(BF16)
            dh_ref[:, 2 * E + j * CONV_CHUNK:2 * E + (j + 1) * CONV_CHUNK] = (dp * c).astype(BF16)
            dh_ref[:, 3 * E + j * CONV_CHUNK:3 * E + (j + 1) * CONV_CHUNK] = dz.astype(BF16)
            dw = jnp.concatenate([jnp.sum(dconv * pm2, axis=0, keepdims=True),
                                  jnp.sum(dconv * pm1, axis=0, keepdims=True),
                                  jnp.sum(dconv * p, axis=0, keepdims=True),
                                  jnp.zeros((5, CONV_CHUNK), F32)], axis=0)

            @pl.when(i == 0)
            def _():
                dw_ref[:, cs] = dw

            @pl.when(i > 0)
            def _():
                dw_ref[:, cs] += dw

    col = lambda c: pl.BlockSpec((tm, E), lambda i: (i, c))
    return pl.pallas_call(
        body, name=name,
        out_shape=(jax.ShapeDtypeStruct((S, 4 * E), BF16), jax.ShapeDtypeStruct((8, E), F32)),
        grid=(nt,),
        in_specs=[col(0), col(1), col(2), col(3), pl.BlockSpec((tm, E), lambda i: (i, 0)),
                  _prev_halo_spec(tm, E, 1), _prev_halo_spec(tm, E, 2),
                  _next_halo_spec(tm, E, 0, S), _next_halo_spec(tm, E, 3, S), _next_halo_spec(tm, E, 0, S),
                  pl.BlockSpec((8, E), lambda i: (0, 0))],
        out_specs=(pl.BlockSpec((tm, 4 * E), lambda i: (i, 0)), pl.BlockSpec((8, E), lambda i: (0, 0))),
        compiler_params=_cparams("arbitrary"),
    )(h, h, h, h, dy, h, h, h, h, dy, cw)


Z_COLS = D_INNER // LANES
KV_LAT_BLK = D_INNER // KV_LORA
Q_LAT_BLK = (D_INNER + KV_LORA) // Q_LORA
K_ROPE_BLK = (D_INNER + KV_LORA + Q_LORA) // LANES


def _rope(blk, c, s1, s2):
    return blk * c + pltpu.roll(blk, LANES - QK_ROPE // 2, 1) * s1 + pltpu.roll(blk, QK_ROPE // 2, 1) * s2


def _unrope(blk, c, s1, s2):
    return blk * c - pltpu.roll(blk, LANES - QK_ROPE // 2, 1) * s1 - pltpu.roll(blk, QK_ROPE // 2, 1) * s2


def _lat_norm(v, g):
    rstd = lax.rsqrt(jnp.mean(v * v, axis=-1, keepdims=True) + NORM_EPS)
    return v * rstd * g


def _mla_latent_fwd(h, q_norm, kv_norm, tabs, *, tm, name):
    S = h.shape[0]

    def body(kv_ref, q_ref, kr_ref, qg_ref, kvg_ref, c_ref, s1_ref, s2_ref, qn_ref, kvn_ref, krr_ref):
        qn_ref[...] = _lat_norm(q_ref[...], qg_ref[...]).astype(BF16)
        kvn_ref[...] = _lat_norm(kv_ref[...], kvg_ref[...]).astype(BF16)
        krr_ref[...] = _rope(kr_ref[...], c_ref[...], s1_ref[...], s2_ref[...]).astype(BF16)

    tab = pl.BlockSpec((tm, LANES), lambda i: (i, 0))
    return pl.pallas_call(
        body, name=name,
        out_shape=(jax.ShapeDtypeStruct((S, Q_LORA), BF16), jax.ShapeDtypeStruct((S, KV_LORA), BF16),
                   jax.ShapeDtypeStruct((S, LANES), BF16)),
        grid=(S // tm,),
        in_specs=[pl.BlockSpec((tm, KV_LORA), lambda i: (i, KV_LAT_BLK)), pl.BlockSpec((tm, Q_LORA), lambda i: (i, Q_LAT_BLK)),
                  pl.BlockSpec((tm, LANES), lambda i: (i, K_ROPE_BLK)),
                  pl.BlockSpec((1, Q_LORA), lambda i: (0, 0)), pl.BlockSpec((1, KV_LORA), lambda i: (0, 0)), tab, tab, tab],
        out_specs=(pl.BlockSpec((tm, Q_LORA), lambda i: (i, 0)), pl.BlockSpec((tm, KV_LORA), lambda i: (i, 0)), tab),
        compiler_params=_cparams("parallel"),
    )(h, h, h, q_norm, kv_norm, *tabs)


def _mla_q_up(q_n, w_q_pad, tabs, *, tm, name):
    S = q_n.shape[0]

    def body(a_ref, w_ref, c_ref, s1_ref, s2_ref, o_ref):
        a = a_ref[...]
        for hd in range(N_HEADS):
            acc = jnp.dot(a, w_ref[:, hd * HEAD_PAD:(hd + 1) * HEAD_PAD], preferred_element_type=F32)
            o_ref[hd, :, :QK_NOPE] = acc[:, :QK_NOPE].astype(BF16)
            o_ref[hd, :, QK_NOPE:] = _rope(acc[:, QK_NOPE:], c_ref[...], s1_ref[...], s2_ref[...]).astype(BF16)

    tab = pl.BlockSpec((tm, LANES), lambda i: (i, 0))
    return pl.pallas_call(
        body, name=name,
        out_shape=jax.ShapeDtypeStruct((N_HEADS, S, HEAD_PAD), BF16),
        grid=(S // tm,),
        in_specs=[pl.BlockSpec((tm, Q_LORA), lambda i: (i, 0)), pl.BlockSpec((Q_LORA, N_HEADS * HEAD_PAD), lambda i: (0, 0)),
                  tab, tab, tab],
        out_specs=pl.BlockSpec((N_HEADS, tm, HEAD_PAD), lambda i: (0, i, 0)),
        compiler_params=_cparams("parallel"),
    )(q_n, w_q_pad, *tabs)


def _mla_kv_up(kv_n, w_kv, krr, *, tm, name):
    S = kv_n.shape[0]
    heads_per_chip = N_HEADS // N_CHIPS

    def body(a_ref, w_ref, krr_ref, k_ref, v_ref):
        a = a_ref[...]
        ones = jnp.ones((tm, V_DIM), BF16)
        for hd in range(N_HEADS):
            lo = (hd % heads_per_chip) * HEAD_PAD
            acc = jnp.dot(a, w_ref[hd // heads_per_chip, :, lo:lo + HEAD_PAD], preferred_element_type=F32)
            k_ref[hd, :, :QK_NOPE] = acc[:, :QK_NOPE].astype(BF16)
            k_ref[hd, :, QK_NOPE:] = krr_ref[...]
            v_ref[hd, :, :V_DIM] = acc[:, QK_NOPE:].astype(BF16)
            v_ref[hd, :, V_DIM:] = ones

    head_blk = pl.BlockSpec((N_HEADS, tm, HEAD_PAD), lambda i: (0, i, 0))
    sds = jax.ShapeDtypeStruct((N_HEADS, S, HEAD_PAD), BF16)
    return pl.pallas_call(
        body, name=name, out_shape=(sds, sds),
        grid=(S // tm,),
        in_specs=[pl.BlockSpec((tm, KV_LORA), lambda i: (i, 0)),
                  pl.BlockSpec((N_CHIPS, KV_LORA, PACK_COLS), lambda i: (0, w_kv.r0 // KV_LORA, 0)),
                  pl.BlockSpec((tm, LANES), lambda i: (i, 0))],
        out_specs=(head_blk, head_blk),
        compiler_params=_cparams("parallel"),
    )(kv_n, w_kv.buf, krr)


LOG2E = 1.4426950408889634
SCORE_TO_LOG2 = ATTN_SCALE * LOG2E


def _flash_fwd(q_full, k_full, v_aug, h, *, tq, name):
    H, S, _ = q_full.shape
    tk = tq
    HP = 2
    QT = max(t for t in (4, 2, 1) if (S // tq) % t == 0)
    rows_of = lambda t: slice(t * tq, (t + 1) * tq)

    def body(q_ref, k_ref, v_ref, z_ref, o_ref, y_ref, lse_ref, m_sc, acc_sc):
        first = pl.program_id(1) * QT
        m_sc[...] = jnp.full(m_sc.shape, -1e30, F32)
        acc_sc[...] = jnp.zeros(acc_sc.shape, F32)

        def chunk(j, masked):
            off = pl.multiple_of(j * tk, tk)
            for hh in range(HP):
                kj = k_ref[hh, pl.ds(off, tk), :]
                vj = v_ref[hh, pl.ds(off, tk), :]
                for t in range(QT):
                    if masked[t] is None:
                        continue
                    rows = rows_of(t)
                    s = lax.dot_general(q_ref[hh, rows], kj, (((1,), (1,)), ((), ())), preferred_element_type=F32) * SCORE_TO_LOG2
                    if masked[t]:
                        keep = lax.broadcasted_iota(jnp.int32, (tq, tk), 1) <= lax.broadcasted_iota(jnp.int32, (tq, tk), 0)
                        s = jnp.where(keep, s, -1e30)
                    m_old = m_sc[hh, rows]
                    m_new = jnp.maximum(m_old, jnp.max(s, axis=1, keepdims=True))
                    p = jnp.exp2(s - jnp.tile(m_new, (1, tk // LANES)))
                    alpha = jnp.exp2(m_old - m_new)
                    acc_sc[hh, rows] = jnp.tile(alpha, (1, 2)) * acc_sc[hh, rows] + jnp.dot(p.astype(BF16), vj, preferred_element_type=F32)
                    m_sc[hh, rows] = m_new

        def step(j, carry):
            chunk(j, (False,) * QT)
            return carry

        lax.fori_loop(0, first, step, 0)
        for d in range(QT):
            chunk(first + d, tuple(None if t < d else t == d for t in range(QT)))
        for hh in range(HP):
            cs = slice(hh * V_DIM, (hh + 1) * V_DIM)
            for t in range(QT):
                rows = rows_of(t)
                l = acc_sc[hh, rows, V_DIM:]
                o = acc_sc[hh, rows, :V_DIM] / l
                z = z_ref[rows, cs]
                o_ref[rows, cs] = o
                y_ref[rows, cs] = (o * (z * _sigmoid(z))).astype(BF16)
                lse_ref[hh, t] = (m_sc[hh, rows] + jnp.log2(l)).T[0:1, :]

    pair = pl.BlockSpec((QT * tq, HP * V_DIM), lambda hd, i: (i, hd))
    return pl.pallas_call(
        body, name=name,
        out_shape=(jax.ShapeDtypeStruct((S, D_INNER), F32), jax.ShapeDtypeStruct((S, D_INNER), BF16),
                   jax.ShapeDtypeStruct((H, S // tq, 1, tq), F32)),
        grid=(H // HP, S // (QT * tq)),
        in_specs=[pl.BlockSpec((HP, QT * tq, HEAD_PAD), lambda hd, i: (hd, i, 0)),
                  pl.BlockSpec((HP, S, HEAD_PAD), lambda hd, i: (hd, 0, 0)),
                  pl.BlockSpec((HP, S, HEAD_PAD), lambda hd, i: (hd, 0, 0)),
                  pair],
        out_specs=(pair, pair, pl.BlockSpec((HP, QT, 1, tq), lambda hd, i: (hd, i, 0, 0))),
        scratch_shapes=[pltpu.VMEM((HP, QT * tq, LANES), F32), pltpu.VMEM((HP, QT * tq, HEAD_PAD), F32)],
        compiler_params=_cparams("parallel", "parallel"),
    )(q_full, k_full, v_aug, h)


def _mla_gate_bwd(dy, o, h, *, tm, name):
    S = dy.shape[0]
    E = D_INNER

    def body(dy_ref, o_ref, z_ref, do_ref, dz_ref, delta_ref):
        for hd in range(N_HEADS):
            cs = slice(hd * V_DIM, (hd + 1) * V_DIM)
            z = z_ref[:, cs]
            sg = _sigmoid(z)
            dyv = dy_ref[:, cs]
            ov = o_ref[:, cs]
            do = dyv * (z * sg)
            do_ref[:, cs] = do.astype(BF16)
            dz_ref[:, cs] = (dyv * ov * (sg * (1.0 + z * (1.0 - sg)))).astype(BF16)
            delta_ref[hd, 0] = jnp.broadcast_to(jnp.sum(do * ov, axis=-1, keepdims=True), (tm, LANES)).T[0:1, :]

    row = pl.BlockSpec((tm, E), lambda i: (i, 0))
    return pl.pallas_call(
        body, name=name,
        out_shape=(jax.ShapeDtypeStruct((S, E), BF16), jax.ShapeDtypeStruct((S, E), BF16),
                   jax.ShapeDtypeStruct((N_HEADS, S // tm, 1, tm), F32)),
        grid=(S // tm,),
        in_specs=[row, row, row],
        out_specs=(row, row, pl.BlockSpec((N_HEADS, 1, 1, tm), lambda i: (0, i, 0, 0))),
        compiler_params=_cparams("parallel"),
    )(dy, o, h)


def _flash_bwd(q_full, k_full, v_aug, do, lse_rows, delta_rows, tabs, *, tq, name):
    H, S, _ = q_full.shape
    tk = tq
    nq = S // tq
    HP = 2
    KT = max(t for t in (4, 2, 1) if nq % t == 0)
    rows_of = lambda t: slice(t * tk, (t + 1) * tk)

    def body(q_ref, k_ref, v_ref, do_ref, lse_ref, dl_ref, c_ref, s1_ref, s2_ref, dqp_ref, dkv_ref, dkr_ref, dq_ref, dk_sc, dv_sc):
        first = pl.program_id(1) * KT

        @pl.when(first == 0)
        def _():
            dq_ref[...] = jnp.zeros(dq_ref.shape, F32)

        dk_sc[...] = jnp.zeros(dk_sc.shape, F32)
        dv_sc[...] = jnp.zeros(dv_sc.shape, F32)

        def chunk(qi, masked):
            off = pl.multiple_of(qi * tq, tq)
            for hh in range(HP):
                q = q_ref[hh, pl.ds(off, tq), :]
                dov = do_ref[pl.ds(off, tq), hh * V_DIM:(hh + 1) * V_DIM]
                for t in range(KT):
                    if masked[t] is None:
                        continue
                    rows = rows_of(t)
                    k = k_ref[hh, rows]
                    s_t = lax.dot_general(k, q, (((1,), (1,)), ((), ())), preferred_element_type=F32) * SCORE_TO_LOG2
                    p_t = jnp.exp2(s_t - lse_ref[hh, qi])
                    if masked[t]:
                        keep = lax.broadcasted_iota(jnp.int32, (tk, tq), 0) <= lax.broadcasted_iota(jnp.int32, (tk, tq), 1)
                        p_t = jnp.where(keep, p_t, 0.0)
                    dv_sc[hh, rows] += jnp.dot(p_t.astype(BF16), dov, preferred_element_type=F32)
                    dp_t = lax.dot_general(v_ref[hh, rows], dov, (((1,), (1,)), ((), ())), preferred_element_type=F32)
                    ds = (p_t * (dp_t - dl_ref[hh, qi])).astype(BF16)
                    dk_sc[hh, rows] += jnp.dot(ds, q, preferred_element_type=F32)
                    dq_ref[pl.ds(off, tq), hh * HEAD_PAD:(hh + 1) * HEAD_PAD] += lax.dot_general(
                        ds, k, (((0,), (0,)), ((), ())), preferred_element_type=F32)

        def step(qi, carry):
            chunk(qi, (False,) * KT)
            return carry

        for d in range(KT):
            chunk(first + d, tuple(None if t > d else t == d for t in range(KT)))
            done = pl.ds(pl.multiple_of((first + d) * tq, tq), tq)
            here = rows_of(d)
            for hh in range(HP):
                lo = hh * HEAD_PAD
                dqp_ref[here, lo:lo + QK_NOPE] = (dq_ref[done, lo:lo + QK_NOPE] * ATTN_SCALE).astype(BF16)
                dqp_ref[here, lo + QK_NOPE:lo + HEAD_PAD] = _unrope(dq_ref[done, lo + QK_NOPE:lo + HEAD_PAD] * ATTN_SCALE,
                                                                    c_ref[here, :], s1_ref[here, :], s2_ref[here, :]).astype(BF16)
        lax.fori_loop(first + KT, nq, step, 0)
        for hh in range(HP):
            lo = hh * HEAD_PAD
            dkv_ref[:, lo:lo + QK_NOPE] = (dk_sc[hh, :, :QK_NOPE] * ATTN_SCALE).astype(BF16)
            dkv_ref[:, lo + QK_NOPE:lo + HEAD_PAD] = dv_sc[hh].astype(BF16)
            dkr_ref[hh] = dk_sc[hh, :, QK_NOPE:] * ATTN_SCALE

    tab = pl.BlockSpec((KT * tk, LANES), lambda hd, j: (j, 0))
    pair_rows = pl.BlockSpec((KT * tk, HP * HEAD_PAD), lambda hd, j: (j, hd))
    return pl.pallas_call(
        body, name=name,
        out_shape=(jax.ShapeDtypeStruct((S, H * HEAD_PAD), BF16), jax.ShapeDtypeStruct((S, H * HEAD_PAD), BF16),
                   jax.ShapeDtypeStruct((H, S, LANES), F32)),
        grid=(H // HP, S // (KT * tk)),
        in_specs=[pl.BlockSpec((HP, S, HEAD_PAD), lambda hd, j: (hd, 0, 0)),
                  pl.BlockSpec((HP, KT * tk, HEAD_PAD), lambda hd, j: (hd, j, 0)),
                  pl.BlockSpec((HP, KT * tk, V_DIM), lambda hd, j: (hd, j, 0)),
                  pl.BlockSpec((S, HP * V_DIM), lambda hd, j: (0, hd)),
                  pl.BlockSpec((HP, nq, 1, tq), lambda hd, j: (hd, 0, 0, 0)),
                  pl.BlockSpec((HP, nq, 1, tq), lambda hd, j: (hd, 0, 0, 0)), tab, tab, tab],
        out_specs=(pair_rows, pair_rows, pl.BlockSpec((HP, KT * tk, LANES), lambda hd, j: (hd, j, 0))),
        scratch_shapes=[pltpu.VMEM((S, HP * HEAD_PAD), F32), pltpu.VMEM((HP, KT * tk, HEAD_PAD), F32), pltpu.VMEM((HP, KT * tk, V_DIM), F32)],
        compiler_params=_cparams("parallel", "arbitrary"),
    )(q_full, k_full, v_aug, do, lse_rows, delta_rows, *tabs)


def _mla_latent_bwd(h, dq_n, dkv_n, dkr, dz, q_norm, kv_norm, tabs, *, tm, name):
    S = h.shape[0]

    def body(kv_ref, q_ref, dqn_ref, dkvn_ref, dkr_ref, dz_ref, qg_ref, kvg_ref, c_ref, s1_ref, s2_ref, dh_ref, dqg_ref, dkvg_ref):
        i = pl.program_id(0)
        dq_lat, dqg = _rms_bwd_math(q_ref[...], qg_ref[...], dqn_ref[...])
        dkv_lat, dkvg = _rms_bwd_math(kv_ref[...], kvg_ref[...], dkvn_ref[...])
        dkr_sum = dkr_ref[0]
        for hd in range(1, N_HEADS):
            dkr_sum = dkr_sum + dkr_ref[hd]
        dh_ref[:, :D_INNER] = dz_ref[...]
        dh_ref[:, D_INNER:D_INNER + KV_LORA] = dkv_lat.astype(BF16)
        dh_ref[:, D_INNER + KV_LORA:D_INNER + KV_LORA + Q_LORA] = dq_lat.astype(BF16)
        dh_ref[:, D_INNER + KV_LORA + Q_LORA:] = _unrope(dkr_sum, c_ref[...], s1_ref[...], s2_ref[...]).astype(BF16)

        @pl.when(i == 0)
        def _():
            dqg_ref[...] = dqg
            dkvg_ref[...] = dkvg

        @pl.when(i > 0)
        def _():
            dqg_ref[...] += dqg
            dkvg_ref[...] += dkvg

    tab = pl.BlockSpec((tm, LANES), lambda i: (i, 0))
    qvec = pl.BlockSpec((1, Q_LORA), lambda i: (0, 0))
    kvvec = pl.BlockSpec((1, KV_LORA), lambda i: (0, 0))
    return pl.pallas_call(
        body, name=name,
        out_shape=(jax.ShapeDtypeStruct((S, MLA_IN_PAD), BF16), jax.ShapeDtypeStruct((1, Q_LORA), F32),
                   jax.ShapeDtypeStruct((1, KV_LORA), F32)),
        grid=(S // tm,),
        in_specs=[pl.BlockSpec((tm, KV_LORA), lambda i: (i, KV_LAT_BLK)), pl.BlockSpec((tm, Q_LORA), lambda i: (i, Q_LAT_BLK)),
                  pl.BlockSpec((tm, Q_LORA), lambda i: (i, 0)), pl.BlockSpec((tm, KV_LORA), lambda i: (i, 0)),
                  pl.BlockSpec((N_HEADS, tm, LANES), lambda i: (0, i, 0)), pl.BlockSpec((tm, D_INNER), lambda i: (i, 0)),
                  qvec, kvvec, tab, tab, tab],
        out_specs=(pl.BlockSpec((tm, MLA_IN_PAD), lambda i: (i, 0)), qvec, kvvec),
        compiler_params=_cparams("arbitrary"),
    )(h, h, dq_n, dkv_n, dkr, dz, q_norm, kv_norm, *tabs)


def _adamw(w, g, m, v, *, name):
    R, C = w.shape
    tr = R
    for cand in (512, 256, 128, 64, 32, 16, 8):
        if R % cand == 0 and cand * C * 4 <= 2 * 1024 * 1024:
            tr = cand
            break

    def body(w_ref, g_ref, m_ref, v_ref, d_ref, nm_ref, nv_ref):
        d_ref[...], nm_ref[...], nv_ref[...] = _adam_math(w_ref[...], g_ref[...], m_ref[...], v_ref[...])

    spec = pl.BlockSpec((tr, C), lambda i: (i, 0))
    sds = jax.ShapeDtypeStruct((R, C), F32)
    return pl.pallas_call(
        body, name=name, out_shape=(sds, sds, sds), grid=(R // tr,),
        in_specs=[spec] * 4, out_specs=(spec,) * 3,
        compiler_params=_cparams("parallel"),
    )(w, g, m, v)


def _adamw_rows(w, m, v, srcs, *, name):
    nj, R, C = w.shape
    assert len(srcs) == nj and C % PACK_COLS == 0
    tr = min(R, 256)
    assert R % tr == 0 and all(r0 % tr == 0 for _, r0 in srcs)

    def body(*refs):
        w_ref, m_ref, v_ref = refs[:3]
        g_refs = refs[3:3 + nj]
        go_ref, d_ref, nm_ref, nv_ref = refs[3 + nj:]
        gv = g_refs[0][...]
        for jj in range(1, nj):
            gv = jnp.where(pl.program_id(0) == jj, g_refs[jj][...], gv)
        d, m_new, v_new = _adam_math(w_ref[...], gv, m_ref[...], v_ref[...])
        go_ref[...] = gv
        d_ref[...] = d
        nm_ref[...] = m_new
        nv_ref[...] = v_new

    nat = pl.BlockSpec((None, tr, PACK_COLS), lambda j, cb, i: (j, i, cb))

    def src_spec(jj, r0):
        return pl.BlockSpec((tr, PACK_COLS), lambda j, cb, i: (jnp.where(j == jj, (r0 + cb * R) // tr + i, r0 // tr), 0))

    sds = jax.ShapeDtypeStruct((nj, R, C), F32)
    return pl.pallas_call(
        body, name=name, out_shape=(sds,) * 4, grid=(nj, C // PACK_COLS, R // tr),
        in_specs=[nat] * 3 + [src_spec(jj, r0) for jj, (_, r0) in enumerate(srcs)], out_specs=(nat,) * 4,
        compiler_params=_cparams("parallel", "parallel", "parallel"),
    )(w, m, v, *[rows for rows, _ in srcs])


HBM_SPEC = pl.BlockSpec(memory_space=pltpu.HBM)
VMEM_SPEC = pl.BlockSpec(memory_space=pltpu.VMEM)
SEM_SPEC = pl.BlockSpec(memory_space=pltpu.SEMAPHORE)
ANY_SPEC = pl.BlockSpec(memory_space=pl.ANY)
SPLIT_EFFECT = pltpu.SideEffectType.DATAFLOW_SIDE_EFFECTING


def _place():
    return lax.axis_index("x"), lax.axis_index("y"), lax.axis_index("c")


def _other_chips(x, y):
    return [(1 - x, y), (x, 1 - y), (1 - x, 1 - y)]


def _remote(src, dst, send_sem, recv_sem, dev):
    return pltpu.make_async_remote_copy(src_ref=src, dst_ref=dst, send_sem=send_sem, recv_sem=recv_sem,
                                        device_id=dev, device_id_type=MESH)


def _ag_ici_start(wp, after, *, name):
    R, C = wp.shape
    H = R // 2

    def body(w_ref, land_ref, after_ref, send_sems, recv_sems, w_thru, land_thru, token):
        x, y, c = _place()
        rows = pl.ds(pl.multiple_of(c * H, 16), H)
        for r, chip in enumerate(_other_chips(x, y)):
            _remote(w_ref.at[rows, :], land_ref.at[2 * x + y, rows, :], send_sems.at[r], recv_sems.at[r], (*chip, c)).start()
        token[...] = jnp.zeros(token.shape, F32)

    land = lax.empty((N_CHIPS, R, C), wp.dtype)
    return pl.pallas_call(
        body, name=name,
        out_shape=(pltpu.SemaphoreType.DMA((3,)), pltpu.SemaphoreType.DMA((3,)), pltpu.HBM(wp.shape, wp.dtype), pltpu.HBM(land.shape, land.dtype),
                   jax.ShapeDtypeStruct((8, LANES), F32)),
        in_specs=(HBM_SPEC, HBM_SPEC, ANY_SPEC), out_specs=(SEM_SPEC, SEM_SPEC, HBM_SPEC, HBM_SPEC, VMEM_SPEC),
        input_output_aliases={0: 2, 1: 3},
        compiler_params=pltpu.CompilerParams(has_side_effects=SPLIT_EFFECT),
    )(pltpu.with_memory_space_constraint(wp, pltpu.HBM), pltpu.with_memory_space_constraint(land, pltpu.HBM), after)


def _ag_ici_wait(send_sems, recv_sems, w_thru, land_thru, after, *, name):
    R, C = w_thru.shape
    H = R // 2
    after = after if isinstance(after, (tuple, list)) else (after,)

    def body(w_ref, land_ref, send_sems, recv_sems, *rest):
        x, y, c = _place()
        rows = pl.ds(pl.multiple_of(c * H, 16), H)
        for r, (px, py) in enumerate(_other_chips(x, y)):
            cp = _remote(w_ref.at[rows, :], land_ref.at[2 * px + py, rows, :], send_sems.at[r], recv_sems.at[r], (px, py, c))
            cp.wait_send()
            cp.wait_recv()

    return pl.pallas_call(
        body, name=name,
        out_shape=(pltpu.HBM(w_thru.shape, w_thru.dtype), pltpu.HBM(land_thru.shape, land_thru.dtype)),
        in_specs=(HBM_SPEC, HBM_SPEC, SEM_SPEC, SEM_SPEC) + (ANY_SPEC,) * len(after), out_specs=(HBM_SPEC, HBM_SPEC),
        input_output_aliases={0: 0, 1: 1},
        compiler_params=pltpu.CompilerParams(has_side_effects=SPLIT_EFFECT),
    )(w_thru, land_thru, send_sems, recv_sems, *after)


def _ag_sibling_forward(land, wp, *, name):
    _, R, C = land.shape
    H = R // 2

    def body(land_ref, w_ref, out_ref, send_sems, recv_sems):
        x, y, c = _place()
        sib = (x, y, 1 - c)
        mine = pl.ds(pl.multiple_of(c * H, 16), H)
        theirs = pl.ds(pl.multiple_of((1 - c) * H, 16), H)
        chips = _other_chips(x, y)
        sends = [_remote(land_ref.at[2 * px + py, mine, :], out_ref.at[2 * px + py, mine, :], send_sems.at[r], recv_sems.at[r], sib)
                 for r, (px, py) in enumerate(chips)]
        sends.append(_remote(w_ref, out_ref.at[2 * x + y], send_sems.at[3], recv_sems.at[3], sib))
        for cp in sends:
            cp.start()
        for r, (px, py) in enumerate(chips):
            _remote(land_ref.at[2 * px + py, mine, :], out_ref.at[2 * px + py, theirs, :], send_sems.at[r], recv_sems.at[r], sib).wait_recv()
        sends[3].wait_recv()
        for cp in sends:
            cp.wait_send()

    return pl.pallas_call(
        body, name=name, out_shape=jax.ShapeDtypeStruct(land.shape, land.dtype),
        in_specs=[HBM_SPEC, HBM_SPEC], out_specs=HBM_SPEC, input_output_aliases={0: 0},
        scratch_shapes=[pltpu.SemaphoreType.DMA((4,)), pltpu.SemaphoreType.DMA((4,))],
    )(land, wp)


def _rs_sibling_swap(g, *, name):
    _, R, C = g.shape
    H = R // 2

    def body(g_ref, theirs_ref, send_sems, recv_sems):
        x, y, c = _place()
        sib = (x, y, 1 - c)
        copies = [_remote(g_ref.at[k, pl.ds(pl.multiple_of((1 - c) * H, 16), H), :], theirs_ref.at[k],
                          send_sems.at[k], recv_sems.at[k], sib) for k in range(N_CHIPS)]
        for cp in copies:
            cp.start()
        for cp in copies:
            cp.wait()

    return pl.pallas_call(
        body, name=name, out_shape=jax.ShapeDtypeStruct((N_CHIPS, H, C), g.dtype),
        in_specs=[HBM_SPEC], out_specs=HBM_SPEC,
        scratch_shapes=[pltpu.SemaphoreType.DMA((N_CHIPS,)), pltpu.SemaphoreType.DMA((N_CHIPS,))],
    )(g)


def _row_tile(h):
    best = 16
    for d in range(16, 1025, 16):
        if h % d == 0:
            best = d
    return best


def _add2_bf16(g, theirs, core, *, name):
    K, H, C = theirs.shape
    tr = _row_tile(H)
    nb = H // tr

    def body(c_ref, a_ref, b_ref, o_ref):
        o_ref[...] = (a_ref[...].astype(F32) + b_ref[...].astype(F32)).astype(o_ref.dtype)

    spec = pl.BlockSpec((None, tr, C), lambda k, i, c: (k, i, 0))
    return pl.pallas_call(
        body, name=name, out_shape=jax.ShapeDtypeStruct((K, H, C), theirs.dtype),
        grid_spec=pltpu.PrefetchScalarGridSpec(
            num_scalar_prefetch=1, grid=(K, nb),
            in_specs=[pl.BlockSpec((None, tr, C), lambda k, i, c: (k, c[0] * nb + i, 0)), spec], out_specs=spec),
        compiler_params=_cparams("parallel", "parallel"),
    )(core, g, theirs)


def _rs_chip_exchange_start(p, *, name):
    _, H, C = p.shape

    def body(p_ref, land_ref, send_sems, recv_sems, p_thru, land_thru, token):
        x, y, c = _place()
        for r, (px, py) in enumerate(_other_chips(x, y)):
            _remote(p_ref.at[2 * px + py], land_ref.at[r], send_sems.at[r], recv_sems.at[r], (px, py, c)).start()
        token[...] = jnp.zeros(token.shape, F32)

    land = lax.empty((3, H, C), p.dtype)
    return pl.pallas_call(
        body, name=name,
        out_shape=(pltpu.SemaphoreType.DMA((3,)), pltpu.SemaphoreType.DMA((3,)), pltpu.HBM(p.shape, p.dtype), pltpu.HBM(land.shape, land.dtype),
                   jax.ShapeDtypeStruct((8, LANES), F32)),
        in_specs=(HBM_SPEC, HBM_SPEC), out_specs=(SEM_SPEC, SEM_SPEC, HBM_SPEC, HBM_SPEC, VMEM_SPEC),
        input_output_aliases={0: 2, 1: 3},
        compiler_params=pltpu.CompilerParams(has_side_effects=SPLIT_EFFECT),
    )(pltpu.with_memory_space_constraint(p, pltpu.HBM), pltpu.with_memory_space_constraint(land, pltpu.HBM))


def _rs_chip_exchange_wait(send_sems, recv_sems, p_thru, land_thru, after, *, name):
    after = after if isinstance(after, (tuple, list)) else (after,)

    def body(p_ref, land_ref, send_sems, recv_sems, *rest):
        x, y, c = _place()
        for r, (px, py) in enumerate(_other_chips(x, y)):
            cp = _remote(p_ref.at[2 * px + py], land_ref.at[r], send_sems.at[r], recv_sems.at[r], (px, py, c))
            cp.wait_send()
            cp.wait_recv()

    return pl.pallas_call(
        body, name=name,
        out_shape=(pltpu.HBM(p_thru.shape, p_thru.dtype), pltpu.HBM(land_thru.shape, land_thru.dtype)),
        in_specs=(HBM_SPEC, HBM_SPEC, SEM_SPEC, SEM_SPEC) + (ANY_SPEC,) * len(after), out_specs=(HBM_SPEC, HBM_SPEC),
        input_output_aliases={0: 0, 1: 1},
        compiler_params=pltpu.CompilerParams(has_side_effects=SPLIT_EFFECT),
    )(p_thru, land_thru, send_sems, recv_sems, *after)


def _add4_f32(p, recv, chip_core, *, name):
    _, H, C = p.shape
    tr = _row_tile(H)
    nb = H // tr

    def body(s_ref, o_ref, r_ref, out_ref):
        out_ref[...] = ((o_ref[...].astype(F32) + r_ref[0].astype(F32)) + r_ref[1].astype(F32)) + r_ref[2].astype(F32)

    return pl.pallas_call(
        body, name=name, out_shape=jax.ShapeDtypeStruct((2 * H, C), F32),
        grid_spec=pltpu.PrefetchScalarGridSpec(
            num_scalar_prefetch=1, grid=(nb,),
            in_specs=[pl.BlockSpec((None, tr, C), lambda i, s: (s[0], i, 0)), pl.BlockSpec((3, tr, C), lambda i, s: (0, i, 0))],
            out_specs=pl.BlockSpec((tr, C), lambda i, s: (s[1] * nb + i, 0))),
        compiler_params=_cparams("parallel"),
    )(chip_core, p, recv)


def _rs_sibling_join(f, *, name):
    R, C = f.shape
    H = R // 2

    def body(f_ref, out_ref, send_sem, recv_sem):
        x, y, c = _place()
        sib = (x, y, 1 - c)
        mine = pl.ds(pl.multiple_of(c * H, 8), H)
        theirs = pl.ds(pl.multiple_of((1 - c) * H, 8), H)
        cp = _remote(f_ref.at[mine, :], out_ref.at[mine, :], send_sem, recv_sem, sib)
        cp.start()
        _remote(f_ref.at[mine, :], out_ref.at[theirs, :], send_sem, recv_sem, sib).wait_recv()
        cp.wait_send()

    return pl.pallas_call(
        body, name=name, out_shape=jax.ShapeDtypeStruct((R, C), f.dtype),
        in_specs=[HBM_SPEC], out_specs=HBM_SPEC, input_output_aliases={0: 0},
        scratch_shapes=[pltpu.SemaphoreType.DMA, pltpu.SemaphoreType.DMA],
    )(f)


SMALL_GATHER = (("pool_norm", 2, 256), ("pool_scale", 2, 512), ("conv_w", 3, 512), ("mla_norm", 1, 256),
                ("mla_q_norm", 1, 96), ("mla_kv_norm", 1, 64))
SMALL_SLOT = (16, 512)


def _gather_small(shards, *, name):
    def body(pn_ref, ps_ref, cw_ref, mn_ref, qn_ref, kn_ref, pn_o, ps_o, cw_o, mn_o, qn_o, kn_o, all_ref, send_sems, recv_sems):
        x, y, c = _place()
        mine = 2 * x + y
        all_ref[mine] = jnp.zeros(SMALL_SLOT, F32)
        all_ref[mine, 0:2, 0:256] = pn_ref[...]
        all_ref[mine, 2:4, :] = ps_ref[...]
        all_ref[mine, 4:7, :] = cw_ref[0]
        all_ref[mine, 7:8, 0:256] = mn_ref[...]
        all_ref[mine, 8:9, 0:96] = qn_ref[...]
        all_ref[mine, 9:10, 0:64] = kn_ref[...]
        chips = _other_chips(x, y)
        sends = [_remote(all_ref.at[mine], all_ref.at[mine], send_sems.at[r], recv_sems.at[r], (*chip, c)) for r, chip in enumerate(chips)]
        for cp in sends:
            cp.start()
        for r, (px, py) in enumerate(chips):
            _remote(all_ref.at[mine], all_ref.at[2 * px + py], send_sems.at[r], recv_sems.at[r], (px, py, c)).wait_recv()
        for cp in sends:
            cp.wait_send()
        for k in range(N_CHIPS):
            pn_o[:, k * 256:(k + 1) * 256] = all_ref[k, 0:2, 0:256]
            ps_o[:, k * 512:(k + 1) * 512] = all_ref[k, 2:4, :]
            cw_o[0, :, k * 512:(k + 1) * 512] = all_ref[k, 4:7, :]
            mn_o[:, k * 256:(k + 1) * 256] = all_ref[k, 7:8, 0:256]
            qn_o[k] = all_ref[k, 8:9, 0:96]
            kn_o[k] = all_ref[k, 9:10, 0:64]

    sds = lambda *shape: jax.ShapeDtypeStruct(shape, F32)
    out = pl.pallas_call(
        body, name=name,
        out_shape=(sds(2, 1024), sds(2, 2048), sds(1, 3, 2048), sds(1, 1024), sds(N_CHIPS, 1, 96), sds(N_CHIPS, 1, 64)),
        in_specs=[VMEM_SPEC] * 6, out_specs=(VMEM_SPEC,) * 6,
        scratch_shapes=[pltpu.VMEM((N_CHIPS,) + SMALL_SLOT, F32), pltpu.SemaphoreType.DMA((3,)), pltpu.SemaphoreType.DMA((3,))],
    )(*[shards[n] for n, _, _ in SMALL_GATHER])
    full = dict(zip([n for n, _, _ in SMALL_GATHER], out))
    full["mla_q_norm"] = full["mla_q_norm"].reshape(1, Q_LORA)
    full["mla_kv_norm"] = full["mla_kv_norm"].reshape(1, KV_LORA)
    return full


SMALL_REDUCE = (("pool_norm_0", 0, 1, 1024), ("pool_norm_1", 1, 1, 1024), ("pool_scale_0", 2, 1, 2048), ("pool_scale_1", 3, 1, 2048),
                ("conv_norm", 4, 1, 1024), ("mla_norm", 5, 1, 1024), ("mla_q_norm", 6, 1, 384), ("mla_kv_norm", 7, 1, 256),
                ("conv_w", 8, 8, 2048), ("final_norm", 16, 1, 1024))
REDUCE_SLOT = (24, 2048)


def _reduce_small(parts, after, *, name):
    keys = [k for k, _, _, _ in SMALL_REDUCE]

    def body(*refs):
        ins = dict(zip(keys, refs[:len(keys)]))
        pn_o, ps_o, cn_o, cw_o, mn_o, qn_o, kn_o, fn_o, all_ref, send_sems, recv_sems = refs[len(keys) + 1:]
        x, y, c = _place()
        me = 4 * x + 2 * y + c
        all_ref[me] = jnp.zeros(REDUCE_SLOT, F32)
        for k, r0, nr, wd in SMALL_REDUCE:
            all_ref[me, r0:r0 + nr, 0:wd] = ins[k][...]
        peers = []
        for rel in range(1, N_DEV):
            dx, dy, dc = (rel >> 2) & 1, (rel >> 1) & 1, rel & 1
            peers.append((1 - x if dx else x, 1 - y if dy else y, 1 - c if dc else c))
        sends = [_remote(all_ref.at[me], all_ref.at[me], send_sems.at[k], recv_sems.at[k], peer) for k, peer in enumerate(peers)]
        for cp in sends:
            cp.start()
        for k, (px, py, pc) in enumerate(peers):
            _remote(all_ref.at[me], all_ref.at[4 * px + 2 * py + pc], send_sems.at[k], recv_sems.at[k], (px, py, pc)).wait_recv()
        for cp in sends:
            cp.wait_send()

        def total(r0, nr, wd):
            acc = all_ref[0, r0:r0 + nr, 0:wd]
            for d in range(1, N_DEV):
                acc = acc + all_ref[d, r0:r0 + nr, 0:wd]
            return acc

        pn_o[0:1, :] = total(0, 1, 1024)
        pn_o[1:2, :] = total(1, 1, 1024)
        ps_o[0:1, :] = total(2, 1, 2048)
        ps_o[1:2, :] = total(3, 1, 2048)
        cn_o[...] = total(4, 1, 1024)
        mn_o[...] = total(5, 1, 1024)
        qn_o[...] = total(6, 1, Q_LORA)
        kn_o[...] = total(7, 1, KV_LORA)
        cw_o[0] = total(8, 3, 2048)
        fn_o[...] = total(16, 1, 1024)

    sds = lambda *shape: jax.ShapeDtypeStruct(shape, F32)
    out = pl.pallas_call(
        body, name=name,
        out_shape=(sds(2, 1024), sds(2, 2048), sds(1, 1024), sds(1, 3, 2048), sds(1, 1024), sds(1, Q_LORA), sds(1, KV_LORA), sds(1, 1024)),
        in_specs=[VMEM_SPEC] * len(keys) + [ANY_SPEC], out_specs=(VMEM_SPEC,) * 8,
        scratch_shapes=[pltpu.VMEM((N_DEV,) + REDUCE_SLOT, F32), pltpu.SemaphoreType.DMA((N_DEV - 1,)), pltpu.SemaphoreType.DMA((N_DEV - 1,))],
    )(*[parts[k] for k in keys], after)
    return dict(zip(("pool_norm", "pool_scale", "conv_norm", "conv_w", "mla_norm", "mla_q_norm", "mla_kv_norm", "final_norm"), out))


def _adam_math(w, g, m, v):
    m_new = ADAM_B1 * m + (1.0 - ADAM_B1) * g
    v_new = ADAM_B2 * v + (1.0 - ADAM_B2) * (g * g)
    m_hat = m_new / (1.0 - ADAM_B1 ** ADAM_STEP)
    v_hat = v_new / (1.0 - ADAM_B2 ** ADAM_STEP)
    return -ADAM_LR * (m_hat / (jnp.sqrt(v_hat) + ADAM_EPS) + ADAM_WD * w), m_new, v_new


def _adamw_small(w, m, v, g_full, chip, *, name):
    shp = {n: w[n].shape for n in SMALL}
    whole = lambda s: pl.BlockSpec(s, lambda i, c: (0,) * len(s))
    g_in, g_specs = {}, {}
    for n in SMALL:
        if not SMALL_SHARDED[n]:
            g_in[n], g_specs[n] = g_full[n].reshape(shp[n]), whole(shp[n])
        elif shp[n][-1] % LANES:
            g_in[n] = g_full[n].reshape(N_CHIPS, 1, shp[n][-1])
            g_specs[n] = pl.BlockSpec((None,) + shp[n], lambda i, c: (c[0], 0, 0))
        else:
            g_in[n] = g_full[n]
            nd = len(shp[n])
            g_specs[n] = pl.BlockSpec(shp[n], lambda i, c, nd=nd: (0,) * (nd - 1) + (c[0],))

    def body(c_ref, *refs):
        k = len(SMALL)
        w_r, m_r, v_r, g_r = refs[0:k], refs[k:2 * k], refs[2 * k:3 * k], refs[3 * k:4 * k]
        go_r, d_r, nm_r, nv_r = refs[4 * k:5 * k], refs[5 * k:6 * k], refs[6 * k:7 * k], refs[7 * k:8 * k]
        for i in range(k):
            gv = g_r[i][...]
            d, m_new, v_new = _adam_math(w_r[i][...], gv, m_r[i][...], v_r[i][...])
            go_r[i][...] = gv
            d_r[i][...] = d
            nm_r[i][...] = m_new
            nv_r[i][...] = v_new

    nat = [whole(shp[n]) for n in SMALL]
    out_sds = tuple(jax.ShapeDtypeStruct(shp[n], F32) for n in SMALL)
    out = pl.pallas_call(
        body, name=name, out_shape=out_sds * 4,
        grid_spec=pltpu.PrefetchScalarGridSpec(
            num_scalar_prefetch=1, grid=(1,),
            in_specs=nat * 3 + [g_specs[n] for n in SMALL], out_specs=tuple(nat) * 4),
        compiler_params=_cparams("arbitrary"),
    )(chip, *[w[n] for n in SMALL], *[m[n] for n in SMALL], *[v[n] for n in SMALL], *[g_in[n] for n in SMALL])
    k = len(SMALL)
    return tuple(dict(zip(SMALL, out[j * k:(j + 1) * k])) for j in range(4))


BIG = ("pool_w_in", "pool_w_grp", "pool_w_out", "conv_w_in", "conv_w_out", "mla_w_in", "mla_w_q_up", "mla_w_kv_up", "mla_w_out")
BIG_SHARD_AXIS = {"pool_w_in": 2, "pool_w_grp": 2, "pool_w_out": 1, "conv_w_in": 2, "conv_w_out": 1,
                  "mla_w_in": 2, "mla_w_q_up": 2, "mla_w_kv_up": 2, "mla_w_out": 1}
GATHER_LAYOUT = {
    "p0": ((("pool_w_in", 0), 0, "cols"), (("pool_w_out", 0), 1024, "rows"), (("pool_w_grp", 0), 1536, "flat")),
    "cv": ((("conv_w_in", 0), 0, "cols"), (("conv_w_out", 0), 2048, "rows")),
    "ml": ((("pool_w_in", 1), 0, "cols"), (("mla_w_out", 0), 1024, "rows"), (("pool_w_out", 1), 1536, "rows"),
           (("mla_w_kv_up", 0), 2048, "cols"), (("pool_w_grp", 1), 2304, "flat"), (("mla_w_q_up", 0), 2560, "flat"),
           (("mla_w_in", 0), 2848, "flat")),
}
REDUCE_LAYOUT = {
    "late": ((("conv_w_in", 0), 0, "cols"), (("pool_w_in", 1), 2048, "cols"), (("conv_w_out", 0), 3072, "rows"),
             (("mla_w_out", 0), 3584, "rows"), (("pool_w_out", 1), 4096, "rows"), (("mla_w_kv_up", 0), 4608, "cols"),
             (("pool_w_grp", 1), 4864, "flat"), (("mla_w_q_up", 0), 5120, "flat"), (("mla_w_in", 0), 5408, "flat")),
    "first": ((("pool_w_in", 0), 0, "cols"), (("pool_w_out", 0), 1024, "rows"), (("pool_w_grp", 0), 1536, "flat")),
}
PACK_ROW_ALIGN = 32
RS_ROW_ALIGN = 512


def _slot_rows(layout, shard_shape, align):
    where, end = {}, 0
    for piece, r0, kind in layout:
        n = 1
        for d in shard_shape(piece):
            n *= d
        assert r0 >= end and n % PACK_COLS == 0, (piece, r0, end)
        where[piece] = (r0, n // PACK_COLS, kind)
        end = r0 + n // PACK_COLS
    return end + (-end) % align, where


def _as_slot_rows(shard, kind):
    if kind == "cols":
        k, n = shard.shape
        return shard.reshape(k, n // PACK_COLS, PACK_COLS).swapaxes(0, 1).reshape(-1, PACK_COLS)
    return shard.reshape(-1, PACK_COLS)


def _pack_slot(shards, layout, rows, dtype):
    parts, end = [], 0
    for piece, r0, kind in layout:
        if r0 > end:
            parts.append(jnp.zeros((r0 - end, PACK_COLS), dtype))
        parts.append(_as_slot_rows(shards[piece], kind).astype(dtype))
        end = r0 + parts[-1].shape[0]
    if rows > end:
        parts.append(jnp.zeros((rows - end, PACK_COLS), dtype))
    return jnp.concatenate(parts, axis=0)


SMALL = ("pool_norm", "pool_scale", "conv_norm", "conv_w", "mla_norm", "mla_q_norm", "mla_kv_norm", "final_norm")
SMALL_SHARDED = {"pool_norm": True, "pool_scale": True, "conv_norm": False, "conv_w": True, "mla_norm": True,
                 "mla_q_norm": True, "mla_kv_norm": True, "final_norm": False}


def _rope_tables(positions):
    inv_freq = ROPE_BASE ** (-jnp.arange(0, QK_ROPE, 2, dtype=F32) / QK_ROPE)
    ang = positions.astype(F32).reshape(-1, 1) * inv_freq
    cos, sin = jnp.cos(ang), jnp.sin(ang)
    z32 = jnp.zeros_like(cos)
    z64 = jnp.concatenate([z32, z32], axis=1)
    return (jnp.concatenate([cos, cos, z64], axis=1), jnp.concatenate([-sin, z32, z64], axis=1),
            jnp.concatenate([z32, sin, z64], axis=1))


def _mla_in_to_padded(w):
    q, kv, kr, z = w[:, :Q_LORA], w[:, Q_LORA:Q_LORA + KV_LORA], w[:, Q_LORA + KV_LORA:Q_LORA + KV_LORA + QK_ROPE], w[:, Q_LORA + KV_LORA + QK_ROPE:]
    return jnp.concatenate([z, kv, q, kr, jnp.zeros((w.shape[0], MLA_IN_PAD - MLA_IN), w.dtype)], axis=1)


def _mla_in_from_padded(w):
    z, kv, q, kr = w[:, :D_INNER], w[:, D_INNER:D_INNER + KV_LORA], w[:, D_INNER + KV_LORA:D_INNER + KV_LORA + Q_LORA], w[:, D_INNER + KV_LORA + Q_LORA:D_INNER + KV_LORA + Q_LORA + QK_ROPE]
    return jnp.concatenate([q, kv, kr, z], axis=1)


def _q_up_to_padded(w):
    k = w.shape[0]
    return jnp.pad(w.reshape(k, N_HEADS, QK_NOPE + QK_ROPE), ((0, 0), (0, 0), (0, HEAD_PAD - QK_NOPE - QK_ROPE))).reshape(k, N_HEADS * HEAD_PAD)


def _q_up_from_padded(w):
    k = w.shape[0]
    return w.reshape(k, N_HEADS, HEAD_PAD)[:, :, :QK_NOPE + QK_ROPE].reshape(k, N_HEADS * (QK_NOPE + QK_ROPE))


def _local_step(x, positions, target, weights_for, ws, sink):
    S = x.shape[0]
    tm = min(512, S)
    te = min(256, S)
    tq = min(512, S)
    tabs = _rope_tables(positions)
    gs = {}

    def mm_in(xn, w, name):
        n = w.shape[1]
        tn = PACK_COLS if isinstance(w, Packed) else _pick(n, 1536 if n == MLA_IN_PAD else 1024)
        return _mm(xn, w, tm=min(1024, S), tn=tn, tk=D_MODEL, name=name)

    def mm_out(y, w, res, name):
        return _mm(y, w, residual=res, tm=tm, tn=D_MODEL, tk=D_INNER, name=name)

    def mm_dx(dy, w, name, after=None):
        k, n = w.shape
        if isinstance(w, Packed):
            tn, tk = (k if w.kind == "rows" else min(k, 1024)), PACK_COLS
        else:
            tn, tk = _pick(k, 1024), _pick(n, 1408)
        return _mm(dy, w, trans_b=True, after=after, tm=min(1024, S), tn=tn, tk=tk, name=name)

    def mm_dw(piece, a, b, name, after=None, post=None):
        ka, nb = a.shape[1], b.shape[1]
        into = sink.dest(piece)
        tokens = min(1024, S)
        if into is None:
            out = _mm(a, b, trans_a=True, out_dtype=BF16, after=after, tm=_pick(ka, 1024), tn=_pick(nb, 1408), tk=tokens, name=name)
            sink.put(piece, out if post is None else post(out))
        else:
            rows = ka if into.kind == "rows" else min(ka, 1024)
            sink.put(piece, _mm(a, b, trans_a=True, after=after, into=into, tm=rows, tn=PACK_COLS, tk=tokens, name=name))

    def pool_layer_fwd(xin, xn, wts, j, tag):
        h = mm_in(xn, wts[("pool_w_in", j)], f"{tag}_in")
        y = _pool_fwd(h, wts[("pool_w_grp", j)], ws["pool_scale"][j:j + 1], tm=te, name=f"{tag}_mix")
        xo = mm_out(y, wts[("pool_w_out", j)], xin, f"{tag}_out")
        return xo, (xin, xn, h, y)

    def pool_layer_bwd(dx, dxb, saved, wts, j, tag, after=None):
        xin, xn, h, y = saved
        dy = mm_dx(dxb, wts[("pool_w_out", j)], f"{tag}_dy", after)
        mm_dw(("pool_w_out", j), y, dxb, f"{tag}_dwo", after)
        pooled, dmixed, dpooled, dz, dsc = _pool_bwd1(h, dy, wts[("pool_w_grp", j)], ws["pool_scale"][j:j + 1], tm=te, name=f"{tag}_bmix")
        sink.put(("pool_w_grp", j), _grouped_tn(pooled, dmixed, tk=tm, name=f"{tag}_dwg"))
        dh = _pool_bwd2(dpooled, dz, tm=te, name=f"{tag}_bshift")
        dxn = mm_dx(dh, wts[("pool_w_in", j)], f"{tag}_dxn")
        mm_dw(("pool_w_in", j), xn, dh, f"{tag}_dwi")
        dxo, dxob, dg = _rms_bwd(xin, ws["pool_norm"][j:j + 1], dxn, dx, tm=tm, name=f"{tag}_bnorm")
        gs[f"pool_norm_{j}"], gs[f"pool_scale_{j}"] = dg, dsc
        return dxo, dxob

    xn0 = _rms_fwd(x, ws["pool_norm"][0:1], tm=tm, name="p0_norm")
    w_p0 = weights_for("p0", xn0)
    x1, sv0 = pool_layer_fwd(x, xn0, w_p0, 0, "p0")

    w_cv = weights_for("cv", x1)
    xn1 = _rms_fwd(x1, ws["conv_norm"][0:1], tm=tm, name="cv_norm")
    h1 = mm_in(xn1, w_cv[("conv_w_in", 0)], "cv_in")
    cw = jnp.pad(ws["conv_w"][0], ((0, 5), (0, 0)))
    y1 = _conv_fwd(h1, cw, tm=te, name="cv_mix")
    x2 = mm_out(y1, w_cv[("conv_w_out", 0)], x1, "cv_out")

    w_ml = weights_for("ml", x2)
    w_mi = _mla_in_to_padded(w_ml[("mla_w_in", 0)])
    w_q = _q_up_to_padded(w_ml[("mla_w_q_up", 0)])
    w_kv = w_ml[("mla_w_kv_up", 0)]
    qg, kvg = ws["mla_q_norm"][0:1], ws["mla_kv_norm"][0:1]
    xn2 = _rms_fwd(x2, ws["mla_norm"][0:1], tm=tm, name="ml_norm")
    h2 = mm_in(xn2, w_mi, "ml_in")
    q_n, kv_n, krr = _mla_latent_fwd(h2, qg, kvg, tabs, tm=tm, name="ml_lat")
    q_full = _mla_q_up(q_n, w_q, tabs, tm=tm, name="ml_qup")
    k_full, v = _mla_kv_up(kv_n, w_kv, krr, tm=tm, name="ml_kvup")
    o, y2, lse = _flash_fwd(q_full, k_full, v, h2, tq=tq, name="ml_attn")
    x3 = mm_out(y2, w_ml[("mla_w_out", 0)], x2, "ml_out")

    x4, sv3 = pool_layer_fwd(x3, _rms_fwd(x3, ws["pool_norm"][1:2], tm=tm, name="p1_norm"), w_ml, 1, "p1")

    loss_part, dx, dxb, dgf = _final_loss(x4, ws["final_norm"].reshape(1, -1), target, tm=tm, name="final")
    gs["final_norm"] = dgf

    dx, dxb = pool_layer_bwd(dx, dxb, sv3, w_ml, 1, "p1")

    dy = mm_dx(dxb, w_ml[("mla_w_out", 0)], "ml_dy")
    mm_dw(("mla_w_out", 0), y2, dxb, "ml_dwo")
    do, dz, delta = _mla_gate_bwd(dy, o, h2, tm=tq, name="ml_bgate")
    dq_pre, dkv, dkr = _flash_bwd(q_full, k_full, v, do, lse, delta, tabs, tq=tq, name="ml_battn")
    dq_n = mm_dx(dq_pre, w_q, "ml_dqn")
    mm_dw(("mla_w_q_up", 0), q_n, dq_pre, "ml_dwq", post=_q_up_from_padded)
    dkv_n = mm_dx(dkv, w_kv, "ml_dkvn")
    mm_dw(("mla_w_kv_up", 0), kv_n, dkv, "ml_dwkv")
    dh2, dqg, dkvg = _mla_latent_bwd(h2, dq_n, dkv_n, dkr, dz, qg, kvg, tabs, tm=te, name="ml_blat")
    dxn2 = mm_dx(dh2, w_mi, "ml_dxn")
    mm_dw(("mla_w_in", 0), xn2, dh2, "ml_dwi", post=_mla_in_from_padded)
    dx, dxb, dg2 = _rms_bwd(x2, ws["mla_norm"][0:1], dxn2, dx, tm=tm, name="ml_bnorm")
    gs["mla_norm"], gs["mla_q_norm"], gs["mla_kv_norm"] = dg2, dqg, dkvg

    dy = mm_dx(dxb, w_cv[("conv_w_out", 0)], "cv_dy")
    mm_dw(("conv_w_out", 0), y1, dxb, "cv_dwo")
    dh1, dcw = _conv_bwd(h1, dy, cw, tm=te, name="cv_bmix")
    dxn1 = mm_dx(dh1, w_cv[("conv_w_in", 0)], "cv_dxn")
    mm_dw(("conv_w_in", 0), xn1, dh1, "cv_dwi")
    dx, dxb, dg1 = _rms_bwd(x1, ws["conv_norm"][0:1], dxn1, dx, tm=tm, name="cv_bnorm")
    gs["conv_norm"], gs["conv_w"] = dg1, dcw

    dx, dxb = pool_layer_bwd(dx, dxb, sv0, w_p0, 0, "p0", after=sink.late_ready())
    return loss_part, dx, gs


def kernel(x, positions, pool_norm, pool_w_in, pool_w_grp, pool_scale, pool_w_out, conv_norm, conv_w_in, conv_w, conv_w_out, mla_norm, mla_w_in, mla_q_norm, mla_w_q_up, mla_kv_norm, mla_w_kv_up, mla_w_out, final_norm, loss_target, m_pool_norm, m_pool_w_in, m_pool_w_grp, m_pool_scale, m_pool_w_out, m_conv_norm, m_conv_w_in, m_conv_w, m_conv_w_out, m_mla_norm, m_mla_w_in, m_mla_q_norm, m_mla_w_q_up, m_mla_kv_norm, m_mla_w_kv_up, m_mla_w_out, m_final_norm, v_pool_norm, v_pool_w_in, v_pool_w_grp, v_pool_scale, v_pool_w_out, v_conv_norm, v_conv_w_in, v_conv_w, v_conv_w_out, v_mla_norm, v_mla_w_in, v_mla_q_norm, v_mla_w_q_up, v_mla_kv_norm, v_mla_w_kv_up, v_mla_w_out, v_final_norm):
    names = ("pool_norm", "pool_w_in", "pool_w_grp", "pool_scale", "pool_w_out", "conv_norm", "conv_w_in", "conv_w", "conv_w_out",
             "mla_norm", "mla_w_in", "mla_q_norm", "mla_w_q_up", "mla_kv_norm", "mla_w_kv_up", "mla_w_out", "final_norm")
    w = dict(zip(names, (pool_norm, pool_w_in, pool_w_grp, pool_scale, pool_w_out, conv_norm, conv_w_in, conv_w, conv_w_out,
                         mla_norm, mla_w_in, mla_q_norm, mla_w_q_up, mla_kv_norm, mla_w_kv_up, mla_w_out, final_norm)))
    m = dict(zip(names, (m_pool_norm, m_pool_w_in, m_pool_w_grp, m_pool_scale, m_pool_w_out, m_conv_norm, m_conv_w_in, m_conv_w, m_conv_w_out,
                         m_mla_norm, m_mla_w_in, m_mla_q_norm, m_mla_w_q_up, m_mla_kv_norm, m_mla_w_kv_up, m_mla_w_out, m_final_norm)))
    v = dict(zip(names, (v_pool_norm, v_pool_w_in, v_pool_w_grp, v_pool_scale, v_pool_w_out, v_conv_norm, v_conv_w_in, v_conv_w, v_conv_w_out,
                         v_mla_norm, v_mla_w_in, v_mla_q_norm, v_mla_w_q_up, v_mla_kv_norm, v_mla_w_kv_up, v_mla_w_out, v_final_norm)))
    chip = 2 * lax.axis_index("x") + lax.axis_index("y")
    core = lax.axis_index("c")

    core1 = core.astype(jnp.int32).reshape(1)
    chip_core = jnp.stack([chip, core]).astype(jnp.int32)
    shard_shape = lambda piece: w[piece[0]].shape[1:]
    shard_axis = lambda piece: BIG_SHARD_AXIS[piece[0]] - 1
    full_shape = lambda piece: tuple(d * (N_CHIPS if a == shard_axis(piece) else 1) for a, d in enumerate(shard_shape(piece)))

    gather_rows, gather_at, packs = {}, {}, {}
    for grp, layout in GATHER_LAYOUT.items():
        gather_rows[grp], gather_at[grp] = _slot_rows(layout, shard_shape, PACK_ROW_ALIGN)
        packs[grp] = _pack_slot({(n, j): w[n][j] for (n, j), _, _ in layout}, layout, gather_rows[grp], BF16)

    def gathered_weights(grp, gathered):
        out = {}
        for piece, (r0, n, kind) in gather_at[grp].items():
            if kind == "flat":
                out[piece] = jnp.concatenate([gathered[k, r0:r0 + n].reshape(shard_shape(piece)) for k in range(N_CHIPS)], axis=shard_axis(piece))
            else:
                out[piece] = Packed(gathered, r0, kind, full_shape(piece))
        return out

    p0_start = _ag_ici_start(packs["p0"], w["final_norm"], name="ag_p0_start")
    cv_start = _ag_ici_start(packs["cv"], p0_start[4], name="ag_cv_start")
    ml_start = _ag_ici_start(packs["ml"], cv_start[4], name="ag_ml_start")
    in_flight = {"p0": p0_start, "cv": cv_start, "ml": ml_start}

    def weights_for(grp, after):
        if grp == "p0":
            after = (after, ml_start[4])
        send_sems, recv_sems, w_thru, land, _ = in_flight[grp]
        w_thru, land = _ag_ici_wait(send_sems, recv_sems, w_thru, land, after, name=f"ag_{grp}_wait")
        return gathered_weights(grp, _ag_sibling_forward(land, w_thru, name=f"ag_{grp}_fwd"))

    ws = {"conv_norm": w["conv_norm"], "final_norm": w["final_norm"]}
    ws.update(_gather_small({n: w[n] for n, _, _ in SMALL_GATHER}, name="ag_small"))

    reduce_rows, reduce_at = {}, {}
    for grp, layout in REDUCE_LAYOUT.items():
        reduce_rows[grp], reduce_at[grp] = _slot_rows(layout, shard_shape, RS_ROW_ALIGN)
    group_of = {piece: grp for grp, layout in REDUCE_LAYOUT.items() for piece, _, _ in layout}

    class Sink:
        def __init__(self):
            self.buf = {grp: lax.empty((N_CHIPS, rows, PACK_COLS), BF16) for grp, rows in reduce_rows.items()}
            self.started = {}

        def dest(self, piece):
            grp = group_of[piece]
            r0, _, kind = reduce_at[grp][piece]
            return None if kind == "flat" else Packed(self.buf[grp], r0, kind, full_shape(piece))

        def put(self, piece, result):
            grp = group_of[piece]
            r0, n, kind = reduce_at[grp][piece]
            if kind == "flat":
                parts = jnp.split(result, N_CHIPS, axis=shard_axis(piece))
                result = lax.dynamic_update_slice(self.buf[grp], jnp.stack([p.reshape(n, PACK_COLS) for p in parts]), (0, r0, 0))
            self.buf[grp] = result

        def start(self, grp, tag):
            theirs = _rs_sibling_swap(self.buf[grp], name=f"{tag}_swap")
            chip_sum = _add2_bf16(self.buf[grp], theirs, core1, name=f"{tag}_add2")
            self.started[grp] = _rs_chip_exchange_start(chip_sum, name=f"{tag}_chips_start")
            return self.started[grp][4]

        def finish(self, grp, after, tag):
            send_sems, recv_sems, chip_sum, land, _ = self.started[grp]
            chip_sum, recv = _rs_chip_exchange_wait(send_sems, recv_sems, chip_sum, land, after, name=f"{tag}_chips_wait")
            half_sum = _add4_f32(chip_sum, recv, chip_core, name=f"{tag}_add4")
            return _rs_sibling_join(half_sum, name=f"{tag}_join")

        def late_ready(self):
            return self.start("late", "rsa")

    sink = Sink()

    loss_part, grad_x, gs = _local_step(x[0], positions, loss_target[0], weights_for, ws, sink)
    loss = lax.psum(loss_part[0, 0], ("x", "y", "c"))

    g, delta, new_m, new_v = {}, {}, {}, {}

    def adam_big(n):
        nj = w[n].shape[0]
        where = [(group_of[(n, j)],) + reduce_at[group_of[(n, j)]][(n, j)] for j in range(nj)]
        if where[0][3] == "flat":
            g[n] = jnp.stack([g_rows[grp][r0:r0 + rows].reshape(w[n].shape[1:]) for grp, r0, rows, _ in where])
            shp = w[n].shape
            two_d = lambda a: a.reshape(-1, shp[-1])
            d_, m_, v_ = _adamw(two_d(w[n]), two_d(g[n]), two_d(m[n]), two_d(v[n]), name=f"adamw_{n}")
            delta[n], new_m[n], new_v[n] = d_.reshape(shp), m_.reshape(shp), v_.reshape(shp)
        else:
            g[n], delta[n], new_m[n], new_v[n] = _adamw_rows(w[n], m[n], v[n], [(g_rows[grp], r0) for grp, r0, _, _ in where], name=f"adamw_{n}")

    first_token = sink.start("first", "rsb")
    g_rows = {"late": sink.finish("late", first_token, "rsa")}
    late_only = [n for n in BIG if all(group_of[(n, j)] == "late" for j in range(w[n].shape[0]))]
    for n in late_only:
        adam_big(n)
    g_rows["first"] = sink.finish("first", tuple(delta[n] for n in late_only), "rsb")
    for n in BIG:
        if n not in late_only:
            adam_big(n)

    gs_sum = _reduce_small(gs, g_rows["first"], name="ar_small")
    row = lambda d: {n: (d[n].reshape(1, -1) if d[n].ndim == 1 else d[n]) for n in SMALL}
    small_out = _adamw_small(row(w), row(m), row(v), gs_sum, chip.astype(jnp.int32).reshape(1), name="adamw_small")
    for dst, res in zip((g, delta, new_m, new_v), small_out):
        for n in SMALL:
            dst[n] = res[n].reshape(w[n].shape)

    return (loss, grad_x[None], *[g[n] for n in names], *[delta[n] for n in names],
            *[new_m[n] for n in names], *[new_v[n] for n in names])
```

```python
import functools

import jax
import jax.numpy as jnp
from jax import lax
from jax.experimental import pallas as pl
from jax.experimental.pallas import tpu as pltpu

F32 = jnp.float32
BF16 = jnp.bfloat16

D_MODEL = 1024
D_INNER = 2048
POOL_WINDOWS = (2, 4, 8, 16)
POOL_GROUP = 512
N_HEADS = 16
QK_NOPE = 128
QK_ROPE = 64
V_DIM = 128
HEAD_PAD = 256
Q_LORA = 384
KV_LORA = 256
MLA_IN = Q_LORA + KV_LORA + QK_ROPE + D_INNER
MLA_IN_PAD = 2816
ATTN_SCALE = (QK_NOPE + QK_ROPE) ** -0.5
ROPE_BASE = 10000.0
NORM_EPS = 1e-6
HALO = 16

ADAM_LR = 0.001
ADAM_B1 = 0.9
ADAM_B2 = 0.999
ADAM_EPS = 1e-08
ADAM_WD = 0.01
ADAM_STEP = 10

N_CHIPS = 4
N_DEV = 8
LANES = 128
PACK_COLS = 1024
V7X_VMEM_LIMIT = 56 * 1024 * 1024
MESH = pl.DeviceIdType.MESH


def _cparams(*sem):
    return pltpu.CompilerParams(dimension_semantics=sem, vmem_limit_bytes=V7X_VMEM_LIMIT)


def _pick(n, cap):
    best = None
    for d in range(LANES, min(n, cap) + 1, LANES):
        if n % d == 0:
            best = d
    assert best is not None, (n, cap)
    return best


def _sigmoid(z):
    return 1.0 / (1.0 + jnp.exp(-z))


class Packed:
    def __init__(self, buf, r0, kind, shape):
        self.buf, self.r0, self.kind, self.shape = buf, r0, kind, shape

    def block(self, rb, cb, bk):
        assert self.r0 % bk == 0, (self.r0, bk)
        if self.kind == "cols":
            K = self.shape[0]
            per = self.shape[1] // (N_CHIPS * PACK_COLS)
            assert K % bk == 0
            return cb // per, (self.r0 + (cb % per) * K) // bk + rb
        kk = self.shape[0] // N_CHIPS
        assert kk % bk == 0
        per = kk // bk
        return rb // per, self.r0 // bk + rb % per


def _mm(a, b, *, trans_a=False, trans_b=False, out_dtype=F32, residual=None, after=None, into=None, tm, tn, tk, name):
    if trans_a:
        K, M = a.shape
    else:
        M, K = a.shape
    if trans_b:
        N, K2 = b.shape
    else:
        K2, N = b.shape
    assert K == K2 and M % tm == 0 and N % tn == 0 and K % tk == 0, (name, a.shape, b.shape, tm, tn, tk)
    nk = K // tk
    dn = (((0 if trans_a else 1,), (1 if trans_b else 0,)), ((), ()))
    has_res = residual is not None
    n_skip = (after is not None) + (into is not None)
    b_all = isinstance(b, Packed) and b.kind == "rows"
    o_all = into is not None and into.kind == "rows"
    assert not b_all or (not trans_a and (tn == N if trans_b else tk == K)), name
    assert not o_all or tm == M, name

    def body(*refs):
        if has_res:
            a_ref, b_ref, r_ref = refs[:3]
            refs = refs[3:]
        else:
            a_ref, b_ref = refs[:2]
            r_ref = None
            refs = refs[2:]
        refs = refs[n_skip:]
        o_ref, rest = refs[0], refs[1:]
        if b_all and trans_b:
            part = jnp.concatenate([lax.dot_general(a_ref[...], b_ref[c], dn, preferred_element_type=F32) for c in range(N_CHIPS)], axis=1)
        elif b_all:
            kk = K // N_CHIPS
            part = lax.dot_general(a_ref[:, 0:kk], b_ref[0], dn, preferred_element_type=F32)
            for c in range(1, N_CHIPS):
                part = part + lax.dot_general(a_ref[:, c * kk:(c + 1) * kk], b_ref[c], dn, preferred_element_type=F32)
        else:
            part = lax.dot_general(a_ref[...], b_ref[...], dn, preferred_element_type=F32)

        def finish(acc):
            if has_res:
                acc = acc + r_ref[...]
            if o_all:
                mk = M // N_CHIPS
                for c in range(N_CHIPS):
                    o_ref[c] = acc[c * mk:(c + 1) * mk].astype(o_ref.dtype)
            else:
                o_ref[...] = acc.astype(o_ref.dtype)

        if nk == 1:
            finish(part)
        else:
            acc_ref = rest[0]
            k = pl.program_id(2)

            @pl.when(k == 0)
            def _():
                acc_ref[...] = part

            @pl.when(k > 0)
            def _():
                acc_ref[...] += part

            @pl.when(k == nk - 1)
            def _():
                finish(acc_ref[...])

    a_spec = pl.BlockSpec((tk, tm), lambda i, j, k: (k, i)) if trans_a else pl.BlockSpec((tm, tk), lambda i, j, k: (i, k))
    if b_all:
        kkb = b.shape[0] // N_CHIPS
        b_spec = pl.BlockSpec((N_CHIPS, kkb, PACK_COLS), lambda i, j, k: (0, b.r0 // kkb, 0))
        b_arg = b.buf
    elif isinstance(b, Packed):
        if trans_b:
            assert tk == PACK_COLS
            b_spec = pl.BlockSpec((None, tn, tk), lambda i, j, k: (*b.block(j, k, tn), 0))
        else:
            assert tn == PACK_COLS
            b_spec = pl.BlockSpec((None, tk, tn), lambda i, j, k: (*b.block(k, j, tk), 0))
        b_arg = b.buf
    else:
        b_spec = pl.BlockSpec((tn, tk), lambda i, j, k: (j, k)) if trans_b else pl.BlockSpec((tk, tn), lambda i, j, k: (k, j))
        b_arg = b
    o_spec = pl.BlockSpec((tm, tn), lambda i, j, k: (i, j))
    in_specs = [a_spec, b_spec] + ([o_spec] if has_res else [])
    args = (a, b_arg) + ((residual,) if has_res else ())
    aliases = {}
    if after is not None:
        in_specs.append(pl.BlockSpec(memory_space=pl.ANY))
        args += (after,)
    if into is None:
        out_shape, out_spec = jax.ShapeDtypeStruct((M, N), out_dtype), o_spec
    else:
        assert tn == PACK_COLS and into.shape == (M, N)
        in_specs.append(pl.BlockSpec(memory_space=pl.ANY))
        aliases = {len(args): 0}
        args += (into.buf,)
        out_shape = jax.ShapeDtypeStruct(into.buf.shape, into.buf.dtype)
        if o_all:
            out_spec = pl.BlockSpec((N_CHIPS, M // N_CHIPS, tn), lambda i, j, k: (0, into.r0 // (M // N_CHIPS), 0))
        else:
            out_spec = pl.BlockSpec((None, tm, tn), lambda i, j, k: (*into.block(i, j, tm), 0))
    return pl.pallas_call(
        body, name=name, out_shape=out_shape,
        grid=(M // tm, N // tn, nk),
        in_specs=in_specs, out_specs=out_spec, input_output_aliases=aliases,
        scratch_shapes=[pltpu.VMEM((tm, tn), F32)] if nk > 1 else [],
        compiler_params=_cparams("parallel", "parallel", "arbitrary"),
    )(*args)


def _grouped_tn(a, b, *, tk, name):
    S = a.shape[0]
    G = POOL_GROUP
    nk = S // tk

    def body(a_ref, b_ref, o_ref, acc_ref):
        k = pl.program_id(1)
        part = lax.dot_general(a_ref[...], b_ref[...], (((0,), (0,)), ((), ())), preferred_element_type=F32)

        @pl.when(k == 0)
        def _():
            acc_ref[...] = part

        @pl.when(k > 0)
        def _():
            acc_ref[...] += part

        @pl.when(k == nk - 1)
        def _():
            o_ref[...] = acc_ref[...].astype(o_ref.dtype)

    return pl.pallas_call(
        body, name=name,
        out_shape=jax.ShapeDtypeStruct((len(POOL_WINDOWS), G, G), BF16),
        grid=(len(POOL_WINDOWS), nk),
        in_specs=[pl.BlockSpec((tk, G), lambda g, k: (k, g)), pl.BlockSpec((tk, G), lambda g, k: (k, g))],
        out_specs=pl.BlockSpec((None, G, G), lambda g, k: (g, 0, 0)),
        scratch_shapes=[pltpu.VMEM((G, G), F32)],
        compiler_params=_cparams("parallel", "arbitrary"),
    )(a, b)


def _rms_fwd(x, g, *, tm, name):
    S, D = x.shape

    def body(x_ref, g_ref, o_ref):
        xv = x_ref[...]
        rstd = lax.rsqrt(jnp.mean(xv * xv, axis=-1, keepdims=True) + NORM_EPS)
        o_ref[...] = (xv * rstd * g_ref[...]).astype(o_ref.dtype)

    return pl.pallas_call(
        body, name=name,
        out_shape=jax.ShapeDtypeStruct((S, D), BF16),
        grid=(S // tm,),
        in_specs=[pl.BlockSpec((tm, D), lambda i: (i, 0)), pl.BlockSpec((1, D), lambda i: (0, 0))],
        out_specs=pl.BlockSpec((tm, D), lambda i: (i, 0)),
        compiler_params=_cparams("parallel"),
    )(x, g)


def _rms_bwd_math(xv, gv, dxn):
    rstd = lax.rsqrt(jnp.mean(xv * xv, axis=-1, keepdims=True) + NORM_EPS)
    xh = xv * rstd
    dg = jnp.sum(dxn * xh, axis=0, keepdims=True)
    dxh = dxn * gv
    dx = rstd * (dxh - xh * jnp.mean(dxh * xh, axis=-1, keepdims=True))
    return dx, dg


def _rms_bwd(x, g, dxn, dres, *, tm, name):
    S, D = x.shape

    def body(x_ref, g_ref, dxn_ref, dres_ref, dx_ref, dxb_ref, dg_ref):
        dx, dg = _rms_bwd_math(x_ref[...], g_ref[...], dxn_ref[...])
        dx = dx + dres_ref[...]
        dx_ref[...] = dx
        dxb_ref[...] = dx.astype(BF16)

        @pl.when(pl.program_id(0) == 0)
        def _():
            dg_ref[...] = dg

        @pl.when(pl.program_id(0) > 0)
        def _():
            dg_ref[...] += dg

    row = pl.BlockSpec((tm, D), lambda i: (i, 0))
    vec = pl.BlockSpec((1, D), lambda i: (0, 0))
    return pl.pallas_call(
        body, name=name,
        out_shape=(jax.ShapeDtypeStruct((S, D), F32), jax.ShapeDtypeStruct((S, D), BF16), jax.ShapeDtypeStruct((1, D), F32)),
        grid=(S // tm,),
        in_specs=[row, vec, row, row],
        out_specs=(row, row, vec),
        compiler_params=_cparams("arbitrary"),
    )(x, g, dxn, dres)


def _final_loss(x, g, target, *, tm, name):
    S, D = x.shape

    def body(x_ref, g_ref, t_ref, loss_ref, dx_ref, dxb_ref, dg_ref):
        xv = x_ref[...]
        gv = g_ref[...]
        rstd = lax.rsqrt(jnp.mean(xv * xv, axis=-1, keepdims=True) + NORM_EPS)
        xh = xv * rstd
        err = xh * gv - t_ref[...]
        part = 0.5 * jnp.sum(jnp.mean(err * err, axis=-1, keepdims=True), axis=0, keepdims=True)
        dy = err * (1.0 / D)
        dg = jnp.sum(dy * xh, axis=0, keepdims=True)
        dxh = dy * gv
        dx = rstd * (dxh - xh * jnp.mean(dxh * xh, axis=-1, keepdims=True))
        dx_ref[...] = dx
        dxb_ref[...] = dx.astype(BF16)
        lossb = jnp.broadcast_to(part, loss_ref.shape)

        @pl.when(pl.program_id(0) == 0)
        def _():
            dg_ref[...] = dg
            loss_ref[...] = lossb

        @pl.when(pl.program_id(0) > 0)
        def _():
            dg_ref[...] += dg
            loss_ref[...] += lossb

    row = pl.BlockSpec((tm, D), lambda i: (i, 0))
    vec = pl.BlockSpec((1, D), lambda i: (0, 0))
    lspec = pl.BlockSpec((8, LANES), lambda i: (0, 0))
    return pl.pallas_call(
        body, name=name,
        out_shape=(jax.ShapeDtypeStruct((8, LANES), F32), jax.ShapeDtypeStruct((S, D), F32),
                   jax.ShapeDtypeStruct((S, D), BF16), jax.ShapeDtypeStruct((1, D), F32)),
        grid=(S // tm,),
        in_specs=[row, vec, row],
        out_specs=(lspec, row, row, vec),
        compiler_params=_cparams("arbitrary"),
    )(x, g, target)


def _prev_halo_spec(tm, width, col):
    r = tm // HALO
    return pl.BlockSpec((HALO, width), lambda i: (jnp.maximum(i * r - 1, 0), col))


def _next_halo_spec(tm, width, col, S):
    r = tm // HALO
    last = S // HALO - 1
    return pl.BlockSpec((HALO, width), lambda i: (jnp.minimum((i + 1) * r, last), col))


def _shift_down(ext, k):
    return pltpu.roll(ext, k, 0)[HALO:, :]


def _shift_up(ext, k, tm):
    n = ext.shape[0]
    return pltpu.roll(ext, n - k, 0)[:tm, :]


def _pool_window_sum(ext, w):
    s = ext
    k = 1
    while k < w:
        s = s + pltpu.roll(s, k, 0)
        k *= 2
    return s[HALO:, :]


def _pooled_group(u_ref, halo, g, w, t_idx):
    cs = slice(g * POOL_GROUP, (g + 1) * POOL_GROUP)
    u = u_ref[:, cs]
    ext = jnp.concatenate([halo[:, cs], u], axis=0)
    inv = 1.0 / jnp.minimum(t_idx + 1, w).astype(F32)
    return _pool_window_sum(ext, w) * inv - u


def _pool_fwd(h, w_grp, scale, *, tm, name):
    S = h.shape[0]
    E = D_INNER

    def body(u_ref, uh_ref, z_ref, wg_ref, sc_ref, y_ref):
        i = pl.program_id(0)
        halo = jnp.where(i > 0, uh_ref[...], 0.0)
        t_idx = i * tm + lax.broadcasted_iota(jnp.int32, (tm, 1), 0)
        for g, w in enumerate(POOL_WINDOWS):
            cs = slice(g * POOL_GROUP, (g + 1) * POOL_GROUP)
            pooled = _pooled_group(u_ref, halo, g, w, t_idx)
            mixed = jnp.dot(pooled.astype(BF16), wg_ref[g], preferred_element_type=F32)
            z = z_ref[:, cs]
            y_ref[:, cs] = (mixed * sc_ref[:, cs] * (z * _sigmoid(z))).astype(BF16)

    return pl.pallas_call(
        body, name=name,
        out_shape=jax.ShapeDtypeStruct((S, E), BF16),
        grid=(S // tm,),
        in_specs=[pl.BlockSpec((tm, E), lambda i: (i, 0)), _prev_halo_spec(tm, E, 0),
                  pl.BlockSpec((tm, E), lambda i: (i, 1)),
                  pl.BlockSpec((len(POOL_WINDOWS), POOL_GROUP, POOL_GROUP), lambda i: (0, 0, 0)),
                  pl.BlockSpec((1, E), lambda i: (0, 0))],
        out_specs=pl.BlockSpec((tm, E), lambda i: (i, 0)),
        compiler_params=_cparams("parallel"),
    )(h, h, h, w_grp, scale)


def _pool_bwd1(h, dy, w_grp, scale, *, tm, name):
    S = h.shape[0]
    E = D_INNER

    def body(u_ref, uh_ref, z_ref, dy_ref, wg_ref, sc_ref, pooled_ref, dmixed_ref, dpooled_ref, dz_ref, dsc_ref):
        i = pl.program_id(0)
        halo = jnp.where(i > 0, uh_ref[...], 0.0)
        t_idx = i * tm + lax.broadcasted_iota(jnp.int32, (tm, 1), 0)
        for g, w in enumerate(POOL_WINDOWS):
            cs = slice(g * POOL_GROUP, (g + 1) * POOL_GROUP)
            pooled = _pooled_group(u_ref, halo, g, w, t_idx).astype(BF16)
            wg = wg_ref[g]
            mixed = jnp.dot(pooled, wg, preferred_element_type=F32)
            z = z_ref[:, cs]
            sg = _sigmoid(z)
            dyv = dy_ref[:, cs]
            sc = sc_ref[:, cs]
            dms = dyv * (z * sg)
            dz = dyv * (mixed * sc) * (sg * (1.0 + z * (1.0 - sg)))
            dsc = jnp.sum(dms * mixed, axis=0, keepdims=True)
            dmixed = (dms * sc).astype(BF16)
            dpooled = lax.dot_general(dmixed, wg, (((1,), (1,)), ((), ())), preferred_element_type=F32)
            pooled_ref[:, cs] = pooled
            dmixed_ref[:, cs] = dmixed
            dpooled_ref[:, cs] = dpooled
            dz_ref[:, cs] = dz.astype(BF16)

            @pl.when(i == 0)
            def _():
                dsc_ref[:, cs] = dsc

            @pl.when(i > 0)
            def _():
                dsc_ref[:, cs] += dsc

    row = pl.BlockSpec((tm, E), lambda i: (i, 0))
    vec = pl.BlockSpec((1, E), lambda i: (0, 0))
    return pl.pallas_call(
        body, name=name,
        out_shape=(jax.ShapeDtypeStruct((S, E), BF16), jax.ShapeDtypeStruct((S, E), BF16),
                   jax.ShapeDtypeStruct((S, E), F32), jax.ShapeDtypeStruct((S, E), BF16),
                   jax.ShapeDtypeStruct((1, E), F32)),
        grid=(S // tm,),
        in_specs=[row, _prev_halo_spec(tm, E, 0), pl.BlockSpec((tm, E), lambda i: (i, 1)), row,
                  pl.BlockSpec((len(POOL_WINDOWS), POOL_GROUP, POOL_GROUP), lambda i: (0, 0, 0)), vec],
        out_specs=(row, row, row, row, vec),
        compiler_params=_cparams("arbitrary"),
    )(h, h, h, dy, w_grp, scale)


def _pool_bwd2(dpooled, dz, *, tm, name):
    S = dpooled.shape[0]
    E = D_INNER
    nt = S // tm

    def body(dp_ref, dpn_ref, dz_ref, dh_ref):
        i = pl.program_id(0)
        nxt = jnp.where(i < nt - 1, dpn_ref[...], 0.0)
        t_ext = i * tm + lax.broadcasted_iota(jnp.int32, (tm + HALO, 1), 0)
        for g, w in enumerate(POOL_WINDOWS):
            cs = slice(g * POOL_GROUP, (g + 1) * POOL_GROUP)
            dp = dp_ref[:, cs]
            inv = 1.0 / jnp.minimum(t_ext + 1, w).astype(F32)
            s = jnp.concatenate([dp, nxt[:, cs]], axis=0) * inv
            n = tm + HALO
            k = 1
            while k < w:
                s = s + pltpu.roll(s, n - k, 0)
                k *= 2
            dh_ref[:, cs] = (s[:tm, :] - dp).astype(BF16)
        dh_ref[:, E:] = dz_ref[...]

    return pl.pallas_call(
        body, name=name,
        out_shape=jax.ShapeDtypeStruct((S, 2 * E), BF16),
        grid=(nt,),
        in_specs=[pl.BlockSpec((tm, E), lambda i: (i, 0)), _next_halo_spec(tm, E, 0, S),
                  pl.BlockSpec((tm, E), lambda i: (i, 0))],
        out_specs=pl.BlockSpec((tm, 2 * E), lambda i: (i, 0)),
        compiler_params=_cparams("parallel"),
    )(dpooled, dpooled, dz)


CONV_CHUNK = 512


def _conv_fwd(h, cw, *, tm, name):
    S = h.shape[0]
    E = D_INNER

    def body(b_ref, c_ref, hh_ref, z_ref, ch_ref, hhh_ref, w_ref, y_ref):
        i = pl.program_id(0)
        for j in range(E // CONV_CHUNK):
            cs = slice(j * CONV_CHUNK, (j + 1) * CONV_CHUNK)
            p = c_ref[:, cs] * hh_ref[:, cs]
            ph = jnp.where(i > 0, ch_ref[:, cs] * hhh_ref[:, cs], 0.0)
            ext = jnp.concatenate([ph, p], axis=0)
            conv = w_ref[2:3, cs] * p + w_ref[1:2, cs] * _shift_down(ext, 1) + w_ref[0:1, cs] * _shift_down(ext, 2)
            z = z_ref[:, cs]
            y_ref[:, cs] = (b_ref[:, cs] * conv * (z * _sigmoid(z))).astype(BF16)

    col = lambda c: pl.BlockSpec((tm, E), lambda i: (i, c))
    return pl.pallas_call(
        body, name=name,
        out_shape=jax.ShapeDtypeStruct((S, E), BF16),
        grid=(S // tm,),
        in_specs=[col(0), col(1), col(2), col(3), _prev_halo_spec(tm, E, 1), _prev_halo_spec(tm, E, 2),
                  pl.BlockSpec((8, E), lambda i: (0, 0))],
        out_specs=pl.BlockSpec((tm, E), lambda i: (i, 0)),
        compiler_params=_cparams("parallel"),
    )(h, h, h, h, h, h, cw)


def _conv_bwd(h, dy, cw, *, tm, name):
    S = h.shape[0]
    E = D_INNER
    nt = S // tm

    def body(b_ref, c_ref, hh_ref, z_ref, dy_ref, ch_ref, hhh_ref, bn_ref, zn_ref, dyn_ref, w_ref, dh_ref, dw_ref):
        i = pl.program_id(0)
        for j in range(E // CONV_CHUNK):
            cs = slice(j * CONV_CHUNK, (j + 1) * CONV_CHUNK)
            w0, w1, w2 = w_ref[0:1, cs], w_ref[1:2, cs], w_ref[2:3, cs]
            c, hh, b, z, dyv = c_ref[:, cs], hh_ref[:, cs], b_ref[:, cs], z_ref[:, cs], dy_ref[:, cs]
            p = c * hh
            ph = jnp.where(i > 0, ch_ref[:, cs] * hhh_ref[:, cs], 0.0)
            ext = jnp.concatenate([ph, p], axis=0)
            pm1 = _shift_down(ext, 1)
            pm2 = _shift_down(ext, 2)
            conv = w2 * p + w1 * pm1 + w0 * pm2
            sg = _sigmoid(z)
            dy0 = dyv * (z * sg)
            dz = dyv * (b * conv) * (sg * (1.0 + z * (1.0 - sg)))
            db = dy0 * conv
            dconv = dy0 * b
            zn = zn_ref[:, cs]
            dconv_n = jnp.where(i < nt - 1, dyn_ref[:, cs] * (zn * _sigmoid(zn)) * bn_ref[:, cs], 0.0)
            dext = jnp.concatenate([dconv, dconv_n], axis=0)
            dp = w2 * dconv + w1 * _shift_up(dext, 1, tm) + w0 * _shift_up(dext, 2, tm)
            dh_ref[:, 0 * E + j * CONV_CHUNK:0 * E + (j + 1) * CONV_CHUNK] = db.astype(BF16)
            dh_ref[:, 1 * E + j * CONV_CHUNK:1 * E + (j + 1) * CONV_CHUNK] = (dp * hh).astype(BF16)
            dh_ref[:, 2 * E + j * CONV_CHUNK:2 * E + (j + 1) * CONV_CHUNK] = (dp * c).astype(BF16)
            dh_ref[:, 3 * E + j * CONV_CHUNK:3 * E + (j + 1) * CONV_CHUNK] = dz.astype(BF16)
            dw = jnp.concatenate([jnp.sum(dconv * pm2, axis=0, keepdims=True),
                                  jnp.sum(dconv * pm1, axis=0, keepdims=True),
                                  jnp.sum(dconv * p, axis=0, keepdims=True),
                                  jnp.zeros((5, CONV_CHUNK), F32)], axis=0)

            @pl.when(i == 0)
            def _():
                dw_ref[:, cs] = dw

            @pl.when(i > 0)
            def _():
                dw_ref[:, cs] += dw

    col = lambda c: pl.BlockSpec((tm, E), lambda i: (i, c))
    return pl.pallas_call(
        body, name=name,
        out_shape=(jax.ShapeDtypeStruct((S, 4 * E), BF16), jax.ShapeDtypeStruct((8, E), F32)),
        grid=(nt,),
        in_specs=[col(0), col(1), col(2), col(3), pl.BlockSpec((tm, E), lambda i: (i, 0)),
                  _prev_halo_spec(tm, E, 1), _prev_halo_spec(tm, E, 2),
                  _next_halo_spec(tm, E, 0, S), _next_halo_spec(tm, E, 3, S), _next_halo_spec(tm, E, 0, S),
                  pl.BlockSpec((8, E), lambda i: (0, 0))],
        out_specs=(pl.BlockSpec((tm, 4 * E), lambda i: (i, 0)), pl.BlockSpec((8, E), lambda i: (0, 0))),
        compiler_params=_cparams("arbitrary"),
    )(h, h, h, h, dy, h, h, h, h, dy, cw)


Z_COLS = D_INNER // LANES
KV_LAT_BLK = D_INNER // KV_LORA
Q_LAT_BLK = (D_INNER + KV_LORA) // Q_LORA
K_ROPE_BLK = (D_INNER + KV_LORA + Q_LORA) // LANES


def _rope(blk, c, s1, s2):
    return blk * c + pltpu.roll(blk, LANES - QK_ROPE // 2, 1) * s1 + pltpu.roll(blk, QK_ROPE // 2, 1) * s2


def _unrope(blk, c, s1, s2):
    return blk * c - pltpu.roll(blk, LANES - QK_ROPE // 2, 1) * s1 - pltpu.roll(blk, QK_ROPE // 2, 1) * s2


def _lat_norm(v, g):
    rstd = lax.rsqrt(jnp.mean(v * v, axis=-1, keepdims=True) + NORM_EPS)
    return v * rstd * g


def _mla_latent_fwd(h, q_norm, kv_norm, tabs, *, tm, name):
    S = h.shape[0]

    def body(kv_ref, q_ref, kr_ref, qg_ref, kvg_ref, c_ref, s1_ref, s2_ref, qn_ref, kvn_ref, krr_ref):
        qn_ref[...] = _lat_norm(q_ref[...], qg_ref[...]).astype(BF16)
        kvn_ref[...] = _lat_norm(kv_ref[...], kvg_ref[...]).astype(BF16)
        krr_ref[...] = _rope(kr_ref[...], c_ref[...], s1_ref[...], s2_ref[...]).astype(BF16)

    tab = pl.BlockSpec((tm, LANES), lambda i: (i, 0))
    return pl.pallas_call(
        body, name=name,
        out_shape=(jax.ShapeDtypeStruct((S, Q_LORA), BF16), jax.ShapeDtypeStruct((S, KV_LORA), BF16),
                   jax.ShapeDtypeStruct((S, LANES), BF16)),
        grid=(S // tm,),
        in_specs=[pl.BlockSpec((tm, KV_LORA), lambda i: (i, KV_LAT_BLK)), pl.BlockSpec((tm, Q_LORA), lambda i: (i, Q_LAT_BLK)),
                  pl.BlockSpec((tm, LANES), lambda i: (i, K_ROPE_BLK)),
                  pl.BlockSpec((1, Q_LORA), lambda i: (0, 0)), pl.BlockSpec((1, KV_LORA), lambda i: (0, 0)), tab, tab, tab],
        out_specs=(pl.BlockSpec((tm, Q_LORA), lambda i: (i, 0)), pl.BlockSpec((tm, KV_LORA), lambda i: (i, 0)), tab),
        compiler_params=_cparams("parallel"),
    )(h, h, h, q_norm, kv_norm, *tabs)


def _mla_q_up(q_n, w_q_pad, tabs, *, tm, name):
    S = q_n.shape[0]

    def body(a_ref, w_ref, c_ref, s1_ref, s2_ref, o_ref):
        a = a_ref[...]
        for hd in range(N_HEADS):
            acc = jnp.dot(a, w_ref[:, hd * HEAD_PAD:(hd + 1) * HEAD_PAD], preferred_element_type=F32)
            o_ref[hd, :, :QK_NOPE] = acc[:, :QK_NOPE].astype(BF16)
            o_ref[hd, :, QK_NOPE:] = _rope(acc[:, QK_NOPE:], c_ref[...], s1_ref[...], s2_ref[...]).astype(BF16)

    tab = pl.BlockSpec((tm, LANES), lambda i: (i, 0))
    return pl.pallas_call(
        body, name=name,
        out_shape=jax.ShapeDtypeStruct((N_HEADS, S, HEAD_PAD), BF16),
        grid=(S // tm,),
        in_specs=[pl.BlockSpec((tm, Q_LORA), lambda i: (i, 0)), pl.BlockSpec((Q_LORA, N_HEADS * HEAD_PAD), lambda i: (0, 0)),
                  tab, tab, tab],
        out_specs=pl.BlockSpec((N_HEADS, tm, HEAD_PAD), lambda i: (0, i, 0)),
        compiler_params=_cparams("parallel"),
    )(q_n, w_q_pad, *tabs)


def _mla_kv_up(kv_n, w_kv, krr, *, tm, name):
    S = kv_n.shape[0]
    heads_per_chip = N_HEADS // N_CHIPS

    def body(a_ref, w_ref, krr_ref, k_ref, v_ref):
        a = a_ref[...]
        ones = jnp.ones((tm, V_DIM), BF16)
        for hd in range(N_HEADS):
            lo = (hd % heads_per_chip) * HEAD_PAD
            acc = jnp.dot(a, w_ref[hd // heads_per_chip, :, lo:lo + HEAD_PAD], preferred_element_type=F32)
            k_ref[hd, :, :QK_NOPE] = acc[:, :QK_NOPE].astype(BF16)
            k_ref[hd, :, QK_NOPE:] = krr_ref[...]
            v_ref[hd, :, :V_DIM] = acc[:, QK_NOPE:].astype(BF16)
            v_ref[hd, :, V_DIM:] = ones

    head_blk = pl.BlockSpec((N_HEADS, tm, HEAD_PAD), lambda i: (0, i, 0))
    sds = jax.ShapeDtypeStruct((N_HEADS, S, HEAD_PAD), BF16)
    return pl.pallas_call(
        body, name=name, out_shape=(sds, sds),
        grid=(S // tm,),
        in_specs=[pl.BlockSpec((tm, KV_LORA), lambda i: (i, 0)),
                  pl.BlockSpec((N_CHIPS, KV_LORA, PACK_COLS), lambda i: (0, w_kv.r0 // KV_LORA, 0)),
                  pl.BlockSpec((tm, LANES), lambda i: (i, 0))],
        out_specs=(head_blk, head_blk),
        compiler_params=_cparams("parallel"),
    )(kv_n, w_kv.buf, krr)


LOG2E = 1.4426950408889634
SCORE_TO_LOG2 = ATTN_SCALE * LOG2E


def _flash_fwd(q_full, k_full, v_aug, h, *, tq, name):
    H, S, _ = q_full.shape
    tk = tq
    HP = 2
    QT = max(t for t in (4, 2, 1) if (S // tq) % t == 0)
    rows_of = lambda t: slice(t * tq, (t + 1) * tq)

    def body(q_ref, k_ref, v_ref, z_ref, o_ref, y_ref, lse_ref, m_sc, acc_sc):
        first = pl.program_id(1) * QT
        m_sc[...] = jnp.full(m_sc.shape, -1e30, F32)
        acc_sc[...] = jnp.zeros(acc_sc.shape, F32)

        def chunk(j, masked):
            off = pl.multiple_of(j * tk, tk)
            for hh in range(HP):
                kj = k_ref[hh, pl.ds(off, tk), :]
                vj = v_ref[hh, pl.ds(off, tk), :]
                for t in range(QT):
                    if masked[t] is None:
                        continue
                    rows = rows_of(t)
                    s = lax.dot_general(q_ref[hh, rows], kj, (((1,), (1,)), ((), ())), preferred_element_type=F32) * SCORE_TO_LOG2
                    if masked[t]:
                        keep = lax.broadcasted_iota(jnp.int32, (tq, tk), 1) <= lax.broadcasted_iota(jnp.int32, (tq, tk), 0)
                        s = jnp.where(keep, s, -1e30)
                    m_old = m_sc[hh, rows]
                    m_new = jnp.maximum(m_old, jnp.max(s, axis=1, keepdims=True))
                    p = jnp.exp2(s - jnp.tile(m_new, (1, tk // LANES)))
                    alpha = jnp.exp2(m_old - m_new)
                    acc_sc[hh, rows] = jnp.tile(alpha, (1, 2)) * acc_sc[hh, rows] + jnp.dot(p.astype(BF16), vj, preferred_element_type=F32)
                    m_sc[hh, rows] = m_new

        def step(j, carry):
            chunk(j, (False,) * QT)
            return carry

        lax.fori_loop(0, first, step, 0)
        for d in range(QT):
            chunk(first + d, tuple(None if t < d else t == d for t in range(QT)))
        for hh in range(HP):
            cs = slice(hh * V_DIM, (hh + 1) * V_DIM)
            for t in range(QT):
                rows = rows_of(t)
                l = acc_sc[hh, rows, V_DIM:]
                o = acc_sc[hh, rows, :V_DIM] / l
                z = z_ref[rows, cs]
                o_ref[rows, cs] = o
                y_ref[rows, cs] = (o * (z * _sigmoid(z))).astype(BF16)
                lse_ref[hh, t] = (m_sc[hh, rows] + jnp.log2(l)).T[0:1, :]

    pair = pl.BlockSpec((QT * tq, HP * V_DIM), lambda hd, i: (i, hd))
    return pl.pallas_call(
        body, name=name,
        out_shape=(jax.ShapeDtypeStruct((S, D_INNER), F32), jax.ShapeDtypeStruct((S, D_INNER), BF16),
                   jax.ShapeDtypeStruct((H, S // tq, 1, tq), F32)),
        grid=(H // HP, S // (QT * tq)),
        in_specs=[pl.BlockSpec((HP, QT * tq, HEAD_PAD), lambda hd, i: (hd, i, 0)),
                  pl.BlockSpec((HP, S, HEAD_PAD), lambda hd, i: (hd, 0, 0)),
                  pl.BlockSpec((HP, S, HEAD_PAD), lambda hd, i: (hd, 0, 0)),
                  pair],
        out_specs=(pair, pair, pl.BlockSpec((HP, QT, 1, tq), lambda hd, i: (hd, i, 0, 0))),
        scratch_shapes=[pltpu.VMEM((HP, QT * tq, LANES), F32), pltpu.VMEM((HP, QT * tq, HEAD_PAD), F32)],
        compiler_params=_cparams("parallel", "parallel"),
    )(q_full, k_full, v_aug, h)


def _mla_gate_bwd(dy, o, h, *, tm, name):
    S = dy.shape[0]
    E = D_INNER

    def body(dy_ref, o_ref, z_ref, do_ref, dz_ref, delta_ref):
        for hd in range(N_HEADS):
            cs = slice(hd * V_DIM, (hd + 1) * V_DIM)
            z = z_ref[:, cs]
            sg = _sigmoid(z)
            dyv = dy_ref[:, cs]
            ov = o_ref[:, cs]
            do = dyv * (z * sg)
            do_ref[:, cs] = do.astype(BF16)
            dz_ref[:, cs] = (dyv * ov * (sg * (1.0 + z * (1.0 - sg)))).astype(BF16)
            delta_ref[hd, 0] = jnp.broadcast_to(jnp.sum(do * ov, axis=-1, keepdims=True), (tm, LANES)).T[0:1, :]

    row = pl.BlockSpec((tm, E), lambda i: (i, 0))
    return pl.pallas_call(
        body, name=name,
        out_shape=(jax.ShapeDtypeStruct((S, E), BF16), jax.ShapeDtypeStruct((S, E), BF16),
                   jax.ShapeDtypeStruct((N_HEADS, S // tm, 1, tm), F32)),
        grid=(S // tm,),
        in_specs=[row, row, row],
        out_specs=(row, row, pl.BlockSpec((N_HEADS, 1, 1, tm), lambda i: (0, i, 0, 0))),
        compiler_params=_cparams("parallel"),
    )(dy, o, h)


def _flash_bwd(q_full, k_full, v_aug, do, lse_rows, delta_rows, tabs, *, tq, name):
    H, S, _ = q_full.shape
    tk = tq
    nq = S // tq
    HP = 2
    KT = max(t for t in (4, 2, 1) if nq % t == 0)
    rows_of = lambda t: slice(t * tk, (t + 1) * tk)

    def body(q_ref, k_ref, v_ref, do_ref, lse_ref, dl_ref, c_ref, s1_ref, s2_ref, dqp_ref, dkv_ref, dkr_ref, dq_ref, dk_sc, dv_sc):
        first = pl.program_id(1) * KT

        @pl.when(first == 0)
        def _():
            dq_ref[...] = jnp.zeros(dq_ref.shape, F32)

        dk_sc[...] = jnp.zeros(dk_sc.shape, F32)
        dv_sc[...] = jnp.zeros(dv_sc.shape, F32)

        def chunk(qi, masked):
            off = pl.multiple_of(qi * tq, tq)
            for hh in range(HP):
                q = q_ref[hh, pl.ds(off, tq), :]
                dov = do_ref[pl.ds(off, tq), hh * V_DIM:(hh + 1) * V_DIM]
                for t in range(KT):
                    if masked[t] is None:
                        continue
                    rows = rows_of(t)
                    k = k_ref[hh, rows]
                    s_t = lax.dot_general(k, q, (((1,), (1,)), ((), ())), preferred_element_type=F32) * SCORE_TO_LOG2
                    p_t = jnp.exp2(s_t - lse_ref[hh, qi])
                    if masked[t]:
                        keep = lax.broadcasted_iota(jnp.int32, (tk, tq), 0) <= lax.broadcasted_iota(jnp.int32, (tk, tq), 1)
                        p_t = jnp.where(keep, p_t, 0.0)
                    dv_sc[hh, rows] += jnp.dot(p_t.astype(BF16), dov, preferred_element_type=F32)
                    dp_t = lax.dot_general(v_ref[hh, rows], dov, (((1,), (1,)), ((), ())), preferred_element_type=F32)
                    ds = (p_t * (dp_t - dl_ref[hh, qi])).astype(BF16)
                    dk_sc[hh, rows] += jnp.dot(ds, q, preferred_element_type=F32)
                    dq_ref[pl.ds(off, tq), hh * HEAD_PAD:(hh + 1) * HEAD_PAD] += lax.dot_general(
                        ds, k, (((0,), (0,)), ((), ())), preferred_element_type=F32)

        def step(qi, carry):
            chunk(qi, (False,) * KT)
            return carry

        for d in range(KT):
            chunk(first + d, tuple(None if t > d else t == d for t in range(KT)))
            done = pl.ds(pl.multiple_of((first + d) * tq, tq), tq)
            here = rows_of(d)
            for hh in range(HP):
                lo = hh * HEAD_PAD
                dqp_ref[here, lo:lo + QK_NOPE] = (dq_ref[done, lo:lo + QK_NOPE] * ATTN_SCALE).astype(BF16)
                dqp_ref[here, lo + QK_NOPE:lo + HEAD_PAD] = _unrope(dq_ref[done, lo + QK_NOPE:lo + HEAD_PAD] * ATTN_SCALE,
                                                                    c_ref[here, :], s1_ref[here, :], s2_ref[here, :]).astype(BF16)
        lax.fori_loop(first + KT, nq, step, 0)
        for hh in range(HP):
            lo = hh * HEAD_PAD
            dkv_ref[:, lo:lo + QK_NOPE] = (dk_sc[hh, :, :QK_NOPE] * ATTN_SCALE).astype(BF16)
            dkv_ref[:, lo + QK_NOPE:lo + HEAD_PAD] = dv_sc[hh].astype(BF16)
            dkr_ref[hh] = dk_sc[hh, :, QK_NOPE:] * ATTN_SCALE

    tab = pl.BlockSpec((KT * tk, LANES), lambda hd, j: (j, 0))
    pair_rows = pl.BlockSpec((KT * tk, HP * HEAD_PAD), lambda hd, j: (j, hd))
    return pl.pallas_call(
        body, name=name,
        out_shape=(jax.ShapeDtypeStruct((S, H * HEAD_PAD), BF16), jax.ShapeDtypeStruct((S, H * HEAD_PAD), BF16),
                   jax.ShapeDtypeStruct((H, S, LANES), F32)),
        grid=(H // HP, S // (KT * tk)),
        in_specs=[pl.BlockSpec((HP, S, HEAD_PAD), lambda hd, j: (hd, 0, 0)),
                  pl.BlockSpec((HP, KT * tk, HEAD_PAD), lambda hd, j: (hd, j, 0)),
                  pl.BlockSpec((HP, KT * tk, V_DIM), lambda hd, j: (hd, j, 0)),
                  pl.BlockSpec((S, HP * V_DIM), lambda hd, j: (0, hd)),
                  pl.BlockSpec((HP, nq, 1, tq), lambda hd, j: (hd, 0, 0, 0)),
                  pl.BlockSpec((HP, nq, 1, tq), lambda hd, j: (hd, 0, 0, 0)), tab, tab, tab],
        out_specs=(pair_rows, pair_rows, pl.BlockSpec((HP, KT * tk, LANES), lambda hd, j: (hd, j, 0))),
        scratch_shapes=[pltpu.VMEM((S, HP * HEAD_PAD), F32), pltpu.VMEM((HP, KT * tk, HEAD_PAD), F32), pltpu.VMEM((HP, KT * tk, V_DIM), F32)],
        compiler_params=_cparams("parallel", "arbitrary"),
    )(q_full, k_full, v_aug, do, lse_rows, delta_rows, *tabs)


def _mla_latent_bwd(h, dq_n, dkv_n, dkr, dz, q_norm, kv_norm, tabs, *, tm, name):
    S = h.shape[0]

    def body(kv_ref, q_ref, dqn_ref, dkvn_ref, dkr_ref, dz_ref, qg_ref, kvg_ref, c_ref, s1_ref, s2_ref, dh_ref, dqg_ref, dkvg_ref):
        i = pl.program_id(0)
        dq_lat, dqg = _rms_bwd_math(q_ref[...], qg_ref[...], dqn_ref[...])
        dkv_lat, dkvg = _rms_bwd_math(kv_ref[...], kvg_ref[...], dkvn_ref[...])
        dkr_sum = dkr_ref[0]
        for hd in range(1, N_HEADS):
            dkr_sum = dkr_sum + dkr_ref[hd]
        dh_ref[:, :D_INNER] = dz_ref[...]
        dh_ref[:, D_INNER:D_INNER + KV_LORA] = dkv_lat.astype(BF16)
        dh_ref[:, D_INNER + KV_LORA:D_INNER + KV_LORA + Q_LORA] = dq_lat.astype(BF16)
        dh_ref[:, D_INNER + KV_LORA + Q_LORA:] = _unrope(dkr_sum, c_ref[...], s1_ref[...], s2_ref[...]).astype(BF16)

        @pl.when(i == 0)
        def _():
            dqg_ref[...] = dqg
            dkvg_ref[...] = dkvg

        @pl.when(i > 0)
        def _():
            dqg_ref[...] += dqg
            dkvg_ref[...] += dkvg

    tab = pl.BlockSpec((tm, LANES), lambda i: (i, 0))
    qvec = pl.BlockSpec((1, Q_LORA), lambda i: (0, 0))
    kvvec = pl.BlockSpec((1, KV_LORA), lambda i: (0, 0))
    return pl.pallas_call(
        body, name=name,
        out_shape=(jax.ShapeDtypeStruct((S, MLA_IN_PAD), BF16), jax.ShapeDtypeStruct((1, Q_LORA), F32),
                   jax.ShapeDtypeStruct((1, KV_LORA), F32)),
        grid=(S // tm,),
        in_specs=[pl.BlockSpec((tm, KV_LORA), lambda i: (i, KV_LAT_BLK)), pl.BlockSpec((tm, Q_LORA), lambda i: (i, Q_LAT_BLK)),
                  pl.BlockSpec((tm, Q_LORA), lambda i: (i, 0)), pl.BlockSpec((tm, KV_LORA), lambda i: (i, 0)),
                  pl.BlockSpec((N_HEADS, tm, LANES), lambda i: (0, i, 0)), pl.BlockSpec((tm, D_INNER), lambda i: (i, 0)),
                  qvec, kvvec, tab, tab, tab],
        out_specs=(pl.BlockSpec((tm, MLA_IN_PAD), lambda i: (i, 0)), qvec, kvvec),
        compiler_params=_cparams("arbitrary"),
    )(h, h, dq_n, dkv_n, dkr, dz, q_norm, kv_norm, *tabs)


def _adamw(w, g, m, v, *, name):
    R, C = w.shape
    tr = R
    for cand in (512, 256, 128, 64, 32, 16, 8):
        if R % cand == 0 and cand * C * 4 <= 2 * 1024 * 1024:
            tr = cand
            break

    def body(w_ref, g_ref, m_ref, v_ref, d_ref, nm_ref, nv_ref):
        d_ref[...], nm_ref[...], nv_ref[...] = _adam_math(w_ref[...], g_ref[...], m_ref[...], v_ref[...])

    spec = pl.BlockSpec((tr, C), lambda i: (i, 0))
    sds = jax.ShapeDtypeStruct((R, C), F32)
    return pl.pallas_call(
        body, name=name, out_shape=(sds, sds, sds), grid=(R // tr,),
        in_specs=[spec] * 4, out_specs=(spec,) * 3,
        compiler_params=_cparams("parallel"),
    )(w, g, m, v)


def _adamw_rows(w, m, v, srcs, *, name):
    nj, R, C = w.shape
    assert len(srcs) == nj and C % PACK_COLS == 0
    tr = min(R, 512)
    assert R % tr == 0 and all(r0 % tr == 0 for _, r0 in srcs)

    def body(*refs):
        w_ref, m_ref, v_ref = refs[:3]
        g_refs = refs[3:3 + nj]
        go_ref, d_ref, nm_ref, nv_ref = refs[3 + nj:]
        gv = g_refs[0][...]
        for jj in range(1, nj):
            gv = jnp.where(pl.program_id(0) == jj, g_refs[jj][...], gv)
        d, m_new, v_new = _adam_math(w_ref[...], gv, m_ref[...], v_ref[...])
        go_ref[...] = gv
        d_ref[...] = d
        nm_ref[...] = m_new
        nv_ref[...] = v_new

    nat = pl.BlockSpec((None, tr, PACK_COLS), lambda j, cb, i: (j, i, cb))

    def src_spec(jj, r0):
        return pl.BlockSpec((tr, PACK_COLS), lambda j, cb, i: (jnp.where(j == jj, (r0 + cb * R) // tr + i, r0 // tr), 0))

    sds = jax.ShapeDtypeStruct((nj, R, C), F32)
    return pl.pallas_call(
        body, name=name, out_shape=(sds,) * 4, grid=(nj, C // PACK_COLS, R // tr),
        in_specs=[nat] * 3 + [src_spec(jj, r0) for jj, (_, r0) in enumerate(srcs)], out_specs=(nat,) * 4,
        compiler_params=_cparams("parallel", "parallel", "parallel"),
    )(w, m, v, *[rows for rows, _ in srcs])


HBM_SPEC = pl.BlockSpec(memory_space=pltpu.HBM)
VMEM_SPEC = pl.BlockSpec(memory_space=pltpu.VMEM)
SEM_SPEC = pl.BlockSpec(memory_space=pltpu.SEMAPHORE)
ANY_SPEC = pl.BlockSpec(memory_space=pl.ANY)
SPLIT_EFFECT = pltpu.SideEffectType.DATAFLOW_SIDE_EFFECTING


def _place():
    return lax.axis_index("x"), lax.axis_index("y"), lax.axis_index("c")


def _other_chips(x, y):
    return [(1 - x, y), (x, 1 - y), (1 - x, 1 - y)]


def _remote(src, dst, send_sem, recv_sem, dev):
    return pltpu.make_async_remote_copy(src_ref=src, dst_ref=dst, send_sem=send_sem, recv_sem=recv_sem,
                                        device_id=dev, device_id_type=MESH)


def _ag_ici_start(wp, after, *, name):
    R, C = wp.shape
    H = R // 2

    def body(w_ref, land_ref, after_ref, send_sems, recv_sems, w_thru, land_thru, token):
        x, y, c = _place()
        rows = pl.ds(pl.multiple_of(c * H, 16), H)
        for r, chip in enumerate(_other_chips(x, y)):
            _remote(w_ref.at[rows, :], land_ref.at[2 * x + y, rows, :], send_sems.at[r], recv_sems.at[r], (*chip, c)).start()
        token[...] = jnp.zeros(token.shape, F32)

    land = lax.empty((N_CHIPS, R, C), wp.dtype)
    return pl.pallas_call(
        body, name=name,
        out_shape=(pltpu.SemaphoreType.DMA((3,)), pltpu.SemaphoreType.DMA((3,)), pltpu.HBM(wp.shape, wp.dtype), pltpu.HBM(land.shape, land.dtype),
                   jax.ShapeDtypeStruct((8, LANES), F32)),
        in_specs=(HBM_SPEC, HBM_SPEC, ANY_SPEC), out_specs=(SEM_SPEC, SEM_SPEC, HBM_SPEC, HBM_SPEC, VMEM_SPEC),
        input_output_aliases={0: 2, 1: 3},
        compiler_params=pltpu.CompilerParams(has_side_effects=SPLIT_EFFECT),
    )(pltpu.with_memory_space_constraint(wp, pltpu.HBM), pltpu.with_memory_space_constraint(land, pltpu.HBM), after)


def _ag_ici_wait(send_sems, recv_sems, w_thru, land_thru, after, *, name):
    R, C = w_thru.shape
    H = R // 2
    after = after if isinstance(after, (tuple, list)) else (after,)

    def body(w_ref, land_ref, send_sems, recv_sems, *rest):
        x, y, c = _place()
        rows = pl.ds(pl.multiple_of(c * H, 16), H)
        for r, (px, py) in enumerate(_other_chips(x, y)):
            cp = _remote(w_ref.at[rows, :], land_ref.at[2 * px + py, rows, :], send_sems.at[r], recv_sems.at[r], (px, py, c))
            cp.wait_send()
            cp.wait_recv()

    return pl.pallas_call(
        body, name=name,
        out_shape=(pltpu.HBM(w_thru.shape, w_thru.dtype), pltpu.HBM(land_thru.shape, land_thru.dtype)),
        in_specs=(HBM_SPEC, HBM_SPEC, SEM_SPEC, SEM_SPEC) + (ANY_SPEC,) * len(after), out_specs=(HBM_SPEC, HBM_SPEC),
        input_output_aliases={0: 0, 1: 1},
        compiler_params=pltpu.CompilerParams(has_side_effects=SPLIT_EFFECT),
    )(w_thru, land_thru, send_sems, recv_sems, *after)


def _ag_forward_start(land, wp, *, name):
    _, R, C = land.shape
    H = R // 2

    def body(land_ref, w_ref, send_sems, recv_sems, land_thru, w_thru):
        x, y, c = _place()
        sib = (x, y, 1 - c)
        mine = pl.ds(pl.multiple_of(c * H, 16), H)
        for r, (px, py) in enumerate(_other_chips(x, y)):
            _remote(land_ref.at[2 * px + py, mine, :], land_ref.at[2 * px + py, mine, :], send_sems.at[r], recv_sems.at[r], sib).start()
        _remote(w_ref, land_ref.at[2 * x + y], send_sems.at[3], recv_sems.at[3], sib).start()

    return pl.pallas_call(
        body, name=name,
        out_shape=(pltpu.SemaphoreType.DMA((4,)), pltpu.SemaphoreType.DMA((4,)), pltpu.HBM(land.shape, land.dtype), pltpu.HBM(wp.shape, wp.dtype)),
        in_specs=(HBM_SPEC, HBM_SPEC), out_specs=(SEM_SPEC, SEM_SPEC, HBM_SPEC, HBM_SPEC),
        input_output_aliases={0: 2, 1: 3},
        compiler_params=pltpu.CompilerParams(has_side_effects=SPLIT_EFFECT),
    )(land, wp)


def _ag_forward_wait(send_sems, recv_sems, land_thru, w_thru, after, *, name):
    _, R, C = land_thru.shape
    H = R // 2

    def body(land_ref, w_ref, send_sems, recv_sems, after_ref, land_out, w_out):
        x, y, c = _place()
        sib = (x, y, 1 - c)
        mine = pl.ds(pl.multiple_of(c * H, 16), H)
        theirs = pl.ds(pl.multiple_of((1 - c) * H, 16), H)
        for r, (px, py) in enumerate(_other_chips(x, y)):
            cp = _remote(land_ref.at[2 * px + py, mine, :], land_ref.at[2 * px + py, theirs, :], send_sems.at[r], recv_sems.at[r], sib)
            cp.wait_send()
            cp.wait_recv()
        own = _remote(w_ref, land_ref.at[2 * x + y], send_sems.at[3], recv_sems.at[3], sib)
        own.wait_send()
        own.wait_recv()

    return pl.pallas_call(
        body, name=name,
        out_shape=(pltpu.HBM(land_thru.shape, land_thru.dtype), pltpu.HBM(w_thru.shape, w_thru.dtype)),
        in_specs=(HBM_SPEC, HBM_SPEC, SEM_SPEC, SEM_SPEC, ANY_SPEC), out_specs=(HBM_SPEC, HBM_SPEC),
        input_output_aliases={0: 0, 1: 1},
        compiler_params=pltpu.CompilerParams(has_side_effects=SPLIT_EFFECT),
    )(land_thru, w_thru, send_sems, recv_sems, after)[0]


def _rs_sibling_swap(g, *, name):
    _, R, C = g.shape
    H = R // 2

    def body(g_ref, theirs_ref, send_sems, recv_sems):
        x, y, c = _place()
        sib = (x, y, 1 - c)
        copies = [_remote(g_ref.at[k, pl.ds(pl.multiple_of((1 - c) * H, 16), H), :], theirs_ref.at[k],
                          send_sems.at[k], recv_sems.at[k], sib) for k in range(N_CHIPS)]
        for cp in copies:
            cp.start()
        for cp in copies:
            cp.wait()

    return pl.pallas_call(
        body, name=name, out_shape=jax.ShapeDtypeStruct((N_CHIPS, H, C), g.dtype),
        in_specs=[HBM_SPEC], out_specs=HBM_SPEC,
        scratch_shapes=[pltpu.SemaphoreType.DMA((N_CHIPS,)), pltpu.SemaphoreType.DMA((N_CHIPS,))],
    )(g)


def _row_tile(h):
    best = 16
    for d in range(16, 1025, 16):
        if h % d == 0:
            best = d
    return best


def _add2_bf16(g, theirs, core, *, name):
    K, H, C = theirs.shape
    tr = _row_tile(H)
    nb = H // tr

    def body(c_ref, a_ref, b_ref, o_ref):
        o_ref[...] = (a_ref[...].astype(F32) + b_ref[...].astype(F32)).astype(o_ref.dtype)

    spec = pl.BlockSpec((None, tr, C), lambda k, i, c: (k, i, 0))
    return pl.pallas_call(
        body, name=name, out_shape=jax.ShapeDtypeStruct((K, H, C), theirs.dtype),
        grid_spec=pltpu.PrefetchScalarGridSpec(
            num_scalar_prefetch=1, grid=(K, nb),
            in_specs=[pl.BlockSpec((None, tr, C), lambda k, i, c: (k, c[0] * nb + i, 0)), spec], out_specs=spec),
        compiler_params=_cparams("parallel", "parallel"),
    )(core, g, theirs)


def _rs_chip_exchange_start(p, *, name):
    _, H, C = p.shape

    def body(p_ref, land_ref, send_sems, recv_sems, p_thru, land_thru, token):
        x, y, c = _place()
        for r, (px, py) in enumerate(_other_chips(x, y)):
            _remote(p_ref.at[2 * px + py], land_ref.at[r], send_sems.at[r], recv_sems.at[r], (px, py, c)).start()
        token[...] = jnp.zeros(token.shape, F32)

    land = lax.empty((3, H, C), p.dtype)
    return pl.pallas_call(
        body, name=name,
        out_shape=(pltpu.SemaphoreType.DMA((3,)), pltpu.SemaphoreType.DMA((3,)), pltpu.HBM(p.shape, p.dtype), pltpu.HBM(land.shape, land.dtype),
                   jax.ShapeDtypeStruct((8, LANES), F32)),
        in_specs=(HBM_SPEC, HBM_SPEC), out_specs=(SEM_SPEC, SEM_SPEC, HBM_SPEC, HBM_SPEC, VMEM_SPEC),
        input_output_aliases={0: 2, 1: 3},
        compiler_params=pltpu.CompilerParams(has_side_effects=SPLIT_EFFECT),
    )(pltpu.with_memory_space_constraint(p, pltpu.HBM), pltpu.with_memory_space_constraint(land, pltpu.HBM))


def _rs_chip_exchange_wait(send_sems, recv_sems, p_thru, land_thru, after, *, name):
    after = after if isinstance(after, (tuple, list)) else (after,)

    def body(p_ref, land_ref, send_sems, recv_sems, *rest):
        x, y, c = _place()
        for r, (px, py) in enumerate(_other_chips(x, y)):
            cp = _remote(p_ref.at[2 * px + py], land_ref.at[r], send_sems.at[r], recv_sems.at[r], (px, py, c))
            cp.wait_send()
            cp.wait_recv()

    return pl.pallas_call(
        body, name=name,
        out_shape=(pltpu.HBM(p_thru.shape, p_thru.dtype), pltpu.HBM(land_thru.shape, land_thru.dtype)),
        in_specs=(HBM_SPEC, HBM_SPEC, SEM_SPEC, SEM_SPEC) + (ANY_SPEC,) * len(after), out_specs=(HBM_SPEC, HBM_SPEC),
        input_output_aliases={0: 0, 1: 1},
        compiler_params=pltpu.CompilerParams(has_side_effects=SPLIT_EFFECT),
    )(p_thru, land_thru, send_sems, recv_sems, *after)


def _add4_f32(p, recv, chip_core, *, name):
    _, H, C = p.shape
    tr = _row_tile(H)
    nb = H // tr

    def body(s_ref, o_ref, r_ref, out_ref):
        out_ref[...] = ((o_ref[...].astype(F32) + r_ref[0].astype(F32)) + r_ref[1].astype(F32)) + r_ref[2].astype(F32)

    return pl.pallas_call(
        body, name=name, out_shape=jax.ShapeDtypeStruct((2 * H, C), F32),
        grid_spec=pltpu.PrefetchScalarGridSpec(
            num_scalar_prefetch=1, grid=(nb,),
            in_specs=[pl.BlockSpec((None, tr, C), lambda i, s: (s[0], i, 0)), pl.BlockSpec((3, tr, C), lambda i, s: (0, i, 0))],
            out_specs=pl.BlockSpec((tr, C), lambda i, s: (s[1] * nb + i, 0))),
        compiler_params=_cparams("parallel"),
    )(chip_core, p, recv)


def _rs_sibling_join(f, *, name):
    R, C = f.shape
    H = R // 2

    def body(f_ref, out_ref, send_sem, recv_sem):
        x, y, c = _place()
        sib = (x, y, 1 - c)
        mine = pl.ds(pl.multiple_of(c * H, 8), H)
        theirs = pl.ds(pl.multiple_of((1 - c) * H, 8), H)
        cp = _remote(f_ref.at[mine, :], out_ref.at[mine, :], send_sem, recv_sem, sib)
        cp.start()
        _remote(f_ref.at[mine, :], out_ref.at[theirs, :], send_sem, recv_sem, sib).wait_recv()
        cp.wait_send()

    return pl.pallas_call(
        body, name=name, out_shape=jax.ShapeDtypeStruct((R, C), f.dtype),
        in_specs=[HBM_SPEC], out_specs=HBM_SPEC, input_output_aliases={0: 0},
        scratch_shapes=[pltpu.SemaphoreType.DMA, pltpu.SemaphoreType.DMA],
    )(f)


SMALL_GATHER = (("pool_norm", 2, 256), ("pool_scale", 2, 512), ("conv_w", 3, 512), ("mla_norm", 1, 256),
                ("mla_q_norm", 1, 96), ("mla_kv_norm", 1, 64))
SMALL_SLOT = (16, 512)


def _gather_small(shards, *, name):
    def body(pn_ref, ps_ref, cw_ref, mn_ref, qn_ref, kn_ref, pn_o, ps_o, cw_o, mn_o, qn_o, kn_o, all_ref, send_sems, recv_sems):
        x, y, c = _place()
        mine = 2 * x + y
        all_ref[mine] = jnp.zeros(SMALL_SLOT, F32)
        all_ref[mine, 0:2, 0:256] = pn_ref[...]
        all_ref[mine, 2:4, :] = ps_ref[...]
        all_ref[mine, 4:7, :] = cw_ref[0]
        all_ref[mine, 7:8, 0:256] = mn_ref[...]
        all_ref[mine, 8:9, 0:96] = qn_ref[...]
        all_ref[mine, 9:10, 0:64] = kn_ref[...]
        chips = _other_chips(x, y)
        sends = [_remote(all_ref.at[mine], all_ref.at[mine], send_sems.at[r], recv_sems.at[r], (*chip, c)) for r, chip in enumerate(chips)]
        for cp in sends:
            cp.start()
        for r, (px, py) in enumerate(chips):
            _remote(all_ref.at[mine], all_ref.at[2 * px + py], send_sems.at[r], recv_sems.at[r], (px, py, c)).wait_recv()
        for cp in sends:
            cp.wait_send()
        for k in range(N_CHIPS):
            pn_o[:, k * 256:(k + 1) * 256] = all_ref[k, 0:2, 0:256]
            ps_o[:, k * 512:(k + 1) * 512] = all_ref[k, 2:4, :]
            cw_o[0, :, k * 512:(k + 1) * 512] = all_ref[k, 4:7, :]
            mn_o[:, k * 256:(k + 1) * 256] = all_ref[k, 7:8, 0:256]
            qn_o[k] = all_ref[k, 8:9, 0:96]
            kn_o[k] = all_ref[k, 9:10, 0:64]

    sds = lambda *shape: jax.ShapeDtypeStruct(shape, F32)
    out = pl.pallas_call(
        body, name=name,
        out_shape=(sds(2, 1024), sds(2, 2048), sds(1, 3, 2048), sds(1, 1024), sds(N_CHIPS, 1, 96), sds(N_CHIPS, 1, 64)),
        in_specs=[VMEM_SPEC] * 6, out_specs=(VMEM_SPEC,) * 6,
        scratch_shapes=[pltpu.VMEM((N_CHIPS,) + SMALL_SLOT, F32), pltpu.SemaphoreType.DMA((3,)), pltpu.SemaphoreType.DMA((3,))],
    )(*[shards[n] for n, _, _ in SMALL_GATHER])
    full = dict(zip([n for n, _, _ in SMALL_GATHER], out))
    full["mla_q_norm"] = full["mla_q_norm"].reshape(1, Q_LORA)
    full["mla_kv_norm"] = full["mla_kv_norm"].reshape(1, KV_LORA)
    return full


SMALL_REDUCE = (("pool_norm_0", 0, 1, 1024), ("pool_norm_1", 1, 1, 1024), ("pool_scale_0", 2, 1, 2048), ("pool_scale_1", 3, 1, 2048),
                ("conv_norm", 4, 1, 1024), ("mla_norm", 5, 1, 1024), ("mla_q_norm", 6, 1, 384), ("mla_kv_norm", 7, 1, 256),
                ("conv_w", 8, 8, 2048), ("final_norm", 16, 1, 1024))
REDUCE_SLOT = (24, 2048)


def _reduce_small(parts, after, *, name):
    keys = [k for k, _, _, _ in SMALL_REDUCE]

    def body(*refs):
        ins = dict(zip(keys, refs[:len(keys)]))
        pn_o, ps_o, cn_o, cw_o, mn_o, qn_o, kn_o, fn_o, all_ref, send_sems, recv_sems = refs[len(keys) + 1:]
        x, y, c = _place()
        me = 4 * x + 2 * y + c
        all_ref[me] = jnp.zeros(REDUCE_SLOT, F32)
        for k, r0, nr, wd in SMALL_REDUCE:
            all_ref[me, r0:r0 + nr, 0:wd] = ins[k][...]
        peers = []
        for rel in range(1, N_DEV):
            dx, dy, dc = (rel >> 2) & 1, (rel >> 1) & 1, rel & 1
            peers.append((1 - x if dx else x, 1 - y if dy else y, 1 - c if dc else c))
        sends = [_remote(all_ref.at[me], all_ref.at[me], send_sems.at[k], recv_sems.at[k], peer) for k, peer in enumerate(peers)]
        for cp in sends:
            cp.start()
        for k, (px, py, pc) in enumerate(peers):
            _remote(all_ref.at[me], all_ref.at[4 * px + 2 * py + pc], send_sems.at[k], recv_sems.at[k], (px, py, pc)).wait_recv()
        for cp in sends:
            cp.wait_send()

        def total(r0, nr, wd):
            acc = all_ref[0, r0:r0 + nr, 0:wd]
            for d in range(1, N_DEV):
                acc = acc + all_ref[d, r0:r0 + nr, 0:wd]
            return acc

        pn_o[0:1, :] = total(0, 1, 1024)
        pn_o[1:2, :] = total(1, 1, 1024)
        ps_o[0:1, :] = total(2, 1, 2048)
        ps_o[1:2, :] = total(3, 1, 2048)
        cn_o[...] = total(4, 1, 1024)
        mn_o[...] = total(5, 1, 1024)
        qn_o[...] = total(6, 1, Q_LORA)
        kn_o[...] = total(7, 1, KV_LORA)
        cw_o[0] = total(8, 3, 2048)
        fn_o[...] = total(16, 1, 1024)

    sds = lambda *shape: jax.ShapeDtypeStruct(shape, F32)
    out = pl.pallas_call(
        body, name=name,
        out_shape=(sds(2, 1024), sds(2, 2048), sds(1, 1024), sds(1, 3, 2048), sds(1, 1024), sds(1, Q_LORA), sds(1, KV_LORA), sds(1, 1024)),
        in_specs=[VMEM_SPEC] * len(keys) + [ANY_SPEC], out_specs=(VMEM_SPEC,) * 8,
        scratch_shapes=[pltpu.VMEM((N_DEV,) + REDUCE_SLOT, F32), pltpu.SemaphoreType.DMA((N_DEV - 1,)), pltpu.SemaphoreType.DMA((N_DEV - 1,))],
    )(*[parts[k] for k in keys], after)
    return dict(zip(("pool_norm", "pool_scale", "conv_norm", "conv_w", "mla_norm", "mla_q_norm", "mla_kv_norm", "final_norm"), out))


def _adam_math(w, g, m, v):
    m_new = ADAM_B1 * m + (1.0 - ADAM_B1) * g
    v_new = ADAM_B2 * v + (1.0 - ADAM_B2) * (g * g)
    m_hat = m_new / (1.0 - ADAM_B1 ** ADAM_STEP)
    v_hat = v_new / (1.0 - ADAM_B2 ** ADAM_STEP)
    return -ADAM_LR * (m_hat / (jnp.sqrt(v_hat) + ADAM_EPS) + ADAM_WD * w), m_new, v_new


def _adamw_small(w, m, v, g_full, chip, *, name):
    shp = {n: w[n].shape for n in SMALL}
    whole = lambda s: pl.BlockSpec(s, lambda i, c: (0,) * len(s))
    g_in, g_specs = {}, {}
    for n in SMALL:
        if not SMALL_SHARDED[n]:
            g_in[n], g_specs[n] = g_full[n].reshape(shp[n]), whole(shp[n])
        elif shp[n][-1] % LANES:
            g_in[n] = g_full[n].reshape(N_CHIPS, 1, shp[n][-1])
            g_specs[n] = pl.BlockSpec((None,) + shp[n], lambda i, c: (c[0], 0, 0))
        else:
            g_in[n] = g_full[n]
            nd = len(shp[n])
            g_specs[n] = pl.BlockSpec(shp[n], lambda i, c, nd=nd: (0,) * (nd - 1) + (c[0],))

    def body(c_ref, *refs):
        k = len(SMALL)
        w_r, m_r, v_r, g_r = refs[0:k], refs[k:2 * k], refs[2 * k:3 * k], refs[3 * k:4 * k]
        go_r, d_r, nm_r, nv_r = refs[4 * k:5 * k], refs[5 * k:6 * k], refs[6 * k:7 * k], refs[7 * k:8 * k]
        for i in range(k):
            gv = g_r[i][...]
            d, m_new, v_new = _adam_math(w_r[i][...], gv, m_r[i][...], v_r[i][...])
            go_r[i][...] = gv
            d_r[i][...] = d
            nm_r[i][...] = m_new
            nv_r[i][...] = v_new

    nat = [whole(shp[n]) for n in SMALL]
    out_sds = tuple(jax.ShapeDtypeStruct(shp[n], F32) for n in SMALL)
    out = pl.pallas_call(
        body, name=name, out_shape=out_sds * 4,
        grid_spec=pltpu.PrefetchScalarGridSpec(
            num_scalar_prefetch=1, grid=(1,),
            in_specs=nat * 3 + [g_specs[n] for n in SMALL], out_specs=tuple(nat) * 4),
        compiler_params=_cparams("arbitrary"),
    )(chip, *[w[n] for n in SMALL], *[m[n] for n in SMALL], *[v[n] for n in SMALL], *[g_in[n] for n in SMALL])
    k = len(SMALL)
    return tuple(dict(zip(SMALL, out[j * k:(j + 1) * k])) for j in range(4))


BIG = ("pool_w_in", "pool_w_grp", "pool_w_out", "conv_w_in", "conv_w_out", "mla_w_in", "mla_w_q_up", "mla_w_kv_up", "mla_w_out")
BIG_SHARD_AXIS = {"pool_w_in": 2, "pool_w_grp": 2, "pool_w_out": 1, "conv_w_in": 2, "conv_w_out": 1,
                  "mla_w_in": 2, "mla_w_q_up": 2, "mla_w_kv_up": 2, "mla_w_out": 1}
GATHER_LAYOUT = {
    "p0": ((("pool_w_in", 0), 0, "cols"), (("pool_w_out", 0), 1024, "rows"), (("pool_w_grp", 0), 1536, "flat")),
    "cv": ((("conv_w_in", 0), 0, "cols"), (("conv_w_out", 0), 2048, "rows")),
    "ml": ((("pool_w_in", 1), 0, "cols"), (("mla_w_out", 0), 1024, "rows"), (("pool_w_out", 1), 1536, "rows"),
           (("mla_w_kv_up", 0), 2048, "cols"), (("pool_w_grp", 1), 2304, "flat"), (("mla_w_q_up", 0), 2560, "flat"),
           (("mla_w_in", 0), 2848, "flat")),
}
REDUCE_LAYOUT = {
    "late": ((("conv_w_in", 0), 0, "cols"), (("pool_w_in", 1), 2048, "cols"), (("conv_w_out", 0), 3072, "rows"),
             (("mla_w_out", 0), 3584, "rows"), (("pool_w_out", 1), 4096, "rows"), (("mla_w_kv_up", 0), 4608, "cols"),
             (("pool_w_grp", 1), 4864, "flat"), (("mla_w_q_up", 0), 5120, "flat"), (("mla_w_in", 0), 5408, "flat")),
    "first": ((("pool_w_in", 0), 0, "cols"), (("pool_w_out", 0), 1024, "rows"), (("pool_w_grp", 0), 1536, "flat")),
}
PACK_ROW_ALIGN = 32
RS_ROW_ALIGN = 512


def _slot_rows(layout, shard_shape, align):
    where, end = {}, 0
    for piece, r0, kind in layout:
        n = 1
        for d in shard_shape(piece):
            n *= d
        assert r0 >= end and n % PACK_COLS == 0, (piece, r0, end)
        where[piece] = (r0, n // PACK_COLS, kind)
        end = r0 + n // PACK_COLS
    return end + (-end) % align, where


def _as_slot_rows(shard, kind):
    if kind == "cols":
        k, n = shard.shape
        return shard.reshape(k, n // PACK_COLS, PACK_COLS).swapaxes(0, 1).reshape(-1, PACK_COLS)
    return shard.reshape(-1, PACK_COLS)


def _pack_slot(shards, layout, rows, dtype):
    parts, end = [], 0
    for piece, r0, kind in layout:
        if r0 > end:
            parts.append(jnp.zeros((r0 - end, PACK_COLS), dtype))
        parts.append(_as_slot_rows(shards[piece], kind).astype(dtype))
        end = r0 + parts[-1].shape[0]
    if rows > end:
        parts.append(jnp.zeros((rows - end, PACK_COLS), dtype))
    return jnp.concatenate(parts, axis=0)


SMALL = ("pool_norm", "pool_scale", "conv_norm", "conv_w", "mla_norm", "mla_q_norm", "mla_kv_norm", "final_norm")
SMALL_SHARDED = {"pool_norm": True, "pool_scale": True, "conv_norm": False, "conv_w": True, "mla_norm": True,
                 "mla_q_norm": True, "mla_kv_norm": True, "final_norm": False}


def _rope_tables(positions):
    inv_freq = ROPE_BASE ** (-jnp.arange(0, QK_ROPE, 2, dtype=F32) / QK_ROPE)
    ang = positions.astype(F32).reshape(-1, 1) * inv_freq
    cos, sin = jnp.cos(ang), jnp.sin(ang)
    z32 = jnp.zeros_like(cos)
    z64 = jnp.concatenate([z32, z32], axis=1)
    return (jnp.concatenate([cos, cos, z64], axis=1), jnp.concatenate([-sin, z32, z64], axis=1),
            jnp.concatenate([z32, sin, z64], axis=1))


def _mla_in_to_padded(w):
    q, kv, kr, z = w[:, :Q_LORA], w[:, Q_LORA:Q_LORA + KV_LORA], w[:, Q_LORA + KV_LORA:Q_LORA + KV_LORA + QK_ROPE], w[:, Q_LORA + KV_LORA + QK_ROPE:]
    return jnp.concatenate([z, kv, q, kr, jnp.zeros((w.shape[0], MLA_IN_PAD - MLA_IN), w.dtype)], axis=1)


def _mla_in_from_padded(w):
    z, kv, q, kr = w[:, :D_INNER], w[:, D_INNER:D_INNER + KV_LORA], w[:, D_INNER + KV_LORA:D_INNER + KV_LORA + Q_LORA], w[:, D_INNER + KV_LORA + Q_LORA:D_INNER + KV_LORA + Q_LORA + QK_ROPE]
    return jnp.concatenate([q, kv, kr, z], axis=1)


def _q_up_to_padded(w):
    k = w.shape[0]
    return jnp.pad(w.reshape(k, N_HEADS, QK_NOPE + QK_ROPE), ((0, 0), (0, 0), (0, HEAD_PAD - QK_NOPE - QK_ROPE))).reshape(k, N_HEADS * HEAD_PAD)


def _q_up_from_padded(w):
    k = w.shape[0]
    return w.reshape(k, N_HEADS, HEAD_PAD)[:, :, :QK_NOPE + QK_ROPE].reshape(k, N_HEADS * (QK_NOPE + QK_ROPE))


def _local_step(x, positions, target, weights_for, ws, sink):
    S = x.shape[0]
    tm = min(512, S)
    te = min(256, S)
    tq = min(512, S)
    tabs = _rope_tables(positions)
    gs = {}

    def mm_in(xn, w, name):
        n = w.shape[1]
        tn = PACK_COLS if isinstance(w, Packed) else _pick(n, 1536 if n == MLA_IN_PAD else 1024)
        return _mm(xn, w, tm=min(1024, S), tn=tn, tk=D_MODEL, name=name)

    def mm_out(y, w, res, name):
        return _mm(y, w, residual=res, tm=tm, tn=D_MODEL, tk=D_INNER, name=name)

    def mm_dx(dy, w, name, after=None):
        k, n = w.shape
        if isinstance(w, Packed):
            tn, tk = (k if w.kind == "rows" else min(k, 1024)), PACK_COLS
        else:
            tn, tk = _pick(k, 1024), _pick(n, 1408)
        return _mm(dy, w, trans_b=True, after=after, tm=min(1024, S), tn=tn, tk=tk, name=name)

    def mm_dw(piece, a, b, name, after=None, post=None):
        ka, nb = a.shape[1], b.shape[1]
        into = sink.dest(piece)
        tokens = min(1024, S)
        if into is None:
            out = _mm(a, b, trans_a=True, out_dtype=BF16, after=after, tm=_pick(ka, 1024), tn=_pick(nb, 1408), tk=tokens, name=name)
            sink.put(piece, out if post is None else post(out))
        else:
            rows = ka if into.kind == "rows" else min(ka, 1024)
            sink.put(piece, _mm(a, b, trans_a=True, after=after, into=into, tm=rows, tn=PACK_COLS, tk=tokens, name=name))

    def pool_layer_fwd(xin, xn, wts, j, tag, then=None):
        h = mm_in(xn, wts[("pool_w_in", j)], f"{tag}_in")
        y = _pool_fwd(h, wts[("pool_w_grp", j)], ws["pool_scale"][j:j + 1], tm=te, name=f"{tag}_mix")
        if then is not None:
            then(y)
        xo = mm_out(y, wts[("pool_w_out", j)], xin, f"{tag}_out")
        return xo, (xin, xn, h, y)

    def pool_layer_bwd(dx, dxb, saved, wts, j, tag, after=None):
        xin, xn, h, y = saved
        dy = mm_dx(dxb, wts[("pool_w_out", j)], f"{tag}_dy", after)
        mm_dw(("pool_w_out", j), y, dxb, f"{tag}_dwo", after)
        pooled, dmixed, dpooled, dz, dsc = _pool_bwd1(h, dy, wts[("pool_w_grp", j)], ws["pool_scale"][j:j + 1], tm=te, name=f"{tag}_bmix")
        sink.put(("pool_w_grp", j), _grouped_tn(pooled, dmixed, tk=tm, name=f"{tag}_dwg"))
        dh = _pool_bwd2(dpooled, dz, tm=te, name=f"{tag}_bshift")
        dxn = mm_dx(dh, wts[("pool_w_in", j)], f"{tag}_dxn")
        mm_dw(("pool_w_in", j), xn, dh, f"{tag}_dwi")
        dxo, dxob, dg = _rms_bwd(xin, ws["pool_norm"][j:j + 1], dxn, dx, tm=tm, name=f"{tag}_bnorm")
        gs[f"pool_norm_{j}"], gs[f"pool_scale_{j}"] = dg, dsc
        return dxo, dxob

    xn0 = _rms_fwd(x, ws["pool_norm"][0:1], tm=tm, name="p0_norm")
    w_p0 = weights_for("p0", xn0)
    x1, sv0 = pool_layer_fwd(x, xn0, w_p0, 0, "p0", then=lambda y: weights_for.prefetch("cv", y))

    xn1 = _rms_fwd(x1, ws["conv_norm"][0:1], tm=tm, name="cv_norm")
    w_cv = weights_for("cv", xn1)
    h1 = mm_in(xn1, w_cv[("conv_w_in", 0)], "cv_in")
    weights_for.prefetch("ml", h1)
    cw = jnp.pad(ws["conv_w"][0], ((0, 5), (0, 0)))
    y1 = _conv_fwd(h1, cw, tm=te, name="cv_mix")
    x2 = mm_out(y1, w_cv[("conv_w_out", 0)], x1, "cv_out")

    xn2 = _rms_fwd(x2, ws["mla_norm"][0:1], tm=tm, name="ml_norm")
    w_ml = weights_for("ml", xn2)
    w_mi = _mla_in_to_padded(w_ml[("mla_w_in", 0)])
    w_q = _q_up_to_padded(w_ml[("mla_w_q_up", 0)])
    w_kv = w_ml[("mla_w_kv_up", 0)]
    qg, kvg = ws["mla_q_norm"][0:1], ws["mla_kv_norm"][0:1]
    h2 = mm_in(xn2, w_mi, "ml_in")
    q_n, kv_n, krr = _mla_latent_fwd(h2, qg, kvg, tabs, tm=tm, name="ml_lat")
    q_full = _mla_q_up(q_n, w_q, tabs, tm=tm, name="ml_qup")
    k_full, v = _mla_kv_up(kv_n, w_kv, krr, tm=tm, name="ml_kvup")
    o, y2, lse = _flash_fwd(q_full, k_full, v, h2, tq=tq, name="ml_attn")
    x3 = mm_out(y2, w_ml[("mla_w_out", 0)], x2, "ml_out")

    x4, sv3 = pool_layer_fwd(x3, _rms_fwd(x3, ws["pool_norm"][1:2], tm=tm, name="p1_norm"), w_ml, 1, "p1")

    loss_part, dx, dxb, dgf = _final_loss(x4, ws["final_norm"].reshape(1, -1), target, tm=tm, name="final")
    gs["final_norm"] = dgf

    dx, dxb = pool_layer_bwd(dx, dxb, sv3, w_ml, 1, "p1")

    dy = mm_dx(dxb, w_ml[("mla_w_out", 0)], "ml_dy")
    mm_dw(("mla_w_out", 0), y2, dxb, "ml_dwo")
    do, dz, delta = _mla_gate_bwd(dy, o, h2, tm=tq, name="ml_bgate")
    dq_pre, dkv, dkr = _flash_bwd(q_full, k_full, v, do, lse, delta, tabs, tq=tq, name="ml_battn")
    dq_n = mm_dx(dq_pre, w_q, "ml_dqn")
    mm_dw(("mla_w_q_up", 0), q_n, dq_pre, "ml_dwq", post=_q_up_from_padded)
    dkv_n = mm_dx(dkv, w_kv, "ml_dkvn")
    mm_dw(("mla_w_kv_up", 0), kv_n, dkv, "ml_dwkv")
    dh2, dqg, dkvg = _mla_latent_bwd(h2, dq_n, dkv_n, dkr, dz, qg, kvg, tabs, tm=te, name="ml_blat")
    dxn2 = mm_dx(dh2, w_mi, "ml_dxn")
    mm_dw(("mla_w_in", 0), xn2, dh2, "ml_dwi", post=_mla_in_from_padded)
    dx, dxb, dg2 = _rms_bwd(x2, ws["mla_norm"][0:1], dxn2, dx, tm=tm, name="ml_bnorm")
    gs["mla_norm"], gs["mla_q_norm"], gs["mla_kv_norm"] = dg2, dqg, dkvg

    dy = mm_dx(dxb, w_cv[("conv_w_out", 0)], "cv_dy")
    mm_dw(("conv_w_out", 0), y1, dxb, "cv_dwo")
    dh1, dcw = _conv_bwd(h1, dy, cw, tm=te, name="cv_bmix")
    dxn1 = mm_dx(dh1, w_cv[("conv_w_in", 0)], "cv_dxn")
    mm_dw(("conv_w_in", 0), xn1, dh1, "cv_dwi")
    dx, dxb, dg1 = _rms_bwd(x1, ws["conv_norm"][0:1], dxn1, dx, tm=tm, name="cv_bnorm")
    gs["conv_norm"], gs["conv_w"] = dg1, dcw

    dx, dxb = pool_layer_bwd(dx, dxb, sv0, w_p0, 0, "p0", after=sink.late_ready())
    return loss_part, dx, gs


def kernel(x, positions, pool_norm, pool_w_in, pool_w_grp, pool_scale, pool_w_out, conv_norm, conv_w_in, conv_w, conv_w_out, mla_norm, mla_w_in, mla_q_norm, mla_w_q_up, mla_kv_norm, mla_w_kv_up, mla_w_out, final_norm, loss_target, m_pool_norm, m_pool_w_in, m_pool_w_grp, m_pool_scale, m_pool_w_out, m_conv_norm, m_conv_w_in, m_conv_w, m_conv_w_out, m_mla_norm, m_mla_w_in, m_mla_q_norm, m_mla_w_q_up, m_mla_kv_norm, m_mla_w_kv_up, m_mla_w_out, m_final_norm, v_pool_norm, v_pool_w_in, v_pool_w_grp, v_pool_scale, v_pool_w_out, v_conv_norm, v_conv_w_in, v_conv_w, v_conv_w_out, v_mla_norm, v_mla_w_in, v_mla_q_norm, v_mla_w_q_up, v_mla_kv_norm, v_mla_w_kv_up, v_mla_w_out, v_final_norm):
    names = ("pool_norm", "pool_w_in", "pool_w_grp", "pool_scale", "pool_w_out", "conv_norm", "conv_w_in", "conv_w", "conv_w_out",
             "mla_norm", "mla_w_in", "mla_q_norm", "mla_w_q_up", "mla_kv_norm", "mla_w_kv_up", "mla_w_out", "final_norm")
    w = dict(zip(names, (pool_norm, pool_w_in, pool_w_grp, pool_scale, pool_w_out, conv_norm, conv_w_in, conv_w, conv_w_out,
                         mla_norm, mla_w_in, mla_q_norm, mla_w_q_up, mla_kv_norm, mla_w_kv_up, mla_w_out, final_norm)))
    m = dict(zip(names, (m_pool_norm, m_pool_w_in, m_pool_w_grp, m_pool_scale, m_pool_w_out, m_conv_norm, m_conv_w_in, m_conv_w, m_conv_w_out,
                         m_mla_norm, m_mla_w_in, m_mla_q_norm, m_mla_w_q_up, m_mla_kv_norm, m_mla_w_kv_up, m_mla_w_out, m_final_norm)))
    v = dict(zip(names, (v_pool_norm, v_pool_w_in, v_pool_w_grp, v_pool_scale, v_pool_w_out, v_conv_norm, v_conv_w_in, v_conv_w, v_conv_w_out,
                         v_mla_norm, v_mla_w_in, v_mla_q_norm, v_mla_w_q_up, v_mla_kv_norm, v_mla_w_kv_up, v_mla_w_out, v_final_norm)))
    chip = 2 * lax.axis_index("x") + lax.axis_index("y")
    core = lax.axis_index("c")

    core1 = core.astype(jnp.int32).reshape(1)
    chip_core = jnp.stack([chip, core]).astype(jnp.int32)
    shard_shape = lambda piece: w[piece[0]].shape[1:]
    shard_axis = lambda piece: BIG_SHARD_AXIS[piece[0]] - 1
    full_shape = lambda piece: tuple(d * (N_CHIPS if a == shard_axis(piece) else 1) for a, d in enumerate(shard_shape(piece)))

    gather_rows, gather_at, packs = {}, {}, {}
    for grp, layout in GATHER_LAYOUT.items():
        gather_rows[grp], gather_at[grp] = _slot_rows(layout, shard_shape, PACK_ROW_ALIGN)
        packs[grp] = _pack_slot({(n, j): w[n][j] for (n, j), _, _ in layout}, layout, gather_rows[grp], BF16)

    def gathered_weights(grp, gathered):
        out = {}
        for piece, (r0, n, kind) in gather_at[grp].items():
            if kind == "flat":
                out[piece] = jnp.concatenate([gathered[k, r0:r0 + n].reshape(shard_shape(piece)) for k in range(N_CHIPS)], axis=shard_axis(piece))
            else:
                out[piece] = Packed(gathered, r0, kind, full_shape(piece))
        return out

    p0_start = _ag_ici_start(packs["p0"], w["final_norm"], name="ag_p0_start")
    cv_start = _ag_ici_start(packs["cv"], p0_start[4], name="ag_cv_start")
    ml_start = _ag_ici_start(packs["ml"], cv_start[4], name="ag_ml_start")
    in_flight = {"p0": p0_start, "cv": cv_start, "ml": ml_start}

    forwarding = {}

    def prefetch(grp, after):
        if grp == "p0":
            after = (after, ml_start[4])
        send_sems, recv_sems, w_thru, land, _ = in_flight[grp]
        w_thru, land = _ag_ici_wait(send_sems, recv_sems, w_thru, land, after, name=f"ag_{grp}_wait")
        forwarding[grp] = _ag_forward_start(land, w_thru, name=f"ag_{grp}_fwd_start")

    def weights_for(grp, after):
        if grp not in forwarding:
            prefetch(grp, after)
        return gathered_weights(grp, _ag_forward_wait(*forwarding[grp], after, name=f"ag_{grp}_fwd_wait"))

    weights_for.prefetch = prefetch

    ws = {"conv_norm": w["conv_norm"], "final_norm": w["final_norm"]}
    ws.update(_gather_small({n: w[n] for n, _, _ in SMALL_GATHER}, name="ag_small"))

    reduce_rows, reduce_at = {}, {}
    for grp, layout in REDUCE_LAYOUT.items():
        reduce_rows[grp], reduce_at[grp] = _slot_rows(layout, shard_shape, RS_ROW_ALIGN)
    group_of = {piece: grp for grp, layout in REDUCE_LAYOUT.items() for piece, _, _ in layout}

    class Sink:
        def __init__(self):
            self.buf = {grp: lax.empty((N_CHIPS, rows, PACK_COLS), BF16) for grp, rows in reduce_rows.items()}
            self.started = {}

        def dest(self, piece):
            grp = group_of[piece]
            r0, _, kind = reduce_at[grp][piece]
            return None if kind == "flat" else Packed(self.buf[grp], r0, kind, full_shape(piece))

        def put(self, piece, result):
            grp = group_of[piece]
            r0, n, kind = reduce_at[grp][piece]
            if kind == "flat":
                parts = jnp.split(result, N_CHIPS, axis=shard_axis(piece))
                result = lax.dynamic_update_slice(self.buf[grp], jnp.stack([p.reshape(n, PACK_COLS) for p in parts]), (0, r0, 0))
            self.buf[grp] = result

        def start(self, grp, tag):
            theirs = _rs_sibling_swap(self.buf[grp], name=f"{tag}_swap")
            chip_sum = _add2_bf16(self.buf[grp], theirs, core1, name=f"{tag}_add2")
            self.started[grp] = _rs_chip_exchange_start(chip_sum, name=f"{tag}_chips_start")
            return self.started[grp][4]

        def finish(self, grp, after, tag):
            send_sems, recv_sems, chip_sum, land, _ = self.started[grp]
            chip_sum, recv = _rs_chip_exchange_wait(send_sems, recv_sems, chip_sum, land, after, name=f"{tag}_chips_wait")
            half_sum = _add4_f32(chip_sum, recv, chip_core, name=f"{tag}_add4")
            return _rs_sibling_join(half_sum, name=f"{tag}_join")

        def late_ready(self):
            return self.start("late", "rsa")

    sink = Sink()

    loss_part, grad_x, gs = _local_step(x[0], positions, loss_target[0], weights_for, ws, sink)
    loss = lax.psum(loss_part[0, 0], ("x", "y", "c"))

    g, delta, new_m, new_v = {}, {}, {}, {}

    def adam_big(n):
        nj = w[n].shape[0]
        where = [(group_of[(n, j)],) + reduce_at[group_of[(n, j)]][(n, j)] for j in range(nj)]
        if where[0][3] == "flat":
            g[n] = jnp.stack([g_rows[grp][r0:r0 + rows].reshape(w[n].shape[1:]) for grp, r0, rows, _ in where])
            shp = w[n].shape
            two_d = lambda a: a.reshape(-1, shp[-1])
            d_, m_, v_ = _adamw(two_d(w[n]), two_d(g[n]), two_d(m[n]), two_d(v[n]), name=f"adamw_{n}")
            delta[n], new_m[n], new_v[n] = d_.reshape(shp), m_.reshape(shp), v_.reshape(shp)
        else:
            g[n], delta[n], new_m[n], new_v[n] = _adamw_rows(w[n], m[n], v[n], [(g_rows[grp], r0) for grp, r0, _, _ in where], name=f"adamw_{n}")

    first_token = sink.start("first", "rsb")
    g_rows = {"late": sink.finish("late", first_token, "rsa")}
    late_only = [n for n in BIG if all(group_of[(n, j)] == "late" for j in range(w[n].shape[0]))]
    for n in late_only:
        adam_big(n)
    g_rows["first"] = sink.finish("first", tuple(delta[n] for n in late_only), "rsb")
    for n in BIG:
        if n not in late_only:
            adam_big(n)

    gs_sum = _reduce_small(gs, g_rows["first"], name="ar_small")
    row = lambda d: {n: (d[n].reshape(1, -1) if d[n].ndim == 1 else d[n]) for n in SMALL}
    small_out = _adamw_small(row(w), row(m), row(v), gs_sum, chip.astype(jnp.int32).reshape(1), name="adamw_small")
    for dst, res in zip((g, delta, new_m, new_v), small_out):
        for n in SMALL:
            dst[n] = res[n].reshape(w[n].shape)

    return (loss, grad_x[None], *[g[n] for n in names], *[delta[n] for n in names],
            *[new_m[n] for n in names], *[new_v[n] for n in names])
```

```python
import functools

import jax
import jax.numpy as jnp
from jax import lax
from jax.experimental import pallas as pl
from jax.experimental.pallas import tpu as pltpu

F32 = jnp.float32
BF16 = jnp.bfloat16

D_MODEL = 1024
D_INNER = 2048
POOL_WINDOWS = (2, 4, 8, 16)
POOL_GROUP = 512
N_HEADS = 16
QK_NOPE = 128
QK_ROPE = 64
V_DIM = 128
HEAD_PAD = 256
Q_LORA = 384
KV_LORA = 256
MLA_IN = Q_LORA + KV_LORA + QK_ROPE + D_INNER
MLA_IN_PAD = 2816
ATTN_SCALE = (QK_NOPE + QK_ROPE) ** -0.5
ROPE_BASE = 10000.0
NORM_EPS = 1e-6
HALO = 16

ADAM_LR = 0.001
ADAM_B1 = 0.9
ADAM_B2 = 0.999
ADAM_EPS = 1e-08
ADAM_WD = 0.01
ADAM_STEP = 10

N_CHIPS = 4
N_DEV = 8
LANES = 128
PACK_COLS = 1024
V7X_VMEM_LIMIT = 56 * 1024 * 1024
MESH = pl.DeviceIdType.MESH


def _cparams(*sem):
    return pltpu.CompilerParams(dimension_semantics=sem, vmem_limit_bytes=V7X_VMEM_LIMIT)


def _pick(n, cap):
    best = None
    for d in range(LANES, min(n, cap) + 1, LANES):
        if n % d == 0:
            best = d
    assert best is not None, (n, cap)
    return best


def _sigmoid(z):
    return 1.0 / (1.0 + jnp.exp(-z))


class Packed:
    def __init__(self, buf, r0, kind, shape):
        self.buf, self.r0, self.kind, self.shape = buf, r0, kind, shape

    def block(self, rb, cb, bk):
        assert self.r0 % bk == 0, (self.r0, bk)
        if self.kind == "cols":
            K = self.shape[0]
            per = self.shape[1] // (N_CHIPS * PACK_COLS)
            assert K % bk == 0
            return cb // per, (self.r0 + (cb % per) * K) // bk + rb
        kk = self.shape[0] // N_CHIPS
        assert kk % bk == 0
        per = kk // bk
        return rb // per, self.r0 // bk + rb % per


def _mm(a, b, *, trans_a=False, trans_b=False, out_dtype=F32, residual=None, after=None, into=None, tm, tn, tk, name):
    if trans_a:
        K, M = a.shape
    else:
        M, K = a.shape
    if trans_b:
        N, K2 = b.shape
    else:
        K2, N = b.shape
    assert K == K2 and M % tm == 0 and N % tn == 0 and K % tk == 0, (name, a.shape, b.shape, tm, tn, tk)
    nk = K // tk
    dn = (((0 if trans_a else 1,), (1 if trans_b else 0,)), ((), ()))
    has_res = residual is not None
    n_skip = (after is not None) + (into is not None)
    b_all = isinstance(b, Packed) and b.kind == "rows"
    o_all = into is not None and into.kind == "rows"
    assert not b_all or (not trans_a and (tn == N if trans_b else tk == K)), name
    assert not o_all or tm == M, name

    def body(*refs):
        if has_res:
            a_ref, b_ref, r_ref = refs[:3]
            refs = refs[3:]
        else:
            a_ref, b_ref = refs[:2]
            r_ref = None
            refs = refs[2:]
        refs = refs[n_skip:]
        o_ref, rest = refs[0], refs[1:]
        if b_all and trans_b:
            part = jnp.concatenate([lax.dot_general(a_ref[...], b_ref[c], dn, preferred_element_type=F32) for c in range(N_CHIPS)], axis=1)
        elif b_all:
            kk = K // N_CHIPS
            part = lax.dot_general(a_ref[:, 0:kk], b_ref[0], dn, preferred_element_type=F32)
            for c in range(1, N_CHIPS):
                part = part + lax.dot_general(a_ref[:, c * kk:(c + 1) * kk], b_ref[c], dn, preferred_element_type=F32)
        else:
            part = lax.dot_general(a_ref[...], b_ref[...], dn, preferred_element_type=F32)

        def finish(acc):
            if has_res:
                acc = acc + r_ref[...]
            if o_all:
                mk = M // N_CHIPS
                for c in range(N_CHIPS):
                    o_ref[c] = acc[c * mk:(c + 1) * mk].astype(o_ref.dtype)
            else:
                o_ref[...] = acc.astype(o_ref.dtype)

        if nk == 1:
            finish(part)
        else:
            acc_ref = rest[0]
            k = pl.program_id(2)

            @pl.when(k == 0)
            def _():
                acc_ref[...] = part

            @pl.when(k > 0)
            def _():
                acc_ref[...] += part

            @pl.when(k == nk - 1)
            def _():
                finish(acc_ref[...])

    a_spec = pl.BlockSpec((tk, tm), lambda i, j, k: (k, i)) if trans_a else pl.BlockSpec((tm, tk), lambda i, j, k: (i, k))
    if b_all:
        kkb = b.shape[0] // N_CHIPS
        b_spec = pl.BlockSpec((N_CHIPS, kkb, PACK_COLS), lambda i, j, k: (0, b.r0 // kkb, 0))
        b_arg = b.buf
    elif isinstance(b, Packed):
        if trans_b:
            assert tk == PACK_COLS
            b_spec = pl.BlockSpec((None, tn, tk), lambda i, j, k: (*b.block(j, k, tn), 0))
        else:
            assert tn == PACK_COLS
            b_spec = pl.BlockSpec((None, tk, tn), lambda i, j, k: (*b.block(k, j, tk), 0))
        b_arg = b.buf
    else:
        b_spec = pl.BlockSpec((tn, tk), lambda i, j, k: (j, k)) if trans_b else pl.BlockSpec((tk, tn), lambda i, j, k: (k, j))
        b_arg = b
    o_spec = pl.BlockSpec((tm, tn), lambda i, j, k: (i, j))
    in_specs = [a_spec, b_spec] + ([o_spec] if has_res else [])
    args = (a, b_arg) + ((residual,) if has_res else ())
    aliases = {}
    if after is not None:
        in_specs.append(pl.BlockSpec(memory_space=pl.ANY))
        args += (after,)
    if into is None:
        out_shape, out_spec = jax.ShapeDtypeStruct((M, N), out_dtype), o_spec
    else:
        assert tn == PACK_COLS and into.shape == (M, N)
        in_specs.append(pl.BlockSpec(memory_space=pl.ANY))
        aliases = {len(args): 0}
        args += (into.buf,)
        out_shape = jax.ShapeDtypeStruct(into.buf.shape, into.buf.dtype)
        if o_all:
            out_spec = pl.BlockSpec((N_CHIPS, M // N_CHIPS, tn), lambda i, j, k: (0, into.r0 // (M // N_CHIPS), 0))
        else:
            out_spec = pl.BlockSpec((None, tm, tn), lambda i, j, k: (*into.block(i, j, tm), 0))
    return pl.pallas_call(
        body, name=name, out_shape=out_shape,
        grid=(M // tm, N // tn, nk),
        in_specs=in_specs, out_specs=out_spec, input_output_aliases=aliases,
        scratch_shapes=[pltpu.VMEM((tm, tn), F32)] if nk > 1 else [],
        compiler_params=_cparams("parallel", "parallel", "arbitrary"),
    )(*args)


def _grouped_tn(a, b, *, tk, name):
    S = a.shape[0]
    G = POOL_GROUP
    nk = S // tk

    def body(a_ref, b_ref, o_ref, acc_ref):
        k = pl.program_id(1)
        part = lax.dot_general(a_ref[...], b_ref[...], (((0,), (0,)), ((), ())), preferred_element_type=F32)

        @pl.when(k == 0)
        def _():
            acc_ref[...] = part

        @pl.when(k > 0)
        def _():
            acc_ref[...] += part

        @pl.when(k == nk - 1)
        def _():
            o_ref[...] = acc_ref[...].astype(o_ref.dtype)

    return pl.pallas_call(
        body, name=name,
        out_shape=jax.ShapeDtypeStruct((len(POOL_WINDOWS), G, G), BF16),
        grid=(len(POOL_WINDOWS), nk),
        in_specs=[pl.BlockSpec((tk, G), lambda g, k: (k, g)), pl.BlockSpec((tk, G), lambda g, k: (k, g))],
        out_specs=pl.BlockSpec((None, G, G), lambda g, k: (g, 0, 0)),
        scratch_shapes=[pltpu.VMEM((G, G), F32)],
        compiler_params=_cparams("parallel", "arbitrary"),
    )(a, b)


def _rms_fwd(x, g, *, tm, name):
    S, D = x.shape

    def body(x_ref, g_ref, o_ref):
        xv = x_ref[...]
        rstd = lax.rsqrt(jnp.mean(xv * xv, axis=-1, keepdims=True) + NORM_EPS)
        o_ref[...] = (xv * rstd * g_ref[...]).astype(o_ref.dtype)

    return pl.pallas_call(
        body, name=name,
        out_shape=jax.ShapeDtypeStruct((S, D), BF16),
        grid=(S // tm,),
        in_specs=[pl.BlockSpec((tm, D), lambda i: (i, 0)), pl.BlockSpec((1, D), lambda i: (0, 0))],
        out_specs=pl.BlockSpec((tm, D), lambda i: (i, 0)),
        compiler_params=_cparams("parallel"),
    )(x, g)


def _rms_bwd_math(xv, gv, dxn):
    rstd = lax.rsqrt(jnp.mean(xv * xv, axis=-1, keepdims=True) + NORM_EPS)
    xh = xv * rstd
    dg = jnp.sum(dxn * xh, axis=0, keepdims=True)
    dxh = dxn * gv
    dx = rstd * (dxh - xh * jnp.mean(dxh * xh, axis=-1, keepdims=True))
    return dx, dg


def _rms_bwd(x, g, dxn, dres, *, tm, name):
    S, D = x.shape

    def body(x_ref, g_ref, dxn_ref, dres_ref, dx_ref, dxb_ref, dg_ref):
        dx, dg = _rms_bwd_math(x_ref[...], g_ref[...], dxn_ref[...])
        dx = dx + dres_ref[...]
        dx_ref[...] = dx
        dxb_ref[...] = dx.astype(BF16)

        @pl.when(pl.program_id(0) == 0)
        def _():
            dg_ref[...] = dg

        @pl.when(pl.program_id(0) > 0)
        def _():
            dg_ref[...] += dg

    row = pl.BlockSpec((tm, D), lambda i: (i, 0))
    vec = pl.BlockSpec((1, D), lambda i: (0, 0))
    return pl.pallas_call(
        body, name=name,
        out_shape=(jax.ShapeDtypeStruct((S, D), F32), jax.ShapeDtypeStruct((S, D), BF16), jax.ShapeDtypeStruct((1, D), F32)),
        grid=(S // tm,),
        in_specs=[row, vec, row, row],
        out_specs=(row, row, vec),
        compiler_params=_cparams("arbitrary"),
    )(x, g, dxn, dres)


def _final_loss(x, g, target, *, tm, name):
    S, D = x.shape

    def body(x_ref, g_ref, t_ref, loss_ref, dx_ref, dxb_ref, dg_ref):
        xv = x_ref[...]
        gv = g_ref[...]
        rstd = lax.rsqrt(jnp.mean(xv * xv, axis=-1, keepdims=True) + NORM_EPS)
        xh = xv * rstd
        err = xh * gv - t_ref[...]
        part = 0.5 * jnp.sum(jnp.mean(err * err, axis=-1, keepdims=True), axis=0, keepdims=True)
        dy = err * (1.0 / D)
        dg = jnp.sum(dy * xh, axis=0, keepdims=True)
        dxh = dy * gv
        dx = rstd * (dxh - xh * jnp.mean(dxh * xh, axis=-1, keepdims=True))
        dx_ref[...] = dx
        dxb_ref[...] = dx.astype(BF16)
        lossb = jnp.broadcast_to(part, loss_ref.shape)

        @pl.when(pl.program_id(0) == 0)
        def _():
            dg_ref[...] = dg
            loss_ref[...] = lossb

        @pl.when(pl.program_id(0) > 0)
        def _():
            dg_ref[...] += dg
            loss_ref[...] += lossb

    row = pl.BlockSpec((tm, D), lambda i: (i, 0))
    vec = pl.BlockSpec((1, D), lambda i: (0, 0))
    lspec = pl.BlockSpec((8, LANES), lambda i: (0, 0))
    return pl.pallas_call(
        body, name=name,
        out_shape=(jax.ShapeDtypeStruct((8, LANES), F32), jax.ShapeDtypeStruct((S, D), F32),
                   jax.ShapeDtypeStruct((S, D), BF16), jax.ShapeDtypeStruct((1, D), F32)),
        grid=(S // tm,),
        in_specs=[row, vec, row],
        out_specs=(lspec, row, row, vec),
        compiler_params=_cparams("arbitrary"),
    )(x, g, target)


def _prev_halo_spec(tm, width, col):
    r = tm // HALO
    return pl.BlockSpec((HALO, width), lambda i: (jnp.maximum(i * r - 1, 0), col))


def _next_halo_spec(tm, width, col, S):
    r = tm // HALO
    last = S // HALO - 1
    return pl.BlockSpec((HALO, width), lambda i: (jnp.minimum((i + 1) * r, last), col))


def _shift_down(ext, k):
    return pltpu.roll(ext, k, 0)[HALO:, :]


def _shift_up(ext, k, tm):
    n = ext.shape[0]
    return pltpu.roll(ext, n - k, 0)[:tm, :]


def _pool_window_sum(ext, w):
    s = ext
    k = 1
    while k < w:
        s = s + pltpu.roll(s, k, 0)
        k *= 2
    return s[HALO:, :]


def _pooled_group(u_ref, halo, g, w, t_idx):
    cs = slice(g * POOL_GROUP, (g + 1) * POOL_GROUP)
    u = u_ref[:, cs]
    ext = jnp.concatenate([halo[:, cs], u], axis=0)
    inv = 1.0 / jnp.minimum(t_idx + 1, w).astype(F32)
    return _pool_window_sum(ext, w) * inv - u


def _pool_fwd(h, w_grp, scale, *, tm, name):
    S = h.shape[0]
    E = D_INNER

    def body(u_ref, uh_ref, z_ref, wg_ref, sc_ref, y_ref):
        i = pl.program_id(0)
        halo = jnp.where(i > 0, uh_ref[...], 0.0)
        t_idx = i * tm + lax.broadcasted_iota(jnp.int32, (tm, 1), 0)
        for g, w in enumerate(POOL_WINDOWS):
            cs = slice(g * POOL_GROUP, (g + 1) * POOL_GROUP)
            pooled = _pooled_group(u_ref, halo, g, w, t_idx)
            mixed = jnp.dot(pooled.astype(BF16), wg_ref[g], preferred_element_type=F32)
            z = z_ref[:, cs]
            y_ref[:, cs] = (mixed * sc_ref[:, cs] * (z * _sigmoid(z))).astype(BF16)

    return pl.pallas_call(
        body, name=name,
        out_shape=jax.ShapeDtypeStruct((S, E), BF16),
        grid=(S // tm,),
        in_specs=[pl.BlockSpec((tm, E), lambda i: (i, 0)), _prev_halo_spec(tm, E, 0),
                  pl.BlockSpec((tm, E), lambda i: (i, 1)),
                  pl.BlockSpec((len(POOL_WINDOWS), POOL_GROUP, POOL_GROUP), lambda i: (0, 0, 0)),
                  pl.BlockSpec((1, E), lambda i: (0, 0))],
        out_specs=pl.BlockSpec((tm, E), lambda i: (i, 0)),
        compiler_params=_cparams("parallel"),
    )(h, h, h, w_grp, scale)


def _pool_bwd1(h, dy, w_grp, scale, *, tm, name):
    S = h.shape[0]
    E = D_INNER

    def body(u_ref, uh_ref, z_ref, dy_ref, wg_ref, sc_ref, pooled_ref, dmixed_ref, dpooled_ref, dz_ref, dsc_ref):
        i = pl.program_id(0)
        halo = jnp.where(i > 0, uh_ref[...], 0.0)
        t_idx = i * tm + lax.broadcasted_iota(jnp.int32, (tm, 1), 0)
        for g, w in enumerate(POOL_WINDOWS):
            cs = slice(g * POOL_GROUP, (g + 1) * POOL_GROUP)
            pooled = _pooled_group(u_ref, halo, g, w, t_idx).astype(BF16)
            wg = wg_ref[g]
            mixed = jnp.dot(pooled, wg, preferred_element_type=F32)
            z = z_ref[:, cs]
            sg = _sigmoid(z)
            dyv = dy_ref[:, cs]
            sc = sc_ref[:, cs]
            dms = dyv * (z * sg)
            dz = dyv * (mixed * sc) * (sg * (1.0 + z * (1.0 - sg)))
            dsc = jnp.sum(dms * mixed, axis=0, keepdims=True)
            dmixed = (dms * sc).astype(BF16)
            dpooled = lax.dot_general(dmixed, wg, (((1,), (1,)), ((), ())), preferred_element_type=F32)
            pooled_ref[:, cs] = pooled
            dmixed_ref[:, cs] = dmixed
            dpooled_ref[:, cs] = dpooled
            dz_ref[:, cs] = dz.astype(BF16)

            @pl.when(i == 0)
            def _():
                dsc_ref[:, cs] = dsc

            @pl.when(i > 0)
            def _():
                dsc_ref[:, cs] += dsc

    row = pl.BlockSpec((tm, E), lambda i: (i, 0))
    vec = pl.BlockSpec((1, E), lambda i: (0, 0))
    return pl.pallas_call(
        body, name=name,
        out_shape=(jax.ShapeDtypeStruct((S, E), BF16), jax.ShapeDtypeStruct((S, E), BF16),
                   jax.ShapeDtypeStruct((S, E), F32), jax.ShapeDtypeStruct((S, E), BF16),
                   jax.ShapeDtypeStruct((1, E), F32)),
        grid=(S // tm,),
        in_specs=[row, _prev_halo_spec(tm, E, 0), pl.BlockSpec((tm, E), lambda i: (i, 1)), row,
                  pl.BlockSpec((len(POOL_WINDOWS), POOL_GROUP, POOL_GROUP), lambda i: (0, 0, 0)), vec],
        out_specs=(row, row, row, row, vec),
        compiler_params=_cparams("arbitrary"),
    )(h, h, h, dy, w_grp, scale)


def _pool_bwd2(dpooled, dz, *, tm, name):
    S = dpooled.shape[0]
    E = D_INNER
    nt = S // tm

    def body(dp_ref, dpn_ref, dz_ref, dh_ref):
        i = pl.program_id(0)
        nxt = jnp.where(i < nt - 1, dpn_ref[...], 0.0)
        t_ext = i * tm + lax.broadcasted_iota(jnp.int32, (tm + HALO, 1), 0)
        for g, w in enumerate(POOL_WINDOWS):
            cs = slice(g * POOL_GROUP, (g + 1) * POOL_GROUP)
            dp = dp_ref[:, cs]
            inv = 1.0 / jnp.minimum(t_ext + 1, w).astype(F32)
            s = jnp.concatenate([dp, nxt[:, cs]], axis=0) * inv
            n = tm + HALO
            k = 1
            while k < w:
                s = s + pltpu.roll(s, n - k, 0)
                k *= 2
            dh_ref[:, cs] = (s[:tm, :] - dp).astype(BF16)
        dh_ref[:, E:] = dz_ref[...]

    return pl.pallas_call(
        body, name=name,
        out_shape=jax.ShapeDtypeStruct((S, 2 * E), BF16),
        grid=(nt,),
        in_specs=[pl.BlockSpec((tm, E), lambda i: (i, 0)), _next_halo_spec(tm, E, 0, S),
                  pl.BlockSpec((tm, E), lambda i: (i, 0))],
        out_specs=pl.BlockSpec((tm, 2 * E), lambda i: (i, 0)),
        compiler_params=_cparams("parallel"),
    )(dpooled, dpooled, dz)


CONV_CHUNK = 512


def _conv_fwd(h, cw, *, tm, name):
    S = h.shape[0]
    E = D_INNER

    def body(b_ref, c_ref, hh_ref, z_ref, ch_ref, hhh_ref, w_ref, y_ref):
        i = pl.program_id(0)
        for j in range(E // CONV_CHUNK):
            cs = slice(j * CONV_CHUNK, (j + 1) * CONV_CHUNK)
            p = c_ref[:, cs] * hh_ref[:, cs]
            ph = jnp.where(i > 0, ch_ref[:, cs] * hhh_ref[:, cs], 0.0)
            ext = jnp.concatenate([ph, p], axis=0)
            conv = w_ref[2:3, cs] * p + w_ref[1:2, cs] * _shift_down(ext, 1) + w_ref[0:1, cs] * _shift_down(ext, 2)
            z = z_ref[:, cs]
            y_ref[:, cs] = (b_ref[:, cs] * conv * (z * _sigmoid(z))).astype(BF16)

    col = lambda c: pl.BlockSpec((tm, E), lambda i: (i, c))
    return pl.pallas_call(
        body, name=name,
        out_shape=jax.ShapeDtypeStruct((S, E), BF16),
        grid=(S // tm,),
        in_specs=[col(0), col(1), col(2), col(3), _prev_halo_spec(tm, E, 1), _prev_halo_spec(tm, E, 2),
                  pl.BlockSpec((8, E), lambda i: (0, 0))],
        out_specs=pl.BlockSpec((tm, E), lambda i: (i, 0)),
        compiler_params=_cparams("parallel"),
    )(h, h, h, h, h, h, cw)


def _conv_bwd(h, dy, cw, *, tm, name):
    S = h.shape[0]
    E = D_INNER
    nt = S // tm

    def body(b_ref, c_ref, hh_ref, z_ref, dy_ref, ch_ref, hhh_ref, bn_ref, zn_ref, dyn_ref, w_ref, dh_ref, dw_ref):
        i = pl.program_id(0)
        for j in range(E // CONV_CHUNK):
            cs = slice(j * CONV_CHUNK, (j + 1) * CONV_CHUNK)
            w0, w1, w2 = w_ref[0:1, cs], w_ref[1:2, cs], w_ref[2:3, cs]
            c, hh, b, z, dyv = c_ref[:, cs], hh_ref[:, cs], b_ref[:, cs], z_ref[:, cs], dy_ref[:, cs]
            p = c * hh
            ph = jnp.where(i > 0, ch_ref[:, cs] * hhh_ref[:, cs], 0.0)
            ext = jnp.concatenate([ph, p], axis=0)
            pm1 = _shift_down(ext, 1)
            pm2 = _shift_down(ext, 2)
            conv = w2 * p + w1 * pm1 + w0 * pm2
            sg = _sigmoid(z)
            dy0 = dyv * (z * sg)
            dz = dyv * (b * conv) * (sg * (1.0 + z * (1.0 - sg)))
            db = dy0 * conv
            dconv = dy0 * b
            zn = zn_ref[:, cs]
            dconv_n = jnp.where(i < nt - 1, dyn_ref[:, cs] * (zn * _sigmoid(zn)) * bn_ref[:, cs], 0.0)
            dext = jnp.concatenate([dconv, dconv_n], axis=0)
            dp = w2 * dconv + w1 * _shift_up(dext, 1, tm) + w0 * _shift_up(dext, 2, tm)
            dh_ref[:, 0 * E + j * CONV_CHUNK:0 * E + (j + 1) * CONV_CHUNK] = db.astype(BF16)
            dh_ref[:, 1 * E + j * CONV_CHUNK:1 * E + (j + 1) * CONV_CHUNK] = (dp * hh).astype(BF16)
            dh_ref[:, 2 * E + j * CONV_CHUNK:2 * E + (j + 1) * CONV_CHUNK] = (dp * c).astype(BF16)
            dh_ref[:, 3 * E + j * CONV_CHUNK:3 * E + (j + 1) * CONV_CHUNK] = dz.astype(BF16)
            dw = jnp.concatenate([jnp.sum(dconv * pm2, axis=0, keepdims=True),
                                  jnp.sum(dconv * pm1, axis=0, keepdims=True),
                                  jnp.sum(dconv * p, axis=0, keepdims=True),
                                  jnp.zeros((5, CONV_CHUNK), F32)], axis=0)

            @pl.when(i == 0)
            def _():
                dw_ref[:, cs] = dw

            @pl.when(i > 0)
            def _():
                dw_ref[:, cs] += dw

    col = lambda c: pl.BlockSpec((tm, E), lambda i: (i, c))
    return pl.pallas_call(
        body, name=name,
        out_shape=(jax.ShapeDtypeStruct((S, 4 * E), BF16), jax.ShapeDtypeStruct((8, E), F32)),
        grid=(nt,),
        in_specs=[col(0), col(1), col(2), col(3), pl.BlockSpec((tm, E), lambda i: (i, 0)),
                  _prev_halo_spec(tm, E, 1), _prev_halo_spec(tm, E, 2),
                  _next_halo_spec(tm, E, 0, S), _next_halo_spec(tm, E, 3, S), _next_halo_spec(tm, E, 0, S),
                  pl.BlockSpec((8, E), lambda i: (0, 0))],
        out_specs=(pl.BlockSpec((tm, 4 * E), lambda i: (i, 0)), pl.BlockSpec((8, E), lambda i: (0, 0))),
        compiler_params=_cparams("arbitrary"),
    )(h, h, h, h, dy, h, h, h, h, dy, cw)


Z_COLS = D_INNER // LANES
KV_LAT_BLK = D_INNER // KV_LORA
Q_LAT_BLK = (D_INNER + KV_LORA) // Q_LORA
K_ROPE_BLK = (D_INNER + KV_LORA + Q_LORA) // LANES


def _rope(blk, c, s1, s2):
    return blk * c + pltpu.roll(blk, LANES - QK_ROPE // 2, 1) * s1 + pltpu.roll(blk, QK_ROPE // 2, 1) * s2


def _unrope(blk, c, s1, s2):
    return blk * c - pltpu.roll(blk, LANES - QK_ROPE // 2, 1) * s1 - pltpu.roll(blk, QK_ROPE // 2, 1) * s2


def _lat_norm(v, g):
    rstd = lax.rsqrt(jnp.mean(v * v, axis=-1, keepdims=True) + NORM_EPS)
    return v * rstd * g


def _mla_latent_fwd(h, q_norm, kv_norm, tabs, *, tm, name):
    S = h.shape[0]

    def body(kv_ref, q_ref, kr_ref, qg_ref, kvg_ref, c_ref, s1_ref, s2_ref, qn_ref, kvn_ref, krr_ref):
        qn_ref[...] = _lat_norm(q_ref[...], qg_ref[...]).astype(BF16)
        kvn_ref[...] = _lat_norm(kv_ref[...], kvg_ref[...]).astype(BF16)
        krr_ref[...] = _rope(kr_ref[...], c_ref[...], s1_ref[...], s2_ref[...]).astype(BF16)

    tab = pl.BlockSpec((tm, LANES), lambda i: (i, 0))
    return pl.pallas_call(
        body, name=name,
        out_shape=(jax.ShapeDtypeStruct((S, Q_LORA), BF16), jax.ShapeDtypeStruct((S, KV_LORA), BF16),
                   jax.ShapeDtypeStruct((S, LANES), BF16)),
        grid=(S // tm,),
        in_specs=[pl.BlockSpec((tm, KV_LORA), lambda i: (i, KV_LAT_BLK)), pl.BlockSpec((tm, Q_LORA), lambda i: (i, Q_LAT_BLK)),
                  pl.BlockSpec((tm, LANES), lambda i: (i, K_ROPE_BLK)),
                  pl.BlockSpec((1, Q_LORA), lambda i: (0, 0)), pl.BlockSpec((1, KV_LORA), lambda i: (0, 0)), tab, tab, tab],
        out_specs=(pl.BlockSpec((tm, Q_LORA), lambda i: (i, 0)), pl.BlockSpec((tm, KV_LORA), lambda i: (i, 0)), tab),
        compiler_params=_cparams("parallel"),
    )(h, h, h, q_norm, kv_norm, *tabs)


def _mla_q_up(q_n, w_q_pad, tabs, *, tm, name):
    S = q_n.shape[0]

    def body(a_ref, w_ref, c_ref, s1_ref, s2_ref, o_ref):
        a = a_ref[...]
        for hd in range(N_HEADS):
            acc = jnp.dot(a, w_ref[:, hd * HEAD_PAD:(hd + 1) * HEAD_PAD], preferred_element_type=F32)
            o_ref[hd, :, :QK_NOPE] = acc[:, :QK_NOPE].astype(BF16)
            o_ref[hd, :, QK_NOPE:] = _rope(acc[:, QK_NOPE:], c_ref[...], s1_ref[...], s2_ref[...]).astype(BF16)

    tab = pl.BlockSpec((tm, LANES), lambda i: (i, 0))
    return pl.pallas_call(
        body, name=name,
        out_shape=jax.ShapeDtypeStruct((N_HEADS, S, HEAD_PAD), BF16),
        grid=(S // tm,),
        in_specs=[pl.BlockSpec((tm, Q_LORA), lambda i: (i, 0)), pl.BlockSpec((Q_LORA, N_HEADS * HEAD_PAD), lambda i: (0, 0)),
                  tab, tab, tab],
        out_specs=pl.BlockSpec((N_HEADS, tm, HEAD_PAD), lambda i: (0, i, 0)),
        compiler_params=_cparams("parallel"),
    )(q_n, w_q_pad, *tabs)


def _mla_kv_up(kv_n, w_kv, krr, *, tm, name):
    S = kv_n.shape[0]
    heads_per_chip = N_HEADS // N_CHIPS

    def body(a_ref, w_ref, krr_ref, k_ref, v_ref):
        a = a_ref[...]
        ones = jnp.ones((tm, V_DIM), BF16)
        for hd in range(N_HEADS):
            lo = (hd % heads_per_chip) * HEAD_PAD
            acc = jnp.dot(a, w_ref[hd // heads_per_chip, :, lo:lo + HEAD_PAD], preferred_element_type=F32)
            k_ref[hd, :, :QK_NOPE] = acc[:, :QK_NOPE].astype(BF16)
            k_ref[hd, :, QK_NOPE:] = krr_ref[...]
            v_ref[hd, :, :V_DIM] = acc[:, QK_NOPE:].astype(BF16)
            v_ref[hd, :, V_DIM:] = ones

    head_blk = pl.BlockSpec((N_HEADS, tm, HEAD_PAD), lambda i: (0, i, 0))
    sds = jax.ShapeDtypeStruct((N_HEADS, S, HEAD_PAD), BF16)
    return pl.pallas_call(
        body, name=name, out_shape=(sds, sds),
        grid=(S // tm,),
        in_specs=[pl.BlockSpec((tm, KV_LORA), lambda i: (i, 0)),
                  pl.BlockSpec((N_CHIPS, KV_LORA, PACK_COLS), lambda i: (0, w_kv.r0 // KV_LORA, 0)),
                  pl.BlockSpec((tm, LANES), lambda i: (i, 0))],
        out_specs=(head_blk, head_blk),
        compiler_params=_cparams("parallel"),
    )(kv_n, w_kv.buf, krr)


LOG2E = 1.4426950408889634
SCORE_TO_LOG2 = ATTN_SCALE * LOG2E


def _flash_fwd(q_full, k_full, v_aug, h, *, tq, name):
    H, S, _ = q_full.shape
    tk = tq
    HP = 2
    QT = max(t for t in (4, 2, 1) if (S // tq) % t == 0)
    rows_of = lambda t: slice(t * tq, (t + 1) * tq)

    def body(q_ref, k_ref, v_ref, z_ref, o_ref, y_ref, lse_ref, m_sc, acc_sc):
        first = pl.program_id(1) * QT
        m_sc[...] = jnp.full(m_sc.shape, -1e30, F32)
        acc_sc[...] = jnp.zeros(acc_sc.shape, F32)

        def chunk(j, masked):
            off = pl.multiple_of(j * tk, tk)
            for hh in range(HP):
                kj = k_ref[hh, pl.ds(off, tk), :]
                vj = v_ref[hh, pl.ds(off, tk), :]
                for t in range(QT):
                    if masked[t] is None:
                        continue
                    rows = rows_of(t)
                    s = lax.dot_general(q_ref[hh, rows], kj, (((1,), (1,)), ((), ())), preferred_element_type=F32) * SCORE_TO_LOG2
                    if masked[t]:
                        keep = lax.broadcasted_iota(jnp.int32, (tq, tk), 1) <= lax.broadcasted_iota(jnp.int32, (tq, tk), 0)
                        s = jnp.where(keep, s, -1e30)
                    m_old = m_sc[hh, rows]
                    m_new = jnp.maximum(m_old, jnp.max(s, axis=1, keepdims=True))
                    p = jnp.exp2(s - jnp.tile(m_new, (1, tk // LANES)))
                    alpha = jnp.exp2(m_old - m_new)
                    acc_sc[hh, rows] = jnp.tile(alpha, (1, 2)) * acc_sc[hh, rows] + jnp.dot(p.astype(BF16), vj, preferred_element_type=F32)
                    m_sc[hh, rows] = m_new

        def step(j, carry):
            chunk(j, (False,) * QT)
            return carry

        lax.fori_loop(0, first, step, 0)
        for d in range(QT):
            chunk(first + d, tuple(None if t < d else t == d for t in range(QT)))
        for hh in range(HP):
            cs = slice(hh * V_DIM, (hh + 1) * V_DIM)
            for t in range(QT):
                rows = rows_of(t)
                l = acc_sc[hh, rows, V_DIM:]
                o = acc_sc[hh, rows, :V_DIM] / l
                z = z_ref[rows, cs]
                o_ref[rows, cs] = o
                y_ref[rows, cs] = (o * (z * _sigmoid(z))).astype(BF16)
                lse_ref[hh, t] = (m_sc[hh, rows] + jnp.log2(l)).T[0:1, :]

    pair = pl.BlockSpec((QT * tq, HP * V_DIM), lambda hd, i: (i, hd))
    return pl.pallas_call(
        body, name=name,
        out_shape=(jax.ShapeDtypeStruct((S, D_INNER), F32), jax.ShapeDtypeStruct((S, D_INNER), BF16),
                   jax.ShapeDtypeStruct((H, S // tq, 1, tq), F32)),
        grid=(H // HP, S // (QT * tq)),
        in_specs=[pl.BlockSpec((HP, QT * tq, HEAD_PAD), lambda hd, i: (hd, i, 0)),
                  pl.BlockSpec((HP, S, HEAD_PAD), lambda hd, i: (hd, 0, 0)),
                  pl.BlockSpec((HP, S, HEAD_PAD), lambda hd, i: (hd, 0, 0)),
                  pair],
        out_specs=(pair, pair, pl.BlockSpec((HP, QT, 1, tq), lambda hd, i: (hd, i, 0, 0))),
        scratch_shapes=[pltpu.VMEM((HP, QT * tq, LANES), F32), pltpu.VMEM((HP, QT * tq, HEAD_PAD), F32)],
        compiler_params=_cparams("parallel", "parallel"),
    )(q_full, k_full, v_aug, h)


def _mla_gate_bwd(dy, o, h, *, tm, name):
    S = dy.shape[0]
    E = D_INNER

    def body(dy_ref, o_ref, z_ref, do_ref, dz_ref, delta_ref):
        for hd in range(N_HEADS):
            cs = slice(hd * V_DIM, (hd + 1) * V_DIM)
            z = z_ref[:, cs]
            sg = _sigmoid(z)
            dyv = dy_ref[:, cs]
            ov = o_ref[:, cs]
            do = dyv * (z * sg)
            do_ref[:, cs] = do.astype(BF16)
            dz_ref[:, cs] = (dyv * ov * (sg * (1.0 + z * (1.0 - sg)))).astype(BF16)
            delta_ref[hd, 0] = jnp.broadcast_to(jnp.sum(do * ov, axis=-1, keepdims=True), (tm, LANES)).T[0:1, :]

    row = pl.BlockSpec((tm, E), lambda i: (i, 0))
    return pl.pallas_call(
        body, name=name,
        out_shape=(jax.ShapeDtypeStruct((S, E), BF16), jax.ShapeDtypeStruct((S, E), BF16),
                   jax.ShapeDtypeStruct((N_HEADS, S // tm, 1, tm), F32)),
        grid=(S // tm,),
        in_specs=[row, row, row],
        out_specs=(row, row, pl.BlockSpec((N_HEADS, 1, 1, tm), lambda i: (0, i, 0, 0))),
        compiler_params=_cparams("parallel"),
    )(dy, o, h)


def _flash_bwd(q_full, k_full, v_aug, do, lse_rows, delta_rows, tabs, *, tq, name):
    H, S, _ = q_full.shape
    tk = tq
    nq = S // tq
    HP = 2
    KT = max(t for t in (4, 2, 1) if nq % t == 0)
    rows_of = lambda t: slice(t * tk, (t + 1) * tk)

    def body(q_ref, k_ref, v_ref, do_ref, lse_ref, dl_ref, c_ref, s1_ref, s2_ref, dqp_ref, dkv_ref, dkr_ref, dq_ref, dk_sc, dv_sc):
        first = pl.program_id(1) * KT

        @pl.when(first == 0)
        def _():
            dq_ref[...] = jnp.zeros(dq_ref.shape, F32)

        dk_sc[...] = jnp.zeros(dk_sc.shape, F32)
        dv_sc[...] = jnp.zeros(dv_sc.shape, F32)

        def chunk(qi, masked):
            off = pl.multiple_of(qi * tq, tq)
            for hh in range(HP):
                q = q_ref[hh, pl.ds(off, tq), :]
                dov = do_ref[pl.ds(off, tq), hh * V_DIM:(hh + 1) * V_DIM]
                for t in range(KT):
                    if masked[t] is None:
                        continue
                    rows = rows_of(t)
                    k = k_ref[hh, rows]
                    s_t = lax.dot_general(k, q, (((1,), (1,)), ((), ())), preferred_element_type=F32) * SCORE_TO_LOG2
                    p_t = jnp.exp2(s_t - lse_ref[hh, qi])
                    if masked[t]:
                        keep = lax.broadcasted_iota(jnp.int32, (tk, tq), 0) <= lax.broadcasted_iota(jnp.int32, (tk, tq), 1)
                        p_t = jnp.where(keep, p_t, 0.0)
                    dv_sc[hh, rows] += jnp.dot(p_t.astype(BF16), dov, preferred_element_type=F32)
                    dp_t = lax.dot_general(v_ref[hh, rows], dov, (((1,), (1,)), ((), ())), preferred_element_type=F32)
                    ds = (p_t * (dp_t - dl_ref[hh, qi])).astype(BF16)
                    dk_sc[hh, rows] += jnp.dot(ds, q, preferred_element_type=F32)
                    dq_ref[pl.ds(off, tq), hh * HEAD_PAD:(hh + 1) * HEAD_PAD] += lax.dot_general(
                        ds, k, (((0,), (0,)), ((), ())), preferred_element_type=F32)

        def step(qi, carry):
            chunk(qi, (False,) * KT)
            return carry

        for d in range(KT):
            chunk(first + d, tuple(None if t > d else t == d for t in range(KT)))
            done = pl.ds(pl.multiple_of((first + d) * tq, tq), tq)
            here = rows_of(d)
            for hh in range(HP):
                lo = hh * HEAD_PAD
                dqp_ref[here, lo:lo + QK_NOPE] = (dq_ref[done, lo:lo + QK_NOPE] * ATTN_SCALE).astype(BF16)
                dqp_ref[here, lo + QK_NOPE:lo + HEAD_PAD] = _unrope(dq_ref[done, lo + QK_NOPE:lo + HEAD_PAD] * ATTN_SCALE,
                                                                    c_ref[here, :], s1_ref[here, :], s2_ref[here, :]).astype(BF16)
        lax.fori_loop(first + KT, nq, step, 0)
        for hh in range(HP):
            lo = hh * HEAD_PAD
            dkv_ref[:, lo:lo + QK_NOPE] = (dk_sc[hh, :, :QK_NOPE] * ATTN_SCALE).astype(BF16)
            dkv_ref[:, lo + QK_NOPE:lo + HEAD_PAD] = dv_sc[hh].astype(BF16)
            dkr_ref[hh] = dk_sc[hh, :, QK_NOPE:] * ATTN_SCALE

    tab = pl.BlockSpec((KT * tk, LANES), lambda hd, j: (j, 0))
    pair_rows = pl.BlockSpec((KT * tk, HP * HEAD_PAD), lambda hd, j: (j, hd))
    return pl.pallas_call(
        body, name=name,
        out_shape=(jax.ShapeDtypeStruct((S, H * HEAD_PAD), BF16), jax.ShapeDtypeStruct((S, H * HEAD_PAD), BF16),
                   jax.ShapeDtypeStruct((H, S, LANES), F32)),
        grid=(H // HP, S // (KT * tk)),
        in_specs=[pl.BlockSpec((HP, S, HEAD_PAD), lambda hd, j: (hd, 0, 0)),
                  pl.BlockSpec((HP, KT * tk, HEAD_PAD), lambda hd, j: (hd, j, 0)),
                  pl.BlockSpec((HP, KT * tk, V_DIM), lambda hd, j: (hd, j, 0)),
                  pl.BlockSpec((S, HP * V_DIM), lambda hd, j: (0, hd)),
                  pl.BlockSpec((HP, nq, 1, tq), lambda hd, j: (hd, 0, 0, 0)),
                  pl.BlockSpec((HP, nq, 1, tq), lambda hd, j: (hd, 0, 0, 0)), tab, tab, tab],
        out_specs=(pair_rows, pair_rows, pl.BlockSpec((HP, KT * tk, LANES), lambda hd, j: (hd, j, 0))),
        scratch_shapes=[pltpu.VMEM((S, HP * HEAD_PAD), F32), pltpu.VMEM((HP, KT * tk, HEAD_PAD), F32), pltpu.VMEM((HP, KT * tk, V_DIM), F32)],
        compiler_params=_cparams("parallel", "arbitrary"),
    )(q_full, k_full, v_aug, do, lse_rows, delta_rows, *tabs)


def _mla_latent_bwd(h, dq_n, dkv_n, dkr, dz, q_norm, kv_norm, tabs, *, tm, name):
    S = h.shape[0]

    def body(kv_ref, q_ref, dqn_ref, dkvn_ref, dkr_ref, dz_ref, qg_ref, kvg_ref, c_ref, s1_ref, s2_ref, dh_ref, dqg_ref, dkvg_ref):
        i = pl.program_id(0)
        dq_lat, dqg = _rms_bwd_math(q_ref[...], qg_ref[...], dqn_ref[...])
        dkv_lat, dkvg = _rms_bwd_math(kv_ref[...], kvg_ref[...], dkvn_ref[...])
        dkr_sum = dkr_ref[0]
        for hd in range(1, N_HEADS):
            dkr_sum = dkr_sum + dkr_ref[hd]
        dh_ref[:, :D_INNER] = dz_ref[...]
        dh_ref[:, D_INNER:D_INNER + KV_LORA] = dkv_lat.astype(BF16)
        dh_ref[:, D_INNER + KV_LORA:D_INNER + KV_LORA + Q_LORA] = dq_lat.astype(BF16)
        dh_ref[:, D_INNER + KV_LORA + Q_LORA:] = _unrope(dkr_sum, c_ref[...], s1_ref[...], s2_ref[...]).astype(BF16)

        @pl.when(i == 0)
        def _():
            dqg_ref[...] = dqg
            dkvg_ref[...] = dkvg

        @pl.when(i > 0)
        def _():
            dqg_ref[...] += dqg
            dkvg_ref[...] += dkvg

    tab = pl.BlockSpec((tm, LANES), lambda i: (i, 0))
    qvec = pl.BlockSpec((1, Q_LORA), lambda i: (0, 0))
    kvvec = pl.BlockSpec((1, KV_LORA), lambda i: (0, 0))
    return pl.pallas_call(
        body, name=name,
        out_shape=(jax.ShapeDtypeStruct((S, MLA_IN_PAD), BF16), jax.ShapeDtypeStruct((1, Q_LORA), F32),
                   jax.ShapeDtypeStruct((1, KV_LORA), F32)),
        grid=(S // tm,),
        in_specs=[pl.BlockSpec((tm, KV_LORA), lambda i: (i, KV_LAT_BLK)), pl.BlockSpec((tm, Q_LORA), lambda i: (i, Q_LAT_BLK)),
                  pl.BlockSpec((tm, Q_LORA), lambda i: (i, 0)), pl.BlockSpec((tm, KV_LORA), lambda i: (i, 0)),
                  pl.BlockSpec((N_HEADS, tm, LANES), lambda i: (0, i, 0)), pl.BlockSpec((tm, D_INNER), lambda i: (i, 0)),
                  qvec, kvvec, tab, tab, tab],
        out_specs=(pl.BlockSpec((tm, MLA_IN_PAD), lambda i: (i, 0)), qvec, kvvec),
        compiler_params=_cparams("arbitrary"),
    )(h, h, dq_n, dkv_n, dkr, dz, q_norm, kv_norm, *tabs)


def _adamw(w, g, m, v, *, name):
    R, C = w.shape
    tr = R
    for cand in (512, 256, 128, 64, 32, 16, 8):
        if R % cand == 0 and cand * C * 4 <= 2 * 1024 * 1024:
            tr = cand
            break

    def body(w_ref, g_ref, m_ref, v_ref, d_ref, nm_ref, nv_ref):
        d_ref[...], nm_ref[...], nv_ref[...] = _adam_math(w_ref[...], g_ref[...], m_ref[...], v_ref[...])

    spec = pl.BlockSpec((tr, C), lambda i: (i, 0))
    sds = jax.ShapeDtypeStruct((R, C), F32)
    return pl.pallas_call(
        body, name=name, out_shape=(sds, sds, sds), grid=(R // tr,),
        in_specs=[spec] * 4, out_specs=(spec,) * 3,
        compiler_params=_cparams("parallel"),
    )(w, g, m, v)


def _adamw_rows(w, m, v, srcs, *, name):
    nj, R, C = w.shape
    assert len(srcs) == nj and C % PACK_COLS == 0
    tr = min(R, 512)
    assert R % tr == 0 and all(r0 % tr == 0 for _, r0 in srcs)

    def body(*refs):
        w_ref, m_ref, v_ref = refs[:3]
        g_refs = refs[3:3 + nj]
        go_ref, d_ref, nm_ref, nv_ref = refs[3 + nj:]
        gv = g_refs[0][...]
        for jj in range(1, nj):
            gv = jnp.where(pl.program_id(0) == jj, g_refs[jj][...], gv)
        d, m_new, v_new = _adam_math(w_ref[...], gv, m_ref[...], v_ref[...])
        go_ref[...] = gv
        d_ref[...] = d
        nm_ref[...] = m_new
        nv_ref[...] = v_new

    nat = pl.BlockSpec((None, tr, PACK_COLS), lambda j, cb, i: (j, i, cb))

    def src_spec(jj, r0):
        return pl.BlockSpec((tr, PACK_COLS), lambda j, cb, i: (jnp.where(j == jj, (r0 + cb * R) // tr + i, r0 // tr), 0))

    sds = jax.ShapeDtypeStruct((nj, R, C), F32)
    return pl.pallas_call(
        body, name=name, out_shape=(sds,) * 4, grid=(nj, C // PACK_COLS, R // tr),
        in_specs=[nat] * 3 + [src_spec(jj, r0) for jj, (_, r0) in enumerate(srcs)], out_specs=(nat,) * 4,
        compiler_params=_cparams("parallel", "parallel", "parallel"),
    )(w, m, v, *[rows for rows, _ in srcs])


HBM_SPEC = pl.BlockSpec(memory_space=pltpu.HBM)
VMEM_SPEC = pl.BlockSpec(memory_space=pltpu.VMEM)
SEM_SPEC = pl.BlockSpec(memory_space=pltpu.SEMAPHORE)
ANY_SPEC = pl.BlockSpec(memory_space=pl.ANY)
SPLIT_EFFECT = pltpu.SideEffectType.DATAFLOW_SIDE_EFFECTING


def _place():
    return lax.axis_index("x"), lax.axis_index("y"), lax.axis_index("c")


def _other_chips(x, y):
    return [(1 - x, y), (x, 1 - y), (1 - x, 1 - y)]


def _remote(src, dst, send_sem, recv_sem, dev):
    return pltpu.make_async_remote_copy(src_ref=src, dst_ref=dst, send_sem=send_sem, recv_sem=recv_sem,
                                        device_id=dev, device_id_type=MESH)


def _ag_ici_start(wp, after, *, name):
    R, C = wp.shape
    H = R // 2

    def body(w_ref, land_ref, after_ref, send_sems, recv_sems, w_thru, land_thru, token):
        x, y, c = _place()
        rows = pl.ds(pl.multiple_of(c * H, 16), H)
        for r, chip in enumerate(_other_chips(x, y)):
            _remote(w_ref.at[rows, :], land_ref.at[2 * x + y, rows, :], send_sems.at[r], recv_sems.at[r], (*chip, c)).start()
        token[...] = jnp.zeros(token.shape, F32)

    land = lax.empty((N_CHIPS, R, C), wp.dtype)
    return pl.pallas_call(
        body, name=name,
        out_shape=(pltpu.SemaphoreType.DMA((3,)), pltpu.SemaphoreType.DMA((3,)), pltpu.HBM(wp.shape, wp.dtype), pltpu.HBM(land.shape, land.dtype),
                   jax.ShapeDtypeStruct((8, LANES), F32)),
        in_specs=(HBM_SPEC, HBM_SPEC, ANY_SPEC), out_specs=(SEM_SPEC, SEM_SPEC, HBM_SPEC, HBM_SPEC, VMEM_SPEC),
        input_output_aliases={0: 2, 1: 3},
        compiler_params=pltpu.CompilerParams(has_side_effects=SPLIT_EFFECT),
    )(pltpu.with_memory_space_constraint(wp, pltpu.HBM), pltpu.with_memory_space_constraint(land, pltpu.HBM), after)


def _ag_ici_wait(send_sems, recv_sems, w_thru, land_thru, after, *, name):
    R, C = w_thru.shape
    H = R // 2
    after = after if isinstance(after, (tuple, list)) else (after,)

    def body(w_ref, land_ref, send_sems, recv_sems, *rest):
        x, y, c = _place()
        rows = pl.ds(pl.multiple_of(c * H, 16), H)
        for r, (px, py) in enumerate(_other_chips(x, y)):
            cp = _remote(w_ref.at[rows, :], land_ref.at[2 * px + py, rows, :], send_sems.at[r], recv_sems.at[r], (px, py, c))
            cp.wait_send()
            cp.wait_recv()

    return pl.pallas_call(
        body, name=name,
        out_shape=(pltpu.HBM(w_thru.shape, w_thru.dtype), pltpu.HBM(land_thru.shape, land_thru.dtype)),
        in_specs=(HBM_SPEC, HBM_SPEC, SEM_SPEC, SEM_SPEC) + (ANY_SPEC,) * len(after), out_specs=(HBM_SPEC, HBM_SPEC),
        input_output_aliases={0: 0, 1: 1},
        compiler_params=pltpu.CompilerParams(has_side_effects=SPLIT_EFFECT),
    )(w_thru, land_thru, send_sems, recv_sems, *after)


def _ag_forward_start(land, wp, *, name):
    _, R, C = land.shape
    H = R // 2

    def body(land_ref, w_ref, send_sems, recv_sems, land_thru, w_thru):
        x, y, c = _place()
        sib = (x, y, 1 - c)
        mine = pl.ds(pl.multiple_of(c * H, 16), H)
        for r, (px, py) in enumerate(_other_chips(x, y)):
            _remote(land_ref.at[2 * px + py, mine, :], land_ref.at[2 * px + py, mine, :], send_sems.at[r], recv_sems.at[r], sib).start()
        _remote(w_ref, land_ref.at[2 * x + y], send_sems.at[3], recv_sems.at[3], sib).start()

    return pl.pallas_call(
        body, name=name,
        out_shape=(pltpu.SemaphoreType.DMA((4,)), pltpu.SemaphoreType.DMA((4,)), pltpu.HBM(land.shape, land.dtype), pltpu.HBM(wp.shape, wp.dtype)),
        in_specs=(HBM_SPEC, HBM_SPEC), out_specs=(SEM_SPEC, SEM_SPEC, HBM_SPEC, HBM_SPEC),
        input_output_aliases={0: 2, 1: 3},
        compiler_params=pltpu.CompilerParams(has_side_effects=SPLIT_EFFECT),
    )(land, wp)


def _ag_forward_wait(send_sems, recv_sems, land_thru, w_thru, after, *, name):
    _, R, C = land_thru.shape
    H = R // 2

    def body(land_ref, w_ref, send_sems, recv_sems, after_ref, land_out, w_out):
        x, y, c = _place()
        sib = (x, y, 1 - c)
        mine = pl.ds(pl.multiple_of(c * H, 16), H)
        theirs = pl.ds(pl.multiple_of((1 - c) * H, 16), H)
        for r, (px, py) in enumerate(_other_chips(x, y)):
            cp = _remote(land_ref.at[2 * px + py, mine, :], land_ref.at[2 * px + py, theirs, :], send_sems.at[r], recv_sems.at[r], sib)
            cp.wait_send()
            cp.wait_recv()
        own = _remote(w_ref, land_ref.at[2 * x + y], send_sems.at[3], recv_sems.at[3], sib)
        own.wait_send()
        own.wait_recv()

    return pl.pallas_call(
        body, name=name,
        out_shape=(pltpu.HBM(land_thru.shape, land_thru.dtype), pltpu.HBM(w_thru.shape, w_thru.dtype)),
        in_specs=(HBM_SPEC, HBM_SPEC, SEM_SPEC, SEM_SPEC, ANY_SPEC), out_specs=(HBM_SPEC, HBM_SPEC),
        input_output_aliases={0: 0, 1: 1},
        compiler_params=pltpu.CompilerParams(has_side_effects=SPLIT_EFFECT),
    )(land_thru, w_thru, send_sems, recv_sems, after)[0]


def _rs_sibling_swap(g, *, name):
    _, R, C = g.shape
    H = R // 2

    def body(g_ref, theirs_ref, send_sems, recv_sems):
        x, y, c = _place()
        sib = (x, y, 1 - c)
        copies = [_remote(g_ref.at[k, pl.ds(pl.multiple_of((1 - c) * H, 16), H), :], theirs_ref.at[k],
                          send_sems.at[k], recv_sems.at[k], sib) for k in range(N_CHIPS)]
        for cp in copies:
            cp.start()
        for cp in copies:
            cp.wait()

    return pl.pallas_call(
        body, name=name, out_shape=jax.ShapeDtypeStruct((N_CHIPS, H, C), g.dtype),
        in_specs=[HBM_SPEC], out_specs=HBM_SPEC,
        scratch_shapes=[pltpu.SemaphoreType.DMA((N_CHIPS,)), pltpu.SemaphoreType.DMA((N_CHIPS,))],
    )(g)


def _row_tile(h):
    best = 16
    for d in range(16, 1025, 16):
        if h % d == 0:
            best = d
    return best


def _add2_bf16(g, theirs, core, *, name):
    K, H, C = theirs.shape
    tr = _row_tile(H)
    nb = H // tr

    def body(c_ref, a_ref, b_ref, o_ref):
        o_ref[...] = (a_ref[...].astype(F32) + b_ref[...].astype(F32)).astype(o_ref.dtype)

    spec = pl.BlockSpec((None, tr, C), lambda k, i, c: (k, i, 0))
    return pl.pallas_call(
        body, name=name, out_shape=jax.ShapeDtypeStruct((K, H, C), theirs.dtype),
        grid_spec=pltpu.PrefetchScalarGridSpec(
            num_scalar_prefetch=1, grid=(K, nb),
            in_specs=[pl.BlockSpec((None, tr, C), lambda k, i, c: (k, c[0] * nb + i, 0)), spec], out_specs=spec),
        compiler_params=_cparams("parallel", "parallel"),
    )(core, g, theirs)


def _rs_chip_exchange_start(p, *, name):
    _, H, C = p.shape

    def body(p_ref, land_ref, send_sems, recv_sems, p_thru, land_thru, token):
        x, y, c = _place()
        for r, (px, py) in enumerate(_other_chips(x, y)):
            _remote(p_ref.at[2 * px + py], land_ref.at[r], send_sems.at[r], recv_sems.at[r], (px, py, c)).start()
        token[...] = jnp.zeros(token.shape, F32)

    land = lax.empty((3, H, C), p.dtype)
    return pl.pallas_call(
        body, name=name,
        out_shape=(pltpu.SemaphoreType.DMA((3,)), pltpu.SemaphoreType.DMA((3,)), pltpu.HBM(p.shape, p.dtype), pltpu.HBM(land.shape, land.dtype),
                   jax.ShapeDtypeStruct((8, LANES), F32)),
        in_specs=(HBM_SPEC, HBM_SPEC), out_specs=(SEM_SPEC, SEM_SPEC, HBM_SPEC, HBM_SPEC, VMEM_SPEC),
        input_output_aliases={0: 2, 1: 3},
        compiler_params=pltpu.CompilerParams(has_side_effects=SPLIT_EFFECT),
    )(pltpu.with_memory_space_constraint(p, pltpu.HBM), pltpu.with_memory_space_constraint(land, pltpu.HBM))


def _rs_chip_exchange_wait(send_sems, recv_sems, p_thru, land_thru, after, *, name):
    after = after if isinstance(after, (tuple, list)) else (after,)

    def body(p_ref, land_ref, send_sems, recv_sems, *rest):
        x, y, c = _place()
        for r, (px, py) in enumerate(_other_chips(x, y)):
            cp = _remote(p_ref.at[2 * px + py], land_ref.at[r], send_sems.at[r], recv_sems.at[r], (px, py, c))
            cp.wait_send()
            cp.wait_recv()

    return pl.pallas_call(
        body, name=name,
        out_shape=(pltpu.HBM(p_thru.shape, p_thru.dtype), pltpu.HBM(land_thru.shape, land_thru.dtype)),
        in_specs=(HBM_SPEC, HBM_SPEC, SEM_SPEC, SEM_SPEC) + (ANY_SPEC,) * len(after), out_specs=(HBM_SPEC, HBM_SPEC),
        input_output_aliases={0: 0, 1: 1},
        compiler_params=pltpu.CompilerParams(has_side_effects=SPLIT_EFFECT),
    )(p_thru, land_thru, send_sems, recv_sems, *after)


def _add4_f32(p, recv, chip_core, *, name):
    _, H, C = p.shape
    tr = _row_tile(H)
    nb = H // tr

    def body(s_ref, o_ref, r_ref, out_ref):
        out_ref[...] = ((o_ref[...].astype(F32) + r_ref[0].astype(F32)) + r_ref[1].astype(F32)) + r_ref[2].astype(F32)

    return pl.pallas_call(
        body, name=name, out_shape=jax.ShapeDtypeStruct((2 * H, C), F32),
        grid_spec=pltpu.PrefetchScalarGridSpec(
            num_scalar_prefetch=1, grid=(nb,),
            in_specs=[pl.BlockSpec((None, tr, C), lambda i, s: (s[0], i, 0)), pl.BlockSpec((3, tr, C), lambda i, s: (0, i, 0))],
            out_specs=pl.BlockSpec((tr, C), lambda i, s: (s[1] * nb + i, 0))),
        compiler_params=_cparams("parallel"),
    )(chip_core, p, recv)


def _rs_sibling_join(f, *, name):
    R, C = f.shape
    H = R // 2

    def body(f_ref, out_ref, send_sem, recv_sem):
        x, y, c = _place()
        sib = (x, y, 1 - c)
        mine = pl.ds(pl.multiple_of(c * H, 8), H)
        theirs = pl.ds(pl.multiple_of((1 - c) * H, 8), H)
        cp = _remote(f_ref.at[mine, :], out_ref.at[mine, :], send_sem, recv_sem, sib)
        cp.start()
        _remote(f_ref.at[mine, :], out_ref.at[theirs, :], send_sem, recv_sem, sib).wait_recv()
        cp.wait_send()

    return pl.pallas_call(
        body, name=name, out_shape=jax.ShapeDtypeStruct((R, C), f.dtype),
        in_specs=[HBM_SPEC], out_specs=HBM_SPEC, input_output_aliases={0: 0},
        scratch_shapes=[pltpu.SemaphoreType.DMA, pltpu.SemaphoreType.DMA],
    )(f)


SMALL_GATHER = (("pool_norm", 2, 256), ("pool_scale", 2, 512), ("conv_w", 3, 512), ("mla_norm", 1, 256),
                ("mla_q_norm", 1, 96), ("mla_kv_norm", 1, 64))
SMALL_SLOT = (16, 512)


def _gather_small(shards, *, name):
    def body(pn_ref, ps_ref, cw_ref, mn_ref, qn_ref, kn_ref, pn_o, ps_o, cw_o, mn_o, qn_o, kn_o, all_ref, send_sems, recv_sems):
        x, y, c = _place()
        mine = 2 * x + y
        all_ref[mine] = jnp.zeros(SMALL_SLOT, F32)
        all_ref[mine, 0:2, 0:256] = pn_ref[...]
        all_ref[mine, 2:4, :] = ps_ref[...]
        all_ref[mine, 4:7, :] = cw_ref[0]
        all_ref[mine, 7:8, 0:256] = mn_ref[...]
        all_ref[mine, 8:9, 0:96] = qn_ref[...]
        all_ref[mine, 9:10, 0:64] = kn_ref[...]
        chips = _other_chips(x, y)
        sends = [_remote(all_ref.at[mine], all_ref.at[mine], send_sems.at[r], recv_sems.at[r], (*chip, c)) for r, chip in enumerate(chips)]
        for cp in sends:
            cp.start()
        for r, (px, py) in enumerate(chips):
            _remote(all_ref.at[mine], all_ref.at[2 * px + py], send_sems.at[r], recv_sems.at[r], (px, py, c)).wait_recv()
        for cp in sends:
            cp.wait_send()
        for k in range(N_CHIPS):
            pn_o[:, k * 256:(k + 1) * 256] = all_ref[k, 0:2, 0:256]
            ps_o[:, k * 512:(k + 1) * 512] = all_ref[k, 2:4, :]
            cw_o[0, :, k * 512:(k + 1) * 512] = all_ref[k, 4:7, :]
            mn_o[:, k * 256:(k + 1) * 256] = all_ref[k, 7:8, 0:256]
            qn_o[k] = all_ref[k, 8:9, 0:96]
            kn_o[k] = all_ref[k, 9:10, 0:64]

    sds = lambda *shape: jax.ShapeDtypeStruct(shape, F32)
    out = pl.pallas_call(
        body, name=name,
        out_shape=(sds(2, 1024), sds(2, 2048), sds(1, 3, 2048), sds(1, 1024), sds(N_CHIPS, 1, 96), sds(N_CHIPS, 1, 64)),
        in_specs=[VMEM_SPEC] * 6, out_specs=(VMEM_SPEC,) * 6,
        scratch_shapes=[pltpu.VMEM((N_CHIPS,) + SMALL_SLOT, F32), pltpu.SemaphoreType.DMA((3,)), pltpu.SemaphoreType.DMA((3,))],
    )(*[shards[n] for n, _, _ in SMALL_GATHER])
    full = dict(zip([n for n, _, _ in SMALL_GATHER], out))
    full["mla_q_norm"] = full["mla_q_norm"].reshape(1, Q_LORA)
    full["mla_kv_norm"] = full["mla_kv_norm"].reshape(1, KV_LORA)
    return full


SMALL_REDUCE = (("pool_norm_0", 0, 1, 1024), ("pool_norm_1", 1, 1, 1024), ("pool_scale_0", 2, 1, 2048), ("pool_scale_1", 3, 1, 2048),
                ("conv_norm", 4, 1, 1024), ("mla_norm", 5, 1, 1024), ("mla_q_norm", 6, 1, 384), ("mla_kv_norm", 7, 1, 256),
                ("conv_w", 8, 8, 2048), ("final_norm", 16, 1, 1024))
REDUCE_SLOT = (24, 2048)


def _reduce_small(parts, after, *, name):
    keys = [k for k, _, _, _ in SMALL_REDUCE]

    def body(*refs):
        ins = dict(zip(keys, refs[:len(keys)]))
        pn_o, ps_o, cn_o, cw_o, mn_o, qn_o, kn_o, fn_o, all_ref, send_sems, recv_sems = refs[len(keys) + 1:]
        x, y, c = _place()
        me = 4 * x + 2 * y + c
        all_ref[me] = jnp.zeros(REDUCE_SLOT, F32)
        for k, r0, nr, wd in SMALL_REDUCE:
            all_ref[me, r0:r0 + nr, 0:wd] = ins[k][...]
        peers = []
        for rel in range(1, N_DEV):
            dx, dy, dc = (rel >> 2) & 1, (rel >> 1) & 1, rel & 1
            peers.append((1 - x if dx else x, 1 - y if dy else y, 1 - c if dc else c))
        sends = [_remote(all_ref.at[me], all_ref.at[me], send_sems.at[k], recv_sems.at[k], peer) for k, peer in enumerate(peers)]
        for cp in sends:
            cp.start()
        for k, (px, py, pc) in enumerate(peers):
            _remote(all_ref.at[me], all_ref.at[4 * px + 2 * py + pc], send_sems.at[k], recv_sems.at[k], (px, py, pc)).wait_recv()
        for cp in sends:
            cp.wait_send()

        def total(r0, nr, wd):
            acc = all_ref[0, r0:r0 + nr, 0:wd]
            for d in range(1, N_DEV):
                acc = acc + all_ref[d, r0:r0 + nr, 0:wd]
            return acc

        pn_o[0:1, :] = total(0, 1, 1024)
        pn_o[1:2, :] = total(1, 1, 1024)
        ps_o[0:1, :] = total(2, 1, 2048)
        ps_o[1:2, :] = total(3, 1, 2048)
        cn_o[...] = total(4, 1, 1024)
        mn_o[...] = total(5, 1, 1024)
        qn_o[...] = total(6, 1, Q_LORA)
        kn_o[...] = total(7, 1, KV_LORA)
        cw_o[0] = total(8, 3, 2048)
        fn_o[...] = total(16, 1, 1024)

    sds = lambda *shape: jax.ShapeDtypeStruct(shape, F32)
    out = pl.pallas_call(
        body, name=name,
        out_shape=(sds(2, 1024), sds(2, 2048), sds(1, 1024), sds(1, 3, 2048), sds(1, 1024), sds(1, Q_LORA), sds(1, KV_LORA), sds(1, 1024)),
        in_specs=[VMEM_SPEC] * len(keys) + [ANY_SPEC], out_specs=(VMEM_SPEC,) * 8,
        scratch_shapes=[pltpu.VMEM((N_DEV,) + REDUCE_SLOT, F32), pltpu.SemaphoreType.DMA((N_DEV - 1,)), pltpu.SemaphoreType.DMA((N_DEV - 1,))],
    )(*[parts[k] for k in keys], after)
    return dict(zip(("pool_norm", "pool_scale", "conv_norm", "conv_w", "mla_norm", "mla_q_norm", "mla_kv_norm", "final_norm"), out))


def _adam_math(w, g, m, v):
    m_new = ADAM_B1 * m + (1.0 - ADAM_B1) * g
    v_new = ADAM_B2 * v + (1.0 - ADAM_B2) * (g * g)
    m_hat = m_new / (1.0 - ADAM_B1 ** ADAM_STEP)
    v_hat = v_new / (1.0 - ADAM_B2 ** ADAM_STEP)
    return -ADAM_LR * (m_hat / (jnp.sqrt(v_hat) + ADAM_EPS) + ADAM_WD * w), m_new, v_new


def _adamw_small(w, m, v, g_full, chip, *, name):
    shp = {n: w[n].shape for n in SMALL}
    whole = lambda s: pl.BlockSpec(s, lambda i, c: (0,) * len(s))
    g_in, g_specs = {}, {}
    for n in SMALL:
        if not SMALL_SHARDED[n]:
            g_in[n], g_specs[n] = g_full[n].reshape(shp[n]), whole(shp[n])
        elif shp[n][-1] % LANES:
            g_in[n] = g_full[n].reshape(N_CHIPS, 1, shp[n][-1])
            g_specs[n] = pl.BlockSpec((None,) + shp[n], lambda i, c: (c[0], 0, 0))
        else:
            g_in[n] = g_full[n]
            nd = len(shp[n])
            g_specs[n] = pl.BlockSpec(shp[n], lambda i, c, nd=nd: (0,) * (nd - 1) + (c[0],))

    def body(c_ref, *refs):
        k = len(SMALL)
        w_r, m_r, v_r, g_r = refs[0:k], refs[k:2 * k], refs[2 * k:3 * k], refs[3 * k:4 * k]
        go_r, d_r, nm_r, nv_r = refs[4 * k:5 * k], refs[5 * k:6 * k], refs[6 * k:7 * k], refs[7 * k:8 * k]
        for i in range(k):
            gv = g_r[i][...]
            d, m_new, v_new = _adam_math(w_r[i][...], gv, m_r[i][...], v_r[i][...])
            go_r[i][...] = gv
            d_r[i][...] = d
            nm_r[i][...] = m_new
            nv_r[i][...] = v_new

    nat = [whole(shp[n]) for n in SMALL]
    out_sds = tuple(jax.ShapeDtypeStruct(shp[n], F32) for n in SMALL)
    out = pl.pallas_call(
        body, name=name, out_shape=out_sds * 4,
        grid_spec=pltpu.PrefetchScalarGridSpec(
            num_scalar_prefetch=1, grid=(1,),
            in_specs=nat * 3 + [g_specs[n] for n in SMALL], out_specs=tuple(nat) * 4),
        compiler_params=_cparams("arbitrary"),
    )(chip, *[w[n] for n in SMALL], *[m[n] for n in SMALL], *[v[n] for n in SMALL], *[g_in[n] for n in SMALL])
    k = len(SMALL)
    return tuple(dict(zip(SMALL, out[j * k:(j + 1) * k])) for j in range(4))


BIG = ("pool_w_in", "pool_w_grp", "pool_w_out", "conv_w_in", "conv_w_out", "mla_w_in", "mla_w_q_up", "mla_w_kv_up", "mla_w_out")
BIG_SHARD_AXIS = {"pool_w_in": 2, "pool_w_grp": 2, "pool_w_out": 1, "conv_w_in": 2, "conv_w_out": 1,
                  "mla_w_in": 2, "mla_w_q_up": 2, "mla_w_kv_up": 2, "mla_w_out": 1}
GATHER_LAYOUT = {
    "p0": ((("pool_w_in", 0), 0, "cols"), (("pool_w_out", 0), 1024, "rows"), (("pool_w_grp", 0), 1536, "flat")),
    "cv": ((("conv_w_in", 0), 0, "cols"), (("conv_w_out", 0), 2048, "rows")),
    "ml": ((("pool_w_in", 1), 0, "cols"), (("mla_w_out", 0), 1024, "rows"), (("pool_w_out", 1), 1536, "rows"),
           (("mla_w_kv_up", 0), 2048, "cols"), (("pool_w_grp", 1), 2304, "flat"), (("mla_w_q_up", 0), 2560, "flat"),
           (("mla_w_in", 0), 2848, "flat")),
}
REDUCE_LAYOUT = {
    "late": ((("conv_w_in", 0), 0, "cols"), (("pool_w_in", 1), 2048, "cols"), (("conv_w_out", 0), 3072, "rows"),
             (("mla_w_out", 0), 3584, "rows"), (("pool_w_out", 1), 4096, "rows"), (("mla_w_kv_up", 0), 4608, "cols"),
             (("pool_w_grp", 1), 4864, "flat"), (("mla_w_q_up", 0), 5120, "flat"), (("mla_w_in", 0), 5408, "flat")),
    "first": ((("pool_w_in", 0), 0, "cols"), (("pool_w_out", 0), 1024, "rows"), (("pool_w_grp", 0), 1536, "flat")),
}
PACK_ROW_ALIGN = 32
RS_ROW_ALIGN = 512


def _slot_rows(layout, shard_shape, align):
    where, end = {}, 0
    for piece, r0, kind in layout:
        n = 1
        for d in shard_shape(piece):
            n *= d
        assert r0 >= end and n % PACK_COLS == 0, (piece, r0, end)
        where[piece] = (r0, n // PACK_COLS, kind)
        end = r0 + n // PACK_COLS
    return end + (-end) % align, where


def _as_slot_rows(shard, kind):
    if kind == "cols":
        k, n = shard.shape
        return shard.reshape(k, n // PACK_COLS, PACK_COLS).swapaxes(0, 1).reshape(-1, PACK_COLS)
    return shard.reshape(-1, PACK_COLS)


def _pack_slot(shards, layout, rows, dtype):
    parts, end = [], 0
    for piece, r0, kind in layout:
        if r0 > end:
            parts.append(jnp.zeros((r0 - end, PACK_COLS), dtype))
        parts.append(_as_slot_rows(shards[piece], kind).astype(dtype))
        end = r0 + parts[-1].shape[0]
    if rows > end:
        parts.append(jnp.zeros((rows - end, PACK_COLS), dtype))
    return jnp.concatenate(parts, axis=0)


SMALL = ("pool_norm", "pool_scale", "conv_norm", "conv_w", "mla_norm", "mla_q_norm", "mla_kv_norm", "final_norm")
SMALL_SHARDED = {"pool_norm": True, "pool_scale": True, "conv_norm": False, "conv_w": True, "mla_norm": True,
                 "mla_q_norm": True, "mla_kv_norm": True, "final_norm": False}


def _rope_tables(positions):
    inv_freq = ROPE_BASE ** (-jnp.arange(0, QK_ROPE, 2, dtype=F32) / QK_ROPE)
    ang = positions.astype(F32).reshape(-1, 1) * inv_freq
    cos, sin = jnp.cos(ang), jnp.sin(ang)
    z32 = jnp.zeros_like(cos)
    z64 = jnp.concatenate([z32, z32], axis=1)
    return (jnp.concatenate([cos, cos, z64], axis=1), jnp.concatenate([-sin, z32, z64], axis=1),
            jnp.concatenate([z32, sin, z64], axis=1))


def _mla_in_to_padded(w):
    q, kv, kr, z = w[:, :Q_LORA], w[:, Q_LORA:Q_LORA + KV_LORA], w[:, Q_LORA + KV_LORA:Q_LORA + KV_LORA + QK_ROPE], w[:, Q_LORA + KV_LORA + QK_ROPE:]
    return jnp.concatenate([z, kv, q, kr, jnp.zeros((w.shape[0], MLA_IN_PAD - MLA_IN), w.dtype)], axis=1)


def _mla_in_from_padded(w):
    z, kv, q, kr = w[:, :D_INNER], w[:, D_INNER:D_INNER + KV_LORA], w[:, D_INNER + KV_LORA:D_INNER + KV_LORA + Q_LORA], w[:, D_INNER + KV_LORA + Q_LORA:D_INNER + KV_LORA + Q_LORA + QK_ROPE]
    return jnp.concatenate([q, kv, kr, z], axis=1)


def _q_up_to_padded(w):
    k = w.shape[0]
    return jnp.pad(w.reshape(k, N_HEADS, QK_NOPE + QK_ROPE), ((0, 0), (0, 0), (0, HEAD_PAD - QK_NOPE - QK_ROPE))).reshape(k, N_HEADS * HEAD_PAD)


def _q_up_from_padded(w):
    k = w.shape[0]
    return w.reshape(k, N_HEADS, HEAD_PAD)[:, :, :QK_NOPE + QK_ROPE].reshape(k, N_HEADS * (QK_NOPE + QK_ROPE))


def _local_step(x, positions, target, weights_for, ws, sink):
    S = x.shape[0]
    tm = min(512, S)
    te = min(256, S)
    tq = min(512, S)
    tabs = _rope_tables(positions)
    gs = {}

    def mm_in(xn, w, name):
        n = w.shape[1]
        tn = PACK_COLS if isinstance(w, Packed) else _pick(n, 1536 if n == MLA_IN_PAD else 1024)
        return _mm(xn, w, tm=min(1024, S), tn=tn, tk=D_MODEL, name=name)

    def mm_out(y, w, res, name, after=None):
        return _mm(y, w, residual=res, after=after, tm=tm, tn=D_MODEL, tk=D_INNER, name=name)

    def mm_dx(dy, w, name, after=None):
        k, n = w.shape
        if isinstance(w, Packed):
            tn, tk = (k if w.kind == "rows" else min(k, 1024)), PACK_COLS
        else:
            tn, tk = _pick(k, 1024), _pick(n, 1408)
        return _mm(dy, w, trans_b=True, after=after, tm=min(1024, S), tn=tn, tk=tk, name=name)

    def mm_dw(piece, a, b, name, after=None, post=None):
        ka, nb = a.shape[1], b.shape[1]
        into = sink.dest(piece)
        tokens = min(1024, S)
        if into is None:
            out = _mm(a, b, trans_a=True, out_dtype=BF16, after=after, tm=_pick(ka, 1024), tn=_pick(nb, 1408), tk=tokens, name=name)
            sink.put(piece, out if post is None else post(out))
        else:
            rows = ka if into.kind == "rows" else min(ka, 1024)
            sink.put(piece, _mm(a, b, trans_a=True, after=after, into=into, tm=rows, tn=PACK_COLS, tk=tokens, name=name))

    def pool_layer_fwd(xin, xn, wts, j, tag, then=None):
        h = mm_in(xn, wts[("pool_w_in", j)], f"{tag}_in")
        y = _pool_fwd(h, wts[("pool_w_grp", j)], ws["pool_scale"][j:j + 1], tm=te, name=f"{tag}_mix")
        xo = mm_out(y, wts[("pool_w_out", j)], xin, f"{tag}_out", after=None if then is None else then(y))
        return xo, (xin, xn, h, y)

    def pool_layer_bwd(dx, dxb, saved, wts, j, tag, after=None):
        xin, xn, h, y = saved
        dy = mm_dx(dxb, wts[("pool_w_out", j)], f"{tag}_dy", after)
        mm_dw(("pool_w_out", j), y, dxb, f"{tag}_dwo", after)
        pooled, dmixed, dpooled, dz, dsc = _pool_bwd1(h, dy, wts[("pool_w_grp", j)], ws["pool_scale"][j:j + 1], tm=te, name=f"{tag}_bmix")
        sink.put(("pool_w_grp", j), _grouped_tn(pooled, dmixed, tk=tm, name=f"{tag}_dwg"))
        dh = _pool_bwd2(dpooled, dz, tm=te, name=f"{tag}_bshift")
        dxn = mm_dx(dh, wts[("pool_w_in", j)], f"{tag}_dxn")
        mm_dw(("pool_w_in", j), xn, dh, f"{tag}_dwi")
        dxo, dxob, dg = _rms_bwd(xin, ws["pool_norm"][j:j + 1], dxn, dx, tm=tm, name=f"{tag}_bnorm")
        gs[f"pool_norm_{j}"], gs[f"pool_scale_{j}"] = dg, dsc
        return dxo, dxob

    xn0 = _rms_fwd(x, ws["pool_norm"][0:1], tm=tm, name="p0_norm")
    w_p0 = weights_for("p0", xn0)
    x1, sv0 = pool_layer_fwd(x, xn0, w_p0, 0, "p0", then=lambda y: weights_for.prefetch("cv", y))

    xn1 = _rms_fwd(x1, ws["conv_norm"][0:1], tm=tm, name="cv_norm")
    w_cv = weights_for("cv", xn1)
    h1 = mm_in(xn1, w_cv[("conv_w_in", 0)], "cv_in")
    cw = jnp.pad(ws["conv_w"][0], ((0, 5), (0, 0)))
    y1 = _conv_fwd(h1, cw, tm=te, name="cv_mix")
    x2 = mm_out(y1, w_cv[("conv_w_out", 0)], x1, "cv_out", after=weights_for.prefetch("ml", y1))

    xn2 = _rms_fwd(x2, ws["mla_norm"][0:1], tm=tm, name="ml_norm")
    w_ml = weights_for("ml", xn2)
    w_mi = _mla_in_to_padded(w_ml[("mla_w_in", 0)])
    w_q = _q_up_to_padded(w_ml[("mla_w_q_up", 0)])
    w_kv = w_ml[("mla_w_kv_up", 0)]
    qg, kvg = ws["mla_q_norm"][0:1], ws["mla_kv_norm"][0:1]
    h2 = mm_in(xn2, w_mi, "ml_in")
    q_n, kv_n, krr = _mla_latent_fwd(h2, qg, kvg, tabs, tm=tm, name="ml_lat")
    q_full = _mla_q_up(q_n, w_q, tabs, tm=tm, name="ml_qup")
    k_full, v = _mla_kv_up(kv_n, w_kv, krr, tm=tm, name="ml_kvup")
    o, y2, lse = _flash_fwd(q_full, k_full, v, h2, tq=tq, name="ml_attn")
    x3 = mm_out(y2, w_ml[("mla_w_out", 0)], x2, "ml_out")

    x4, sv3 = pool_layer_fwd(x3, _rms_fwd(x3, ws["pool_norm"][1:2], tm=tm, name="p1_norm"), w_ml, 1, "p1")

    loss_part, dx, dxb, dgf = _final_loss(x4, ws["final_norm"].reshape(1, -1), target, tm=tm, name="final")
    gs["final_norm"] = dgf

    dx, dxb = pool_layer_bwd(dx, dxb, sv3, w_ml, 1, "p1")

    dy = mm_dx(dxb, w_ml[("mla_w_out", 0)], "ml_dy")
    mm_dw(("mla_w_out", 0), y2, dxb, "ml_dwo")
    do, dz, delta = _mla_gate_bwd(dy, o, h2, tm=tq, name="ml_bgate")
    dq_pre, dkv, dkr = _flash_bwd(q_full, k_full, v, do, lse, delta, tabs, tq=tq, name="ml_battn")
    dq_n = mm_dx(dq_pre, w_q, "ml_dqn")
    mm_dw(("mla_w_q_up", 0), q_n, dq_pre, "ml_dwq", post=_q_up_from_padded)
    dkv_n = mm_dx(dkv, w_kv, "ml_dkvn")
    mm_dw(("mla_w_kv_up", 0), kv_n, dkv, "ml_dwkv")
    dh2, dqg, dkvg = _mla_latent_bwd(h2, dq_n, dkv_n, dkr, dz, qg, kvg, tabs, tm=te, name="ml_blat")
    dxn2 = mm_dx(dh2, w_mi, "ml_dxn")
    mm_dw(("mla_w_in", 0), xn2, dh2, "ml_dwi", post=_mla_in_from_padded)
    dx, dxb, dg2 = _rms_bwd(x2, ws["mla_norm"][0:1], dxn2, dx, tm=tm, name="ml_bnorm")
    gs["mla_norm"], gs["mla_q_norm"], gs["mla_kv_norm"] = dg2, dqg, dkvg

    dy = mm_dx(dxb, w_cv[("conv_w_out", 0)], "cv_dy")
    mm_dw(("conv_w_out", 0), y1, dxb, "cv_dwo")
    dh1, dcw = _conv_bwd(h1, dy, cw, tm=te, name="cv_bmix")
    dxn1 = mm_dx(dh1, w_cv[("conv_w_in", 0)], "cv_dxn")
    mm_dw(("conv_w_in", 0), xn1, dh1, "cv_dwi")
    dx, dxb, dg1 = _rms_bwd(x1, ws["conv_norm"][0:1], dxn1, dx, tm=tm, name="cv_bnorm")
    gs["conv_norm"], gs["conv_w"] = dg1, dcw

    dx, dxb = pool_layer_bwd(dx, dxb, sv0, w_p0, 0, "p0", after=sink.late_ready())
    return loss_part, dx, gs


def kernel(x, positions, pool_norm, pool_w_in, pool_w_grp, pool_scale, pool_w_out, conv_norm, conv_w_in, conv_w, conv_w_out, mla_norm, mla_w_in, mla_q_norm, mla_w_q_up, mla_kv_norm, mla_w_kv_up, mla_w_out, final_norm, loss_target, m_pool_norm, m_pool_w_in, m_pool_w_grp, m_pool_scale, m_pool_w_out, m_conv_norm, m_conv_w_in, m_conv_w, m_conv_w_out, m_mla_norm, m_mla_w_in, m_mla_q_norm, m_mla_w_q_up, m_mla_kv_norm, m_mla_w_kv_up, m_mla_w_out, m_final_norm, v_pool_norm, v_pool_w_in, v_pool_w_grp, v_pool_scale, v_pool_w_out, v_conv_norm, v_conv_w_in, v_conv_w, v_conv_w_out, v_mla_norm, v_mla_w_in, v_mla_q_norm, v_mla_w_q_up, v_mla_kv_norm, v_mla_w_kv_up, v_mla_w_out, v_final_norm):
    names = ("pool_norm", "pool_w_in", "pool_w_grp", "pool_scale", "pool_w_out", "conv_norm", "conv_w_in", "conv_w", "conv_w_out",
             "mla_norm", "mla_w_in", "mla_q_norm", "mla_w_q_up", "mla_kv_norm", "mla_w_kv_up", "mla_w_out", "final_norm")
    w = dict(zip(names, (pool_norm, pool_w_in, pool_w_grp, pool_scale, pool_w_out, conv_norm, conv_w_in, conv_w, conv_w_out,
                         mla_norm, mla_w_in, mla_q_norm, mla_w_q_up, mla_kv_norm, mla_w_kv_up, mla_w_out, final_norm)))
    m = dict(zip(names, (m_pool_norm, m_pool_w_in, m_pool_w_grp, m_pool_scale, m_pool_w_out, m_conv_norm, m_conv_w_in, m_conv_w, m_conv_w_out,
                         m_mla_norm, m_mla_w_in, m_mla_q_norm, m_mla_w_q_up, m_mla_kv_norm, m_mla_w_kv_up, m_mla_w_out, m_final_norm)))
    v = dict(zip(names, (v_pool_norm, v_pool_w_in, v_pool_w_grp, v_pool_scale, v_pool_w_out, v_conv_norm, v_conv_w_in, v_conv_w, v_conv_w_out,
                         v_mla_norm, v_mla_w_in, v_mla_q_norm, v_mla_w_q_up, v_mla_kv_norm, v_mla_w_kv_up, v_mla_w_out, v_final_norm)))
    chip = 2 * lax.axis_index("x") + lax.axis_index("y")
    core = lax.axis_index("c")

    core1 = core.astype(jnp.int32).reshape(1)
    chip_core = jnp.stack([chip, core]).astype(jnp.int32)
    shard_shape = lambda piece: w[piece[0]].shape[1:]
    shard_axis = lambda piece: BIG_SHARD_AXIS[piece[0]] - 1
    full_shape = lambda piece: tuple(d * (N_CHIPS if a == shard_axis(piece) else 1) for a, d in enumerate(shard_shape(piece)))

    gather_rows, gather_at, packs = {}, {}, {}
    for grp, layout in GATHER_LAYOUT.items():
        gather_rows[grp], gather_at[grp] = _slot_rows(layout, shard_shape, PACK_ROW_ALIGN)
        packs[grp] = _pack_slot({(n, j): w[n][j] for (n, j), _, _ in layout}, layout, gather_rows[grp], BF16)

    def gathered_weights(grp, gathered):
        out = {}
        for piece, (r0, n, kind) in gather_at[grp].items():
            if kind == "flat":
                out[piece] = jnp.concatenate([gathered[k, r0:r0 + n].reshape(shard_shape(piece)) for k in range(N_CHIPS)], axis=shard_axis(piece))
            else:
                out[piece] = Packed(gathered, r0, kind, full_shape(piece))
        return out

    p0_start = _ag_ici_start(packs["p0"], w["final_norm"], name="ag_p0_start")
    cv_start = _ag_ici_start(packs["cv"], p0_start[4], name="ag_cv_start")
    ml_start = _ag_ici_start(packs["ml"], cv_start[4], name="ag_ml_start")
    in_flight = {"p0": p0_start, "cv": cv_start, "ml": ml_start}

    forwarding = {}

    def prefetch(grp, after):
        if grp == "p0":
            after = (after, ml_start[4])
        send_sems, recv_sems, w_thru, land, _ = in_flight[grp]
        w_thru, land = _ag_ici_wait(send_sems, recv_sems, w_thru, land, after, name=f"ag_{grp}_wait")
        forwarding[grp] = _ag_forward_start(land, w_thru, name=f"ag_{grp}_fwd_start")
        return forwarding[grp][3]

    def weights_for(grp, after):
        if grp not in forwarding:
            prefetch(grp, after)
        return gathered_weights(grp, _ag_forward_wait(*forwarding[grp], after, name=f"ag_{grp}_fwd_wait"))

    weights_for.prefetch = prefetch

    ws = {"conv_norm": w["conv_norm"], "final_norm": w["final_norm"]}
    ws.update(_gather_small({n: w[n] for n, _, _ in SMALL_GATHER}, name="ag_small"))

    reduce_rows, reduce_at = {}, {}
    for grp, layout in REDUCE_LAYOUT.items():
        reduce_rows[grp], reduce_at[grp] = _slot_rows(layout, shard_shape, RS_ROW_ALIGN)
    group_of = {piece: grp for grp, layout in REDUCE_LAYOUT.items() for piece, _, _ in layout}

    class Sink:
        def __init__(self):
            self.buf = {grp: lax.empty((N_CHIPS, rows, PACK_COLS), BF16) for grp, rows in reduce_rows.items()}
            self.started = {}

        def dest(self, piece):
            grp = group_of[piece]
            r0, _, kind = reduce_at[grp][piece]
            return None if kind == "flat" else Packed(self.buf[grp], r0, kind, full_shape(piece))

        def put(self, piece, result):
            grp = group_of[piece]
            r0, n, kind = reduce_at[grp][piece]
            if kind == "flat":
                parts = jnp.split(result, N_CHIPS, axis=shard_axis(piece))
                result = lax.dynamic_update_slice(self.buf[grp], jnp.stack([p.reshape(n, PACK_COLS) for p in parts]), (0, r0, 0))
            self.buf[grp] = result

        def start(self, grp, tag):
            theirs = _rs_sibling_swap(self.buf[grp], name=f"{tag}_swap")
            chip_sum = _add2_bf16(self.buf[grp], theirs, core1, name=f"{tag}_add2")
            self.started[grp] = _rs_chip_exchange_start(chip_sum, name=f"{tag}_chips_start")
            return self.started[grp][4]

        def finish(self, grp, after, tag):
            send_sems, recv_sems, chip_sum, land, _ = self.started[grp]
            chip_sum, recv = _rs_chip_exchange_wait(send_sems, recv_sems, chip_sum, land, after, name=f"{tag}_chips_wait")
            half_sum = _add4_f32(chip_sum, recv, chip_core, name=f"{tag}_add4")
            return _rs_sibling_join(half_sum, name=f"{tag}_join")

        def late_ready(self):
            return self.start("late", "rsa")

    sink = Sink()

    loss_part, grad_x, gs = _local_step(x[0], positions, loss_target[0], weights_for, ws, sink)
    loss = lax.psum(loss_part[0, 0], ("x", "y", "c"))

    g, delta, new_m, new_v = {}, {}, {}, {}

    def adam_big(n):
        nj = w[n].shape[0]
        where = [(group_of[(n, j)],) + reduce_at[group_of[(n, j)]][(n, j)] for j in range(nj)]
        if where[0][3] == "flat":
            g[n] = jnp.stack([g_rows[grp][r0:r0 + rows].reshape(w[n].shape[1:]) for grp, r0, rows, _ in where])
            shp = w[n].shape
            two_d = lambda a: a.reshape(-1, shp[-1])
            d_, m_, v_ = _adamw(two_d(w[n]), two_d(g[n]), two_d(m[n]), two_d(v[n]), name=f"adamw_{n}")
            delta[n], new_m[n], new_v[n] = d_.reshape(shp), m_.reshape(shp), v_.reshape(shp)
        else:
            g[n], delta[n], new_m[n], new_v[n] = _adamw_rows(w[n], m[n], v[n], [(g_rows[grp], r0) for grp, r0, _, _ in where], name=f"adamw_{n}")

    first_token = sink.start("first", "rsb")
    g_rows = {"late": sink.finish("late", first_token, "rsa")}
    late_only = [n for n in BIG if all(group_of[(n, j)] == "late" for j in range(w[n].shape[0]))]
    for n in late_only:
        adam_big(n)
    g_rows["first"] = sink.finish("first", tuple(delta[n] for n in late_only), "rsb")
    for n in BIG:
        if n not in late_only:
            adam_big(n)

    gs_sum = _reduce_small(gs, g_rows["first"], name="ar_small")
    row = lambda d: {n: (d[n].reshape(1, -1) if d[n].ndim == 1 else d[n]) for n in SMALL}
    small_out = _adamw_small(row(w), row(m), row(v), gs_sum, chip.astype(jnp.int32).reshape(1), name="adamw_small")
    for dst, res in zip((g, delta, new_m, new_v), small_out):
        for n in SMALL:
            dst[n] = res[n].reshape(w[n].shape)

    return (loss, grad_x[None], *[g[n] for n in names], *[delta[n] for n in names],
            *[new_m[n] for n in names], *[new_v[n] for n in names])
```

```python
import functools

import jax
import jax.numpy as jnp
from jax import lax
from jax.experimental import pallas as pl
from jax.experimental.pallas import tpu as pltpu

F32 = jnp.float32
BF16 = jnp.bfloat16

D_MODEL = 1024
D_INNER = 2048
POOL_WINDOWS = (2, 4, 8, 16)
POOL_GROUP = 512
N_HEADS = 16
QK_NOPE = 128
QK_ROPE = 64
V_DIM = 128
HEAD_PAD = 256
Q_LORA = 384
KV_LORA = 256
MLA_IN = Q_LORA + KV_LORA + QK_ROPE + D_INNER
MLA_IN_PAD = 2816
ATTN_SCALE = (QK_NOPE + QK_ROPE) ** -0.5
ROPE_BASE = 10000.0
NORM_EPS = 1e-6
HALO = 16

ADAM_LR = 0.001
ADAM_B1 = 0.9
ADAM_B2 = 0.999
ADAM_EPS = 1e-08
ADAM_WD = 0.01
ADAM_STEP = 10

N_CHIPS = 4
N_DEV = 8
LANES = 128
PACK_COLS = 1024
V7X_VMEM_LIMIT = 56 * 1024 * 1024
MESH = pl.DeviceIdType.MESH


def _cparams(*sem):
    return pltpu.CompilerParams(dimension_semantics=sem, vmem_limit_bytes=V7X_VMEM_LIMIT)


def _pick(n, cap):
    best = None
    for d in range(LANES, min(n, cap) + 1, LANES):
        if n % d == 0:
            best = d
    assert best is not None, (n, cap)
    return best


def _sigmoid(z):
    return 1.0 / (1.0 + jnp.exp(-z))


class Packed:
    def __init__(self, buf, r0, kind, shape):
        self.buf, self.r0, self.kind, self.shape = buf, r0, kind, shape

    def block(self, rb, cb, bk):
        assert self.r0 % bk == 0, (self.r0, bk)
        if self.kind == "cols":
            K = self.shape[0]
            per = self.shape[1] // (N_CHIPS * PACK_COLS)
            assert K % bk == 0
            return cb // per, (self.r0 + (cb % per) * K) // bk + rb
        kk = self.shape[0] // N_CHIPS
        assert kk % bk == 0
        per = kk // bk
        return rb // per, self.r0 // bk + rb % per


def _mm(a, b, *, trans_a=False, trans_b=False, out_dtype=F32, residual=None, after=None, into=None, tm, tn, tk, name):
    if trans_a:
        K, M = a.shape
    else:
        M, K = a.shape
    if trans_b:
        N, K2 = b.shape
    else:
        K2, N = b.shape
    assert K == K2 and M % tm == 0 and N % tn == 0 and K % tk == 0, (name, a.shape, b.shape, tm, tn, tk)
    nk = K // tk
    dn = (((0 if trans_a else 1,), (1 if trans_b else 0,)), ((), ()))
    has_res = residual is not None
    n_skip = (after is not None) + (into is not None)
    b_all = isinstance(b, Packed) and b.kind == "rows"
    o_all = into is not None and into.kind == "rows"
    assert not b_all or (not trans_a and (tn == N if trans_b else tk == K)), name
    assert not o_all or tm == M, name

    def body(*refs):
        if has_res:
            a_ref, b_ref, r_ref = refs[:3]
            refs = refs[3:]
        else:
            a_ref, b_ref = refs[:2]
            r_ref = None
            refs = refs[2:]
        refs = refs[n_skip:]
        o_ref, rest = refs[0], refs[1:]
        if b_all and trans_b:
            part = jnp.concatenate([lax.dot_general(a_ref[...], b_ref[c], dn, preferred_element_type=F32) for c in range(N_CHIPS)], axis=1)
        elif b_all:
            kk = K // N_CHIPS
            part = lax.dot_general(a_ref[:, 0:kk], b_ref[0], dn, preferred_element_type=F32)
            for c in range(1, N_CHIPS):
                part = part + lax.dot_general(a_ref[:, c * kk:(c + 1) * kk], b_ref[c], dn, preferred_element_type=F32)
        else:
            part = lax.dot_general(a_ref[...], b_ref[...], dn, preferred_element_type=F32)

        def finish(acc):
            if has_res:
                acc = acc + r_ref[...]
            if o_all:
                mk = M // N_CHIPS
                for c in range(N_CHIPS):
                    o_ref[c] = acc[c * mk:(c + 1) * mk].astype(o_ref.dtype)
            else:
                o_ref[...] = acc.astype(o_ref.dtype)

        if nk == 1:
            finish(part)
        else:
            acc_ref = rest[0]
            k = pl.program_id(2)

            @pl.when(k == 0)
            def _():
                acc_ref[...] = part

            @pl.when(k > 0)
            def _():
                acc_ref[...] += part

            @pl.when(k == nk - 1)
            def _():
                finish(acc_ref[...])

    a_spec = pl.BlockSpec((tk, tm), lambda i, j, k: (k, i)) if trans_a else pl.BlockSpec((tm, tk), lambda i, j, k: (i, k))
    if b_all:
        kkb = b.shape[0] // N_CHIPS
        b_spec = pl.BlockSpec((N_CHIPS, kkb, PACK_COLS), lambda i, j, k: (0, b.r0 // kkb, 0))
        b_arg = b.buf
    elif isinstance(b, Packed):
        if trans_b:
            assert tk == PACK_COLS
            b_spec = pl.BlockSpec((None, tn, tk), lambda i, j, k: (*b.block(j, k, tn), 0))
        else:
            assert tn == PACK_COLS
            b_spec = pl.BlockSpec((None, tk, tn), lambda i, j, k: (*b.block(k, j, tk), 0))
        b_arg = b.buf
    else:
        b_spec = pl.BlockSpec((tn, tk), lambda i, j, k: (j, k)) if trans_b else pl.BlockSpec((tk, tn), lambda i, j, k: (k, j))
        b_arg = b
    o_spec = pl.BlockSpec((tm, tn), lambda i, j, k: (i, j))
    in_specs = [a_spec, b_spec] + ([o_spec] if has_res else [])
    args = (a, b_arg) + ((residual,) if has_res else ())
    aliases = {}
    if after is not None:
        in_specs.append(pl.BlockSpec(memory_space=pl.ANY))
        args += (after,)
    if into is None:
        out_shape, out_spec = jax.ShapeDtypeStruct((M, N), out_dtype), o_spec
    else:
        assert tn == PACK_COLS and into.shape == (M, N)
        in_specs.append(pl.BlockSpec(memory_space=pl.ANY))
        aliases = {len(args): 0}
        args += (into.buf,)
        out_shape = jax.ShapeDtypeStruct(into.buf.shape, into.buf.dtype)
        if o_all:
            out_spec = pl.BlockSpec((N_CHIPS, M // N_CHIPS, tn), lambda i, j, k: (0, into.r0 // (M // N_CHIPS), 0))
        else:
            out_spec = pl.BlockSpec((None, tm, tn), lambda i, j, k: (*into.block(i, j, tm), 0))
    return pl.pallas_call(
        body, name=name, out_shape=out_shape,
        grid=(M // tm, N // tn, nk),
        in_specs=in_specs, out_specs=out_spec, input_output_aliases=aliases,
        scratch_shapes=[pltpu.VMEM((tm, tn), F32)] if nk > 1 else [],
        compiler_params=_cparams("parallel", "parallel", "arbitrary"),
    )(*args)


def _grouped_tn(a, b, *, tk, name):
    S = a.shape[0]
    G = POOL_GROUP
    nk = S // tk

    def body(a_ref, b_ref, o_ref, acc_ref):
        k = pl.program_id(1)
        part = lax.dot_general(a_ref[...], b_ref[...], (((0,), (0,)), ((), ())), preferred_element_type=F32)

        @pl.when(k == 0)
        def _():
            acc_ref[...] = part

        @pl.when(k > 0)
        def _():
            acc_ref[...] += part

        @pl.when(k == nk - 1)
        def _():
            o_ref[...] = acc_ref[...].astype(o_ref.dtype)

    return pl.pallas_call(
        body, name=name,
        out_shape=jax.ShapeDtypeStruct((len(POOL_WINDOWS), G, G), BF16),
        grid=(len(POOL_WINDOWS), nk),
        in_specs=[pl.BlockSpec((tk, G), lambda g, k: (k, g)), pl.BlockSpec((tk, G), lambda g, k: (k, g))],
        out_specs=pl.BlockSpec((None, G, G), lambda g, k: (g, 0, 0)),
        scratch_shapes=[pltpu.VMEM((G, G), F32)],
        compiler_params=_cparams("parallel", "arbitrary"),
    )(a, b)


def _rms_fwd(x, g, *, tm, name):
    S, D = x.shape

    def body(x_ref, g_ref, o_ref):
        xv = x_ref[...]
        rstd = lax.rsqrt(jnp.mean(xv * xv, axis=-1, keepdims=True) + NORM_EPS)
        o_ref[...] = (xv * rstd * g_ref[...]).astype(o_ref.dtype)

    return pl.pallas_call(
        body, name=name,
        out_shape=jax.ShapeDtypeStruct((S, D), BF16),
        grid=(S // tm,),
        in_specs=[pl.BlockSpec((tm, D), lambda i: (i, 0)), pl.BlockSpec((1, D), lambda i: (0, 0))],
        out_specs=pl.BlockSpec((tm, D), lambda i: (i, 0)),
        compiler_params=_cparams("parallel"),
    )(x, g)


def _rms_bwd_math(xv, gv, dxn):
    rstd = lax.rsqrt(jnp.mean(xv * xv, axis=-1, keepdims=True) + NORM_EPS)
    xh = xv * rstd
    dg = jnp.sum(dxn * xh, axis=0, keepdims=True)
    dxh = dxn * gv
    dx = rstd * (dxh - xh * jnp.mean(dxh * xh, axis=-1, keepdims=True))
    return dx, dg


def _rms_bwd(x, g, dxn, dres, *, tm, name):
    S, D = x.shape

    def body(x_ref, g_ref, dxn_ref, dres_ref, dx_ref, dxb_ref, dg_ref):
        dx, dg = _rms_bwd_math(x_ref[...], g_ref[...], dxn_ref[...])
        dx = dx + dres_ref[...]
        dx_ref[...] = dx
        dxb_ref[...] = dx.astype(BF16)

        @pl.when(pl.program_id(0) == 0)
        def _():
            dg_ref[...] = dg

        @pl.when(pl.program_id(0) > 0)
        def _():
            dg_ref[...] += dg

    row = pl.BlockSpec((tm, D), lambda i: (i, 0))
    vec = pl.BlockSpec((1, D), lambda i: (0, 0))
    return pl.pallas_call(
        body, name=name,
        out_shape=(jax.ShapeDtypeStruct((S, D), F32), jax.ShapeDtypeStruct((S, D), BF16), jax.ShapeDtypeStruct((1, D), F32)),
        grid=(S // tm,),
        in_specs=[row, vec, row, row],
        out_specs=(row, row, vec),
        compiler_params=_cparams("arbitrary"),
    )(x, g, dxn, dres)


def _final_loss(x, g, target, *, tm, name):
    S, D = x.shape

    def body(x_ref, g_ref, t_ref, loss_ref, dx_ref, dxb_ref, dg_ref):
        xv = x_ref[...]
        gv = g_ref[...]
        rstd = lax.rsqrt(jnp.mean(xv * xv, axis=-1, keepdims=True) + NORM_EPS)
        xh = xv * rstd
        err = xh * gv - t_ref[...]
        part = 0.5 * jnp.sum(jnp.mean(err * err, axis=-1, keepdims=True), axis=0, keepdims=True)
        dy = err * (1.0 / D)
        dg = jnp.sum(dy * xh, axis=0, keepdims=True)
        dxh = dy * gv
        dx = rstd * (dxh - xh * jnp.mean(dxh * xh, axis=-1, keepdims=True))
        dx_ref[...] = dx
        dxb_ref[...] = dx.astype(BF16)
        lossb = jnp.broadcast_to(part, loss_ref.shape)

        @pl.when(pl.program_id(0) == 0)
        def _():
            dg_ref[...] = dg
            loss_ref[...] = lossb

        @pl.when(pl.program_id(0) > 0)
        def _():
            dg_ref[...] += dg
            loss_ref[...] += lossb

    row = pl.BlockSpec((tm, D), lambda i: (i, 0))
    vec = pl.BlockSpec((1, D), lambda i: (0, 0))
    lspec = pl.BlockSpec((8, LANES), lambda i: (0, 0))
    return pl.pallas_call(
        body, name=name,
        out_shape=(jax.ShapeDtypeStruct((8, LANES), F32), jax.ShapeDtypeStruct((S, D), F32),
                   jax.ShapeDtypeStruct((S, D), BF16), jax.ShapeDtypeStruct((1, D), F32)),
        grid=(S // tm,),
        in_specs=[row, vec, row],
        out_specs=(lspec, row, row, vec),
        compiler_params=_cparams("arbitrary"),
    )(x, g, target)


def _prev_halo_spec(tm, width, col):
    r = tm // HALO
    return pl.BlockSpec((HALO, width), lambda i: (jnp.maximum(i * r - 1, 0), col))


def _next_halo_spec(tm, width, col, S):
    r = tm // HALO
    last = S // HALO - 1
    return pl.BlockSpec((HALO, width), lambda i: (jnp.minimum((i + 1) * r, last), col))


def _shift_down(ext, k):
    return pltpu.roll(ext, k, 0)[HALO:, :]


def _shift_up(ext, k, tm):
    n = ext.shape[0]
    return pltpu.roll(ext, n - k, 0)[:tm, :]


def _pool_window_sum(ext, w):
    s = ext
    k = 1
    while k < w:
        s = s + pltpu.roll(s, k, 0)
        k *= 2
    return s[HALO:, :]


def _pooled_group(u_ref, halo, g, w, t_idx):
    cs = slice(g * POOL_GROUP, (g + 1) * POOL_GROUP)
    u = u_ref[:, cs]
    ext = jnp.concatenate([halo[:, cs], u], axis=0)
    inv = 1.0 / jnp.minimum(t_idx + 1, w).astype(F32)
    return _pool_window_sum(ext, w) * inv - u


def _pool_fwd(h, w_grp, scale, *, tm, name):
    S = h.shape[0]
    E = D_INNER

    def body(u_ref, uh_ref, z_ref, wg_ref, sc_ref, y_ref):
        i = pl.program_id(0)
        halo = jnp.where(i > 0, uh_ref[...], 0.0)
        t_idx = i * tm + lax.broadcasted_iota(jnp.int32, (tm, 1), 0)
        for g, w in enumerate(POOL_WINDOWS):
            cs = slice(g * POOL_GROUP, (g + 1) * POOL_GROUP)
            pooled = _pooled_group(u_ref, halo, g, w, t_idx)
            mixed = jnp.dot(pooled.astype(BF16), wg_ref[g], preferred_element_type=F32)
            z = z_ref[:, cs]
            y_ref[:, cs] = (mixed * sc_ref[:, cs] * (z * _sigmoid(z))).astype(BF16)

    return pl.pallas_call(
        body, name=name,
        out_shape=jax.ShapeDtypeStruct((S, E), BF16),
        grid=(S // tm,),
        in_specs=[pl.BlockSpec((tm, E), lambda i: (i, 0)), _prev_halo_spec(tm, E, 0),
                  pl.BlockSpec((tm, E), lambda i: (i, 1)),
                  pl.BlockSpec((len(POOL_WINDOWS), POOL_GROUP, POOL_GROUP), lambda i: (0, 0, 0)),
                  pl.BlockSpec((1, E), lambda i: (0, 0))],
        out_specs=pl.BlockSpec((tm, E), lambda i: (i, 0)),
        compiler_params=_cparams("parallel"),
    )(h, h, h, w_grp, scale)


def _pool_bwd1(h, dy, w_grp, scale, *, tm, name):
    S = h.shape[0]
    E = D_INNER

    def body(u_ref, uh_ref, z_ref, dy_ref, wg_ref, sc_ref, pooled_ref, dmixed_ref, dpooled_ref, dz_ref, dsc_ref):
        i = pl.program_id(0)
        halo = jnp.where(i > 0, uh_ref[...], 0.0)
        t_idx = i * tm + lax.broadcasted_iota(jnp.int32, (tm, 1), 0)
        for g, w in enumerate(POOL_WINDOWS):
            cs = slice(g * POOL_GROUP, (g + 1) * POOL_GROUP)
            pooled = _pooled_group(u_ref, halo, g, w, t_idx).astype(BF16)
            wg = wg_ref[g]
            mixed = jnp.dot(pooled, wg, preferred_element_type=F32)
            z = z_ref[:, cs]
            sg = _sigmoid(z)
            dyv = dy_ref[:, cs]
            sc = sc_ref[:, cs]
            dms = dyv * (z * sg)
            dz = dyv * (mixed * sc) * (sg * (1.0 + z * (1.0 - sg)))
            dsc = jnp.sum(dms * mixed, axis=0, keepdims=True)
            dmixed = (dms * sc).astype(BF16)
            dpooled = lax.dot_general(dmixed, wg, (((1,), (1,)), ((), ())), preferred_element_type=F32)
            pooled_ref[:, cs] = pooled
            dmixed_ref[:, cs] = dmixed
            dpooled_ref[:, cs] = dpooled
            dz_ref[:, cs] = dz.astype(BF16)

            @pl.when(i == 0)
            def _():
                dsc_ref[:, cs] = dsc

            @pl.when(i > 0)
            def _():
                dsc_ref[:, cs] += dsc

    row = pl.BlockSpec((tm, E), lambda i: (i, 0))
    vec = pl.BlockSpec((1, E), lambda i: (0, 0))
    return pl.pallas_call(
        body, name=name,
        out_shape=(jax.ShapeDtypeStruct((S, E), BF16), jax.ShapeDtypeStruct((S, E), BF16),
                   jax.ShapeDtypeStruct((S, E), F32), jax.ShapeDtypeStruct((S, E), BF16),
                   jax.ShapeDtypeStruct((1, E), F32)),
        grid=(S // tm,),
        in_specs=[row, _prev_halo_spec(tm, E, 0), pl.BlockSpec((tm, E), lambda i: (i, 1)), row,
                  pl.BlockSpec((len(POOL_WINDOWS), POOL_GROUP, POOL_GROUP), lambda i: (0, 0, 0)), vec],
        out_specs=(row, row, row, row, vec),
        compiler_params=_cparams("arbitrary"),
    )(h, h, h, dy, w_grp, scale)


def _pool_bwd2(dpooled, dz, *, tm, name):
    S = dpooled.shape[0]
    E = D_INNER
    nt = S // tm

    def body(dp_ref, dpn_ref, dz_ref, dh_ref):
        i = pl.program_id(0)
        nxt = jnp.where(i < nt - 1, dpn_ref[...], 0.0)
        t_ext = i * tm + lax.broadcasted_iota(jnp.int32, (tm + HALO, 1), 0)
        for g, w in enumerate(POOL_WINDOWS):
            cs = slice(g * POOL_GROUP, (g + 1) * POOL_GROUP)
            dp = dp_ref[:, cs]
            inv = 1.0 / jnp.minimum(t_ext + 1, w).astype(F32)
            s = jnp.concatenate([dp, nxt[:, cs]], axis=0) * inv
            n = tm + HALO
            k = 1
            while k < w:
                s = s + pltpu.roll(s, n - k, 0)
                k *= 2
            dh_ref[:, cs] = (s[:tm, :] - dp).astype(BF16)
        dh_ref[:, E:] = dz_ref[...]

    return pl.pallas_call(
        body, name=name,
        out_shape=jax.ShapeDtypeStruct((S, 2 * E), BF16),
        grid=(nt,),
        in_specs=[pl.BlockSpec((tm, E), lambda i: (i, 0)), _next_halo_spec(tm, E, 0, S),
                  pl.BlockSpec((tm, E), lambda i: (i, 0))],
        out_specs=pl.BlockSpec((tm, 2 * E), lambda i: (i, 0)),
        compiler_params=_cparams("parallel"),
    )(dpooled, dpooled, dz)


CONV_CHUNK = 512


def _conv_fwd(h, cw, *, tm, name):
    S = h.shape[0]
    E = D_INNER

    def body(b_ref, c_ref, hh_ref, z_ref, ch_ref, hhh_ref, w_ref, y_ref):
        i = pl.program_id(0)
        for j in range(E // CONV_CHUNK):
            cs = slice(j * CONV_CHUNK, (j + 1) * CONV_CHUNK)
            p = c_ref[:, cs] * hh_ref[:, cs]
            ph = jnp.where(i > 0, ch_ref[:, cs] * hhh_ref[:, cs], 0.0)
            ext = jnp.concatenate([ph, p], axis=0)
            conv = w_ref[2:3, cs] * p + w_ref[1:2, cs] * _shift_down(ext, 1) + w_ref[0:1, cs] * _shift_down(ext, 2)
            z = z_ref[:, cs]
            y_ref[:, cs] = (b_ref[:, cs] * conv * (z * _sigmoid(z))).astype(BF16)

    col = lambda c: pl.BlockSpec((tm, E), lambda i: (i, c))
    return pl.pallas_call(
        body, name=name,
        out_shape=jax.ShapeDtypeStruct((S, E), BF16),
        grid=(S // tm,),
        in_specs=[col(0), col(1), col(2), col(3), _prev_halo_spec(tm, E, 1), _prev_halo_spec(tm, E, 2),
                  pl.BlockSpec((8, E), lambda i: (0, 0))],
        out_specs=pl.BlockSpec((tm, E), lambda i: (i, 0)),
        compiler_params=_cparams("parallel"),
    )(h, h, h, h, h, h, cw)


def _conv_bwd(h, dy, cw, *, tm, name):
    S = h.shape[0]
    E = D_INNER
    nt = S // tm

    def body(b_ref, c_ref, hh_ref, z_ref, dy_ref, ch_ref, hhh_ref, bn_ref, zn_ref, dyn_ref, w_ref, dh_ref, dw_ref):
        i = pl.program_id(0)
        for j in range(E // CONV_CHUNK):
            cs = slice(j * CONV_CHUNK, (j + 1) * CONV_CHUNK)
            w0, w1, w2 = w_ref[0:1, cs], w_ref[1:2, cs], w_ref[2:3, cs]
            c, hh, b, z, dyv = c_ref[:, cs], hh_ref[:, cs], b_ref[:, cs], z_ref[:, cs], dy_ref[:, cs]
            p = c * hh
            ph = jnp.where(i > 0, ch_ref[:, cs] * hhh_ref[:, cs], 0.0)
            ext = jnp.concatenate([ph, p], axis=0)
            pm1 = _shift_down(ext, 1)
            pm2 = _shift_down(ext, 2)
            conv = w2 * p + w1 * pm1 + w0 * pm2
            sg = _sigmoid(z)
            dy0 = dyv * (z * sg)
            dz = dyv * (b * conv) * (sg * (1.0 + z * (1.0 - sg)))
            db = dy0 * conv
            dconv = dy0 * b
            zn = zn_ref[:, cs]
            dconv_n = jnp.where(i < nt - 1, dyn_ref[:, cs] * (zn * _sigmoid(zn)) * bn_ref[:, cs], 0.0)
            dext = jnp.concatenate([dconv, dconv_n], axis=0)
            dp = w2 * dconv + w1 * _shift_up(dext, 1, tm) + w0 * _shift_up(dext, 2, tm)
            dh_ref[:, 0 * E + j * CONV_CHUNK:0 * E + (j + 1) * CONV_CHUNK] = db.astype(BF16)
            dh_ref[:, 1 * E + j * CONV_CHUNK:1 * E + (j + 1) * CONV_CHUNK] = (dp * hh).astype(BF16)
            dh_ref[:, 2 * E + j * CONV_CHUNK:2 * E + (j + 1) * CONV_CHUNK] = (dp * c).astype(BF16)
            dh_ref[:, 3 * E + j * CONV_CHUNK:3 * E + (j + 1) * CONV_CHUNK] = dz.astype(BF16)
            dw = jnp.concatenate([jnp.sum(dconv * pm2, axis=0, keepdims=True),
                                  jnp.sum(dconv * pm1, axis=0, keepdims=True),
                                  jnp.sum(dconv * p, axis=0, keepdims=True),
                                  jnp.zeros((5, CONV_CHUNK), F32)], axis=0)

            @pl.when(i == 0)
            def _():
                dw_ref[:, cs] = dw

            @pl.when(i > 0)
            def _():
                dw_ref[:, cs] += dw

    col = lambda c: pl.BlockSpec((tm, E), lambda i: (i, c))
    return pl.pallas_call(
        body, name=name,
        out_shape=(jax.ShapeDtypeStruct((S, 4 * E), BF16), jax.ShapeDtypeStruct((8, E), F32)),
        grid=(nt,),
        in_specs=[col(0), col(1), col(2), col(3), pl.BlockSpec((tm, E), lambda i: (i, 0)),
                  _prev_halo_spec(tm, E, 1), _prev_halo_spec(tm, E, 2),
                  _next_halo_spec(tm, E, 0, S), _next_halo_spec(tm, E, 3, S), _next_halo_spec(tm, E, 0, S),
                  pl.BlockSpec((8, E), lambda i: (0, 0))],
        out_specs=(pl.BlockSpec((tm, 4 * E), lambda i: (i, 0)), pl.BlockSpec((8, E), lambda i: (0, 0))),
        compiler_params=_cparams("arbitrary"),
    )(h, h, h, h, dy, h, h, h, h, dy, cw)


Z_COLS = D_INNER // LANES
KV_LAT_BLK = D_INNER // KV_LORA
Q_LAT_BLK = (D_INNER + KV_LORA) // Q_LORA
K_ROPE_BLK = (D_INNER + KV_LORA + Q_LORA) // LANES


def _rope(blk, c, s1, s2):
    return blk * c + pltpu.roll(blk, LANES - QK_ROPE // 2, 1) * s1 + pltpu.roll(blk, QK_ROPE // 2, 1) * s2


def _unrope(blk, c, s1, s2):
    return blk * c - pltpu.roll(blk, LANES - QK_ROPE // 2, 1) * s1 - pltpu.roll(blk, QK_ROPE // 2, 1) * s2


def _lat_norm(v, g):
    rstd = lax.rsqrt(jnp.mean(v * v, axis=-1, keepdims=True) + NORM_EPS)
    return v * rstd * g


def _mla_latent_fwd(h, q_norm, kv_norm, tabs, *, tm, name):
    S = h.shape[0]

    def body(kv_ref, q_ref, kr_ref, qg_ref, kvg_ref, c_ref, s1_ref, s2_ref, qn_ref, kvn_ref, krr_ref):
        qn_ref[...] = _lat_norm(q_ref[...], qg_ref[...]).astype(BF16)
        kvn_ref[...] = _lat_norm(kv_ref[...], kvg_ref[...]).astype(BF16)
        krr_ref[...] = _rope(kr_ref[...], c_ref[...], s1_ref[...], s2_ref[...]).astype(BF16)

    tab = pl.BlockSpec((tm, LANES), lambda i: (i, 0))
    return pl.pallas_call(
        body, name=name,
        out_shape=(jax.ShapeDtypeStruct((S, Q_LORA), BF16), jax.ShapeDtypeStruct((S, KV_LORA), BF16),
                   jax.ShapeDtypeStruct((S, LANES), BF16)),
        grid=(S // tm,),
        in_specs=[pl.BlockSpec((tm, KV_LORA), lambda i: (i, KV_LAT_BLK)), pl.BlockSpec((tm, Q_LORA), lambda i: (i, Q_LAT_BLK)),
                  pl.BlockSpec((tm, LANES), lambda i: (i, K_ROPE_BLK)),
                  pl.BlockSpec((1, Q_LORA), lambda i: (0, 0)), pl.BlockSpec((1, KV_LORA), lambda i: (0, 0)), tab, tab, tab],
        out_specs=(pl.BlockSpec((tm, Q_LORA), lambda i: (i, 0)), pl.BlockSpec((tm, KV_LORA), lambda i: (i, 0)), tab),
        compiler_params=_cparams("parallel"),
    )(h, h, h, q_norm, kv_norm, *tabs)


def _mla_q_up(q_n, w_q_pad, tabs, *, tm, name):
    S = q_n.shape[0]

    def body(a_ref, w_ref, c_ref, s1_ref, s2_ref, o_ref):
        a = a_ref[...]
        for hd in range(N_HEADS):
            acc = jnp.dot(a, w_ref[:, hd * HEAD_PAD:(hd + 1) * HEAD_PAD], preferred_element_type=F32)
            o_ref[hd, :, :QK_NOPE] = acc[:, :QK_NOPE].astype(BF16)
            o_ref[hd, :, QK_NOPE:] = _rope(acc[:, QK_NOPE:], c_ref[...], s1_ref[...], s2_ref[...]).astype(BF16)

    tab = pl.BlockSpec((tm, LANES), lambda i: (i, 0))
    return pl.pallas_call(
        body, name=name,
        out_shape=jax.ShapeDtypeStruct((N_HEADS, S, HEAD_PAD), BF16),
        grid=(S // tm,),
        in_specs=[pl.BlockSpec((tm, Q_LORA), lambda i: (i, 0)), pl.BlockSpec((Q_LORA, N_HEADS * HEAD_PAD), lambda i: (0, 0)),
                  tab, tab, tab],
        out_specs=pl.BlockSpec((N_HEADS, tm, HEAD_PAD), lambda i: (0, i, 0)),
        compiler_params=_cparams("parallel"),
    )(q_n, w_q_pad, *tabs)


def _mla_kv_up(kv_n, w_kv, krr, *, tm, name):
    S = kv_n.shape[0]
    heads_per_chip = N_HEADS // N_CHIPS

    def body(a_ref, w_ref, krr_ref, k_ref, v_ref):
        a = a_ref[...]
        ones = jnp.ones((tm, V_DIM), BF16)
        for hd in range(N_HEADS):
            lo = (hd % heads_per_chip) * HEAD_PAD
            acc = jnp.dot(a, w_ref[hd // heads_per_chip, :, lo:lo + HEAD_PAD], preferred_element_type=F32)
            k_ref[hd, :, :QK_NOPE] = acc[:, :QK_NOPE].astype(BF16)
            k_ref[hd, :, QK_NOPE:] = krr_ref[...]
            v_ref[hd, :, :V_DIM] = acc[:, QK_NOPE:].astype(BF16)
            v_ref[hd, :, V_DIM:] = ones

    head_blk = pl.BlockSpec((N_HEADS, tm, HEAD_PAD), lambda i: (0, i, 0))
    sds = jax.ShapeDtypeStruct((N_HEADS, S, HEAD_PAD), BF16)
    return pl.pallas_call(
        body, name=name, out_shape=(sds, sds),
        grid=(S // tm,),
        in_specs=[pl.BlockSpec((tm, KV_LORA), lambda i: (i, 0)),
                  pl.BlockSpec((N_CHIPS, KV_LORA, PACK_COLS), lambda i: (0, w_kv.r0 // KV_LORA, 0)),
                  pl.BlockSpec((tm, LANES), lambda i: (i, 0))],
        out_specs=(head_blk, head_blk),
        compiler_params=_cparams("parallel"),
    )(kv_n, w_kv.buf, krr)


LOG2E = 1.4426950408889634
SCORE_TO_LOG2 = ATTN_SCALE * LOG2E


def _flash_fwd(q_full, k_full, v_aug, h, *, tq, name):
    H, S, _ = q_full.shape
    tk = tq
    HP = 2
    QT = max(t for t in (4, 2, 1) if (S // tq) % t == 0)
    rows_of = lambda t: slice(t * tq, (t + 1) * tq)

    def body(q_ref, k_ref, v_ref, z_ref, o_ref, y_ref, lse_ref, m_sc, acc_sc):
        first = pl.program_id(1) * QT
        m_sc[...] = jnp.full(m_sc.shape, -1e30, F32)
        acc_sc[...] = jnp.zeros(acc_sc.shape, F32)

        def chunk(j, masked):
            off = pl.multiple_of(j * tk, tk)
            for hh in range(HP):
                kj = k_ref[hh, pl.ds(off, tk), :]
                vj = v_ref[hh, pl.ds(off, tk), :]
                for t in range(QT):
                    if masked[t] is None:
                        continue
                    rows = rows_of(t)
                    s = lax.dot_general(q_ref[hh, rows], kj, (((1,), (1,)), ((), ())), preferred_element_type=F32) * SCORE_TO_LOG2
                    if masked[t]:
                        keep = lax.broadcasted_iota(jnp.int32, (tq, tk), 1) <= lax.broadcasted_iota(jnp.int32, (tq, tk), 0)
                        s = jnp.where(keep, s, -1e30)
                    m_old = m_sc[hh, rows]
                    m_new = jnp.maximum(m_old, jnp.max(s, axis=1, keepdims=True))
                    p = jnp.exp2(s - jnp.tile(m_new, (1, tk // LANES)))
                    alpha = jnp.exp2(m_old - m_new)
                    acc_sc[hh, rows] = jnp.tile(alpha, (1, 2)) * acc_sc[hh, rows] + jnp.dot(p.astype(BF16), vj, preferred_element_type=F32)
                    m_sc[hh, rows] = m_new

        def step(j, carry):
            chunk(j, (False,) * QT)
            return carry

        lax.fori_loop(0, first, step, 0)
        for d in range(QT):
            chunk(first + d, tuple(None if t < d else t == d for t in range(QT)))
        for hh in range(HP):
            cs = slice(hh * V_DIM, (hh + 1) * V_DIM)
            for t in range(QT):
                rows = rows_of(t)
                l = acc_sc[hh, rows, V_DIM:]
                o = acc_sc[hh, rows, :V_DIM] / l
                z = z_ref[rows, cs]
                o_ref[rows, cs] = o
                y_ref[rows, cs] = (o * (z * _sigmoid(z))).astype(BF16)
                lse_ref[hh, t] = (m_sc[hh, rows] + jnp.log2(l)).T[0:1, :]

    pair = pl.BlockSpec((QT * tq, HP * V_DIM), lambda hd, i: (i, hd))
    return pl.pallas_call(
        body, name=name,
        out_shape=(jax.ShapeDtypeStruct((S, D_INNER), F32), jax.ShapeDtypeStruct((S, D_INNER), BF16),
                   jax.ShapeDtypeStruct((H, S // tq, 1, tq), F32)),
        grid=(H // HP, S // (QT * tq)),
        in_specs=[pl.BlockSpec((HP, QT * tq, HEAD_PAD), lambda hd, i: (hd, i, 0)),
                  pl.BlockSpec((HP, S, HEAD_PAD), lambda hd, i: (hd, 0, 0)),
                  pl.BlockSpec((HP, S, HEAD_PAD), lambda hd, i: (hd, 0, 0)),
                  pair],
        out_specs=(pair, pair, pl.BlockSpec((HP, QT, 1, tq), lambda hd, i: (hd, i, 0, 0))),
        scratch_shapes=[pltpu.VMEM((HP, QT * tq, LANES), F32), pltpu.VMEM((HP, QT * tq, HEAD_PAD), F32)],
        compiler_params=_cparams("parallel", "parallel"),
    )(q_full, k_full, v_aug, h)


def _mla_gate_bwd(dy, o, h, *, tm, name):
    S = dy.shape[0]
    E = D_INNER

    def body(dy_ref, o_ref, z_ref, do_ref, dz_ref, delta_ref):
        for hd in range(N_HEADS):
            cs = slice(hd * V_DIM, (hd + 1) * V_DIM)
            z = z_ref[:, cs]
            sg = _sigmoid(z)
            dyv = dy_ref[:, cs]
            ov = o_ref[:, cs]
            do = dyv * (z * sg)
            do_ref[:, cs] = do.astype(BF16)
            dz_ref[:, cs] = (dyv * ov * (sg * (1.0 + z * (1.0 - sg)))).astype(BF16)
            delta_ref[hd, 0] = jnp.broadcast_to(jnp.sum(do * ov, axis=-1, keepdims=True), (tm, LANES)).T[0:1, :]

    row = pl.BlockSpec((tm, E), lambda i: (i, 0))
    return pl.pallas_call(
        body, name=name,
        out_shape=(jax.ShapeDtypeStruct((S, E), BF16), jax.ShapeDtypeStruct((S, E), BF16),
                   jax.ShapeDtypeStruct((N_HEADS, S // tm, 1, tm), F32)),
        grid=(S // tm,),
        in_specs=[row, row, row],
        out_specs=(row, row, pl.BlockSpec((N_HEADS, 1, 1, tm), lambda i: (0, i, 0, 0))),
        compiler_params=_cparams("parallel"),
    )(dy, o, h)


def _flash_bwd(q_full, k_full, v_aug, do, lse_rows, delta_rows, tabs, *, tq, name):
    H, S, _ = q_full.shape
    tk = tq
    nq = S // tq
    HP = 2
    KT = max(t for t in (4, 2, 1) if nq % t == 0)
    rows_of = lambda t: slice(t * tk, (t + 1) * tk)

    def body(q_ref, k_ref, v_ref, do_ref, lse_ref, dl_ref, c_ref, s1_ref, s2_ref, dqp_ref, dkv_ref, dkr_ref, dq_ref, dk_sc, dv_sc):
        first = pl.program_id(1) * KT

        @pl.when(first == 0)
        def _():
            dq_ref[...] = jnp.zeros(dq_ref.shape, F32)

        dk_sc[...] = jnp.zeros(dk_sc.shape, F32)
        dv_sc[...] = jnp.zeros(dv_sc.shape, F32)

        def chunk(qi, masked):
            off = pl.multiple_of(qi * tq, tq)
            for hh in range(HP):
                q = q_ref[hh, pl.ds(off, tq), :]
                dov = do_ref[pl.ds(off, tq), hh * V_DIM:(hh + 1) * V_DIM]
                for t in range(KT):
                    if masked[t] is None:
                        continue
                    rows = rows_of(t)
                    k = k_ref[hh, rows]
                    s_t = lax.dot_general(k, q, (((1,), (1,)), ((), ())), preferred_element_type=F32) * SCORE_TO_LOG2
                    p_t = jnp.exp2(s_t - lse_ref[hh, qi])
                    if masked[t]:
                        keep = lax.broadcasted_iota(jnp.int32, (tk, tq), 0) <= lax.broadcasted_iota(jnp.int32, (tk, tq), 1)
                        p_t = jnp.where(keep, p_t, 0.0)
                    dv_sc[hh, rows] += jnp.dot(p_t.astype(BF16), dov, preferred_element_type=F32)
                    dp_t = lax.dot_general(v_ref[hh, rows], dov, (((1,), (1,)), ((), ())), preferred_element_type=F32)
                    ds = (p_t * (dp_t - dl_ref[hh, qi])).astype(BF16)
                    dk_sc[hh, rows] += jnp.dot(ds, q, preferred_element_type=F32)
                    dq_ref[pl.ds(off, tq), hh * HEAD_PAD:(hh + 1) * HEAD_PAD] += lax.dot_general(
                        ds, k, (((0,), (0,)), ((), ())), preferred_element_type=F32)

        def step(qi, carry):
            chunk(qi, (False,) * KT)
            return carry

        for d in range(KT):
            chunk(first + d, tuple(None if t > d else t == d for t in range(KT)))
            done = pl.ds(pl.multiple_of((first + d) * tq, tq), tq)
            here = rows_of(d)
            for hh in range(HP):
                lo = hh * HEAD_PAD
                dqp_ref[here, lo:lo + QK_NOPE] = (dq_ref[done, lo:lo + QK_NOPE] * ATTN_SCALE).astype(BF16)
                dqp_ref[here, lo + QK_NOPE:lo + HEAD_PAD] = _unrope(dq_ref[done, lo + QK_NOPE:lo + HEAD_PAD] * ATTN_SCALE,
                                                                    c_ref[here, :], s1_ref[here, :], s2_ref[here, :]).astype(BF16)
        lax.fori_loop(first + KT, nq, step, 0)
        for hh in range(HP):
            lo = hh * HEAD_PAD
            dkv_ref[:, lo:lo + QK_NOPE] = (dk_sc[hh, :, :QK_NOPE] * ATTN_SCALE).astype(BF16)
            dkv_ref[:, lo + QK_NOPE:lo + HEAD_PAD] = dv_sc[hh].astype(BF16)
            dkr_ref[hh] = dk_sc[hh, :, QK_NOPE:] * ATTN_SCALE

    tab = pl.BlockSpec((KT * tk, LANES), lambda hd, j: (j, 0))
    pair_rows = pl.BlockSpec((KT * tk, HP * HEAD_PAD), lambda hd, j: (j, hd))
    return pl.pallas_call(
        body, name=name,
        out_shape=(jax.ShapeDtypeStruct((S, H * HEAD_PAD), BF16), jax.ShapeDtypeStruct((S, H * HEAD_PAD), BF16),
                   jax.ShapeDtypeStruct((H, S, LANES), F32)),
        grid=(H // HP, S // (KT * tk)),
        in_specs=[pl.BlockSpec((HP, S, HEAD_PAD), lambda hd, j: (hd, 0, 0)),
                  pl.BlockSpec((HP, KT * tk, HEAD_PAD), lambda hd, j: (hd, j, 0)),
                  pl.BlockSpec((HP, KT * tk, V_DIM), lambda hd, j: (hd, j, 0)),
                  pl.BlockSpec((S, HP * V_DIM), lambda hd, j: (0, hd)),
                  pl.BlockSpec((HP, nq, 1, tq), lambda hd, j: (hd, 0, 0, 0)),
                  pl.BlockSpec((HP, nq, 1, tq), lambda hd, j: (hd, 0, 0, 0)), tab, tab, tab],
        out_specs=(pair_rows, pair_rows, pl.BlockSpec((HP, KT * tk, LANES), lambda hd, j: (hd, j, 0))),
        scratch_shapes=[pltpu.VMEM((S, HP * HEAD_PAD), F32), pltpu.VMEM((HP, KT * tk, HEAD_PAD), F32), pltpu.VMEM((HP, KT * tk, V_DIM), F32)],
        compiler_params=_cparams("parallel", "arbitrary"),
    )(q_full, k_full, v_aug, do, lse_rows, delta_rows, *tabs)


def _mla_latent_bwd(h, dq_n, dkv_n, dkr, dz, q_norm, kv_norm, tabs, *, tm, name):
    S = h.shape[0]

    def body(kv_ref, q_ref, dqn_ref, dkvn_ref, dkr_ref, dz_ref, qg_ref, kvg_ref, c_ref, s1_ref, s2_ref, dh_ref, dqg_ref, dkvg_ref):
        i = pl.program_id(0)
        dq_lat, dqg = _rms_bwd_math(q_ref[...], qg_ref[...], dqn_ref[...])
        dkv_lat, dkvg = _rms_bwd_math(kv_ref[...], kvg_ref[...], dkvn_ref[...])
        dkr_sum = dkr_ref[0]
        for hd in range(1, N_HEADS):
            dkr_sum = dkr_sum + dkr_ref[hd]
        dh_ref[:, :D_INNER] = dz_ref[...]
        dh_ref[:, D_INNER:D_INNER + KV_LORA] = dkv_lat.astype(BF16)
        dh_ref[:, D_INNER + KV_LORA:D_INNER + KV_LORA + Q_LORA] = dq_lat.astype(BF16)
        dh_ref[:, D_INNER + KV_LORA + Q_LORA:] = _unrope(dkr_sum, c_ref[...], s1_ref[...], s2_ref[...]).astype(BF16)

        @pl.when(i == 0)
        def _():
            dqg_ref[...] = dqg
            dkvg_ref[...] = dkvg

        @pl.when(i > 0)
        def _():
            dqg_ref[...] += dqg
            dkvg_ref[...] += dkvg

    tab = pl.BlockSpec((tm, LANES), lambda i: (i, 0))
    qvec = pl.BlockSpec((1, Q_LORA), lambda i: (0, 0))
    kvvec = pl.BlockSpec((1, KV_LORA), lambda i: (0, 0))
    return pl.pallas_call(
        body, name=name,
        out_shape=(jax.ShapeDtypeStruct((S, MLA_IN_PAD), BF16), jax.ShapeDtypeStruct((1, Q_LORA), F32),
                   jax.ShapeDtypeStruct((1, KV_LORA), F32)),
        grid=(S // tm,),
        in_specs=[pl.BlockSpec((tm, KV_LORA), lambda i: (i, KV_LAT_BLK)), pl.BlockSpec((tm, Q_LORA), lambda i: (i, Q_LAT_BLK)),
                  pl.BlockSpec((tm, Q_LORA), lambda i: (i, 0)), pl.BlockSpec((tm, KV_LORA), lambda i: (i, 0)),
                  pl.BlockSpec((N_HEADS, tm, LANES), lambda i: (0, i, 0)), pl.BlockSpec((tm, D_INNER), lambda i: (i, 0)),
                  qvec, kvvec, tab, tab, tab],
        out_specs=(pl.BlockSpec((tm, MLA_IN_PAD), lambda i: (i, 0)), qvec, kvvec),
        compiler_params=_cparams("arbitrary"),
    )(h, h, dq_n, dkv_n, dkr, dz, q_norm, kv_norm, *tabs)


def _adamw(w, g, m, v, *, name):
    R, C = w.shape
    tr = R
    for cand in (512, 256, 128, 64, 32, 16, 8):
        if R % cand == 0 and cand * C * 4 <= 2 * 1024 * 1024:
            tr = cand
            break

    def body(w_ref, g_ref, m_ref, v_ref, d_ref, nm_ref, nv_ref):
        d_ref[...], nm_ref[...], nv_ref[...] = _adam_math(w_ref[...], g_ref[...], m_ref[...], v_ref[...])

    spec = pl.BlockSpec((tr, C), lambda i: (i, 0))
    sds = jax.ShapeDtypeStruct((R, C), F32)
    return pl.pallas_call(
        body, name=name, out_shape=(sds, sds, sds), grid=(R // tr,),
        in_specs=[spec] * 4, out_specs=(spec,) * 3,
        compiler_params=_cparams("parallel"),
    )(w, g, m, v)


def _adamw_rows(w, m, v, srcs, *, name):
    nj, R, C = w.shape
    assert len(srcs) == nj and C % PACK_COLS == 0
    tr = min(R, 512)
    assert R % tr == 0 and all(r0 % tr == 0 for _, r0 in srcs)

    def body(*refs):
        w_ref, m_ref, v_ref = refs[:3]
        g_refs = refs[3:3 + nj]
        go_ref, d_ref, nm_ref, nv_ref = refs[3 + nj:]
        gv = g_refs[0][...]
        for jj in range(1, nj):
            gv = jnp.where(pl.program_id(0) == jj, g_refs[jj][...], gv)
        d, m_new, v_new = _adam_math(w_ref[...], gv, m_ref[...], v_ref[...])
        go_ref[...] = gv
        d_ref[...] = d
        nm_ref[...] = m_new
        nv_ref[...] = v_new

    nat = pl.BlockSpec((None, tr, PACK_COLS), lambda j, cb, i: (j, i, cb))

    def src_spec(jj, r0):
        return pl.BlockSpec((tr, PACK_COLS), lambda j, cb, i: (jnp.where(j == jj, (r0 + cb * R) // tr + i, r0 // tr), 0))

    sds = jax.ShapeDtypeStruct((nj, R, C), F32)
    return pl.pallas_call(
        body, name=name, out_shape=(sds,) * 4, grid=(nj, C // PACK_COLS, R // tr),
        in_specs=[nat] * 3 + [src_spec(jj, r0) for jj, (_, r0) in enumerate(srcs)], out_specs=(nat,) * 4,
        compiler_params=_cparams("parallel", "parallel", "parallel"),
    )(w, m, v, *[rows for rows, _ in srcs])


HBM_SPEC = pl.BlockSpec(memory_space=pltpu.HBM)
VMEM_SPEC = pl.BlockSpec(memory_space=pltpu.VMEM)
SEM_SPEC = pl.BlockSpec(memory_space=pltpu.SEMAPHORE)
ANY_SPEC = pl.BlockSpec(memory_space=pl.ANY)
SPLIT_EFFECT = pltpu.SideEffectType.DATAFLOW_SIDE_EFFECTING


def _place():
    return lax.axis_index("x"), lax.axis_index("y"), lax.axis_index("c")


def _other_chips(x, y):
    return [(1 - x, y), (x, 1 - y), (1 - x, 1 - y)]


def _remote(src, dst, send_sem, recv_sem, dev):
    return pltpu.make_async_remote_copy(src_ref=src, dst_ref=dst, send_sem=send_sem, recv_sem=recv_sem,
                                        device_id=dev, device_id_type=MESH)


def _ag_ici_start(wp, after, *, name):
    R, C = wp.shape
    H = R // 2

    def body(w_ref, land_ref, after_ref, send_sems, recv_sems, w_thru, land_thru, token):
        x, y, c = _place()
        rows = pl.ds(pl.multiple_of(c * H, 16), H)
        for r, chip in enumerate(_other_chips(x, y)):
            _remote(w_ref.at[rows, :], land_ref.at[2 * x + y, rows, :], send_sems.at[r], recv_sems.at[r], (*chip, c)).start()
        token[...] = jnp.zeros(token.shape, F32)

    land = lax.empty((N_CHIPS, R, C), wp.dtype)
    return pl.pallas_call(
        body, name=name,
        out_shape=(pltpu.SemaphoreType.DMA((3,)), pltpu.SemaphoreType.DMA((3,)), pltpu.HBM(wp.shape, wp.dtype), pltpu.HBM(land.shape, land.dtype),
                   jax.ShapeDtypeStruct((8, LANES), F32)),
        in_specs=(HBM_SPEC, HBM_SPEC, ANY_SPEC), out_specs=(SEM_SPEC, SEM_SPEC, HBM_SPEC, HBM_SPEC, VMEM_SPEC),
        input_output_aliases={0: 2, 1: 3},
        compiler_params=pltpu.CompilerParams(has_side_effects=SPLIT_EFFECT),
    )(pltpu.with_memory_space_constraint(wp, pltpu.HBM), pltpu.with_memory_space_constraint(land, pltpu.HBM), after)


def _ag_ici_wait(send_sems, recv_sems, w_thru, land_thru, after, *, name):
    R, C = w_thru.shape
    H = R // 2
    after = after if isinstance(after, (tuple, list)) else (after,)

    def body(w_ref, land_ref, send_sems, recv_sems, *rest):
        x, y, c = _place()
        rows = pl.ds(pl.multiple_of(c * H, 16), H)
        for r, (px, py) in enumerate(_other_chips(x, y)):
            cp = _remote(w_ref.at[rows, :], land_ref.at[2 * px + py, rows, :], send_sems.at[r], recv_sems.at[r], (px, py, c))
            cp.wait_send()
            cp.wait_recv()

    return pl.pallas_call(
        body, name=name,
        out_shape=(pltpu.HBM(w_thru.shape, w_thru.dtype), pltpu.HBM(land_thru.shape, land_thru.dtype)),
        in_specs=(HBM_SPEC, HBM_SPEC, SEM_SPEC, SEM_SPEC) + (ANY_SPEC,) * len(after), out_specs=(HBM_SPEC, HBM_SPEC),
        input_output_aliases={0: 0, 1: 1},
        compiler_params=pltpu.CompilerParams(has_side_effects=SPLIT_EFFECT),
    )(w_thru, land_thru, send_sems, recv_sems, *after)


def _ag_forward_start(land, wp, *, name):
    _, R, C = land.shape
    H = R // 2

    def body(land_ref, w_ref, send_sems, recv_sems, land_thru, w_thru):
        x, y, c = _place()
        sib = (x, y, 1 - c)
        mine = pl.ds(pl.multiple_of(c * H, 16), H)
        for r, (px, py) in enumerate(_other_chips(x, y)):
            _remote(land_ref.at[2 * px + py, mine, :], land_ref.at[2 * px + py, mine, :], send_sems.at[r], recv_sems.at[r], sib).start()
        _remote(w_ref, land_ref.at[2 * x + y], send_sems.at[3], recv_sems.at[3], sib).start()

    return pl.pallas_call(
        body, name=name,
        out_shape=(pltpu.SemaphoreType.DMA((4,)), pltpu.SemaphoreType.DMA((4,)), pltpu.HBM(land.shape, land.dtype), pltpu.HBM(wp.shape, wp.dtype)),
        in_specs=(HBM_SPEC, HBM_SPEC), out_specs=(SEM_SPEC, SEM_SPEC, HBM_SPEC, HBM_SPEC),
        input_output_aliases={0: 2, 1: 3},
        compiler_params=pltpu.CompilerParams(has_side_effects=SPLIT_EFFECT),
    )(land, wp)


def _ag_forward_wait(send_sems, recv_sems, land_thru, w_thru, after, *, name):
    _, R, C = land_thru.shape
    H = R // 2

    def body(land_ref, w_ref, send_sems, recv_sems, after_ref, land_out, w_out):
        x, y, c = _place()
        sib = (x, y, 1 - c)
        mine = pl.ds(pl.multiple_of(c * H, 16), H)
        theirs = pl.ds(pl.multiple_of((1 - c) * H, 16), H)
        for r, (px, py) in enumerate(_other_chips(x, y)):
            cp = _remote(land_ref.at[2 * px + py, mine, :], land_ref.at[2 * px + py, theirs, :], send_sems.at[r], recv_sems.at[r], sib)
            cp.wait_send()
            cp.wait_recv()
        own = _remote(w_ref, land_ref.at[2 * x + y], send_sems.at[3], recv_sems.at[3], sib)
        own.wait_send()
        own.wait_recv()

    return pl.pallas_call(
        body, name=name,
        out_shape=(pltpu.HBM(land_thru.shape, land_thru.dtype), pltpu.HBM(w_thru.shape, w_thru.dtype)),
        in_specs=(HBM_SPEC, HBM_SPEC, SEM_SPEC, SEM_SPEC, ANY_SPEC), out_specs=(HBM_SPEC, HBM_SPEC),
        input_output_aliases={0: 0, 1: 1},
        compiler_params=pltpu.CompilerParams(has_side_effects=SPLIT_EFFECT),
    )(land_thru, w_thru, send_sems, recv_sems, after)[0]


def _rs_sibling_swap(g, *, name):
    _, R, C = g.shape
    H = R // 2

    def body(g_ref, theirs_ref, send_sems, recv_sems):
        x, y, c = _place()
        sib = (x, y, 1 - c)
        copies = [_remote(g_ref.at[k, pl.ds(pl.multiple_of((1 - c) * H, 16), H), :], theirs_ref.at[k],
                          send_sems.at[k], recv_sems.at[k], sib) for k in range(N_CHIPS)]
        for cp in copies:
            cp.start()
        for cp in copies:
            cp.wait()

    return pl.pallas_call(
        body, name=name, out_shape=jax.ShapeDtypeStruct((N_CHIPS, H, C), g.dtype),
        in_specs=[HBM_SPEC], out_specs=HBM_SPEC,
        scratch_shapes=[pltpu.SemaphoreType.DMA((N_CHIPS,)), pltpu.SemaphoreType.DMA((N_CHIPS,))],
    )(g)


def _row_tile(h):
    best = 16
    for d in range(16, 1025, 16):
        if h % d == 0:
            best = d
    return best


def _add2_bf16(g, theirs, core, *, name):
    K, H, C = theirs.shape
    tr = _row_tile(H)
    nb = H // tr

    def body(c_ref, a_ref, b_ref, o_ref):
        o_ref[...] = (a_ref[...].astype(F32) + b_ref[...].astype(F32)).astype(o_ref.dtype)

    spec = pl.BlockSpec((None, tr, C), lambda k, i, c: (k, i, 0))
    return pl.pallas_call(
        body, name=name, out_shape=jax.ShapeDtypeStruct((K, H, C), theirs.dtype),
        grid_spec=pltpu.PrefetchScalarGridSpec(
            num_scalar_prefetch=1, grid=(K, nb),
            in_specs=[pl.BlockSpec((None, tr, C), lambda k, i, c: (k, c[0] * nb + i, 0)), spec], out_specs=spec),
        compiler_params=_cparams("parallel", "parallel"),
    )(core, g, theirs)


def _rs_chip_exchange_start(p, *, name):
    _, H, C = p.shape

    def body(p_ref, land_ref, send_sems, recv_sems, p_thru, land_thru, token):
        x, y, c = _place()
        for r, (px, py) in enumerate(_other_chips(x, y)):
            _remote(p_ref.at[2 * px + py], land_ref.at[r], send_sems.at[r], recv_sems.at[r], (px, py, c)).start()
        token[...] = jnp.zeros(token.shape, F32)

    land = lax.empty((3, H, C), p.dtype)
    return pl.pallas_call(
        body, name=name,
        out_shape=(pltpu.SemaphoreType.DMA((3,)), pltpu.SemaphoreType.DMA((3,)), pltpu.HBM(p.shape, p.dtype), pltpu.HBM(land.shape, land.dtype),
                   jax.ShapeDtypeStruct((8, LANES), F32)),
        in_specs=(HBM_SPEC, HBM_SPEC), out_specs=(SEM_SPEC, SEM_SPEC, HBM_SPEC, HBM_SPEC, VMEM_SPEC),
        input_output_aliases={0: 2, 1: 3},
        compiler_params=pltpu.CompilerParams(has_side_effects=SPLIT_EFFECT),
    )(pltpu.with_memory_space_constraint(p, pltpu.HBM), pltpu.with_memory_space_constraint(land, pltpu.HBM))


def _rs_chip_exchange_wait(send_sems, recv_sems, p_thru, land_thru, after, *, name):
    after = after if isinstance(after, (tuple, list)) else (after,)

    def body(p_ref, land_ref, send_sems, recv_sems, *rest):
        x, y, c = _place()
        for r, (px, py) in enumerate(_other_chips(x, y)):
            cp = _remote(p_ref.at[2 * px + py], land_ref.at[r], send_sems.at[r], recv_sems.at[r], (px, py, c))
            cp.wait_send()
            cp.wait_recv()

    return pl.pallas_call(
        body, name=name,
        out_shape=(pltpu.HBM(p_thru.shape, p_thru.dtype), pltpu.HBM(land_thru.shape, land_thru.dtype)),
        in_specs=(HBM_SPEC, HBM_SPEC, SEM_SPEC, SEM_SPEC) + (ANY_SPEC,) * len(after), out_specs=(HBM_SPEC, HBM_SPEC),
        input_output_aliases={0: 0, 1: 1},
        compiler_params=pltpu.CompilerParams(has_side_effects=SPLIT_EFFECT),
    )(p_thru, land_thru, send_sems, recv_sems, *after)


def _add4_f32(p, recv, chip_core, *, name):
    _, H, C = p.shape
    tr = _row_tile(H)
    nb = H // tr

    def body(s_ref, o_ref, r_ref, out_ref):
        out_ref[...] = ((o_ref[...].astype(F32) + r_ref[0].astype(F32)) + r_ref[1].astype(F32)) + r_ref[2].astype(F32)

    return pl.pallas_call(
        body, name=name, out_shape=jax.ShapeDtypeStruct((2 * H, C), F32),
        grid_spec=pltpu.PrefetchScalarGridSpec(
            num_scalar_prefetch=1, grid=(nb,),
            in_specs=[pl.BlockSpec((None, tr, C), lambda i, s: (s[0], i, 0)), pl.BlockSpec((3, tr, C), lambda i, s: (0, i, 0))],
            out_specs=pl.BlockSpec((tr, C), lambda i, s: (s[1] * nb + i, 0))),
        compiler_params=_cparams("parallel"),
    )(chip_core, p, recv)


def _rs_sibling_join(f, *, name):
    R, C = f.shape
    H = R // 2

    def body(f_ref, out_ref, send_sem, recv_sem):
        x, y, c = _place()
        sib = (x, y, 1 - c)
        mine = pl.ds(pl.multiple_of(c * H, 8), H)
        theirs = pl.ds(pl.multiple_of((1 - c) * H, 8), H)
        cp = _remote(f_ref.at[mine, :], out_ref.at[mine, :], send_sem, recv_sem, sib)
        cp.start()
        _remote(f_ref.at[mine, :], out_ref.at[theirs, :], send_sem, recv_sem, sib).wait_recv()
        cp.wait_send()

    return pl.pallas_call(
        body, name=name, out_shape=jax.ShapeDtypeStruct((R, C), f.dtype),
        in_specs=[HBM_SPEC], out_specs=HBM_SPEC, input_output_aliases={0: 0},
        scratch_shapes=[pltpu.SemaphoreType.DMA, pltpu.SemaphoreType.DMA],
    )(f)


SMALL_GATHER = (("pool_norm", 2, 256), ("pool_scale", 2, 512), ("conv_w", 3, 512), ("mla_norm", 1, 256),
                ("mla_q_norm", 1, 96), ("mla_kv_norm", 1, 64))
SMALL_SLOT = (16, 512)


def _gather_small(shards, *, name):
    def body(pn_ref, ps_ref, cw_ref, mn_ref, qn_ref, kn_ref, pn_o, ps_o, cw_o, mn_o, qn_o, kn_o, all_ref, send_sems, recv_sems):
        x, y, c = _place()
        mine = 2 * x + y
        all_ref[mine] = jnp.zeros(SMALL_SLOT, F32)
        all_ref[mine, 0:2, 0:256] = pn_ref[...]
        all_ref[mine, 2:4, :] = ps_ref[...]
        all_ref[mine, 4:7, :] = cw_ref[0]
        all_ref[mine, 7:8, 0:256] = mn_ref[...]
        all_ref[mine, 8:9, 0:96] = qn_ref[...]
        all_ref[mine, 9:10, 0:64] = kn_ref[...]
        chips = _other_chips(x, y)
        sends = [_remote(all_ref.at[mine], all_ref.at[mine], send_sems.at[r], recv_sems.at[r], (*chip, c)) for r, chip in enumerate(chips)]
        for cp in sends:
            cp.start()
        for r, (px, py) in enumerate(chips):
            _remote(all_ref.at[mine], all_ref.at[2 * px + py], send_sems.at[r], recv_sems.at[r], (px, py, c)).wait_recv()
        for cp in sends:
            cp.wait_send()
        for k in range(N_CHIPS):
            pn_o[:, k * 256:(k + 1) * 256] = all_ref[k, 0:2, 0:256]
            ps_o[:, k * 512:(k + 1) * 512] = all_ref[k, 2:4, :]
            cw_o[0, :, k * 512:(k + 1) * 512] = all_ref[k, 4:7, :]
            mn_o[:, k * 256:(k + 1) * 256] = all_ref[k, 7:8, 0:256]
            qn_o[k] = all_ref[k, 8:9, 0:96]
            kn_o[k] = all_ref[k, 9:10, 0:64]

    sds = lambda *shape: jax.ShapeDtypeStruct(shape, F32)
    out = pl.pallas_call(
        body, name=name,
        out_shape=(sds(2, 1024), sds(2, 2048), sds(1, 3, 2048), sds(1, 1024), sds(N_CHIPS, 1, 96), sds(N_CHIPS, 1, 64)),
        in_specs=[VMEM_SPEC] * 6, out_specs=(VMEM_SPEC,) * 6,
        scratch_shapes=[pltpu.VMEM((N_CHIPS,) + SMALL_SLOT, F32), pltpu.SemaphoreType.DMA((3,)), pltpu.SemaphoreType.DMA((3,))],
    )(*[shards[n] for n, _, _ in SMALL_GATHER])
    full = dict(zip([n for n, _, _ in SMALL_GATHER], out))
    full["mla_q_norm"] = full["mla_q_norm"].reshape(1, Q_LORA)
    full["mla_kv_norm"] = full["mla_kv_norm"].reshape(1, KV_LORA)
    return full


SMALL_REDUCE = (("pool_norm_0", 0, 1, 1024), ("pool_norm_1", 1, 1, 1024), ("pool_scale_0", 2, 1, 2048), ("pool_scale_1", 3, 1, 2048),
                ("conv_norm", 4, 1, 1024), ("mla_norm", 5, 1, 1024), ("mla_q_norm", 6, 1, 384), ("mla_kv_norm", 7, 1, 256),
                ("conv_w", 8, 8, 2048), ("final_norm", 16, 1, 1024))
REDUCE_SLOT = (24, 2048)


def _reduce_small(parts, after, *, name):
    keys = [k for k, _, _, _ in SMALL_REDUCE]

    def body(*refs):
        ins = dict(zip(keys, refs[:len(keys)]))
        pn_o, ps_o, cn_o, cw_o, mn_o, qn_o, kn_o, fn_o, all_ref, send_sems, recv_sems = refs[len(keys) + 1:]
        x, y, c = _place()
        me = 4 * x + 2 * y + c
        all_ref[me] = jnp.zeros(REDUCE_SLOT, F32)
        for k, r0, nr, wd in SMALL_REDUCE:
            all_ref[me, r0:r0 + nr, 0:wd] = ins[k][...]
        peers = []
        for rel in range(1, N_DEV):
            dx, dy, dc = (rel >> 2) & 1, (rel >> 1) & 1, rel & 1
            peers.append((1 - x if dx else x, 1 - y if dy else y, 1 - c if dc else c))
        sends = [_remote(all_ref.at[me], all_ref.at[me], send_sems.at[k], recv_sems.at[k], peer) for k, peer in enumerate(peers)]
        for cp in sends:
            cp.start()
        for k, (px, py, pc) in enumerate(peers):
            _remote(all_ref.at[me], all_ref.at[4 * px + 2 * py + pc], send_sems.at[k], recv_sems.at[k], (px, py, pc)).wait_recv()
        for cp in sends:
            cp.wait_send()

        def total(r0, nr, wd):
            acc = all_ref[0, r0:r0 + nr, 0:wd]
            for d in range(1, N_DEV):
                acc = acc + all_ref[d, r0:r0 + nr, 0:wd]
            return acc

        pn_o[0:1, :] = total(0, 1, 1024)
        pn_o[1:2, :] = total(1, 1, 1024)
        ps_o[0:1, :] = total(2, 1, 2048)
        ps_o[1:2, :] = total(3, 1, 2048)
        cn_o[...] = total(4, 1, 1024)
        mn_o[...] = total(5, 1, 1024)
        qn_o[...] = total(6, 1, Q_LORA)
        kn_o[...] = total(7, 1, KV_LORA)
        cw_o[0] = total(8, 3, 2048)
        fn_o[...] = total(16, 1, 1024)

    sds = lambda *shape: jax.ShapeDtypeStruct(shape, F32)
    out = pl.pallas_call(
        body, name=name,
        out_shape=(sds(2, 1024), sds(2, 2048), sds(1, 1024), sds(1, 3, 2048), sds(1, 1024), sds(1, Q_LORA), sds(1, KV_LORA), sds(1, 1024)),
        in_specs=[VMEM_SPEC] * len(keys) + [ANY_SPEC], out_specs=(VMEM_SPEC,) * 8,
        scratch_shapes=[pltpu.VMEM((N_DEV,) + REDUCE_SLOT, F32), pltpu.SemaphoreType.DMA((N_DEV - 1,)), pltpu.SemaphoreType.DMA((N_DEV - 1,))],
    )(*[parts[k] for k in keys], after)
    return dict(zip(("pool_norm", "pool_scale", "conv_norm", "conv_w", "mla_norm", "mla_q_norm", "mla_kv_norm", "final_norm"), out))


def _adam_math(w, g, m, v):
    m_new = ADAM_B1 * m + (1.0 - ADAM_B1) * g
    v_new = ADAM_B2 * v + (1.0 - ADAM_B2) * (g * g)
    m_hat = m_new / (1.0 - ADAM_B1 ** ADAM_STEP)
    v_hat = v_new / (1.0 - ADAM_B2 ** ADAM_STEP)
    return -ADAM_LR * (m_hat / (jnp.sqrt(v_hat) + ADAM_EPS) + ADAM_WD * w), m_new, v_new


def _adamw_small(w, m, v, g_full, chip, *, name):
    shp = {n: w[n].shape for n in SMALL}
    whole = lambda s: pl.BlockSpec(s, lambda i, c: (0,) * len(s))
    g_in, g_specs = {}, {}
    for n in SMALL:
        if not SMALL_SHARDED[n]:
            g_in[n], g_specs[n] = g_full[n].reshape(shp[n]), whole(shp[n])
        elif shp[n][-1] % LANES:
            g_in[n] = g_full[n].reshape(N_CHIPS, 1, shp[n][-1])
            g_specs[n] = pl.BlockSpec((None,) + shp[n], lambda i, c: (c[0], 0, 0))
        else:
            g_in[n] = g_full[n]
            nd = len(shp[n])
            g_specs[n] = pl.BlockSpec(shp[n], lambda i, c, nd=nd: (0,) * (nd - 1) + (c[0],))

    def body(c_ref, *refs):
        k = len(SMALL)
        w_r, m_r, v_r, g_r = refs[0:k], refs[k:2 * k], refs[2 * k:3 * k], refs[3 * k:4 * k]
        go_r, d_r, nm_r, nv_r = refs[4 * k:5 * k], refs[5 * k:6 * k], refs[6 * k:7 * k], refs[7 * k:8 * k]
        for i in range(k):
            gv = g_r[i][...]
            d, m_new, v_new = _adam_math(w_r[i][...], gv, m_r[i][...], v_r[i][...])
            go_r[i][...] = gv
            d_r[i][...] = d
            nm_r[i][...] = m_new
            nv_r[i][...] = v_new

    nat = [whole(shp[n]) for n in SMALL]
    out_sds = tuple(jax.ShapeDtypeStruct(shp[n], F32) for n in SMALL)
    out = pl.pallas_call(
        body, name=name, out_shape=out_sds * 4,
        grid_spec=pltpu.PrefetchScalarGridSpec(
            num_scalar_prefetch=1, grid=(1,),
            in_specs=nat * 3 + [g_specs[n] for n in SMALL], out_specs=tuple(nat) * 4),
        compiler_params=_cparams("arbitrary"),
    )(chip, *[w[n] for n in SMALL], *[m[n] for n in SMALL], *[v[n] for n in SMALL], *[g_in[n] for n in SMALL])
    k = len(SMALL)
    return tuple(dict(zip(SMALL, out[j * k:(j + 1) * k])) for j in range(4))


BIG = ("pool_w_in", "pool_w_grp", "pool_w_out", "conv_w_in", "conv_w_out", "mla_w_in", "mla_w_q_up", "mla_w_kv_up", "mla_w_out")
BIG_SHARD_AXIS = {"pool_w_in": 2, "pool_w_grp": 2, "pool_w_out": 1, "conv_w_in": 2, "conv_w_out": 1,
                  "mla_w_in": 2, "mla_w_q_up": 2, "mla_w_kv_up": 2, "mla_w_out": 1}
GATHER_LAYOUT = {
    "p0": ((("pool_w_in", 0), 0, "cols"),),
    "p0b": ((("pool_w_out", 0), 0, "rows"), (("pool_w_grp", 0), 512, "flat")),
    "cv": ((("conv_w_in", 0), 0, "cols"), (("conv_w_out", 0), 2048, "rows")),
    "ml": ((("pool_w_in", 1), 0, "cols"), (("mla_w_out", 0), 1024, "rows"), (("pool_w_out", 1), 1536, "rows"),
           (("mla_w_kv_up", 0), 2048, "cols"), (("pool_w_grp", 1), 2304, "flat"), (("mla_w_q_up", 0), 2560, "flat"),
           (("mla_w_in", 0), 2848, "flat")),
}
REDUCE_LAYOUT = {
    "late": ((("conv_w_in", 0), 0, "cols"), (("pool_w_in", 1), 2048, "cols"), (("conv_w_out", 0), 3072, "rows"),
             (("mla_w_out", 0), 3584, "rows"), (("pool_w_out", 1), 4096, "rows"), (("mla_w_kv_up", 0), 4608, "cols"),
             (("pool_w_grp", 1), 4864, "flat"), (("mla_w_q_up", 0), 5120, "flat"), (("mla_w_in", 0), 5408, "flat")),
    "first": ((("pool_w_in", 0), 0, "cols"), (("pool_w_out", 0), 1024, "rows"), (("pool_w_grp", 0), 1536, "flat")),
}
PACK_ROW_ALIGN = 32
RS_ROW_ALIGN = 512


def _slot_rows(layout, shard_shape, align):
    where, end = {}, 0
    for piece, r0, kind in layout:
        n = 1
        for d in shard_shape(piece):
            n *= d
        assert r0 >= end and n % PACK_COLS == 0, (piece, r0, end)
        where[piece] = (r0, n // PACK_COLS, kind)
        end = r0 + n // PACK_COLS
    return end + (-end) % align, where


def _as_slot_rows(shard, kind):
    if kind == "cols":
        k, n = shard.shape
        return shard.reshape(k, n // PACK_COLS, PACK_COLS).swapaxes(0, 1).reshape(-1, PACK_COLS)
    return shard.reshape(-1, PACK_COLS)


def _pack_slot(shards, layout, rows, dtype):
    parts, end = [], 0
    for piece, r0, kind in layout:
        if r0 > end:
            parts.append(jnp.zeros((r0 - end, PACK_COLS), dtype))
        parts.append(_as_slot_rows(shards[piece], kind).astype(dtype))
        end = r0 + parts[-1].shape[0]
    if rows > end:
        parts.append(jnp.zeros((rows - end, PACK_COLS), dtype))
    return jnp.concatenate(parts, axis=0)


SMALL = ("pool_norm", "pool_scale", "conv_norm", "conv_w", "mla_norm", "mla_q_norm", "mla_kv_norm", "final_norm")
SMALL_SHARDED = {"pool_norm": True, "pool_scale": True, "conv_norm": False, "conv_w": True, "mla_norm": True,
                 "mla_q_norm": True, "mla_kv_norm": True, "final_norm": False}


def _rope_tables(positions):
    inv_freq = ROPE_BASE ** (-jnp.arange(0, QK_ROPE, 2, dtype=F32) / QK_ROPE)
    ang = positions.astype(F32).reshape(-1, 1) * inv_freq
    cos, sin = jnp.cos(ang), jnp.sin(ang)
    z32 = jnp.zeros_like(cos)
    z64 = jnp.concatenate([z32, z32], axis=1)
    return (jnp.concatenate([cos, cos, z64], axis=1), jnp.concatenate([-sin, z32, z64], axis=1),
            jnp.concatenate([z32, sin, z64], axis=1))


def _mla_in_to_padded(w):
    q, kv, kr, z = w[:, :Q_LORA], w[:, Q_LORA:Q_LORA + KV_LORA], w[:, Q_LORA + KV_LORA:Q_LORA + KV_LORA + QK_ROPE], w[:, Q_LORA + KV_LORA + QK_ROPE:]
    return jnp.concatenate([z, kv, q, kr, jnp.zeros((w.shape[0], MLA_IN_PAD - MLA_IN), w.dtype)], axis=1)


def _mla_in_from_padded(w):
    z, kv, q, kr = w[:, :D_INNER], w[:, D_INNER:D_INNER + KV_LORA], w[:, D_INNER + KV_LORA:D_INNER + KV_LORA + Q_LORA], w[:, D_INNER + KV_LORA + Q_LORA:D_INNER + KV_LORA + Q_LORA + QK_ROPE]
    return jnp.concatenate([q, kv, kr, z], axis=1)


def _q_up_to_padded(w):
    k = w.shape[0]
    return jnp.pad(w.reshape(k, N_HEADS, QK_NOPE + QK_ROPE), ((0, 0), (0, 0), (0, HEAD_PAD - QK_NOPE - QK_ROPE))).reshape(k, N_HEADS * HEAD_PAD)


def _q_up_from_padded(w):
    k = w.shape[0]
    return w.reshape(k, N_HEADS, HEAD_PAD)[:, :, :QK_NOPE + QK_ROPE].reshape(k, N_HEADS * (QK_NOPE + QK_ROPE))


def _local_step(x, positions, target, weights_for, ws, sink):
    S = x.shape[0]
    tm = min(512, S)
    te = min(256, S)
    tq = min(512, S)
    tabs = _rope_tables(positions)
    gs = {}

    def mm_in(xn, w, name):
        n = w.shape[1]
        tn = PACK_COLS if isinstance(w, Packed) else _pick(n, 1536 if n == MLA_IN_PAD else 1024)
        return _mm(xn, w, tm=min(1024, S), tn=tn, tk=D_MODEL, name=name)

    def mm_out(y, w, res, name, after=None):
        return _mm(y, w, residual=res, after=after, tm=tm, tn=D_MODEL, tk=D_INNER, name=name)

    def mm_dx(dy, w, name, after=None):
        k, n = w.shape
        if isinstance(w, Packed):
            tn, tk = (k if w.kind == "rows" else min(k, 1024)), PACK_COLS
        else:
            tn, tk = _pick(k, 1024), _pick(n, 1408)
        return _mm(dy, w, trans_b=True, after=after, tm=min(1024, S), tn=tn, tk=tk, name=name)

    def mm_dw(piece, a, b, name, after=None, post=None):
        ka, nb = a.shape[1], b.shape[1]
        into = sink.dest(piece)
        tokens = min(1024, S)
        if into is None:
            out = _mm(a, b, trans_a=True, out_dtype=BF16, after=after, tm=_pick(ka, 1024), tn=_pick(nb, 1408), tk=tokens, name=name)
            sink.put(piece, out if post is None else post(out))
        else:
            rows = ka if into.kind == "rows" else min(ka, 1024)
            sink.put(piece, _mm(a, b, trans_a=True, after=after, into=into, tm=rows, tn=PACK_COLS, tk=tokens, name=name))

    def pool_layer_fwd(xin, xn, wts, j, tag, then=None, rest=None):
        h = mm_in(xn, wts[("pool_w_in", j)], f"{tag}_in")
        if rest is not None:
            wts.update(rest(h))
        y = _pool_fwd(h, wts[("pool_w_grp", j)], ws["pool_scale"][j:j + 1], tm=te, name=f"{tag}_mix")
        xo = mm_out(y, wts[("pool_w_out", j)], xin, f"{tag}_out", after=None if then is None else then(y))
        return xo, (xin, xn, h, y)

    def pool_layer_bwd(dx, dxb, saved, wts, j, tag, after=None):
        xin, xn, h, y = saved
        dy = mm_dx(dxb, wts[("pool_w_out", j)], f"{tag}_dy", after)
        mm_dw(("pool_w_out", j), y, dxb, f"{tag}_dwo", after)
        pooled, dmixed, dpooled, dz, dsc = _pool_bwd1(h, dy, wts[("pool_w_grp", j)], ws["pool_scale"][j:j + 1], tm=te, name=f"{tag}_bmix")
        sink.put(("pool_w_grp", j), _grouped_tn(pooled, dmixed, tk=tm, name=f"{tag}_dwg"))
        dh = _pool_bwd2(dpooled, dz, tm=te, name=f"{tag}_bshift")
        dxn = mm_dx(dh, wts[("pool_w_in", j)], f"{tag}_dxn")
        mm_dw(("pool_w_in", j), xn, dh, f"{tag}_dwi")
        dxo, dxob, dg = _rms_bwd(xin, ws["pool_norm"][j:j + 1], dxn, dx, tm=tm, name=f"{tag}_bnorm")
        gs[f"pool_norm_{j}"], gs[f"pool_scale_{j}"] = dg, dsc
        return dxo, dxob

    xn0 = _rms_fwd(x, ws["pool_norm"][0:1], tm=tm, name="p0_norm")
    w_p0 = weights_for("p0", xn0)
    x1, sv0 = pool_layer_fwd(x, xn0, w_p0, 0, "p0", then=lambda y: weights_for.prefetch("cv", y), rest=lambda h: weights_for("p0b", h))

    xn1 = _rms_fwd(x1, ws["conv_norm"][0:1], tm=tm, name="cv_norm")
    w_cv = weights_for("cv", xn1)
    h1 = mm_in(xn1, w_cv[("conv_w_in", 0)], "cv_in")
    cw = jnp.pad(ws["conv_w"][0], ((0, 5), (0, 0)))
    y1 = _conv_fwd(h1, cw, tm=te, name="cv_mix")
    x2 = mm_out(y1, w_cv[("conv_w_out", 0)], x1, "cv_out", after=weights_for.prefetch("ml", y1))

    xn2 = _rms_fwd(x2, ws["mla_norm"][0:1], tm=tm, name="ml_norm")
    w_ml = weights_for("ml", xn2)
    w_mi = _mla_in_to_padded(w_ml[("mla_w_in", 0)])
    w_q = _q_up_to_padded(w_ml[("mla_w_q_up", 0)])
    w_kv = w_ml[("mla_w_kv_up", 0)]
    qg, kvg = ws["mla_q_norm"][0:1], ws["mla_kv_norm"][0:1]
    h2 = mm_in(xn2, w_mi, "ml_in")
    q_n, kv_n, krr = _mla_latent_fwd(h2, qg, kvg, tabs, tm=tm, name="ml_lat")
    q_full = _mla_q_up(q_n, w_q, tabs, tm=tm, name="ml_qup")
    k_full, v = _mla_kv_up(kv_n, w_kv, krr, tm=tm, name="ml_kvup")
    o, y2, lse = _flash_fwd(q_full, k_full, v, h2, tq=tq, name="ml_attn")
    x3 = mm_out(y2, w_ml[("mla_w_out", 0)], x2, "ml_out")

    x4, sv3 = pool_layer_fwd(x3, _rms_fwd(x3, ws["pool_norm"][1:2], tm=tm, name="p1_norm"), w_ml, 1, "p1")

    loss_part, dx, dxb, dgf = _final_loss(x4, ws["final_norm"].reshape(1, -1), target, tm=tm, name="final")
    gs["final_norm"] = dgf

    dx, dxb = pool_layer_bwd(dx, dxb, sv3, w_ml, 1, "p1")

    dy = mm_dx(dxb, w_ml[("mla_w_out", 0)], "ml_dy")
    mm_dw(("mla_w_out", 0), y2, dxb, "ml_dwo")
    do, dz, delta = _mla_gate_bwd(dy, o, h2, tm=tq, name="ml_bgate")
    dq_pre, dkv, dkr = _flash_bwd(q_full, k_full, v, do, lse, delta, tabs, tq=tq, name="ml_battn")
    dq_n = mm_dx(dq_pre, w_q, "ml_dqn")
    mm_dw(("mla_w_q_up", 0), q_n, dq_pre, "ml_dwq", post=_q_up_from_padded)
    dkv_n = mm_dx(dkv, w_kv, "ml_dkvn")
    mm_dw(("mla_w_kv_up", 0), kv_n, dkv, "ml_dwkv")
    dh2, dqg, dkvg = _mla_latent_bwd(h2, dq_n, dkv_n, dkr, dz, qg, kvg, tabs, tm=te, name="ml_blat")
    dxn2 = mm_dx(dh2, w_mi, "ml_dxn")
    mm_dw(("mla_w_in", 0), xn2, dh2, "ml_dwi", post=_mla_in_from_padded)
    dx, dxb, dg2 = _rms_bwd(x2, ws["mla_norm"][0:1], dxn2, dx, tm=tm, name="ml_bnorm")
    gs["mla_norm"], gs["mla_q_norm"], gs["mla_kv_norm"] = dg2, dqg, dkvg

    dy = mm_dx(dxb, w_cv[("conv_w_out", 0)], "cv_dy")
    mm_dw(("conv_w_out", 0), y1, dxb, "cv_dwo")
    dh1, dcw = _conv_bwd(h1, dy, cw, tm=te, name="cv_bmix")
    dxn1 = mm_dx(dh1, w_cv[("conv_w_in", 0)], "cv_dxn")
    mm_dw(("conv_w_in", 0), xn1, dh1, "cv_dwi")
    dx, dxb, dg1 = _rms_bwd(x1, ws["conv_norm"][0:1], dxn1, dx, tm=tm, name="cv_bnorm")
    gs["conv_norm"], gs["conv_w"] = dg1, dcw

    dx, dxb = pool_layer_bwd(dx, dxb, sv0, w_p0, 0, "p0", after=sink.late_ready())
    return loss_part, dx, gs


def kernel(x, positions, pool_norm, pool_w_in, pool_w_grp, pool_scale, pool_w_out, conv_norm, conv_w_in, conv_w, conv_w_out, mla_norm, mla_w_in, mla_q_norm, mla_w_q_up, mla_kv_norm, mla_w_kv_up, mla_w_out, final_norm, loss_target, m_pool_norm, m_pool_w_in, m_pool_w_grp, m_pool_scale, m_pool_w_out, m_conv_norm, m_conv_w_in, m_conv_w, m_conv_w_out, m_mla_norm, m_mla_w_in, m_mla_q_norm, m_mla_w_q_up, m_mla_kv_norm, m_mla_w_kv_up, m_mla_w_out, m_final_norm, v_pool_norm, v_pool_w_in, v_pool_w_grp, v_pool_scale, v_pool_w_out, v_conv_norm, v_conv_w_in, v_conv_w, v_conv_w_out, v_mla_norm, v_mla_w_in, v_mla_q_norm, v_mla_w_q_up, v_mla_kv_norm, v_mla_w_kv_up, v_mla_w_out, v_final_norm):
    names = ("pool_norm", "pool_w_in", "pool_w_grp", "pool_scale", "pool_w_out", "conv_norm", "conv_w_in", "conv_w", "conv_w_out",
             "mla_norm", "mla_w_in", "mla_q_norm", "mla_w_q_up", "mla_kv_norm", "mla_w_kv_up", "mla_w_out", "final_norm")
    w = dict(zip(names, (pool_norm, pool_w_in, pool_w_grp, pool_scale, pool_w_out, conv_norm, conv_w_in, conv_w, conv_w_out,
                         mla_norm, mla_w_in, mla_q_norm, mla_w_q_up, mla_kv_norm, mla_w_kv_up, mla_w_out, final_norm)))
    m = dict(zip(names, (m_pool_norm, m_pool_w_in, m_pool_w_grp, m_pool_scale, m_pool_w_out, m_conv_norm, m_conv_w_in, m_conv_w, m_conv_w_out,
                         m_mla_norm, m_mla_w_in, m_mla_q_norm, m_mla_w_q_up, m_mla_kv_norm, m_mla_w_kv_up, m_mla_w_out, m_final_norm)))
    v = dict(zip(names, (v_pool_norm, v_pool_w_in, v_pool_w_grp, v_pool_scale, v_pool_w_out, v_conv_norm, v_conv_w_in, v_conv_w, v_conv_w_out,
                         v_mla_norm, v_mla_w_in, v_mla_q_norm, v_mla_w_q_up, v_mla_kv_norm, v_mla_w_kv_up, v_mla_w_out, v_final_norm)))
    chip = 2 * lax.axis_index("x") + lax.axis_index("y")
    core = lax.axis_index("c")

    core1 = core.astype(jnp.int32).reshape(1)
    chip_core = jnp.stack([chip, core]).astype(jnp.int32)
    shard_shape = lambda piece: w[piece[0]].shape[1:]
    shard_axis = lambda piece: BIG_SHARD_AXIS[piece[0]] - 1
    full_shape = lambda piece: tuple(d * (N_CHIPS if a == shard_axis(piece) else 1) for a, d in enumerate(shard_shape(piece)))

    gather_rows, gather_at, packs = {}, {}, {}
    for grp, layout in GATHER_LAYOUT.items():
        gather_rows[grp], gather_at[grp] = _slot_rows(layout, shard_shape, PACK_ROW_ALIGN)
        packs[grp] = _pack_slot({(n, j): w[n][j] for (n, j), _, _ in layout}, layout, gather_rows[grp], BF16)

    def gathered_weights(grp, gathered):
        out = {}
        for piece, (r0, n, kind) in gather_at[grp].items():
            if kind == "flat":
                out[piece] = jnp.concatenate([gathered[k, r0:r0 + n].reshape(shard_shape(piece)) for k in range(N_CHIPS)], axis=shard_axis(piece))
            else:
                out[piece] = Packed(gathered, r0, kind, full_shape(piece))
        return out

    p0_start = _ag_ici_start(packs["p0"], w["final_norm"], name="ag_p0_start")
    p0b_start = _ag_ici_start(packs["p0b"], p0_start[4], name="ag_p0b_start")
    cv_start = _ag_ici_start(packs["cv"], p0b_start[4], name="ag_cv_start")
    ml_start = _ag_ici_start(packs["ml"], cv_start[4], name="ag_ml_start")
    in_flight = {"p0": p0_start, "p0b": p0b_start, "cv": cv_start, "ml": ml_start}

    forwarding = {}

    def prefetch(grp, after):
        if grp == "p0":
            after = (after, ml_start[4])
        send_sems, recv_sems, w_thru, land, _ = in_flight[grp]
        w_thru, land = _ag_ici_wait(send_sems, recv_sems, w_thru, land, after, name=f"ag_{grp}_wait")
        forwarding[grp] = _ag_forward_start(land, w_thru, name=f"ag_{grp}_fwd_start")
        return forwarding[grp][3]

    def weights_for(grp, after):
        if grp not in forwarding:
            prefetch(grp, after)
        return gathered_weights(grp, _ag_forward_wait(*forwarding[grp], after, name=f"ag_{grp}_fwd_wait"))

    weights_for.prefetch = prefetch

    ws = {"conv_norm": w["conv_norm"], "final_norm": w["final_norm"]}
    ws.update(_gather_small({n: w[n] for n, _, _ in SMALL_GATHER}, name="ag_small"))

    reduce_rows, reduce_at = {}, {}
    for grp, layout in REDUCE_LAYOUT.items():
        reduce_rows[grp], reduce_at[grp] = _slot_rows(layout, shard_shape, RS_ROW_ALIGN)
    group_of = {piece: grp for grp, layout in REDUCE_LAYOUT.items() for piece, _, _ in layout}

    class Sink:
        def __init__(self):
            self.buf = {grp: lax.empty((N_CHIPS, rows, PACK_COLS), BF16) for grp, rows in reduce_rows.items()}
            self.started = {}

        def dest(self, piece):
            grp = group_of[piece]
            r0, _, kind = reduce_at[grp][piece]
            return None if kind == "flat" else Packed(self.buf[grp], r0, kind, full_shape(piece))

        def put(self, piece, result):
            grp = group_of[piece]
            r0, n, kind = reduce_at[grp][piece]
            if kind == "flat":
                parts = jnp.split(result, N_CHIPS, axis=shard_axis(piece))
                result = lax.dynamic_update_slice(self.buf[grp], jnp.stack([p.reshape(n, PACK_COLS) for p in parts]), (0, r0, 0))
            self.buf[grp] = result

        def start(self, grp, tag):
            theirs = _rs_sibling_swap(self.buf[grp], name=f"{tag}_swap")
            chip_sum = _add2_bf16(self.buf[grp], theirs, core1, name=f"{tag}_add2")
            self.started[grp] = _rs_chip_exchange_start(chip_sum, name=f"{tag}_chips_start")
            return self.started[grp][4]

        def finish(self, grp, after, tag):
            send_sems, recv_sems, chip_sum, land, _ = self.started[grp]
            chip_sum, recv = _rs_chip_exchange_wait(send_sems, recv_sems, chip_sum, land, after, name=f"{tag}_chips_wait")
            half_sum = _add4_f32(chip_sum, recv, chip_core, name=f"{tag}_add4")
            return _rs_sibling_join(half_sum, name=f"{tag}_join")

        def late_ready(self):
            return self.start("late", "rsa")

    sink = Sink()

    loss_part, grad_x, gs = _local_step(x[0], positions, loss_target[0], weights_for, ws, sink)
    loss = lax.psum(loss_part[0, 0], ("x", "y", "c"))

    g, delta, new_m, new_v = {}, {}, {}, {}

    def adam_big(n):
        nj = w[n].shape[0]
        where = [(group_of[(n, j)],) + reduce_at[group_of[(n, j)]][(n, j)] for j in range(nj)]
        if where[0][3] == "flat":
            g[n] = jnp.stack([g_rows[grp][r0:r0 + rows].reshape(w[n].shape[1:]) for grp, r0, rows, _ in where])
            shp = w[n].shape
            two_d = lambda a: a.reshape(-1, shp[-1])
            d_, m_, v_ = _adamw(two_d(w[n]), two_d(g[n]), two_d(m[n]), two_d(v[n]), name=f"adamw_{n}")
            delta[n], new_m[n], new_v[n] = d_.reshape(shp), m_.reshape(shp), v_.reshape(shp)
        else:
            g[n], delta[n], new_m[n], new_v[n] = _adamw_rows(w[n], m[n], v[n], [(g_rows[grp], r0) for grp, r0, _, _ in where], name=f"adamw_{n}")

    first_token = sink.start("first", "rsb")
    g_rows = {"late": sink.finish("late", first_token, "rsa")}
    late_only = [n for n in BIG if all(group_of[(n, j)] == "late" for j in range(w[n].shape[0]))]
    for n in late_only:
        adam_big(n)
    g_rows["first"] = sink.finish("first", tuple(delta[n] for n in late_only), "rsb")
    for n in BIG:
        if n not in late_only:
            adam_big(n)

    gs_sum = _reduce_small(gs, g_rows["first"], name="ar_small")
    row = lambda d: {n: (d[n].reshape(1, -1) if d[n].ndim == 1 else d[n]) for n in SMALL}
    small_out = _adamw_small(row(w), row(m), row(v), gs_sum, chip.astype(jnp.int32).reshape(1), name="adamw_small")
    for dst, res in zip((g, delta, new_m, new_v), small_out):
        for n in SMALL:
            dst[n] = res[n].reshape(w[n].shape)

    return (loss, grad_x[None], *[g[n] for n in names], *[delta[n] for n in names],
            *[new_m[n] for n in names], *[new_v[n] for n in names])
```

```python
import jax
import jax.numpy as jnp
from jax import lax
from jax.experimental import pallas as pl
from jax.experimental.pallas import tpu as pltpu

F32 = jnp.float32
BF16 = jnp.bfloat16

D_MODEL = 1024
D_INNER = 2048
POOL_WINDOWS = (2, 4, 8, 16)
POOL_GROUP = 512
N_HEADS = 16
QK_NOPE = 128
QK_ROPE = 64
V_DIM = 128
HEAD_PAD = 256
Q_LORA = 384
KV_LORA = 256
MLA_IN = Q_LORA + KV_LORA + QK_ROPE + D_INNER
MLA_IN_PAD = 2816
ATTN_SCALE = (QK_NOPE + QK_ROPE) ** -0.5
ROPE_BASE = 10000.0
NORM_EPS = 1e-6
HALO = 16

ADAM_LR = 0.001
ADAM_B1 = 0.9
ADAM_B2 = 0.999
ADAM_EPS = 1e-08
ADAM_WD = 0.01
ADAM_STEP = 10

N_CHIPS = 4
N_DEV = 8
LANES = 128
PACK_COLS = 1024
V7X_VMEM_LIMIT = 56 * 1024 * 1024
MESH = pl.DeviceIdType.MESH


def _cparams(*sem):
    return pltpu.CompilerParams(dimension_semantics=sem, vmem_limit_bytes=V7X_VMEM_LIMIT)


def _pick(n, cap):
    best = None
    for d in range(LANES, min(n, cap) + 1, LANES):
        if n % d == 0:
            best = d
    assert best is not None, (n, cap)
    return best


def _sigmoid(z):
    return 1.0 / (1.0 + jnp.exp(-z))


class Packed:
    def __init__(self, buf, r0, kind, shape):
        self.buf, self.r0, self.kind, self.shape = buf, r0, kind, shape

    def block(self, rb, cb, bk):
        assert self.r0 % bk == 0, (self.r0, bk)
        if self.kind == "cols":
            K = self.shape[0]
            per = self.shape[1] // (N_CHIPS * PACK_COLS)
            assert K % bk == 0
            return cb // per, (self.r0 + (cb % per) * K) // bk + rb
        kk = self.shape[0] // N_CHIPS
        assert kk % bk == 0
        per = kk // bk
        return rb // per, self.r0 // bk + rb % per


def _mm(a, b, *, trans_a=False, trans_b=False, out_dtype=F32, residual=None, after=None, into=None, tm, tn, tk, name):
    if trans_a:
        K, M = a.shape
    else:
        M, K = a.shape
    if trans_b:
        N, K2 = b.shape
    else:
        K2, N = b.shape
    assert K == K2 and M % tm == 0 and N % tn == 0 and K % tk == 0, (name, a.shape, b.shape, tm, tn, tk)
    nk = K // tk
    dn = (((0 if trans_a else 1,), (1 if trans_b else 0,)), ((), ()))
    has_res = residual is not None
    n_skip = (after is not None) + (into is not None)
    b_all = isinstance(b, Packed) and b.kind == "rows"
    o_all = into is not None and into.kind == "rows"
    assert not b_all or (not trans_a and (tn == N if trans_b else tk == K)), name
    assert not o_all or tm == M, name

    def body(*refs):
        if has_res:
            a_ref, b_ref, r_ref = refs[:3]
            refs = refs[3:]
        else:
            a_ref, b_ref = refs[:2]
            r_ref = None
            refs = refs[2:]
        refs = refs[n_skip:]
        o_ref, rest = refs[0], refs[1:]
        if b_all and trans_b:
            part = jnp.concatenate([lax.dot_general(a_ref[...], b_ref[c], dn, preferred_element_type=F32) for c in range(N_CHIPS)], axis=1)
        elif b_all:
            kk = K // N_CHIPS
            part = lax.dot_general(a_ref[:, 0:kk], b_ref[0], dn, preferred_element_type=F32)
            for c in range(1, N_CHIPS):
                part = part + lax.dot_general(a_ref[:, c * kk:(c + 1) * kk], b_ref[c], dn, preferred_element_type=F32)
        else:
            part = lax.dot_general(a_ref[...], b_ref[...], dn, preferred_element_type=F32)

        def finish(acc):
            if has_res:
                acc = acc + r_ref[...]
            if o_all:
                mk = M // N_CHIPS
                for c in range(N_CHIPS):
                    o_ref[c] = acc[c * mk:(c + 1) * mk].astype(o_ref.dtype)
            else:
                o_ref[...] = acc.astype(o_ref.dtype)

        if nk == 1:
            finish(part)
        else:
            acc_ref = rest[0]
            k = pl.program_id(2)

            @pl.when(k == 0)
            def _():
                acc_ref[...] = part

            @pl.when(k > 0)
            def _():
                acc_ref[...] += part

            @pl.when(k == nk - 1)
            def _():
                finish(acc_ref[...])

    a_spec = pl.BlockSpec((tk, tm), lambda i, j, k: (k, i)) if trans_a else pl.BlockSpec((tm, tk), lambda i, j, k: (i, k))
    if b_all:
        kkb = b.shape[0] // N_CHIPS
        b_spec = pl.BlockSpec((N_CHIPS, kkb, PACK_COLS), lambda i, j, k: (0, b.r0 // kkb, 0))
        b_arg = b.buf
    elif isinstance(b, Packed):
        if trans_b:
            assert tk == PACK_COLS
            b_spec = pl.BlockSpec((None, tn, tk), lambda i, j, k: (*b.block(j, k, tn), 0))
        else:
            assert tn == PACK_COLS
            b_spec = pl.BlockSpec((None, tk, tn), lambda i, j, k: (*b.block(k, j, tk), 0))
        b_arg = b.buf
    else:
        b_spec = pl.BlockSpec((tn, tk), lambda i, j, k: (j, k)) if trans_b else pl.BlockSpec((tk, tn), lambda i, j, k: (k, j))
        b_arg = b
    o_spec = pl.BlockSpec((tm, tn), lambda i, j, k: (i, j))
    in_specs = [a_spec, b_spec] + ([o_spec] if has_res else [])
    args = (a, b_arg) + ((residual,) if has_res else ())
    aliases = {}
    if after is not None:
        in_specs.append(pl.BlockSpec(memory_space=pl.ANY))
        args += (after,)
    if into is None:
        out_shape, out_spec = jax.ShapeDtypeStruct((M, N), out_dtype), o_spec
    else:
        assert tn == PACK_COLS and into.shape == (M, N)
        in_specs.append(pl.BlockSpec(memory_space=pl.ANY))
        aliases = {len(args): 0}
        args += (into.buf,)
        out_shape = jax.ShapeDtypeStruct(into.buf.shape, into.buf.dtype)
        if o_all:
            out_spec = pl.BlockSpec((N_CHIPS, M // N_CHIPS, tn), lambda i, j, k: (0, into.r0 // (M // N_CHIPS), 0))
        else:
            out_spec = pl.BlockSpec((None, tm, tn), lambda i, j, k: (*into.block(i, j, tm), 0))
    return pl.pallas_call(
        body, name=name, out_shape=out_shape,
        grid=(M // tm, N // tn, nk),
        in_specs=in_specs, out_specs=out_spec, input_output_aliases=aliases,
        scratch_shapes=[pltpu.VMEM((tm, tn), F32)] if nk > 1 else [],
        compiler_params=_cparams("parallel", "parallel", "arbitrary"),
    )(*args)


def _grouped_tn(a, b, *, tk, name):
    S = a.shape[0]
    G = POOL_GROUP
    nk = S // tk

    def body(a_ref, b_ref, o_ref, acc_ref):
        k = pl.program_id(1)
        part = lax.dot_general(a_ref[...], b_ref[...], (((0,), (0,)), ((), ())), preferred_element_type=F32)

        @pl.when(k == 0)
        def _():
            acc_ref[...] = part

        @pl.when(k > 0)
        def _():
            acc_ref[...] += part

        @pl.when(k == nk - 1)
        def _():
            o_ref[...] = acc_ref[...].astype(o_ref.dtype)

    return pl.pallas_call(
        body, name=name,
        out_shape=jax.ShapeDtypeStruct((len(POOL_WINDOWS), G, G), BF16),
        grid=(len(POOL_WINDOWS), nk),
        in_specs=[pl.BlockSpec((tk, G), lambda g, k: (k, g)), pl.BlockSpec((tk, G), lambda g, k: (k, g))],
        out_specs=pl.BlockSpec((None, G, G), lambda g, k: (g, 0, 0)),
        scratch_shapes=[pltpu.VMEM((G, G), F32)],
        compiler_params=_cparams("parallel", "arbitrary"),
    )(a, b)


def _rms_fwd(x, g, *, tm, name):
    S, D = x.shape

    def body(x_ref, g_ref, o_ref):
        xv = x_ref[...]
        rstd = lax.rsqrt(jnp.mean(xv * xv, axis=-1, keepdims=True) + NORM_EPS)
        o_ref[...] = (xv * rstd * g_ref[...]).astype(o_ref.dtype)

    return pl.pallas_call(
        body, name=name,
        out_shape=jax.ShapeDtypeStruct((S, D), BF16),
        grid=(S // tm,),
        in_specs=[pl.BlockSpec((tm, D), lambda i: (i, 0)), pl.BlockSpec((1, D), lambda i: (0, 0))],
        out_specs=pl.BlockSpec((tm, D), lambda i: (i, 0)),
        compiler_params=_cparams("parallel"),
    )(x, g)


def _rms_bwd_math(xv, gv, dxn):
    rstd = lax.rsqrt(jnp.mean(xv * xv, axis=-1, keepdims=True) + NORM_EPS)
    xh = xv * rstd
    dg = jnp.sum(dxn * xh, axis=0, keepdims=True)
    dxh = dxn * gv
    dx = rstd * (dxh - xh * jnp.mean(dxh * xh, axis=-1, keepdims=True))
    return dx, dg


def _rms_bwd(x, g, dxn, dres, *, tm, name):
    S, D = x.shape

    def body(x_ref, g_ref, dxn_ref, dres_ref, dx_ref, dxb_ref, dg_ref):
        dx, dg = _rms_bwd_math(x_ref[...], g_ref[...], dxn_ref[...])
        dx = dx + dres_ref[...]
        dx_ref[...] = dx
        dxb_ref[...] = dx.astype(BF16)

        @pl.when(pl.program_id(0) == 0)
        def _():
            dg_ref[...] = dg

        @pl.when(pl.program_id(0) > 0)
        def _():
            dg_ref[...] += dg

    row = pl.BlockSpec((tm, D), lambda i: (i, 0))
    vec = pl.BlockSpec((1, D), lambda i: (0, 0))
    return pl.pallas_call(
        body, name=name,
        out_shape=(jax.ShapeDtypeStruct((S, D), F32), jax.ShapeDtypeStruct((S, D), BF16), jax.ShapeDtypeStruct((1, D), F32)),
        grid=(S // tm,),
        in_specs=[row, vec, row, row],
        out_specs=(row, row, vec),
        compiler_params=_cparams("arbitrary"),
    )(x, g, dxn, dres)


def _final_loss(x, g, target, *, tm, name):
    S, D = x.shape

    def body(x_ref, g_ref, t_ref, loss_ref, dx_ref, dxb_ref, dg_ref):
        xv = x_ref[...]
        gv = g_ref[...]
        rstd = lax.rsqrt(jnp.mean(xv * xv, axis=-1, keepdims=True) + NORM_EPS)
        xh = xv * rstd
        err = xh * gv - t_ref[...]
        part = 0.5 * jnp.sum(jnp.mean(err * err, axis=-1, keepdims=True), axis=0, keepdims=True)
        dy = err * (1.0 / D)
        dg = jnp.sum(dy * xh, axis=0, keepdims=True)
        dxh = dy * gv
        dx = rstd * (dxh - xh * jnp.mean(dxh * xh, axis=-1, keepdims=True))
        dx_ref[...] = dx
        dxb_ref[...] = dx.astype(BF16)
        lossb = jnp.broadcast_to(part, loss_ref.shape)

        @pl.when(pl.program_id(0) == 0)
        def _():
            dg_ref[...] = dg
            loss_ref[...] = lossb

        @pl.when(pl.program_id(0) > 0)
        def _():
            dg_ref[...] += dg
            loss_ref[...] += lossb

    row = pl.BlockSpec((tm, D), lambda i: (i, 0))
    vec = pl.BlockSpec((1, D), lambda i: (0, 0))
    lspec = pl.BlockSpec((8, LANES), lambda i: (0, 0))
    return pl.pallas_call(
        body, name=name,
        out_shape=(jax.ShapeDtypeStruct((8, LANES), F32), jax.ShapeDtypeStruct((S, D), F32),
                   jax.ShapeDtypeStruct((S, D), BF16), jax.ShapeDtypeStruct((1, D), F32)),
        grid=(S // tm,),
        in_specs=[row, vec, row],
        out_specs=(lspec, row, row, vec),
        compiler_params=_cparams("arbitrary"),
    )(x, g, target)


def _prev_halo_spec(tm, width, col):
    r = tm // HALO
    return pl.BlockSpec((HALO, width), lambda i: (jnp.maximum(i * r - 1, 0), col))


def _next_halo_spec(tm, width, col, S):
    r = tm // HALO
    last = S // HALO - 1
    return pl.BlockSpec((HALO, width), lambda i: (jnp.minimum((i + 1) * r, last), col))


def _shift_down(ext, k):
    return pltpu.roll(ext, k, 0)[HALO:, :]


def _shift_up(ext, k, tm):
    n = ext.shape[0]
    return pltpu.roll(ext, n - k, 0)[:tm, :]


def _pool_window_sum(ext, w):
    s = ext
    k = 1
    while k < w:
        s = s + pltpu.roll(s, k, 0)
        k *= 2
    return s[HALO:, :]


def _pooled_group(u_ref, halo, g, w, t_idx):
    cs = slice(g * POOL_GROUP, (g + 1) * POOL_GROUP)
    u = u_ref[:, cs]
    ext = jnp.concatenate([halo[:, cs], u], axis=0)
    inv = 1.0 / jnp.minimum(t_idx + 1, w).astype(F32)
    return _pool_window_sum(ext, w) * inv - u


def _pool_fwd(h, w_grp, scale, *, tm, name):
    S = h.shape[0]
    E = D_INNER

    def body(u_ref, uh_ref, z_ref, wg_ref, sc_ref, y_ref):
        i = pl.program_id(0)
        halo = jnp.where(i > 0, uh_ref[...], 0.0)
        t_idx = i * tm + lax.broadcasted_iota(jnp.int32, (tm, 1), 0)
        for g, w in enumerate(POOL_WINDOWS):
            cs = slice(g * POOL_GROUP, (g + 1) * POOL_GROUP)
            pooled = _pooled_group(u_ref, halo, g, w, t_idx)
            mixed = jnp.dot(pooled.astype(BF16), wg_ref[g], preferred_element_type=F32)
            z = z_ref[:, cs]
            y_ref[:, cs] = (mixed * sc_ref[:, cs] * (z * _sigmoid(z))).astype(BF16)

    return pl.pallas_call(
        body, name=name,
        out_shape=jax.ShapeDtypeStruct((S, E), BF16),
        grid=(S // tm,),
        in_specs=[pl.BlockSpec((tm, E), lambda i: (i, 0)), _prev_halo_spec(tm, E, 0),
                  pl.BlockSpec((tm, E), lambda i: (i, 1)),
                  pl.BlockSpec((len(POOL_WINDOWS), POOL_GROUP, POOL_GROUP), lambda i: (0, 0, 0)),
                  pl.BlockSpec((1, E), lambda i: (0, 0))],
        out_specs=pl.BlockSpec((tm, E), lambda i: (i, 0)),
        compiler_params=_cparams("parallel"),
    )(h, h, h, w_grp, scale)


def _pool_bwd1(h, dy, w_grp, scale, *, tm, name):
    S = h.shape[0]
    E = D_INNER

    def body(u_ref, uh_ref, z_ref, dy_ref, wg_ref, sc_ref, pooled_ref, dmixed_ref, dpooled_ref, dz_ref, dsc_ref):
        i = pl.program_id(0)
        halo = jnp.where(i > 0, uh_ref[...], 0.0)
        t_idx = i * tm + lax.broadcasted_iota(jnp.int32, (tm, 1), 0)
        for g, w in enumerate(POOL_WINDOWS):
            cs = slice(g * POOL_GROUP, (g + 1) * POOL_GROUP)
            pooled = _pooled_group(u_ref, halo, g, w, t_idx).astype(BF16)
            wg = wg_ref[g]
            mixed = jnp.dot(pooled, wg, preferred_element_type=F32)
            z = z_ref[:, cs]
            sg = _sigmoid(z)
            dyv = dy_ref[:, cs]
            sc = sc_ref[:, cs]
            dms = dyv * (z * sg)
            dz = dyv * (mixed * sc) * (sg * (1.0 + z * (1.0 - sg)))
            dsc = jnp.sum(dms * mixed, axis=0, keepdims=True)
            dmixed = (dms * sc).astype(BF16)
            dpooled = lax.dot_general(dmixed, wg, (((1,), (1,)), ((), ())), preferred_element_type=F32)
            pooled_ref[:, cs] = pooled
            dmixed_ref[:, cs] = dmixed
            dpooled_ref[:, cs] = dpooled
            dz_ref[:, cs] = dz.astype(BF16)

            @pl.when(i == 0)
            def _():
                dsc_ref[:, cs] = dsc

            @pl.when(i > 0)
            def _():
                dsc_ref[:, cs] += dsc

    row = pl.BlockSpec((tm, E), lambda i: (i, 0))
    vec = pl.BlockSpec((1, E), lambda i: (0, 0))
    return pl.pallas_call(
        body, name=name,
        out_shape=(jax.ShapeDtypeStruct((S, E), BF16), jax.ShapeDtypeStruct((S, E), BF16),
                   jax.ShapeDtypeStruct((S, E), F32), jax.ShapeDtypeStruct((S, E), BF16),
                   jax.ShapeDtypeStruct((1, E), F32)),
        grid=(S // tm,),
        in_specs=[row, _prev_halo_spec(tm, E, 0), pl.BlockSpec((tm, E), lambda i: (i, 1)), row,
                  pl.BlockSpec((len(POOL_WINDOWS), POOL_GROUP, POOL_GROUP), lambda i: (0, 0, 0)), vec],
        out_specs=(row, row, row, row, vec),
        compiler_params=_cparams("arbitrary"),
    )(h, h, h, dy, w_grp, scale)


def _pool_bwd2(dpooled, dz, *, tm, name):
    S = dpooled.shape[0]
    E = D_INNER
    nt = S // tm

    def body(dp_ref, dpn_ref, dz_ref, dh_ref):
        i = pl.program_id(0)
        nxt = jnp.where(i < nt - 1, dpn_ref[...], 0.0)
        t_ext = i * tm + lax.broadcasted_iota(jnp.int32, (tm + HALO, 1), 0)
        for g, w in enumerate(POOL_WINDOWS):
            cs = slice(g * POOL_GROUP, (g + 1) * POOL_GROUP)
            dp = dp_ref[:, cs]
            inv = 1.0 / jnp.minimum(t_ext + 1, w).astype(F32)
            s = jnp.concatenate([dp, nxt[:, cs]], axis=0) * inv
            n = tm + HALO
            k = 1
            while k < w:
                s = s + pltpu.roll(s, n - k, 0)
                k *= 2
            dh_ref[:, cs] = (s[:tm, :] - dp).astype(BF16)
        dh_ref[:, E:] = dz_ref[...]

    return pl.pallas_call(
        body, name=name,
        out_shape=jax.ShapeDtypeStruct((S, 2 * E), BF16),
        grid=(nt,),
        in_specs=[pl.BlockSpec((tm, E), lambda i: (i, 0)), _next_halo_spec(tm, E, 0, S),
                  pl.BlockSpec((tm, E), lambda i: (i, 0))],
        out_specs=pl.BlockSpec((tm, 2 * E), lambda i: (i, 0)),
        compiler_params=_cparams("parallel"),
    )(dpooled, dpooled, dz)


CONV_CHUNK = 512


def _conv_fwd(h, cw, *, tm, name):
    S = h.shape[0]
    E = D_INNER

    def body(b_ref, c_ref, hh_ref, z_ref, ch_ref, hhh_ref, w_ref, y_ref):
        i = pl.program_id(0)
        for j in range(E // CONV_CHUNK):
            cs = slice(j * CONV_CHUNK, (j + 1) * CONV_CHUNK)
            p = c_ref[:, cs] * hh_ref[:, cs]
            ph = jnp.where(i > 0, ch_ref[:, cs] * hhh_ref[:, cs], 0.0)
            ext = jnp.concatenate([ph, p], axis=0)
            conv = w_ref[2:3, cs] * p + w_ref[1:2, cs] * _shift_down(ext, 1) + w_ref[0:1, cs] * _shift_down(ext, 2)
            z = z_ref[:, cs]
            y_ref[:, cs] = (b_ref[:, cs] * conv * (z * _sigmoid(z))).astype(BF16)

    col = lambda c: pl.BlockSpec((tm, E), lambda i: (i, c))
    return pl.pallas_call(
        body, name=name,
        out_shape=jax.ShapeDtypeStruct((S, E), BF16),
        grid=(S // tm,),
        in_specs=[col(0), col(1), col(2), col(3), _prev_halo_spec(tm, E, 1), _prev_halo_spec(tm, E, 2),
                  pl.BlockSpec((8, E), lambda i: (0, 0))],
        out_specs=pl.BlockSpec((tm, E), lambda i: (i, 0)),
        compiler_params=_cparams("parallel"),
    )(h, h, h, h, h, h, cw)


def _conv_bwd(h, dy, cw, *, tm, name):
    S = h.shape[0]
    E = D_INNER
    nt = S // tm

    def body(b_ref, c_ref, hh_ref, z_ref, dy_ref, ch_ref, hhh_ref, bn_ref, zn_ref, dyn_ref, w_ref, dh_ref, dw_ref):
        i = pl.program_id(0)
        for j in range(E // CONV_CHUNK):
            cs = slice(j * CONV_CHUNK, (j + 1) * CONV_CHUNK)
            w0, w1, w2 = w_ref[0:1, cs], w_ref[1:2, cs], w_ref[2:3, cs]
            c, hh, b, z, dyv = c_ref[:, cs], hh_ref[:, cs], b_ref[:, cs], z_ref[:, cs], dy_ref[:, cs]
            p = c * hh
            ph = jnp.where(i > 0, ch_ref[:, cs] * hhh_ref[:, cs], 0.0)
            ext = jnp.concatenate([ph, p], axis=0)
            pm1 = _shift_down(ext, 1)
            pm2 = _shift_down(ext, 2)
            conv = w2 * p + w1 * pm1 + w0 * pm2
            sg = _sigmoid(z)
            dy0 = dyv * (z * sg)
            dz = dyv * (b * conv) * (sg * (1.0 + z * (1.0 - sg)))
            db = dy0 * conv
            dconv = dy0 * b
            zn = zn_ref[:, cs]
            dconv_n = jnp.where(i < nt - 1, dyn_ref[:, cs] * (zn * _sigmoid(zn)) * bn_ref[:, cs], 0.0)
            dext = jnp.concatenate([dconv, dconv_n], axis=0)
            dp = w2 * dconv + w1 * _shift_up(dext, 1, tm) + w0 * _shift_up(dext, 2, tm)
            dh_ref[:, 0 * E + j * CONV_CHUNK:0 * E + (j + 1) * CONV_CHUNK] = db.astype(BF16)
            dh_ref[:, 1 * E + j * CONV_CHUNK:1 * E + (j + 1) * CONV_CHUNK] = (dp * hh).astype(BF16)
            dh_ref[:, 2 * E + j * CONV_CHUNK:2 * E + (j + 1) * CONV_CHUNK] = (dp * c).astype(BF16)
            dh_ref[:, 3 * E + j * CONV_CHUNK:3 * E + (j + 1) * CONV_CHUNK] = dz.astype(BF16)
            dw = jnp.concatenate([jnp.sum(dconv * pm2, axis=0, keepdims=True),
                                  jnp.sum(dconv * pm1, axis=0, keepdims=True),
                                  jnp.sum(dconv * p, axis=0, keepdims=True),
                                  jnp.zeros((5, CONV_CHUNK), F32)], axis=0)

            @pl.when(i == 0)
            def _():
                dw_ref[:, cs] = dw

            @pl.when(i > 0)
            def _():
                dw_ref[:, cs] += dw

    col = lambda c: pl.BlockSpec((tm, E), lambda i: (i, c))
    return pl.pallas_call(
        body, name=name,
        out_shape=(jax.ShapeDtypeStruct((S, 4 * E), BF16), jax.ShapeDtypeStruct((8, E), F32)),
        grid=(nt,),
        in_specs=[col(0), col(1), col(2), col(3), pl.BlockSpec((tm, E), lambda i: (i, 0)),
                  _prev_halo_spec(tm, E, 1), _prev_halo_spec(tm, E, 2),
                  _next_halo_spec(tm, E, 0, S), _next_halo_spec(tm, E, 3, S), _next_halo_spec(tm, E, 0, S),
                  pl.BlockSpec((8, E), lambda i: (0, 0))],
        out_specs=(pl.BlockSpec((tm, 4 * E), lambda i: (i, 0)), pl.BlockSpec((8, E), lambda i: (0, 0))),
        compiler_params=_cparams("arbitrary"),
    )(h, h, h, h, dy, h, h, h, h, dy, cw)


KV_LAT_BLK = D_INNER // KV_LORA
Q_LAT_BLK = (D_INNER + KV_LORA) // Q_LORA
K_ROPE_BLK = (D_INNER + KV_LORA + Q_LORA) // LANES


def _rope(blk, c, s1, s2):
    return blk * c + pltpu.roll(blk, LANES - QK_ROPE // 2, 1) * s1 + pltpu.roll(blk, QK_ROPE // 2, 1) * s2


def _unrope(blk, c, s1, s2):
    return blk * c - pltpu.roll(blk, LANES - QK_ROPE // 2, 1) * s1 - pltpu.roll(blk, QK_ROPE // 2, 1) * s2


def _lat_norm(v, g):
    rstd = lax.rsqrt(jnp.mean(v * v, axis=-1, keepdims=True) + NORM_EPS)
    return v * rstd * g


def _mla_latent_fwd(h, q_norm, kv_norm, tabs, *, tm, name):
    S = h.shape[0]

    def body(kv_ref, q_ref, kr_ref, qg_ref, kvg_ref, c_ref, s1_ref, s2_ref, qn_ref, kvn_ref, krr_ref):
        qn_ref[...] = _lat_norm(q_ref[...], qg_ref[...]).astype(BF16)
        kvn_ref[...] = _lat_norm(kv_ref[...], kvg_ref[...]).astype(BF16)
        krr_ref[...] = _rope(kr_ref[...], c_ref[...], s1_ref[...], s2_ref[...]).astype(BF16)

    tab = pl.BlockSpec((tm, LANES), lambda i: (i, 0))
    return pl.pallas_call(
        body, name=name,
        out_shape=(jax.ShapeDtypeStruct((S, Q_LORA), BF16), jax.ShapeDtypeStruct((S, KV_LORA), BF16),
                   jax.ShapeDtypeStruct((S, LANES), BF16)),
        grid=(S // tm,),
        in_specs=[pl.BlockSpec((tm, KV_LORA), lambda i: (i, KV_LAT_BLK)), pl.BlockSpec((tm, Q_LORA), lambda i: (i, Q_LAT_BLK)),
                  pl.BlockSpec((tm, LANES), lambda i: (i, K_ROPE_BLK)),
                  pl.BlockSpec((1, Q_LORA), lambda i: (0, 0)), pl.BlockSpec((1, KV_LORA), lambda i: (0, 0)), tab, tab, tab],
        out_specs=(pl.BlockSpec((tm, Q_LORA), lambda i: (i, 0)), pl.BlockSpec((tm, KV_LORA), lambda i: (i, 0)), tab),
        compiler_params=_cparams("parallel"),
    )(h, h, h, q_norm, kv_norm, *tabs)


def _mla_q_up(q_n, w_q_pad, tabs, *, tm, name):
    S = q_n.shape[0]

    def body(a_ref, w_ref, c_ref, s1_ref, s2_ref, o_ref):
        a = a_ref[...]
        for hd in range(N_HEADS):
            acc = jnp.dot(a, w_ref[:, hd * HEAD_PAD:(hd + 1) * HEAD_PAD], preferred_element_type=F32)
            o_ref[hd, :, :QK_NOPE] = acc[:, :QK_NOPE].astype(BF16)
            o_ref[hd, :, QK_NOPE:] = _rope(acc[:, QK_NOPE:], c_ref[...], s1_ref[...], s2_ref[...]).astype(BF16)

    tab = pl.BlockSpec((tm, LANES), lambda i: (i, 0))
    return pl.pallas_call(
        body, name=name,
        out_shape=jax.ShapeDtypeStruct((N_HEADS, S, HEAD_PAD), BF16),
        grid=(S // tm,),
        in_specs=[pl.BlockSpec((tm, Q_LORA), lambda i: (i, 0)), pl.BlockSpec((Q_LORA, N_HEADS * HEAD_PAD), lambda i: (0, 0)),
                  tab, tab, tab],
        out_specs=pl.BlockSpec((N_HEADS, tm, HEAD_PAD), lambda i: (0, i, 0)),
        compiler_params=_cparams("parallel"),
    )(q_n, w_q_pad, *tabs)


def _mla_kv_up(kv_n, w_kv, krr, *, tm, name):
    S = kv_n.shape[0]
    heads_per_chip = N_HEADS // N_CHIPS

    def body(a_ref, w_ref, krr_ref, k_ref, v_ref):
        a = a_ref[...]
        ones = jnp.ones((tm, V_DIM), BF16)
        for hd in range(N_HEADS):
            lo = (hd % heads_per_chip) * HEAD_PAD
            acc = jnp.dot(a, w_ref[hd // heads_per_chip, :, lo:lo + HEAD_PAD], preferred_element_type=F32)
            k_ref[hd, :, :QK_NOPE] = acc[:, :QK_NOPE].astype(BF16)
            k_ref[hd, :, QK_NOPE:] = krr_ref[...]
            v_ref[hd, :, :V_DIM] = acc[:, QK_NOPE:].astype(BF16)
            v_ref[hd, :, V_DIM:] = ones

    head_blk = pl.BlockSpec((N_HEADS, tm, HEAD_PAD), lambda i: (0, i, 0))
    sds = jax.ShapeDtypeStruct((N_HEADS, S, HEAD_PAD), BF16)
    return pl.pallas_call(
        body, name=name, out_shape=(sds, sds),
        grid=(S // tm,),
        in_specs=[pl.BlockSpec((tm, KV_LORA), lambda i: (i, 0)),
                  pl.BlockSpec((N_CHIPS, KV_LORA, PACK_COLS), lambda i: (0, w_kv.r0 // KV_LORA, 0)),
                  pl.BlockSpec((tm, LANES), lambda i: (i, 0))],
        out_specs=(head_blk, head_blk),
        compiler_params=_cparams("parallel"),
    )(kv_n, w_kv.buf, krr)


LOG2E = 1.4426950408889634
SCORE_TO_LOG2 = ATTN_SCALE * LOG2E


def _flash_fwd(q_full, k_full, v_aug, h, *, tq, name):
    H, S, _ = q_full.shape
    tk = tq
    HP = 2
    QT = max(t for t in (4, 2, 1) if (S // tq) % t == 0)
    rows_of = lambda t: slice(t * tq, (t + 1) * tq)

    def body(q_ref, k_ref, v_ref, z_ref, o_ref, y_ref, lse_ref, m_sc, acc_sc):
        first = pl.program_id(1) * QT
        m_sc[...] = jnp.full(m_sc.shape, -1e30, F32)
        acc_sc[...] = jnp.zeros(acc_sc.shape, F32)

        def chunk(j, masked):
            off = pl.multiple_of(j * tk, tk)
            for hh in range(HP):
                kj = k_ref[hh, pl.ds(off, tk), :]
                vj = v_ref[hh, pl.ds(off, tk), :]
                for t in range(QT):
                    if masked[t] is None:
                        continue
                    rows = rows_of(t)
                    s = lax.dot_general(q_ref[hh, rows], kj, (((1,), (1,)), ((), ())), preferred_element_type=F32) * SCORE_TO_LOG2
                    if masked[t]:
                        keep = lax.broadcasted_iota(jnp.int32, (tq, tk), 1) <= lax.broadcasted_iota(jnp.int32, (tq, tk), 0)
                        s = jnp.where(keep, s, -1e30)
                    m_old = m_sc[hh, rows]
                    m_new = jnp.maximum(m_old, jnp.max(s, axis=1, keepdims=True))
                    p = jnp.exp2(s - jnp.tile(m_new, (1, tk // LANES)))
                    alpha = jnp.exp2(m_old - m_new)
                    acc_sc[hh, rows] = jnp.tile(alpha, (1, 2)) * acc_sc[hh, rows] + jnp.dot(p.astype(BF16), vj, preferred_element_type=F32)
                    m_sc[hh, rows] = m_new

        def step(j, carry):
            chunk(j, (False,) * QT)
            return carry

        lax.fori_loop(0, first, step, 0)
        for d in range(QT):
            chunk(first + d, tuple(None if t < d else t == d for t in range(QT)))
        for hh in range(HP):
            cs = slice(hh * V_DIM, (hh + 1) * V_DIM)
            for t in range(QT):
                rows = rows_of(t)
                l = acc_sc[hh, rows, V_DIM:]
                o = acc_sc[hh, rows, :V_DIM] / l
                z = z_ref[rows, cs]
                o_ref[rows, cs] = o
                y_ref[rows, cs] = (o * (z * _sigmoid(z))).astype(BF16)
                lse_ref[hh, t] = (m_sc[hh, rows] + jnp.log2(l)).T[0:1, :]

    pair = pl.BlockSpec((QT * tq, HP * V_DIM), lambda hd, i: (i, hd))
    return pl.pallas_call(
        body, name=name,
        out_shape=(jax.ShapeDtypeStruct((S, D_INNER), F32), jax.ShapeDtypeStruct((S, D_INNER), BF16),
                   jax.ShapeDtypeStruct((H, S // tq, 1, tq), F32)),
        grid=(H // HP, S // (QT * tq)),
        in_specs=[pl.BlockSpec((HP, QT * tq, HEAD_PAD), lambda hd, i: (hd, i, 0)),
                  pl.BlockSpec((HP, S, HEAD_PAD), lambda hd, i: (hd, 0, 0)),
                  pl.BlockSpec((HP, S, HEAD_PAD), lambda hd, i: (hd, 0, 0)),
                  pair],
        out_specs=(pair, pair, pl.BlockSpec((HP, QT, 1, tq), lambda hd, i: (hd, i, 0, 0))),
        scratch_shapes=[pltpu.VMEM((HP, QT * tq, LANES), F32), pltpu.VMEM((HP, QT * tq, HEAD_PAD), F32)],
        compiler_params=_cparams("parallel", "parallel"),
    )(q_full, k_full, v_aug, h)


def _mla_gate_bwd(dy, o, h, *, tm, name):
    S = dy.shape[0]
    E = D_INNER

    def body(dy_ref, o_ref, z_ref, do_ref, dz_ref, delta_ref):
        for hd in range(N_HEADS):
            cs = slice(hd * V_DIM, (hd + 1) * V_DIM)
            z = z_ref[:, cs]
            sg = _sigmoid(z)
            dyv = dy_ref[:, cs]
            ov = o_ref[:, cs]
            do = dyv * (z * sg)
            do_ref[:, cs] = do.astype(BF16)
            dz_ref[:, cs] = (dyv * ov * (sg * (1.0 + z * (1.0 - sg)))).astype(BF16)
            delta_ref[hd, 0] = jnp.broadcast_to(jnp.sum(do * ov, axis=-1, keepdims=True), (tm, LANES)).T[0:1, :]

    row = pl.BlockSpec((tm, E), lambda i: (i, 0))
    return pl.pallas_call(
        body, name=name,
        out_shape=(jax.ShapeDtypeStruct((S, E), BF16), jax.ShapeDtypeStruct((S, E), BF16),
                   jax.ShapeDtypeStruct((N_HEADS, S // tm, 1, tm), F32)),
        grid=(S // tm,),
        in_specs=[row, row, row],
        out_specs=(row, row, pl.BlockSpec((N_HEADS, 1, 1, tm), lambda i: (0, i, 0, 0))),
        compiler_params=_cparams("parallel"),
    )(dy, o, h)


def _flash_bwd(q_full, k_full, v_aug, do, lse_rows, delta_rows, tabs, *, tq, name):
    H, S, _ = q_full.shape
    tk = tq
    nq = S // tq
    HP = 2
    KT = max(t for t in (4, 2, 1) if nq % t == 0)
    rows_of = lambda t: slice(t * tk, (t + 1) * tk)

    def body(q_ref, k_ref, v_ref, do_ref, lse_ref, dl_ref, c_ref, s1_ref, s2_ref, dqp_ref, dkv_ref, dkr_ref, dq_ref, dk_sc, dv_sc):
        first = pl.program_id(1) * KT

        @pl.when(first == 0)
        def _():
            dq_ref[...] = jnp.zeros(dq_ref.shape, F32)

        dk_sc[...] = jnp.zeros(dk_sc.shape, F32)
        dv_sc[...] = jnp.zeros(dv_sc.shape, F32)

        def chunk(qi, masked):
            off = pl.multiple_of(qi * tq, tq)
            for hh in range(HP):
                q = q_ref[hh, pl.ds(off, tq), :]
                dov = do_ref[pl.ds(off, tq), hh * V_DIM:(hh + 1) * V_DIM]
                for t in range(KT):
                    if masked[t] is None:
                        continue
                    rows = rows_of(t)
                    k = k_ref[hh, rows]
                    s_t = lax.dot_general(k, q, (((1,), (1,)), ((), ())), preferred_element_type=F32) * SCORE_TO_LOG2
                    p_t = jnp.exp2(s_t - lse_ref[hh, qi])
                    if masked[t]:
                        keep = lax.broadcasted_iota(jnp.int32, (tk, tq), 0) <= lax.broadcasted_iota(jnp.int32, (tk, tq), 1)
                        p_t = jnp.where(keep, p_t, 0.0)
                    dv_sc[hh, rows] += jnp.dot(p_t.astype(BF16), dov, preferred_element_type=F32)
                    dp_t = lax.dot_general(v_ref[hh, rows], dov, (((1,), (1,)), ((), ())), preferred_element_type=F32)
                    ds = (p_t * (dp_t - dl_ref[hh, qi])).astype(BF16)
                    dk_sc[hh, rows] += jnp.dot(ds, q, preferred_element_type=F32)
                    dq_ref[pl.ds(off, tq), hh * HEAD_PAD:(hh + 1) * HEAD_PAD] += lax.dot_general(
                        ds, k, (((0,), (0,)), ((), ())), preferred_element_type=F32)

        def step(qi, carry):
            chunk(qi, (False,) * KT)
            return carry

        for d in range(KT):
            chunk(first + d, tuple(None if t > d else t == d for t in range(KT)))
            done = pl.ds(pl.multiple_of((first + d) * tq, tq), tq)
            here = rows_of(d)
            for hh in range(HP):
                lo = hh * HEAD_PAD
                dqp_ref[here, lo:lo + QK_NOPE] = (dq_ref[done, lo:lo + QK_NOPE] * ATTN_SCALE).astype(BF16)
                dqp_ref[here, lo + QK_NOPE:lo + HEAD_PAD] = _unrope(dq_ref[done, lo + QK_NOPE:lo + HEAD_PAD] * ATTN_SCALE,
                                                                    c_ref[here, :], s1_ref[here, :], s2_ref[here, :]).astype(BF16)
        lax.fori_loop(first + KT, nq, step, 0)
        for hh in range(HP):
            lo = hh * HEAD_PAD
            dkv_ref[:, lo:lo + QK_NOPE] = (dk_sc[hh, :, :QK_NOPE] * ATTN_SCALE).astype(BF16)
            dkv_ref[:, lo + QK_NOPE:lo + HEAD_PAD] = dv_sc[hh].astype(BF16)
            dkr_ref[hh] = dk_sc[hh, :, QK_NOPE:] * ATTN_SCALE

    tab = pl.BlockSpec((KT * tk, LANES), lambda hd, j: (j, 0))
    pair_rows = pl.BlockSpec((KT * tk, HP * HEAD_PAD), lambda hd, j: (j, hd))
    return pl.pallas_call(
        body, name=name,
        out_shape=(jax.ShapeDtypeStruct((S, H * HEAD_PAD), BF16), jax.ShapeDtypeStruct((S, H * HEAD_PAD), BF16),
                   jax.ShapeDtypeStruct((H, S, LANES), F32)),
        grid=(H // HP, S // (KT * tk)),
        in_specs=[pl.BlockSpec((HP, S, HEAD_PAD), lambda hd, j: (hd, 0, 0)),
                  pl.BlockSpec((HP, KT * tk, HEAD_PAD), lambda hd, j: (hd, j, 0)),
                  pl.BlockSpec((HP, KT * tk, V_DIM), lambda hd, j: (hd, j, 0)),
                  pl.BlockSpec((S, HP * V_DIM), lambda hd, j: (0, hd)),
                  pl.BlockSpec((HP, nq, 1, tq), lambda hd, j: (hd, 0, 0, 0)),
                  pl.BlockSpec((HP, nq, 1, tq), lambda hd, j: (hd, 0, 0, 0)), tab, tab, tab],
        out_specs=(pair_rows, pair_rows, pl.BlockSpec((HP, KT * tk, LANES), lambda hd, j: (hd, j, 0))),
        scratch_shapes=[pltpu.VMEM((S, HP * HEAD_PAD), F32), pltpu.VMEM((HP, KT * tk, HEAD_PAD), F32), pltpu.VMEM((HP, KT * tk, V_DIM), F32)],
        compiler_params=_cparams("parallel", "arbitrary"),
    )(q_full, k_full, v_aug, do, lse_rows, delta_rows, *tabs)


def _mla_latent_bwd(h, dq_n, dkv_n, dkr, dz, q_norm, kv_norm, tabs, *, tm, name):
    S = h.shape[0]

    def body(kv_ref, q_ref, dqn_ref, dkvn_ref, dkr_ref, dz_ref, qg_ref, kvg_ref, c_ref, s1_ref, s2_ref, dh_ref, dqg_ref, dkvg_ref):
        i = pl.program_id(0)
        dq_lat, dqg = _rms_bwd_math(q_ref[...], qg_ref[...], dqn_ref[...])
        dkv_lat, dkvg = _rms_bwd_math(kv_ref[...], kvg_ref[...], dkvn_ref[...])
        dkr_sum = dkr_ref[0]
        for hd in range(1, N_HEADS):
            dkr_sum = dkr_sum + dkr_ref[hd]
        dh_ref[:, :D_INNER] = dz_ref[...]
        dh_ref[:, D_INNER:D_INNER + KV_LORA] = dkv_lat.astype(BF16)
        dh_ref[:, D_INNER + KV_LORA:D_INNER + KV_LORA + Q_LORA] = dq_lat.astype(BF16)
        dh_ref[:, D_INNER + KV_LORA + Q_LORA:] = _unrope(dkr_sum, c_ref[...], s1_ref[...], s2_ref[...]).astype(BF16)

        @pl.when(i == 0)
        def _():
            dqg_ref[...] = dqg
            dkvg_ref[...] = dkvg

        @pl.when(i > 0)
        def _():
            dqg_ref[...] += dqg
            dkvg_ref[...] += dkvg

    tab = pl.BlockSpec((tm, LANES), lambda i: (i, 0))
    qvec = pl.BlockSpec((1, Q_LORA), lambda i: (0, 0))
    kvvec = pl.BlockSpec((1, KV_LORA), lambda i: (0, 0))
    return pl.pallas_call(
        body, name=name,
        out_shape=(jax.ShapeDtypeStruct((S, MLA_IN_PAD), BF16), jax.ShapeDtypeStruct((1, Q_LORA), F32),
                   jax.ShapeDtypeStruct((1, KV_LORA), F32)),
        grid=(S // tm,),
        in_specs=[pl.BlockSpec((tm, KV_LORA), lambda i: (i, KV_LAT_BLK)), pl.BlockSpec((tm, Q_LORA), lambda i: (i, Q_LAT_BLK)),
                  pl.BlockSpec((tm, Q_LORA), lambda i: (i, 0)), pl.BlockSpec((tm, KV_LORA), lambda i: (i, 0)),
                  pl.BlockSpec((N_HEADS, tm, LANES), lambda i: (0, i, 0)), pl.BlockSpec((tm, D_INNER), lambda i: (i, 0)),
                  qvec, kvvec, tab, tab, tab],
        out_specs=(pl.BlockSpec((tm, MLA_IN_PAD), lambda i: (i, 0)), qvec, kvvec),
        compiler_params=_cparams("arbitrary"),
    )(h, h, dq_n, dkv_n, dkr, dz, q_norm, kv_norm, *tabs)


def _adamw(w, g, m, v, *, name):
    R, C = w.shape
    tr = R
    for cand in (512, 256, 128, 64, 32, 16, 8):
        if R % cand == 0 and cand * C * 4 <= 2 * 1024 * 1024:
            tr = cand
            break

    def body(w_ref, g_ref, m_ref, v_ref, d_ref, nm_ref, nv_ref):
        d_ref[...], nm_ref[...], nv_ref[...] = _adam_math(w_ref[...], g_ref[...], m_ref[...], v_ref[...])

    spec = pl.BlockSpec((tr, C), lambda i: (i, 0))
    sds = jax.ShapeDtypeStruct((R, C), F32)
    return pl.pallas_call(
        body, name=name, out_shape=(sds, sds, sds), grid=(R // tr,),
        in_specs=[spec] * 4, out_specs=(spec,) * 3,
        compiler_params=_cparams("parallel"),
    )(w, g, m, v)


def _adamw_rows(w, m, v, srcs, *, name):
    nj, R, C = w.shape
    assert len(srcs) == nj and C % PACK_COLS == 0
    tr = min(R, 512)
    assert R % tr == 0 and all(r0 % tr == 0 for _, r0 in srcs)

    def body(*refs):
        w_ref, m_ref, v_ref = refs[:3]
        g_refs = refs[3:3 + nj]
        go_ref, d_ref, nm_ref, nv_ref = refs[3 + nj:]
        gv = g_refs[0][...]
        for jj in range(1, nj):
            gv = jnp.where(pl.program_id(0) == jj, g_refs[jj][...], gv)
        d, m_new, v_new = _adam_math(w_ref[...], gv, m_ref[...], v_ref[...])
        go_ref[...] = gv
        d_ref[...] = d
        nm_ref[...] = m_new
        nv_ref[...] = v_new

    nat = pl.BlockSpec((None, tr, PACK_COLS), lambda j, cb, i: (j, i, cb))

    def src_spec(jj, r0):
        return pl.BlockSpec((tr, PACK_COLS), lambda j, cb, i: (jnp.where(j == jj, (r0 + cb * R) // tr + i, r0 // tr), 0))

    sds = jax.ShapeDtypeStruct((nj, R, C), F32)
    return pl.pallas_call(
        body, name=name, out_shape=(sds,) * 4, grid=(nj, C // PACK_COLS, R // tr),
        in_specs=[nat] * 3 + [src_spec(jj, r0) for jj, (_, r0) in enumerate(srcs)], out_specs=(nat,) * 4,
        compiler_params=_cparams("parallel", "parallel", "parallel"),
    )(w, m, v, *[rows for rows, _ in srcs])


HBM_SPEC = pl.BlockSpec(memory_space=pltpu.HBM)
VMEM_SPEC = pl.BlockSpec(memory_space=pltpu.VMEM)
SEM_SPEC = pl.BlockSpec(memory_space=pltpu.SEMAPHORE)
ANY_SPEC = pl.BlockSpec(memory_space=pl.ANY)
SPLIT_EFFECT = pltpu.SideEffectType.DATAFLOW_SIDE_EFFECTING


def _place():
    return lax.axis_index("x"), lax.axis_index("y"), lax.axis_index("c")


def _other_chips(x, y):
    return [(1 - x, y), (x, 1 - y), (1 - x, 1 - y)]


def _remote(src, dst, send_sem, recv_sem, dev):
    return pltpu.make_async_remote_copy(src_ref=src, dst_ref=dst, send_sem=send_sem, recv_sem=recv_sem,
                                        device_id=dev, device_id_type=MESH)


def _ag_ici_start(wp, after, *, name):
    R, C = wp.shape
    H = R // 2

    def body(w_ref, land_ref, after_ref, send_sems, recv_sems, w_thru, land_thru, token):
        x, y, c = _place()
        rows = pl.ds(pl.multiple_of(c * H, 16), H)
        for r, chip in enumerate(_other_chips(x, y)):
            _remote(w_ref.at[rows, :], land_ref.at[2 * x + y, rows, :], send_sems.at[r], recv_sems.at[r], (*chip, c)).start()
        token[...] = jnp.zeros(token.shape, F32)

    land = lax.empty((N_CHIPS, R, C), wp.dtype)
    return pl.pallas_call(
        body, name=name,
        out_shape=(pltpu.SemaphoreType.DMA((3,)), pltpu.SemaphoreType.DMA((3,)), pltpu.HBM(wp.shape, wp.dtype), pltpu.HBM(land.shape, land.dtype),
                   jax.ShapeDtypeStruct((8, LANES), F32)),
        in_specs=(HBM_SPEC, HBM_SPEC, ANY_SPEC), out_specs=(SEM_SPEC, SEM_SPEC, HBM_SPEC, HBM_SPEC, VMEM_SPEC),
        input_output_aliases={0: 2, 1: 3},
        compiler_params=pltpu.CompilerParams(has_side_effects=SPLIT_EFFECT),
    )(pltpu.with_memory_space_constraint(wp, pltpu.HBM), pltpu.with_memory_space_constraint(land, pltpu.HBM), after)


def _ag_ici_wait(send_sems, recv_sems, w_thru, land_thru, after, *, name):
    R, C = w_thru.shape
    H = R // 2
    after = after if isinstance(after, (tuple, list)) else (after,)

    def body(w_ref, land_ref, send_sems, recv_sems, *rest):
        x, y, c = _place()
        rows = pl.ds(pl.multiple_of(c * H, 16), H)
        for r, (px, py) in enumerate(_other_chips(x, y)):
            cp = _remote(w_ref.at[rows, :], land_ref.at[2 * px + py, rows, :], send_sems.at[r], recv_sems.at[r], (px, py, c))
            cp.wait_send()
            cp.wait_recv()

    return pl.pallas_call(
        body, name=name,
        out_shape=(pltpu.HBM(w_thru.shape, w_thru.dtype), pltpu.HBM(land_thru.shape, land_thru.dtype)),
        in_specs=(HBM_SPEC, HBM_SPEC, SEM_SPEC, SEM_SPEC) + (ANY_SPEC,) * len(after), out_specs=(HBM_SPEC, HBM_SPEC),
        input_output_aliases={0: 0, 1: 1},
        compiler_params=pltpu.CompilerParams(has_side_effects=SPLIT_EFFECT),
    )(w_thru, land_thru, send_sems, recv_sems, *after)


def _ag_forward_start(land, wp, *, name):
    _, R, C = land.shape
    H = R // 2

    def body(land_ref, w_ref, send_sems, recv_sems, land_thru, w_thru):
        x, y, c = _place()
        sib = (x, y, 1 - c)
        mine = pl.ds(pl.multiple_of(c * H, 16), H)
        for r, (px, py) in enumerate(_other_chips(x, y)):
            _remote(land_ref.at[2 * px + py, mine, :], land_ref.at[2 * px + py, mine, :], send_sems.at[r], recv_sems.at[r], sib).start()
        _remote(w_ref, land_ref.at[2 * x + y], send_sems.at[3], recv_sems.at[3], sib).start()

    return pl.pallas_call(
        body, name=name,
        out_shape=(pltpu.SemaphoreType.DMA((4,)), pltpu.SemaphoreType.DMA((4,)), pltpu.HBM(land.shape, land.dtype), pltpu.HBM(wp.shape, wp.dtype)),
        in_specs=(HBM_SPEC, HBM_SPEC), out_specs=(SEM_SPEC, SEM_SPEC, HBM_SPEC, HBM_SPEC),
        input_output_aliases={0: 2, 1: 3},
        compiler_params=pltpu.CompilerParams(has_side_effects=SPLIT_EFFECT),
    )(land, wp)


def _ag_forward_wait(send_sems, recv_sems, land_thru, w_thru, after, *, name):
    _, R, C = land_thru.shape
    H = R // 2

    def body(land_ref, w_ref, send_sems, recv_sems, after_ref, land_out, w_out):
        x, y, c = _place()
        sib = (x, y, 1 - c)
        mine = pl.ds(pl.multiple_of(c * H, 16), H)
        theirs = pl.ds(pl.multiple_of((1 - c) * H, 16), H)
        for r, (px, py) in enumerate(_other_chips(x, y)):
            cp = _remote(land_ref.at[2 * px + py, mine, :], land_ref.at[2 * px + py, theirs, :], send_sems.at[r], recv_sems.at[r], sib)
            cp.wait_send()
            cp.wait_recv()
        own = _remote(w_ref, land_ref.at[2 * x + y], send_sems.at[3], recv_sems.at[3], sib)
        own.wait_send()
        own.wait_recv()

    return pl.pallas_call(
        body, name=name,
        out_shape=(pltpu.HBM(land_thru.shape, land_thru.dtype), pltpu.HBM(w_thru.shape, w_thru.dtype)),
        in_specs=(HBM_SPEC, HBM_SPEC, SEM_SPEC, SEM_SPEC, ANY_SPEC), out_specs=(HBM_SPEC, HBM_SPEC),
        input_output_aliases={0: 0, 1: 1},
        compiler_params=pltpu.CompilerParams(has_side_effects=SPLIT_EFFECT),
    )(land_thru, w_thru, send_sems, recv_sems, after)[0]


def _rs_sibling_swap(g, *, name):
    _, R, C = g.shape
    H = R // 2

    def body(g_ref, theirs_ref, send_sems, recv_sems):
        x, y, c = _place()
        sib = (x, y, 1 - c)
        copies = [_remote(g_ref.at[k, pl.ds(pl.multiple_of((1 - c) * H, 16), H), :], theirs_ref.at[k],
                          send_sems.at[k], recv_sems.at[k], sib) for k in range(N_CHIPS)]
        for cp in copies:
            cp.start()
        for cp in copies:
            cp.wait()

    return pl.pallas_call(
        body, name=name, out_shape=jax.ShapeDtypeStruct((N_CHIPS, H, C), g.dtype),
        in_specs=[HBM_SPEC], out_specs=HBM_SPEC,
        scratch_shapes=[pltpu.SemaphoreType.DMA((N_CHIPS,)), pltpu.SemaphoreType.DMA((N_CHIPS,))],
    )(g)


def _row_tile(h):
    best = 16
    for d in range(16, 1025, 16):
        if h % d == 0:
            best = d
    return best


def _add2_bf16(g, theirs, core, *, name):
    K, H, C = theirs.shape
    tr = _row_tile(H)
    nb = H // tr

    def body(c_ref, a_ref, b_ref, o_ref):
        o_ref[...] = (a_ref[...].astype(F32) + b_ref[...].astype(F32)).astype(o_ref.dtype)

    spec = pl.BlockSpec((None, tr, C), lambda k, i, c: (k, i, 0))
    return pl.pallas_call(
        body, name=name, out_shape=jax.ShapeDtypeStruct((K, H, C), theirs.dtype),
        grid_spec=pltpu.PrefetchScalarGridSpec(
            num_scalar_prefetch=1, grid=(K, nb),
            in_specs=[pl.BlockSpec((None, tr, C), lambda k, i, c: (k, c[0] * nb + i, 0)), spec], out_specs=spec),
        compiler_params=_cparams("parallel", "parallel"),
    )(core, g, theirs)


def _rs_chip_exchange_start(p, *, name):
    _, H, C = p.shape

    def body(p_ref, land_ref, send_sems, recv_sems, p_thru, land_thru, token):
        x, y, c = _place()
        for r, (px, py) in enumerate(_other_chips(x, y)):
            _remote(p_ref.at[2 * px + py], land_ref.at[r], send_sems.at[r], recv_sems.at[r], (px, py, c)).start()
        token[...] = jnp.zeros(token.shape, F32)

    land = lax.empty((3, H, C), p.dtype)
    return pl.pallas_call(
        body, name=name,
        out_shape=(pltpu.SemaphoreType.DMA((3,)), pltpu.SemaphoreType.DMA((3,)), pltpu.HBM(p.shape, p.dtype), pltpu.HBM(land.shape, land.dtype),
                   jax.ShapeDtypeStruct((8, LANES), F32)),
        in_specs=(HBM_SPEC, HBM_SPEC), out_specs=(SEM_SPEC, SEM_SPEC, HBM_SPEC, HBM_SPEC, VMEM_SPEC),
        input_output_aliases={0: 2, 1: 3},
        compiler_params=pltpu.CompilerParams(has_side_effects=SPLIT_EFFECT),
    )(pltpu.with_memory_space_constraint(p, pltpu.HBM), pltpu.with_memory_space_constraint(land, pltpu.HBM))


def _rs_chip_exchange_wait(send_sems, recv_sems, p_thru, land_thru, after, *, name):
    after = after if isinstance(after, (tuple, list)) else (after,)

    def body(p_ref, land_ref, send_sems, recv_sems, *rest):
        x, y, c = _place()
        for r, (px, py) in enumerate(_other_chips(x, y)):
            cp = _remote(p_ref.at[2 * px + py], land_ref.at[r], send_sems.at[r], recv_sems.at[r], (px, py, c))
            cp.wait_send()
            cp.wait_recv()

    return pl.pallas_call(
        body, name=name,
        out_shape=(pltpu.HBM(p_thru.shape, p_thru.dtype), pltpu.HBM(land_thru.shape, land_thru.dtype)),
        in_specs=(HBM_SPEC, HBM_SPEC, SEM_SPEC, SEM_SPEC) + (ANY_SPEC,) * len(after), out_specs=(HBM_SPEC, HBM_SPEC),
        input_output_aliases={0: 0, 1: 1},
        compiler_params=pltpu.CompilerParams(has_side_effects=SPLIT_EFFECT),
    )(p_thru, land_thru, send_sems, recv_sems, *after)


def _add4_f32(p, recv, chip_core, *, name):
    _, H, C = p.shape
    tr = _row_tile(H)
    nb = H // tr

    def body(s_ref, o_ref, r_ref, out_ref):
        out_ref[...] = ((o_ref[...].astype(F32) + r_ref[0].astype(F32)) + r_ref[1].astype(F32)) + r_ref[2].astype(F32)

    return pl.pallas_call(
        body, name=name, out_shape=jax.ShapeDtypeStruct((2 * H, C), F32),
        grid_spec=pltpu.PrefetchScalarGridSpec(
            num_scalar_prefetch=1, grid=(nb,),
            in_specs=[pl.BlockSpec((None, tr, C), lambda i, s: (s[0], i, 0)), pl.BlockSpec((3, tr, C), lambda i, s: (0, i, 0))],
            out_specs=pl.BlockSpec((tr, C), lambda i, s: (s[1] * nb + i, 0))),
        compiler_params=_cparams("parallel"),
    )(chip_core, p, recv)


def _rs_sibling_join(f, *, name):
    R, C = f.shape
    H = R // 2

    def body(f_ref, out_ref, send_sem, recv_sem):
        x, y, c = _place()
        sib = (x, y, 1 - c)
        mine = pl.ds(pl.multiple_of(c * H, 8), H)
        theirs = pl.ds(pl.multiple_of((1 - c) * H, 8), H)
        cp = _remote(f_ref.at[mine, :], out_ref.at[mine, :], send_sem, recv_sem, sib)
        cp.start()
        _remote(f_ref.at[mine, :], out_ref.at[theirs, :], send_sem, recv_sem, sib).wait_recv()
        cp.wait_send()

    return pl.pallas_call(
        body, name=name, out_shape=jax.ShapeDtypeStruct((R, C), f.dtype),
        in_specs=[HBM_SPEC], out_specs=HBM_SPEC, input_output_aliases={0: 0},
        scratch_shapes=[pltpu.SemaphoreType.DMA, pltpu.SemaphoreType.DMA],
    )(f)


SMALL_GATHER = (("pool_norm", 2, 256), ("pool_scale", 2, 512), ("conv_w", 3, 512), ("mla_norm", 1, 256),
                ("mla_q_norm", 1, 96), ("mla_kv_norm", 1, 64))
SMALL_SLOT = (16, 512)


def _gather_small(shards, *, name):
    def body(pn_ref, ps_ref, cw_ref, mn_ref, qn_ref, kn_ref, pn_o, ps_o, cw_o, mn_o, qn_o, kn_o, all_ref, send_sems, recv_sems):
        x, y, c = _place()
        mine = 2 * x + y
        all_ref[mine] = jnp.zeros(SMALL_SLOT, F32)
        all_ref[mine, 0:2, 0:256] = pn_ref[...]
        all_ref[mine, 2:4, :] = ps_ref[...]
        all_ref[mine, 4:7, :] = cw_ref[0]
        all_ref[mine, 7:8, 0:256] = mn_ref[...]
        all_ref[mine, 8:9, 0:96] = qn_ref[...]
        all_ref[mine, 9:10, 0:64] = kn_ref[...]
        chips = _other_chips(x, y)
        sends = [_remote(all_ref.at[mine], all_ref.at[mine], send_sems.at[r], recv_sems.at[r], (*chip, c)) for r, chip in enumerate(chips)]
        for cp in sends:
            cp.start()
        for r, (px, py) in enumerate(chips):
            _remote(all_ref.at[mine], all_ref.at[2 * px + py], send_sems.at[r], recv_sems.at[r], (px, py, c)).wait_recv()
        for cp in sends:
            cp.wait_send()
        for k in range(N_CHIPS):
            pn_o[:, k * 256:(k + 1) * 256] = all_ref[k, 0:2, 0:256]
            ps_o[:, k * 512:(k + 1) * 512] = all_ref[k, 2:4, :]
            cw_o[0, :, k * 512:(k + 1) * 512] = all_ref[k, 4:7, :]
            mn_o[:, k * 256:(k + 1) * 256] = all_ref[k, 7:8, 0:256]
            qn_o[k] = all_ref[k, 8:9, 0:96]
            kn_o[k] = all_ref[k, 9:10, 0:64]

    sds = lambda *shape: jax.ShapeDtypeStruct(shape, F32)
    out = pl.pallas_call(
        body, name=name,
        out_shape=(sds(2, 1024), sds(2, 2048), sds(1, 3, 2048), sds(1, 1024), sds(N_CHIPS, 1, 96), sds(N_CHIPS, 1, 64)),
        in_specs=[VMEM_SPEC] * 6, out_specs=(VMEM_SPEC,) * 6,
        scratch_shapes=[pltpu.VMEM((N_CHIPS,) + SMALL_SLOT, F32), pltpu.SemaphoreType.DMA((3,)), pltpu.SemaphoreType.DMA((3,))],
    )(*[shards[n] for n, _, _ in SMALL_GATHER])
    full = dict(zip([n for n, _, _ in SMALL_GATHER], out))
    full["mla_q_norm"] = full["mla_q_norm"].reshape(1, Q_LORA)
    full["mla_kv_norm"] = full["mla_kv_norm"].reshape(1, KV_LORA)
    return full


SMALL_REDUCE = (("pool_norm_0", 0, 1, 1024), ("pool_norm_1", 1, 1, 1024), ("pool_scale_0", 2, 1, 2048), ("pool_scale_1", 3, 1, 2048),
                ("conv_norm", 4, 1, 1024), ("mla_norm", 5, 1, 1024), ("mla_q_norm", 6, 1, 384), ("mla_kv_norm", 7, 1, 256),
                ("conv_w", 8, 8, 2048), ("final_norm", 16, 1, 1024), ("loss", 17, 1, LANES))
REDUCE_SLOT = (24, 2048)


def _reduce_small(parts, after, *, name):
    keys = [k for k, _, _, _ in SMALL_REDUCE]

    def body(*refs):
        ins = dict(zip(keys, refs[:len(keys)]))
        pn_o, ps_o, cn_o, cw_o, mn_o, qn_o, kn_o, fn_o, ls_o, all_ref, send_sems, recv_sems = refs[len(keys) + 1:]
        x, y, c = _place()
        me = 4 * x + 2 * y + c
        all_ref[me] = jnp.zeros(REDUCE_SLOT, F32)
        for k, r0, nr, wd in SMALL_REDUCE:
            all_ref[me, r0:r0 + nr, 0:wd] = ins[k][...]
        peers = []
        for rel in range(1, N_DEV):
            dx, dy, dc = (rel >> 2) & 1, (rel >> 1) & 1, rel & 1
            peers.append((1 - x if dx else x, 1 - y if dy else y, 1 - c if dc else c))
        sends = [_remote(all_ref.at[me], all_ref.at[me], send_sems.at[k], recv_sems.at[k], peer) for k, peer in enumerate(peers)]
        for cp in sends:
            cp.start()
        for k, (px, py, pc) in enumerate(peers):
            _remote(all_ref.at[me], all_ref.at[4 * px + 2 * py + pc], send_sems.at[k], recv_sems.at[k], (px, py, pc)).wait_recv()
        for cp in sends:
            cp.wait_send()

        def total(r0, nr, wd):
            acc = all_ref[0, r0:r0 + nr, 0:wd]
            for d in range(1, N_DEV):
                acc = acc + all_ref[d, r0:r0 + nr, 0:wd]
            return acc

        pn_o[0:1, :] = total(0, 1, 1024)
        pn_o[1:2, :] = total(1, 1, 1024)
        ps_o[0:1, :] = total(2, 1, 2048)
        ps_o[1:2, :] = total(3, 1, 2048)
        cn_o[...] = total(4, 1, 1024)
        mn_o[...] = total(5, 1, 1024)
        qn_o[...] = total(6, 1, Q_LORA)
        kn_o[...] = total(7, 1, KV_LORA)
        cw_o[0] = total(8, 3, 2048)
        fn_o[...] = total(16, 1, 1024)
        ls_o[...] = total(17, 1, LANES)

    sds = lambda *shape: jax.ShapeDtypeStruct(shape, F32)
    out = pl.pallas_call(
        body, name=name,
        out_shape=(sds(2, 1024), sds(2, 2048), sds(1, 1024), sds(1, 3, 2048), sds(1, 1024), sds(1, Q_LORA), sds(1, KV_LORA), sds(1, 1024),
                   sds(1, LANES)),
        in_specs=[VMEM_SPEC] * len(keys) + [ANY_SPEC], out_specs=(VMEM_SPEC,) * 9,
        scratch_shapes=[pltpu.VMEM((N_DEV,) + REDUCE_SLOT, F32), pltpu.SemaphoreType.DMA((N_DEV - 1,)), pltpu.SemaphoreType.DMA((N_DEV - 1,))],
    )(*[parts[k] for k in keys], after)
    return dict(zip(("pool_norm", "pool_scale", "conv_norm", "conv_w", "mla_norm", "mla_q_norm", "mla_kv_norm", "final_norm", "loss"), out))


def _adam_math(w, g, m, v):
    m_new = ADAM_B1 * m + (1.0 - ADAM_B1) * g
    v_new = ADAM_B2 * v + (1.0 - ADAM_B2) * (g * g)
    m_hat = m_new / (1.0 - ADAM_B1 ** ADAM_STEP)
    v_hat = v_new / (1.0 - ADAM_B2 ** ADAM_STEP)
    return -ADAM_LR * (m_hat / (jnp.sqrt(v_hat) + ADAM_EPS) + ADAM_WD * w), m_new, v_new


def _adamw_small(w, m, v, g_full, chip, *, name):
    shp = {n: w[n].shape for n in SMALL}
    whole = lambda s: pl.BlockSpec(s, lambda i, c: (0,) * len(s))
    g_in, g_specs = {}, {}
    for n in SMALL:
        if not SMALL_SHARDED[n]:
            g_in[n], g_specs[n] = g_full[n].reshape(shp[n]), whole(shp[n])
        elif shp[n][-1] % LANES:
            g_in[n] = g_full[n].reshape(N_CHIPS, 1, shp[n][-1])
            g_specs[n] = pl.BlockSpec((None,) + shp[n], lambda i, c: (c[0], 0, 0))
        else:
            g_in[n] = g_full[n]
            nd = len(shp[n])
            g_specs[n] = pl.BlockSpec(shp[n], lambda i, c, nd=nd: (0,) * (nd - 1) + (c[0],))

    def body(c_ref, *refs):
        k = len(SMALL)
        w_r, m_r, v_r, g_r = refs[0:k], refs[k:2 * k], refs[2 * k:3 * k], refs[3 * k:4 * k]
        go_r, d_r, nm_r, nv_r = refs[4 * k:5 * k], refs[5 * k:6 * k], refs[6 * k:7 * k], refs[7 * k:8 * k]
        for i in range(k):
            gv = g_r[i][...]
            d, m_new, v_new = _adam_math(w_r[i][...], gv, m_r[i][...], v_r[i][...])
            go_r[i][...] = gv
            d_r[i][...] = d
            nm_r[i][...] = m_new
            nv_r[i][...] = v_new

    nat = [whole(shp[n]) for n in SMALL]
    out_sds = tuple(jax.ShapeDtypeStruct(shp[n], F32) for n in SMALL)
    out = pl.pallas_call(
        body, name=name, out_shape=out_sds * 4,
        grid_spec=pltpu.PrefetchScalarGridSpec(
            num_scalar_prefetch=1, grid=(1,),
            in_specs=nat * 3 + [g_specs[n] for n in SMALL], out_specs=tuple(nat) * 4),
        compiler_params=_cparams("arbitrary"),
    )(chip, *[w[n] for n in SMALL], *[m[n] for n in SMALL], *[v[n] for n in SMALL], *[g_in[n] for n in SMALL])
    k = len(SMALL)
    return tuple(dict(zip(SMALL, out[j * k:(j + 1) * k])) for j in range(4))


BIG = ("pool_w_in", "pool_w_grp", "pool_w_out", "conv_w_in", "conv_w_out", "mla_w_in", "mla_w_q_up", "mla_w_kv_up", "mla_w_out")
BIG_SHARD_AXIS = {"pool_w_in": 2, "pool_w_grp": 2, "pool_w_out": 1, "conv_w_in": 2, "conv_w_out": 1,
                  "mla_w_in": 2, "mla_w_q_up": 2, "mla_w_kv_up": 2, "mla_w_out": 1}
GATHER_LAYOUT = {
    "p0": ((("pool_w_in", 0), 0, "cols"),),
    "p0b": ((("pool_w_out", 0), 0, "rows"), (("pool_w_grp", 0), 512, "flat")),
    "cv": ((("conv_w_in", 0), 0, "cols"), (("conv_w_out", 0), 2048, "rows")),
    "ml": ((("pool_w_in", 1), 0, "cols"), (("mla_w_out", 0), 1024, "rows"), (("pool_w_out", 1), 1536, "rows"),
           (("mla_w_kv_up", 0), 2048, "cols"), (("pool_w_grp", 1), 2304, "flat"), (("mla_w_q_up", 0), 2560, "flat"),
           (("mla_w_in", 0), 2848, "flat")),
}
REDUCE_LAYOUT = {
    "late": ((("conv_w_in", 0), 0, "cols"), (("pool_w_in", 1), 2048, "cols"), (("conv_w_out", 0), 3072, "rows"),
             (("mla_w_out", 0), 3584, "rows"), (("pool_w_out", 1), 4096, "rows"), (("mla_w_kv_up", 0), 4608, "cols"),
             (("pool_w_grp", 1), 4864, "flat"), (("mla_w_q_up", 0), 5120, "flat"), (("mla_w_in", 0), 5408, "flat")),
    "first": ((("pool_w_in", 0), 0, "cols"), (("pool_w_out", 0), 1024, "rows"), (("pool_w_grp", 0), 1536, "flat")),
}
PACK_ROW_ALIGN = 32
RS_ROW_ALIGN = 512


def _slot_rows(layout, shard_shape, align):
    where, end = {}, 0
    for piece, r0, kind in layout:
        n = 1
        for d in shard_shape(piece):
            n *= d
        assert r0 >= end and n % PACK_COLS == 0, (piece, r0, end)
        where[piece] = (r0, n // PACK_COLS, kind)
        end = r0 + n // PACK_COLS
    return end + (-end) % align, where


def _as_slot_rows(shard, kind):
    if kind == "cols":
        k, n = shard.shape
        return shard.reshape(k, n // PACK_COLS, PACK_COLS).swapaxes(0, 1).reshape(-1, PACK_COLS)
    return shard.reshape(-1, PACK_COLS)


def _pack_slot(shards, layout, rows, dtype):
    parts, end = [], 0
    for piece, r0, kind in layout:
        if r0 > end:
            parts.append(jnp.zeros((r0 - end, PACK_COLS), dtype))
        parts.append(_as_slot_rows(shards[piece], kind).astype(dtype))
        end = r0 + parts[-1].shape[0]
    if rows > end:
        parts.append(jnp.zeros((rows - end, PACK_COLS), dtype))
    return jnp.concatenate(parts, axis=0)


SMALL = ("pool_norm", "pool_scale", "conv_norm", "conv_w", "mla_norm", "mla_q_norm", "mla_kv_norm", "final_norm")
SMALL_SHARDED = {"pool_norm": True, "pool_scale": True, "conv_norm": False, "conv_w": True, "mla_norm": True,
                 "mla_q_norm": True, "mla_kv_norm": True, "final_norm": False}


def _rope_tables(positions):
    inv_freq = ROPE_BASE ** (-jnp.arange(0, QK_ROPE, 2, dtype=F32) / QK_ROPE)
    ang = positions.astype(F32).reshape(-1, 1) * inv_freq
    cos, sin = jnp.cos(ang), jnp.sin(ang)
    z32 = jnp.zeros_like(cos)
    z64 = jnp.concatenate([z32, z32], axis=1)
    return (jnp.concatenate([cos, cos, z64], axis=1), jnp.concatenate([-sin, z32, z64], axis=1),
            jnp.concatenate([z32, sin, z64], axis=1))


def _mla_in_to_padded(w):
    q, kv, kr, z = w[:, :Q_LORA], w[:, Q_LORA:Q_LORA + KV_LORA], w[:, Q_LORA + KV_LORA:Q_LORA + KV_LORA + QK_ROPE], w[:, Q_LORA + KV_LORA + QK_ROPE:]
    return jnp.concatenate([z, kv, q, kr, jnp.zeros((w.shape[0], MLA_IN_PAD - MLA_IN), w.dtype)], axis=1)


def _mla_in_from_padded(w):
    z, kv, q, kr = w[:, :D_INNER], w[:, D_INNER:D_INNER + KV_LORA], w[:, D_INNER + KV_LORA:D_INNER + KV_LORA + Q_LORA], w[:, D_INNER + KV_LORA + Q_LORA:D_INNER + KV_LORA + Q_LORA + QK_ROPE]
    return jnp.concatenate([q, kv, kr, z], axis=1)


def _q_up_to_padded(w):
    k = w.shape[0]
    return jnp.pad(w.reshape(k, N_HEADS, QK_NOPE + QK_ROPE), ((0, 0), (0, 0), (0, HEAD_PAD - QK_NOPE - QK_ROPE))).reshape(k, N_HEADS * HEAD_PAD)


def _q_up_from_padded(w):
    k = w.shape[0]
    return w.reshape(k, N_HEADS, HEAD_PAD)[:, :, :QK_NOPE + QK_ROPE].reshape(k, N_HEADS * (QK_NOPE + QK_ROPE))


def _local_step(x, positions, target, weights_for, ws, sink):
    S = x.shape[0]
    tm = min(512, S)
    te = min(256, S)
    tq = min(512, S)
    tabs = _rope_tables(positions)
    gs = {}

    def mm_in(xn, w, name):
        n = w.shape[1]
        tn = PACK_COLS if isinstance(w, Packed) else _pick(n, 1536 if n == MLA_IN_PAD else 1024)
        return _mm(xn, w, tm=min(1024, S), tn=tn, tk=D_MODEL, name=name)

    def mm_out(y, w, res, name, after=None):
        return _mm(y, w, residual=res, after=after, tm=tm, tn=D_MODEL, tk=D_INNER, name=name)

    def mm_dx(dy, w, name, after=None):
        k, n = w.shape
        if isinstance(w, Packed):
            tn, tk = (k if w.kind == "rows" else min(k, 1024)), PACK_COLS
        else:
            tn, tk = _pick(k, 1024), _pick(n, 1408)
        return _mm(dy, w, trans_b=True, after=after, tm=min(1024, S), tn=tn, tk=tk, name=name)

    def mm_dw(piece, a, b, name, after=None, post=None):
        ka, nb = a.shape[1], b.shape[1]
        into = sink.dest(piece)
        tokens = min(1024, S)
        if into is None:
            out = _mm(a, b, trans_a=True, out_dtype=BF16, after=after, tm=_pick(ka, 1024), tn=_pick(nb, 1408), tk=tokens, name=name)
            sink.put(piece, out if post is None else post(out))
        else:
            rows = ka if into.kind == "rows" else min(ka, 1024)
            sink.put(piece, _mm(a, b, trans_a=True, after=after, into=into, tm=rows, tn=PACK_COLS, tk=tokens, name=name))

    def pool_layer_fwd(xin, xn, wts, j, tag, then=None, rest=None):
        h = mm_in(xn, wts[("pool_w_in", j)], f"{tag}_in")
        if rest is not None:
            wts.update(rest(h))
        y = _pool_fwd(h, wts[("pool_w_grp", j)], ws["pool_scale"][j:j + 1], tm=te, name=f"{tag}_mix")
        xo = mm_out(y, wts[("pool_w_out", j)], xin, f"{tag}_out", after=None if then is None else then(y))
        return xo, (xin, xn, h, y)

    def pool_layer_bwd(dx, dxb, saved, wts, j, tag, after=None):
        xin, xn, h, y = saved
        dy = mm_dx(dxb, wts[("pool_w_out", j)], f"{tag}_dy", after)
        mm_dw(("pool_w_out", j), y, dxb, f"{tag}_dwo", after)
        pooled, dmixed, dpooled, dz, dsc = _pool_bwd1(h, dy, wts[("pool_w_grp", j)], ws["pool_scale"][j:j + 1], tm=te, name=f"{tag}_bmix")
        sink.put(("pool_w_grp", j), _grouped_tn(pooled, dmixed, tk=tm, name=f"{tag}_dwg"))
        dh = _pool_bwd2(dpooled, dz, tm=te, name=f"{tag}_bshift")
        dxn = mm_dx(dh, wts[("pool_w_in", j)], f"{tag}_dxn")
        mm_dw(("pool_w_in", j), xn, dh, f"{tag}_dwi")
        dxo, dxob, dg = _rms_bwd(xin, ws["pool_norm"][j:j + 1], dxn, dx, tm=tm, name=f"{tag}_bnorm")
        gs[f"pool_norm_{j}"], gs[f"pool_scale_{j}"] = dg, dsc
        return dxo, dxob

    xn0 = _rms_fwd(x, ws["pool_norm"][0:1], tm=tm, name="p0_norm")
    w_p0 = weights_for("p0", xn0)
    x1, sv0 = pool_layer_fwd(x, xn0, w_p0, 0, "p0", then=lambda y: weights_for.prefetch("cv", y), rest=lambda h: weights_for("p0b", h))

    xn1 = _rms_fwd(x1, ws["conv_norm"][0:1], tm=tm, name="cv_norm")
    w_cv = weights_for("cv", xn1)
    h1 = mm_in(xn1, w_cv[("conv_w_in", 0)], "cv_in")
    cw = jnp.pad(ws["conv_w"][0], ((0, 5), (0, 0)))
    y1 = _conv_fwd(h1, cw, tm=te, name="cv_mix")
    x2 = mm_out(y1, w_cv[("conv_w_out", 0)], x1, "cv_out", after=weights_for.prefetch("ml", y1))

    xn2 = _rms_fwd(x2, ws["mla_norm"][0:1], tm=tm, name="ml_norm")
    w_ml = weights_for("ml", xn2)
    w_mi = _mla_in_to_padded(w_ml[("mla_w_in", 0)])
    w_q = _q_up_to_padded(w_ml[("mla_w_q_up", 0)])
    w_kv = w_ml[("mla_w_kv_up", 0)]
    qg, kvg = ws["mla_q_norm"][0:1], ws["mla_kv_norm"][0:1]
    h2 = mm_in(xn2, w_mi, "ml_in")
    q_n, kv_n, krr = _mla_latent_fwd(h2, qg, kvg, tabs, tm=tm, name="ml_lat")
    q_full = _mla_q_up(q_n, w_q, tabs, tm=tm, name="ml_qup")
    k_full, v = _mla_kv_up(kv_n, w_kv, krr, tm=tm, name="ml_kvup")
    o, y2, lse = _flash_fwd(q_full, k_full, v, h2, tq=tq, name="ml_attn")
    x3 = mm_out(y2, w_ml[("mla_w_out", 0)], x2, "ml_out")

    x4, sv3 = pool_layer_fwd(x3, _rms_fwd(x3, ws["pool_norm"][1:2], tm=tm, name="p1_norm"), w_ml, 1, "p1")

    loss_part, dx, dxb, dgf = _final_loss(x4, ws["final_norm"].reshape(1, -1), target, tm=tm, name="final")
    gs["final_norm"] = dgf

    dx, dxb = pool_layer_bwd(dx, dxb, sv3, w_ml, 1, "p1")

    dy = mm_dx(dxb, w_ml[("mla_w_out", 0)], "ml_dy")
    mm_dw(("mla_w_out", 0), y2, dxb, "ml_dwo")
    do, dz, delta = _mla_gate_bwd(dy, o, h2, tm=tq, name="ml_bgate")
    dq_pre, dkv, dkr = _flash_bwd(q_full, k_full, v, do, lse, delta, tabs, tq=tq, name="ml_battn")
    dq_n = mm_dx(dq_pre, w_q, "ml_dqn")
    mm_dw(("mla_w_q_up", 0), q_n, dq_pre, "ml_dwq", post=_q_up_from_padded)
    dkv_n = mm_dx(dkv, w_kv, "ml_dkvn")
    mm_dw(("mla_w_kv_up", 0), kv_n, dkv, "ml_dwkv")
    dh2, dqg, dkvg = _mla_latent_bwd(h2, dq_n, dkv_n, dkr, dz, qg, kvg, tabs, tm=te, name="ml_blat")
    dxn2 = mm_dx(dh2, w_mi, "ml_dxn")
    mm_dw(("mla_w_in", 0), xn2, dh2, "ml_dwi", post=_mla_in_from_padded)
    dx, dxb, dg2 = _rms_bwd(x2, ws["mla_norm"][0:1], dxn2, dx, tm=tm, name="ml_bnorm")
    gs["mla_norm"], gs["mla_q_norm"], gs["mla_kv_norm"] = dg2, dqg, dkvg

    dy = mm_dx(dxb, w_cv[("conv_w_out", 0)], "cv_dy")
    mm_dw(("conv_w_out", 0), y1, dxb, "cv_dwo")
    dh1, dcw = _conv_bwd(h1, dy, cw, tm=te, name="cv_bmix")
    dxn1 = mm_dx(dh1, w_cv[("conv_w_in", 0)], "cv_dxn")
    mm_dw(("conv_w_in", 0), xn1, dh1, "cv_dwi")
    dx, dxb, dg1 = _rms_bwd(x1, ws["conv_norm"][0:1], dxn1, dx, tm=tm, name="cv_bnorm")
    gs["conv_norm"], gs["conv_w"] = dg1, dcw

    dx, dxb = pool_layer_bwd(dx, dxb, sv0, w_p0, 0, "p0", after=sink.late_ready())
    return loss_part, dx, gs


def kernel(x, positions, pool_norm, pool_w_in, pool_w_grp, pool_scale, pool_w_out, conv_norm, conv_w_in, conv_w, conv_w_out, mla_norm, mla_w_in, mla_q_norm, mla_w_q_up, mla_kv_norm, mla_w_kv_up, mla_w_out, final_norm, loss_target, m_pool_norm, m_pool_w_in, m_pool_w_grp, m_pool_scale, m_pool_w_out, m_conv_norm, m_conv_w_in, m_conv_w, m_conv_w_out, m_mla_norm, m_mla_w_in, m_mla_q_norm, m_mla_w_q_up, m_mla_kv_norm, m_mla_w_kv_up, m_mla_w_out, m_final_norm, v_pool_norm, v_pool_w_in, v_pool_w_grp, v_pool_scale, v_pool_w_out, v_conv_norm, v_conv_w_in, v_conv_w, v_conv_w_out, v_mla_norm, v_mla_w_in, v_mla_q_norm, v_mla_w_q_up, v_mla_kv_norm, v_mla_w_kv_up, v_mla_w_out, v_final_norm):
    names = ("pool_norm", "pool_w_in", "pool_w_grp", "pool_scale", "pool_w_out", "conv_norm", "conv_w_in", "conv_w", "conv_w_out",
             "mla_norm", "mla_w_in", "mla_q_norm", "mla_w_q_up", "mla_kv_norm", "mla_w_kv_up", "mla_w_out", "final_norm")
    w = dict(zip(names, (pool_norm, pool_w_in, pool_w_grp, pool_scale, pool_w_out, conv_norm, conv_w_in, conv_w, conv_w_out,
                         mla_norm, mla_w_in, mla_q_norm, mla_w_q_up, mla_kv_norm, mla_w_kv_up, mla_w_out, final_norm)))
    m = dict(zip(names, (m_pool_norm, m_pool_w_in, m_pool_w_grp, m_pool_scale, m_pool_w_out, m_conv_norm, m_conv_w_in, m_conv_w, m_conv_w_out,
                         m_mla_norm, m_mla_w_in, m_mla_q_norm, m_mla_w_q_up, m_mla_kv_norm, m_mla_w_kv_up, m_mla_w_out, m_final_norm)))
    v = dict(zip(names, (v_pool_norm, v_pool_w_in, v_pool_w_grp, v_pool_scale, v_pool_w_out, v_conv_norm, v_conv_w_in, v_conv_w, v_conv_w_out,
                         v_mla_norm, v_mla_w_in, v_mla_q_norm, v_mla_w_q_up, v_mla_kv_norm, v_mla_w_kv_up, v_mla_w_out, v_final_norm)))
    chip = 2 * lax.axis_index("x") + lax.axis_index("y")
    core = lax.axis_index("c")

    core1 = core.astype(jnp.int32).reshape(1)
    chip_core = jnp.stack([chip, core]).astype(jnp.int32)
    shard_shape = lambda piece: w[piece[0]].shape[1:]
    shard_axis = lambda piece: BIG_SHARD_AXIS[piece[0]] - 1
    full_shape = lambda piece: tuple(d * (N_CHIPS if a == shard_axis(piece) else 1) for a, d in enumerate(shard_shape(piece)))

    gather_rows, gather_at, packs = {}, {}, {}
    for grp, layout in GATHER_LAYOUT.items():
        gather_rows[grp], gather_at[grp] = _slot_rows(layout, shard_shape, PACK_ROW_ALIGN)
        packs[grp] = _pack_slot({(n, j): w[n][j] for (n, j), _, _ in layout}, layout, gather_rows[grp], BF16)

    def gathered_weights(grp, gathered):
        out = {}
        for piece, (r0, n, kind) in gather_at[grp].items():
            if kind == "flat":
                out[piece] = jnp.concatenate([gathered[k, r0:r0 + n].reshape(shard_shape(piece)) for k in range(N_CHIPS)], axis=shard_axis(piece))
            else:
                out[piece] = Packed(gathered, r0, kind, full_shape(piece))
        return out

    p0_start = _ag_ici_start(packs["p0"], w["final_norm"], name="ag_p0_start")
    p0b_start = _ag_ici_start(packs["p0b"], p0_start[4], name="ag_p0b_start")
    cv_start = _ag_ici_start(packs["cv"], p0b_start[4], name="ag_cv_start")
    ml_start = _ag_ici_start(packs["ml"], cv_start[4], name="ag_ml_start")
    in_flight = {"p0": p0_start, "p0b": p0b_start, "cv": cv_start, "ml": ml_start}

    forwarding = {}

    def prefetch(grp, after):
        if grp == "p0":
            after = (after, ml_start[4])
        send_sems, recv_sems, w_thru, land, _ = in_flight[grp]
        w_thru, land = _ag_ici_wait(send_sems, recv_sems, w_thru, land, after, name=f"ag_{grp}_wait")
        forwarding[grp] = _ag_forward_start(land, w_thru, name=f"ag_{grp}_fwd_start")
        return forwarding[grp][3]

    def weights_for(grp, after):
        if grp not in forwarding:
            prefetch(grp, after)
        return gathered_weights(grp, _ag_forward_wait(*forwarding[grp], after, name=f"ag_{grp}_fwd_wait"))

    weights_for.prefetch = prefetch

    ws = {"conv_norm": w["conv_norm"], "final_norm": w["final_norm"]}
    ws.update(_gather_small({n: w[n] for n, _, _ in SMALL_GATHER}, name="ag_small"))

    reduce_rows, reduce_at = {}, {}
    for grp, layout in REDUCE_LAYOUT.items():
        reduce_rows[grp], reduce_at[grp] = _slot_rows(layout, shard_shape, RS_ROW_ALIGN)
    group_of = {piece: grp for grp, layout in REDUCE_LAYOUT.items() for piece, _, _ in layout}

    class Sink:
        def __init__(self):
            self.buf = {grp: lax.empty((N_CHIPS, rows, PACK_COLS), BF16) for grp, rows in reduce_rows.items()}
            self.started = {}

        def dest(self, piece):
            grp = group_of[piece]
            r0, _, kind = reduce_at[grp][piece]
            return None if kind == "flat" else Packed(self.buf[grp], r0, kind, full_shape(piece))

        def put(self, piece, result):
            grp = group_of[piece]
            r0, n, kind = reduce_at[grp][piece]
            if kind == "flat":
                parts = jnp.split(result, N_CHIPS, axis=shard_axis(piece))
                result = lax.dynamic_update_slice(self.buf[grp], jnp.stack([p.reshape(n, PACK_COLS) for p in parts]), (0, r0, 0))
            self.buf[grp] = result

        def start(self, grp, tag):
            theirs = _rs_sibling_swap(self.buf[grp], name=f"{tag}_swap")
            chip_sum = _add2_bf16(self.buf[grp], theirs, core1, name=f"{tag}_add2")
            self.started[grp] = _rs_chip_exchange_start(chip_sum, name=f"{tag}_chips_start")
            return self.started[grp][4]

        def finish(self, grp, after, tag):
            send_sems, recv_sems, chip_sum, land, _ = self.started[grp]
            chip_sum, recv = _rs_chip_exchange_wait(send_sems, recv_sems, chip_sum, land, after, name=f"{tag}_chips_wait")
            half_sum = _add4_f32(chip_sum, recv, chip_core, name=f"{tag}_add4")
            return _rs_sibling_join(half_sum, name=f"{tag}_join")

        def late_ready(self):
            return self.start("late", "rsa")

    sink = Sink()

    loss_part, grad_x, gs = _local_step(x[0], positions, loss_target[0], weights_for, ws, sink)
    gs["loss"] = loss_part[0:1, :]

    g, delta, new_m, new_v = {}, {}, {}, {}

    def adam_big(n):
        nj = w[n].shape[0]
        where = [(group_of[(n, j)],) + reduce_at[group_of[(n, j)]][(n, j)] for j in range(nj)]
        if where[0][3] == "flat":
            g[n] = jnp.stack([g_rows[grp][r0:r0 + rows].reshape(w[n].shape[1:]) for grp, r0, rows, _ in where])
            shp = w[n].shape
            two_d = lambda a: a.reshape(-1, shp[-1])
            d_, m_, v_ = _adamw(two_d(w[n]), two_d(g[n]), two_d(m[n]), two_d(v[n]), name=f"adamw_{n}")
            delta[n], new_m[n], new_v[n] = d_.reshape(shp), m_.reshape(shp), v_.reshape(shp)
        else:
            g[n], delta[n], new_m[n], new_v[n] = _adamw_rows(w[n], m[n], v[n], [(g_rows[grp], r0) for grp, r0, _, _ in where], name=f"adamw_{n}")

    first_token = sink.start("first", "rsb")
    g_rows = {"late": sink.finish("late", first_token, "rsa")}
    late_only = [n for n in BIG if all(group_of[(n, j)] == "late" for j in range(w[n].shape[0]))]
    for n in late_only:
        adam_big(n)
    g_rows["first"] = sink.finish("first", tuple(delta[n] for n in late_only), "rsb")
    for n in BIG:
        if n not in late_only:
            adam_big(n)

    gs_sum = _reduce_small(gs, g_rows["first"], name="ar_small")
    loss = gs_sum["loss"][0, 0]
    row = lambda d: {n: (d[n].reshape(1, -1) if d[n].ndim == 1 else d[n]) for n in SMALL}
    small_out = _adamw_small(row(w), row(m), row(v), gs_sum, chip.astype(jnp.int32).reshape(1), name="adamw_small")
    for dst, res in zip((g, delta, new_m, new_v), small_out):
        for n in SMALL:
            dst[n] = res[n].reshape(w[n].shape)

    return (loss, grad_x[None], *[g[n] for n in names], *[delta[n] for n in names],
            *[new_m[n] for n in names], *[new_v[n] for n in names])
```
